```python
import jax, jax.numpy as jnp
from jax import lax
import numpy as np

D_MODEL = 1024
BATCH = 16
SEQ = 256
DEPTH = 2
DEC_BATCH = 2
DEC_SEQ = 2048
PAST_LEN = 512

GRID_W = 64
N_HEADS_A = 8
N_KV_A = 2
GQA_GROUP = N_HEADS_A // N_KV_A
HEAD_DIM_A = 64
WINDOW = 128
ATT_BLOCK = 128
N_HEADS_R = 4
DK_R = 64
DV_R = 128
RET_CHUNK = 128
N_EXPERTS = 16
N_GROUPS = 4
EXPERTS_PER_GROUP = N_EXPERTS // N_GROUPS
TOP_K = 2
D_EXPERT = 512
ROPE_BASE = 10000.0
EPS = 1e-6
NEG_INF = -1e30

Q_A = N_HEADS_A * HEAD_DIM_A
KV_A = N_KV_A * HEAD_DIM_A
QK_R = N_HEADS_R * DK_R
V_R = N_HEADS_R * DV_R
SPLIT_POINTS = (Q_A, Q_A + KV_A, Q_A + 2 * KV_A, Q_A + 2 * KV_A + QK_R, Q_A + 2 * KV_A + 2 * QK_R,
                Q_A + 2 * KV_A + 2 * QK_R + V_R, Q_A + 2 * KV_A + 2 * QK_R + 2 * V_R,
                Q_A + 2 * KV_A + 2 * QK_R + 2 * V_R + D_MODEL)
D_IN = Q_A + 2 * KV_A + 2 * QK_R + 2 * V_R + 2 * D_MODEL

kernel_name = 'hybrid_diffusion_gated_swa_retention_grouped_moe'


def rmsnorm(x, w):
    xf = x.astype(jnp.float32)
    y = xf * lax.rsqrt(jnp.mean(xf * xf, axis=-1, keepdims=True) + EPS)
    return (y * w.astype(jnp.float32)).astype(x.dtype)


def head_groupnorm(o, w):
    of = o.astype(jnp.float32)
    mu = jnp.mean(of, axis=-1, keepdims=True)
    var = jnp.mean(jnp.square(of - mu), axis=-1, keepdims=True)
    y = (of - mu) * lax.rsqrt(var + EPS)
    B, L = o.shape[:2]
    return (y.reshape(B, L, -1) * w.astype(jnp.float32)).astype(o.dtype)


def adaln(cond, w_ada, b_ada):
    mod = jax.nn.silu(cond) @ w_ada + b_ada
    mod = mod.reshape(cond.shape[0], 1, 6, D_MODEL)
    return tuple(mod[:, :, i] for i in range(6))


def modulate(x, shift, scale):
    return x * (1 + scale) + shift


def grid_positions(L):
    rows = L // GRID_W
    pos = jnp.arange(rows * GRID_W)
    return pos // GRID_W, pos % GRID_W


def rope_1d(x, pos):
    half = x.shape[-1] // 2
    freqs = ROPE_BASE ** (-jnp.arange(half, dtype=jnp.float32) / half)
    ang = pos.astype(jnp.float32)[:, None] * freqs[None, :]
    cos = jnp.cos(ang)[:, None, :].astype(x.dtype)
    sin = jnp.sin(ang)[:, None, :].astype(x.dtype)
    x1, x2 = x[..., :half], x[..., half:]
    return jnp.concatenate([x1 * cos - x2 * sin, x2 * cos + x1 * sin], axis=-1)


def rope_2d(x, pos_row, pos_col):
    half = x.shape[-1] // 2
    return jnp.concatenate([rope_1d(x[..., :half], pos_row), rope_1d(x[..., half:], pos_col)], axis=-1)


def sink_softmax(s, sink):
    sk = sink.astype(jnp.float32).reshape(1, N_KV_A, GQA_GROUP, 1, 1)
    m = jnp.maximum(jnp.max(s, axis=-1, keepdims=True), sk)
    p = jnp.exp(s - m)
    return p / (jnp.sum(p, axis=-1, keepdims=True) + jnp.exp(sk - m))


def attn_context(q, k, v, sink):
    B, S = q.shape[:2]
    nb = S // ATT_BLOCK
    scale = HEAD_DIM_A ** -0.5
    qb = jnp.moveaxis(q.reshape(B, nb, ATT_BLOCK, N_KV_A, GQA_GROUP, HEAD_DIM_A), 1, 0)

    def block(qc):
        s = jnp.einsum('bqkgd,btkd->bkgqt', qc, k).astype(jnp.float32) * scale
        p = sink_softmax(s, sink).astype(v.dtype)
        return jnp.einsum('bkgqt,btkd->bqkgd', p, v)

    out = lax.map(block, qb)
    return jnp.moveaxis(out, 0, 1).reshape(B, S, Q_A)


def attn_latent(q, k, v, k_ctx, v_ctx, sink):
    B, L = q.shape[:2]
    C = ATT_BLOCK
    nb = L // C
    P = k_ctx.shape[1]
    scale = HEAD_DIM_A ** -0.5
    qb = jnp.moveaxis(q.reshape(B, nb, C, N_KV_A, GQA_GROUP, HEAD_DIM_A), 1, 0)

    def windows(t):
        tp = jnp.pad(t, ((0, 0), (C, C), (0, 0), (0, 0))).reshape(B, nb + 2, C, N_KV_A, HEAD_DIM_A)
        w = jnp.concatenate([tp[:, :nb], tp[:, 1:nb + 1], tp[:, 2:]], axis=2)
        return jnp.moveaxis(w, 1, 0)

    kw, vw = windows(k), windows(v)
    rows = jnp.arange(C)[:, None]
    cols = jnp.arange(3 * C)[None, :]

    def block(args):
        qc, kc, vc, i = args
        s_ctx = jnp.einsum('bqkgd,btkd->bkgqt', qc, k_ctx).astype(jnp.float32) * scale
        s_win = jnp.einsum('bqkgd,btkd->bkgqt', qc, kc).astype(jnp.float32) * scale
        n = i * C + rows
        m = (i - 1) * C + cols
        valid = (jnp.abs(n - m) <= WINDOW) & (m >= 0) & (m < L)
        s_win = jnp.where(valid, s_win, NEG_INF)
        p = sink_softmax(jnp.concatenate([s_ctx, s_win], axis=-1), sink).astype(v.dtype)
        return (jnp.einsum('bkgqt,btkd->bqkgd', p[..., :P], v_ctx)
                + jnp.einsum('bkgqt,btkd->bqkgd', p[..., P:], vc))

    out = lax.map(block, (qb, kw, vw, jnp.arange(nb)))
    return jnp.moveaxis(out, 0, 1).reshape(B, L, Q_A)


def retention_scan(q, k, v, log_gamma, s0):
    dt = q.dtype
    B, L, H, _ = q.shape
    C = RET_CHUNK
    n = L // C
    idx = jnp.arange(C, dtype=jnp.float32)
    diff = idx[:, None] - idx[None, :]
    intra = jnp.where(diff[None] >= 0, jnp.exp(jnp.maximum(diff, 0.0)[None] * log_gamma[:, None, None]), 0.0).astype(dt)
    q_dec = jnp.exp((idx + 1.0)[:, None] * log_gamma[None, :]).astype(dt)
    k_dec = jnp.exp((C - 1.0 - idx)[:, None] * log_gamma[None, :]).astype(dt)
    c_dec = jnp.exp(C * log_gamma).astype(dt)

    def chunks(t):
        return jnp.moveaxis(t.reshape(B, n, C, H, t.shape[-1]), 1, 0)

    def step(S, xs):
        qc, kc, vc = xs
        a = jnp.einsum('bihd,bjhd->bhij', qc, kc) * intra
        o = (jnp.einsum('bhij,bjhe->bihe', a, vc)
             + jnp.einsum('bihd,bhde->bihe', qc, S) * q_dec[None, :, :, None])
        S = S * c_dec[None, :, None, None] + jnp.einsum('bjhd,bjhe->bhde', kc * k_dec[None, :, :, None], vc)
        return S, o

    S, o = lax.scan(step, s0.astype(dt), (chunks(q), chunks(k), chunks(v)))
    return jnp.moveaxis(o, 0, 1).reshape(B, L, H, v.shape[-1]), S


def bidir_retention(q, k, v, lg_fwd, lg_bwd, s0_fwd, s0_bwd):
    o_f, s_f = retention_scan(q, k, v, lg_fwd, s0_fwd)
    o_b, s_b = retention_scan(q[:, ::-1], k[:, ::-1], v[:, ::-1], lg_bwd, s0_bwd)
    return o_f + o_b[:, ::-1], s_f, s_b


def mixer(h, w_in, q_norm_w, k_norm_w, sink, lg_fwd, lg_bwd, gn_w, w_pa, w_pb, w_o,
          pos, k_ctx, v_ctx, s0_fwd, s0_bwd):
    B, L, _ = h.shape
    qa, ka, va, qr, kr, vr, gr, ga, gb = jnp.split(h @ w_in, SPLIT_POINTS, axis=-1)
    qa = rmsnorm(qa.reshape(B, L, N_HEADS_A, HEAD_DIM_A), q_norm_w)
    ka = rmsnorm(ka.reshape(B, L, N_KV_A, HEAD_DIM_A), k_norm_w)
    va = va.reshape(B, L, N_KV_A, HEAD_DIM_A)
    qr = qr.reshape(B, L, N_HEADS_R, DK_R)
    kr = kr.reshape(B, L, N_HEADS_R, DK_R) * (DK_R ** -0.5)
    vr = vr.reshape(B, L, N_HEADS_R, DV_R)
    if pos is None:
        o_a = attn_context(qa, ka, va, sink)
        s0_fwd = jnp.zeros((B, N_HEADS_R, DK_R, DV_R), qr.dtype)
        s0_bwd = s0_fwd
    else:
        pos_row, pos_col = pos
        o_a = attn_latent(rope_2d(qa, pos_row, pos_col), rope_2d(ka, pos_row, pos_col), va, k_ctx, v_ctx, sink)
        qr = rope_2d(qr, pos_row, pos_col)
        kr = rope_2d(kr, pos_row, pos_col)
    o_r, s_f, s_b = bidir_retention(qr, kr, vr, lg_fwd, lg_bwd, s0_fwd, s0_bwd)
    y_r = head_groupnorm(o_r, gn_w) * jax.nn.silu(gr)
    merged = jax.nn.sigmoid(ga) * (o_a @ w_pa) + jax.nn.sigmoid(gb) * (y_r @ w_pb)
    return merged @ w_o, ka, va, s_f, s_b


def moe(h, w_router, b_router, w_gate, w_up, w_down):
    B, L, _ = h.shape
    scores = jax.nn.sigmoid((h @ w_router).astype(jnp.float32))
    sel = scores + b_router.astype(jnp.float32)
    group_score = lax.top_k(sel.reshape(B, L, N_GROUPS, EXPERTS_PER_GROUP), TOP_K)[0].sum(-1)
    best_group = jnp.argmax(group_score, axis=-1)
    in_group = (jnp.arange(N_EXPERTS) // EXPERTS_PER_GROUP) == best_group[..., None]
    _, idx = lax.top_k(jnp.where(in_group, sel, NEG_INF), TOP_K)
    w = jnp.take_along_axis(scores, idx, axis=-1)
    w = w / jnp.sum(w, axis=-1, keepdims=True)
    combine = jnp.sum(jax.nn.one_hot(idx, N_EXPERTS, dtype=jnp.float32) * w[..., None], axis=-2).astype(h.dtype)
    a = jnp.einsum('bld,edf->blef', h, w_gate)
    u = jnp.einsum('bld,edf->blef', h, w_up)
    act = jax.nn.silu(a) * u * combine[..., None]
    return jnp.einsum('blef,efd->bld', act, w_down)


def setup_inputs(seed: int = 0) -> dict:
    key = jax.random.key(seed)
    ks = jax.random.split(key, 32)
    f32 = jnp.float32

    def nrm(k, shape, scale=1.0):
        return jax.random.normal(k, shape, f32) * scale

    base_decay = -(5.0 + jnp.arange(N_HEADS_R, dtype=f32)) * jnp.log(2.0)
    return {
        'x_prompt': nrm(ks[0], (BATCH, SEQ, D_MODEL)),
        'x_sample': nrm(ks[1], (DEC_BATCH, DEC_SEQ, D_MODEL)),
        'c': nrm(ks[2], (DEC_BATCH, D_MODEL)),
        'cache_k': nrm(ks[3], (DEC_BATCH, DEPTH, PAST_LEN, N_KV_A, HEAD_DIM_A)),
        'cache_v': nrm(ks[4], (DEC_BATCH, DEPTH, PAST_LEN, N_KV_A, HEAD_DIM_A)),
        'state_ret_fwd': nrm(ks[5], (DEC_BATCH, DEPTH, N_HEADS_R, DK_R, DV_R)),
        'state_ret_bwd': nrm(ks[6], (DEC_BATCH, DEPTH, N_HEADS_R, DK_R, DV_R)),
        'c_ctx': nrm(ks[7], (D_MODEL,)),
        'norm1_w': 1.0 + nrm(ks[8], (DEPTH, D_MODEL), 0.02),
        'norm2_w': 1.0 + nrm(ks[9], (DEPTH, D_MODEL), 0.02),
        'w_ada': nrm(ks[10], (DEPTH, D_MODEL, 6 * D_MODEL), 0.5 * D_MODEL ** -0.5),
        'b_ada': nrm(ks[11], (DEPTH, 6 * D_MODEL), 0.01),
        'w_in': nrm(ks[12], (DEPTH, D_MODEL, D_IN), D_MODEL ** -0.5),
        'q_norm_w': 1.0 + nrm(ks[13], (DEPTH, HEAD_DIM_A), 0.02),
        'k_norm_w': 1.0 + nrm(ks[14], (DEPTH, HEAD_DIM_A), 0.02),
        'attn_sink': nrm(ks[15], (DEPTH, N_HEADS_A), 0.5),
        'ret_decay_fwd': base_decay[None, :] + nrm(ks[16], (DEPTH, N_HEADS_R), 0.1),
        'ret_decay_bwd': base_decay[None, :] + nrm(ks[17], (DEPTH, N_HEADS_R), 0.1),
        'ret_gn_w': 1.0 + nrm(ks[18], (DEPTH, V_R), 0.02),
        'w_pa': nrm(ks[19], (DEPTH, Q_A, D_MODEL), Q_A ** -0.5),
        'w_pb': nrm(ks[20], (DEPTH, V_R, D_MODEL), V_R ** -0.5),
        'w_o': nrm(ks[21], (DEPTH, D_MODEL, D_MODEL), D_MODEL ** -0.5),
        'w_router': nrm(ks[22], (D_MODEL, N_EXPERTS), D_MODEL ** -0.5),
        'b_router': nrm(ks[23], (N_EXPERTS,), 0.01),
        'w_exp_gate': nrm(ks[24], (DEPTH, N_EXPERTS, D_MODEL, D_EXPERT), D_MODEL ** -0.5),
        'w_exp_up': nrm(ks[25], (DEPTH, N_EXPERTS, D_MODEL, D_EXPERT), D_MODEL ** -0.5),
        'w_exp_down': nrm(ks[26], (DEPTH, N_EXPERTS, D_EXPERT, D_MODEL), D_EXPERT ** -0.5),
    }


def reference(x_prompt, x_sample, c, cache_k, cache_v, state_ret_fwd, state_ret_bwd, c_ctx,
              norm1_w, norm2_w, w_ada, b_ada, w_in, q_norm_w, k_norm_w, attn_sink,
              ret_decay_fwd, ret_decay_bwd, ret_gn_w, w_pa, w_pb, w_o, w_router, b_router,
              w_exp_gate, w_exp_up, w_exp_down):
    pos = grid_positions(x_sample.shape[1])
    xp, xs = x_prompt, x_sample
    new_k, new_v, new_sf, new_sb = [], [], [], []
    for l in range(DEPTH):
        lg_f = jnp.log1p(-jnp.exp(ret_decay_fwd[l].astype(jnp.float32)))
        lg_b = jnp.log1p(-jnp.exp(ret_decay_bwd[l].astype(jnp.float32)))

        sh1, sc1, g1, sh2, sc2, g2 = adaln(c_ctx[None, :], w_ada[l], b_ada[l])
        h = modulate(rmsnorm(xp, norm1_w[l]), sh1, sc1)
        mix, k_l, v_l, sf_l, sb_l = mixer(h, w_in[l], q_norm_w[l], k_norm_w[l], attn_sink[l], lg_f, lg_b,
                                          ret_gn_w[l], w_pa[l], w_pb[l], w_o[l], None, None, None, None, None)
        xp = xp + g1 * mix
        h = modulate(rmsnorm(xp, norm2_w[l]), sh2, sc2)
        xp = xp + g2 * moe(h, w_router, b_router, w_exp_gate[l], w_exp_up[l], w_exp_down[l])
        new_k.append(k_l)
        new_v.append(v_l)
        new_sf.append(sf_l)
        new_sb.append(sb_l)

        sh1, sc1, g1, sh2, sc2, g2 = adaln(c, w_ada[l], b_ada[l])
        h = modulate(rmsnorm(xs, norm1_w[l]), sh1, sc1)
        mix = mixer(h, w_in[l], q_norm_w[l], k_norm_w[l], attn_sink[l], lg_f, lg_b,
                    ret_gn_w[l], w_pa[l], w_pb[l], w_o[l], pos, cache_k[:, l], cache_v[:, l],
                    state_ret_fwd[:, l], state_ret_bwd[:, l])[0]
        xs = xs + g1 * mix
        h = modulate(rmsnorm(xs, norm2_w[l]), sh2, sc2)
        xs = xs + g2 * moe(h, w_router, b_router, w_exp_gate[l], w_exp_up[l], w_exp_down[l])

    new_cache_k = jnp.stack(new_k, axis=1)
    new_cache_v = jnp.stack(new_v, axis=1)
    new_state_ret_fwd = jnp.stack(new_sf, axis=1)
    new_state_ret_bwd = jnp.stack(new_sb, axis=1)
    return (xp, xs, new_cache_k, new_cache_v, new_state_ret_fwd, new_state_ret_bwd)
```

```python
import functools

import numpy as np
import jax
import jax.numpy as jnp
from jax import lax
from jax.experimental import pallas as pl
from jax.experimental.pallas import tpu as pltpu

D = 1024
BATCH, SEQ = 16, 256
DEC_BATCH, DEC_SEQ = 2, 2048
DEPTH = 2
PAST = 512
GRID_W = 64
NH_A, NKV_A, HD_A = 8, 2, 64
WINDOW = 128
NH_R, DK_R, DV_R = 4, 64, 128
CHUNK = 128
N_EXP, N_GROUPS, EXP_PER_GROUP = 16, 4, 4
D_EXP = 512
ROPE_BASE = 10000.0
EPS = 1e-6
NEG_INF = -1e30

Q_A = NH_A * HD_A
KV_A = NKV_A * HD_A
QK_R = NH_R * DK_R
V_R = NH_R * DV_R
C_QA = (0, Q_A)
C_KA = (C_QA[1], C_QA[1] + KV_A)
C_VA = (C_KA[1], C_KA[1] + KV_A)
C_QR = (C_VA[1], C_VA[1] + QK_R)
C_KR = (C_QR[1], C_QR[1] + QK_R)
C_VR = (C_KR[1], C_KR[1] + V_R)
C_GR = (C_VR[1], C_VR[1] + V_R)
C_GA = (C_GR[1], C_GR[1] + D)
C_GB = (C_GA[1], C_GA[1] + D)
D_IN = C_GB[1]

T_CTX = BATCH * SEQ
T_LAT = DEC_BATCH * DEC_SEQ
T = T_CTX + T_LAT
N_COND = 8

LANES = 128
V7X_VMEM_LIMIT = 56 * 1024 * 1024

TM_PROJ = 512
TM_POST = 256
TM_MOE = 1024
ATT_QB = 256

f32 = jnp.float32
bf16 = jnp.bfloat16


def _dot(a, b):
    return jnp.dot(a, b, preferred_element_type=f32)


def _dot_t(a, b):
    return lax.dot_general(a, b, (((1,), (1,)), ((), ())), preferred_element_type=f32)


def _split(x):
    hi = x.astype(bf16)
    lo = (x - hi.astype(f32)).astype(bf16)
    return hi, lo


def _dot3(a, b):
    ah, al = _split(a)
    bh, bl = _split(b)
    return _dot(ah, bh) + (_dot(ah, bl) + _dot(al, bh))


def _sigmoid(x):
    return 1.0 / (1.0 + jnp.exp(-x))


def _silu(x):
    return x * _sigmoid(x)


def _cond_of_tile(i, tm):
    n_ctx = T_CTX // tm
    per_b = DEC_SEQ // tm
    return jnp.where(i < n_ctx, 0, 1 + jnp.maximum(i - n_ctx, 0) // per_b)


def _adaln_kernel(cond_ref, w_ref, b_ref, o_ref):
    a = _silu(cond_ref[...])
    o_ref[0] = _dot3(a, w_ref[0]) + b_ref[0]


def _adaln(cond8, w_ada, b_ada):
    tn = 1536
    return pl.pallas_call(
        _adaln_kernel,
        grid=(DEPTH, 6 * D // tn),
        in_specs=[
            pl.BlockSpec((N_COND, D), lambda l, j: (0, 0)),
            pl.BlockSpec((1, D, tn), lambda l, j: (l, 0, j)),
            pl.BlockSpec((1, 1, tn), lambda l, j: (l, 0, j)),
        ],
        out_specs=pl.BlockSpec((1, N_COND, tn), lambda l, j: (l, 0, j)),
        out_shape=jax.ShapeDtypeStruct((DEPTH, N_COND, 6 * D), f32),
        compiler_params=pltpu.CompilerParams(
            dimension_semantics=("parallel", "parallel"), vmem_limit_bytes=V7X_VMEM_LIMIT),
        name="adaln",
    )(cond8, w_ada, b_ada.reshape(DEPTH, 1, 6 * D))


def _rope(x, cos, sin_signed, first_half):
    fwd = pltpu.roll(x, 16, 1)
    bwd = pltpu.roll(x, LANES - 16, 1)
    partner = jnp.where(first_half, bwd, fwd)
    return x * cos + partner * sin_signed


def _head_rms(x, ones_blk, w):
    n = x.shape[1]
    sq_hi, sq_lo = _split(x * x)
    blk = ones_blk[0:n, 0:n]
    mean = _dot(sq_hi, blk) + _dot(sq_lo, blk)
    return x * lax.rsqrt(mean + EPS) * w


def _inproj_kernel(x_ref, n1_ref, mod_ref, w_ref, cos_ref, sin_ref, ones_ref, qnw_ref, knw_ref,
                   qa_ref, ka_ref, va_ref, qr_ref, kr_ref, vr_ref, gr_ref, ga_ref, gb_ref):
    x = x_ref[...]
    y = x * lax.rsqrt(jnp.mean(x * x, axis=-1, keepdims=True) + EPS) * n1_ref[...]
    h = (y * (1.0 + mod_ref[1:2, :]) + mod_ref[0:1, :]).astype(bf16)

    def proj(c):
        return _dot(h, w_ref[:, c[0]:c[1]])

    cos = cos_ref[...]
    sin = sin_ref[...]
    lane = lax.broadcasted_iota(jnp.int32, cos.shape, 1)
    first_half = (lane % 32) < 16

    def rope_all(v):
        parts = [_rope(v[:, j:j + LANES], cos, sin, first_half) for j in range(0, v.shape[1], LANES)]
        return parts[0] if len(parts) == 1 else jnp.concatenate(parts, axis=1)

    ones_blk = ones_ref[...]
    qa = _head_rms(proj(C_QA), ones_blk, qnw_ref[...])
    qa_ref[...] = rope_all(qa).astype(bf16)
    ka_ref[...] = _head_rms(proj(C_KA), ones_blk, knw_ref[...])
    va_ref[...] = proj(C_VA)
    qr_ref[...] = rope_all(proj(C_QR)).astype(bf16)
    kr_ref[...] = (rope_all(proj(C_KR)) * (DK_R ** -0.5)).astype(bf16)
    vr_ref[...] = proj(C_VR).astype(bf16)
    gr_ref[...] = proj(C_GR).astype(bf16)
    ga_ref[...] = proj(C_GA).astype(bf16)
    gb_ref[...] = proj(C_GB).astype(bf16)


def _inproj(l, x, norm1_w, mod, w_in_bf, cos_t, sin_t, ones_blk, qnw, knw):
    tm = TM_PROJ
    n_ctx = T_CTX // tm
    per_b = DEC_SEQ // tm

    def tab_map(i):
        return (jnp.where(i < n_ctx, per_b, jnp.maximum(i - n_ctx, 0) % per_b), 0)

    row = lambda i: (i, 0)
    const = lambda i: (0, 0)
    widths = [(Q_A, bf16), (KV_A, f32), (KV_A, f32), (QK_R, bf16), (QK_R, bf16), (V_R, bf16),
              (V_R, bf16), (D, bf16), (D, bf16)]
    return pl.pallas_call(
        _inproj_kernel,
        grid=(T // tm,),
        in_specs=[
            pl.BlockSpec((tm, D), row),
            pl.BlockSpec((1, D), const),
            pl.BlockSpec((None, None, 6, D), lambda i: (l, _cond_of_tile(i, tm), 0, 0)),
            pl.BlockSpec((None, D, D_IN), lambda i: (l, 0, 0)),
            pl.BlockSpec((tm, LANES), tab_map),
            pl.BlockSpec((tm, LANES), tab_map),
            pl.BlockSpec((Q_A, Q_A), const),
            pl.BlockSpec((1, Q_A), const),
            pl.BlockSpec((1, KV_A), const),
        ],
        out_specs=[pl.BlockSpec((tm, w), row) for w, _ in widths],
        out_shape=[jax.ShapeDtypeStruct((T, w), dt) for w, dt in widths],
        compiler_params=pltpu.CompilerParams(
            dimension_semantics=("parallel",), vmem_limit_bytes=V7X_VMEM_LIMIT),
        name="inproj",
    )(x, norm1_w[l:l + 1], mod, w_in_bf, cos_t, sin_t, ones_blk, qnw[l:l + 1], knw[l:l + 1])


def _head_blocks(t, kv, lo_mask):
    r = pltpu.roll(t, HD_A, 1)
    if kv == 0:
        a = jnp.where(lo_mask, t, 0.0)
        b = jnp.where(lo_mask, 0.0, r)
    else:
        a = jnp.where(lo_mask, r, 0.0)
        b = jnp.where(lo_mask, 0.0, t)
    return jnp.concatenate([a, b], axis=0).astype(bf16)


def _ctx_attn_kernel(sink_ref, q_ref, k_ref, v_ref, o_ref):
    k = k_ref[...]
    v = v_ref[...]
    n = k.shape[0]
    lo_mask = lax.broadcasted_iota(jnp.int32, k.shape, 1) < HD_A
    scale = HD_A ** -0.5
    for kv in range(NKV_A):
        kblk = _head_blocks(k, kv, lo_mask)
        vblk = _head_blocks(v, kv, lo_mask)
        for pr in range(2):
            pi = kv * 2 + pr
            s = _dot_t(q_ref[:, pi * LANES:(pi + 1) * LANES], kblk) * scale
            ps = []
            for hh in range(2):
                sk = sink_ref[2 * pi + hh]
                sh = s[:, hh * n:(hh + 1) * n]
                m = jnp.maximum(jnp.max(sh, axis=-1, keepdims=True), sk)
                p = jnp.exp(sh - m)
                den = jnp.sum(p, axis=-1, keepdims=True) + jnp.exp(sk - m)
                ps.append((p / den).astype(bf16))
            o = _dot(jnp.concatenate(ps, axis=1), vblk)
            o_ref[:, pi * LANES:(pi + 1) * LANES] = o.astype(bf16)


def _ctx_attn(l, qa, ka, va, sink):
    blk = lambda w: pl.BlockSpec((SEQ, w), lambda b: (b, 0))
    return pl.pallas_call(
        _ctx_attn_kernel,
        grid=(BATCH,),
        in_specs=[pl.BlockSpec(memory_space=pltpu.SMEM), blk(Q_A), blk(KV_A), blk(KV_A)],
        out_specs=blk(Q_A),
        out_shape=jax.ShapeDtypeStruct((T_CTX, Q_A), bf16),
        compiler_params=pltpu.CompilerParams(
            dimension_semantics=("parallel",), vmem_limit_bytes=V7X_VMEM_LIMIT),
        name="ctx_attn",
    )(sink[l], qa, ka, va)


def _lat_attn_kernel(sink_ref, q_ref, k_ref, v_ref, kc_ref, vc_ref, cos_ref, sin_ref, o_ref):
    j = pl.program_id(1)
    qb = ATT_QB
    win = 2 * qb
    ws = pl.multiple_of(jnp.clip(j * qb - WINDOW, 0, DEC_SEQ - win), WINDOW)
    lo_mask = lax.broadcasted_iota(jnp.int32, (win, LANES), 1) < HD_A
    lane = lax.broadcasted_iota(jnp.int32, (win, LANES), 1)
    kw = _rope(k_ref[pl.ds(ws, win), :], cos_ref[pl.ds(ws, win), :], sin_ref[pl.ds(ws, win), :],
               (lane % 32) < 16)
    vw = v_ref[pl.ds(ws, win), :]
    kc = kc_ref[...]
    vc = vc_ref[...]
    qpos = j * qb + (lax.broadcasted_iota(jnp.int32, (2 * qb, win), 0) & (qb - 1))
    kpos = ws + lax.broadcasted_iota(jnp.int32, (2 * qb, win), 1)
    valid = jnp.abs(qpos - kpos) <= WINDOW
    scale = HD_A ** -0.5
    for kv in range(NKV_A):
        kc_blk = _head_blocks(kc, kv, lo_mask[:PAST])
        vc_blk = _head_blocks(vc, kv, lo_mask[:PAST])
        kw_blk = _head_blocks(kw, kv, lo_mask)
        vw_blk = _head_blocks(vw, kv, lo_mask)
        q2 = jnp.concatenate([q_ref[:, (2 * kv) * LANES:(2 * kv + 1) * LANES],
                              q_ref[:, (2 * kv + 1) * LANES:(2 * kv + 2) * LANES]], axis=0)
        s_c = _dot_t(q2, kc_blk) * scale
        s_w = _dot_t(q2, kw_blk) * scale
        pcs, pws = [], []
        for hh in range(2):
            row = lax.broadcasted_iota(jnp.int32, (2 * qb, 1), 0)
            sk = jnp.where(row < qb, sink_ref[4 * kv + hh], sink_ref[4 * kv + 2 + hh])
            sc = s_c[:, hh * PAST:(hh + 1) * PAST]
            sw = jnp.where(valid, s_w[:, hh * win:(hh + 1) * win], NEG_INF)
            m = jnp.maximum(jnp.maximum(jnp.max(sc, axis=-1, keepdims=True),
                                        jnp.max(sw, axis=-1, keepdims=True)), sk)
            pc = jnp.exp(sc - m)
            pw = jnp.exp(sw - m)
            den = (jnp.sum(pc, axis=-1, keepdims=True) + jnp.sum(pw, axis=-1, keepdims=True)
                   + jnp.exp(sk - m))
            inv = 1.0 / den
            pcs.append((pc * inv).astype(bf16))
            pws.append((pw * inv).astype(bf16))
        o = _dot(jnp.concatenate(pcs, axis=1), vc_blk) + _dot(jnp.concatenate(pws, axis=1), vw_blk)
        o_ref[:, (2 * kv) * LANES:(2 * kv + 1) * LANES] = o[:qb].astype(bf16)
        o_ref[:, (2 * kv + 1) * LANES:(2 * kv + 2) * LANES] = o[qb:].astype(bf16)


def _lat_attn(l, qa, ka, va, cache_k, cache_v, cos_l, sin_l, sink):
    qb = ATT_QB
    nq = DEC_SEQ // qb
    ctx_blocks = T_CTX // DEC_SEQ
    seq = lambda b, j: (ctx_blocks + b, 0)
    return pl.pallas_call(
        _lat_attn_kernel,
        grid=(DEC_BATCH, nq),
        in_specs=[
            pl.BlockSpec(memory_space=pltpu.SMEM),
            pl.BlockSpec((qb, Q_A), lambda b, j: (T_CTX // qb + b * nq + j, 0)),
            pl.BlockSpec((DEC_SEQ, KV_A), seq),
            pl.BlockSpec((DEC_SEQ, KV_A), seq),
            pl.BlockSpec((None, None, PAST, KV_A), lambda b, j: (b, l, 0, 0)),
            pl.BlockSpec((None, None, PAST, KV_A), lambda b, j: (b, l, 0, 0)),
            pl.BlockSpec((DEC_SEQ, LANES), lambda b, j: (0, 0)),
            pl.BlockSpec((DEC_SEQ, LANES), lambda b, j: (0, 0)),
        ],
        out_specs=pl.BlockSpec((qb, Q_A), lambda b, j: (b * nq + j, 0)),
        out_shape=jax.ShapeDtypeStruct((T_LAT, Q_A), bf16),
        compiler_params=pltpu.CompilerParams(
            dimension_semantics=("parallel", "parallel"), vmem_limit_bytes=V7X_VMEM_LIMIT),
        name="lat_attn",
    )(sink[l], qa, ka, va, cache_k, cache_v, cos_l, sin_l)


def _dup_heads(pair, lo_mask):
    r = pltpu.roll(pair, DK_R, 1)
    return jnp.where(lo_mask, pair, r), jnp.where(lo_mask, r, pair)


def _retention_kernel(has_s0, n_chunks, *refs):
    if has_s0:
        (q_ref, k_ref, v_ref, g_ref, s0_ref, mask_ref, qdec_ref, kdec_ref, cdec_ref, gnw_ref,
         y_ref, sf_ref, sb_ref, ds_ref, st_ref) = refs
    else:
        (q_ref, k_ref, v_ref, g_ref, mask_ref, qdec_ref, kdec_ref, cdec_ref, gnw_ref,
         y_ref, sf_ref, sb_ref, ds_ref, st_ref) = refs
        s0_ref = None
    C = CHUNK
    lo_mask = lax.broadcasted_iota(jnp.int32, (C, LANES), 1) < DK_R

    def inc_body(c, carry):
        r0 = pl.multiple_of(c * C, C)
        for pr in range(2):
            kp = k_ref[pl.ds(r0, C), pr * LANES:(pr + 1) * LANES].astype(f32)
            for hh, kd in enumerate(_dup_heads(kp, lo_mask)):
                h = 2 * pr + hh
                kd = (kd * kdec_ref[:, h * LANES:(h + 1) * LANES]).astype(bf16)
                vh = v_ref[pl.ds(r0, C), h * DV_R:(h + 1) * DV_R]
                ds_ref[c, h] = lax.dot_general(kd, vh, (((0,), (0,)), ((), ())),
                                               preferred_element_type=f32)
        return carry

    lax.fori_loop(0, n_chunks, inc_body, 0)

    for h in range(NH_R):
        cf = cdec_ref[h, 0:DK_R, :]
        cb = cdec_ref[h, DK_R:2 * DK_R, :]
        if has_s0:
            init_f = s0_ref[h, 0:DK_R, :]
            init_b = s0_ref[h, DK_R:2 * DK_R, :]
        else:
            init_f = jnp.zeros((DK_R, DV_R), f32)
            init_b = init_f

        def fwd_body(c, s, h=h, cf=cf):
            st_ref[c, h, 0:DK_R, :] = s
            return s * cf + ds_ref[c, h, 0:DK_R, :]

        def bwd_body(i, s, h=h, cb=cb):
            c = n_chunks - 1 - i
            st_ref[c, h, DK_R:2 * DK_R, :] = s
            return s * cb + ds_ref[c, h, DK_R:2 * DK_R, :]

        sf_ref[h] = lax.fori_loop(0, n_chunks, fwd_body, init_f)
        sb_ref[h] = lax.fori_loop(0, n_chunks, bwd_body, init_b)

    def out_body(c, carry):
        r0 = pl.multiple_of(c * C, C)
        for pr in range(2):
            qp = q_ref[pl.ds(r0, C), pr * LANES:(pr + 1) * LANES]
            kp = k_ref[pl.ds(r0, C), pr * LANES:(pr + 1) * LANES].astype(f32)
            kblk = jnp.concatenate([jnp.where(lo_mask, kp, 0.0), jnp.where(lo_mask, 0.0, kp)],
                                   axis=0).astype(bf16)
            a2 = _dot_t(qp, kblk)
            for hh, qd in enumerate(_dup_heads(qp.astype(f32), lo_mask)):
                h = 2 * pr + hh
                a = (a2[:, hh * C:(hh + 1) * C] * mask_ref[h]).astype(bf16)
                vh = v_ref[pl.ds(r0, C), h * DV_R:(h + 1) * DV_R]
                qd = (qd * qdec_ref[:, h * LANES:(h + 1) * LANES]).astype(bf16)
                o = _dot(a, vh) + _dot(qd, st_ref[c, h].astype(bf16))
                mu = jnp.mean(o, axis=-1, keepdims=True)
                d = o - mu
                var = jnp.mean(d * d, axis=-1, keepdims=True)
                yh = d * lax.rsqrt(var + EPS) * gnw_ref[:, h * DV_R:(h + 1) * DV_R]
                g = g_ref[pl.ds(r0, C), h * DV_R:(h + 1) * DV_R].astype(f32)
                y_ref[pl.ds(r0, C), h * DV_R:(h + 1) * DV_R] = (yh * _silu(g)).astype(bf16)
        return carry

    lax.fori_loop(0, n_chunks, out_body, 0)


def _retention(l, qr, kr, vr, gr, s0, dec, gnw, nb, seq, row_block0):
    n_chunks = seq // CHUNK
    mask, qdec, kdec, cdec = dec
    has_s0 = s0 is not None
    tok = lambda w: pl.BlockSpec((seq, w), lambda b: (row_block0 + b, 0))
    full = lambda a: pl.BlockSpec(a.shape, lambda b: (0,) * a.ndim)
    in_specs = [tok(QK_R), tok(QK_R), tok(V_R), tok(V_R)]
    args = [qr, kr, vr, gr]
    if has_s0:
        in_specs.append(pl.BlockSpec((None, NH_R, 2 * DK_R, DV_R), lambda b: (b, 0, 0, 0)))
        args.append(s0)
    in_specs += [full(mask), full(qdec), full(kdec), full(cdec),
                 pl.BlockSpec((1, V_R), lambda b: (0, 0))]
    args += [mask, qdec, kdec, cdec, gnw[l:l + 1]]
    st_spec = pl.BlockSpec((None, NH_R, DK_R, DV_R), lambda b: (b, 0, 0, 0))
    return pl.pallas_call(
        functools.partial(_retention_kernel, has_s0, n_chunks),
        grid=(nb,),
        in_specs=in_specs,
        out_specs=[pl.BlockSpec((seq, V_R), lambda b: (b, 0)), st_spec, st_spec],
        out_shape=[jax.ShapeDtypeStruct((nb * seq, V_R), bf16),
                   jax.ShapeDtypeStruct((nb, NH_R, DK_R, DV_R), f32),
                   jax.ShapeDtypeStruct((nb, NH_R, DK_R, DV_R), f32)],
        scratch_shapes=[pltpu.VMEM((n_chunks, NH_R, 2 * DK_R, DV_R), f32),
                        pltpu.VMEM((n_chunks, NH_R, 2 * DK_R, DV_R), f32)],
        compiler_params=pltpu.CompilerParams(
            dimension_semantics=("parallel",), vmem_limit_bytes=V7X_VMEM_LIMIT),
        name="retention_lat" if has_s0 else "retention_ctx",
    )(*args)


def _decay_tables(lg_f, lg_b):
    C = CHUNK
    idx = jnp.arange(C, dtype=f32)
    diff = idx[:, None] - idx[None, :]
    lower = jnp.where(diff >= 0, jnp.exp(jnp.maximum(diff, 0.0)[None] * lg_f[:, None, None]), 0.0)
    upper = jnp.where(diff <= 0, jnp.exp(jnp.maximum(-diff, 0.0)[None] * lg_b[:, None, None]), 0.0)
    mask = lower + upper
    qf = jnp.exp((idx + 1.0)[:, None] * lg_f[None, :])
    qb = jnp.exp((C - idx)[:, None] * lg_b[None, :])
    kf = jnp.exp((C - 1.0 - idx)[:, None] * lg_f[None, :])
    kb = jnp.exp(idx[:, None] * lg_b[None, :])

    def lanes(f, b):
        both = jnp.stack([f, b], axis=-1)
        return jnp.repeat(both, DK_R, axis=-1).reshape(C, NH_R * 2 * DK_R)

    cf = jnp.exp(C * lg_f)
    cb = jnp.exp(C * lg_b)
    cdec = jnp.broadcast_to(jnp.stack([cf, cb], axis=-1)[:, :, None, None],
                            (NH_R, 2, DK_R, DV_R)).reshape(NH_R, 2 * DK_R, DV_R)
    return mask, lanes(qf, qb), lanes(kf, kb), cdec


def _post_kernel(x_ref, oa_ref, yr_ref, ga_ref, gb_ref, wpa_ref, wpb_ref, wo_ref, mod_ref, n2_ref,
                 wr_ref, br_ref, tri_ref, x1_ref, h2_ref, route_ref, cnt_ref, carry_ref):
    i = pl.program_id(0)
    tm = x_ref.shape[0]

    @pl.when(i == 0)
    def _():
        carry_ref[...] = jnp.zeros_like(carry_ref)

    ga = _sigmoid(ga_ref[...].astype(f32))
    gb = _sigmoid(gb_ref[...].astype(f32))
    merged = ga * _dot(oa_ref[...], wpa_ref[...]) + gb * _dot(yr_ref[...], wpb_ref[...])
    mix = _dot(merged.astype(bf16), wo_ref[...])
    x1 = x_ref[...] + mod_ref[2:3, :] * mix
    x1_ref[...] = x1
    y = x1 * lax.rsqrt(jnp.mean(x1 * x1, axis=-1, keepdims=True) + EPS) * n2_ref[...]
    h2 = y * (1.0 + mod_ref[4:5, :]) + mod_ref[3:4, :]
    h2_ref[...] = h2.astype(bf16)

    logits = _dot3(h2, wr_ref[...])
    lt = logits.T[0:N_EXP, :]
    scores = _sigmoid(lt)
    sel = scores + br_ref[:, 0:1]
    row = lax.broadcasted_iota(jnp.int32, (N_EXP, tm), 0)

    best = None
    bg = None
    for g in range(N_GROUPS):
        a, b, c, d = (sel[EXP_PER_GROUP * g + k:EXP_PER_GROUP * g + k + 1, :] for k in range(4))
        p, q = jnp.maximum(a, b), jnp.minimum(a, b)
        r, s = jnp.maximum(c, d), jnp.minimum(c, d)
        gs = jnp.maximum(p, r) + jnp.maximum(jnp.minimum(p, r), jnp.maximum(q, s))
        if g == 0:
            best, bg = gs, jnp.zeros((1, tm), jnp.int32)
        else:
            upd = gs > best
            bg = jnp.where(upd, g, bg)
            best = jnp.where(upd, gs, best)
    masked = jnp.where(jnp.right_shift(row, 2) == bg, sel, NEG_INF)
    m1 = jnp.max(masked, axis=0, keepdims=True)
    i1 = jnp.min(jnp.where(masked == m1, row, N_EXP), axis=0, keepdims=True)
    masked2 = jnp.where(row == i1, NEG_INF, masked)
    m2 = jnp.max(masked2, axis=0, keepdims=True)
    i2 = jnp.min(jnp.where(masked2 == m2, row, N_EXP), axis=0, keepdims=True)
    oh1 = row == i1
    oh2 = row == i2
    s1 = jnp.sum(jnp.where(oh1, scores, 0.0), axis=0, keepdims=True)
    s2 = jnp.sum(jnp.where(oh2, scores, 0.0), axis=0, keepdims=True)
    den = s1 + s2

    oh = jnp.where(oh1 | oh2, 1.0, 0.0)
    tot = carry_ref[:, 0:1] + _dot(oh.astype(bf16), tri_ref[...])
    r1 = jnp.sum(jnp.where(oh1, tot, 0.0), axis=0, keepdims=True)
    r2 = jnp.sum(jnp.where(oh2, tot, 0.0), axis=0, keepdims=True)
    carry_ref[...] = carry_ref[...] + jnp.sum(oh, axis=1, keepdims=True)
    cnt_ref[...] = carry_ref[...]

    route_ref[0:1, :] = i1.astype(f32)
    route_ref[1:2, :] = i2.astype(f32)
    route_ref[2:3, :] = r1
    route_ref[3:4, :] = r2
    route_ref[4:5, :] = s1 / den
    route_ref[5:6, :] = s2 / den
    route_ref[6:8, :] = jnp.zeros((2, tm), f32)


def _post(l, x, oa, yr, ga, gb, wpa, wpb, wo, mod, norm2_w, wr_pad, br_col, tri):
    tm = TM_POST
    row = lambda w: pl.BlockSpec((tm, w), lambda i: (i, 0))
    lay = lambda a: pl.BlockSpec((None,) + a.shape[1:], lambda i: (l,) + (0,) * (a.ndim - 1))
    const = lambda a: pl.BlockSpec(a.shape, lambda i: (0,) * a.ndim)
    return pl.pallas_call(
        _post_kernel,
        grid=(T // tm,),
        in_specs=[row(D), row(Q_A), row(V_R), row(D), row(D), lay(wpa), lay(wpb), lay(wo),
                  pl.BlockSpec((None, None, 6, D), lambda i: (l, _cond_of_tile(i, tm), 0, 0)),
                  pl.BlockSpec((1, D), lambda i: (0, 0)), const(wr_pad), const(br_col), const(tri)],
        out_specs=[row(D), row(D), pl.BlockSpec((8, tm), lambda i: (0, i)),
                   pl.BlockSpec((N_EXP, LANES), lambda i: (0, 0))],
        out_shape=[jax.ShapeDtypeStruct((T, D), f32), jax.ShapeDtypeStruct((T, D), bf16),
                   jax.ShapeDtypeStruct((8, T), f32), jax.ShapeDtypeStruct((N_EXP, LANES), f32)],
        scratch_shapes=[pltpu.VMEM((N_EXP, LANES), f32)],
        compiler_params=pltpu.CompilerParams(
            dimension_semantics=("arbitrary",), vmem_limit_bytes=V7X_VMEM_LIMIT),
        name="post_router",
    )(x, oa, yr, ga, gb, wpa, wpb, wo, mod, norm2_w[l:l + 1], wr_pad, br_col, tri)


def _moe_dense_kernel(h_ref, x1_ref, rt_ref, mod_ref, wg_ref, wu_ref, wd_ref, o_ref, acc_ref):
    e = pl.program_id(1)

    @pl.when(e == 0)
    def _():
        acc_ref[...] = jnp.zeros_like(acc_ref)

    h = h_ref[...]
    ef = e.astype(f32)
    comb = (jnp.where(rt_ref[:, 0:1] == ef, rt_ref[:, 4:5], 0.0)
            + jnp.where(rt_ref[:, 1:2] == ef, rt_ref[:, 5:6], 0.0))
    act = _silu(_dot(h, wg_ref[...])) * _dot(h, wu_ref[...]) * comb
    acc_ref[...] += _dot(act.astype(bf16), wd_ref[...])

    @pl.when(e == N_EXP - 1)
    def _():
        o_ref[...] = x1_ref[...] + mod_ref[5:6, :] * acc_ref[...]


def _moe_dense(l, h2, x1, route_t, mod, wg, wu, wd):
    tm = TM_MOE
    row = lambda w: pl.BlockSpec((tm, w), lambda i, e: (i, 0))
    return pl.pallas_call(
        _moe_dense_kernel,
        grid=(T // tm, N_EXP),
        in_specs=[row(D), row(D), row(8),
                  pl.BlockSpec((None, None, 6, D), lambda i, e: (l, _cond_of_tile(i, tm), 0, 0)),
                  pl.BlockSpec((None, None, D, D_EXP), lambda i, e: (l, e, 0, 0)),
                  pl.BlockSpec((None, None, D, D_EXP), lambda i, e: (l, e, 0, 0)),
                  pl.BlockSpec((None, None, D_EXP, D), lambda i, e: (l, e, 0, 0))],
        out_specs=row(D),
        out_shape=jax.ShapeDtypeStruct((T, D), f32),
        scratch_shapes=[pltpu.VMEM((tm, D), f32)],
        compiler_params=pltpu.CompilerParams(
            dimension_semantics=("parallel", "arbitrary"), vmem_limit_bytes=V7X_VMEM_LIMIT),
        name="moe_dense",
    )(h2, x1, route_t, mod, wg, wu, wd)


def _rope_tables():
    pos = np.arange(DEC_SEQ)
    half = HD_A // 4
    freqs = ROPE_BASE ** (-np.arange(half, dtype=np.float64) / half)
    ang_r = (pos // GRID_W)[:, None] * freqs[None, :]
    ang_c = (pos % GRID_W)[:, None] * freqs[None, :]
    ang = np.concatenate([ang_r, ang_r, ang_c, ang_c], axis=1)
    sign = np.concatenate([-np.ones(half), np.ones(half)] * 2)[None, :]
    cos = np.tile(np.cos(ang), (1, 2))
    sin = np.tile(np.sin(ang) * sign, (1, 2))
    ident_c = np.ones((TM_PROJ, LANES))
    ident_s = np.zeros((TM_PROJ, LANES))
    return (jnp.asarray(np.concatenate([cos, ident_c]), f32),
            jnp.asarray(np.concatenate([sin, ident_s]), f32))


def kernel(x_prompt, x_sample, c, cache_k, cache_v, state_ret_fwd, state_ret_bwd, c_ctx,
           norm1_w, norm2_w, w_ada, b_ada, w_in, q_norm_w, k_norm_w, attn_sink,
           ret_decay_fwd, ret_decay_bwd, ret_gn_w, w_pa, w_pb, w_o, w_router, b_router,
           w_exp_gate, w_exp_up, w_exp_down):
    x = jnp.concatenate([x_prompt.reshape(T_CTX, D), x_sample.reshape(T_LAT, D)], axis=0)
    cond8 = jnp.zeros((N_COND, D), f32).at[0].set(c_ctx).at[1:1 + DEC_BATCH].set(c)
    mod = _adaln(cond8, w_ada, b_ada).reshape(DEPTH, N_COND, 6, D)

    cos_t, sin_t = _rope_tables()
    blk = np.arange(Q_A) // HD_A
    ones_blk = jnp.asarray((blk[:, None] == blk[None, :]) / HD_A, bf16)
    tri = jnp.asarray(np.triu(np.ones((TM_POST, TM_POST)), 1), bf16)
    qnw = jnp.tile(q_norm_w, (1, NH_A))
    knw = jnp.tile(k_norm_w, (1, NKV_A))
    wr_pad = jnp.pad(w_router, ((0, 0), (0, LANES - N_EXP)))
    br_col = jnp.broadcast_to(b_router[:, None], (N_EXP, LANES))
    ck = cache_k.reshape(DEC_BATCH, DEPTH, PAST, KV_A)
    cv = cache_v.reshape(DEC_BATCH, DEPTH, PAST, KV_A)
    s0 = jnp.concatenate([state_ret_fwd, state_ret_bwd], axis=3)
    w_in_bf = w_in.astype(bf16)
    wpa, wpb, wo = w_pa.astype(bf16), w_pb.astype(bf16), w_o.astype(bf16)
    wg, wu, wd = w_exp_gate.astype(bf16), w_exp_up.astype(bf16), w_exp_down.astype(bf16)

    new_k, new_v, new_sf, new_sb = [], [], [], []
    for l in range(DEPTH):
        lg_f = jnp.log1p(-jnp.exp(ret_decay_fwd[l].astype(f32)))
        lg_b = jnp.log1p(-jnp.exp(ret_decay_bwd[l].astype(f32)))
        dec = _decay_tables(lg_f, lg_b)

        qa, ka, va, qr, kr, vr, gr, ga, gb = _inproj(l, x, norm1_w, mod, w_in_bf, cos_t, sin_t,
                                                     ones_blk, qnw, knw)
        oa_ctx = _ctx_attn(l, qa, ka, va, attn_sink)
        oa_lat = _lat_attn(l, qa, ka, va, ck, cv, cos_t, sin_t, attn_sink)
        yr_ctx, sf, sb = _retention(l, qr, kr, vr, gr, None, dec, ret_gn_w, BATCH, SEQ, 0)
        yr_lat, _, _ = _retention(l, qr, kr, vr, gr, s0[:, l], dec, ret_gn_w, DEC_BATCH, DEC_SEQ,
                                  T_CTX // DEC_SEQ)
        oa = jnp.concatenate([oa_ctx, oa_lat], axis=0)
        yr = jnp.concatenate([yr_ctx, yr_lat], axis=0)
        x1, h2, route, _ = _post(l, x, oa, yr, ga, gb, wpa, wpb, wo, mod, norm2_w, wr_pad, br_col, tri)
        x = _moe_dense(l, h2, x1, route.T, mod, wg, wu, wd)

        new_k.append(ka[:T_CTX].reshape(BATCH, SEQ, NKV_A, HD_A))
        new_v.append(va[:T_CTX].reshape(BATCH, SEQ, NKV_A, HD_A))
        new_sf.append(sf)
        new_sb.append(sb)

    return (x[:T_CTX].reshape(BATCH, SEQ, D), x[T_CTX:].reshape(DEC_BATCH, DEC_SEQ, D),
            jnp.stack(new_k, axis=1), jnp.stack(new_v, axis=1),
            jnp.stack(new_sf, axis=1), jnp.stack(new_sb, axis=1))
```

```python
import functools

import numpy as np
import jax
import jax.numpy as jnp
from jax import lax
from jax.experimental import pallas as pl
from jax.experimental.pallas import tpu as pltpu

D = 1024
BATCH, SEQ = 16, 256
DEC_BATCH, DEC_SEQ = 2, 2048
DEPTH = 2
PAST = 512
GRID_W = 64
NH_A, NKV_A, HD_A = 8, 2, 64
WINDOW = 128
NH_R, DK_R, DV_R = 4, 64, 128
CHUNK = 128
N_EXP, N_GROUPS, EXP_PER_GROUP = 16, 4, 4
D_EXP = 512
ROPE_BASE = 10000.0
EPS = 1e-6
NEG_INF = -1e30

Q_A = NH_A * HD_A
KV_A = NKV_A * HD_A
QK_R = NH_R * DK_R
V_R = NH_R * DV_R
C_QA = (0, Q_A)
C_KA = (C_QA[1], C_QA[1] + KV_A)
C_VA = (C_KA[1], C_KA[1] + KV_A)
C_QR = (C_VA[1], C_VA[1] + QK_R)
C_KR = (C_QR[1], C_QR[1] + QK_R)
C_VR = (C_KR[1], C_KR[1] + V_R)
C_GR = (C_VR[1], C_VR[1] + V_R)
C_GA = (C_GR[1], C_GR[1] + D)
C_GB = (C_GA[1], C_GA[1] + D)
D_IN = C_GB[1]

T_CTX = BATCH * SEQ
T_LAT = DEC_BATCH * DEC_SEQ
T = T_CTX + T_LAT
N_COND = 8

LANES = 128
V7X_VMEM_LIMIT = 56 * 1024 * 1024

TM_PROJ = 512
TM_POST = 256
TM_DISP = 256
MOE_TR = 256
MOE_TILES = 2 * T // MOE_TR + N_EXP
MOE_ROWS = MOE_TILES * MOE_TR
ATT_QB = 256

f32 = jnp.float32
bf16 = jnp.bfloat16


def _dot(a, b):
    return jnp.dot(a, b, preferred_element_type=f32)


def _dot_t(a, b):
    return lax.dot_general(a, b, (((1,), (1,)), ((), ())), preferred_element_type=f32)


def _split(x):
    hi = x.astype(bf16)
    lo = (x - hi.astype(f32)).astype(bf16)
    return hi, lo


def _dot3(a, b):
    ah, al = _split(a)
    bh, bl = _split(b)
    return _dot(ah, bh) + (_dot(ah, bl) + _dot(al, bh))


def _sigmoid(x):
    return 1.0 / (1.0 + jnp.exp(-x))


def _silu(x):
    return x * _sigmoid(x)


def _cond_of_tile(i, tm):
    n_ctx = T_CTX // tm
    per_b = DEC_SEQ // tm
    return jnp.where(i < n_ctx, 0, 1 + jnp.maximum(i - n_ctx, 0) // per_b)


def _adaln_kernel(cond_ref, w_ref, b_ref, o_ref):
    a = _silu(cond_ref[...])
    o_ref[0] = _dot3(a, w_ref[0]) + b_ref[0]


def _adaln(cond8, w_ada, b_ada):
    tn = 1536
    return pl.pallas_call(
        _adaln_kernel,
        grid=(DEPTH, 6 * D // tn),
        in_specs=[
            pl.BlockSpec((N_COND, D), lambda l, j: (0, 0)),
            pl.BlockSpec((1, D, tn), lambda l, j: (l, 0, j)),
            pl.BlockSpec((1, 1, tn), lambda l, j: (l, 0, j)),
        ],
        out_specs=pl.BlockSpec((1, N_COND, tn), lambda l, j: (l, 0, j)),
        out_shape=jax.ShapeDtypeStruct((DEPTH, N_COND, 6 * D), f32),
        compiler_params=pltpu.CompilerParams(
            dimension_semantics=("parallel", "parallel"), vmem_limit_bytes=V7X_VMEM_LIMIT),
        name="adaln",
    )(cond8, w_ada, b_ada.reshape(DEPTH, 1, 6 * D))


def _rope(x, cos, sin_signed, first_half):
    fwd = pltpu.roll(x, 16, 1)
    bwd = pltpu.roll(x, LANES - 16, 1)
    partner = jnp.where(first_half, bwd, fwd)
    return x * cos + partner * sin_signed


def _head_rms(x, ones_blk, w):
    n = x.shape[1]
    sq_hi, sq_lo = _split(x * x)
    blk = ones_blk[0:n, 0:n]
    mean = _dot(sq_hi, blk) + _dot(sq_lo, blk)
    return x * lax.rsqrt(mean + EPS) * w


def _inproj_kernel(x_ref, n1_ref, mod_ref, w_ref, cos_ref, sin_ref, ones_ref, qnw_ref, knw_ref,
                   qa_ref, ka_ref, va_ref, qr_ref, kr_ref, vr_ref, gr_ref, ga_ref, gb_ref):
    x = x_ref[...]
    y = x * lax.rsqrt(jnp.mean(x * x, axis=-1, keepdims=True) + EPS) * n1_ref[...]
    h = (y * (1.0 + mod_ref[1:2, :]) + mod_ref[0:1, :]).astype(bf16)

    def proj(c):
        return _dot(h, w_ref[:, c[0]:c[1]])

    cos = cos_ref[...]
    sin = sin_ref[...]
    lane = lax.broadcasted_iota(jnp.int32, cos.shape, 1)
    first_half = (lane % 32) < 16

    def rope_all(v):
        parts = [_rope(v[:, j:j + LANES], cos, sin, first_half) for j in range(0, v.shape[1], LANES)]
        return parts[0] if len(parts) == 1 else jnp.concatenate(parts, axis=1)

    ones_blk = ones_ref[...]
    qa = _head_rms(proj(C_QA), ones_blk, qnw_ref[...])
    qa_ref[...] = rope_all(qa).astype(bf16)
    ka_ref[...] = _head_rms(proj(C_KA), ones_blk, knw_ref[...])
    va_ref[...] = proj(C_VA)
    qr_ref[...] = rope_all(proj(C_QR)).astype(bf16)
    kr_ref[...] = (rope_all(proj(C_KR)) * (DK_R ** -0.5)).astype(bf16)
    vr_ref[...] = proj(C_VR).astype(bf16)
    gr_ref[...] = proj(C_GR).astype(bf16)
    ga_ref[...] = proj(C_GA).astype(bf16)
    gb_ref[...] = proj(C_GB).astype(bf16)


def _inproj(l, x, norm1_w, mod, w_in_bf, cos_t, sin_t, ones_blk, qnw, knw):
    tm = TM_PROJ
    n_ctx = T_CTX // tm
    per_b = DEC_SEQ // tm

    def tab_map(i):
        return (jnp.where(i < n_ctx, per_b, jnp.maximum(i - n_ctx, 0) % per_b), 0)

    row = lambda i: (i, 0)
    const = lambda i: (0, 0)
    widths = [(Q_A, bf16), (KV_A, f32), (KV_A, f32), (QK_R, bf16), (QK_R, bf16), (V_R, bf16),
              (V_R, bf16), (D, bf16), (D, bf16)]
    return pl.pallas_call(
        _inproj_kernel,
        grid=(T // tm,),
        in_specs=[
            pl.BlockSpec((tm, D), row),
            pl.BlockSpec((1, D), const),
            pl.BlockSpec((None, None, 6, D), lambda i: (l, _cond_of_tile(i, tm), 0, 0)),
            pl.BlockSpec((None, D, D_IN), lambda i: (l, 0, 0)),
            pl.BlockSpec((tm, LANES), tab_map),
            pl.BlockSpec((tm, LANES), tab_map),
            pl.BlockSpec((Q_A, Q_A), const),
            pl.BlockSpec((1, Q_A), const),
            pl.BlockSpec((1, KV_A), const),
        ],
        out_specs=[pl.BlockSpec((tm, w), row) for w, _ in widths],
        out_shape=[jax.ShapeDtypeStruct((T, w), dt) for w, dt in widths],
        compiler_params=pltpu.CompilerParams(
            dimension_semantics=("parallel",), vmem_limit_bytes=V7X_VMEM_LIMIT),
        name="inproj",
    )(x, norm1_w[l:l + 1], mod, w_in_bf, cos_t, sin_t, ones_blk, qnw[l:l + 1], knw[l:l + 1])


def _head_blocks(t, kv, lo_mask):
    r = pltpu.roll(t, HD_A, 1)
    if kv == 0:
        a = jnp.where(lo_mask, t, 0.0)
        b = jnp.where(lo_mask, 0.0, r)
    else:
        a = jnp.where(lo_mask, r, 0.0)
        b = jnp.where(lo_mask, 0.0, t)
    return jnp.concatenate([a, b], axis=0).astype(bf16)


def _ctx_attn_kernel(sink_ref, q_ref, k_ref, v_ref, o_ref):
    k = k_ref[...]
    v = v_ref[...]
    n = k.shape[0]
    lo_mask = lax.broadcasted_iota(jnp.int32, k.shape, 1) < HD_A
    scale = HD_A ** -0.5
    for kv in range(NKV_A):
        kblk = _head_blocks(k, kv, lo_mask)
        vblk = _head_blocks(v, kv, lo_mask)
        for pr in range(2):
            pi = kv * 2 + pr
            s = _dot_t(q_ref[:, pi * LANES:(pi + 1) * LANES], kblk) * scale
            ps = []
            for hh in range(2):
                sk = sink_ref[2 * pi + hh]
                sh = s[:, hh * n:(hh + 1) * n]
                m = jnp.maximum(jnp.max(sh, axis=-1, keepdims=True), sk)
                p = jnp.exp(sh - m)
                den = jnp.sum(p, axis=-1, keepdims=True) + jnp.exp(sk - m)
                ps.append((p / den).astype(bf16))
            o = _dot(jnp.concatenate(ps, axis=1), vblk)
            o_ref[:, pi * LANES:(pi + 1) * LANES] = o.astype(bf16)


def _ctx_attn(l, qa, ka, va, sink):
    blk = lambda w: pl.BlockSpec((SEQ, w), lambda b: (b, 0))
    return pl.pallas_call(
        _ctx_attn_kernel,
        grid=(BATCH,),
        in_specs=[pl.BlockSpec(memory_space=pltpu.SMEM), blk(Q_A), blk(KV_A), blk(KV_A)],
        out_specs=blk(Q_A),
        out_shape=jax.ShapeDtypeStruct((T_CTX, Q_A), bf16),
        compiler_params=pltpu.CompilerParams(
            dimension_semantics=("parallel",), vmem_limit_bytes=V7X_VMEM_LIMIT),
        name="ctx_attn",
    )(sink[l], qa, ka, va)


def _lat_attn_kernel(sink_ref, q_ref, k_ref, v_ref, kc_ref, vc_ref, cos_ref, sin_ref, o_ref):
    j = pl.program_id(1)
    qb = ATT_QB
    win = 2 * qb
    ws = pl.multiple_of(jnp.clip(j * qb - WINDOW, 0, DEC_SEQ - win), WINDOW)
    lo_mask = lax.broadcasted_iota(jnp.int32, (win, LANES), 1) < HD_A
    lane = lax.broadcasted_iota(jnp.int32, (win, LANES), 1)
    kw = _rope(k_ref[pl.ds(ws, win), :], cos_ref[pl.ds(ws, win), :], sin_ref[pl.ds(ws, win), :],
               (lane % 32) < 16)
    vw = v_ref[pl.ds(ws, win), :]
    kc = kc_ref[...]
    vc = vc_ref[...]
    qpos = j * qb + (lax.broadcasted_iota(jnp.int32, (2 * qb, win), 0) & (qb - 1))
    kpos = ws + lax.broadcasted_iota(jnp.int32, (2 * qb, win), 1)
    valid = jnp.abs(qpos - kpos) <= WINDOW
    scale = HD_A ** -0.5
    for kv in range(NKV_A):
        kc_blk = _head_blocks(kc, kv, lo_mask[:PAST])
        vc_blk = _head_blocks(vc, kv, lo_mask[:PAST])
        kw_blk = _head_blocks(kw, kv, lo_mask)
        vw_blk = _head_blocks(vw, kv, lo_mask)
        q2 = jnp.concatenate([q_ref[:, (2 * kv) * LANES:(2 * kv + 1) * LANES],
                              q_ref[:, (2 * kv + 1) * LANES:(2 * kv + 2) * LANES]], axis=0)
        s_c = _dot_t(q2, kc_blk) * scale
        s_w = _dot_t(q2, kw_blk) * scale
        pcs, pws = [], []
        for hh in range(2):
            row = lax.broadcasted_iota(jnp.int32, (2 * qb, 1), 0)
            sk = jnp.where(row < qb, sink_ref[4 * kv + hh], sink_ref[4 * kv + 2 + hh])
            sc = s_c[:, hh * PAST:(hh + 1) * PAST]
            sw = jnp.where(valid, s_w[:, hh * win:(hh + 1) * win], NEG_INF)
            m = jnp.maximum(jnp.maximum(jnp.max(sc, axis=-1, keepdims=True),
                                        jnp.max(sw, axis=-1, keepdims=True)), sk)
            pc = jnp.exp(sc - m)
            pw = jnp.exp(sw - m)
            den = (jnp.sum(pc, axis=-1, keepdims=True) + jnp.sum(pw, axis=-1, keepdims=True)
                   + jnp.exp(sk - m))
            inv = 1.0 / den
            pcs.append((pc * inv).astype(bf16))
            pws.append((pw * inv).astype(bf16))
        o = _dot(jnp.concatenate(pcs, axis=1), vc_blk) + _dot(jnp.concatenate(pws, axis=1), vw_blk)
        o_ref[:, (2 * kv) * LANES:(2 * kv + 1) * LANES] = o[:qb].astype(bf16)
        o_ref[:, (2 * kv + 1) * LANES:(2 * kv + 2) * LANES] = o[qb:].astype(bf16)


def _lat_attn(l, qa, ka, va, cache_k, cache_v, cos_l, sin_l, sink):
    qb = ATT_QB
    nq = DEC_SEQ // qb
    ctx_blocks = T_CTX // DEC_SEQ
    seq = lambda b, j: (ctx_blocks + b, 0)
    return pl.pallas_call(
        _lat_attn_kernel,
        grid=(DEC_BATCH, nq),
        in_specs=[
            pl.BlockSpec(memory_space=pltpu.SMEM),
            pl.BlockSpec((qb, Q_A), lambda b, j: (T_CTX // qb + b * nq + j, 0)),
            pl.BlockSpec((DEC_SEQ, KV_A), seq),
            pl.BlockSpec((DEC_SEQ, KV_A), seq),
            pl.BlockSpec((None, None, PAST, KV_A), lambda b, j: (b, l, 0, 0)),
            pl.BlockSpec((None, None, PAST, KV_A), lambda b, j: (b, l, 0, 0)),
            pl.BlockSpec((DEC_SEQ, LANES), lambda b, j: (0, 0)),
            pl.BlockSpec((DEC_SEQ, LANES), lambda b, j: (0, 0)),
        ],
        out_specs=pl.BlockSpec((qb, Q_A), lambda b, j: (b * nq + j, 0)),
        out_shape=jax.ShapeDtypeStruct((T_LAT, Q_A), bf16),
        compiler_params=pltpu.CompilerParams(
            dimension_semantics=("parallel", "parallel"), vmem_limit_bytes=V7X_VMEM_LIMIT),
        name="lat_attn",
    )(sink[l], qa, ka, va, cache_k, cache_v, cos_l, sin_l)


def _dup_heads(pair, lo_mask):
    r = pltpu.roll(pair, DK_R, 1)
    return jnp.where(lo_mask, pair, r), jnp.where(lo_mask, r, pair)


def _retention_kernel(has_s0, n_chunks, *refs):
    if has_s0:
        (q_ref, k_ref, v_ref, g_ref, s0_ref, mask_ref, qdec_ref, kdec_ref, cdec_ref, gnw_ref,
         y_ref, sf_ref, sb_ref, ds_ref, st_ref) = refs
    else:
        (q_ref, k_ref, v_ref, g_ref, mask_ref, qdec_ref, kdec_ref, cdec_ref, gnw_ref,
         y_ref, sf_ref, sb_ref, ds_ref, st_ref) = refs
        s0_ref = None
    C = CHUNK
    lo_mask = lax.broadcasted_iota(jnp.int32, (C, LANES), 1) < DK_R

    def inc_body(c, carry):
        r0 = pl.multiple_of(c * C, C)
        for pr in range(2):
            kp = k_ref[pl.ds(r0, C), pr * LANES:(pr + 1) * LANES].astype(f32)
            for hh, kd in enumerate(_dup_heads(kp, lo_mask)):
                h = 2 * pr + hh
                kd = (kd * kdec_ref[:, h * LANES:(h + 1) * LANES]).astype(bf16)
                vh = v_ref[pl.ds(r0, C), h * DV_R:(h + 1) * DV_R]
                ds_ref[c, h] = lax.dot_general(kd, vh, (((0,), (0,)), ((), ())),
                                               preferred_element_type=f32)
        return carry

    lax.fori_loop(0, n_chunks, inc_body, 0)

    for h in range(NH_R):
        cf = cdec_ref[h, 0:DK_R, :]
        cb = cdec_ref[h, DK_R:2 * DK_R, :]
        if has_s0:
            init_f = s0_ref[h, 0:DK_R, :]
            init_b = s0_ref[h, DK_R:2 * DK_R, :]
        else:
            init_f = jnp.zeros((DK_R, DV_R), f32)
            init_b = init_f

        def fwd_body(c, s, h=h, cf=cf):
            st_ref[c, h, 0:DK_R, :] = s
            return s * cf + ds_ref[c, h, 0:DK_R, :]

        def bwd_body(i, s, h=h, cb=cb):
            c = n_chunks - 1 - i
            st_ref[c, h, DK_R:2 * DK_R, :] = s
            return s * cb + ds_ref[c, h, DK_R:2 * DK_R, :]

        sf_ref[h] = lax.fori_loop(0, n_chunks, fwd_body, init_f)
        sb_ref[h] = lax.fori_loop(0, n_chunks, bwd_body, init_b)

    def out_body(c, carry):
        r0 = pl.multiple_of(c * C, C)
        for pr in range(2):
            qp = q_ref[pl.ds(r0, C), pr * LANES:(pr + 1) * LANES]
            kp = k_ref[pl.ds(r0, C), pr * LANES:(pr + 1) * LANES].astype(f32)
            kblk = jnp.concatenate([jnp.where(lo_mask, kp, 0.0), jnp.where(lo_mask, 0.0, kp)],
                                   axis=0).astype(bf16)
            a2 = _dot_t(qp, kblk)
            for hh, qd in enumerate(_dup_heads(qp.astype(f32), lo_mask)):
                h = 2 * pr + hh
                a = (a2[:, hh * C:(hh + 1) * C] * mask_ref[h]).astype(bf16)
                vh = v_ref[pl.ds(r0, C), h * DV_R:(h + 1) * DV_R]
                qd = (qd * qdec_ref[:, h * LANES:(h + 1) * LANES]).astype(bf16)
                o = _dot(a, vh) + _dot(qd, st_ref[c, h].astype(bf16))
                mu = jnp.mean(o, axis=-1, keepdims=True)
                d = o - mu
                var = jnp.mean(d * d, axis=-1, keepdims=True)
                yh = d * lax.rsqrt(var + EPS) * gnw_ref[:, h * DV_R:(h + 1) * DV_R]
                g = g_ref[pl.ds(r0, C), h * DV_R:(h + 1) * DV_R].astype(f32)
                y_ref[pl.ds(r0, C), h * DV_R:(h + 1) * DV_R] = (yh * _silu(g)).astype(bf16)
        return carry

    lax.fori_loop(0, n_chunks, out_body, 0)


def _retention(l, qr, kr, vr, gr, s0, dec, gnw, nb, seq, row_block0):
    n_chunks = seq // CHUNK
    mask, qdec, kdec, cdec = dec
    has_s0 = s0 is not None
    tok = lambda w: pl.BlockSpec((seq, w), lambda b: (row_block0 + b, 0))
    full = lambda a: pl.BlockSpec(a.shape, lambda b: (0,) * a.ndim)
    in_specs = [tok(QK_R), tok(QK_R), tok(V_R), tok(V_R)]
    args = [qr, kr, vr, gr]
    if has_s0:
        in_specs.append(pl.BlockSpec((None, NH_R, 2 * DK_R, DV_R), lambda b: (b, 0, 0, 0)))
        args.append(s0)
    in_specs += [full(mask), full(qdec), full(kdec), full(cdec),
                 pl.BlockSpec((1, V_R), lambda b: (0, 0))]
    args += [mask, qdec, kdec, cdec, gnw[l:l + 1]]
    st_spec = pl.BlockSpec((None, NH_R, DK_R, DV_R), lambda b: (b, 0, 0, 0))
    return pl.pallas_call(
        functools.partial(_retention_kernel, has_s0, n_chunks),
        grid=(nb,),
        in_specs=in_specs,
        out_specs=[pl.BlockSpec((seq, V_R), lambda b: (b, 0)), st_spec, st_spec],
        out_shape=[jax.ShapeDtypeStruct((nb * seq, V_R), bf16),
                   jax.ShapeDtypeStruct((nb, NH_R, DK_R, DV_R), f32),
                   jax.ShapeDtypeStruct((nb, NH_R, DK_R, DV_R), f32)],
        scratch_shapes=[pltpu.VMEM((n_chunks, NH_R, 2 * DK_R, DV_R), f32),
                        pltpu.VMEM((n_chunks, NH_R, 2 * DK_R, DV_R), f32)],
        compiler_params=pltpu.CompilerParams(
            dimension_semantics=("parallel",), vmem_limit_bytes=V7X_VMEM_LIMIT),
        name="retention_lat" if has_s0 else "retention_ctx",
    )(*args)


def _decay_tables(lg_f, lg_b):
    C = CHUNK
    idx = jnp.arange(C, dtype=f32)
    diff = idx[:, None] - idx[None, :]
    lower = jnp.where(diff >= 0, jnp.exp(jnp.maximum(diff, 0.0)[None] * lg_f[:, None, None]), 0.0)
    upper = jnp.where(diff <= 0, jnp.exp(jnp.maximum(-diff, 0.0)[None] * lg_b[:, None, None]), 0.0)
    mask = lower + upper
    qf = jnp.exp((idx + 1.0)[:, None] * lg_f[None, :])
    qb = jnp.exp((C - idx)[:, None] * lg_b[None, :])
    kf = jnp.exp((C - 1.0 - idx)[:, None] * lg_f[None, :])
    kb = jnp.exp(idx[:, None] * lg_b[None, :])

    def lanes(f, b):
        both = jnp.stack([f, b], axis=-1)
        return jnp.repeat(both, DK_R, axis=-1).reshape(C, NH_R * 2 * DK_R)

    cf = jnp.exp(C * lg_f)
    cb = jnp.exp(C * lg_b)
    cdec = jnp.broadcast_to(jnp.stack([cf, cb], axis=-1)[:, :, None, None],
                            (NH_R, 2, DK_R, DV_R)).reshape(NH_R, 2 * DK_R, DV_R)
    return mask, lanes(qf, qb), lanes(kf, kb), cdec


def _post_kernel(x_ref, oa_ref, yr_ref, ga_ref, gb_ref, wpa_ref, wpb_ref, wo_ref, mod_ref, n2_ref,
                 wr_ref, br_ref, tri_ref, x1_ref, h2_ref, route_ref, cnt_ref, carry_ref):
    i = pl.program_id(0)
    tm = x_ref.shape[0]

    @pl.when(i == 0)
    def _():
        carry_ref[...] = jnp.zeros_like(carry_ref)

    ga = _sigmoid(ga_ref[...].astype(f32))
    gb = _sigmoid(gb_ref[...].astype(f32))
    merged = ga * _dot(oa_ref[...], wpa_ref[...]) + gb * _dot(yr_ref[...], wpb_ref[...])
    mix = _dot(merged.astype(bf16), wo_ref[...])
    x1 = x_ref[...] + mod_ref[2:3, :] * mix
    x1_ref[...] = x1
    y = x1 * lax.rsqrt(jnp.mean(x1 * x1, axis=-1, keepdims=True) + EPS) * n2_ref[...]
    h2 = y * (1.0 + mod_ref[4:5, :]) + mod_ref[3:4, :]
    h2_ref[...] = h2

    logits = _dot3(h2, wr_ref[...])
    lt = logits.T[0:N_EXP, :]
    scores = _sigmoid(lt)
    sel = scores + br_ref[:, 0:1]
    row = lax.broadcasted_iota(jnp.int32, (N_EXP, tm), 0)

    best = None
    bg = None
    for g in range(N_GROUPS):
        a, b, c, d = (sel[EXP_PER_GROUP * g + k:EXP_PER_GROUP * g + k + 1, :] for k in range(4))
        p, q = jnp.maximum(a, b), jnp.minimum(a, b)
        r, s = jnp.maximum(c, d), jnp.minimum(c, d)
        gs = jnp.maximum(p, r) + jnp.maximum(jnp.minimum(p, r), jnp.maximum(q, s))
        if g == 0:
            best, bg = gs, jnp.zeros((1, tm), jnp.int32)
        else:
            upd = gs > best
            bg = jnp.where(upd, g, bg)
            best = jnp.where(upd, gs, best)
    masked = jnp.where(jnp.right_shift(row, 2) == bg, sel, NEG_INF)
    m1 = jnp.max(masked, axis=0, keepdims=True)
    i1 = jnp.min(jnp.where(masked == m1, row, N_EXP), axis=0, keepdims=True)
    masked2 = jnp.where(row == i1, NEG_INF, masked)
    m2 = jnp.max(masked2, axis=0, keepdims=True)
    i2 = jnp.min(jnp.where(masked2 == m2, row, N_EXP), axis=0, keepdims=True)
    oh1 = row == i1
    oh2 = row == i2
    s1 = jnp.sum(jnp.where(oh1, scores, 0.0), axis=0, keepdims=True)
    s2 = jnp.sum(jnp.where(oh2, scores, 0.0), axis=0, keepdims=True)
    den = s1 + s2

    oh = jnp.where(oh1 | oh2, 1.0, 0.0)
    tot = carry_ref[:, 0:1] + _dot(oh.astype(bf16), tri_ref[...])
    r1 = jnp.sum(jnp.where(oh1, tot, 0.0), axis=0, keepdims=True)
    r2 = jnp.sum(jnp.where(oh2, tot, 0.0), axis=0, keepdims=True)
    carry_ref[...] = carry_ref[...] + jnp.sum(oh, axis=1, keepdims=True)
    cnt_ref[...] = carry_ref[...]

    route_ref[0:1, :] = i1.astype(f32)
    route_ref[1:2, :] = i2.astype(f32)
    route_ref[2:3, :] = r1
    route_ref[3:4, :] = r2
    route_ref[4:5, :] = s1 / den
    route_ref[5:6, :] = s2 / den
    route_ref[6:8, :] = jnp.zeros((2, tm), f32)


def _post(l, x, oa, yr, ga, gb, wpa, wpb, wo, mod, norm2_w, wr_pad, br_col, tri):
    tm = TM_POST
    row = lambda w: pl.BlockSpec((tm, w), lambda i: (i, 0))
    lay = lambda a: pl.BlockSpec((None,) + a.shape[1:], lambda i: (l,) + (0,) * (a.ndim - 1))
    const = lambda a: pl.BlockSpec(a.shape, lambda i: (0,) * a.ndim)
    return pl.pallas_call(
        _post_kernel,
        grid=(T // tm,),
        in_specs=[row(D), row(Q_A), row(V_R), row(D), row(D), lay(wpa), lay(wpb), lay(wo),
                  pl.BlockSpec((None, None, 6, D), lambda i: (l, _cond_of_tile(i, tm), 0, 0)),
                  pl.BlockSpec((1, D), lambda i: (0, 0)), const(wr_pad), const(br_col), const(tri)],
        out_specs=[row(D), row(D), pl.BlockSpec((8, tm), lambda i: (0, i)),
                   pl.BlockSpec((N_EXP, LANES), lambda i: (0, 0))],
        out_shape=[jax.ShapeDtypeStruct((T, D), f32), jax.ShapeDtypeStruct((T, D), f32),
                   jax.ShapeDtypeStruct((8, T), f32), jax.ShapeDtypeStruct((N_EXP, LANES), f32)],
        scratch_shapes=[pltpu.VMEM((N_EXP, LANES), f32)],
        compiler_params=pltpu.CompilerParams(
            dimension_semantics=("arbitrary",), vmem_limit_bytes=V7X_VMEM_LIMIT),
        name="post_router",
    )(x, oa, yr, ga, gb, wpa, wpb, wo, mod, norm2_w[l:l + 1], wr_pad, br_col, tri)


def _row_copy(src_ref, src_row, dst_ref, dst_row, sem):
    return pltpu.make_async_copy(src_ref.at[pl.ds(src_row, 1)], dst_ref.at[pl.ds(dst_row, 1)], sem)


def _dispatch_kernel(zf_ref, pos_ref, h_ref, xs_ref, zero_ref, sem):
    tm = h_ref.shape[0]

    @pl.when(pl.program_id(0) == 0)
    def _():
        zero_ref[...] = jnp.zeros_like(zero_ref)

        def tile_copy(t):
            return pltpu.make_async_copy(zero_ref, xs_ref.at[pl.ds(t * MOE_TR, MOE_TR)], sem)

        def start(t, carry):
            @pl.when(zf_ref[t] != 0)
            def _():
                tile_copy(t).start()
            return carry

        def wait(t, carry):
            @pl.when(zf_ref[t] != 0)
            def _():
                tile_copy(t).wait()
            return carry

        lax.fori_loop(0, MOE_TILES, start, 0)
        lax.fori_loop(0, MOE_TILES, wait, 0)

    def issue(r, carry):
        _row_copy(h_ref, r, xs_ref, pos_ref[0, 0, 2 * r], sem).start()
        _row_copy(h_ref, r, xs_ref, pos_ref[0, 0, 2 * r + 1], sem).start()
        return carry

    lax.fori_loop(0, tm, issue, 0, unroll=8)
    for _ in range(2):
        pltpu.make_async_copy(h_ref, xs_ref.at[pl.ds(0, tm)], sem).wait()


def _dispatch(zero_flags, pos_blocks, h2):
    tm = TM_DISP
    return pl.pallas_call(
        _dispatch_kernel,
        grid_spec=pltpu.PrefetchScalarGridSpec(
            num_scalar_prefetch=1,
            grid=(T // tm,),
            in_specs=[pl.BlockSpec((1, 1, 2 * tm), lambda i, zf: (i, 0, 0), memory_space=pltpu.SMEM),
                      pl.BlockSpec((tm, D), lambda i, zf: (i, 0))],
            out_specs=pl.BlockSpec(memory_space=pl.ANY),
            scratch_shapes=[pltpu.VMEM((MOE_TR, D), f32), pltpu.SemaphoreType.DMA(())],
        ),
        out_shape=jax.ShapeDtypeStruct((MOE_ROWS, D), f32),
        compiler_params=pltpu.CompilerParams(
            dimension_semantics=("arbitrary",), vmem_limit_bytes=V7X_VMEM_LIMIT),
        name="moe_dispatch",
    )(zero_flags, pos_blocks, h2)


def _experts_kernel(te_ref, nt_ref, xs_ref, wg_ref, wu_ref, wd_ref, y_ref):
    i = pl.program_id(0)

    @pl.when(i < nt_ref[0])
    def _():
        x = xs_ref[...].astype(bf16)
        act = _silu(_dot(x, wg_ref[...])) * _dot(x, wu_ref[...])
        y_ref[...] = _dot(act.astype(bf16), wd_ref[...])

    @pl.when(i >= nt_ref[0])
    def _():
        y_ref[...] = jnp.zeros_like(y_ref)


def _experts(l, tile_expert, n_tiles, xs, wg, wu, wd):
    def tile(i, nt):
        return jnp.minimum(i, nt[0] - 1)

    wspec = lambda a, b: pl.BlockSpec((None, None, a, b),
                                      lambda i, te, nt: (l, te[tile(i, nt)], 0, 0))
    return pl.pallas_call(
        _experts_kernel,
        grid_spec=pltpu.PrefetchScalarGridSpec(
            num_scalar_prefetch=2,
            grid=(MOE_TILES,),
            in_specs=[pl.BlockSpec((MOE_TR, D), lambda i, te, nt: (tile(i, nt), 0)),
                      wspec(D, D_EXP), wspec(D, D_EXP), wspec(D_EXP, D)],
            out_specs=pl.BlockSpec((MOE_TR, D), lambda i, te, nt: (i, 0)),
        ),
        out_shape=jax.ShapeDtypeStruct((MOE_ROWS, D), f32),
        compiler_params=pltpu.CompilerParams(
            dimension_semantics=("arbitrary",), vmem_limit_bytes=V7X_VMEM_LIMIT),
        name="moe_experts",
    )(tile_expert, n_tiles, xs, wg, wu, wd)


def _combine_kernel(pos_ref, x1_ref, rt_ref, mod_ref, y_ref, o_ref, y0_ref, y1_ref, sem):
    tm = x1_ref.shape[0]

    def issue(r, carry):
        _row_copy(y_ref, pos_ref[0, 0, 2 * r], y0_ref, r, sem).start()
        _row_copy(y_ref, pos_ref[0, 0, 2 * r + 1], y1_ref, r, sem).start()
        return carry

    lax.fori_loop(0, tm, issue, 0, unroll=8)
    pltpu.make_async_copy(y_ref.at[pl.ds(0, tm)], y0_ref, sem).wait()
    pltpu.make_async_copy(y_ref.at[pl.ds(0, tm)], y1_ref, sem).wait()
    moe = rt_ref[:, 4:5] * y0_ref[...] + rt_ref[:, 5:6] * y1_ref[...]
    o_ref[...] = x1_ref[...] + mod_ref[5:6, :] * moe


def _combine(l, pos_blocks, x1, route_t, mod, y):
    tm = TM_DISP
    row = lambda w: pl.BlockSpec((tm, w), lambda i: (i, 0))
    return pl.pallas_call(
        _combine_kernel,
        grid=(T // tm,),
        in_specs=[pl.BlockSpec((1, 1, 2 * tm), lambda i: (i, 0, 0), memory_space=pltpu.SMEM),
                  row(D), row(8),
                  pl.BlockSpec((None, None, 6, D), lambda i: (l, _cond_of_tile(i, tm), 0, 0)),
                  pl.BlockSpec(memory_space=pl.ANY)],
        out_specs=row(D),
        out_shape=jax.ShapeDtypeStruct((T, D), f32),
        scratch_shapes=[pltpu.VMEM((tm, D), f32), pltpu.VMEM((tm, D), f32),
                        pltpu.SemaphoreType.DMA(())],
        compiler_params=pltpu.CompilerParams(
            dimension_semantics=("arbitrary",), vmem_limit_bytes=V7X_VMEM_LIMIT),
        name="moe_combine",
    )(pos_blocks, x1, route_t, mod, y)


def _moe_plan(route, counts):
    cnt = counts[:, 0].astype(jnp.int32)
    tiles = (cnt + MOE_TR - 1) // MOE_TR
    tile_end = jnp.cumsum(tiles)
    tile_start = tile_end - tiles
    off = (tile_start * MOE_TR).astype(f32)
    experts = jnp.arange(N_EXP, dtype=f32)[:, None]
    pos = [jnp.sum(jnp.where(route[k][None, :] == experts, off[:, None], 0.0), axis=0) + route[2 + k]
           for k in range(2)]
    pos_blocks = jnp.stack(pos, axis=-1).astype(jnp.int32).reshape(T // TM_DISP, 1, 2 * TM_DISP)
    tid = jnp.arange(MOE_TILES, dtype=jnp.int32)
    te = jnp.minimum(jnp.sum(tid[:, None] >= tile_end[None, :], axis=1), N_EXP - 1).astype(jnp.int32)
    n_tiles = tile_end[-1:].astype(jnp.int32)
    zero_flags = ((tid >= n_tiles[0]) | (tid == tile_end[te] - 1)).astype(jnp.int32)
    return pos_blocks, te, n_tiles, zero_flags


def _rope_tables():
    pos = np.arange(DEC_SEQ)
    half = HD_A // 4
    freqs = ROPE_BASE ** (-np.arange(half, dtype=np.float64) / half)
    ang_r = (pos // GRID_W)[:, None] * freqs[None, :]
    ang_c = (pos % GRID_W)[:, None] * freqs[None, :]
    ang = np.concatenate([ang_r, ang_r, ang_c, ang_c], axis=1)
    sign = np.concatenate([-np.ones(half), np.ones(half)] * 2)[None, :]
    cos = np.tile(np.cos(ang), (1, 2))
    sin = np.tile(np.sin(ang) * sign, (1, 2))
    ident_c = np.ones((TM_PROJ, LANES))
    ident_s = np.zeros((TM_PROJ, LANES))
    return (jnp.asarray(np.concatenate([cos, ident_c]), f32),
            jnp.asarray(np.concatenate([sin, ident_s]), f32))


def kernel(x_prompt, x_sample, c, cache_k, cache_v, state_ret_fwd, state_ret_bwd, c_ctx,
           norm1_w, norm2_w, w_ada, b_ada, w_in, q_norm_w, k_norm_w, attn_sink,
           ret_decay_fwd, ret_decay_bwd, ret_gn_w, w_pa, w_pb, w_o, w_router, b_router,
           w_exp_gate, w_exp_up, w_exp_down):
    x = jnp.concatenate([x_prompt.reshape(T_CTX, D), x_sample.reshape(T_LAT, D)], axis=0)
    cond8 = jnp.zeros((N_COND, D), f32).at[0].set(c_ctx).at[1:1 + DEC_BATCH].set(c)
    mod = _adaln(cond8, w_ada, b_ada).reshape(DEPTH, N_COND, 6, D)

    cos_t, sin_t = _rope_tables()
    blk = np.arange(Q_A) // HD_A
    ones_blk = jnp.asarray((blk[:, None] == blk[None, :]) / HD_A, bf16)
    tri = jnp.asarray(np.triu(np.ones((TM_POST, TM_POST)), 1), bf16)
    qnw = jnp.tile(q_norm_w, (1, NH_A))
    knw = jnp.tile(k_norm_w, (1, NKV_A))
    wr_pad = jnp.pad(w_router, ((0, 0), (0, LANES - N_EXP)))
    br_col = jnp.broadcast_to(b_router[:, None], (N_EXP, LANES))
    ck = cache_k.reshape(DEC_BATCH, DEPTH, PAST, KV_A)
    cv = cache_v.reshape(DEC_BATCH, DEPTH, PAST, KV_A)
    s0 = jnp.concatenate([state_ret_fwd, state_ret_bwd], axis=3)
    w_in_bf = w_in.astype(bf16)
    wpa, wpb, wo = w_pa.astype(bf16), w_pb.astype(bf16), w_o.astype(bf16)
    wg, wu, wd = w_exp_gate.astype(bf16), w_exp_up.astype(bf16), w_exp_down.astype(bf16)

    new_k, new_v, new_sf, new_sb = [], [], [], []
    for l in range(DEPTH):
        lg_f = jnp.log1p(-jnp.exp(ret_decay_fwd[l].astype(f32)))
        lg_b = jnp.log1p(-jnp.exp(ret_decay_bwd[l].astype(f32)))
        dec = _decay_tables(lg_f, lg_b)

        qa, ka, va, qr, kr, vr, gr, ga, gb = _inproj(l, x, norm1_w, mod, w_in_bf, cos_t, sin_t,
                                                     ones_blk, qnw, knw)
        oa_ctx = _ctx_attn(l, qa, ka, va, attn_sink)
        oa_lat = _lat_attn(l, qa, ka, va, ck, cv, cos_t, sin_t, attn_sink)
        yr_ctx, sf, sb = _retention(l, qr, kr, vr, gr, None, dec, ret_gn_w, BATCH, SEQ, 0)
        yr_lat, _, _ = _retention(l, qr, kr, vr, gr, s0[:, l], dec, ret_gn_w, DEC_BATCH, DEC_SEQ,
                                  T_CTX // DEC_SEQ)
        oa = jnp.concatenate([oa_ctx, oa_lat], axis=0)
        yr = jnp.concatenate([yr_ctx, yr_lat], axis=0)
        x1, h2, route, counts = _post(l, x, oa, yr, ga, gb, wpa, wpb, wo, mod, norm2_w, wr_pad,
                                      br_col, tri)
        pos_blocks, tile_expert, n_tiles, zero_flags = _moe_plan(route, counts)
        xs = _dispatch(zero_flags, pos_blocks, h2)
        y = _experts(l, tile_expert, n_tiles, xs, wg, wu, wd)
        x = _combine(l, pos_blocks, x1, route.T, mod, y)

        new_k.append(ka[:T_CTX].reshape(BATCH, SEQ, NKV_A, HD_A))
        new_v.append(va[:T_CTX].reshape(BATCH, SEQ, NKV_A, HD_A))
        new_sf.append(sf)
        new_sb.append(sb)

    return (x[:T_CTX].reshape(BATCH, SEQ, D), x[T_CTX:].reshape(DEC_BATCH, DEC_SEQ, D),
            jnp.stack(new_k, axis=1), jnp.stack(new_v, axis=1),
            jnp.stack(new_sf, axis=1), jnp.stack(new_sb, axis=1))
```

```python
import functools

import numpy as np
import jax
import jax.numpy as jnp
from jax import lax
from jax.experimental import pallas as pl
from jax.experimental.pallas import tpu as pltpu

D = 1024
BATCH, SEQ = 16, 256
DEC_BATCH, DEC_SEQ = 2, 2048
DEPTH = 2
PAST = 512
GRID_W = 64
NH_A, NKV_A, HD_A = 8, 2, 64
WINDOW = 128
NH_R, DK_R, DV_R = 4, 64, 128
CHUNK = 128
N_EXP, N_GROUPS, EXP_PER_GROUP = 16, 4, 4
D_EXP = 512
ROPE_BASE = 10000.0
EPS = 1e-6
NEG_INF = -1e30

Q_A = NH_A * HD_A
KV_A = NKV_A * HD_A
QK_R = NH_R * DK_R
V_R = NH_R * DV_R
C_QA = (0, Q_A)
C_KA = (C_QA[1], C_QA[1] + KV_A)
C_VA = (C_KA[1], C_KA[1] + KV_A)
C_QR = (C_VA[1], C_VA[1] + QK_R)
C_KR = (C_QR[1], C_QR[1] + QK_R)
C_VR = (C_KR[1], C_KR[1] + V_R)
C_GR = (C_VR[1], C_VR[1] + V_R)
C_GA = (C_GR[1], C_GR[1] + D)
C_GB = (C_GA[1], C_GA[1] + D)
D_IN = C_GB[1]

T_CTX = BATCH * SEQ
T_LAT = DEC_BATCH * DEC_SEQ
T = T_CTX + T_LAT
N_COND = 8

LANES = 128
V7X_VMEM_LIMIT = 56 * 1024 * 1024

TM_PROJ = 512
TM_POST = 256
W_CHUNK = 256
TM_DISP = 256
MOE_TR = 256
MOE_TILES = 2 * T // MOE_TR + N_EXP
MOE_ROWS = MOE_TILES * MOE_TR
ATT_QB = 256

f32 = jnp.float32
bf16 = jnp.bfloat16


def _dot(a, b):
    return jnp.dot(a, b, preferred_element_type=f32)


def _dot_t(a, b):
    return lax.dot_general(a, b, (((1,), (1,)), ((), ())), preferred_element_type=f32)


def _split(x):
    hi = x.astype(bf16)
    lo = (x - hi.astype(f32)).astype(bf16)
    return hi, lo


def _dot3(a, b):
    ah, al = _split(a)
    bh, bl = _split(b)
    return _dot(ah, bh) + (_dot(ah, bl) + _dot(al, bh))


def _sigmoid(x):
    return 1.0 / (1.0 + jnp.exp(-x))


def _silu(x):
    return x * _sigmoid(x)


def _cond_of_tile(i, tm):
    n_ctx = T_CTX // tm
    per_b = DEC_SEQ // tm
    return jnp.where(i < n_ctx, 0, 1 + jnp.maximum(i - n_ctx, 0) // per_b)


def _adaln_kernel(cond_ref, w_ref, b_ref, o_ref):
    a = _silu(cond_ref[...])
    o_ref[0] = _dot3(a, w_ref[0]) + b_ref[0]


def _adaln(cond8, w_ada, b_ada):
    tn = 1536
    return pl.pallas_call(
        _adaln_kernel,
        grid=(DEPTH, 6 * D // tn),
        in_specs=[
            pl.BlockSpec((N_COND, D), lambda l, j: (0, 0)),
            pl.BlockSpec((1, D, tn), lambda l, j: (l, 0, j)),
            pl.BlockSpec((1, 1, tn), lambda l, j: (l, 0, j)),
        ],
        out_specs=pl.BlockSpec((1, N_COND, tn), lambda l, j: (l, 0, j)),
        out_shape=jax.ShapeDtypeStruct((DEPTH, N_COND, 6 * D), f32),
        compiler_params=pltpu.CompilerParams(
            dimension_semantics=("parallel", "parallel"), vmem_limit_bytes=V7X_VMEM_LIMIT),
        name="adaln",
    )(cond8, w_ada, b_ada.reshape(DEPTH, 1, 6 * D))


def _rope(x, cos, sin_signed, first_half):
    fwd = pltpu.roll(x, 16, 1)
    bwd = pltpu.roll(x, LANES - 16, 1)
    partner = jnp.where(first_half, bwd, fwd)
    return x * cos + partner * sin_signed


def _head_rms(x, ones_blk, w):
    n = x.shape[1]
    sq_hi, sq_lo = _split(x * x)
    blk = ones_blk[0:n, 0:n]
    mean = _dot(sq_hi, blk) + _dot(sq_lo, blk)
    return x * lax.rsqrt(mean + EPS) * w


def _load_cast(w_hbm, dst_ref, stage_ref, sems):
    k, n = w_hbm.shape

    def chunk_copy(c):
        return pltpu.make_async_copy(w_hbm.at[:, pl.ds(c * W_CHUNK, W_CHUNK)],
                                     stage_ref.at[c % 2, pl.ds(0, k)], sems.at[c % 2])

    n_chunks = n // W_CHUNK
    chunk_copy(0).start()
    for c in range(n_chunks):
        if c + 1 < n_chunks:
            chunk_copy(c + 1).start()
        chunk_copy(c).wait()
        dst_ref[:, c * W_CHUNK:(c + 1) * W_CHUNK] = stage_ref[c % 2, 0:k, :].astype(bf16)


def _tile_of_two(i, n_first, a_ref, b_ref):
    return jnp.where(i < n_first, a_ref[...], b_ref[...])


def _inproj_kernel(l, xc_ref, xl_ref, n1_ref, mod_ref, w_hbm, cos_ref, sin_ref, ones_ref, qnw_ref,
                   knw_ref, qa_ref, ka_ref, va_ref, qr_ref, kr_ref, vr_ref, gr_ref, ga_ref, gb_ref,
                   w_ref, stage_ref, sems):
    i = pl.program_id(0)

    @pl.when(i == 0)
    def _():
        _load_cast(w_hbm.at[l], w_ref, stage_ref, sems)

    x = _tile_of_two(i, T_CTX // xc_ref.shape[0], xc_ref, xl_ref)
    y = x * lax.rsqrt(jnp.mean(x * x, axis=-1, keepdims=True) + EPS) * n1_ref[...]
    h = (y * (1.0 + mod_ref[1:2, :]) + mod_ref[0:1, :]).astype(bf16)

    def proj(c):
        return _dot(h, w_ref[:, c[0]:c[1]])

    cos = cos_ref[...]
    sin = sin_ref[...]
    lane = lax.broadcasted_iota(jnp.int32, cos.shape, 1)
    first_half = (lane % 32) < 16

    def rope_all(v):
        parts = [_rope(v[:, j:j + LANES], cos, sin, first_half) for j in range(0, v.shape[1], LANES)]
        return parts[0] if len(parts) == 1 else jnp.concatenate(parts, axis=1)

    ones_blk = ones_ref[...]
    qa = _head_rms(proj(C_QA), ones_blk, qnw_ref[...])
    qa_ref[...] = rope_all(qa).astype(bf16)
    ka_ref[...] = _head_rms(proj(C_KA), ones_blk, knw_ref[...])
    va_ref[...] = proj(C_VA)
    qr_ref[...] = rope_all(proj(C_QR)).astype(bf16)
    kr_ref[...] = (rope_all(proj(C_KR)) * (DK_R ** -0.5)).astype(bf16)
    vr_ref[...] = proj(C_VR).astype(bf16)
    gr_ref[...] = proj(C_GR).astype(bf16)
    ga_ref[...] = proj(C_GA).astype(bf16)
    gb_ref[...] = proj(C_GB).astype(bf16)


def _two_stream_specs(tm, width):
    n_ctx = T_CTX // tm
    return [pl.BlockSpec((tm, width), lambda i, *_: (jnp.minimum(i, n_ctx - 1), 0)),
            pl.BlockSpec((tm, width), lambda i, *_: (jnp.maximum(i - n_ctx, 0), 0))]


def _inproj(l, x_ctx, x_lat, norm1_w, mod, w_in, cos_t, sin_t, ones_blk, qnw, knw):
    tm = TM_PROJ
    n_ctx = T_CTX // tm
    per_b = DEC_SEQ // tm

    def tab_map(i):
        return (jnp.where(i < n_ctx, per_b, jnp.maximum(i - n_ctx, 0) % per_b), 0)

    row = lambda i: (i, 0)
    const = lambda i: (0, 0)
    widths = [(Q_A, bf16), (KV_A, f32), (KV_A, f32), (QK_R, bf16), (QK_R, bf16), (V_R, bf16),
              (V_R, bf16), (D, bf16), (D, bf16)]
    return pl.pallas_call(
        functools.partial(_inproj_kernel, l),
        grid=(T // tm,),
        in_specs=_two_stream_specs(tm, D) + [
            pl.BlockSpec((1, D), const),
            pl.BlockSpec((None, None, 6, D), lambda i: (l, _cond_of_tile(i, tm), 0, 0)),
            pl.BlockSpec(memory_space=pl.ANY),
            pl.BlockSpec((tm, LANES), tab_map),
            pl.BlockSpec((tm, LANES), tab_map),
            pl.BlockSpec((Q_A, Q_A), const),
            pl.BlockSpec((1, Q_A), const),
            pl.BlockSpec((1, KV_A), const),
        ],
        out_specs=[pl.BlockSpec((tm, w), row) for w, _ in widths],
        out_shape=[jax.ShapeDtypeStruct((T, w), dt) for w, dt in widths],
        scratch_shapes=[pltpu.VMEM((D, D_IN), bf16), pltpu.VMEM((2, D, W_CHUNK), f32),
                        pltpu.SemaphoreType.DMA((2,))],
        compiler_params=pltpu.CompilerParams(
            dimension_semantics=("arbitrary",), vmem_limit_bytes=V7X_VMEM_LIMIT),
        name="inproj",
    )(x_ctx, x_lat, norm1_w[l:l + 1], mod, w_in, cos_t, sin_t, ones_blk, qnw[l:l + 1], knw[l:l + 1])


def _head_blocks(t, kv, lo_mask):
    r = pltpu.roll(t, HD_A, 1)
    if kv == 0:
        a = jnp.where(lo_mask, t, 0.0)
        b = jnp.where(lo_mask, 0.0, r)
    else:
        a = jnp.where(lo_mask, r, 0.0)
        b = jnp.where(lo_mask, 0.0, t)
    return jnp.concatenate([a, b], axis=0).astype(bf16)


def _ctx_attn_kernel(sink_ref, q_ref, k_ref, v_ref, o_ref):
    k = k_ref[...]
    v = v_ref[...]
    n = k.shape[0]
    lo_mask = lax.broadcasted_iota(jnp.int32, k.shape, 1) < HD_A
    scale = HD_A ** -0.5
    for kv in range(NKV_A):
        kblk = _head_blocks(k, kv, lo_mask)
        vblk = _head_blocks(v, kv, lo_mask)
        for pr in range(2):
            pi = kv * 2 + pr
            s = _dot_t(q_ref[:, pi * LANES:(pi + 1) * LANES], kblk) * scale
            ps = []
            for hh in range(2):
                sk = sink_ref[2 * pi + hh]
                sh = s[:, hh * n:(hh + 1) * n]
                m = jnp.maximum(jnp.max(sh, axis=-1, keepdims=True), sk)
                p = jnp.exp(sh - m)
                den = jnp.sum(p, axis=-1, keepdims=True) + jnp.exp(sk - m)
                ps.append((p / den).astype(bf16))
            o = _dot(jnp.concatenate(ps, axis=1), vblk)
            o_ref[:, pi * LANES:(pi + 1) * LANES] = o.astype(bf16)


def _ctx_attn(l, qa, ka, va, sink):
    blk = lambda w: pl.BlockSpec((SEQ, w), lambda b: (b, 0))
    return pl.pallas_call(
        _ctx_attn_kernel,
        grid=(BATCH,),
        in_specs=[pl.BlockSpec(memory_space=pltpu.SMEM), blk(Q_A), blk(KV_A), blk(KV_A)],
        out_specs=blk(Q_A),
        out_shape=jax.ShapeDtypeStruct((T_CTX, Q_A), bf16),
        compiler_params=pltpu.CompilerParams(
            dimension_semantics=("parallel",), vmem_limit_bytes=V7X_VMEM_LIMIT),
        name="ctx_attn",
    )(sink[l], qa, ka, va)


def _lat_attn_kernel(sink_ref, q_ref, k_ref, v_ref, kc_ref, vc_ref, cos_ref, sin_ref, o_ref):
    j = pl.program_id(1)
    qb = ATT_QB
    win = 2 * qb
    ws = pl.multiple_of(jnp.clip(j * qb - WINDOW, 0, DEC_SEQ - win), WINDOW)
    lo_mask = lax.broadcasted_iota(jnp.int32, (win, LANES), 1) < HD_A
    lane = lax.broadcasted_iota(jnp.int32, (win, LANES), 1)
    kw = _rope(k_ref[pl.ds(ws, win), :], cos_ref[pl.ds(ws, win), :], sin_ref[pl.ds(ws, win), :],
               (lane % 32) < 16)
    vw = v_ref[pl.ds(ws, win), :]
    kc = kc_ref[...]
    vc = vc_ref[...]
    qpos = j * qb + (lax.broadcasted_iota(jnp.int32, (2 * qb, win), 0) & (qb - 1))
    kpos = ws + lax.broadcasted_iota(jnp.int32, (2 * qb, win), 1)
    valid = jnp.abs(qpos - kpos) <= WINDOW
    scale = HD_A ** -0.5
    for kv in range(NKV_A):
        kc_blk = _head_blocks(kc, kv, lo_mask[:PAST])
        vc_blk = _head_blocks(vc, kv, lo_mask[:PAST])
        kw_blk = _head_blocks(kw, kv, lo_mask)
        vw_blk = _head_blocks(vw, kv, lo_mask)
        q2 = jnp.concatenate([q_ref[:, (2 * kv) * LANES:(2 * kv + 1) * LANES],
                              q_ref[:, (2 * kv + 1) * LANES:(2 * kv + 2) * LANES]], axis=0)
        s_c = _dot_t(q2, kc_blk) * scale
        s_w = _dot_t(q2, kw_blk) * scale
        pcs, pws = [], []
        for hh in range(2):
            row = lax.broadcasted_iota(jnp.int32, (2 * qb, 1), 0)
            sk = jnp.where(row < qb, sink_ref[4 * kv + hh], sink_ref[4 * kv + 2 + hh])
            sc = s_c[:, hh * PAST:(hh + 1) * PAST]
            sw = jnp.where(valid, s_w[:, hh * win:(hh + 1) * win], NEG_INF)
            m = jnp.maximum(jnp.maximum(jnp.max(sc, axis=-1, keepdims=True),
                                        jnp.max(sw, axis=-1, keepdims=True)), sk)
            pc = jnp.exp(sc - m)
            pw = jnp.exp(sw - m)
            den = (jnp.sum(pc, axis=-1, keepdims=True) + jnp.sum(pw, axis=-1, keepdims=True)
                   + jnp.exp(sk - m))
            inv = 1.0 / den
            pcs.append((pc * inv).astype(bf16))
            pws.append((pw * inv).astype(bf16))
        o = _dot(jnp.concatenate(pcs, axis=1), vc_blk) + _dot(jnp.concatenate(pws, axis=1), vw_blk)
        o_ref[:, (2 * kv) * LANES:(2 * kv + 1) * LANES] = o[:qb].astype(bf16)
        o_ref[:, (2 * kv + 1) * LANES:(2 * kv + 2) * LANES] = o[qb:].astype(bf16)


def _lat_attn(l, qa, ka, va, cache_k, cache_v, cos_l, sin_l, sink):
    qb = ATT_QB
    nq = DEC_SEQ // qb
    ctx_blocks = T_CTX // DEC_SEQ
    seq = lambda b, j: (ctx_blocks + b, 0)
    return pl.pallas_call(
        _lat_attn_kernel,
        grid=(DEC_BATCH, nq),
        in_specs=[
            pl.BlockSpec(memory_space=pltpu.SMEM),
            pl.BlockSpec((qb, Q_A), lambda b, j: (T_CTX // qb + b * nq + j, 0)),
            pl.BlockSpec((DEC_SEQ, KV_A), seq),
            pl.BlockSpec((DEC_SEQ, KV_A), seq),
            pl.BlockSpec((None, None, PAST, KV_A), lambda b, j: (b, l, 0, 0)),
            pl.BlockSpec((None, None, PAST, KV_A), lambda b, j: (b, l, 0, 0)),
            pl.BlockSpec((DEC_SEQ, LANES), lambda b, j: (0, 0)),
            pl.BlockSpec((DEC_SEQ, LANES), lambda b, j: (0, 0)),
        ],
        out_specs=pl.BlockSpec((qb, Q_A), lambda b, j: (b * nq + j, 0)),
        out_shape=jax.ShapeDtypeStruct((T_LAT, Q_A), bf16),
        compiler_params=pltpu.CompilerParams(
            dimension_semantics=("parallel", "parallel"), vmem_limit_bytes=V7X_VMEM_LIMIT),
        name="lat_attn",
    )(sink[l], qa, ka, va, cache_k, cache_v, cos_l, sin_l)


def _dup_heads(pair, lo_mask):
    r = pltpu.roll(pair, DK_R, 1)
    return jnp.where(lo_mask, pair, r), jnp.where(lo_mask, r, pair)


def _retention_kernel(has_s0, n_chunks, *refs):
    if has_s0:
        (q_ref, k_ref, v_ref, g_ref, s0_ref, mask_ref, qdec_ref, kdec_ref, cdec_ref, gnw_ref,
         y_ref, sf_ref, sb_ref, ds_ref, st_ref) = refs
    else:
        (q_ref, k_ref, v_ref, g_ref, mask_ref, qdec_ref, kdec_ref, cdec_ref, gnw_ref,
         y_ref, sf_ref, sb_ref, ds_ref, st_ref) = refs
        s0_ref = None
    C = CHUNK
    lo_mask = lax.broadcasted_iota(jnp.int32, (C, LANES), 1) < DK_R

    def inc_body(c, carry):
        r0 = pl.multiple_of(c * C, C)
        for pr in range(2):
            kp = k_ref[pl.ds(r0, C), pr * LANES:(pr + 1) * LANES].astype(f32)
            for hh, kd in enumerate(_dup_heads(kp, lo_mask)):
                h = 2 * pr + hh
                kd = (kd * kdec_ref[:, h * LANES:(h + 1) * LANES]).astype(bf16)
                vh = v_ref[pl.ds(r0, C), h * DV_R:(h + 1) * DV_R]
                ds_ref[c, h] = lax.dot_general(kd, vh, (((0,), (0,)), ((), ())),
                                               preferred_element_type=f32)
        return carry

    lax.fori_loop(0, n_chunks, inc_body, 0)

    for h in range(NH_R):
        cf = cdec_ref[h, 0:DK_R, :]
        cb = cdec_ref[h, DK_R:2 * DK_R, :]
        if has_s0:
            init_f = s0_ref[h, 0:DK_R, :]
            init_b = s0_ref[h, DK_R:2 * DK_R, :]
        else:
            init_f = jnp.zeros((DK_R, DV_R), f32)
            init_b = init_f

        def fwd_body(c, s, h=h, cf=cf):
            st_ref[c, h, 0:DK_R, :] = s
            return s * cf + ds_ref[c, h, 0:DK_R, :]

        def bwd_body(i, s, h=h, cb=cb):
            c = n_chunks - 1 - i
            st_ref[c, h, DK_R:2 * DK_R, :] = s
            return s * cb + ds_ref[c, h, DK_R:2 * DK_R, :]

        sf_ref[h] = lax.fori_loop(0, n_chunks, fwd_body, init_f)
        sb_ref[h] = lax.fori_loop(0, n_chunks, bwd_body, init_b)

    def out_body(c, carry):
        r0 = pl.multiple_of(c * C, C)
        for pr in range(2):
            qp = q_ref[pl.ds(r0, C), pr * LANES:(pr + 1) * LANES]
            kp = k_ref[pl.ds(r0, C), pr * LANES:(pr + 1) * LANES].astype(f32)
            kblk = jnp.concatenate([jnp.where(lo_mask, kp, 0.0), jnp.where(lo_mask, 0.0, kp)],
                                   axis=0).astype(bf16)
            a2 = _dot_t(qp, kblk)
            for hh, qd in enumerate(_dup_heads(qp.astype(f32), lo_mask)):
                h = 2 * pr + hh
                a = (a2[:, hh * C:(hh + 1) * C] * mask_ref[h]).astype(bf16)
                vh = v_ref[pl.ds(r0, C), h * DV_R:(h + 1) * DV_R]
                qd = (qd * qdec_ref[:, h * LANES:(h + 1) * LANES]).astype(bf16)
                o = _dot(a, vh) + _dot(qd, st_ref[c, h].astype(bf16))
                mu = jnp.mean(o, axis=-1, keepdims=True)
                d = o - mu
                var = jnp.mean(d * d, axis=-1, keepdims=True)
                yh = d * lax.rsqrt(var + EPS) * gnw_ref[:, h * DV_R:(h + 1) * DV_R]
                g = g_ref[pl.ds(r0, C), h * DV_R:(h + 1) * DV_R].astype(f32)
                y_ref[pl.ds(r0, C), h * DV_R:(h + 1) * DV_R] = (yh * _silu(g)).astype(bf16)
        return carry

    lax.fori_loop(0, n_chunks, out_body, 0)


def _retention(l, qr, kr, vr, gr, s0, dec, gnw, nb, seq, row_block0):
    n_chunks = seq // CHUNK
    mask, qdec, kdec, cdec = dec
    has_s0 = s0 is not None
    tok = lambda w: pl.BlockSpec((seq, w), lambda b: (row_block0 + b, 0))
    full = lambda a: pl.BlockSpec(a.shape, lambda b: (0,) * a.ndim)
    in_specs = [tok(QK_R), tok(QK_R), tok(V_R), tok(V_R)]
    args = [qr, kr, vr, gr]
    if has_s0:
        in_specs.append(pl.BlockSpec((None, NH_R, 2 * DK_R, DV_R), lambda b: (b, 0, 0, 0)))
        args.append(s0)
    in_specs += [full(mask), full(qdec), full(kdec), full(cdec),
                 pl.BlockSpec((1, V_R), lambda b: (0, 0))]
    args += [mask, qdec, kdec, cdec, gnw[l:l + 1]]
    st_spec = pl.BlockSpec((None, NH_R, DK_R, DV_R), lambda b: (b, 0, 0, 0))
    return pl.pallas_call(
        functools.partial(_retention_kernel, has_s0, n_chunks),
        grid=(nb,),
        in_specs=in_specs,
        out_specs=[pl.BlockSpec((seq, V_R), lambda b: (b, 0)), st_spec, st_spec],
        out_shape=[jax.ShapeDtypeStruct((nb * seq, V_R), bf16),
                   jax.ShapeDtypeStruct((nb, NH_R, DK_R, DV_R), f32),
                   jax.ShapeDtypeStruct((nb, NH_R, DK_R, DV_R), f32)],
        scratch_shapes=[pltpu.VMEM((n_chunks, NH_R, 2 * DK_R, DV_R), f32),
                        pltpu.VMEM((n_chunks, NH_R, 2 * DK_R, DV_R), f32)],
        compiler_params=pltpu.CompilerParams(
            dimension_semantics=("parallel",), vmem_limit_bytes=V7X_VMEM_LIMIT),
        name="retention_lat" if has_s0 else "retention_ctx",
    )(*args)


def _decay_tables(lg_f, lg_b):
    C = CHUNK
    idx = jnp.arange(C, dtype=f32)
    diff = idx[:, None] - idx[None, :]
    lower = jnp.where(diff >= 0, jnp.exp(jnp.maximum(diff, 0.0)[None] * lg_f[:, None, None]), 0.0)
    upper = jnp.where(diff <= 0, jnp.exp(jnp.maximum(-diff, 0.0)[None] * lg_b[:, None, None]), 0.0)
    mask = lower + upper
    qf = jnp.exp((idx + 1.0)[:, None] * lg_f[None, :])
    qb = jnp.exp((C - idx)[:, None] * lg_b[None, :])
    kf = jnp.exp((C - 1.0 - idx)[:, None] * lg_f[None, :])
    kb = jnp.exp(idx[:, None] * lg_b[None, :])

    def lanes(f, b):
        both = jnp.stack([f, b], axis=-1)
        return jnp.repeat(both, DK_R, axis=-1).reshape(C, NH_R * 2 * DK_R)

    cf = jnp.exp(C * lg_f)
    cb = jnp.exp(C * lg_b)
    cdec = jnp.broadcast_to(jnp.stack([cf, cb], axis=-1)[:, :, None, None],
                            (NH_R, 2, DK_R, DV_R)).reshape(NH_R, 2 * DK_R, DV_R)
    return mask, lanes(qf, qb), lanes(kf, kb), cdec


def _post_kernel(l, xc_ref, xl_ref, oac_ref, oal_ref, yrc_ref, yrl_ref, ga_ref, gb_ref, wpa_hbm, wpb_hbm,
                 wo_hbm, mod_ref, n2_ref, wr_ref, br_ref, tri_ref, x1_ref, h2_ref, route_ref, cnt_ref,
                 carry_ref, wpa_ref, wpb_ref, wo_ref, stage_ref, sems):
    i = pl.program_id(0)
    tm = xc_ref.shape[0]
    n_ctx = T_CTX // tm

    @pl.when(i == 0)
    def _():
        carry_ref[...] = jnp.zeros_like(carry_ref)
        _load_cast(wpa_hbm.at[l], wpa_ref, stage_ref, sems)
        _load_cast(wpb_hbm.at[l], wpb_ref, stage_ref, sems)
        _load_cast(wo_hbm.at[l], wo_ref, stage_ref, sems)

    ga = _sigmoid(ga_ref[...].astype(f32))
    gb = _sigmoid(gb_ref[...].astype(f32))
    oa = _tile_of_two(i, n_ctx, oac_ref, oal_ref)
    yr = _tile_of_two(i, n_ctx, yrc_ref, yrl_ref)
    merged = ga * _dot(oa, wpa_ref[...]) + gb * _dot(yr, wpb_ref[...])
    mix = _dot(merged.astype(bf16), wo_ref[...])
    x1 = _tile_of_two(i, n_ctx, xc_ref, xl_ref) + mod_ref[2:3, :] * mix
    x1_ref[...] = x1
    y = x1 * lax.rsqrt(jnp.mean(x1 * x1, axis=-1, keepdims=True) + EPS) * n2_ref[...]
    h2 = y * (1.0 + mod_ref[4:5, :]) + mod_ref[3:4, :]
    h2_ref[...] = h2

    logits = _dot3(h2, wr_ref[...])
    lt = logits.T[0:N_EXP, :]
    scores = _sigmoid(lt)
    sel = scores + br_ref[:, 0:1]
    row = lax.broadcasted_iota(jnp.int32, (N_EXP, tm), 0)

    best = None
    bg = None
    for g in range(N_GROUPS):
        a, b, c, d = (sel[EXP_PER_GROUP * g + k:EXP_PER_GROUP * g + k + 1, :] for k in range(4))
        p, q = jnp.maximum(a, b), jnp.minimum(a, b)
        r, s = jnp.maximum(c, d), jnp.minimum(c, d)
        gs = jnp.maximum(p, r) + jnp.maximum(jnp.minimum(p, r), jnp.maximum(q, s))
        if g == 0:
            best, bg = gs, jnp.zeros((1, tm), jnp.int32)
        else:
            upd = gs > best
            bg = jnp.where(upd, g, bg)
            best = jnp.where(upd, gs, best)
    masked = jnp.where(jnp.right_shift(row, 2) == bg, sel, NEG_INF)
    m1 = jnp.max(masked, axis=0, keepdims=True)
    i1 = jnp.min(jnp.where(masked == m1, row, N_EXP), axis=0, keepdims=True)
    masked2 = jnp.where(row == i1, NEG_INF, masked)
    m2 = jnp.max(masked2, axis=0, keepdims=True)
    i2 = jnp.min(jnp.where(masked2 == m2, row, N_EXP), axis=0, keepdims=True)
    oh1 = row == i1
    oh2 = row == i2
    s1 = jnp.sum(jnp.where(oh1, scores, 0.0), axis=0, keepdims=True)
    s2 = jnp.sum(jnp.where(oh2, scores, 0.0), axis=0, keepdims=True)
    den = s1 + s2

    oh = jnp.where(oh1 | oh2, 1.0, 0.0)
    tot = carry_ref[:, 0:1] + _dot(oh.astype(bf16), tri_ref[...])
    r1 = jnp.sum(jnp.where(oh1, tot, 0.0), axis=0, keepdims=True)
    r2 = jnp.sum(jnp.where(oh2, tot, 0.0), axis=0, keepdims=True)
    carry_ref[...] = carry_ref[...] + jnp.sum(oh, axis=1, keepdims=True)
    cnt_ref[...] = carry_ref[...]

    route_ref[0:1, :] = i1.astype(f32)
    route_ref[1:2, :] = i2.astype(f32)
    route_ref[2:3, :] = r1
    route_ref[3:4, :] = r2
    route_ref[4:5, :] = s1 / den
    route_ref[5:6, :] = s2 / den
    route_ref[6:8, :] = jnp.zeros((2, tm), f32)


def _post(l, x_ctx, x_lat, oa_ctx, oa_lat, yr_ctx, yr_lat, ga, gb, wpa, wpb, wo, mod, norm2_w, wr_pad,
          br_col, tri):
    tm = TM_POST
    row = lambda w: pl.BlockSpec((tm, w), lambda i: (i, 0))
    const = lambda a: pl.BlockSpec(a.shape, lambda i: (0,) * a.ndim)
    hbm = pl.BlockSpec(memory_space=pl.ANY)
    return pl.pallas_call(
        functools.partial(_post_kernel, l),
        grid=(T // tm,),
        in_specs=(_two_stream_specs(tm, D) + _two_stream_specs(tm, Q_A) + _two_stream_specs(tm, V_R)
                  + [row(D), row(D), hbm, hbm, hbm,
                     pl.BlockSpec((None, None, 6, D), lambda i: (l, _cond_of_tile(i, tm), 0, 0)),
                     pl.BlockSpec((1, D), lambda i: (0, 0)), const(wr_pad), const(br_col), const(tri)]),
        out_specs=[row(D), row(D), pl.BlockSpec((8, tm), lambda i: (0, i)),
                   pl.BlockSpec((N_EXP, LANES), lambda i: (0, 0))],
        out_shape=[jax.ShapeDtypeStruct((T, D), f32), jax.ShapeDtypeStruct((T, D), f32),
                   jax.ShapeDtypeStruct((8, T), f32), jax.ShapeDtypeStruct((N_EXP, LANES), f32)],
        scratch_shapes=[pltpu.VMEM((N_EXP, LANES), f32), pltpu.VMEM((Q_A, D), bf16),
                        pltpu.VMEM((V_R, D), bf16), pltpu.VMEM((D, D), bf16),
                        pltpu.VMEM((2, D, W_CHUNK), f32), pltpu.SemaphoreType.DMA((2,))],
        compiler_params=pltpu.CompilerParams(
            dimension_semantics=("arbitrary",), vmem_limit_bytes=V7X_VMEM_LIMIT),
        name="post_router",
    )(x_ctx, x_lat, oa_ctx, oa_lat, yr_ctx, yr_lat, ga, gb, wpa, wpb, wo, mod, norm2_w[l:l + 1],
      wr_pad, br_col, tri)


def _row_copy(src_ref, src_row, dst_ref, dst_row, sem):
    return pltpu.make_async_copy(src_ref.at[pl.ds(src_row, 1)], dst_ref.at[pl.ds(dst_row, 1)], sem)


def _dispatch_kernel(zf_ref, pos_ref, h_ref, xs_ref, zero_ref, sem):
    tm = h_ref.shape[0]

    @pl.when(pl.program_id(0) == 0)
    def _():
        zero_ref[...] = jnp.zeros_like(zero_ref)

        def tile_copy(t):
            return pltpu.make_async_copy(zero_ref, xs_ref.at[pl.ds(t * MOE_TR, MOE_TR)], sem)

        def start(t, carry):
            @pl.when(zf_ref[t] != 0)
            def _():
                tile_copy(t).start()
            return carry

        def wait(t, carry):
            @pl.when(zf_ref[t] != 0)
            def _():
                tile_copy(t).wait()
            return carry

        lax.fori_loop(0, MOE_TILES, start, 0)
        lax.fori_loop(0, MOE_TILES, wait, 0)

    def issue(r, carry):
        _row_copy(h_ref, r, xs_ref, pos_ref[0, 0, 2 * r], sem).start()
        _row_copy(h_ref, r, xs_ref, pos_ref[0, 0, 2 * r + 1], sem).start()
        return carry

    lax.fori_loop(0, tm, issue, 0, unroll=8)
    for _ in range(2):
        pltpu.make_async_copy(h_ref, xs_ref.at[pl.ds(0, tm)], sem).wait()


def _dispatch(zero_flags, pos_blocks, h2):
    tm = TM_DISP
    return pl.pallas_call(
        _dispatch_kernel,
        grid_spec=pltpu.PrefetchScalarGridSpec(
            num_scalar_prefetch=1,
            grid=(T // tm,),
            in_specs=[pl.BlockSpec((1, 1, 2 * tm), lambda i, zf: (i, 0, 0), memory_space=pltpu.SMEM),
                      pl.BlockSpec((tm, D), lambda i, zf: (i, 0))],
            out_specs=pl.BlockSpec(memory_space=pl.ANY),
            scratch_shapes=[pltpu.VMEM((MOE_TR, D), f32), pltpu.SemaphoreType.DMA(())],
        ),
        out_shape=jax.ShapeDtypeStruct((MOE_ROWS, D), f32),
        compiler_params=pltpu.CompilerParams(
            dimension_semantics=("arbitrary",), vmem_limit_bytes=V7X_VMEM_LIMIT),
        name="moe_dispatch",
    )(zero_flags, pos_blocks, h2)


def _experts_kernel(te_ref, nt_ref, xs_ref, wg32_ref, wu32_ref, wd32_ref, y_ref, wg_ref, wu_ref, wd_ref):
    i = pl.program_id(0)
    live = i < nt_ref[0]

    @pl.when(live & ((i == 0) | (te_ref[i] != te_ref[jnp.maximum(i - 1, 0)])))
    def _():
        wg_ref[...] = wg32_ref[...].astype(bf16)
        wu_ref[...] = wu32_ref[...].astype(bf16)
        wd_ref[...] = wd32_ref[...].astype(bf16)

    @pl.when(live)
    def _():
        x = xs_ref[...].astype(bf16)
        act = _silu(_dot(x, wg_ref[...])) * _dot(x, wu_ref[...])
        y_ref[...] = _dot(act.astype(bf16), wd_ref[...])

    @pl.when(i >= nt_ref[0])
    def _():
        y_ref[...] = jnp.zeros_like(y_ref)


def _experts(l, tile_expert, n_tiles, xs, wg, wu, wd):
    def tile(i, nt):
        return jnp.minimum(i, nt[0] - 1)

    wspec = lambda a, b: pl.BlockSpec((None, None, a, b),
                                      lambda i, te, nt: (l, te[tile(i, nt)], 0, 0))
    return pl.pallas_call(
        _experts_kernel,
        grid_spec=pltpu.PrefetchScalarGridSpec(
            num_scalar_prefetch=2,
            grid=(MOE_TILES,),
            in_specs=[pl.BlockSpec((MOE_TR, D), lambda i, te, nt: (tile(i, nt), 0)),
                      wspec(D, D_EXP), wspec(D, D_EXP), wspec(D_EXP, D)],
            out_specs=pl.BlockSpec((MOE_TR, D), lambda i, te, nt: (i, 0)),
            scratch_shapes=[pltpu.VMEM((D, D_EXP), bf16), pltpu.VMEM((D, D_EXP), bf16),
                            pltpu.VMEM((D_EXP, D), bf16)],
        ),
        out_shape=jax.ShapeDtypeStruct((MOE_ROWS, D), f32),
        compiler_params=pltpu.CompilerParams(
            dimension_semantics=("arbitrary",), vmem_limit_bytes=V7X_VMEM_LIMIT),
        name="moe_experts",
    )(tile_expert, n_tiles, xs, wg, wu, wd)


def _combine_kernel(pos_ref, x1_ref, rt_ref, mod_ref, y_ref, oc_ref, ol_ref, y0_ref, y1_ref, sem):
    tm = x1_ref.shape[0]
    i = pl.program_id(0)
    n_ctx = T_CTX // tm

    def issue(r, carry):
        _row_copy(y_ref, pos_ref[0, 0, 2 * r], y0_ref, r, sem).start()
        _row_copy(y_ref, pos_ref[0, 0, 2 * r + 1], y1_ref, r, sem).start()
        return carry

    lax.fori_loop(0, tm, issue, 0, unroll=8)
    pltpu.make_async_copy(y_ref.at[pl.ds(0, tm)], y0_ref, sem).wait()
    pltpu.make_async_copy(y_ref.at[pl.ds(0, tm)], y1_ref, sem).wait()
    moe = rt_ref[:, 4:5] * y0_ref[...] + rt_ref[:, 5:6] * y1_ref[...]
    out = x1_ref[...] + mod_ref[5:6, :] * moe

    @pl.when(i < n_ctx)
    def _():
        oc_ref[...] = out

    @pl.when(i >= n_ctx)
    def _():
        ol_ref[...] = out


def _combine(l, pos_blocks, x1, route_t, mod, y):
    tm = TM_DISP
    row = lambda w: pl.BlockSpec((tm, w), lambda i: (i, 0))
    return pl.pallas_call(
        _combine_kernel,
        grid=(T // tm,),
        in_specs=[pl.BlockSpec((1, 1, 2 * tm), lambda i: (i, 0, 0), memory_space=pltpu.SMEM),
                  row(D), row(8),
                  pl.BlockSpec((None, None, 6, D), lambda i: (l, _cond_of_tile(i, tm), 0, 0)),
                  pl.BlockSpec(memory_space=pl.ANY)],
        out_specs=_two_stream_specs(tm, D),
        out_shape=[jax.ShapeDtypeStruct((T_CTX, D), f32), jax.ShapeDtypeStruct((T_LAT, D), f32)],
        scratch_shapes=[pltpu.VMEM((tm, D), f32), pltpu.VMEM((tm, D), f32),
                        pltpu.SemaphoreType.DMA(())],
        compiler_params=pltpu.CompilerParams(
            dimension_semantics=("arbitrary",), vmem_limit_bytes=V7X_VMEM_LIMIT),
        name="moe_combine",
    )(pos_blocks, x1, route_t, mod, y)


def _moe_plan(route, counts):
    cnt = counts[:, 0].astype(jnp.int32)
    tiles = (cnt + MOE_TR - 1) // MOE_TR
    tile_end = jnp.cumsum(tiles)
    tile_start = tile_end - tiles
    off = (tile_start * MOE_TR).astype(f32)
    experts = jnp.arange(N_EXP, dtype=f32)[:, None]
    pos = [jnp.sum(jnp.where(route[k][None, :] == experts, off[:, None], 0.0), axis=0) + route[2 + k]
           for k in range(2)]
    pos_blocks = jnp.stack(pos, axis=-1).astype(jnp.int32).reshape(T // TM_DISP, 1, 2 * TM_DISP)
    tid = jnp.arange(MOE_TILES, dtype=jnp.int32)
    te = jnp.minimum(jnp.sum(tid[:, None] >= tile_end[None, :], axis=1), N_EXP - 1).astype(jnp.int32)
    n_tiles = tile_end[-1:].astype(jnp.int32)
    is_last = jnp.any((tid[:, None] == tile_end[None, :] - 1) & (tiles[None, :] > 0), axis=1)
    zero_flags = ((tid >= n_tiles[0]) | is_last).astype(jnp.int32)
    return pos_blocks, te, n_tiles, zero_flags


def _rope_tables():
    pos = np.arange(DEC_SEQ)
    half = HD_A // 4
    freqs = ROPE_BASE ** (-np.arange(half, dtype=np.float64) / half)
    ang_r = (pos // GRID_W)[:, None] * freqs[None, :]
    ang_c = (pos % GRID_W)[:, None] * freqs[None, :]
    ang = np.concatenate([ang_r, ang_r, ang_c, ang_c], axis=1)
    sign = np.concatenate([-np.ones(half), np.ones(half)] * 2)[None, :]
    cos = np.tile(np.cos(ang), (1, 2))
    sin = np.tile(np.sin(ang) * sign, (1, 2))
    ident_c = np.ones((TM_PROJ, LANES))
    ident_s = np.zeros((TM_PROJ, LANES))
    return (jnp.asarray(np.concatenate([cos, ident_c]), f32),
            jnp.asarray(np.concatenate([sin, ident_s]), f32))


def kernel(x_prompt, x_sample, c, cache_k, cache_v, state_ret_fwd, state_ret_bwd, c_ctx,
           norm1_w, norm2_w, w_ada, b_ada, w_in, q_norm_w, k_norm_w, attn_sink,
           ret_decay_fwd, ret_decay_bwd, ret_gn_w, w_pa, w_pb, w_o, w_router, b_router,
           w_exp_gate, w_exp_up, w_exp_down):
    x_ctx, x_lat = x_prompt.reshape(T_CTX, D), x_sample.reshape(T_LAT, D)
    cond8 = jnp.zeros((N_COND, D), f32).at[0].set(c_ctx).at[1:1 + DEC_BATCH].set(c)
    mod = _adaln(cond8, w_ada, b_ada).reshape(DEPTH, N_COND, 6, D)

    cos_t, sin_t = _rope_tables()
    blk = np.arange(Q_A) // HD_A
    ones_blk = jnp.asarray((blk[:, None] == blk[None, :]) / HD_A, bf16)
    tri = jnp.asarray(np.triu(np.ones((TM_POST, TM_POST)), 1), bf16)
    qnw = jnp.tile(q_norm_w, (1, NH_A))
    knw = jnp.tile(k_norm_w, (1, NKV_A))
    wr_pad = jnp.pad(w_router, ((0, 0), (0, LANES - N_EXP)))
    br_col = jnp.broadcast_to(b_router[:, None], (N_EXP, LANES))
    ck = cache_k.reshape(DEC_BATCH, DEPTH, PAST, KV_A)
    cv = cache_v.reshape(DEC_BATCH, DEPTH, PAST, KV_A)
    s0 = jnp.concatenate([state_ret_fwd, state_ret_bwd], axis=3)

    new_k, new_v, new_sf, new_sb = [], [], [], []
    for l in range(DEPTH):
        lg_f = jnp.log1p(-jnp.exp(ret_decay_fwd[l].astype(f32)))
        lg_b = jnp.log1p(-jnp.exp(ret_decay_bwd[l].astype(f32)))
        dec = _decay_tables(lg_f, lg_b)

        qa, ka, va, qr, kr, vr, gr, ga, gb = _inproj(l, x_ctx, x_lat, norm1_w, mod, w_in, cos_t, sin_t,
                                                     ones_blk, qnw, knw)
        oa_ctx = _ctx_attn(l, qa, ka, va, attn_sink)
        oa_lat = _lat_attn(l, qa, ka, va, ck, cv, cos_t, sin_t, attn_sink)
        yr_ctx, sf, sb = _retention(l, qr, kr, vr, gr, None, dec, ret_gn_w, BATCH, SEQ, 0)
        yr_lat, _, _ = _retention(l, qr, kr, vr, gr, s0[:, l], dec, ret_gn_w, DEC_BATCH, DEC_SEQ,
                                  T_CTX // DEC_SEQ)
        x1, h2, route, counts = _post(l, x_ctx, x_lat, oa_ctx, oa_lat, yr_ctx, yr_lat, ga, gb,
                                      w_pa, w_pb, w_o, mod, norm2_w, wr_pad, br_col, tri)
        pos_blocks, tile_expert, n_tiles, zero_flags = _moe_plan(route, counts)
        xs = _dispatch(zero_flags, pos_blocks, h2)
        y = _experts(l, tile_expert, n_tiles, xs, w_exp_gate, w_exp_up, w_exp_down)
        x_ctx, x_lat = _combine(l, pos_blocks, x1, route.T, mod, y)

        new_k.append(ka[:T_CTX].reshape(BATCH, SEQ, NKV_A, HD_A))
        new_v.append(va[:T_CTX].reshape(BATCH, SEQ, NKV_A, HD_A))
        new_sf.append(sf)
        new_sb.append(sb)

    return (x_ctx.reshape(BATCH, SEQ, D), x_lat.reshape(DEC_BATCH, DEC_SEQ, D),
            jnp.stack(new_k, axis=1), jnp.stack(new_v, axis=1),
            jnp.stack(new_sf, axis=1), jnp.stack(new_sb, axis=1))
```

```python
import functools

import numpy as np
import jax
import jax.numpy as jnp
from jax import lax
from jax.experimental import pallas as pl
from jax.experimental.pallas import tpu as pltpu

D = 1024
BATCH, SEQ = 16, 256
DEC_BATCH, DEC_SEQ = 2, 2048
DEPTH = 2
PAST = 512
GRID_W = 64
NH_A, NKV_A, HD_A = 8, 2, 64
WINDOW = 128
NH_R, DK_R, DV_R = 4, 64, 128
CHUNK = 128
N_EXP, N_GROUPS, EXP_PER_GROUP = 16, 4, 4
D_EXP = 512
ROPE_BASE = 10000.0
EPS = 1e-6
NEG_INF = -1e30

Q_A = NH_A * HD_A
KV_A = NKV_A * HD_A
QK_R = NH_R * DK_R
V_R = NH_R * DV_R
C_QA = (0, Q_A)
C_KA = (C_QA[1], C_QA[1] + KV_A)
C_VA = (C_KA[1], C_KA[1] + KV_A)
C_QR = (C_VA[1], C_VA[1] + QK_R)
C_KR = (C_QR[1], C_QR[1] + QK_R)
C_VR = (C_KR[1], C_KR[1] + V_R)
C_GR = (C_VR[1], C_VR[1] + V_R)
C_GA = (C_GR[1], C_GR[1] + D)
C_GB = (C_GA[1], C_GA[1] + D)
D_IN = C_GB[1]

T_CTX = BATCH * SEQ
T_LAT = DEC_BATCH * DEC_SEQ
T = T_CTX + T_LAT
N_COND = 8

LANES = 128
V7X_VMEM_LIMIT = 56 * 1024 * 1024

TM_PROJ = 512
TM_POST = 256
W_CHUNK = 256
TM_DISP = 256
MOE_TR = 256
MOE_TILES = 2 * T // MOE_TR + N_EXP
MOE_ROWS = MOE_TILES * MOE_TR
MOE_W = D // 2
ATT_QB = 256

f32 = jnp.float32
bf16 = jnp.bfloat16


def _dot(a, b):
    return jnp.dot(a, b, preferred_element_type=f32)


def _dot_t(a, b):
    return lax.dot_general(a, b, (((1,), (1,)), ((), ())), preferred_element_type=f32)


def _split(x):
    hi = x.astype(bf16)
    lo = (x - hi.astype(f32)).astype(bf16)
    return hi, lo


def _dot3(a, b):
    ah, al = _split(a)
    bh, bl = _split(b)
    return _dot(ah, bh) + (_dot(ah, bl) + _dot(al, bh))


def _pack_halves(x):
    m = x.shape[1] // 2
    hi = lax.bitcast_convert_type(x[:, :m].astype(bf16).astype(f32), jnp.uint32)
    lo = lax.bitcast_convert_type(x[:, m:].astype(bf16).astype(f32), jnp.uint32)
    return hi | (lo >> 16)


def _unpack_halves(w):
    hi = lax.bitcast_convert_type(w & jnp.uint32(0xFFFF0000), f32)
    lo = lax.bitcast_convert_type(w << 16, f32)
    return jnp.concatenate([hi, lo], axis=1)


def _sigmoid(x):
    return 1.0 / (1.0 + jnp.exp(-x))


def _silu(x):
    return x * _sigmoid(x)


def _cond_of_tile(i, tm):
    n_ctx = T_CTX // tm
    per_b = DEC_SEQ // tm
    return jnp.where(i < n_ctx, 0, 1 + jnp.maximum(i - n_ctx, 0) // per_b)


def _adaln_kernel(cond_ref, w_ref, b_ref, o_ref):
    a = _silu(cond_ref[...])
    o_ref[0] = _dot3(a, w_ref[0]) + b_ref[0]


def _adaln(cond8, w_ada, b_ada):
    tn = 1536
    return pl.pallas_call(
        _adaln_kernel,
        grid=(DEPTH, 6 * D // tn),
        in_specs=[
            pl.BlockSpec((N_COND, D), lambda l, j: (0, 0)),
            pl.BlockSpec((1, D, tn), lambda l, j: (l, 0, j)),
            pl.BlockSpec((1, 1, tn), lambda l, j: (l, 0, j)),
        ],
        out_specs=pl.BlockSpec((1, N_COND, tn), lambda l, j: (l, 0, j)),
        out_shape=jax.ShapeDtypeStruct((DEPTH, N_COND, 6 * D), f32),
        compiler_params=pltpu.CompilerParams(
            dimension_semantics=("parallel", "parallel"), vmem_limit_bytes=V7X_VMEM_LIMIT),
        name="adaln",
    )(cond8, w_ada, b_ada.reshape(DEPTH, 1, 6 * D))


def _rope(x, cos, sin_signed, first_half):
    fwd = pltpu.roll(x, 16, 1)
    bwd = pltpu.roll(x, LANES - 16, 1)
    partner = jnp.where(first_half, bwd, fwd)
    return x * cos + partner * sin_signed


def _head_rms(x, ones_blk, w):
    n = x.shape[1]
    sq_hi, sq_lo = _split(x * x)
    blk = ones_blk[0:n, 0:n]
    mean = _dot(sq_hi, blk) + _dot(sq_lo, blk)
    return x * lax.rsqrt(mean + EPS) * w


def _load_cast(w_hbm, dst_ref, stage_ref, sems):
    k, n = w_hbm.shape

    def chunk_copy(c):
        return pltpu.make_async_copy(w_hbm.at[:, pl.ds(c * W_CHUNK, W_CHUNK)],
                                     stage_ref.at[c % 2, pl.ds(0, k)], sems.at[c % 2])

    n_chunks = n // W_CHUNK
    chunk_copy(0).start()
    for c in range(n_chunks):
        if c + 1 < n_chunks:
            chunk_copy(c + 1).start()
        chunk_copy(c).wait()
        dst_ref[:, c * W_CHUNK:(c + 1) * W_CHUNK] = stage_ref[c % 2, 0:k, :].astype(bf16)


def _tile_of_two(i, n_first, a_ref, b_ref):
    return jnp.where(i < n_first, a_ref[...], b_ref[...])


def _inproj_kernel(l, xc_ref, xl_ref, n1_ref, mod_ref, w_hbm, cos_ref, sin_ref, ones_ref, qnw_ref,
                   knw_ref, qa_ref, ka_ref, va_ref, qr_ref, kr_ref, vr_ref, gr_ref, ga_ref, gb_ref,
                   w_ref, stage_ref, sems):
    i = pl.program_id(0)

    @pl.when(i == 0)
    def _():
        _load_cast(w_hbm.at[l], w_ref, stage_ref, sems)

    x = _tile_of_two(i, T_CTX // xc_ref.shape[0], xc_ref, xl_ref)
    y = x * lax.rsqrt(jnp.mean(x * x, axis=-1, keepdims=True) + EPS) * n1_ref[...]
    h = (y * (1.0 + mod_ref[1:2, :]) + mod_ref[0:1, :]).astype(bf16)

    def proj(c):
        return _dot(h, w_ref[:, c[0]:c[1]])

    cos = cos_ref[...]
    sin = sin_ref[...]
    lane = lax.broadcasted_iota(jnp.int32, cos.shape, 1)
    first_half = (lane % 32) < 16

    def rope_all(v):
        parts = [_rope(v[:, j:j + LANES], cos, sin, first_half) for j in range(0, v.shape[1], LANES)]
        return parts[0] if len(parts) == 1 else jnp.concatenate(parts, axis=1)

    ones_blk = ones_ref[...]
    qa = _head_rms(proj(C_QA), ones_blk, qnw_ref[...])
    qa_ref[...] = rope_all(qa).astype(bf16)
    ka_ref[...] = _head_rms(proj(C_KA), ones_blk, knw_ref[...])
    va_ref[...] = proj(C_VA)
    qr_ref[...] = rope_all(proj(C_QR)).astype(bf16)
    kr_ref[...] = (rope_all(proj(C_KR)) * (DK_R ** -0.5)).astype(bf16)
    vr_ref[...] = proj(C_VR).astype(bf16)
    gr_ref[...] = proj(C_GR).astype(bf16)
    ga_ref[...] = proj(C_GA).astype(bf16)
    gb_ref[...] = proj(C_GB).astype(bf16)


def _two_stream_specs(tm, width):
    n_ctx = T_CTX // tm
    return [pl.BlockSpec((tm, width), lambda i, *_: (jnp.minimum(i, n_ctx - 1), 0)),
            pl.BlockSpec((tm, width), lambda i, *_: (jnp.maximum(i - n_ctx, 0), 0))]


def _inproj(l, x_ctx, x_lat, norm1_w, mod, w_in, cos_t, sin_t, ones_blk, qnw, knw):
    tm = TM_PROJ
    n_ctx = T_CTX // tm
    per_b = DEC_SEQ // tm

    def tab_map(i):
        return (jnp.where(i < n_ctx, per_b, jnp.maximum(i - n_ctx, 0) % per_b), 0)

    row = lambda i: (i, 0)
    const = lambda i: (0, 0)
    widths = [(Q_A, bf16), (KV_A, f32), (KV_A, f32), (QK_R, bf16), (QK_R, bf16), (V_R, bf16),
              (V_R, bf16), (D, bf16), (D, bf16)]
    return pl.pallas_call(
        functools.partial(_inproj_kernel, l),
        grid=(T // tm,),
        in_specs=_two_stream_specs(tm, D) + [
            pl.BlockSpec((1, D), const),
            pl.BlockSpec((None, None, 6, D), lambda i: (l, _cond_of_tile(i, tm), 0, 0)),
            pl.BlockSpec(memory_space=pl.ANY),
            pl.BlockSpec((tm, LANES), tab_map),
            pl.BlockSpec((tm, LANES), tab_map),
            pl.BlockSpec((Q_A, Q_A), const),
            pl.BlockSpec((1, Q_A), const),
            pl.BlockSpec((1, KV_A), const),
        ],
        out_specs=[pl.BlockSpec((tm, w), row) for w, _ in widths],
        out_shape=[jax.ShapeDtypeStruct((T, w), dt) for w, dt in widths],
        scratch_shapes=[pltpu.VMEM((D, D_IN), bf16), pltpu.VMEM((2, D, W_CHUNK), f32),
                        pltpu.SemaphoreType.DMA((2,))],
        compiler_params=pltpu.CompilerParams(
            dimension_semantics=("arbitrary",), vmem_limit_bytes=V7X_VMEM_LIMIT),
        name="inproj",
    )(x_ctx, x_lat, norm1_w[l:l + 1], mod, w_in, cos_t, sin_t, ones_blk, qnw[l:l + 1], knw[l:l + 1])


def _head_blocks(t, kv, lo_mask):
    r = pltpu.roll(t, HD_A, 1)
    if kv == 0:
        a = jnp.where(lo_mask, t, 0.0)
        b = jnp.where(lo_mask, 0.0, r)
    else:
        a = jnp.where(lo_mask, r, 0.0)
        b = jnp.where(lo_mask, 0.0, t)
    return jnp.concatenate([a, b], axis=0).astype(bf16)


def _ctx_attn_kernel(sink_ref, q_ref, k_ref, v_ref, o_ref):
    k = k_ref[...]
    v = v_ref[...]
    n = k.shape[0]
    lo_mask = lax.broadcasted_iota(jnp.int32, k.shape, 1) < HD_A
    scale = HD_A ** -0.5
    for kv in range(NKV_A):
        kblk = _head_blocks(k, kv, lo_mask)
        vblk = _head_blocks(v, kv, lo_mask)
        for pr in range(2):
            pi = kv * 2 + pr
            s = _dot_t(q_ref[:, pi * LANES:(pi + 1) * LANES], kblk) * scale
            ps = []
            for hh in range(2):
                sk = sink_ref[2 * pi + hh]
                sh = s[:, hh * n:(hh + 1) * n]
                m = jnp.maximum(jnp.max(sh, axis=-1, keepdims=True), sk)
                p = jnp.exp(sh - m)
                den = jnp.sum(p, axis=-1, keepdims=True) + jnp.exp(sk - m)
                ps.append((p / den).astype(bf16))
            o = _dot(jnp.concatenate(ps, axis=1), vblk)
            o_ref[:, pi * LANES:(pi + 1) * LANES] = o.astype(bf16)


def _ctx_attn(l, qa, ka, va, sink):
    blk = lambda w: pl.BlockSpec((SEQ, w), lambda b: (b, 0))
    return pl.pallas_call(
        _ctx_attn_kernel,
        grid=(BATCH,),
        in_specs=[pl.BlockSpec(memory_space=pltpu.SMEM), blk(Q_A), blk(KV_A), blk(KV_A)],
        out_specs=blk(Q_A),
        out_shape=jax.ShapeDtypeStruct((T_CTX, Q_A), bf16),
        compiler_params=pltpu.CompilerParams(
            dimension_semantics=("parallel",), vmem_limit_bytes=V7X_VMEM_LIMIT),
        name="ctx_attn",
    )(sink[l], qa, ka, va)


def _lat_attn_kernel(sink_ref, q_ref, k_ref, v_ref, kc_ref, vc_ref, cos_ref, sin_ref, o_ref):
    j = pl.program_id(1)
    qb = ATT_QB
    win = 2 * qb
    ws = pl.multiple_of(jnp.clip(j * qb - WINDOW, 0, DEC_SEQ - win), WINDOW)
    lo_mask = lax.broadcasted_iota(jnp.int32, (win, LANES), 1) < HD_A
    lane = lax.broadcasted_iota(jnp.int32, (win, LANES), 1)
    kw = _rope(k_ref[pl.ds(ws, win), :], cos_ref[pl.ds(ws, win), :], sin_ref[pl.ds(ws, win), :],
               (lane % 32) < 16)
    vw = v_ref[pl.ds(ws, win), :]
    kc = kc_ref[...]
    vc = vc_ref[...]
    qpos = j * qb + (lax.broadcasted_iota(jnp.int32, (2 * qb, win), 0) & (qb - 1))
    kpos = ws + lax.broadcasted_iota(jnp.int32, (2 * qb, win), 1)
    valid = jnp.abs(qpos - kpos) <= WINDOW
    scale = HD_A ** -0.5
    for kv in range(NKV_A):
        kc_blk = _head_blocks(kc, kv, lo_mask[:PAST])
        vc_blk = _head_blocks(vc, kv, lo_mask[:PAST])
        kw_blk = _head_blocks(kw, kv, lo_mask)
        vw_blk = _head_blocks(vw, kv, lo_mask)
        q2 = jnp.concatenate([q_ref[:, (2 * kv) * LANES:(2 * kv + 1) * LANES],
                              q_ref[:, (2 * kv + 1) * LANES:(2 * kv + 2) * LANES]], axis=0)
        s_c = _dot_t(q2, kc_blk) * scale
        s_w = _dot_t(q2, kw_blk) * scale
        pcs, pws = [], []
        for hh in range(2):
            row = lax.broadcasted_iota(jnp.int32, (2 * qb, 1), 0)
            sk = jnp.where(row < qb, sink_ref[4 * kv + hh], sink_ref[4 * kv + 2 + hh])
            sc = s_c[:, hh * PAST:(hh + 1) * PAST]
            sw = jnp.where(valid, s_w[:, hh * win:(hh + 1) * win], NEG_INF)
            m = jnp.maximum(jnp.maximum(jnp.max(sc, axis=-1, keepdims=True),
                                        jnp.max(sw, axis=-1, keepdims=True)), sk)
            pc = jnp.exp(sc - m)
            pw = jnp.exp(sw - m)
            den = (jnp.sum(pc, axis=-1, keepdims=True) + jnp.sum(pw, axis=-1, keepdims=True)
                   + jnp.exp(sk - m))
            inv = 1.0 / den
            pcs.append((pc * inv).astype(bf16))
            pws.append((pw * inv).astype(bf16))
        o = _dot(jnp.concatenate(pcs, axis=1), vc_blk) + _dot(jnp.concatenate(pws, axis=1), vw_blk)
        o_ref[:, (2 * kv) * LANES:(2 * kv + 1) * LANES] = o[:qb].astype(bf16)
        o_ref[:, (2 * kv + 1) * LANES:(2 * kv + 2) * LANES] = o[qb:].astype(bf16)


def _lat_attn(l, qa, ka, va, cache_k, cache_v, cos_l, sin_l, sink):
    qb = ATT_QB
    nq = DEC_SEQ // qb
    ctx_blocks = T_CTX // DEC_SEQ
    seq = lambda b, j: (ctx_blocks + b, 0)
    return pl.pallas_call(
        _lat_attn_kernel,
        grid=(DEC_BATCH, nq),
        in_specs=[
            pl.BlockSpec(memory_space=pltpu.SMEM),
            pl.BlockSpec((qb, Q_A), lambda b, j: (T_CTX // qb + b * nq + j, 0)),
            pl.BlockSpec((DEC_SEQ, KV_A), seq),
            pl.BlockSpec((DEC_SEQ, KV_A), seq),
            pl.BlockSpec((None, None, PAST, KV_A), lambda b, j: (b, l, 0, 0)),
            pl.BlockSpec((None, None, PAST, KV_A), lambda b, j: (b, l, 0, 0)),
            pl.BlockSpec((DEC_SEQ, LANES), lambda b, j: (0, 0)),
            pl.BlockSpec((DEC_SEQ, LANES), lambda b, j: (0, 0)),
        ],
        out_specs=pl.BlockSpec((qb, Q_A), lambda b, j: (b * nq + j, 0)),
        out_shape=jax.ShapeDtypeStruct((T_LAT, Q_A), bf16),
        compiler_params=pltpu.CompilerParams(
            dimension_semantics=("parallel", "parallel"), vmem_limit_bytes=V7X_VMEM_LIMIT),
        name="lat_attn",
    )(sink[l], qa, ka, va, cache_k, cache_v, cos_l, sin_l)


def _dup_heads(pair, lo_mask):
    r = pltpu.roll(pair, DK_R, 1)
    return jnp.where(lo_mask, pair, r), jnp.where(lo_mask, r, pair)


def _retention_kernel(has_s0, n_chunks, *refs):
    if has_s0:
        (q_ref, k_ref, v_ref, g_ref, s0_ref, mask_ref, qdec_ref, kdec_ref, cdec_ref, gnw_ref,
         y_ref, sf_ref, sb_ref, ds_ref, st_ref) = refs
    else:
        (q_ref, k_ref, v_ref, g_ref, mask_ref, qdec_ref, kdec_ref, cdec_ref, gnw_ref,
         y_ref, sf_ref, sb_ref, ds_ref, st_ref) = refs
        s0_ref = None
    C = CHUNK
    lo_mask = lax.broadcasted_iota(jnp.int32, (C, LANES), 1) < DK_R

    def inc_body(c, carry):
        r0 = pl.multiple_of(c * C, C)
        for pr in range(2):
            kp = k_ref[pl.ds(r0, C), pr * LANES:(pr + 1) * LANES].astype(f32)
            for hh, kd in enumerate(_dup_heads(kp, lo_mask)):
                h = 2 * pr + hh
                kd = (kd * kdec_ref[:, h * LANES:(h + 1) * LANES]).astype(bf16)
                vh = v_ref[pl.ds(r0, C), h * DV_R:(h + 1) * DV_R]
                ds_ref[c, h] = lax.dot_general(kd, vh, (((0,), (0,)), ((), ())),
                                               preferred_element_type=f32)
        return carry

    lax.fori_loop(0, n_chunks, inc_body, 0)

    for h in range(NH_R):
        cf = cdec_ref[h, 0:DK_R, :]
        cb = cdec_ref[h, DK_R:2 * DK_R, :]
        if has_s0:
            init_f = s0_ref[h, 0:DK_R, :]
            init_b = s0_ref[h, DK_R:2 * DK_R, :]
        else:
            init_f = jnp.zeros((DK_R, DV_R), f32)
            init_b = init_f

        def fwd_body(c, s, h=h, cf=cf):
            st_ref[c, h, 0:DK_R, :] = s
            return s * cf + ds_ref[c, h, 0:DK_R, :]

        def bwd_body(i, s, h=h, cb=cb):
            c = n_chunks - 1 - i
            st_ref[c, h, DK_R:2 * DK_R, :] = s
            return s * cb + ds_ref[c, h, DK_R:2 * DK_R, :]

        sf_ref[h] = lax.fori_loop(0, n_chunks, fwd_body, init_f)
        sb_ref[h] = lax.fori_loop(0, n_chunks, bwd_body, init_b)

    def out_body(c, carry):
        r0 = pl.multiple_of(c * C, C)
        for pr in range(2):
            qp = q_ref[pl.ds(r0, C), pr * LANES:(pr + 1) * LANES]
            kp = k_ref[pl.ds(r0, C), pr * LANES:(pr + 1) * LANES].astype(f32)
            kblk = jnp.concatenate([jnp.where(lo_mask, kp, 0.0), jnp.where(lo_mask, 0.0, kp)],
                                   axis=0).astype(bf16)
            a2 = _dot_t(qp, kblk)
            for hh, qd in enumerate(_dup_heads(qp.astype(f32), lo_mask)):
                h = 2 * pr + hh
                a = (a2[:, hh * C:(hh + 1) * C] * mask_ref[h]).astype(bf16)
                vh = v_ref[pl.ds(r0, C), h * DV_R:(h + 1) * DV_R]
                qd = (qd * qdec_ref[:, h * LANES:(h + 1) * LANES]).astype(bf16)
                o = _dot(a, vh) + _dot(qd, st_ref[c, h].astype(bf16))
                mu = jnp.mean(o, axis=-1, keepdims=True)
                d = o - mu
                var = jnp.mean(d * d, axis=-1, keepdims=True)
                yh = d * lax.rsqrt(var + EPS) * gnw_ref[:, h * DV_R:(h + 1) * DV_R]
                g = g_ref[pl.ds(r0, C), h * DV_R:(h + 1) * DV_R].astype(f32)
                y_ref[pl.ds(r0, C), h * DV_R:(h + 1) * DV_R] = (yh * _silu(g)).astype(bf16)
        return carry

    lax.fori_loop(0, n_chunks, out_body, 0)


def _retention(l, qr, kr, vr, gr, s0, dec, gnw, nb, seq, row_block0):
    n_chunks = seq // CHUNK
    mask, qdec, kdec, cdec = dec
    has_s0 = s0 is not None
    tok = lambda w: pl.BlockSpec((seq, w), lambda b: (row_block0 + b, 0))
    full = lambda a: pl.BlockSpec(a.shape, lambda b: (0,) * a.ndim)
    in_specs = [tok(QK_R), tok(QK_R), tok(V_R), tok(V_R)]
    args = [qr, kr, vr, gr]
    if has_s0:
        in_specs.append(pl.BlockSpec((None, NH_R, 2 * DK_R, DV_R), lambda b: (b, 0, 0, 0)))
        args.append(s0)
    in_specs += [full(mask), full(qdec), full(kdec), full(cdec),
                 pl.BlockSpec((1, V_R), lambda b: (0, 0))]
    args += [mask, qdec, kdec, cdec, gnw[l:l + 1]]
    st_spec = pl.BlockSpec((None, NH_R, DK_R, DV_R), lambda b: (b, 0, 0, 0))
    return pl.pallas_call(
        functools.partial(_retention_kernel, has_s0, n_chunks),
        grid=(nb,),
        in_specs=in_specs,
        out_specs=[pl.BlockSpec((seq, V_R), lambda b: (b, 0)), st_spec, st_spec],
        out_shape=[jax.ShapeDtypeStruct((nb * seq, V_R), bf16),
                   jax.ShapeDtypeStruct((nb, NH_R, DK_R, DV_R), f32),
                   jax.ShapeDtypeStruct((nb, NH_R, DK_R, DV_R), f32)],
        scratch_shapes=[pltpu.VMEM((n_chunks, NH_R, 2 * DK_R, DV_R), f32),
                        pltpu.VMEM((n_chunks, NH_R, 2 * DK_R, DV_R), f32)],
        compiler_params=pltpu.CompilerParams(
            dimension_semantics=("parallel",), vmem_limit_bytes=V7X_VMEM_LIMIT),
        name="retention_lat" if has_s0 else "retention_ctx",
    )(*args)


def _decay_tables(lg_f, lg_b):
    C = CHUNK
    idx = jnp.arange(C, dtype=f32)
    diff = idx[:, None] - idx[None, :]
    lower = jnp.where(diff >= 0, jnp.exp(jnp.maximum(diff, 0.0)[None] * lg_f[:, None, None]), 0.0)
    upper = jnp.where(diff <= 0, jnp.exp(jnp.maximum(-diff, 0.0)[None] * lg_b[:, None, None]), 0.0)
    mask = lower + upper
    qf = jnp.exp((idx + 1.0)[:, None] * lg_f[None, :])
    qb = jnp.exp((C - idx)[:, None] * lg_b[None, :])
    kf = jnp.exp((C - 1.0 - idx)[:, None] * lg_f[None, :])
    kb = jnp.exp(idx[:, None] * lg_b[None, :])

    def lanes(f, b):
        both = jnp.stack([f, b], axis=-1)
        return jnp.repeat(both, DK_R, axis=-1).reshape(C, NH_R * 2 * DK_R)

    cf = jnp.exp(C * lg_f)
    cb = jnp.exp(C * lg_b)
    cdec = jnp.broadcast_to(jnp.stack([cf, cb], axis=-1)[:, :, None, None],
                            (NH_R, 2, DK_R, DV_R)).reshape(NH_R, 2 * DK_R, DV_R)
    return mask, lanes(qf, qb), lanes(kf, kb), cdec


def _post_kernel(l, xc_ref, xl_ref, oac_ref, oal_ref, yrc_ref, yrl_ref, ga_ref, gb_ref, wpa_hbm, wpb_hbm,
                 wo_hbm, mod_ref, n2_ref, wr_ref, br_ref, tri_ref, x1_ref, h2_ref, route_ref, cnt_ref,
                 carry_ref, wpa_ref, wpb_ref, wo_ref, stage_ref, sems):
    i = pl.program_id(0)
    tm = xc_ref.shape[0]
    n_ctx = T_CTX // tm

    @pl.when(i == 0)
    def _():
        carry_ref[...] = jnp.zeros_like(carry_ref)
        _load_cast(wpa_hbm.at[l], wpa_ref, stage_ref, sems)
        _load_cast(wpb_hbm.at[l], wpb_ref, stage_ref, sems)
        _load_cast(wo_hbm.at[l], wo_ref, stage_ref, sems)

    ga = _sigmoid(ga_ref[...].astype(f32))
    gb = _sigmoid(gb_ref[...].astype(f32))
    oa = _tile_of_two(i, n_ctx, oac_ref, oal_ref)
    yr = _tile_of_two(i, n_ctx, yrc_ref, yrl_ref)
    merged = ga * _dot(oa, wpa_ref[...]) + gb * _dot(yr, wpb_ref[...])
    mix = _dot(merged.astype(bf16), wo_ref[...])
    x1 = _tile_of_two(i, n_ctx, xc_ref, xl_ref) + mod_ref[2:3, :] * mix
    x1_ref[...] = x1
    y = x1 * lax.rsqrt(jnp.mean(x1 * x1, axis=-1, keepdims=True) + EPS) * n2_ref[...]
    h2 = y * (1.0 + mod_ref[4:5, :]) + mod_ref[3:4, :]
    h2_ref[...] = _pack_halves(h2)

    logits = _dot3(h2, wr_ref[...])
    lt = logits.T[0:N_EXP, :]
    scores = _sigmoid(lt)
    sel = scores + br_ref[:, 0:1]
    row = lax.broadcasted_iota(jnp.int32, (N_EXP, tm), 0)

    best = None
    bg = None
    for g in range(N_GROUPS):
        a, b, c, d = (sel[EXP_PER_GROUP * g + k:EXP_PER_GROUP * g + k + 1, :] for k in range(4))
        p, q = jnp.maximum(a, b), jnp.minimum(a, b)
        r, s = jnp.maximum(c, d), jnp.minimum(c, d)
        gs = jnp.maximum(p, r) + jnp.maximum(jnp.minimum(p, r), jnp.maximum(q, s))
        if g == 0:
            best, bg = gs, jnp.zeros((1, tm), jnp.int32)
        else:
            upd = gs > best
            bg = jnp.where(upd, g, bg)
            best = jnp.where(upd, gs, best)
    masked = jnp.where(jnp.right_shift(row, 2) == bg, sel, NEG_INF)
    m1 = jnp.max(masked, axis=0, keepdims=True)
    i1 = jnp.min(jnp.where(masked == m1, row, N_EXP), axis=0, keepdims=True)
    masked2 = jnp.where(row == i1, NEG_INF, masked)
    m2 = jnp.max(masked2, axis=0, keepdims=True)
    i2 = jnp.min(jnp.where(masked2 == m2, row, N_EXP), axis=0, keepdims=True)
    oh1 = row == i1
    oh2 = row == i2
    s1 = jnp.sum(jnp.where(oh1, scores, 0.0), axis=0, keepdims=True)
    s2 = jnp.sum(jnp.where(oh2, scores, 0.0), axis=0, keepdims=True)
    den = s1 + s2

    oh = jnp.where(oh1 | oh2, 1.0, 0.0)
    tot = carry_ref[:, 0:1] + _dot(oh.astype(bf16), tri_ref[...])
    r1 = jnp.sum(jnp.where(oh1, tot, 0.0), axis=0, keepdims=True)
    r2 = jnp.sum(jnp.where(oh2, tot, 0.0), axis=0, keepdims=True)
    carry_ref[...] = carry_ref[...] + jnp.sum(oh, axis=1, keepdims=True)
    cnt_ref[...] = carry_ref[...]

    route_ref[0:1, :] = i1.astype(f32)
    route_ref[1:2, :] = i2.astype(f32)
    route_ref[2:3, :] = r1
    route_ref[3:4, :] = r2
    route_ref[4:5, :] = s1 / den
    route_ref[5:6, :] = s2 / den
    route_ref[6:8, :] = jnp.zeros((2, tm), f32)


def _post(l, x_ctx, x_lat, oa_ctx, oa_lat, yr_ctx, yr_lat, ga, gb, wpa, wpb, wo, mod, norm2_w, wr_pad,
          br_col, tri):
    tm = TM_POST
    row = lambda w: pl.BlockSpec((tm, w), lambda i: (i, 0))
    const = lambda a: pl.BlockSpec(a.shape, lambda i: (0,) * a.ndim)
    hbm = pl.BlockSpec(memory_space=pl.ANY)
    return pl.pallas_call(
        functools.partial(_post_kernel, l),
        grid=(T // tm,),
        in_specs=(_two_stream_specs(tm, D) + _two_stream_specs(tm, Q_A) + _two_stream_specs(tm, V_R)
                  + [row(D), row(D), hbm, hbm, hbm,
                     pl.BlockSpec((None, None, 6, D), lambda i: (l, _cond_of_tile(i, tm), 0, 0)),
                     pl.BlockSpec((1, D), lambda i: (0, 0)), const(wr_pad), const(br_col), const(tri)]),
        out_specs=[row(D), row(D // 2), pl.BlockSpec((8, tm), lambda i: (0, i)),
                   pl.BlockSpec((N_EXP, LANES), lambda i: (0, 0))],
        out_shape=[jax.ShapeDtypeStruct((T, D), f32), jax.ShapeDtypeStruct((T, D // 2), jnp.uint32),
                   jax.ShapeDtypeStruct((8, T), f32), jax.ShapeDtypeStruct((N_EXP, LANES), f32)],
        scratch_shapes=[pltpu.VMEM((N_EXP, LANES), f32), pltpu.VMEM((Q_A, D), bf16),
                        pltpu.VMEM((V_R, D), bf16), pltpu.VMEM((D, D), bf16),
                        pltpu.VMEM((2, D, W_CHUNK), f32), pltpu.SemaphoreType.DMA((2,))],
        compiler_params=pltpu.CompilerParams(
            dimension_semantics=("arbitrary",), vmem_limit_bytes=V7X_VMEM_LIMIT),
        name="post_router",
    )(x_ctx, x_lat, oa_ctx, oa_lat, yr_ctx, yr_lat, ga, gb, wpa, wpb, wo, mod, norm2_w[l:l + 1],
      wr_pad, br_col, tri)


def _row_copy(src_ref, src_row, dst_ref, dst_row, sem):
    return pltpu.make_async_copy(src_ref.at[pl.ds(src_row, 1)], dst_ref.at[pl.ds(dst_row, 1)], sem)


def _dispatch_kernel(zf_ref, pos_ref, h_ref, xs_ref, zero_ref, sem):
    tm = h_ref.shape[0]

    @pl.when(pl.program_id(0) == 0)
    def _():
        zero_ref[...] = jnp.zeros_like(zero_ref)

        def tile_copy(t):
            return pltpu.make_async_copy(zero_ref, xs_ref.at[pl.ds(t * MOE_TR, MOE_TR)], sem)

        def start(t, carry):
            @pl.when(zf_ref[t] != 0)
            def _():
                tile_copy(t).start()
            return carry

        def wait(t, carry):
            @pl.when(zf_ref[t] != 0)
            def _():
                tile_copy(t).wait()
            return carry

        lax.fori_loop(0, MOE_TILES, start, 0)
        lax.fori_loop(0, MOE_TILES, wait, 0)

    for r in range(tm):
        for k in range(2):
            _row_copy(h_ref, r, xs_ref, pos_ref[0, 0, 2 * r + k], sem).start(priority=k)
    for _ in range(2):
        pltpu.make_async_copy(h_ref, xs_ref.at[pl.ds(0, tm)], sem).wait()


def _dispatch(zero_flags, pos_blocks, h2p):
    tm = TM_DISP
    return pl.pallas_call(
        _dispatch_kernel,
        grid_spec=pltpu.PrefetchScalarGridSpec(
            num_scalar_prefetch=1,
            grid=(T // tm,),
            in_specs=[pl.BlockSpec((1, 1, 2 * tm), lambda i, zf: (i, 0, 0), memory_space=pltpu.SMEM),
                      pl.BlockSpec((tm, MOE_W), lambda i, zf: (i, 0))],
            out_specs=pl.BlockSpec(memory_space=pl.ANY),
            scratch_shapes=[pltpu.VMEM((MOE_TR, MOE_W), jnp.uint32), pltpu.SemaphoreType.DMA(())],
        ),
        out_shape=jax.ShapeDtypeStruct((MOE_ROWS, MOE_W), jnp.uint32),
        compiler_params=pltpu.CompilerParams(
            dimension_semantics=("arbitrary",), vmem_limit_bytes=V7X_VMEM_LIMIT),
        name="moe_dispatch",
    )(zero_flags, pos_blocks, h2p)


def _experts_kernel(te_ref, nt_ref, xs_ref, wg32_ref, wu32_ref, wd32_ref, y_ref, wg_ref, wu_ref, wd_ref):
    i = pl.program_id(0)
    live = i < nt_ref[0]

    @pl.when(live & ((i == 0) | (te_ref[i] != te_ref[jnp.maximum(i - 1, 0)])))
    def _():
        wg_ref[...] = wg32_ref[...].astype(bf16)
        wu_ref[...] = wu32_ref[...].astype(bf16)
        wd_ref[...] = wd32_ref[...].astype(bf16)

    @pl.when(live)
    def _():
        x = _unpack_halves(xs_ref[...]).astype(bf16)
        act = _silu(_dot(x, wg_ref[...])) * _dot(x, wu_ref[...])
        y_ref[...] = _pack_halves(_dot(act.astype(bf16), wd_ref[...]))

    @pl.when(i >= nt_ref[0])
    def _():
        y_ref[...] = jnp.zeros_like(y_ref)


def _experts(l, tile_expert, n_tiles, xs, wg, wu, wd):
    def tile(i, nt):
        return jnp.minimum(i, nt[0] - 1)

    wspec = lambda a, b: pl.BlockSpec((None, None, a, b),
                                      lambda i, te, nt: (l, te[tile(i, nt)], 0, 0))
    return pl.pallas_call(
        _experts_kernel,
        grid_spec=pltpu.PrefetchScalarGridSpec(
            num_scalar_prefetch=2,
            grid=(MOE_TILES,),
            in_specs=[pl.BlockSpec((MOE_TR, MOE_W), lambda i, te, nt: (tile(i, nt), 0)),
                      wspec(D, D_EXP), wspec(D, D_EXP), wspec(D_EXP, D)],
            out_specs=pl.BlockSpec((MOE_TR, MOE_W), lambda i, te, nt: (i, 0)),
            scratch_shapes=[pltpu.VMEM((D, D_EXP), bf16), pltpu.VMEM((D, D_EXP), bf16),
                            pltpu.VMEM((D_EXP, D), bf16)],
        ),
        out_shape=jax.ShapeDtypeStruct((MOE_ROWS, MOE_W), jnp.uint32),
        compiler_params=pltpu.CompilerParams(
            dimension_semantics=("arbitrary",), vmem_limit_bytes=V7X_VMEM_LIMIT),
        name="moe_experts",
    )(tile_expert, n_tiles, xs, wg, wu, wd)


def _combine_kernel(pos_ref, x1_ref, rt_ref, mod_ref, y_ref, oc_ref, ol_ref, y0_ref, y1_ref, sem):
    tm = x1_ref.shape[0]
    i = pl.program_id(0)
    n_ctx = T_CTX // tm

    for r in range(tm):
        _row_copy(y_ref, pos_ref[0, 0, 2 * r], y0_ref, r, sem).start(priority=0)
        _row_copy(y_ref, pos_ref[0, 0, 2 * r + 1], y1_ref, r, sem).start(priority=1)
    pltpu.make_async_copy(y_ref.at[pl.ds(0, tm)], y0_ref, sem).wait()
    pltpu.make_async_copy(y_ref.at[pl.ds(0, tm)], y1_ref, sem).wait()
    moe = rt_ref[:, 4:5] * _unpack_halves(y0_ref[...]) + rt_ref[:, 5:6] * _unpack_halves(y1_ref[...])
    out = x1_ref[...] + mod_ref[5:6, :] * moe

    @pl.when(i < n_ctx)
    def _():
        oc_ref[...] = out

    @pl.when(i >= n_ctx)
    def _():
        ol_ref[...] = out


def _combine(l, pos_blocks, x1, route_t, mod, y):
    tm = TM_DISP
    row = lambda w: pl.BlockSpec((tm, w), lambda i: (i, 0))
    return pl.pallas_call(
        _combine_kernel,
        grid=(T // tm,),
        in_specs=[pl.BlockSpec((1, 1, 2 * tm), lambda i: (i, 0, 0), memory_space=pltpu.SMEM),
                  row(D), row(8),
                  pl.BlockSpec((None, None, 6, D), lambda i: (l, _cond_of_tile(i, tm), 0, 0)),
                  pl.BlockSpec(memory_space=pl.ANY)],
        out_specs=_two_stream_specs(tm, D),
        out_shape=[jax.ShapeDtypeStruct((T_CTX, D), f32), jax.ShapeDtypeStruct((T_LAT, D), f32)],
        scratch_shapes=[pltpu.VMEM((tm, MOE_W), jnp.uint32), pltpu.VMEM((tm, MOE_W), jnp.uint32),
                        pltpu.SemaphoreType.DMA(())],
        compiler_params=pltpu.CompilerParams(
            dimension_semantics=("arbitrary",), vmem_limit_bytes=V7X_VMEM_LIMIT),
        name="moe_combine",
    )(pos_blocks, x1, route_t, mod, y)


def _moe_plan(route, counts):
    cnt = counts[:, 0].astype(jnp.int32)
    tiles = (cnt + MOE_TR - 1) // MOE_TR
    tile_end = jnp.cumsum(tiles)
    tile_start = tile_end - tiles
    off = (tile_start * MOE_TR).astype(f32)
    experts = jnp.arange(N_EXP, dtype=f32)[:, None]
    pos = [jnp.sum(jnp.where(route[k][None, :] == experts, off[:, None], 0.0), axis=0) + route[2 + k]
           for k in range(2)]
    pos_blocks = jnp.stack(pos, axis=-1).astype(jnp.int32).reshape(T // TM_DISP, 1, 2 * TM_DISP)
    tid = jnp.arange(MOE_TILES, dtype=jnp.int32)
    te = jnp.minimum(jnp.sum(tid[:, None] >= tile_end[None, :], axis=1), N_EXP - 1).astype(jnp.int32)
    n_tiles = tile_end[-1:].astype(jnp.int32)
    is_last = jnp.any((tid[:, None] == tile_end[None, :] - 1) & (tiles[None, :] > 0), axis=1)
    zero_flags = ((tid >= n_tiles[0]) | is_last).astype(jnp.int32)
    return pos_blocks, te, n_tiles, zero_flags


def _rope_tables():
    pos = np.arange(DEC_SEQ)
    half = HD_A // 4
    freqs = ROPE_BASE ** (-np.arange(half, dtype=np.float64) / half)
    ang_r = (pos // GRID_W)[:, None] * freqs[None, :]
    ang_c = (pos % GRID_W)[:, None] * freqs[None, :]
    ang = np.concatenate([ang_r, ang_r, ang_c, ang_c], axis=1)
    sign = np.concatenate([-np.ones(half), np.ones(half)] * 2)[None, :]
    cos = np.tile(np.cos(ang), (1, 2))
    sin = np.tile(np.sin(ang) * sign, (1, 2))
    ident_c = np.ones((TM_PROJ, LANES))
    ident_s = np.zeros((TM_PROJ, LANES))
    return (jnp.asarray(np.concatenate([cos, ident_c]), f32),
            jnp.asarray(np.concatenate([sin, ident_s]), f32))


def kernel(x_prompt, x_sample, c, cache_k, cache_v, state_ret_fwd, state_ret_bwd, c_ctx,
           norm1_w, norm2_w, w_ada, b_ada, w_in, q_norm_w, k_norm_w, attn_sink,
           ret_decay_fwd, ret_decay_bwd, ret_gn_w, w_pa, w_pb, w_o, w_router, b_router,
           w_exp_gate, w_exp_up, w_exp_down):
    x_ctx, x_lat = x_prompt.reshape(T_CTX, D), x_sample.reshape(T_LAT, D)
    cond8 = jnp.zeros((N_COND, D), f32).at[0].set(c_ctx).at[1:1 + DEC_BATCH].set(c)
    mod = _adaln(cond8, w_ada, b_ada).reshape(DEPTH, N_COND, 6, D)

    cos_t, sin_t = _rope_tables()
    blk = np.arange(Q_A) // HD_A
    ones_blk = jnp.asarray((blk[:, None] == blk[None, :]) / HD_A, bf16)
    tri = jnp.asarray(np.triu(np.ones((TM_POST, TM_POST)), 1), bf16)
    qnw = jnp.tile(q_norm_w, (1, NH_A))
    knw = jnp.tile(k_norm_w, (1, NKV_A))
    wr_pad = jnp.pad(w_router, ((0, 0), (0, LANES - N_EXP)))
    br_col = jnp.broadcast_to(b_router[:, None], (N_EXP, LANES))
    ck = cache_k.reshape(DEC_BATCH, DEPTH, PAST, KV_A)
    cv = cache_v.reshape(DEC_BATCH, DEPTH, PAST, KV_A)
    s0 = jnp.concatenate([state_ret_fwd, state_ret_bwd], axis=3)

    new_k, new_v, new_sf, new_sb = [], [], [], []
    for l in range(DEPTH):
        lg_f = jnp.log1p(-jnp.exp(ret_decay_fwd[l].astype(f32)))
        lg_b = jnp.log1p(-jnp.exp(ret_decay_bwd[l].astype(f32)))
        dec = _decay_tables(lg_f, lg_b)

        qa, ka, va, qr, kr, vr, gr, ga, gb = _inproj(l, x_ctx, x_lat, norm1_w, mod, w_in, cos_t, sin_t,
                                                     ones_blk, qnw, knw)
        oa_ctx = _ctx_attn(l, qa, ka, va, attn_sink)
        oa_lat = _lat_attn(l, qa, ka, va, ck, cv, cos_t, sin_t, attn_sink)
        yr_ctx, sf, sb = _retention(l, qr, kr, vr, gr, None, dec, ret_gn_w, BATCH, SEQ, 0)
        yr_lat, _, _ = _retention(l, qr, kr, vr, gr, s0[:, l], dec, ret_gn_w, DEC_BATCH, DEC_SEQ,
                                  T_CTX // DEC_SEQ)
        x1, h2, route, counts = _post(l, x_ctx, x_lat, oa_ctx, oa_lat, yr_ctx, yr_lat, ga, gb,
                                      w_pa, w_pb, w_o, mod, norm2_w, wr_pad, br_col, tri)
        pos_blocks, tile_expert, n_tiles, zero_flags = _moe_plan(route, counts)
        xs = _dispatch(zero_flags, pos_blocks, h2)
        y = _experts(l, tile_expert, n_tiles, xs, w_exp_gate, w_exp_up, w_exp_down)
        x_ctx, x_lat = _combine(l, pos_blocks, x1, route.T, mod, y)

        new_k.append(ka[:T_CTX].reshape(BATCH, SEQ, NKV_A, HD_A))
        new_v.append(va[:T_CTX].reshape(BATCH, SEQ, NKV_A, HD_A))
        new_sf.append(sf)
        new_sb.append(sb)

    return (x_ctx.reshape(BATCH, SEQ, D), x_lat.reshape(DEC_BATCH, DEC_SEQ, D),
            jnp.stack(new_k, axis=1), jnp.stack(new_v, axis=1),
            jnp.stack(new_sf, axis=1), jnp.stack(new_sb, axis=1))
```

```python
import functools

import numpy as np
import jax
import jax.numpy as jnp
from jax import lax
from jax.experimental import pallas as pl
from jax.experimental.pallas import tpu as pltpu

D = 1024
BATCH, SEQ = 16, 256
DEC_BATCH, DEC_SEQ = 2, 2048
DEPTH = 2
PAST = 512
GRID_W = 64
NH_A, NKV_A, HD_A = 8, 2, 64
WINDOW = 128
NH_R, DK_R, DV_R = 4, 64, 128
CHUNK = 128
N_EXP, N_GROUPS, EXP_PER_GROUP = 16, 4, 4
D_EXP = 512
ROPE_BASE = 10000.0
EPS = 1e-6
NEG_INF = -1e30

Q_A = NH_A * HD_A
KV_A = NKV_A * HD_A
QK_R = NH_R * DK_R
V_R = NH_R * DV_R
C_QA = (0, Q_A)
C_KA = (C_QA[1], C_QA[1] + KV_A)
C_VA = (C_KA[1], C_KA[1] + KV_A)
C_QR = (C_VA[1], C_VA[1] + QK_R)
C_KR = (C_QR[1], C_QR[1] + QK_R)
C_VR = (C_KR[1], C_KR[1] + V_R)
C_GR = (C_VR[1], C_VR[1] + V_R)
C_GA = (C_GR[1], C_GR[1] + D)
C_GB = (C_GA[1], C_GA[1] + D)
D_IN = C_GB[1]

T_CTX = BATCH * SEQ
T_LAT = DEC_BATCH * DEC_SEQ
T = T_CTX + T_LAT
N_COND = 8

LANES = 128
V7X_VMEM_LIMIT = 56 * 1024 * 1024

TM_PROJ = 512
TM_POST = 256
W_CHUNK = 256
TM_DISP = 256
MOE_TR = 256
MOE_TILES = 2 * T // MOE_TR + N_EXP
MOE_ROWS = MOE_TILES * MOE_TR
ATT_QB = 256

f32 = jnp.float32
bf16 = jnp.bfloat16


def _dot(a, b):
    return jnp.dot(a, b, preferred_element_type=f32)


def _dot_t(a, b):
    return lax.dot_general(a, b, (((1,), (1,)), ((), ())), preferred_element_type=f32)


def _split(x):
    hi = x.astype(bf16)
    lo = (x - hi.astype(f32)).astype(bf16)
    return hi, lo


def _dot3(a, b):
    ah, al = _split(a)
    bh, bl = _split(b)
    return _dot(ah, bh) + (_dot(ah, bl) + _dot(al, bh))


def _sigmoid(x):
    return 1.0 / (1.0 + jnp.exp(-x))


def _silu(x):
    return x * _sigmoid(x)


def _cond_of_tile(i, tm):
    n_ctx = T_CTX // tm
    per_b = DEC_SEQ // tm
    return jnp.where(i < n_ctx, 0, 1 + jnp.maximum(i - n_ctx, 0) // per_b)


def _adaln_kernel(cond_ref, w_ref, b_ref, o_ref):
    a = _silu(cond_ref[...])
    o_ref[0] = _dot3(a, w_ref[0]) + b_ref[0]


def _adaln(cond8, w_ada, b_ada):
    tn = 1536
    return pl.pallas_call(
        _adaln_kernel,
        grid=(DEPTH, 6 * D // tn),
        in_specs=[
            pl.BlockSpec((N_COND, D), lambda l, j: (0, 0)),
            pl.BlockSpec((1, D, tn), lambda l, j: (l, 0, j)),
            pl.BlockSpec((1, 1, tn), lambda l, j: (l, 0, j)),
        ],
        out_specs=pl.BlockSpec((1, N_COND, tn), lambda l, j: (l, 0, j)),
        out_shape=jax.ShapeDtypeStruct((DEPTH, N_COND, 6 * D), f32),
        compiler_params=pltpu.CompilerParams(
            dimension_semantics=("parallel", "parallel"), vmem_limit_bytes=V7X_VMEM_LIMIT),
        name="adaln",
    )(cond8, w_ada, b_ada.reshape(DEPTH, 1, 6 * D))


def _rope(x, cos, sin_signed, first_half):
    fwd = pltpu.roll(x, 16, 1)
    bwd = pltpu.roll(x, LANES - 16, 1)
    partner = jnp.where(first_half, bwd, fwd)
    return x * cos + partner * sin_signed


def _head_rms(x, ones_blk, w):
    n = x.shape[1]
    sq_hi, sq_lo = _split(x * x)
    blk = ones_blk[0:n, 0:n]
    mean = _dot(sq_hi, blk) + _dot(sq_lo, blk)
    return x * lax.rsqrt(mean + EPS) * w


def _load_cast(w_hbm, dst_ref, stage_ref, sems):
    k, n = w_hbm.shape

    def chunk_copy(c):
        return pltpu.make_async_copy(w_hbm.at[:, pl.ds(c * W_CHUNK, W_CHUNK)],
                                     stage_ref.at[c % 2, pl.ds(0, k)], sems.at[c % 2])

    n_chunks = n // W_CHUNK
    chunk_copy(0).start()
    for c in range(n_chunks):
        if c + 1 < n_chunks:
            chunk_copy(c + 1).start()
        chunk_copy(c).wait()
        dst_ref[:, c * W_CHUNK:(c + 1) * W_CHUNK] = stage_ref[c % 2, 0:k, :].astype(bf16)


def _tile_of_two(i, n_first, a_ref, b_ref):
    return jnp.where(i < n_first, a_ref[...], b_ref[...])


def _inproj_kernel(l, xc_ref, xl_ref, n1_ref, mod_ref, w_hbm, cos_ref, sin_ref, ones_ref, qnw_ref,
                   knw_ref, qa_ref, ka_ref, va_ref, qr_ref, kr_ref, vr_ref, gr_ref, ga_ref, gb_ref,
                   w_ref, stage_ref, sems):
    i = pl.program_id(0)

    @pl.when(i == 0)
    def _():
        _load_cast(w_hbm.at[l], w_ref, stage_ref, sems)

    x = _tile_of_two(i, T_CTX // xc_ref.shape[0], xc_ref, xl_ref)
    y = x * lax.rsqrt(jnp.mean(x * x, axis=-1, keepdims=True) + EPS) * n1_ref[...]
    h = (y * (1.0 + mod_ref[1:2, :]) + mod_ref[0:1, :]).astype(bf16)

    def proj(c):
        return _dot(h, w_ref[:, c[0]:c[1]])

    cos = cos_ref[...]
    sin = sin_ref[...]
    lane = lax.broadcasted_iota(jnp.int32, cos.shape, 1)
    first_half = (lane % 32) < 16

    def rope_all(v):
        parts = [_rope(v[:, j:j + LANES], cos, sin, first_half) for j in range(0, v.shape[1], LANES)]
        return parts[0] if len(parts) == 1 else jnp.concatenate(parts, axis=1)

    ones_blk = ones_ref[...]
    qa = _head_rms(proj(C_QA), ones_blk, qnw_ref[...])
    qa_ref[...] = rope_all(qa).astype(bf16)
    ka_ref[...] = _head_rms(proj(C_KA), ones_blk, knw_ref[...])
    va_ref[...] = proj(C_VA)
    qr_ref[...] = rope_all(proj(C_QR)).astype(bf16)
    kr_ref[...] = (rope_all(proj(C_KR)) * (DK_R ** -0.5)).astype(bf16)
    vr_ref[...] = proj(C_VR).astype(bf16)
    gr_ref[...] = proj(C_GR).astype(bf16)
    ga_ref[...] = proj(C_GA).astype(bf16)
    gb_ref[...] = proj(C_GB).astype(bf16)


def _two_stream_specs(tm, width):
    n_ctx = T_CTX // tm
    return [pl.BlockSpec((tm, width), lambda i, *_: (jnp.minimum(i, n_ctx - 1), 0)),
            pl.BlockSpec((tm, width), lambda i, *_: (jnp.maximum(i - n_ctx, 0), 0))]


def _inproj(l, x_ctx, x_lat, norm1_w, mod, w_in, cos_t, sin_t, ones_blk, qnw, knw):
    tm = TM_PROJ
    n_ctx = T_CTX // tm
    per_b = DEC_SEQ // tm

    def tab_map(i):
        return (jnp.where(i < n_ctx, per_b, jnp.maximum(i - n_ctx, 0) % per_b), 0)

    row = lambda i: (i, 0)
    const = lambda i: (0, 0)
    widths = [(Q_A, bf16), (KV_A, f32), (KV_A, f32), (QK_R, bf16), (QK_R, bf16), (V_R, bf16),
              (V_R, bf16), (D, bf16), (D, bf16)]
    return pl.pallas_call(
        functools.partial(_inproj_kernel, l),
        grid=(T // tm,),
        in_specs=_two_stream_specs(tm, D) + [
            pl.BlockSpec((1, D), const),
            pl.BlockSpec((None, None, 6, D), lambda i: (l, _cond_of_tile(i, tm), 0, 0)),
            pl.BlockSpec(memory_space=pl.ANY),
            pl.BlockSpec((tm, LANES), tab_map),
            pl.BlockSpec((tm, LANES), tab_map),
            pl.BlockSpec((Q_A, Q_A), const),
            pl.BlockSpec((1, Q_A), const),
            pl.BlockSpec((1, KV_A), const),
        ],
        out_specs=[pl.BlockSpec((tm, w), row) for w, _ in widths],
        out_shape=[jax.ShapeDtypeStruct((T, w), dt) for w, dt in widths],
        scratch_shapes=[pltpu.VMEM((D, D_IN), bf16), pltpu.VMEM((2, D, W_CHUNK), f32),
                        pltpu.SemaphoreType.DMA((2,))],
        compiler_params=pltpu.CompilerParams(
            dimension_semantics=("arbitrary",), vmem_limit_bytes=V7X_VMEM_LIMIT),
        name="inproj",
    )(x_ctx, x_lat, norm1_w[l:l + 1], mod, w_in, cos_t, sin_t, ones_blk, qnw[l:l + 1], knw[l:l + 1])


def _head_blocks(t, kv, lo_mask):
    r = pltpu.roll(t, HD_A, 1)
    if kv == 0:
        a = jnp.where(lo_mask, t, 0.0)
        b = jnp.where(lo_mask, 0.0, r)
    else:
        a = jnp.where(lo_mask, r, 0.0)
        b = jnp.where(lo_mask, 0.0, t)
    return jnp.concatenate([a, b], axis=0).astype(bf16)


def _ctx_attn_kernel(sink_ref, q_ref, k_ref, v_ref, o_ref):
    k = k_ref[...]
    v = v_ref[...]
    n = k.shape[0]
    lo_mask = lax.broadcasted_iota(jnp.int32, k.shape, 1) < HD_A
    scale = HD_A ** -0.5
    for kv in range(NKV_A):
        kblk = _head_blocks(k, kv, lo_mask)
        vblk = _head_blocks(v, kv, lo_mask)
        for pr in range(2):
            pi = kv * 2 + pr
            s = _dot_t(q_ref[:, pi * LANES:(pi + 1) * LANES], kblk) * scale
            ps = []
            for hh in range(2):
                sk = sink_ref[2 * pi + hh]
                sh = s[:, hh * n:(hh + 1) * n]
                m = jnp.maximum(jnp.max(sh, axis=-1, keepdims=True), sk)
                p = jnp.exp(sh - m)
                den = jnp.sum(p, axis=-1, keepdims=True) + jnp.exp(sk - m)
                ps.append((p / den).astype(bf16))
            o = _dot(jnp.concatenate(ps, axis=1), vblk)
            o_ref[:, pi * LANES:(pi + 1) * LANES] = o.astype(bf16)


def _ctx_attn(l, qa, ka, va, sink):
    blk = lambda w: pl.BlockSpec((SEQ, w), lambda b: (b, 0))
    return pl.pallas_call(
        _ctx_attn_kernel,
        grid=(BATCH,),
        in_specs=[pl.BlockSpec(memory_space=pltpu.SMEM), blk(Q_A), blk(KV_A), blk(KV_A)],
        out_specs=blk(Q_A),
        out_shape=jax.ShapeDtypeStruct((T_CTX, Q_A), bf16),
        compiler_params=pltpu.CompilerParams(
            dimension_semantics=("parallel",), vmem_limit_bytes=V7X_VMEM_LIMIT),
        name="ctx_attn",
    )(sink[l], qa, ka, va)


def _lat_attn_kernel(sink_ref, q_ref, k_ref, v_ref, kc_ref, vc_ref, cos_ref, sin_ref, o_ref):
    j = pl.program_id(1)
    qb = ATT_QB
    win = 2 * qb
    ws = pl.multiple_of(jnp.clip(j * qb - WINDOW, 0, DEC_SEQ - win), WINDOW)
    lo_mask = lax.broadcasted_iota(jnp.int32, (win, LANES), 1) < HD_A
    lane = lax.broadcasted_iota(jnp.int32, (win, LANES), 1)
    kw = _rope(k_ref[pl.ds(ws, win), :], cos_ref[pl.ds(ws, win), :], sin_ref[pl.ds(ws, win), :],
               (lane % 32) < 16)
    vw = v_ref[pl.ds(ws, win), :]
    kc = kc_ref[...]
    vc = vc_ref[...]
    qpos = j * qb + (lax.broadcasted_iota(jnp.int32, (2 * qb, win), 0) & (qb - 1))
    kpos = ws + lax.broadcasted_iota(jnp.int32, (2 * qb, win), 1)
    valid = jnp.abs(qpos - kpos) <= WINDOW
    scale = HD_A ** -0.5
    for kv in range(NKV_A):
        kc_blk = _head_blocks(kc, kv, lo_mask[:PAST])
        vc_blk = _head_blocks(vc, kv, lo_mask[:PAST])
        kw_blk = _head_blocks(kw, kv, lo_mask)
        vw_blk = _head_blocks(vw, kv, lo_mask)
        q2 = jnp.concatenate([q_ref[:, (2 * kv) * LANES:(2 * kv + 1) * LANES],
                              q_ref[:, (2 * kv + 1) * LANES:(2 * kv + 2) * LANES]], axis=0)
        s_c = _dot_t(q2, kc_blk) * scale
        s_w = _dot_t(q2, kw_blk) * scale
        pcs, pws = [], []
        for hh in range(2):
            row = lax.broadcasted_iota(jnp.int32, (2 * qb, 1), 0)
            sk = jnp.where(row < qb, sink_ref[4 * kv + hh], sink_ref[4 * kv + 2 + hh])
            sc = s_c[:, hh * PAST:(hh + 1) * PAST]
            sw = jnp.where(valid, s_w[:, hh * win:(hh + 1) * win], NEG_INF)
            m = jnp.maximum(jnp.maximum(jnp.max(sc, axis=-1, keepdims=True),
                                        jnp.max(sw, axis=-1, keepdims=True)), sk)
            pc = jnp.exp(sc - m)
            pw = jnp.exp(sw - m)
            den = (jnp.sum(pc, axis=-1, keepdims=True) + jnp.sum(pw, axis=-1, keepdims=True)
                   + jnp.exp(sk - m))
            inv = 1.0 / den
            pcs.append((pc * inv).astype(bf16))
            pws.append((pw * inv).astype(bf16))
        o = _dot(jnp.concatenate(pcs, axis=1), vc_blk) + _dot(jnp.concatenate(pws, axis=1), vw_blk)
        o_ref[:, (2 * kv) * LANES:(2 * kv + 1) * LANES] = o[:qb].astype(bf16)
        o_ref[:, (2 * kv + 1) * LANES:(2 * kv + 2) * LANES] = o[qb:].astype(bf16)


def _lat_attn(l, qa, ka, va, cache_k, cache_v, cos_l, sin_l, sink):
    qb = ATT_QB
    nq = DEC_SEQ // qb
    ctx_blocks = T_CTX // DEC_SEQ
    seq = lambda b, j: (ctx_blocks + b, 0)
    return pl.pallas_call(
        _lat_attn_kernel,
        grid=(DEC_BATCH, nq),
        in_specs=[
            pl.BlockSpec(memory_space=pltpu.SMEM),
            pl.BlockSpec((qb, Q_A), lambda b, j: (T_CTX // qb + b * nq + j, 0)),
            pl.BlockSpec((DEC_SEQ, KV_A), seq),
            pl.BlockSpec((DEC_SEQ, KV_A), seq),
            pl.BlockSpec((None, None, PAST, KV_A), lambda b, j: (b, l, 0, 0)),
            pl.BlockSpec((None, None, PAST, KV_A), lambda b, j: (b, l, 0, 0)),
            pl.BlockSpec((DEC_SEQ, LANES), lambda b, j: (0, 0)),
            pl.BlockSpec((DEC_SEQ, LANES), lambda b, j: (0, 0)),
        ],
        out_specs=pl.BlockSpec((qb, Q_A), lambda b, j: (b * nq + j, 0)),
        out_shape=jax.ShapeDtypeStruct((T_LAT, Q_A), bf16),
        compiler_params=pltpu.CompilerParams(
            dimension_semantics=("parallel", "parallel"), vmem_limit_bytes=V7X_VMEM_LIMIT),
        name="lat_attn",
    )(sink[l], qa, ka, va, cache_k, cache_v, cos_l, sin_l)


def _dup_heads(pair, lo_mask):
    r = pltpu.roll(pair, DK_R, 1)
    return jnp.where(lo_mask, pair, r), jnp.where(lo_mask, r, pair)


def _retention_kernel(has_s0, n_chunks, *refs):
    if has_s0:
        (q_ref, k_ref, v_ref, g_ref, s0_ref, mask_ref, qdec_ref, kdec_ref, cdec_ref, gnw_ref,
         y_ref, sf_ref, sb_ref, ds_ref, st_ref) = refs
    else:
        (q_ref, k_ref, v_ref, g_ref, mask_ref, qdec_ref, kdec_ref, cdec_ref, gnw_ref,
         y_ref, sf_ref, sb_ref, ds_ref, st_ref) = refs
        s0_ref = None
    C = CHUNK
    lo_mask = lax.broadcasted_iota(jnp.int32, (C, LANES), 1) < DK_R

    def inc_body(c, carry):
        r0 = pl.multiple_of(c * C, C)
        for pr in range(2):
            kp = k_ref[pl.ds(r0, C), pr * LANES:(pr + 1) * LANES].astype(f32)
            for hh, kd in enumerate(_dup_heads(kp, lo_mask)):
                h = 2 * pr + hh
                kd = (kd * kdec_ref[:, h * LANES:(h + 1) * LANES]).astype(bf16)
                vh = v_ref[pl.ds(r0, C), h * DV_R:(h + 1) * DV_R]
                ds_ref[c, h] = lax.dot_general(kd, vh, (((0,), (0,)), ((), ())),
                                               preferred_element_type=f32)
        return carry

    lax.fori_loop(0, n_chunks, inc_body, 0)

    for h in range(NH_R):
        cf = cdec_ref[h, 0:DK_R, :]
        cb = cdec_ref[h, DK_R:2 * DK_R, :]
        if has_s0:
            init_f = s0_ref[h, 0:DK_R, :]
            init_b = s0_ref[h, DK_R:2 * DK_R, :]
        else:
            init_f = jnp.zeros((DK_R, DV_R), f32)
            init_b = init_f

        def fwd_body(c, s, h=h, cf=cf):
            st_ref[c, h, 0:DK_R, :] = s
            return s * cf + ds_ref[c, h, 0:DK_R, :]

        def bwd_body(i, s, h=h, cb=cb):
            c = n_chunks - 1 - i
            st_ref[c, h, DK_R:2 * DK_R, :] = s
            return s * cb + ds_ref[c, h, DK_R:2 * DK_R, :]

        sf_ref[h] = lax.fori_loop(0, n_chunks, fwd_body, init_f)
        sb_ref[h] = lax.fori_loop(0, n_chunks, bwd_body, init_b)

    def out_body(c, carry):
        r0 = pl.multiple_of(c * C, C)
        for pr in range(2):
            qp = q_ref[pl.ds(r0, C), pr * LANES:(pr + 1) * LANES]
            kp = k_ref[pl.ds(r0, C), pr * LANES:(pr + 1) * LANES].astype(f32)
            kblk = jnp.concatenate([jnp.where(lo_mask, kp, 0.0), jnp.where(lo_mask, 0.0, kp)],
                                   axis=0).astype(bf16)
            a2 = _dot_t(qp, kblk)
            for hh, qd in enumerate(_dup_heads(qp.astype(f32), lo_mask)):
                h = 2 * pr + hh
                a = (a2[:, hh * C:(hh + 1) * C] * mask_ref[h]).astype(bf16)
                vh = v_ref[pl.ds(r0, C), h * DV_R:(h + 1) * DV_R]
                qd = (qd * qdec_ref[:, h * LANES:(h + 1) * LANES]).astype(bf16)
                o = _dot(a, vh) + _dot(qd, st_ref[c, h].astype(bf16))
                mu = jnp.mean(o, axis=-1, keepdims=True)
                d = o - mu
                var = jnp.mean(d * d, axis=-1, keepdims=True)
                yh = d * lax.rsqrt(var + EPS) * gnw_ref[:, h * DV_R:(h + 1) * DV_R]
                g = g_ref[pl.ds(r0, C), h * DV_R:(h + 1) * DV_R].astype(f32)
                y_ref[pl.ds(r0, C), h * DV_R:(h + 1) * DV_R] = (yh * _silu(g)).astype(bf16)
        return carry

    lax.fori_loop(0, n_chunks, out_body, 0)


def _retention(l, qr, kr, vr, gr, s0, dec, gnw, nb, seq, row_block0):
    n_chunks = seq // CHUNK
    mask, qdec, kdec, cdec = dec
    has_s0 = s0 is not None
    tok = lambda w: pl.BlockSpec((seq, w), lambda b: (row_block0 + b, 0))
    full = lambda a: pl.BlockSpec(a.shape, lambda b: (0,) * a.ndim)
    in_specs = [tok(QK_R), tok(QK_R), tok(V_R), tok(V_R)]
    args = [qr, kr, vr, gr]
    if has_s0:
        in_specs.append(pl.BlockSpec((None, NH_R, 2 * DK_R, DV_R), lambda b: (b, 0, 0, 0)))
        args.append(s0)
    in_specs += [full(mask), full(qdec), full(kdec), full(cdec),
                 pl.BlockSpec((1, V_R), lambda b: (0, 0))]
    args += [mask, qdec, kdec, cdec, gnw[l:l + 1]]
    st_spec = pl.BlockSpec((None, NH_R, DK_R, DV_R), lambda b: (b, 0, 0, 0))
    return pl.pallas_call(
        functools.partial(_retention_kernel, has_s0, n_chunks),
        grid=(nb,),
        in_specs=in_specs,
        out_specs=[pl.BlockSpec((seq, V_R), lambda b: (b, 0)), st_spec, st_spec],
        out_shape=[jax.ShapeDtypeStruct((nb * seq, V_R), bf16),
                   jax.ShapeDtypeStruct((nb, NH_R, DK_R, DV_R), f32),
                   jax.ShapeDtypeStruct((nb, NH_R, DK_R, DV_R), f32)],
        scratch_shapes=[pltpu.VMEM((n_chunks, NH_R, 2 * DK_R, DV_R), f32),
                        pltpu.VMEM((n_chunks, NH_R, 2 * DK_R, DV_R), f32)],
        compiler_params=pltpu.CompilerParams(
            dimension_semantics=("parallel",), vmem_limit_bytes=V7X_VMEM_LIMIT),
        name="retention_lat" if has_s0 else "retention_ctx",
    )(*args)


def _decay_tables(lg_f, lg_b):
    C = CHUNK
    idx = jnp.arange(C, dtype=f32)
    diff = idx[:, None] - idx[None, :]
    lower = jnp.where(diff >= 0, jnp.exp(jnp.maximum(diff, 0.0)[None] * lg_f[:, None, None]), 0.0)
    upper = jnp.where(diff <= 0, jnp.exp(jnp.maximum(-diff, 0.0)[None] * lg_b[:, None, None]), 0.0)
    mask = lower + upper
    qf = jnp.exp((idx + 1.0)[:, None] * lg_f[None, :])
    qb = jnp.exp((C - idx)[:, None] * lg_b[None, :])
    kf = jnp.exp((C - 1.0 - idx)[:, None] * lg_f[None, :])
    kb = jnp.exp(idx[:, None] * lg_b[None, :])

    def lanes(f, b):
        both = jnp.stack([f, b], axis=-1)
        return jnp.repeat(both, DK_R, axis=-1).reshape(C, NH_R * 2 * DK_R)

    cf = jnp.exp(C * lg_f)
    cb = jnp.exp(C * lg_b)
    cdec = jnp.broadcast_to(jnp.stack([cf, cb], axis=-1)[:, :, None, None],
                            (NH_R, 2, DK_R, DV_R)).reshape(NH_R, 2 * DK_R, DV_R)
    return mask, lanes(qf, qb), lanes(kf, kb), cdec


def _post_kernel(l, xc_ref, xl_ref, oac_ref, oal_ref, yrc_ref, yrl_ref, ga_ref, gb_ref, wpa_hbm, wpb_hbm,
                 wo_hbm, mod_ref, n2_ref, wr_ref, br_ref, tri_ref, x1_ref, h2_ref, route_ref, cnt_ref,
                 carry_ref, wpa_ref, wpb_ref, wo_ref, stage_ref, sems):
    i = pl.program_id(0)
    tm = xc_ref.shape[0]
    n_ctx = T_CTX // tm

    @pl.when(i == 0)
    def _():
        carry_ref[...] = jnp.zeros_like(carry_ref)
        _load_cast(wpa_hbm.at[l], wpa_ref, stage_ref, sems)
        _load_cast(wpb_hbm.at[l], wpb_ref, stage_ref, sems)
        _load_cast(wo_hbm.at[l], wo_ref, stage_ref, sems)

    ga = _sigmoid(ga_ref[...].astype(f32))
    gb = _sigmoid(gb_ref[...].astype(f32))
    oa = _tile_of_two(i, n_ctx, oac_ref, oal_ref)
    yr = _tile_of_two(i, n_ctx, yrc_ref, yrl_ref)
    merged = ga * _dot(oa, wpa_ref[...]) + gb * _dot(yr, wpb_ref[...])
    mix = _dot(merged.astype(bf16), wo_ref[...])
    x1 = _tile_of_two(i, n_ctx, xc_ref, xl_ref) + mod_ref[2:3, :] * mix
    x1_ref[...] = x1
    y = x1 * lax.rsqrt(jnp.mean(x1 * x1, axis=-1, keepdims=True) + EPS) * n2_ref[...]
    h2 = y * (1.0 + mod_ref[4:5, :]) + mod_ref[3:4, :]
    h2_ref[...] = h2

    logits = _dot3(h2, wr_ref[...])
    lt = logits.T[0:N_EXP, :]
    scores = _sigmoid(lt)
    sel = scores + br_ref[:, 0:1]
    row = lax.broadcasted_iota(jnp.int32, (N_EXP, tm), 0)

    best = None
    bg = None
    for g in range(N_GROUPS):
        a, b, c, d = (sel[EXP_PER_GROUP * g + k:EXP_PER_GROUP * g + k + 1, :] for k in range(4))
        p, q = jnp.maximum(a, b), jnp.minimum(a, b)
        r, s = jnp.maximum(c, d), jnp.minimum(c, d)
        gs = jnp.maximum(p, r) + jnp.maximum(jnp.minimum(p, r), jnp.maximum(q, s))
        if g == 0:
            best, bg = gs, jnp.zeros((1, tm), jnp.int32)
        else:
            upd = gs > best
            bg = jnp.where(upd, g, bg)
            best = jnp.where(upd, gs, best)
    masked = jnp.where(jnp.right_shift(row, 2) == bg, sel, NEG_INF)
    m1 = jnp.max(masked, axis=0, keepdims=True)
    i1 = jnp.min(jnp.where(masked == m1, row, N_EXP), axis=0, keepdims=True)
    masked2 = jnp.where(row == i1, NEG_INF, masked)
    m2 = jnp.max(masked2, axis=0, keepdims=True)
    i2 = jnp.min(jnp.where(masked2 == m2, row, N_EXP), axis=0, keepdims=True)
    oh1 = row == i1
    oh2 = row == i2
    s1 = jnp.sum(jnp.where(oh1, scores, 0.0), axis=0, keepdims=True)
    s2 = jnp.sum(jnp.where(oh2, scores, 0.0), axis=0, keepdims=True)
    den = s1 + s2

    oh = jnp.where(oh1 | oh2, 1.0, 0.0)
    tot = carry_ref[:, 0:1] + _dot(oh.astype(bf16), tri_ref[...])
    r1 = jnp.sum(jnp.where(oh1, tot, 0.0), axis=0, keepdims=True)
    r2 = jnp.sum(jnp.where(oh2, tot, 0.0), axis=0, keepdims=True)
    carry_ref[...] = carry_ref[...] + jnp.sum(oh, axis=1, keepdims=True)
    cnt_ref[...] = carry_ref[...]

    route_ref[0:1, :] = i1.astype(f32)
    route_ref[1:2, :] = i2.astype(f32)
    route_ref[2:3, :] = r1
    route_ref[3:4, :] = r2
    route_ref[4:5, :] = s1 / den
    route_ref[5:6, :] = s2 / den
    route_ref[6:8, :] = jnp.zeros((2, tm), f32)


def _post(l, x_ctx, x_lat, oa_ctx, oa_lat, yr_ctx, yr_lat, ga, gb, wpa, wpb, wo, mod, norm2_w, wr_pad,
          br_col, tri):
    tm = TM_POST
    row = lambda w: pl.BlockSpec((tm, w), lambda i: (i, 0))
    const = lambda a: pl.BlockSpec(a.shape, lambda i: (0,) * a.ndim)
    hbm = pl.BlockSpec(memory_space=pl.ANY)
    return pl.pallas_call(
        functools.partial(_post_kernel, l),
        grid=(T // tm,),
        in_specs=(_two_stream_specs(tm, D) + _two_stream_specs(tm, Q_A) + _two_stream_specs(tm, V_R)
                  + [row(D), row(D), hbm, hbm, hbm,
                     pl.BlockSpec((None, None, 6, D), lambda i: (l, _cond_of_tile(i, tm), 0, 0)),
                     pl.BlockSpec((1, D), lambda i: (0, 0)), const(wr_pad), const(br_col), const(tri)]),
        out_specs=[row(D), row(D), pl.BlockSpec((8, tm), lambda i: (0, i)),
                   pl.BlockSpec((N_EXP, LANES), lambda i: (0, 0))],
        out_shape=[jax.ShapeDtypeStruct((T, D), f32), jax.ShapeDtypeStruct((T, D), f32),
                   jax.ShapeDtypeStruct((8, T), f32), jax.ShapeDtypeStruct((N_EXP, LANES), f32)],
        scratch_shapes=[pltpu.VMEM((N_EXP, LANES), f32), pltpu.VMEM((Q_A, D), bf16),
                        pltpu.VMEM((V_R, D), bf16), pltpu.VMEM((D, D), bf16),
                        pltpu.VMEM((2, D, W_CHUNK), f32), pltpu.SemaphoreType.DMA((2,))],
        compiler_params=pltpu.CompilerParams(
            dimension_semantics=("arbitrary",), vmem_limit_bytes=V7X_VMEM_LIMIT),
        name="post_router",
    )(x_ctx, x_lat, oa_ctx, oa_lat, yr_ctx, yr_lat, ga, gb, wpa, wpb, wo, mod, norm2_w[l:l + 1],
      wr_pad, br_col, tri)


def _row_copy(src_ref, src_row, dst_ref, dst_row, sem):
    return pltpu.make_async_copy(src_ref.at[pl.ds(src_row, 1)], dst_ref.at[pl.ds(dst_row, 1)], sem)


def _dispatch_kernel(zf_ref, pos_ref, h_ref, xs_ref, zero_ref, sem):
    tm = h_ref.shape[0]

    @pl.when(pl.program_id(0) == 0)
    def _():
        zero_ref[...] = jnp.zeros_like(zero_ref)

        def tile_copy(t):
            return pltpu.make_async_copy(zero_ref, xs_ref.at[pl.ds(t * MOE_TR, MOE_TR)], sem)

        def start(t, carry):
            @pl.when(zf_ref[t] != 0)
            def _():
                tile_copy(t).start()
            return carry

        def wait(t, carry):
            @pl.when(zf_ref[t] != 0)
            def _():
                tile_copy(t).wait()
            return carry

        lax.fori_loop(0, MOE_TILES, start, 0)
        lax.fori_loop(0, MOE_TILES, wait, 0)

    for r in range(tm):
        for k in range(2):
            _row_copy(h_ref, r, xs_ref, pos_ref[0, 0, 2 * r + k], sem).start(priority=k)
    for _ in range(2):
        pltpu.make_async_copy(h_ref, xs_ref.at[pl.ds(0, tm)], sem).wait()


def _dispatch(zero_flags, pos_blocks, h2p):
    tm = TM_DISP
    return pl.pallas_call(
        _dispatch_kernel,
        grid_spec=pltpu.PrefetchScalarGridSpec(
            num_scalar_prefetch=1,
            grid=(T // tm,),
            in_specs=[pl.BlockSpec((1, 1, 2 * tm), lambda i, zf: (i, 0, 0), memory_space=pltpu.SMEM),
                      pl.BlockSpec((tm, D), lambda i, zf: (i, 0))],
            out_specs=pl.BlockSpec(memory_space=pl.ANY),
            scratch_shapes=[pltpu.VMEM((MOE_TR, D), f32), pltpu.SemaphoreType.DMA(())],
        ),
        out_shape=jax.ShapeDtypeStruct((MOE_ROWS, D), f32),
        compiler_params=pltpu.CompilerParams(
            dimension_semantics=("arbitrary",), vmem_limit_bytes=V7X_VMEM_LIMIT),
        name="moe_dispatch",
    )(zero_flags, pos_blocks, h2p)


def _experts_kernel(l, te_ref, first_ref, slot_ref, nxt_ref, nt_ref, xs_ref, wg_hbm, wu_hbm, wd_hbm,
                    y_ref, wg_ref, wu_ref, wd_ref, sg_ref, su_ref, sd_ref, sems):
    i = pl.program_id(0)
    live = i < nt_ref[0]

    def fetch(e, slot):
        return [pltpu.make_async_copy(w.at[l, e], s.at[slot], sems.at[slot, k])
                for k, (w, s) in enumerate(((wg_hbm, sg_ref), (wu_hbm, su_ref), (wd_hbm, sd_ref)))]

    @pl.when(i == 0)
    def _():
        for c in fetch(te_ref[0], 0):
            c.start()

    @pl.when(live & (first_ref[i] != 0))
    def _():
        slot = slot_ref[i]
        for c in fetch(te_ref[i], slot):
            c.wait()
        wg_ref[...] = sg_ref[slot].astype(bf16)
        wu_ref[...] = su_ref[slot].astype(bf16)
        wd_ref[...] = sd_ref[slot].astype(bf16)

        @pl.when(nxt_ref[i] >= 0)
        def _():
            for c in fetch(nxt_ref[i], 1 - slot):
                c.start()

    @pl.when(live)
    def _():
        x = xs_ref[...].astype(bf16)
        act = _silu(_dot(x, wg_ref[...])) * _dot(x, wu_ref[...])
        y_ref[...] = _dot(act.astype(bf16), wd_ref[...])

    @pl.when(i >= nt_ref[0])
    def _():
        y_ref[...] = jnp.zeros_like(y_ref)


def _experts(l, plan, xs, wg, wu, wd):
    def tile(i, *prefetch):
        return jnp.minimum(i, prefetch[-1][0] - 1)

    hbm = pl.BlockSpec(memory_space=pl.ANY)
    return pl.pallas_call(
        functools.partial(_experts_kernel, l),
        grid_spec=pltpu.PrefetchScalarGridSpec(
            num_scalar_prefetch=5,
            grid=(MOE_TILES,),
            in_specs=[pl.BlockSpec((MOE_TR, D), lambda i, *p: (tile(i, *p), 0)), hbm, hbm, hbm],
            out_specs=pl.BlockSpec((MOE_TR, D), lambda i, *p: (i, 0)),
            scratch_shapes=[pltpu.VMEM((D, D_EXP), bf16), pltpu.VMEM((D, D_EXP), bf16),
                            pltpu.VMEM((D_EXP, D), bf16),
                            pltpu.VMEM((2, D, D_EXP), f32), pltpu.VMEM((2, D, D_EXP), f32),
                            pltpu.VMEM((2, D_EXP, D), f32), pltpu.SemaphoreType.DMA((2, 3))],
        ),
        out_shape=jax.ShapeDtypeStruct((MOE_ROWS, D), f32),
        compiler_params=pltpu.CompilerParams(
            dimension_semantics=("arbitrary",), vmem_limit_bytes=V7X_VMEM_LIMIT),
        name="moe_experts",
    )(*plan, xs, wg, wu, wd)


def _combine_kernel(pos_ref, x1_ref, rt_ref, mod_ref, y_ref, oc_ref, ol_ref, y0_ref, y1_ref, sem):
    tm = x1_ref.shape[0]
    i = pl.program_id(0)
    n_ctx = T_CTX // tm

    for r in range(tm):
        _row_copy(y_ref, pos_ref[0, 0, 2 * r], y0_ref, r, sem).start(priority=0)
        _row_copy(y_ref, pos_ref[0, 0, 2 * r + 1], y1_ref, r, sem).start(priority=1)
    pltpu.make_async_copy(y_ref.at[pl.ds(0, tm)], y0_ref, sem).wait()
    pltpu.make_async_copy(y_ref.at[pl.ds(0, tm)], y1_ref, sem).wait()
    moe = rt_ref[:, 4:5] * y0_ref[...] + rt_ref[:, 5:6] * y1_ref[...]
    out = x1_ref[...] + mod_ref[5:6, :] * moe

    @pl.when(i < n_ctx)
    def _():
        oc_ref[...] = out

    @pl.when(i >= n_ctx)
    def _():
        ol_ref[...] = out


def _combine(l, pos_blocks, x1, route_t, mod, y):
    tm = TM_DISP
    row = lambda w: pl.BlockSpec((tm, w), lambda i: (i, 0))
    return pl.pallas_call(
        _combine_kernel,
        grid=(T // tm,),
        in_specs=[pl.BlockSpec((1, 1, 2 * tm), lambda i: (i, 0, 0), memory_space=pltpu.SMEM),
                  row(D), row(8),
                  pl.BlockSpec((None, None, 6, D), lambda i: (l, _cond_of_tile(i, tm), 0, 0)),
                  pl.BlockSpec(memory_space=pl.ANY)],
        out_specs=_two_stream_specs(tm, D),
        out_shape=[jax.ShapeDtypeStruct((T_CTX, D), f32), jax.ShapeDtypeStruct((T_LAT, D), f32)],
        scratch_shapes=[pltpu.VMEM((tm, D), f32), pltpu.VMEM((tm, D), f32),
                        pltpu.SemaphoreType.DMA(())],
        compiler_params=pltpu.CompilerParams(
            dimension_semantics=("arbitrary",), vmem_limit_bytes=V7X_VMEM_LIMIT),
        name="moe_combine",
    )(pos_blocks, x1, route_t, mod, y)


def _moe_plan(route, counts):
    cnt = counts[:, 0].astype(jnp.int32)
    tiles = (cnt + MOE_TR - 1) // MOE_TR
    tile_end = jnp.cumsum(tiles)
    tile_start = tile_end - tiles
    off = (tile_start * MOE_TR).astype(f32)
    experts = jnp.arange(N_EXP, dtype=f32)[:, None]
    pos = [jnp.sum(jnp.where(route[k][None, :] == experts, off[:, None], 0.0), axis=0) + route[2 + k]
           for k in range(2)]
    pos_blocks = jnp.stack(pos, axis=-1).astype(jnp.int32).reshape(T // TM_DISP, 1, 2 * TM_DISP)
    tid = jnp.arange(MOE_TILES, dtype=jnp.int32)
    te = jnp.minimum(jnp.sum(tid[:, None] >= tile_end[None, :], axis=1), N_EXP - 1).astype(jnp.int32)
    n_tiles = tile_end[-1:].astype(jnp.int32)
    used = tiles > 0
    is_last = jnp.any((tid[:, None] == tile_end[None, :] - 1) & used[None, :], axis=1)
    zero_flags = ((tid >= n_tiles[0]) | is_last).astype(jnp.int32)
    eid = jnp.arange(N_EXP, dtype=jnp.int32)
    of_tile = te[:, None] == eid[None, :]
    first = jnp.any((tid[:, None] == tile_start[None, :]) & used[None, :], axis=1).astype(jnp.int32)
    run_slot = (jnp.cumsum(used.astype(jnp.int32)) - 1) % 2
    later = (eid[None, :] > eid[:, None]) & used[None, :]
    nxt_e = jnp.min(jnp.where(later, eid[None, :], N_EXP), axis=1)
    nxt_e = jnp.where(nxt_e == N_EXP, -1, nxt_e)
    pick = lambda v: jnp.sum(jnp.where(of_tile, v[None, :], 0), axis=1).astype(jnp.int32)
    return pos_blocks, zero_flags, (te, first, pick(run_slot), pick(nxt_e), n_tiles)


def _rope_tables():
    pos = np.arange(DEC_SEQ)
    half = HD_A // 4
    freqs = ROPE_BASE ** (-np.arange(half, dtype=np.float64) / half)
    ang_r = (pos // GRID_W)[:, None] * freqs[None, :]
    ang_c = (pos % GRID_W)[:, None] * freqs[None, :]
    ang = np.concatenate([ang_r, ang_r, ang_c, ang_c], axis=1)
    sign = np.concatenate([-np.ones(half), np.ones(half)] * 2)[None, :]
    cos = np.tile(np.cos(ang), (1, 2))
    sin = np.tile(np.sin(ang) * sign, (1, 2))
    ident_c = np.ones((TM_PROJ, LANES))
    ident_s = np.zeros((TM_PROJ, LANES))
    return (jnp.asarray(np.concatenate([cos, ident_c]), f32),
            jnp.asarray(np.concatenate([sin, ident_s]), f32))


def kernel(x_prompt, x_sample, c, cache_k, cache_v, state_ret_fwd, state_ret_bwd, c_ctx,
           norm1_w, norm2_w, w_ada, b_ada, w_in, q_norm_w, k_norm_w, attn_sink,
           ret_decay_fwd, ret_decay_bwd, ret_gn_w, w_pa, w_pb, w_o, w_router, b_router,
           w_exp_gate, w_exp_up, w_exp_down):
    x_ctx, x_lat = x_prompt.reshape(T_CTX, D), x_sample.reshape(T_LAT, D)
    cond8 = jnp.zeros((N_COND, D), f32).at[0].set(c_ctx).at[1:1 + DEC_BATCH].set(c)
    mod = _adaln(cond8, w_ada, b_ada).reshape(DEPTH, N_COND, 6, D)

    cos_t, sin_t = _rope_tables()
    blk = np.arange(Q_A) // HD_A
    ones_blk = jnp.asarray((blk[:, None] == blk[None, :]) / HD_A, bf16)
    tri = jnp.asarray(np.triu(np.ones((TM_POST, TM_POST)), 1), bf16)
    qnw = jnp.tile(q_norm_w, (1, NH_A))
    knw = jnp.tile(k_norm_w, (1, NKV_A))
    wr_pad = jnp.pad(w_router, ((0, 0), (0, LANES - N_EXP)))
    br_col = jnp.broadcast_to(b_router[:, None], (N_EXP, LANES))
    ck = cache_k.reshape(DEC_BATCH, DEPTH, PAST, KV_A)
    cv = cache_v.reshape(DEC_BATCH, DEPTH, PAST, KV_A)
    s0 = jnp.concatenate([state_ret_fwd, state_ret_bwd], axis=3)

    new_k, new_v, new_sf, new_sb = [], [], [], []
    for l in range(DEPTH):
        lg_f = jnp.log1p(-jnp.exp(ret_decay_fwd[l].astype(f32)))
        lg_b = jnp.log1p(-jnp.exp(ret_decay_bwd[l].astype(f32)))
        dec = _decay_tables(lg_f, lg_b)

        qa, ka, va, qr, kr, vr, gr, ga, gb = _inproj(l, x_ctx, x_lat, norm1_w, mod, w_in, cos_t, sin_t,
                                                     ones_blk, qnw, knw)
        oa_ctx = _ctx_attn(l, qa, ka, va, attn_sink)
        oa_lat = _lat_attn(l, qa, ka, va, ck, cv, cos_t, sin_t, attn_sink)
        yr_ctx, sf, sb = _retention(l, qr, kr, vr, gr, None, dec, ret_gn_w, BATCH, SEQ, 0)
        yr_lat, _, _ = _retention(l, qr, kr, vr, gr, s0[:, l], dec, ret_gn_w, DEC_BATCH, DEC_SEQ,
                                  T_CTX // DEC_SEQ)
        x1, h2, route, counts = _post(l, x_ctx, x_lat, oa_ctx, oa_lat, yr_ctx, yr_lat, ga, gb,
                                      w_pa, w_pb, w_o, mod, norm2_w, wr_pad, br_col, tri)
        pos_blocks, zero_flags, plan = _moe_plan(route, counts)
        xs = _dispatch(zero_flags, pos_blocks, h2)
        y = _experts(l, plan, xs, w_exp_gate, w_exp_up, w_exp_down)
        x_ctx, x_lat = _combine(l, pos_blocks, x1, route.T, mod, y)

        new_k.append(ka[:T_CTX].reshape(BATCH, SEQ, NKV_A, HD_A))
        new_v.append(va[:T_CTX].reshape(BATCH, SEQ, NKV_A, HD_A))
        new_sf.append(sf)
        new_sb.append(sb)

    return (x_ctx.reshape(BATCH, SEQ, D), x_lat.reshape(DEC_BATCH, DEC_SEQ, D),
            jnp.stack(new_k, axis=1), jnp.stack(new_v, axis=1),
            jnp.stack(new_sf, axis=1), jnp.stack(new_sb, axis=1))
```

```python
import functools

import numpy as np
import jax
import jax.numpy as jnp
from jax import lax
from jax.experimental import pallas as pl
from jax.experimental.pallas import tpu as pltpu

D = 1024
BATCH, SEQ = 16, 256
DEC_BATCH, DEC_SEQ = 2, 2048
DEPTH = 2
PAST = 512
GRID_W = 64
NH_A, NKV_A, HD_A = 8, 2, 64
WINDOW = 128
NH_R, DK_R, DV_R = 4, 64, 128
CHUNK = 128
N_EXP, N_GROUPS, EXP_PER_GROUP = 16, 4, 4
D_EXP = 512
ROPE_BASE = 10000.0
EPS = 1e-6
NEG_INF = -1e30

Q_A = NH_A * HD_A
KV_A = NKV_A * HD_A
QK_R = NH_R * DK_R
V_R = NH_R * DV_R
C_QA = (0, Q_A)
C_KA = (C_QA[1], C_QA[1] + KV_A)
C_VA = (C_KA[1], C_KA[1] + KV_A)
C_QR = (C_VA[1], C_VA[1] + QK_R)
C_KR = (C_QR[1], C_QR[1] + QK_R)
C_VR = (C_KR[1], C_KR[1] + V_R)
C_GR = (C_VR[1], C_VR[1] + V_R)
C_GA = (C_GR[1], C_GR[1] + D)
C_GB = (C_GA[1], C_GA[1] + D)
D_IN = C_GB[1]

T_CTX = BATCH * SEQ
T_LAT = DEC_BATCH * DEC_SEQ
T = T_CTX + T_LAT
N_COND = 8

LANES = 128
V7X_VMEM_LIMIT = 56 * 1024 * 1024

TM_PROJ = 512
TM_POST = 256
W_CHUNK = 256
TM_COMB = 512
MOE_TR = 256
MOE_TILES = 2 * T // MOE_TR + N_EXP
MOE_ROWS = MOE_TILES * MOE_TR
ATT_QB = 256

f32 = jnp.float32
bf16 = jnp.bfloat16


def _dot(a, b):
    return jnp.dot(a, b, preferred_element_type=f32)


def _dot_t(a, b):
    return lax.dot_general(a, b, (((1,), (1,)), ((), ())), preferred_element_type=f32)


def _split(x):
    hi = x.astype(bf16)
    lo = (x - hi.astype(f32)).astype(bf16)
    return hi, lo


def _dot3(a, b):
    ah, al = _split(a)
    bh, bl = _split(b)
    return _dot(ah, bh) + (_dot(ah, bl) + _dot(al, bh))


def _sigmoid(x):
    return 1.0 / (1.0 + jnp.exp(-x))


def _silu(x):
    return x * _sigmoid(x)


def _cond_of_tile(i, tm):
    n_ctx = T_CTX // tm
    per_b = DEC_SEQ // tm
    return jnp.where(i < n_ctx, 0, 1 + jnp.maximum(i - n_ctx, 0) // per_b)


def _adaln_kernel(cond_ref, w_ref, b_ref, o_ref):
    a = _silu(cond_ref[...])
    o_ref[0] = _dot3(a, w_ref[0]) + b_ref[0]


def _adaln(cond8, w_ada, b_ada):
    tn = 1536
    return pl.pallas_call(
        _adaln_kernel,
        grid=(DEPTH, 6 * D // tn),
        in_specs=[
            pl.BlockSpec((N_COND, D), lambda l, j: (0, 0)),
            pl.BlockSpec((1, D, tn), lambda l, j: (l, 0, j)),
            pl.BlockSpec((1, 1, tn), lambda l, j: (l, 0, j)),
        ],
        out_specs=pl.BlockSpec((1, N_COND, tn), lambda l, j: (l, 0, j)),
        out_shape=jax.ShapeDtypeStruct((DEPTH, N_COND, 6 * D), f32),
        compiler_params=pltpu.CompilerParams(
            dimension_semantics=("parallel", "parallel"), vmem_limit_bytes=V7X_VMEM_LIMIT),
        name="adaln",
    )(cond8, w_ada, b_ada.reshape(DEPTH, 1, 6 * D))


def _rope(x, cos, sin_signed, first_half):
    fwd = pltpu.roll(x, 16, 1)
    bwd = pltpu.roll(x, LANES - 16, 1)
    partner = jnp.where(first_half, bwd, fwd)
    return x * cos + partner * sin_signed


def _head_rms(x, ones_blk, w):
    n = x.shape[1]
    sq_hi, sq_lo = _split(x * x)
    blk = ones_blk[0:n, 0:n]
    mean = _dot(sq_hi, blk) + _dot(sq_lo, blk)
    return x * lax.rsqrt(mean + EPS) * w


def _load_cast(w_hbm, dst_ref, stage_ref, sems):
    k, n = w_hbm.shape

    def chunk_copy(c):
        return pltpu.make_async_copy(w_hbm.at[:, pl.ds(c * W_CHUNK, W_CHUNK)],
                                     stage_ref.at[c % 2, pl.ds(0, k)], sems.at[c % 2])

    n_chunks = n // W_CHUNK
    chunk_copy(0).start()
    for c in range(n_chunks):
        if c + 1 < n_chunks:
            chunk_copy(c + 1).start()
        chunk_copy(c).wait()
        dst_ref[:, c * W_CHUNK:(c + 1) * W_CHUNK] = stage_ref[c % 2, 0:k, :].astype(bf16)


def _tile_of_two(i, n_first, a_ref, b_ref):
    return jnp.where(i < n_first, a_ref[...], b_ref[...])


def _inproj_kernel(l, xc_ref, xl_ref, n1_ref, mod_ref, w_hbm, cos_ref, sin_ref, ones_ref, qnw_ref,
                   knw_ref, qa_ref, ka_ref, va_ref, qr_ref, kr_ref, vr_ref, gr_ref, ga_ref, gb_ref,
                   w_ref, stage_ref, sems):
    i = pl.program_id(0)

    @pl.when(i == 0)
    def _():
        _load_cast(w_hbm.at[l], w_ref, stage_ref, sems)

    x = _tile_of_two(i, T_CTX // xc_ref.shape[0], xc_ref, xl_ref)
    y = x * lax.rsqrt(jnp.mean(x * x, axis=-1, keepdims=True) + EPS) * n1_ref[...]
    h = (y * (1.0 + mod_ref[1:2, :]) + mod_ref[0:1, :]).astype(bf16)

    def proj(c):
        return _dot(h, w_ref[:, c[0]:c[1]])

    cos = cos_ref[...]
    sin = sin_ref[...]
    lane = lax.broadcasted_iota(jnp.int32, cos.shape, 1)
    first_half = (lane % 32) < 16

    def rope_all(v):
        parts = [_rope(v[:, j:j + LANES], cos, sin, first_half) for j in range(0, v.shape[1], LANES)]
        return parts[0] if len(parts) == 1 else jnp.concatenate(parts, axis=1)

    ones_blk = ones_ref[...]
    qa = _head_rms(proj(C_QA), ones_blk, qnw_ref[...])
    qa_ref[...] = rope_all(qa).astype(bf16)
    ka_ref[...] = _head_rms(proj(C_KA), ones_blk, knw_ref[...])
    va_ref[...] = proj(C_VA)
    qr_ref[...] = rope_all(proj(C_QR)).astype(bf16)
    kr_ref[...] = (rope_all(proj(C_KR)) * (DK_R ** -0.5)).astype(bf16)
    vr_ref[...] = proj(C_VR).astype(bf16)
    gr_ref[...] = proj(C_GR).astype(bf16)
    ga_ref[...] = proj(C_GA).astype(bf16)
    gb_ref[...] = proj(C_GB).astype(bf16)


def _two_stream_specs(tm, width):
    n_ctx = T_CTX // tm
    return [pl.BlockSpec((tm, width), lambda i, *_: (jnp.minimum(i, n_ctx - 1), 0)),
            pl.BlockSpec((tm, width), lambda i, *_: (jnp.maximum(i - n_ctx, 0), 0))]


def _inproj(l, x_ctx, x_lat, norm1_w, mod, w_in, cos_t, sin_t, ones_blk, qnw, knw):
    tm = TM_PROJ
    n_ctx = T_CTX // tm
    per_b = DEC_SEQ // tm

    def tab_map(i):
        return (jnp.where(i < n_ctx, per_b, jnp.maximum(i - n_ctx, 0) % per_b), 0)

    row = lambda i: (i, 0)
    const = lambda i: (0, 0)
    widths = [(Q_A, bf16), (KV_A, f32), (KV_A, f32), (QK_R, bf16), (QK_R, bf16), (V_R, bf16),
              (V_R, bf16), (D, bf16), (D, bf16)]
    return pl.pallas_call(
        functools.partial(_inproj_kernel, l),
        grid=(T // tm,),
        in_specs=_two_stream_specs(tm, D) + [
            pl.BlockSpec((1, D), const),
            pl.BlockSpec((None, None, 6, D), lambda i: (l, _cond_of_tile(i, tm), 0, 0)),
            pl.BlockSpec(memory_space=pl.ANY),
            pl.BlockSpec((tm, LANES), tab_map),
            pl.BlockSpec((tm, LANES), tab_map),
            pl.BlockSpec((Q_A, Q_A), const),
            pl.BlockSpec((1, Q_A), const),
            pl.BlockSpec((1, KV_A), const),
        ],
        out_specs=[pl.BlockSpec((tm, w), row) for w, _ in widths],
        out_shape=[jax.ShapeDtypeStruct((T, w), dt) for w, dt in widths],
        scratch_shapes=[pltpu.VMEM((D, D_IN), bf16), pltpu.VMEM((2, D, W_CHUNK), f32),
                        pltpu.SemaphoreType.DMA((2,))],
        compiler_params=pltpu.CompilerParams(
            dimension_semantics=("arbitrary",), vmem_limit_bytes=V7X_VMEM_LIMIT),
        name="inproj",
    )(x_ctx, x_lat, norm1_w[l:l + 1], mod, w_in, cos_t, sin_t, ones_blk, qnw[l:l + 1], knw[l:l + 1])


def _head_blocks(t, kv, lo_mask):
    r = pltpu.roll(t, HD_A, 1)
    if kv == 0:
        a = jnp.where(lo_mask, t, 0.0)
        b = jnp.where(lo_mask, 0.0, r)
    else:
        a = jnp.where(lo_mask, r, 0.0)
        b = jnp.where(lo_mask, 0.0, t)
    return jnp.concatenate([a, b], axis=0).astype(bf16)


def _ctx_attn_kernel(sink_ref, q_ref, k_ref, v_ref, o_ref):
    k = k_ref[...]
    v = v_ref[...]
    n = k.shape[0]
    lo_mask = lax.broadcasted_iota(jnp.int32, k.shape, 1) < HD_A
    scale = HD_A ** -0.5
    for kv in range(NKV_A):
        kblk = _head_blocks(k, kv, lo_mask)
        vblk = _head_blocks(v, kv, lo_mask)
        for pr in range(2):
            pi = kv * 2 + pr
            s = _dot_t(q_ref[:, pi * LANES:(pi + 1) * LANES], kblk) * scale
            ps = []
            for hh in range(2):
                sk = sink_ref[2 * pi + hh]
                sh = s[:, hh * n:(hh + 1) * n]
                m = jnp.maximum(jnp.max(sh, axis=-1, keepdims=True), sk)
                p = jnp.exp(sh - m)
                den = jnp.sum(p, axis=-1, keepdims=True) + jnp.exp(sk - m)
                ps.append((p / den).astype(bf16))
            o = _dot(jnp.concatenate(ps, axis=1), vblk)
            o_ref[:, pi * LANES:(pi + 1) * LANES] = o.astype(bf16)


def _ctx_attn(l, qa, ka, va, sink):
    blk = lambda w: pl.BlockSpec((SEQ, w), lambda b: (b, 0))
    return pl.pallas_call(
        _ctx_attn_kernel,
        grid=(BATCH,),
        in_specs=[pl.BlockSpec(memory_space=pltpu.SMEM), blk(Q_A), blk(KV_A), blk(KV_A)],
        out_specs=blk(Q_A),
        out_shape=jax.ShapeDtypeStruct((T_CTX, Q_A), bf16),
        compiler_params=pltpu.CompilerParams(
            dimension_semantics=("parallel",), vmem_limit_bytes=V7X_VMEM_LIMIT),
        name="ctx_attn",
    )(sink[l], qa, ka, va)


def _lat_attn_kernel(sink_ref, q_ref, k_ref, v_ref, kc_ref, vc_ref, cos_ref, sin_ref, o_ref):
    j = pl.program_id(1)
    qb = ATT_QB
    win = 2 * qb
    ws = pl.multiple_of(jnp.clip(j * qb - WINDOW, 0, DEC_SEQ - win), WINDOW)
    lo_mask = lax.broadcasted_iota(jnp.int32, (win, LANES), 1) < HD_A
    lane = lax.broadcasted_iota(jnp.int32, (win, LANES), 1)
    kw = _rope(k_ref[pl.ds(ws, win), :], cos_ref[pl.ds(ws, win), :], sin_ref[pl.ds(ws, win), :],
               (lane % 32) < 16)
    vw = v_ref[pl.ds(ws, win), :]
    kc = kc_ref[...]
    vc = vc_ref[...]
    qpos = j * qb + (lax.broadcasted_iota(jnp.int32, (2 * qb, win), 0) & (qb - 1))
    kpos = ws + lax.broadcasted_iota(jnp.int32, (2 * qb, win), 1)
    valid = jnp.abs(qpos - kpos) <= WINDOW
    scale = HD_A ** -0.5
    for kv in range(NKV_A):
        kc_blk = _head_blocks(kc, kv, lo_mask[:PAST])
        vc_blk = _head_blocks(vc, kv, lo_mask[:PAST])
        kw_blk = _head_blocks(kw, kv, lo_mask)
        vw_blk = _head_blocks(vw, kv, lo_mask)
        q2 = jnp.concatenate([q_ref[:, (2 * kv) * LANES:(2 * kv + 1) * LANES],
                              q_ref[:, (2 * kv + 1) * LANES:(2 * kv + 2) * LANES]], axis=0)
        s_c = _dot_t(q2, kc_blk) * scale
        s_w = _dot_t(q2, kw_blk) * scale
        pcs, pws = [], []
        for hh in range(2):
            row = lax.broadcasted_iota(jnp.int32, (2 * qb, 1), 0)
            sk = jnp.where(row < qb, sink_ref[4 * kv + hh], sink_ref[4 * kv + 2 + hh])
            sc = s_c[:, hh * PAST:(hh + 1) * PAST]
            sw = jnp.where(valid, s_w[:, hh * win:(hh + 1) * win], NEG_INF)
            m = jnp.maximum(jnp.maximum(jnp.max(sc, axis=-1, keepdims=True),
                                        jnp.max(sw, axis=-1, keepdims=True)), sk)
            pc = jnp.exp(sc - m)
            pw = jnp.exp(sw - m)
            den = (jnp.sum(pc, axis=-1, keepdims=True) + jnp.sum(pw, axis=-1, keepdims=True)
                   + jnp.exp(sk - m))
            inv = 1.0 / den
            pcs.append((pc * inv).astype(bf16))
            pws.append((pw * inv).astype(bf16))
        o = _dot(jnp.concatenate(pcs, axis=1), vc_blk) + _dot(jnp.concatenate(pws, axis=1), vw_blk)
        o_ref[:, (2 * kv) * LANES:(2 * kv + 1) * LANES] = o[:qb].astype(bf16)
        o_ref[:, (2 * kv + 1) * LANES:(2 * kv + 2) * LANES] = o[qb:].astype(bf16)


def _lat_attn(l, qa, ka, va, cache_k, cache_v, cos_l, sin_l, sink):
    qb = ATT_QB
    nq = DEC_SEQ // qb
    ctx_blocks = T_CTX // DEC_SEQ
    seq = lambda b, j: (ctx_blocks + b, 0)
    return pl.pallas_call(
        _lat_attn_kernel,
        grid=(DEC_BATCH, nq),
        in_specs=[
            pl.BlockSpec(memory_space=pltpu.SMEM),
            pl.BlockSpec((qb, Q_A), lambda b, j: (T_CTX // qb + b * nq + j, 0)),
            pl.BlockSpec((DEC_SEQ, KV_A), seq),
            pl.BlockSpec((DEC_SEQ, KV_A), seq),
            pl.BlockSpec((None, None, PAST, KV_A), lambda b, j: (b, l, 0, 0)),
            pl.BlockSpec((None, None, PAST, KV_A), lambda b, j: (b, l, 0, 0)),
            pl.BlockSpec((DEC_SEQ, LANES), lambda b, j: (0, 0)),
            pl.BlockSpec((DEC_SEQ, LANES), lambda b, j: (0, 0)),
        ],
        out_specs=pl.BlockSpec((qb, Q_A), lambda b, j: (b * nq + j, 0)),
        out_shape=jax.ShapeDtypeStruct((T_LAT, Q_A), bf16),
        compiler_params=pltpu.CompilerParams(
            dimension_semantics=("parallel", "parallel"), vmem_limit_bytes=V7X_VMEM_LIMIT),
        name="lat_attn",
    )(sink[l], qa, ka, va, cache_k, cache_v, cos_l, sin_l)


def _dup_heads(pair, lo_mask):
    r = pltpu.roll(pair, DK_R, 1)
    return jnp.where(lo_mask, pair, r), jnp.where(lo_mask, r, pair)


def _retention_kernel(has_s0, n_chunks, *refs):
    if has_s0:
        (q_ref, k_ref, v_ref, g_ref, s0_ref, mask_ref, qdec_ref, kdec_ref, cdec_ref, gnw_ref,
         y_ref, sf_ref, sb_ref, ds_ref, st_ref) = refs
    else:
        (q_ref, k_ref, v_ref, g_ref, mask_ref, qdec_ref, kdec_ref, cdec_ref, gnw_ref,
         y_ref, sf_ref, sb_ref, ds_ref, st_ref) = refs
        s0_ref = None
    C = CHUNK
    lo_mask = lax.broadcasted_iota(jnp.int32, (C, LANES), 1) < DK_R

    def inc_body(c, carry):
        r0 = pl.multiple_of(c * C, C)
        for pr in range(2):
            kp = k_ref[pl.ds(r0, C), pr * LANES:(pr + 1) * LANES].astype(f32)
            for hh, kd in enumerate(_dup_heads(kp, lo_mask)):
                h = 2 * pr + hh
                kd = (kd * kdec_ref[:, h * LANES:(h + 1) * LANES]).astype(bf16)
                vh = v_ref[pl.ds(r0, C), h * DV_R:(h + 1) * DV_R]
                ds_ref[c, h] = lax.dot_general(kd, vh, (((0,), (0,)), ((), ())),
                                               preferred_element_type=f32)
        return carry

    lax.fori_loop(0, n_chunks, inc_body, 0)

    for h in range(NH_R):
        cf = cdec_ref[h, 0:DK_R, :]
        cb = cdec_ref[h, DK_R:2 * DK_R, :]
        if has_s0:
            init_f = s0_ref[h, 0:DK_R, :]
            init_b = s0_ref[h, DK_R:2 * DK_R, :]
        else:
            init_f = jnp.zeros((DK_R, DV_R), f32)
            init_b = init_f

        def fwd_body(c, s, h=h, cf=cf):
            st_ref[c, h, 0:DK_R, :] = s
            return s * cf + ds_ref[c, h, 0:DK_R, :]

        def bwd_body(i, s, h=h, cb=cb):
            c = n_chunks - 1 - i
            st_ref[c, h, DK_R:2 * DK_R, :] = s
            return s * cb + ds_ref[c, h, DK_R:2 * DK_R, :]

        sf_ref[h] = lax.fori_loop(0, n_chunks, fwd_body, init_f)
        sb_ref[h] = lax.fori_loop(0, n_chunks, bwd_body, init_b)

    def out_body(c, carry):
        r0 = pl.multiple_of(c * C, C)
        for pr in range(2):
            qp = q_ref[pl.ds(r0, C), pr * LANES:(pr + 1) * LANES]
            kp = k_ref[pl.ds(r0, C), pr * LANES:(pr + 1) * LANES].astype(f32)
            kblk = jnp.concatenate([jnp.where(lo_mask, kp, 0.0), jnp.where(lo_mask, 0.0, kp)],
                                   axis=0).astype(bf16)
            a2 = _dot_t(qp, kblk)
            for hh, qd in enumerate(_dup_heads(qp.astype(f32), lo_mask)):
                h = 2 * pr + hh
                a = (a2[:, hh * C:(hh + 1) * C] * mask_ref[h]).astype(bf16)
                vh = v_ref[pl.ds(r0, C), h * DV_R:(h + 1) * DV_R]
                qd = (qd * qdec_ref[:, h * LANES:(h + 1) * LANES]).astype(bf16)
                o = _dot(a, vh) + _dot(qd, st_ref[c, h].astype(bf16))
                mu = jnp.mean(o, axis=-1, keepdims=True)
                d = o - mu
                var = jnp.mean(d * d, axis=-1, keepdims=True)
                yh = d * lax.rsqrt(var + EPS) * gnw_ref[:, h * DV_R:(h + 1) * DV_R]
                g = g_ref[pl.ds(r0, C), h * DV_R:(h + 1) * DV_R].astype(f32)
                y_ref[pl.ds(r0, C), h * DV_R:(h + 1) * DV_R] = (yh * _silu(g)).astype(bf16)
        return carry

    lax.fori_loop(0, n_chunks, out_body, 0)


def _retention(l, qr, kr, vr, gr, s0, dec, gnw, nb, seq, row_block0):
    n_chunks = seq // CHUNK
    mask, qdec, kdec, cdec = dec
    has_s0 = s0 is not None
    tok = lambda w: pl.BlockSpec((seq, w), lambda b: (row_block0 + b, 0))
    full = lambda a: pl.BlockSpec(a.shape, lambda b: (0,) * a.ndim)
    in_specs = [tok(QK_R), tok(QK_R), tok(V_R), tok(V_R)]
    args = [qr, kr, vr, gr]
    if has_s0:
        in_specs.append(pl.BlockSpec((None, NH_R, 2 * DK_R, DV_R), lambda b: (b, 0, 0, 0)))
        args.append(s0)
    in_specs += [full(mask), full(qdec), full(kdec), full(cdec),
                 pl.BlockSpec((1, V_R), lambda b: (0, 0))]
    args += [mask, qdec, kdec, cdec, gnw[l:l + 1]]
    st_spec = pl.BlockSpec((None, NH_R, DK_R, DV_R), lambda b: (b, 0, 0, 0))
    return pl.pallas_call(
        functools.partial(_retention_kernel, has_s0, n_chunks),
        grid=(nb,),
        in_specs=in_specs,
        out_specs=[pl.BlockSpec((seq, V_R), lambda b: (b, 0)), st_spec, st_spec],
        out_shape=[jax.ShapeDtypeStruct((nb * seq, V_R), bf16),
                   jax.ShapeDtypeStruct((nb, NH_R, DK_R, DV_R), f32),
                   jax.ShapeDtypeStruct((nb, NH_R, DK_R, DV_R), f32)],
        scratch_shapes=[pltpu.VMEM((n_chunks, NH_R, 2 * DK_R, DV_R), f32),
                        pltpu.VMEM((n_chunks, NH_R, 2 * DK_R, DV_R), f32)],
        compiler_params=pltpu.CompilerParams(
            dimension_semantics=("parallel",), vmem_limit_bytes=V7X_VMEM_LIMIT),
        name="retention_lat" if has_s0 else "retention_ctx",
    )(*args)


def _decay_tables(lg_f, lg_b):
    C = CHUNK
    idx = jnp.arange(C, dtype=f32)
    diff = idx[:, None] - idx[None, :]
    lower = jnp.where(diff >= 0, jnp.exp(jnp.maximum(diff, 0.0)[None] * lg_f[:, None, None]), 0.0)
    upper = jnp.where(diff <= 0, jnp.exp(jnp.maximum(-diff, 0.0)[None] * lg_b[:, None, None]), 0.0)
    mask = lower + upper
    qf = jnp.exp((idx + 1.0)[:, None] * lg_f[None, :])
    qb = jnp.exp((C - idx)[:, None] * lg_b[None, :])
    kf = jnp.exp((C - 1.0 - idx)[:, None] * lg_f[None, :])
    kb = jnp.exp(idx[:, None] * lg_b[None, :])

    def lanes(f, b):
        both = jnp.stack([f, b], axis=-1)
        return jnp.repeat(both, DK_R, axis=-1).reshape(C, NH_R * 2 * DK_R)

    cf = jnp.exp(C * lg_f)
    cb = jnp.exp(C * lg_b)
    cdec = jnp.broadcast_to(jnp.stack([cf, cb], axis=-1)[:, :, None, None],
                            (NH_R, 2, DK_R, DV_R)).reshape(NH_R, 2 * DK_R, DV_R)
    return mask, lanes(qf, qb), lanes(kf, kb), cdec


def _post_kernel(l, xc_ref, xl_ref, oac_ref, oal_ref, yrc_ref, yrl_ref, ga_ref, gb_ref, wpa_hbm, wpb_hbm,
                 wo_hbm, mod_ref, n2_ref, wr_ref, br_ref, tri_ref, x1_ref, h2_ref, route_ref, cnt_ref,
                 carry_ref, wpa_ref, wpb_ref, wo_ref, stage_ref, sems):
    i = pl.program_id(0)
    tm = xc_ref.shape[0]
    n_ctx = T_CTX // tm

    @pl.when(i == 0)
    def _():
        carry_ref[...] = jnp.zeros_like(carry_ref)
        _load_cast(wpa_hbm.at[l], wpa_ref, stage_ref, sems)
        _load_cast(wpb_hbm.at[l], wpb_ref, stage_ref, sems)
        _load_cast(wo_hbm.at[l], wo_ref, stage_ref, sems)

    ga = _sigmoid(ga_ref[...].astype(f32))
    gb = _sigmoid(gb_ref[...].astype(f32))
    oa = _tile_of_two(i, n_ctx, oac_ref, oal_ref)
    yr = _tile_of_two(i, n_ctx, yrc_ref, yrl_ref)
    merged = ga * _dot(oa, wpa_ref[...]) + gb * _dot(yr, wpb_ref[...])
    mix = _dot(merged.astype(bf16), wo_ref[...])
    x1 = _tile_of_two(i, n_ctx, xc_ref, xl_ref) + mod_ref[2:3, :] * mix
    x1_ref[...] = x1
    y = x1 * lax.rsqrt(jnp.mean(x1 * x1, axis=-1, keepdims=True) + EPS) * n2_ref[...]
    h2 = y * (1.0 + mod_ref[4:5, :]) + mod_ref[3:4, :]
    h2_ref[...] = h2

    logits = _dot3(h2, wr_ref[...])
    lt = logits.T[0:N_EXP, :]
    scores = _sigmoid(lt)
    sel = scores + br_ref[:, 0:1]
    row = lax.broadcasted_iota(jnp.int32, (N_EXP, tm), 0)

    best = None
    bg = None
    for g in range(N_GROUPS):
        a, b, c, d = (sel[EXP_PER_GROUP * g + k:EXP_PER_GROUP * g + k + 1, :] for k in range(4))
        p, q = jnp.maximum(a, b), jnp.minimum(a, b)
        r, s = jnp.maximum(c, d), jnp.minimum(c, d)
        gs = jnp.maximum(p, r) + jnp.maximum(jnp.minimum(p, r), jnp.maximum(q, s))
        if g == 0:
            best, bg = gs, jnp.zeros((1, tm), jnp.int32)
        else:
            upd = gs > best
            bg = jnp.where(upd, g, bg)
            best = jnp.where(upd, gs, best)
    masked = jnp.where(jnp.right_shift(row, 2) == bg, sel, NEG_INF)
    m1 = jnp.max(masked, axis=0, keepdims=True)
    i1 = jnp.min(jnp.where(masked == m1, row, N_EXP), axis=0, keepdims=True)
    masked2 = jnp.where(row == i1, NEG_INF, masked)
    m2 = jnp.max(masked2, axis=0, keepdims=True)
    i2 = jnp.min(jnp.where(masked2 == m2, row, N_EXP), axis=0, keepdims=True)
    oh1 = row == i1
    oh2 = row == i2
    s1 = jnp.sum(jnp.where(oh1, scores, 0.0), axis=0, keepdims=True)
    s2 = jnp.sum(jnp.where(oh2, scores, 0.0), axis=0, keepdims=True)
    den = s1 + s2

    oh = jnp.where(oh1 | oh2, 1.0, 0.0)
    tot = carry_ref[:, 0:1] + _dot(oh.astype(bf16), tri_ref[...])
    r1 = jnp.sum(jnp.where(oh1, tot, 0.0), axis=0, keepdims=True)
    r2 = jnp.sum(jnp.where(oh2, tot, 0.0), axis=0, keepdims=True)
    carry_ref[...] = carry_ref[...] + jnp.sum(oh, axis=1, keepdims=True)
    cnt_ref[...] = carry_ref[...]

    route_ref[0:1, :] = i1.astype(f32)
    route_ref[1:2, :] = i2.astype(f32)
    route_ref[2:3, :] = r1
    route_ref[3:4, :] = r2
    route_ref[4:5, :] = s1 / den
    route_ref[5:6, :] = s2 / den
    route_ref[6:8, :] = jnp.zeros((2, tm), f32)


def _post(l, x_ctx, x_lat, oa_ctx, oa_lat, yr_ctx, yr_lat, ga, gb, wpa, wpb, wo, mod, norm2_w, wr_pad,
          br_col, tri):
    tm = TM_POST
    row = lambda w: pl.BlockSpec((tm, w), lambda i: (i, 0))
    const = lambda a: pl.BlockSpec(a.shape, lambda i: (0,) * a.ndim)
    hbm = pl.BlockSpec(memory_space=pl.ANY)
    return pl.pallas_call(
        functools.partial(_post_kernel, l),
        grid=(T // tm,),
        in_specs=(_two_stream_specs(tm, D) + _two_stream_specs(tm, Q_A) + _two_stream_specs(tm, V_R)
                  + [row(D), row(D), hbm, hbm, hbm,
                     pl.BlockSpec((None, None, 6, D), lambda i: (l, _cond_of_tile(i, tm), 0, 0)),
                     pl.BlockSpec((1, D), lambda i: (0, 0)), const(wr_pad), const(br_col), const(tri)]),
        out_specs=[row(D), row(D), pl.BlockSpec((8, tm), lambda i: (0, i)),
                   pl.BlockSpec((N_EXP, LANES), lambda i: (0, 0))],
        out_shape=[jax.ShapeDtypeStruct((T, D), f32), jax.ShapeDtypeStruct((T, D), f32),
                   jax.ShapeDtypeStruct((8, T), f32), jax.ShapeDtypeStruct((N_EXP, LANES), f32)],
        scratch_shapes=[pltpu.VMEM((N_EXP, LANES), f32), pltpu.VMEM((Q_A, D), bf16),
                        pltpu.VMEM((V_R, D), bf16), pltpu.VMEM((D, D), bf16),
                        pltpu.VMEM((2, D, W_CHUNK), f32), pltpu.SemaphoreType.DMA((2,))],
        compiler_params=pltpu.CompilerParams(
            dimension_semantics=("arbitrary",), vmem_limit_bytes=V7X_VMEM_LIMIT),
        name="post_router",
    )(x_ctx, x_lat, oa_ctx, oa_lat, yr_ctx, yr_lat, ga, gb, wpa, wpb, wo, mod, norm2_w[l:l + 1],
      wr_pad, br_col, tri)


def _row_maps_kernel(valid_ref, pos_ref, src_ref, dst_ref):
    def pad_tile(j, carry):
        dst_ref[j] = 2 * T + j
        return carry

    lax.fori_loop(0, MOE_TR, pad_tile, 0)

    def tile_tail(i, carry):
        def row(j, c):
            src_ref[i * MOE_TR + j] = 0
            dst_ref[(i + 1) * MOE_TR + j] = 2 * T + j
            return c
        return lax.fori_loop(valid_ref[i], MOE_TR, row, carry)

    lax.fori_loop(0, MOE_TILES, tile_tail, 0)

    def token(t, carry):
        for k in range(2):
            r = pos_ref[2 * t + k]
            src_ref[r] = t
            dst_ref[MOE_TR + r] = k * T + t
        return carry

    lax.fori_loop(0, T, token, 0, unroll=8)


def _row_maps(tile_valid, pos):
    smem = pl.BlockSpec(memory_space=pltpu.SMEM)
    src, dst = pl.pallas_call(
        _row_maps_kernel,
        in_specs=[smem, smem],
        out_specs=[smem, smem],
        out_shape=[jax.ShapeDtypeStruct((MOE_ROWS,), jnp.int32),
                   jax.ShapeDtypeStruct((MOE_ROWS + MOE_TR,), jnp.int32)],
        name="moe_row_maps",
    )(tile_valid, pos)
    return src.reshape(MOE_TILES, 1, MOE_TR), dst.reshape(MOE_TILES + 1, 1, MOE_TR)


def _experts_kernel(l, te_ref, first_ref, slot_ref, nxt_ref, nt_ref, src0_ref, srcn_ref, dstp_ref,
                    h_hbm, wg_hbm, wu_hbm, wd_hbm, y_hbm, wg_ref, wu_ref, wd_ref, sg_ref, su_ref, sd_ref,
                    xa_ref, xb_ref, ya_ref, yb_ref, wsems, gsems, ssems):
    i = pl.program_id(0)
    nt = nt_ref[0]
    even = (i % 2) == 0

    def fetch(e, slot):
        return [pltpu.make_async_copy(w.at[l, e], s.at[slot], wsems.at[slot, k])
                for k, (w, s) in enumerate(((wg_hbm, sg_ref), (wu_hbm, su_ref), (wd_hbm, sd_ref)))]

    def pull(src_ref, x_ref, sem):
        for j in range(MOE_TR):
            pltpu.make_async_copy(h_hbm.at[pl.ds(src_ref[0, 0, j], 1)], x_ref.at[pl.ds(j, 1)],
                                  sem).start(priority=0)

    def push(y_ref, sem):
        for j in range(MOE_TR):
            pltpu.make_async_copy(y_ref.at[pl.ds(j, 1)], y_hbm.at[pl.ds(dstp_ref[0, 0, j], 1)],
                                  sem).start(priority=1)

    def wait_pull(x_ref, sem):
        pltpu.make_async_copy(h_hbm.at[pl.ds(0, MOE_TR)], x_ref, sem).wait()

    def wait_push(y_ref, sem):
        pltpu.make_async_copy(y_ref, y_hbm.at[pl.ds(0, MOE_TR)], sem).wait()

    @pl.when(i == 0)
    def _():
        for c in fetch(te_ref[0], 0):
            c.start()
        pull(src0_ref, xa_ref, gsems.at[0])
        yb_ref[...] = jnp.zeros_like(yb_ref)

    it = jnp.minimum(i, MOE_TILES - 1)

    @pl.when((i < nt) & (first_ref[it] != 0))
    def _():
        slot = slot_ref[it]
        for c in fetch(te_ref[it], slot):
            c.wait()
        wg_ref[...] = sg_ref[slot].astype(bf16)
        wu_ref[...] = su_ref[slot].astype(bf16)
        wd_ref[...] = sd_ref[slot].astype(bf16)

        @pl.when(nxt_ref[it] >= 0)
        def _():
            for c in fetch(nxt_ref[it], 1 - slot):
                c.start()

    def tile_step(x_ref, x_next, y_ref, y_prev, g_cur, g_next, s_prev):
        wait_pull(x_ref, g_cur)
        pull(srcn_ref, x_next, g_next)
        push(y_prev, s_prev)
        x = x_ref[...].astype(bf16)
        act = _silu(_dot(x, wg_ref[...])) * _dot(x, wu_ref[...])
        y_ref[...] = _dot(act.astype(bf16), wd_ref[...])
        wait_push(y_prev, s_prev)

    @pl.when((i < nt) & even)
    def _():
        tile_step(xa_ref, xb_ref, ya_ref, yb_ref, gsems.at[0], gsems.at[1], ssems.at[1])

    @pl.when((i < nt) & jnp.logical_not(even))
    def _():
        tile_step(xb_ref, xa_ref, yb_ref, ya_ref, gsems.at[1], gsems.at[0], ssems.at[0])

    def drain(x_ref, y_prev, g_cur, s_prev):
        wait_pull(x_ref, g_cur)
        push(y_prev, s_prev)
        wait_push(y_prev, s_prev)

    @pl.when((i == nt) & even)
    def _():
        drain(xa_ref, yb_ref, gsems.at[0], ssems.at[1])

    @pl.when((i == nt) & jnp.logical_not(even))
    def _():
        drain(xb_ref, ya_ref, gsems.at[1], ssems.at[0])


def _experts(l, plan, src_tiles, dst_tiles, h2, wg, wu, wd):
    def tiles(shift, last):
        return pl.BlockSpec((1, 1, MOE_TR),
                            lambda i, *p: (jnp.clip(i + shift, 0, p[-1][0] - 1 + last), 0, 0),
                            memory_space=pltpu.SMEM)

    hbm = pl.BlockSpec(memory_space=pl.ANY)
    row_buf = pltpu.VMEM((MOE_TR, D), f32)
    return pl.pallas_call(
        functools.partial(_experts_kernel, l),
        grid_spec=pltpu.PrefetchScalarGridSpec(
            num_scalar_prefetch=5,
            grid=(MOE_TILES + 1,),
            in_specs=[tiles(0, 0), tiles(1, 0), tiles(0, 1), hbm, hbm, hbm, hbm],
            out_specs=hbm,
            scratch_shapes=[pltpu.VMEM((D, D_EXP), bf16), pltpu.VMEM((D, D_EXP), bf16),
                            pltpu.VMEM((D_EXP, D), bf16),
                            pltpu.VMEM((2, D, D_EXP), f32), pltpu.VMEM((2, D, D_EXP), f32),
                            pltpu.VMEM((2, D_EXP, D), f32),
                            row_buf, row_buf, row_buf, row_buf,
                            pltpu.SemaphoreType.DMA((2, 3)), pltpu.SemaphoreType.DMA((2,)),
                            pltpu.SemaphoreType.DMA((2,))],
        ),
        out_shape=jax.ShapeDtypeStruct((2 * T + MOE_TR, D), f32),
        compiler_params=pltpu.CompilerParams(
            dimension_semantics=("arbitrary",), vmem_limit_bytes=V7X_VMEM_LIMIT),
        name="moe_experts",
    )(*plan, src_tiles, src_tiles, dst_tiles, h2, wg, wu, wd)


def _combine_kernel(x1_ref, rt_ref, mod_ref, y0_ref, y1_ref, oc_ref, ol_ref):
    tm = x1_ref.shape[0]
    i = pl.program_id(0)
    n_ctx = T_CTX // tm
    moe = rt_ref[:, 4:5] * y0_ref[...] + rt_ref[:, 5:6] * y1_ref[...]
    out = x1_ref[...] + mod_ref[5:6, :] * moe

    @pl.when(i < n_ctx)
    def _():
        oc_ref[...] = out

    @pl.when(i >= n_ctx)
    def _():
        ol_ref[...] = out


def _combine(l, x1, route_t, mod, y):
    tm = TM_COMB
    row = lambda w: pl.BlockSpec((tm, w), lambda i: (i, 0))
    return pl.pallas_call(
        _combine_kernel,
        grid=(T // tm,),
        in_specs=[row(D), row(8),
                  pl.BlockSpec((None, None, 6, D), lambda i: (l, _cond_of_tile(i, tm), 0, 0)),
                  row(D), pl.BlockSpec((tm, D), lambda i: (T // tm + i, 0))],
        out_specs=_two_stream_specs(tm, D),
        out_shape=[jax.ShapeDtypeStruct((T_CTX, D), f32), jax.ShapeDtypeStruct((T_LAT, D), f32)],
        compiler_params=pltpu.CompilerParams(
            dimension_semantics=("arbitrary",), vmem_limit_bytes=V7X_VMEM_LIMIT),
        name="moe_combine",
    )(x1, route_t, mod, y, y)


def _moe_plan(route, counts):
    cnt = counts[:, 0].astype(jnp.int32)
    tiles = (cnt + MOE_TR - 1) // MOE_TR
    tile_end = jnp.cumsum(tiles)
    tile_start = tile_end - tiles
    off = (tile_start * MOE_TR).astype(f32)
    experts = jnp.arange(N_EXP, dtype=f32)[:, None]
    pos = [jnp.sum(jnp.where(route[k][None, :] == experts, off[:, None], 0.0), axis=0) + route[2 + k]
           for k in range(2)]
    pos = jnp.stack(pos, axis=-1).astype(jnp.int32).reshape(2 * T)
    tid = jnp.arange(MOE_TILES, dtype=jnp.int32)
    te = jnp.minimum(jnp.sum(tid[:, None] >= tile_end[None, :], axis=1), N_EXP - 1).astype(jnp.int32)
    n_tiles = tile_end[-1:].astype(jnp.int32)
    used = tiles > 0
    eid = jnp.arange(N_EXP, dtype=jnp.int32)
    of_tile = te[:, None] == eid[None, :]
    first = jnp.any((tid[:, None] == tile_start[None, :]) & used[None, :], axis=1).astype(jnp.int32)
    run_slot = (jnp.cumsum(used.astype(jnp.int32)) - 1) % 2
    later = (eid[None, :] > eid[:, None]) & used[None, :]
    nxt_e = jnp.min(jnp.where(later, eid[None, :], N_EXP), axis=1)
    nxt_e = jnp.where(nxt_e == N_EXP, -1, nxt_e)
    pick = lambda v: jnp.sum(jnp.where(of_tile, v[None, :], 0), axis=1).astype(jnp.int32)
    rows_left = pick(cnt) - (tid - pick(tile_start)) * MOE_TR
    tile_valid = jnp.where(tid < n_tiles[0], jnp.clip(rows_left, 0, MOE_TR), 0).astype(jnp.int32)
    return pos, tile_valid, (te, first, pick(run_slot), pick(nxt_e), n_tiles)


def _rope_tables():
    pos = np.arange(DEC_SEQ)
    half = HD_A // 4
    freqs = ROPE_BASE ** (-np.arange(half, dtype=np.float64) / half)
    ang_r = (pos // GRID_W)[:, None] * freqs[None, :]
    ang_c = (pos % GRID_W)[:, None] * freqs[None, :]
    ang = np.concatenate([ang_r, ang_r, ang_c, ang_c], axis=1)
    sign = np.concatenate([-np.ones(half), np.ones(half)] * 2)[None, :]
    cos = np.tile(np.cos(ang), (1, 2))
    sin = np.tile(np.sin(ang) * sign, (1, 2))
    ident_c = np.ones((TM_PROJ, LANES))
    ident_s = np.zeros((TM_PROJ, LANES))
    return (jnp.asarray(np.concatenate([cos, ident_c]), f32),
            jnp.asarray(np.concatenate([sin, ident_s]), f32))


def kernel(x_prompt, x_sample, c, cache_k, cache_v, state_ret_fwd, state_ret_bwd, c_ctx,
           norm1_w, norm2_w, w_ada, b_ada, w_in, q_norm_w, k_norm_w, attn_sink,
           ret_decay_fwd, ret_decay_bwd, ret_gn_w, w_pa, w_pb, w_o, w_router, b_router,
           w_exp_gate, w_exp_up, w_exp_down):
    x_ctx, x_lat = x_prompt.reshape(T_CTX, D), x_sample.reshape(T_LAT, D)
    cond8 = jnp.zeros((N_COND, D), f32).at[0].set(c_ctx).at[1:1 + DEC_BATCH].set(c)
    mod = _adaln(cond8, w_ada, b_ada).reshape(DEPTH, N_COND, 6, D)

    cos_t, sin_t = _rope_tables()
    blk = np.arange(Q_A) // HD_A
    ones_blk = jnp.asarray((blk[:, None] == blk[None, :]) / HD_A, bf16)
    tri = jnp.asarray(np.triu(np.ones((TM_POST, TM_POST)), 1), bf16)
    qnw = jnp.tile(q_norm_w, (1, NH_A))
    knw = jnp.tile(k_norm_w, (1, NKV_A))
    wr_pad = jnp.pad(w_router, ((0, 0), (0, LANES - N_EXP)))
    br_col = jnp.broadcast_to(b_router[:, None], (N_EXP, LANES))
    ck = cache_k.reshape(DEC_BATCH, DEPTH, PAST, KV_A)
    cv = cache_v.reshape(DEC_BATCH, DEPTH, PAST, KV_A)
    s0 = jnp.concatenate([state_ret_fwd, state_ret_bwd], axis=3)

    new_k, new_v, new_sf, new_sb = [], [], [], []
    for l in range(DEPTH):
        lg_f = jnp.log1p(-jnp.exp(ret_decay_fwd[l].astype(f32)))
        lg_b = jnp.log1p(-jnp.exp(ret_decay_bwd[l].astype(f32)))
        dec = _decay_tables(lg_f, lg_b)

        qa, ka, va, qr, kr, vr, gr, ga, gb = _inproj(l, x_ctx, x_lat, norm1_w, mod, w_in, cos_t, sin_t,
                                                     ones_blk, qnw, knw)
        oa_ctx = _ctx_attn(l, qa, ka, va, attn_sink)
        oa_lat = _lat_attn(l, qa, ka, va, ck, cv, cos_t, sin_t, attn_sink)
        yr_ctx, sf, sb = _retention(l, qr, kr, vr, gr, None, dec, ret_gn_w, BATCH, SEQ, 0)
        yr_lat, _, _ = _retention(l, qr, kr, vr, gr, s0[:, l], dec, ret_gn_w, DEC_BATCH, DEC_SEQ,
                                  T_CTX // DEC_SEQ)
        x1, h2, route, counts = _post(l, x_ctx, x_lat, oa_ctx, oa_lat, yr_ctx, yr_lat, ga, gb,
                                      w_pa, w_pb, w_o, mod, norm2_w, wr_pad, br_col, tri)
        pos, tile_valid, plan = _moe_plan(route, counts)
        src_tiles, dst_tiles = _row_maps(tile_valid, pos)
        y = _experts(l, plan, src_tiles, dst_tiles, h2, w_exp_gate, w_exp_up, w_exp_down)
        x_ctx, x_lat = _combine(l, x1, route.T, mod, y)

        new_k.append(ka[:T_CTX].reshape(BATCH, SEQ, NKV_A, HD_A))
        new_v.append(va[:T_CTX].reshape(BATCH, SEQ, NKV_A, HD_A))
        new_sf.append(sf)
        new_sb.append(sb)

    return (x_ctx.reshape(BATCH, SEQ, D), x_lat.reshape(DEC_BATCH, DEC_SEQ, D),
            jnp.stack(new_k, axis=1), jnp.stack(new_v, axis=1),
            jnp.stack(new_sf, axis=1), jnp.stack(new_sb, axis=1))
```

```python
import functools

import numpy as np
import jax
import jax.numpy as jnp
from jax import lax
from jax.experimental import pallas as pl
from jax.experimental.pallas import tpu as pltpu

D = 1024
BATCH, SEQ = 16, 256
DEC_BATCH, DEC_SEQ = 2, 2048
DEPTH = 2
PAST = 512
GRID_W = 64
NH_A, NKV_A, HD_A = 8, 2, 64
WINDOW = 128
NH_R, DK_R, DV_R = 4, 64, 128
CHUNK = 128
N_EXP, N_GROUPS, EXP_PER_GROUP = 16, 4, 4
D_EXP = 512
ROPE_BASE = 10000.0
EPS = 1e-6
NEG_INF = -1e30

Q_A = NH_A * HD_A
KV_A = NKV_A * HD_A
QK_R = NH_R * DK_R
V_R = NH_R * DV_R
C_QA = (0, Q_A)
C_KA = (C_QA[1], C_QA[1] + KV_A)
C_VA = (C_KA[1], C_KA[1] + KV_A)
C_QR = (C_VA[1], C_VA[1] + QK_R)
C_KR = (C_QR[1], C_QR[1] + QK_R)
C_VR = (C_KR[1], C_KR[1] + V_R)
C_GR = (C_VR[1], C_VR[1] + V_R)
C_GA = (C_GR[1], C_GR[1] + D)
C_GB = (C_GA[1], C_GA[1] + D)
D_IN = C_GB[1]

T_CTX = BATCH * SEQ
T_LAT = DEC_BATCH * DEC_SEQ
T = T_CTX + T_LAT
N_COND = 8

LANES = 128
V7X_VMEM_LIMIT = 56 * 1024 * 1024

TM_PROJ = 512
TM_POST = 512
POST_SUB = 256
W_CHUNK = 256
TM_DISP = 256
MOE_TR = 256
MOE_TILES = 2 * T // MOE_TR + N_EXP
MOE_ROWS = MOE_TILES * MOE_TR
ATT_QB = 256

f32 = jnp.float32
bf16 = jnp.bfloat16


def _dot(a, b):
    return jnp.dot(a, b, preferred_element_type=f32)


def _dot_t(a, b):
    return lax.dot_general(a, b, (((1,), (1,)), ((), ())), preferred_element_type=f32)


def _split(x):
    hi = x.astype(bf16)
    lo = (x - hi.astype(f32)).astype(bf16)
    return hi, lo


def _dot3(a, b):
    ah, al = _split(a)
    bh, bl = _split(b)
    return _dot(ah, bh) + (_dot(ah, bl) + _dot(al, bh))


def _sigmoid(x):
    return 1.0 / (1.0 + jnp.exp(-x))


def _silu(x):
    return x * _sigmoid(x)


def _cond_of_tile(i, tm):
    n_ctx = T_CTX // tm
    per_b = DEC_SEQ // tm
    return jnp.where(i < n_ctx, 0, 1 + jnp.maximum(i - n_ctx, 0) // per_b)


def _adaln_kernel(cond_ref, w_ref, b_ref, o_ref):
    a = _silu(cond_ref[...])
    o_ref[0] = _dot3(a, w_ref[0]) + b_ref[0]


def _adaln(cond8, w_ada, b_ada):
    tn = 1536
    return pl.pallas_call(
        _adaln_kernel,
        grid=(DEPTH, 6 * D // tn),
        in_specs=[
            pl.BlockSpec((N_COND, D), lambda l, j: (0, 0)),
            pl.BlockSpec((1, D, tn), lambda l, j: (l, 0, j)),
            pl.BlockSpec((1, 1, tn), lambda l, j: (l, 0, j)),
        ],
        out_specs=pl.BlockSpec((1, N_COND, tn), lambda l, j: (l, 0, j)),
        out_shape=jax.ShapeDtypeStruct((DEPTH, N_COND, 6 * D), f32),
        compiler_params=pltpu.CompilerParams(
            dimension_semantics=("parallel", "parallel"), vmem_limit_bytes=V7X_VMEM_LIMIT),
        name="adaln",
    )(cond8, w_ada, b_ada.reshape(DEPTH, 1, 6 * D))


def _rope(x, cos, sin_signed, first_half):
    fwd = pltpu.roll(x, 16, 1)
    bwd = pltpu.roll(x, LANES - 16, 1)
    partner = jnp.where(first_half, bwd, fwd)
    return x * cos + partner * sin_signed


def _head_rms(x, ones_blk, w):
    n = x.shape[1]
    sq_hi, sq_lo = _split(x * x)
    blk = ones_blk[0:n, 0:n]
    mean = _dot(sq_hi, blk) + _dot(sq_lo, blk)
    return x * lax.rsqrt(mean + EPS) * w


def _load_cast(w_hbm, dst_ref, stage_ref, sems):
    k, n = w_hbm.shape

    def chunk_copy(c):
        return pltpu.make_async_copy(w_hbm.at[:, pl.ds(c * W_CHUNK, W_CHUNK)],
                                     stage_ref.at[c % 2, pl.ds(0, k)], sems.at[c % 2])

    n_chunks = n // W_CHUNK
    chunk_copy(0).start()
    for c in range(n_chunks):
        if c + 1 < n_chunks:
            chunk_copy(c + 1).start()
        chunk_copy(c).wait()
        dst_ref[:, c * W_CHUNK:(c + 1) * W_CHUNK] = stage_ref[c % 2, 0:k, :].astype(bf16)


def _tile_of_two(i, n_first, a_ref, b_ref):
    return jnp.where(i < n_first, a_ref[...], b_ref[...])


def _inproj_kernel(l, xc_ref, xl_ref, n1_ref, mod_ref, w_hbm, cos_ref, sin_ref, ones_ref, qnw_ref,
                   knw_ref, qa_ref, ka_ref, va_ref, qr_ref, kr_ref, vr_ref, gr_ref, ga_ref, gb_ref,
                   w_ref, stage_ref, sems):
    i = pl.program_id(0)

    @pl.when(i == 0)
    def _():
        _load_cast(w_hbm.at[l], w_ref, stage_ref, sems)

    x = _tile_of_two(i, T_CTX // xc_ref.shape[0], xc_ref, xl_ref)
    y = x * lax.rsqrt(jnp.mean(x * x, axis=-1, keepdims=True) + EPS) * n1_ref[...]
    h = (y * (1.0 + mod_ref[1:2, :]) + mod_ref[0:1, :]).astype(bf16)

    def proj(c):
        return _dot(h, w_ref[:, c[0]:c[1]])

    cos = cos_ref[...]
    sin = sin_ref[...]
    lane = lax.broadcasted_iota(jnp.int32, cos.shape, 1)
    first_half = (lane % 32) < 16

    def rope_all(v):
        parts = [_rope(v[:, j:j + LANES], cos, sin, first_half) for j in range(0, v.shape[1], LANES)]
        return parts[0] if len(parts) == 1 else jnp.concatenate(parts, axis=1)

    ones_blk = ones_ref[...]
    qa = _head_rms(proj(C_QA), ones_blk, qnw_ref[...])
    qa_ref[...] = (rope_all(qa) * (HD_A ** -0.5)).astype(bf16)
    ka_ref[...] = _head_rms(proj(C_KA), ones_blk, knw_ref[...])
    va_ref[...] = proj(C_VA)
    qr_ref[...] = rope_all(proj(C_QR)).astype(bf16)
    kr_ref[...] = (rope_all(proj(C_KR)) * (DK_R ** -0.5)).astype(bf16)
    vr_ref[...] = proj(C_VR).astype(bf16)
    gr_ref[...] = proj(C_GR).astype(bf16)
    ga_ref[...] = proj(C_GA).astype(bf16)
    gb_ref[...] = proj(C_GB).astype(bf16)


def _two_stream_specs(tm, width):
    n_ctx = T_CTX // tm
    return [pl.BlockSpec((tm, width), lambda i, *_: (jnp.minimum(i, n_ctx - 1), 0)),
            pl.BlockSpec((tm, width), lambda i, *_: (jnp.maximum(i - n_ctx, 0), 0))]


def _inproj(l, x_ctx, x_lat, norm1_w, mod, w_in, cos_t, sin_t, ones_blk, qnw, knw):
    tm = TM_PROJ
    n_ctx = T_CTX // tm
    per_b = DEC_SEQ // tm

    def tab_map(i):
        return (jnp.where(i < n_ctx, per_b, jnp.maximum(i - n_ctx, 0) % per_b), 0)

    row = lambda i: (i, 0)
    const = lambda i: (0, 0)
    widths = [(Q_A, bf16), (KV_A, f32), (KV_A, f32), (QK_R, bf16), (QK_R, bf16), (V_R, bf16),
              (V_R, bf16), (D, bf16), (D, bf16)]
    return pl.pallas_call(
        functools.partial(_inproj_kernel, l),
        grid=(T // tm,),
        in_specs=_two_stream_specs(tm, D) + [
            pl.BlockSpec((1, D), const),
            pl.BlockSpec((None, None, 6, D), lambda i: (l, _cond_of_tile(i, tm), 0, 0)),
            pl.BlockSpec(memory_space=pl.ANY),
            pl.BlockSpec((tm, LANES), tab_map),
            pl.BlockSpec((tm, LANES), tab_map),
            pl.BlockSpec((Q_A, Q_A), const),
            pl.BlockSpec((1, Q_A), const),
            pl.BlockSpec((1, KV_A), const),
        ],
        out_specs=[pl.BlockSpec((tm, w), row) for w, _ in widths],
        out_shape=[jax.ShapeDtypeStruct((T, w), dt) for w, dt in widths],
        scratch_shapes=[pltpu.VMEM((D, D_IN), bf16), pltpu.VMEM((2, D, W_CHUNK), f32),
                        pltpu.SemaphoreType.DMA((2,))],
        compiler_params=pltpu.CompilerParams(
            dimension_semantics=("arbitrary",), vmem_limit_bytes=V7X_VMEM_LIMIT),
        name="inproj",
    )(x_ctx, x_lat, norm1_w[l:l + 1], mod, w_in, cos_t, sin_t, ones_blk, qnw[l:l + 1], knw[l:l + 1])


def _head_blocks(t, kv, lo_mask):
    r = pltpu.roll(t, HD_A, 1)
    if kv == 0:
        a = jnp.where(lo_mask, t, 0.0)
        b = jnp.where(lo_mask, 0.0, r)
    else:
        a = jnp.where(lo_mask, r, 0.0)
        b = jnp.where(lo_mask, 0.0, t)
    return jnp.concatenate([a, b], axis=0).astype(bf16)


def _ctx_attn_kernel(sink_ref, q_ref, k_ref, v_ref, o_ref):
    k = k_ref[...]
    v = v_ref[...]
    n = k.shape[0]
    lo_mask = lax.broadcasted_iota(jnp.int32, k.shape, 1) < HD_A
    for kv in range(NKV_A):
        kblk = _head_blocks(k, kv, lo_mask)
        vblk = _head_blocks(v, kv, lo_mask)
        for pr in range(2):
            pi = kv * 2 + pr
            s = _dot_t(q_ref[:, pi * LANES:(pi + 1) * LANES], kblk)
            ps, invs = [], []
            for hh in range(2):
                sk = sink_ref[2 * pi + hh]
                sh = s[:, hh * n:(hh + 1) * n]
                m = jnp.maximum(jnp.max(sh, axis=-1, keepdims=True), sk)
                p = jnp.exp(sh - m)
                invs.append(1.0 / (jnp.sum(p, axis=-1, keepdims=True) + jnp.exp(sk - m)))
                ps.append(p.astype(bf16))
            o = _dot(jnp.concatenate(ps, axis=1), vblk) * jnp.where(lo_mask, invs[0], invs[1])
            o_ref[:, pi * LANES:(pi + 1) * LANES] = o.astype(bf16)


def _ctx_attn(l, qa, ka, va, sink):
    blk = lambda w: pl.BlockSpec((SEQ, w), lambda b: (b, 0))
    return pl.pallas_call(
        _ctx_attn_kernel,
        grid=(BATCH,),
        in_specs=[pl.BlockSpec(memory_space=pltpu.SMEM), blk(Q_A), blk(KV_A), blk(KV_A)],
        out_specs=blk(Q_A),
        out_shape=jax.ShapeDtypeStruct((T_CTX, Q_A), bf16),
        compiler_params=pltpu.CompilerParams(
            dimension_semantics=("parallel",), vmem_limit_bytes=V7X_VMEM_LIMIT),
        name="ctx_attn",
    )(sink[l], qa, ka, va)


def _lat_attn_kernel(sink_ref, q_ref, k_ref, v_ref, kc_ref, vc_ref, cos_ref, sin_ref, o_ref):
    j = pl.program_id(1)
    qb = ATT_QB
    win = 2 * qb
    ws = pl.multiple_of(jnp.clip(j * qb - WINDOW, 0, DEC_SEQ - win), WINDOW)
    lo_mask = lax.broadcasted_iota(jnp.int32, (win, LANES), 1) < HD_A
    lane = lax.broadcasted_iota(jnp.int32, (win, LANES), 1)
    kw = _rope(k_ref[pl.ds(ws, win), :], cos_ref[pl.ds(ws, win), :], sin_ref[pl.ds(ws, win), :],
               (lane % 32) < 16)
    vw = v_ref[pl.ds(ws, win), :]
    kc = kc_ref[...]
    vc = vc_ref[...]
    qpos = j * qb + (lax.broadcasted_iota(jnp.int32, (2 * qb, win), 0) & (qb - 1))
    kpos = ws + lax.broadcasted_iota(jnp.int32, (2 * qb, win), 1)
    valid = jnp.abs(qpos - kpos) <= WINDOW
    out_lo = lax.broadcasted_iota(jnp.int32, (2 * qb, LANES), 1) < HD_A
    for kv in range(NKV_A):
        kc_blk = _head_blocks(kc, kv, lo_mask[:PAST])
        vc_blk = _head_blocks(vc, kv, lo_mask[:PAST])
        kw_blk = _head_blocks(kw, kv, lo_mask)
        vw_blk = _head_blocks(vw, kv, lo_mask)
        q2 = jnp.concatenate([q_ref[:, (2 * kv) * LANES:(2 * kv + 1) * LANES],
                              q_ref[:, (2 * kv + 1) * LANES:(2 * kv + 2) * LANES]], axis=0)
        s_c = _dot_t(q2, kc_blk)
        s_w = _dot_t(q2, kw_blk)
        pcs, pws, invs = [], [], []
        for hh in range(2):
            row = lax.broadcasted_iota(jnp.int32, (2 * qb, 1), 0)
            sk = jnp.where(row < qb, sink_ref[4 * kv + hh], sink_ref[4 * kv + 2 + hh])
            sc = s_c[:, hh * PAST:(hh + 1) * PAST]
            sw = jnp.where(valid, s_w[:, hh * win:(hh + 1) * win], NEG_INF)
            m = jnp.maximum(jnp.maximum(jnp.max(sc, axis=-1, keepdims=True),
                                        jnp.max(sw, axis=-1, keepdims=True)), sk)
            pc = jnp.exp(sc - m)
            pw = jnp.exp(sw - m)
            den = (jnp.sum(pc, axis=-1, keepdims=True) + jnp.sum(pw, axis=-1, keepdims=True)
                   + jnp.exp(sk - m))
            invs.append(1.0 / den)
            pcs.append(pc.astype(bf16))
            pws.append(pw.astype(bf16))
        o = _dot(jnp.concatenate(pcs, axis=1), vc_blk) + _dot(jnp.concatenate(pws, axis=1), vw_blk)
        o = o * jnp.where(out_lo, invs[0], invs[1])
        o_ref[:, (2 * kv) * LANES:(2 * kv + 1) * LANES] = o[:qb].astype(bf16)
        o_ref[:, (2 * kv + 1) * LANES:(2 * kv + 2) * LANES] = o[qb:].astype(bf16)


def _lat_attn(l, qa, ka, va, cache_k, cache_v, cos_l, sin_l, sink):
    qb = ATT_QB
    nq = DEC_SEQ // qb
    ctx_blocks = T_CTX // DEC_SEQ
    seq = lambda b, j: (ctx_blocks + b, 0)
    return pl.pallas_call(
        _lat_attn_kernel,
        grid=(DEC_BATCH, nq),
        in_specs=[
            pl.BlockSpec(memory_space=pltpu.SMEM),
            pl.BlockSpec((qb, Q_A), lambda b, j: (T_CTX // qb + b * nq + j, 0)),
            pl.BlockSpec((DEC_SEQ, KV_A), seq),
            pl.BlockSpec((DEC_SEQ, KV_A), seq),
            pl.BlockSpec((None, None, PAST, KV_A), lambda b, j: (b, l, 0, 0)),
            pl.BlockSpec((None, None, PAST, KV_A), lambda b, j: (b, l, 0, 0)),
            pl.BlockSpec((DEC_SEQ, LANES), lambda b, j: (0, 0)),
            pl.BlockSpec((DEC_SEQ, LANES), lambda b, j: (0, 0)),
        ],
        out_specs=pl.BlockSpec((qb, Q_A), lambda b, j: (b * nq + j, 0)),
        out_shape=jax.ShapeDtypeStruct((T_LAT, Q_A), bf16),
        compiler_params=pltpu.CompilerParams(
            dimension_semantics=("parallel", "parallel"), vmem_limit_bytes=V7X_VMEM_LIMIT),
        name="lat_attn",
    )(sink[l], qa, ka, va, cache_k, cache_v, cos_l, sin_l)


def _dup_heads(pair, lo_mask):
    r = pltpu.roll(pair, DK_R, 1)
    return jnp.where(lo_mask, pair, r), jnp.where(lo_mask, r, pair)


def _retention_kernel(has_s0, n_chunks, *refs):
    if has_s0:
        (q_ref, k_ref, v_ref, g_ref, s0_ref, mask_ref, qdec_ref, kdec_ref, cdec_ref, gnw_ref,
         y_ref, sf_ref, sb_ref, ds_ref, st_ref) = refs
    else:
        (q_ref, k_ref, v_ref, g_ref, mask_ref, qdec_ref, kdec_ref, cdec_ref, gnw_ref,
         y_ref, sf_ref, sb_ref, ds_ref, st_ref) = refs
        s0_ref = None
    C = CHUNK
    lo_mask = lax.broadcasted_iota(jnp.int32, (C, LANES), 1) < DK_R

    def inc_body(c, carry):
        r0 = pl.multiple_of(c * C, C)
        for pr in range(2):
            kp = k_ref[pl.ds(r0, C), pr * LANES:(pr + 1) * LANES].astype(f32)
            for hh, kd in enumerate(_dup_heads(kp, lo_mask)):
                h = 2 * pr + hh
                kd = (kd * kdec_ref[:, h * LANES:(h + 1) * LANES]).astype(bf16)
                vh = v_ref[pl.ds(r0, C), h * DV_R:(h + 1) * DV_R]
                ds_ref[c, h] = lax.dot_general(kd, vh, (((0,), (0,)), ((), ())),
                                               preferred_element_type=f32)
        return carry

    lax.fori_loop(0, n_chunks, inc_body, 0)

    for h in range(NH_R):
        cf = cdec_ref[h, 0:DK_R, :]
        cb = cdec_ref[h, DK_R:2 * DK_R, :]
        if has_s0:
            init_f = s0_ref[h, 0:DK_R, :]
            init_b = s0_ref[h, DK_R:2 * DK_R, :]
        else:
            init_f = jnp.zeros((DK_R, DV_R), f32)
            init_b = init_f

        def fwd_body(c, s, h=h, cf=cf):
            st_ref[c, h, 0:DK_R, :] = s
            return s * cf + ds_ref[c, h, 0:DK_R, :]

        def bwd_body(i, s, h=h, cb=cb):
            c = n_chunks - 1 - i
            st_ref[c, h, DK_R:2 * DK_R, :] = s
            return s * cb + ds_ref[c, h, DK_R:2 * DK_R, :]

        sf_ref[h] = lax.fori_loop(0, n_chunks, fwd_body, init_f)
        sb_ref[h] = lax.fori_loop(0, n_chunks, bwd_body, init_b)

    def out_body(c, carry):
        r0 = pl.multiple_of(c * C, C)
        for pr in range(2):
            qp = q_ref[pl.ds(r0, C), pr * LANES:(pr + 1) * LANES]
            kp = k_ref[pl.ds(r0, C), pr * LANES:(pr + 1) * LANES].astype(f32)
            kblk = jnp.concatenate([jnp.where(lo_mask, kp, 0.0), jnp.where(lo_mask, 0.0, kp)],
                                   axis=0).astype(bf16)
            a2 = _dot_t(qp, kblk)
            for hh, qd in enumerate(_dup_heads(qp.astype(f32), lo_mask)):
                h = 2 * pr + hh
                a = (a2[:, hh * C:(hh + 1) * C] * mask_ref[h]).astype(bf16)
                vh = v_ref[pl.ds(r0, C), h * DV_R:(h + 1) * DV_R]
                qd = (qd * qdec_ref[:, h * LANES:(h + 1) * LANES]).astype(bf16)
                o = _dot(a, vh) + _dot(qd, st_ref[c, h].astype(bf16))
                mu = jnp.mean(o, axis=-1, keepdims=True)
                d = o - mu
                var = jnp.mean(d * d, axis=-1, keepdims=True)
                yh = d * lax.rsqrt(var + EPS) * gnw_ref[:, h * DV_R:(h + 1) * DV_R]
                g = g_ref[pl.ds(r0, C), h * DV_R:(h + 1) * DV_R].astype(f32)
                y_ref[pl.ds(r0, C), h * DV_R:(h + 1) * DV_R] = (yh * _silu(g)).astype(bf16)
        return carry

    lax.fori_loop(0, n_chunks, out_body, 0)


def _retention(l, qr, kr, vr, gr, s0, dec, gnw, nb, seq, row_block0):
    n_chunks = seq // CHUNK
    mask, qdec, kdec, cdec = dec
    has_s0 = s0 is not None
    tok = lambda w: pl.BlockSpec((seq, w), lambda b: (row_block0 + b, 0))
    full = lambda a: pl.BlockSpec(a.shape, lambda b: (0,) * a.ndim)
    in_specs = [tok(QK_R), tok(QK_R), tok(V_R), tok(V_R)]
    args = [qr, kr, vr, gr]
    if has_s0:
        in_specs.append(pl.BlockSpec((None, NH_R, 2 * DK_R, DV_R), lambda b: (b, 0, 0, 0)))
        args.append(s0)
    in_specs += [full(mask), full(qdec), full(kdec), full(cdec),
                 pl.BlockSpec((1, V_R), lambda b: (0, 0))]
    args += [mask, qdec, kdec, cdec, gnw[l:l + 1]]
    st_spec = pl.BlockSpec((None, NH_R, DK_R, DV_R), lambda b: (b, 0, 0, 0))
    return pl.pallas_call(
        functools.partial(_retention_kernel, has_s0, n_chunks),
        grid=(nb,),
        in_specs=in_specs,
        out_specs=[pl.BlockSpec((seq, V_R), lambda b: (b, 0)), st_spec, st_spec],
        out_shape=[jax.ShapeDtypeStruct((nb * seq, V_R), bf16),
                   jax.ShapeDtypeStruct((nb, NH_R, DK_R, DV_R), f32),
                   jax.ShapeDtypeStruct((nb, NH_R, DK_R, DV_R), f32)],
        scratch_shapes=[pltpu.VMEM((n_chunks, NH_R, 2 * DK_R, DV_R), f32),
                        pltpu.VMEM((n_chunks, NH_R, 2 * DK_R, DV_R), f32)],
        compiler_params=pltpu.CompilerParams(
            dimension_semantics=("parallel",), vmem_limit_bytes=V7X_VMEM_LIMIT),
        name="retention_lat" if has_s0 else "retention_ctx",
    )(*args)


def _decay_tables(lg_f, lg_b):
    C = CHUNK
    idx = jnp.arange(C, dtype=f32)
    diff = idx[:, None] - idx[None, :]
    lower = jnp.where(diff >= 0, jnp.exp(jnp.maximum(diff, 0.0)[None] * lg_f[:, None, None]), 0.0)
    upper = jnp.where(diff <= 0, jnp.exp(jnp.maximum(-diff, 0.0)[None] * lg_b[:, None, None]), 0.0)
    mask = lower + upper
    qf = jnp.exp((idx + 1.0)[:, None] * lg_f[None, :])
    qb = jnp.exp((C - idx)[:, None] * lg_b[None, :])
    kf = jnp.exp((C - 1.0 - idx)[:, None] * lg_f[None, :])
    kb = jnp.exp(idx[:, None] * lg_b[None, :])

    def lanes(f, b):
        both = jnp.stack([f, b], axis=-1)
        return jnp.repeat(both, DK_R, axis=-1).reshape(C, NH_R * 2 * DK_R)

    cf = jnp.exp(C * lg_f)
    cb = jnp.exp(C * lg_b)
    cdec = jnp.broadcast_to(jnp.stack([cf, cb], axis=-1)[:, :, None, None],
                            (NH_R, 2, DK_R, DV_R)).reshape(NH_R, 2 * DK_R, DV_R)
    return mask, lanes(qf, qb), lanes(kf, kb), cdec


def _post_kernel(l, xc_ref, xl_ref, oac_ref, oal_ref, yrc_ref, yrl_ref, ga_ref, gb_ref, wpa_hbm, wpb_hbm,
                 wo_hbm, mod_ref, n2_ref, wr_ref, br_ref, tri_ref, x1_ref, h2_ref, route_ref, cnt_ref,
                 carry_ref, wpa_ref, wpb_ref, wo_ref, stage_ref, sems):
    i = pl.program_id(0)
    tm = xc_ref.shape[0]
    n_ctx = T_CTX // tm

    @pl.when(i == 0)
    def _():
        carry_ref[...] = jnp.zeros_like(carry_ref)
        _load_cast(wpa_hbm.at[l], wpa_ref, stage_ref, sems)
        _load_cast(wpb_hbm.at[l], wpb_ref, stage_ref, sems)
        _load_cast(wo_hbm.at[l], wo_ref, stage_ref, sems)

    for r0 in range(0, tm, POST_SUB):
        _post_subtile(i < n_ctx, slice(r0, r0 + POST_SUB), xc_ref, xl_ref, oac_ref, oal_ref, yrc_ref,
                      yrl_ref, ga_ref, gb_ref, mod_ref, n2_ref, wr_ref, br_ref, tri_ref, x1_ref, h2_ref,
                      route_ref, carry_ref, wpa_ref, wpb_ref, wo_ref)
    cnt_ref[...] = carry_ref[...]


def _post_subtile(is_ctx, rows, xc_ref, xl_ref, oac_ref, oal_ref, yrc_ref, yrl_ref, ga_ref, gb_ref, mod_ref,
                  n2_ref, wr_ref, br_ref, tri_ref, x1_ref, h2_ref, route_ref, carry_ref, wpa_ref, wpb_ref,
                  wo_ref):
    tm = rows.stop - rows.start
    pick = lambda a_ref, b_ref: jnp.where(is_ctx, a_ref[rows, :], b_ref[rows, :])
    ga = _sigmoid(ga_ref[rows, :].astype(f32))
    gb = _sigmoid(gb_ref[rows, :].astype(f32))
    merged = (ga * _dot(pick(oac_ref, oal_ref), wpa_ref[...])
              + gb * _dot(pick(yrc_ref, yrl_ref), wpb_ref[...]))
    mix = _dot(merged.astype(bf16), wo_ref[...])
    x1 = pick(xc_ref, xl_ref) + mod_ref[2:3, :] * mix
    x1_ref[rows, :] = x1
    y = x1 * lax.rsqrt(jnp.mean(x1 * x1, axis=-1, keepdims=True) + EPS) * n2_ref[...]
    h2 = y * (1.0 + mod_ref[4:5, :]) + mod_ref[3:4, :]
    h2_ref[rows, :] = h2

    logits = _dot3(h2, wr_ref[...])
    lt = logits.T[0:N_EXP, :]
    scores = _sigmoid(lt)
    sel = scores + br_ref[:, 0:1]
    row = lax.broadcasted_iota(jnp.int32, (N_EXP, tm), 0)

    best = None
    bg = None
    for g in range(N_GROUPS):
        a, b, c, d = (sel[EXP_PER_GROUP * g + k:EXP_PER_GROUP * g + k + 1, :] for k in range(4))
        p, q = jnp.maximum(a, b), jnp.minimum(a, b)
        r, s = jnp.maximum(c, d), jnp.minimum(c, d)
        gs = jnp.maximum(p, r) + jnp.maximum(jnp.minimum(p, r), jnp.maximum(q, s))
        if g == 0:
            best, bg = gs, jnp.zeros((1, tm), jnp.int32)
        else:
            upd = gs > best
            bg = jnp.where(upd, g, bg)
            best = jnp.where(upd, gs, best)
    masked = jnp.where(jnp.right_shift(row, 2) == bg, sel, NEG_INF)
    m1 = jnp.max(masked, axis=0, keepdims=True)
    i1 = jnp.min(jnp.where(masked == m1, row, N_EXP), axis=0, keepdims=True)
    masked2 = jnp.where(row == i1, NEG_INF, masked)
    m2 = jnp.max(masked2, axis=0, keepdims=True)
    i2 = jnp.min(jnp.where(masked2 == m2, row, N_EXP), axis=0, keepdims=True)
    oh1 = row == i1
    oh2 = row == i2
    s1 = jnp.sum(jnp.where(oh1, scores, 0.0), axis=0, keepdims=True)
    s2 = jnp.sum(jnp.where(oh2, scores, 0.0), axis=0, keepdims=True)
    den = s1 + s2

    oh = jnp.where(oh1 | oh2, 1.0, 0.0)
    tot = carry_ref[:, 0:1] + _dot(oh.astype(bf16), tri_ref[...])
    r1 = jnp.sum(jnp.where(oh1, tot, 0.0), axis=0, keepdims=True)
    r2 = jnp.sum(jnp.where(oh2, tot, 0.0), axis=0, keepdims=True)
    carry_ref[...] = carry_ref[...] + jnp.sum(oh, axis=1, keepdims=True)

    route_ref[0:1, rows] = i1.astype(f32)
    route_ref[1:2, rows] = i2.astype(f32)
    route_ref[2:3, rows] = r1
    route_ref[3:4, rows] = r2
    route_ref[4:5, rows] = s1 / den
    route_ref[5:6, rows] = s2 / den
    route_ref[6:8, rows] = jnp.zeros((2, tm), f32)


def _post(l, x_ctx, x_lat, oa_ctx, oa_lat, yr_ctx, yr_lat, ga, gb, wpa, wpb, wo, mod, norm2_w, wr_pad,
          br_col, tri):
    tm = TM_POST
    row = lambda w: pl.BlockSpec((tm, w), lambda i: (i, 0))
    const = lambda a: pl.BlockSpec(a.shape, lambda i: (0,) * a.ndim)
    hbm = pl.BlockSpec(memory_space=pl.ANY)
    return pl.pallas_call(
        functools.partial(_post_kernel, l),
        grid=(T // tm,),
        in_specs=(_two_stream_specs(tm, D) + _two_stream_specs(tm, Q_A) + _two_stream_specs(tm, V_R)
                  + [row(D), row(D), hbm, hbm, hbm,
                     pl.BlockSpec((None, None, 6, D), lambda i: (l, _cond_of_tile(i, tm), 0, 0)),
                     pl.BlockSpec((1, D), lambda i: (0, 0)), const(wr_pad), const(br_col), const(tri)]),
        out_specs=[row(D), row(D), pl.BlockSpec((8, tm), lambda i: (0, i)),
                   pl.BlockSpec((N_EXP, LANES), lambda i: (0, 0))],
        out_shape=[jax.ShapeDtypeStruct((T, D), f32), jax.ShapeDtypeStruct((T, D), f32),
                   jax.ShapeDtypeStruct((8, T), f32), jax.ShapeDtypeStruct((N_EXP, LANES), f32)],
        scratch_shapes=[pltpu.VMEM((N_EXP, LANES), f32), pltpu.VMEM((Q_A, D), bf16),
                        pltpu.VMEM((V_R, D), bf16), pltpu.VMEM((D, D), bf16),
                        pltpu.VMEM((2, D, W_CHUNK), f32), pltpu.SemaphoreType.DMA((2,))],
        compiler_params=pltpu.CompilerParams(
            dimension_semantics=("arbitrary",), vmem_limit_bytes=V7X_VMEM_LIMIT),
        name="post_router",
    )(x_ctx, x_lat, oa_ctx, oa_lat, yr_ctx, yr_lat, ga, gb, wpa, wpb, wo, mod, norm2_w[l:l + 1],
      wr_pad, br_col, tri)


def _row_copy(src_ref, src_row, dst_ref, dst_row, sem):
    return pltpu.make_async_copy(src_ref.at[pl.ds(src_row, 1)], dst_ref.at[pl.ds(dst_row, 1)], sem)


def _dispatch_kernel(zf_ref, pos_ref, h_ref, xs_ref, zero_ref, sem):
    tm = h_ref.shape[0]

    @pl.when(pl.program_id(0) == 0)
    def _():
        zero_ref[...] = jnp.zeros_like(zero_ref)

        def tile_copy(t):
            return pltpu.make_async_copy(zero_ref, xs_ref.at[pl.ds(t * MOE_TR, MOE_TR)], sem)

        def start(t, carry):
            @pl.when(zf_ref[t] != 0)
            def _():
                tile_copy(t).start()
            return carry

        def wait(t, carry):
            @pl.when(zf_ref[t] != 0)
            def _():
                tile_copy(t).wait()
            return carry

        lax.fori_loop(0, MOE_TILES, start, 0)
        lax.fori_loop(0, MOE_TILES, wait, 0)

    for r in range(tm):
        for k in range(2):
            _row_copy(h_ref, r, xs_ref, pos_ref[0, 0, 2 * r + k], sem).start(priority=k)
    for _ in range(2):
        pltpu.make_async_copy(h_ref, xs_ref.at[pl.ds(0, tm)], sem).wait()


def _dispatch(zero_flags, pos_blocks, h2):
    tm = TM_DISP
    return pl.pallas_call(
        _dispatch_kernel,
        grid_spec=pltpu.PrefetchScalarGridSpec(
            num_scalar_prefetch=1,
            grid=(T // tm,),
            in_specs=[pl.BlockSpec((1, 1, 2 * tm), lambda i, zf: (i, 0, 0), memory_space=pltpu.SMEM),
                      pl.BlockSpec((tm, D), lambda i, zf: (i, 0))],
            out_specs=pl.BlockSpec(memory_space=pl.ANY),
            scratch_shapes=[pltpu.VMEM((MOE_TR, D), f32), pltpu.SemaphoreType.DMA(())],
        ),
        out_shape=jax.ShapeDtypeStruct((MOE_ROWS, D), f32),
        compiler_params=pltpu.CompilerParams(
            dimension_semantics=("arbitrary",), vmem_limit_bytes=V7X_VMEM_LIMIT),
        name="moe_dispatch",
    )(zero_flags, pos_blocks, h2)


def _experts_kernel(l, te_ref, first_ref, slot_ref, nxt_ref, nt_ref, xs_ref, wg_hbm, wu_hbm, wd_hbm,
                    y_ref, wg_ref, wu_ref, wd_ref, sg_ref, su_ref, sd_ref, sems):
    i = pl.program_id(0)
    live = i < nt_ref[0]

    def fetch(e, slot):
        return [pltpu.make_async_copy(w.at[l, e], s.at[slot], sems.at[slot, k])
                for k, (w, s) in enumerate(((wg_hbm, sg_ref), (wu_hbm, su_ref), (wd_hbm, sd_ref)))]

    @pl.when(i == 0)
    def _():
        for c in fetch(te_ref[0], 0):
            c.start()

    @pl.when(live & (first_ref[i] != 0))
    def _():
        slot = slot_ref[i]
        for c in fetch(te_ref[i], slot):
            c.wait()
        wg_ref[...] = sg_ref[slot].astype(bf16)
        wu_ref[...] = su_ref[slot].astype(bf16)
        wd_ref[...] = sd_ref[slot].astype(bf16)

        @pl.when(nxt_ref[i] >= 0)
        def _():
            for c in fetch(nxt_ref[i], 1 - slot):
                c.start(priority=1)

    @pl.when(live)
    def _():
        x = xs_ref[...].astype(bf16)
        act = _silu(_dot(x, wg_ref[...])) * _dot(x, wu_ref[...])
        y_ref[...] = _dot(act.astype(bf16), wd_ref[...])

    @pl.when(i >= nt_ref[0])
    def _():
        y_ref[...] = jnp.zeros_like(y_ref)


def _experts(l, plan, xs, wg, wu, wd):
    def tile(i, *prefetch):
        return jnp.minimum(i, prefetch[-1][0] - 1)

    hbm = pl.BlockSpec(memory_space=pl.ANY)
    return pl.pallas_call(
        functools.partial(_experts_kernel, l),
        grid_spec=pltpu.PrefetchScalarGridSpec(
            num_scalar_prefetch=5,
            grid=(MOE_TILES,),
            in_specs=[pl.BlockSpec((MOE_TR, D), lambda i, *p: (tile(i, *p), 0)), hbm, hbm, hbm],
            out_specs=pl.BlockSpec((MOE_TR, D), lambda i, *p: (i, 0)),
            scratch_shapes=[pltpu.VMEM((D, D_EXP), bf16), pltpu.VMEM((D, D_EXP), bf16),
                            pltpu.VMEM((D_EXP, D), bf16),
                            pltpu.VMEM((2, D, D_EXP), f32), pltpu.VMEM((2, D, D_EXP), f32),
                            pltpu.VMEM((2, D_EXP, D), f32), pltpu.SemaphoreType.DMA((2, 3))],
        ),
        out_shape=jax.ShapeDtypeStruct((MOE_ROWS, D), f32),
        compiler_params=pltpu.CompilerParams(
            dimension_semantics=("arbitrary",), vmem_limit_bytes=V7X_VMEM_LIMIT),
        name="moe_experts",
    )(*plan, xs, wg, wu, wd)


def _combine_kernel(pos_ref, x1_ref, rt_ref, mod_ref, y_ref, oc_ref, ol_ref, y0_ref, y1_ref, sem):
    tm = x1_ref.shape[0]
    i = pl.program_id(0)
    n_ctx = T_CTX // tm

    for r in range(tm):
        _row_copy(y_ref, pos_ref[0, 0, 2 * r], y0_ref, r, sem).start(priority=0)
        _row_copy(y_ref, pos_ref[0, 0, 2 * r + 1], y1_ref, r, sem).start(priority=1)
    pltpu.make_async_copy(y_ref.at[pl.ds(0, tm)], y0_ref, sem).wait()
    pltpu.make_async_copy(y_ref.at[pl.ds(0, tm)], y1_ref, sem).wait()
    moe = rt_ref[:, 4:5] * y0_ref[...] + rt_ref[:, 5:6] * y1_ref[...]
    out = x1_ref[...] + mod_ref[5:6, :] * moe

    @pl.when(i < n_ctx)
    def _():
        oc_ref[...] = out

    @pl.when(i >= n_ctx)
    def _():
        ol_ref[...] = out


def _combine(l, pos_blocks, x1, route_t, mod, y):
    tm = TM_DISP
    row = lambda w: pl.BlockSpec((tm, w), lambda i: (i, 0))
    return pl.pallas_call(
        _combine_kernel,
        grid=(T // tm,),
        in_specs=[pl.BlockSpec((1, 1, 2 * tm), lambda i: (i, 0, 0), memory_space=pltpu.SMEM),
                  row(D), row(8),
                  pl.BlockSpec((None, None, 6, D), lambda i: (l, _cond_of_tile(i, tm), 0, 0)),
                  pl.BlockSpec(memory_space=pl.ANY)],
        out_specs=_two_stream_specs(tm, D),
        out_shape=[jax.ShapeDtypeStruct((T_CTX, D), f32), jax.ShapeDtypeStruct((T_LAT, D), f32)],
        scratch_shapes=[pltpu.VMEM((tm, D), f32), pltpu.VMEM((tm, D), f32),
                        pltpu.SemaphoreType.DMA(())],
        compiler_params=pltpu.CompilerParams(
            dimension_semantics=("arbitrary",), vmem_limit_bytes=V7X_VMEM_LIMIT),
        name="moe_combine",
    )(pos_blocks, x1, route_t, mod, y)


def _moe_plan(route, counts):
    cnt = counts[:, 0].astype(jnp.int32)
    tiles = (cnt + MOE_TR - 1) // MOE_TR
    tile_end = jnp.cumsum(tiles)
    tile_start = tile_end - tiles
    off = (tile_start * MOE_TR).astype(f32)
    experts = jnp.arange(N_EXP, dtype=f32)[:, None]
    pos = [jnp.sum(jnp.where(route[k][None, :] == experts, off[:, None], 0.0), axis=0) + route[2 + k]
           for k in range(2)]
    pos_blocks = jnp.stack(pos, axis=-1).astype(jnp.int32).reshape(T // TM_DISP, 1, 2 * TM_DISP)
    tid = jnp.arange(MOE_TILES, dtype=jnp.int32)
    te = jnp.minimum(jnp.sum(tid[:, None] >= tile_end[None, :], axis=1), N_EXP - 1).astype(jnp.int32)
    n_tiles = tile_end[-1:].astype(jnp.int32)
    used = tiles > 0
    is_last = jnp.any((tid[:, None] == tile_end[None, :] - 1) & used[None, :], axis=1)
    zero_flags = ((tid >= n_tiles[0]) | is_last).astype(jnp.int32)
    eid = jnp.arange(N_EXP, dtype=jnp.int32)
    of_tile = te[:, None] == eid[None, :]
    first = jnp.any((tid[:, None] == tile_start[None, :]) & used[None, :], axis=1).astype(jnp.int32)
    run_slot = (jnp.cumsum(used.astype(jnp.int32)) - 1) % 2
    later = (eid[None, :] > eid[:, None]) & used[None, :]
    nxt_e = jnp.min(jnp.where(later, eid[None, :], N_EXP), axis=1)
    nxt_e = jnp.where(nxt_e == N_EXP, -1, nxt_e)
    pick = lambda v: jnp.sum(jnp.where(of_tile, v[None, :], 0), axis=1).astype(jnp.int32)
    return pos_blocks, zero_flags, (te, first, pick(run_slot), pick(nxt_e), n_tiles)


def _rope_tables():
    pos = np.arange(DEC_SEQ)
    half = HD_A // 4
    freqs = ROPE_BASE ** (-np.arange(half, dtype=np.float64) / half)
    ang_r = (pos // GRID_W)[:, None] * freqs[None, :]
    ang_c = (pos % GRID_W)[:, None] * freqs[None, :]
    ang = np.concatenate([ang_r, ang_r, ang_c, ang_c], axis=1)
    sign = np.concatenate([-np.ones(half), np.ones(half)] * 2)[None, :]
    cos = np.tile(np.cos(ang), (1, 2))
    sin = np.tile(np.sin(ang) * sign, (1, 2))
    ident_c = np.ones((TM_PROJ, LANES))
    ident_s = np.zeros((TM_PROJ, LANES))
    return (jnp.asarray(np.concatenate([cos, ident_c]), f32),
            jnp.asarray(np.concatenate([sin, ident_s]), f32))


def kernel(x_prompt, x_sample, c, cache_k, cache_v, state_ret_fwd, state_ret_bwd, c_ctx,
           norm1_w, norm2_w, w_ada, b_ada, w_in, q_norm_w, k_norm_w, attn_sink,
           ret_decay_fwd, ret_decay_bwd, ret_gn_w, w_pa, w_pb, w_o, w_router, b_router,
           w_exp_gate, w_exp_up, w_exp_down):
    x_ctx, x_lat = x_prompt.reshape(T_CTX, D), x_sample.reshape(T_LAT, D)
    cond8 = jnp.zeros((N_COND, D), f32).at[0].set(c_ctx).at[1:1 + DEC_BATCH].set(c)
    mod = _adaln(cond8, w_ada, b_ada).reshape(DEPTH, N_COND, 6, D)

    cos_t, sin_t = _rope_tables()
    blk = np.arange(Q_A) // HD_A
    ones_blk = jnp.asarray((blk[:, None] == blk[None, :]) / HD_A, bf16)
    tri = jnp.asarray(np.triu(np.ones((POST_SUB, POST_SUB)), 1), bf16)
    qnw = jnp.tile(q_norm_w, (1, NH_A))
    knw = jnp.tile(k_norm_w, (1, NKV_A))
    wr_pad = jnp.pad(w_router, ((0, 0), (0, LANES - N_EXP)))
    br_col = jnp.broadcast_to(b_router[:, None], (N_EXP, LANES))
    ck = cache_k.reshape(DEC_BATCH, DEPTH, PAST, KV_A)
    cv = cache_v.reshape(DEC_BATCH, DEPTH, PAST, KV_A)
    s0 = jnp.concatenate([state_ret_fwd, state_ret_bwd], axis=3)

    new_k, new_v, new_sf, new_sb = [], [], [], []
    for l in range(DEPTH):
        lg_f = jnp.log1p(-jnp.exp(ret_decay_fwd[l].astype(f32)))
        lg_b = jnp.log1p(-jnp.exp(ret_decay_bwd[l].astype(f32)))
        dec = _decay_tables(lg_f, lg_b)

        qa, ka, va, qr, kr, vr, gr, ga, gb = _inproj(l, x_ctx, x_lat, norm1_w, mod, w_in, cos_t, sin_t,
                                                     ones_blk, qnw, knw)
        oa_ctx = _ctx_attn(l, qa, ka, va, attn_sink)
        oa_lat = _lat_attn(l, qa, ka, va, ck, cv, cos_t, sin_t, attn_sink)
        yr_ctx, sf, sb = _retention(l, qr, kr, vr, gr, None, dec, ret_gn_w, BATCH, SEQ, 0)
        yr_lat, _, _ = _retention(l, qr, kr, vr, gr, s0[:, l], dec, ret_gn_w, DEC_BATCH, DEC_SEQ,
                                  T_CTX // DEC_SEQ)
        x1, h2, route, counts = _post(l, x_ctx, x_lat, oa_ctx, oa_lat, yr_ctx, yr_lat, ga, gb,
                                      w_pa, w_pb, w_o, mod, norm2_w, wr_pad, br_col, tri)
        pos_blocks, zero_flags, plan = _moe_plan(route, counts)
        xs = _dispatch(zero_flags, pos_blocks, h2)
        y = _experts(l, plan, xs, w_exp_gate, w_exp_up, w_exp_down)
        x_ctx, x_lat = _combine(l, pos_blocks, x1, route.T, mod, y)

        new_k.append(ka[:T_CTX].reshape(BATCH, SEQ, NKV_A, HD_A))
        new_v.append(va[:T_CTX].reshape(BATCH, SEQ, NKV_A, HD_A))
        new_sf.append(sf)
        new_sb.append(sb)

    return (x_ctx.reshape(BATCH, SEQ, D), x_lat.reshape(DEC_BATCH, DEC_SEQ, D),
            jnp.stack(new_k, axis=1), jnp.stack(new_v, axis=1),
            jnp.stack(new_sf, axis=1), jnp.stack(new_sb, axis=1))
```

```python
import functools

import numpy as np
import jax
import jax.numpy as jnp
from jax import lax
from jax.experimental import pallas as pl
from jax.experimental.pallas import tpu as pltpu

D = 1024
BATCH, SEQ = 16, 256
DEC_BATCH, DEC_SEQ = 2, 2048
DEPTH = 2
PAST = 512
GRID_W = 64
NH_A, NKV_A, HD_A = 8, 2, 64
WINDOW = 128
NH_R, DK_R, DV_R = 4, 64, 128
CHUNK = 128
N_EXP, N_GROUPS, EXP_PER_GROUP = 16, 4, 4
D_EXP = 512
ROPE_BASE = 10000.0
EPS = 1e-6
NEG_INF = -1e30

Q_A = NH_A * HD_A
KV_A = NKV_A * HD_A
QK_R = NH_R * DK_R
V_R = NH_R * DV_R
C_QA = (0, Q_A)
C_KA = (C_QA[1], C_QA[1] + KV_A)
C_VA = (C_KA[1], C_KA[1] + KV_A)
C_QR = (C_VA[1], C_VA[1] + QK_R)
C_KR = (C_QR[1], C_QR[1] + QK_R)
C_VR = (C_KR[1], C_KR[1] + V_R)
C_GR = (C_VR[1], C_VR[1] + V_R)
C_GA = (C_GR[1], C_GR[1] + D)
C_GB = (C_GA[1], C_GA[1] + D)
D_IN = C_GB[1]

T_CTX = BATCH * SEQ
T_LAT = DEC_BATCH * DEC_SEQ
T = T_CTX + T_LAT
N_COND = 8

LANES = 128
V7X_VMEM_LIMIT = 56 * 1024 * 1024

TM_PROJ = 512
TM_POST = 512
POST_SUB = 256
W_CHUNK = 256
TM_DISP = 256
MOE_TR = 256
MOE_TILES = T // MOE_TR + N_GROUPS
MOE_ROWS = MOE_TILES * MOE_TR
MOE_W = D + LANES
ATT_QB = 256

f32 = jnp.float32
bf16 = jnp.bfloat16


def _dot(a, b):
    return jnp.dot(a, b, preferred_element_type=f32)


def _dot_t(a, b):
    return lax.dot_general(a, b, (((1,), (1,)), ((), ())), preferred_element_type=f32)


def _split(x):
    hi = x.astype(bf16)
    lo = (x - hi.astype(f32)).astype(bf16)
    return hi, lo


def _dot3(a, b):
    ah, al = _split(a)
    bh, bl = _split(b)
    return _dot(ah, bh) + (_dot(ah, bl) + _dot(al, bh))


def _sigmoid(x):
    return 1.0 / (1.0 + jnp.exp(-x))


def _silu(x):
    return x * _sigmoid(x)


def _cond_of_tile(i, tm):
    n_ctx = T_CTX // tm
    per_b = DEC_SEQ // tm
    return jnp.where(i < n_ctx, 0, 1 + jnp.maximum(i - n_ctx, 0) // per_b)


def _adaln_kernel(cond_ref, w_ref, b_ref, o_ref):
    a = _silu(cond_ref[...])
    o_ref[0] = _dot3(a, w_ref[0]) + b_ref[0]


def _adaln(cond8, w_ada, b_ada):
    tn = 1536
    return pl.pallas_call(
        _adaln_kernel,
        grid=(DEPTH, 6 * D // tn),
        in_specs=[
            pl.BlockSpec((N_COND, D), lambda l, j: (0, 0)),
            pl.BlockSpec((1, D, tn), lambda l, j: (l, 0, j)),
            pl.BlockSpec((1, 1, tn), lambda l, j: (l, 0, j)),
        ],
        out_specs=pl.BlockSpec((1, N_COND, tn), lambda l, j: (l, 0, j)),
        out_shape=jax.ShapeDtypeStruct((DEPTH, N_COND, 6 * D), f32),
        compiler_params=pltpu.CompilerParams(
            dimension_semantics=("parallel", "parallel"), vmem_limit_bytes=V7X_VMEM_LIMIT),
        name="adaln",
    )(cond8, w_ada, b_ada.reshape(DEPTH, 1, 6 * D))


def _rope(x, cos, sin_signed, first_half):
    fwd = pltpu.roll(x, 16, 1)
    bwd = pltpu.roll(x, LANES - 16, 1)
    partner = jnp.where(first_half, bwd, fwd)
    return x * cos + partner * sin_signed


def _head_rms(x, ones_blk, w):
    n = x.shape[1]
    sq_hi, sq_lo = _split(x * x)
    blk = ones_blk[0:n, 0:n]
    mean = _dot(sq_hi, blk) + _dot(sq_lo, blk)
    return x * lax.rsqrt(mean + EPS) * w


def _load_cast(w_hbm, dst_ref, stage_ref, sems):
    k, n = w_hbm.shape

    def chunk_copy(c):
        return pltpu.make_async_copy(w_hbm.at[:, pl.ds(c * W_CHUNK, W_CHUNK)],
                                     stage_ref.at[c % 2, pl.ds(0, k)], sems.at[c % 2])

    n_chunks = n // W_CHUNK
    chunk_copy(0).start()
    for c in range(n_chunks):
        if c + 1 < n_chunks:
            chunk_copy(c + 1).start()
        chunk_copy(c).wait()
        dst_ref[:, c * W_CHUNK:(c + 1) * W_CHUNK] = stage_ref[c % 2, 0:k, :].astype(bf16)


def _tile_of_two(i, n_first, a_ref, b_ref):
    return jnp.where(i < n_first, a_ref[...], b_ref[...])


def _inproj_kernel(l, xc_ref, xl_ref, n1_ref, mod_ref, w_hbm, cos_ref, sin_ref, ones_ref, qnw_ref,
                   knw_ref, qa_ref, ka_ref, va_ref, qr_ref, kr_ref, vr_ref, gr_ref, ga_ref, gb_ref,
                   w_ref, stage_ref, sems):
    i = pl.program_id(0)

    @pl.when(i == 0)
    def _():
        _load_cast(w_hbm.at[l], w_ref, stage_ref, sems)

    x = _tile_of_two(i, T_CTX // xc_ref.shape[0], xc_ref, xl_ref)
    y = x * lax.rsqrt(jnp.mean(x * x, axis=-1, keepdims=True) + EPS) * n1_ref[...]
    h = (y * (1.0 + mod_ref[1:2, :]) + mod_ref[0:1, :]).astype(bf16)

    def proj(c):
        return _dot(h, w_ref[:, c[0]:c[1]])

    cos = cos_ref[...]
    sin = sin_ref[...]
    lane = lax.broadcasted_iota(jnp.int32, cos.shape, 1)
    first_half = (lane % 32) < 16

    def rope_all(v):
        parts = [_rope(v[:, j:j + LANES], cos, sin, first_half) for j in range(0, v.shape[1], LANES)]
        return parts[0] if len(parts) == 1 else jnp.concatenate(parts, axis=1)

    ones_blk = ones_ref[...]
    qa = _head_rms(proj(C_QA), ones_blk, qnw_ref[...])
    qa_ref[...] = (rope_all(qa) * (HD_A ** -0.5)).astype(bf16)
    ka_ref[...] = _head_rms(proj(C_KA), ones_blk, knw_ref[...])
    va_ref[...] = proj(C_VA)
    qr_ref[...] = rope_all(proj(C_QR)).astype(bf16)
    kr_ref[...] = (rope_all(proj(C_KR)) * (DK_R ** -0.5)).astype(bf16)
    vr_ref[...] = proj(C_VR).astype(bf16)
    gr_ref[...] = proj(C_GR).astype(bf16)
    ga_ref[...] = proj(C_GA).astype(bf16)
    gb_ref[...] = proj(C_GB).astype(bf16)


def _two_stream_specs(tm, width):
    n_ctx = T_CTX // tm
    return [pl.BlockSpec((tm, width), lambda i, *_: (jnp.minimum(i, n_ctx - 1), 0)),
            pl.BlockSpec((tm, width), lambda i, *_: (jnp.maximum(i - n_ctx, 0), 0))]


def _inproj(l, x_ctx, x_lat, norm1_w, mod, w_in, cos_t, sin_t, ones_blk, qnw, knw):
    tm = TM_PROJ
    n_ctx = T_CTX // tm
    per_b = DEC_SEQ // tm

    def tab_map(i):
        return (jnp.where(i < n_ctx, per_b, jnp.maximum(i - n_ctx, 0) % per_b), 0)

    row = lambda i: (i, 0)
    const = lambda i: (0, 0)
    widths = [(Q_A, bf16), (KV_A, f32), (KV_A, f32), (QK_R, bf16), (QK_R, bf16), (V_R, bf16),
              (V_R, bf16), (D, bf16), (D, bf16)]
    return pl.pallas_call(
        functools.partial(_inproj_kernel, l),
        grid=(T // tm,),
        in_specs=_two_stream_specs(tm, D) + [
            pl.BlockSpec((1, D), const),
            pl.BlockSpec((None, None, 6, D), lambda i: (l, _cond_of_tile(i, tm), 0, 0)),
            pl.BlockSpec(memory_space=pl.ANY),
            pl.BlockSpec((tm, LANES), tab_map),
            pl.BlockSpec((tm, LANES), tab_map),
            pl.BlockSpec((Q_A, Q_A), const),
            pl.BlockSpec((1, Q_A), const),
            pl.BlockSpec((1, KV_A), const),
        ],
        out_specs=[pl.BlockSpec((tm, w), row) for w, _ in widths],
        out_shape=[jax.ShapeDtypeStruct((T, w), dt) for w, dt in widths],
        scratch_shapes=[pltpu.VMEM((D, D_IN), bf16), pltpu.VMEM((2, D, W_CHUNK), f32),
                        pltpu.SemaphoreType.DMA((2,))],
        compiler_params=pltpu.CompilerParams(
            dimension_semantics=("arbitrary",), vmem_limit_bytes=V7X_VMEM_LIMIT),
        name="inproj",
    )(x_ctx, x_lat, norm1_w[l:l + 1], mod, w_in, cos_t, sin_t, ones_blk, qnw[l:l + 1], knw[l:l + 1])


def _head_blocks(t, kv, lo_mask):
    r = pltpu.roll(t, HD_A, 1)
    if kv == 0:
        a = jnp.where(lo_mask, t, 0.0)
        b = jnp.where(lo_mask, 0.0, r)
    else:
        a = jnp.where(lo_mask, r, 0.0)
        b = jnp.where(lo_mask, 0.0, t)
    return jnp.concatenate([a, b], axis=0).astype(bf16)


def _ctx_attn_kernel(sink_ref, q_ref, k_ref, v_ref, o_ref):
    k = k_ref[...]
    v = v_ref[...]
    n = k.shape[0]
    lo_mask = lax.broadcasted_iota(jnp.int32, k.shape, 1) < HD_A
    for kv in range(NKV_A):
        kblk = _head_blocks(k, kv, lo_mask)
        vblk = _head_blocks(v, kv, lo_mask)
        for pr in range(2):
            pi = kv * 2 + pr
            s = _dot_t(q_ref[:, pi * LANES:(pi + 1) * LANES], kblk)
            ps, invs = [], []
            for hh in range(2):
                sk = sink_ref[2 * pi + hh]
                sh = s[:, hh * n:(hh + 1) * n]
                m = jnp.maximum(jnp.max(sh, axis=-1, keepdims=True), sk)
                p = jnp.exp(sh - m)
                invs.append(1.0 / (jnp.sum(p, axis=-1, keepdims=True) + jnp.exp(sk - m)))
                ps.append(p.astype(bf16))
            o = _dot(jnp.concatenate(ps, axis=1), vblk) * jnp.where(lo_mask, invs[0], invs[1])
            o_ref[:, pi * LANES:(pi + 1) * LANES] = o.astype(bf16)


def _ctx_attn(l, qa, ka, va, sink):
    blk = lambda w: pl.BlockSpec((SEQ, w), lambda b: (b, 0))
    return pl.pallas_call(
        _ctx_attn_kernel,
        grid=(BATCH,),
        in_specs=[pl.BlockSpec(memory_space=pltpu.SMEM), blk(Q_A), blk(KV_A), blk(KV_A)],
        out_specs=blk(Q_A),
        out_shape=jax.ShapeDtypeStruct((T_CTX, Q_A), bf16),
        compiler_params=pltpu.CompilerParams(
            dimension_semantics=("parallel",), vmem_limit_bytes=V7X_VMEM_LIMIT),
        name="ctx_attn",
    )(sink[l], qa, ka, va)


def _lat_attn_kernel(sink_ref, q_ref, k_ref, v_ref, kc_ref, vc_ref, cos_ref, sin_ref, o_ref):
    j = pl.program_id(1)
    qb = ATT_QB
    win = 2 * qb
    ws = pl.multiple_of(jnp.clip(j * qb - WINDOW, 0, DEC_SEQ - win), WINDOW)
    lo_mask = lax.broadcasted_iota(jnp.int32, (win, LANES), 1) < HD_A
    lane = lax.broadcasted_iota(jnp.int32, (win, LANES), 1)
    kw = _rope(k_ref[pl.ds(ws, win), :], cos_ref[pl.ds(ws, win), :], sin_ref[pl.ds(ws, win), :],
               (lane % 32) < 16)
    vw = v_ref[pl.ds(ws, win), :]
    kc = kc_ref[...]
    vc = vc_ref[...]
    qpos = j * qb + (lax.broadcasted_iota(jnp.int32, (2 * qb, win), 0) & (qb - 1))
    kpos = ws + lax.broadcasted_iota(jnp.int32, (2 * qb, win), 1)
    valid = jnp.abs(qpos - kpos) <= WINDOW
    out_lo = lax.broadcasted_iota(jnp.int32, (2 * qb, LANES), 1) < HD_A
    for kv in range(NKV_A):
        kc_blk = _head_blocks(kc, kv, lo_mask[:PAST])
        vc_blk = _head_blocks(vc, kv, lo_mask[:PAST])
        kw_blk = _head_blocks(kw, kv, lo_mask)
        vw_blk = _head_blocks(vw, kv, lo_mask)
        q2 = jnp.concatenate([q_ref[:, (2 * kv) * LANES:(2 * kv + 1) * LANES],
                              q_ref[:, (2 * kv + 1) * LANES:(2 * kv + 2) * LANES]], axis=0)
        s_c = _dot_t(q2, kc_blk)
        s_w = _dot_t(q2, kw_blk)
        pcs, pws, invs = [], [], []
        for hh in range(2):
            row = lax.broadcasted_iota(jnp.int32, (2 * qb, 1), 0)
            sk = jnp.where(row < qb, sink_ref[4 * kv + hh], sink_ref[4 * kv + 2 + hh])
            sc = s_c[:, hh * PAST:(hh + 1) * PAST]
            sw = jnp.where(valid, s_w[:, hh * win:(hh + 1) * win], NEG_INF)
            m = jnp.maximum(jnp.maximum(jnp.max(sc, axis=-1, keepdims=True),
                                        jnp.max(sw, axis=-1, keepdims=True)), sk)
            pc = jnp.exp(sc - m)
            pw = jnp.exp(sw - m)
            den = (jnp.sum(pc, axis=-1, keepdims=True) + jnp.sum(pw, axis=-1, keepdims=True)
                   + jnp.exp(sk - m))
            invs.append(1.0 / den)
            pcs.append(pc.astype(bf16))
            pws.append(pw.astype(bf16))
        o = _dot(jnp.concatenate(pcs, axis=1), vc_blk) + _dot(jnp.concatenate(pws, axis=1), vw_blk)
        o = o * jnp.where(out_lo, invs[0], invs[1])
        o_ref[:, (2 * kv) * LANES:(2 * kv + 1) * LANES] = o[:qb].astype(bf16)
        o_ref[:, (2 * kv + 1) * LANES:(2 * kv + 2) * LANES] = o[qb:].astype(bf16)


def _lat_attn(l, qa, ka, va, cache_k, cache_v, cos_l, sin_l, sink):
    qb = ATT_QB
    nq = DEC_SEQ // qb
    ctx_blocks = T_CTX // DEC_SEQ
    seq = lambda b, j: (ctx_blocks + b, 0)
    return pl.pallas_call(
        _lat_attn_kernel,
        grid=(DEC_BATCH, nq),
        in_specs=[
            pl.BlockSpec(memory_space=pltpu.SMEM),
            pl.BlockSpec((qb, Q_A), lambda b, j: (T_CTX // qb + b * nq + j, 0)),
            pl.BlockSpec((DEC_SEQ, KV_A), seq),
            pl.BlockSpec((DEC_SEQ, KV_A), seq),
            pl.BlockSpec((None, None, PAST, KV_A), lambda b, j: (b, l, 0, 0)),
            pl.BlockSpec((None, None, PAST, KV_A), lambda b, j: (b, l, 0, 0)),
            pl.BlockSpec((DEC_SEQ, LANES), lambda b, j: (0, 0)),
            pl.BlockSpec((DEC_SEQ, LANES), lambda b, j: (0, 0)),
        ],
        out_specs=pl.BlockSpec((qb, Q_A), lambda b, j: (b * nq + j, 0)),
        out_shape=jax.ShapeDtypeStruct((T_LAT, Q_A), bf16),
        compiler_params=pltpu.CompilerParams(
            dimension_semantics=("parallel", "parallel"), vmem_limit_bytes=V7X_VMEM_LIMIT),
        name="lat_attn",
    )(sink[l], qa, ka, va, cache_k, cache_v, cos_l, sin_l)


def _dup_heads(pair, lo_mask):
    r = pltpu.roll(pair, DK_R, 1)
    return jnp.where(lo_mask, pair, r), jnp.where(lo_mask, r, pair)


def _retention_kernel(has_s0, n_chunks, *refs):
    if has_s0:
        (q_ref, k_ref, v_ref, g_ref, s0_ref, mask_ref, qdec_ref, kdec_ref, cdec_ref, gnw_ref,
         y_ref, sf_ref, sb_ref, ds_ref, st_ref) = refs
    else:
        (q_ref, k_ref, v_ref, g_ref, mask_ref, qdec_ref, kdec_ref, cdec_ref, gnw_ref,
         y_ref, sf_ref, sb_ref, ds_ref, st_ref) = refs
        s0_ref = None
    C = CHUNK
    lo_mask = lax.broadcasted_iota(jnp.int32, (C, LANES), 1) < DK_R

    def inc_body(c, carry):
        r0 = pl.multiple_of(c * C, C)
        for pr in range(2):
            kp = k_ref[pl.ds(r0, C), pr * LANES:(pr + 1) * LANES].astype(f32)
            for hh, kd in enumerate(_dup_heads(kp, lo_mask)):
                h = 2 * pr + hh
                kd = (kd * kdec_ref[:, h * LANES:(h + 1) * LANES]).astype(bf16)
                vh = v_ref[pl.ds(r0, C), h * DV_R:(h + 1) * DV_R]
                ds_ref[c, h] = lax.dot_general(kd, vh, (((0,), (0,)), ((), ())),
                                               preferred_element_type=f32)
        return carry

    lax.fori_loop(0, n_chunks, inc_body, 0)

    for h in range(NH_R):
        cf = cdec_ref[h, 0:DK_R, :]
        cb = cdec_ref[h, DK_R:2 * DK_R, :]
        if has_s0:
            init_f = s0_ref[h, 0:DK_R, :]
            init_b = s0_ref[h, DK_R:2 * DK_R, :]
        else:
            init_f = jnp.zeros((DK_R, DV_R), f32)
            init_b = init_f

        def fwd_body(c, s, h=h, cf=cf):
            st_ref[c, h, 0:DK_R, :] = s
            return s * cf + ds_ref[c, h, 0:DK_R, :]

        def bwd_body(i, s, h=h, cb=cb):
            c = n_chunks - 1 - i
            st_ref[c, h, DK_R:2 * DK_R, :] = s
            return s * cb + ds_ref[c, h, DK_R:2 * DK_R, :]

        sf_ref[h] = lax.fori_loop(0, n_chunks, fwd_body, init_f)
        sb_ref[h] = lax.fori_loop(0, n_chunks, bwd_body, init_b)

    def out_body(c, carry):
        r0 = pl.multiple_of(c * C, C)
        for pr in range(2):
            qp = q_ref[pl.ds(r0, C), pr * LANES:(pr + 1) * LANES]
            kp = k_ref[pl.ds(r0, C), pr * LANES:(pr + 1) * LANES].astype(f32)
            kblk = jnp.concatenate([jnp.where(lo_mask, kp, 0.0), jnp.where(lo_mask, 0.0, kp)],
                                   axis=0).astype(bf16)
            a2 = _dot_t(qp, kblk)
            for hh, qd in enumerate(_dup_heads(qp.astype(f32), lo_mask)):
                h = 2 * pr + hh
                a = (a2[:, hh * C:(hh + 1) * C] * mask_ref[h]).astype(bf16)
                vh = v_ref[pl.ds(r0, C), h * DV_R:(h + 1) * DV_R]
                qd = (qd * qdec_ref[:, h * LANES:(h + 1) * LANES]).astype(bf16)
                o = _dot(a, vh) + _dot(qd, st_ref[c, h].astype(bf16))
                mu = jnp.mean(o, axis=-1, keepdims=True)
                d = o - mu
                var = jnp.mean(d * d, axis=-1, keepdims=True)
                yh = d * lax.rsqrt(var + EPS) * gnw_ref[:, h * DV_R:(h + 1) * DV_R]
                g = g_ref[pl.ds(r0, C), h * DV_R:(h + 1) * DV_R].astype(f32)
                y_ref[pl.ds(r0, C), h * DV_R:(h + 1) * DV_R] = (yh * _silu(g)).astype(bf16)
        return carry

    lax.fori_loop(0, n_chunks, out_body, 0)


def _retention(l, qr, kr, vr, gr, s0, dec, gnw, nb, seq, row_block0):
    n_chunks = seq // CHUNK
    mask, qdec, kdec, cdec = dec
    has_s0 = s0 is not None
    tok = lambda w: pl.BlockSpec((seq, w), lambda b: (row_block0 + b, 0))
    full = lambda a: pl.BlockSpec(a.shape, lambda b: (0,) * a.ndim)
    in_specs = [tok(QK_R), tok(QK_R), tok(V_R), tok(V_R)]
    args = [qr, kr, vr, gr]
    if has_s0:
        in_specs.append(pl.BlockSpec((None, NH_R, 2 * DK_R, DV_R), lambda b: (b, 0, 0, 0)))
        args.append(s0)
    in_specs += [full(mask), full(qdec), full(kdec), full(cdec),
                 pl.BlockSpec((1, V_R), lambda b: (0, 0))]
    args += [mask, qdec, kdec, cdec, gnw[l:l + 1]]
    st_spec = pl.BlockSpec((None, NH_R, DK_R, DV_R), lambda b: (b, 0, 0, 0))
    return pl.pallas_call(
        functools.partial(_retention_kernel, has_s0, n_chunks),
        grid=(nb,),
        in_specs=in_specs,
        out_specs=[pl.BlockSpec((seq, V_R), lambda b: (b, 0)), st_spec, st_spec],
        out_shape=[jax.ShapeDtypeStruct((nb * seq, V_R), bf16),
                   jax.ShapeDtypeStruct((nb, NH_R, DK_R, DV_R), f32),
                   jax.ShapeDtypeStruct((nb, NH_R, DK_R, DV_R), f32)],
        scratch_shapes=[pltpu.VMEM((n_chunks, NH_R, 2 * DK_R, DV_R), f32),
                        pltpu.VMEM((n_chunks, NH_R, 2 * DK_R, DV_R), f32)],
        compiler_params=pltpu.CompilerParams(
            dimension_semantics=("parallel",), vmem_limit_bytes=V7X_VMEM_LIMIT),
        name="retention_lat" if has_s0 else "retention_ctx",
    )(*args)


def _decay_tables(lg_f, lg_b):
    C = CHUNK
    idx = jnp.arange(C, dtype=f32)
    diff = idx[:, None] - idx[None, :]
    lower = jnp.where(diff >= 0, jnp.exp(jnp.maximum(diff, 0.0)[None] * lg_f[:, None, None]), 0.0)
    upper = jnp.where(diff <= 0, jnp.exp(jnp.maximum(-diff, 0.0)[None] * lg_b[:, None, None]), 0.0)
    mask = lower + upper
    qf = jnp.exp((idx + 1.0)[:, None] * lg_f[None, :])
    qb = jnp.exp((C - idx)[:, None] * lg_b[None, :])
    kf = jnp.exp((C - 1.0 - idx)[:, None] * lg_f[None, :])
    kb = jnp.exp(idx[:, None] * lg_b[None, :])

    def lanes(f, b):
        both = jnp.stack([f, b], axis=-1)
        return jnp.repeat(both, DK_R, axis=-1).reshape(C, NH_R * 2 * DK_R)

    cf = jnp.exp(C * lg_f)
    cb = jnp.exp(C * lg_b)
    cdec = jnp.broadcast_to(jnp.stack([cf, cb], axis=-1)[:, :, None, None],
                            (NH_R, 2, DK_R, DV_R)).reshape(NH_R, 2 * DK_R, DV_R)
    return mask, lanes(qf, qb), lanes(kf, kb), cdec


def _post_kernel(l, xc_ref, xl_ref, oac_ref, oal_ref, yrc_ref, yrl_ref, ga_ref, gb_ref, wpa_hbm, wpb_hbm,
                 wo_hbm, mod_ref, n2_ref, wr_ref, br_ref, tri_ref, x1_ref, h2_ref, route_ref, cnt_ref,
                 carry_ref, wpa_ref, wpb_ref, wo_ref, stage_ref, sems):
    i = pl.program_id(0)
    tm = xc_ref.shape[0]
    n_ctx = T_CTX // tm

    @pl.when(i == 0)
    def _():
        carry_ref[...] = jnp.zeros_like(carry_ref)
        _load_cast(wpa_hbm.at[l], wpa_ref, stage_ref, sems)
        _load_cast(wpb_hbm.at[l], wpb_ref, stage_ref, sems)
        _load_cast(wo_hbm.at[l], wo_ref, stage_ref, sems)

    for r0 in range(0, tm, POST_SUB):
        _post_subtile(i < n_ctx, slice(r0, r0 + POST_SUB), xc_ref, xl_ref, oac_ref, oal_ref, yrc_ref,
                      yrl_ref, ga_ref, gb_ref, mod_ref, n2_ref, wr_ref, br_ref, tri_ref, x1_ref, h2_ref,
                      route_ref, carry_ref, wpa_ref, wpb_ref, wo_ref)
    cnt_ref[...] = carry_ref[...]


def _post_subtile(is_ctx, rows, xc_ref, xl_ref, oac_ref, oal_ref, yrc_ref, yrl_ref, ga_ref, gb_ref, mod_ref,
                  n2_ref, wr_ref, br_ref, tri_ref, x1_ref, h2_ref, route_ref, carry_ref, wpa_ref, wpb_ref,
                  wo_ref):
    tm = rows.stop - rows.start
    pick = lambda a_ref, b_ref: jnp.where(is_ctx, a_ref[rows, :], b_ref[rows, :])
    ga = _sigmoid(ga_ref[rows, :].astype(f32))
    gb = _sigmoid(gb_ref[rows, :].astype(f32))
    merged = (ga * _dot(pick(oac_ref, oal_ref), wpa_ref[...])
              + gb * _dot(pick(yrc_ref, yrl_ref), wpb_ref[...]))
    mix = _dot(merged.astype(bf16), wo_ref[...])
    x1 = pick(xc_ref, xl_ref) + mod_ref[2:3, :] * mix
    x1_ref[rows, :] = x1
    y = x1 * lax.rsqrt(jnp.mean(x1 * x1, axis=-1, keepdims=True) + EPS) * n2_ref[...]
    h2 = y * (1.0 + mod_ref[4:5, :]) + mod_ref[3:4, :]
    h2_ref[rows, 0:D] = h2

    logits = _dot3(h2, wr_ref[...])
    lt = logits.T[0:N_EXP, :]
    scores = _sigmoid(lt)
    sel = scores + br_ref[:, 0:1]
    row = lax.broadcasted_iota(jnp.int32, (N_EXP, tm), 0)

    best = None
    bg = None
    for g in range(N_GROUPS):
        a, b, c, d = (sel[EXP_PER_GROUP * g + k:EXP_PER_GROUP * g + k + 1, :] for k in range(4))
        p, q = jnp.maximum(a, b), jnp.minimum(a, b)
        r, s = jnp.maximum(c, d), jnp.minimum(c, d)
        gs = jnp.maximum(p, r) + jnp.maximum(jnp.minimum(p, r), jnp.maximum(q, s))
        if g == 0:
            best, bg = gs, jnp.zeros((1, tm), jnp.int32)
        else:
            upd = gs > best
            bg = jnp.where(upd, g, bg)
            best = jnp.where(upd, gs, best)
    masked = jnp.where(jnp.right_shift(row, 2) == bg, sel, NEG_INF)
    m1 = jnp.max(masked, axis=0, keepdims=True)
    i1 = jnp.min(jnp.where(masked == m1, row, N_EXP), axis=0, keepdims=True)
    masked2 = jnp.where(row == i1, NEG_INF, masked)
    m2 = jnp.max(masked2, axis=0, keepdims=True)
    i2 = jnp.min(jnp.where(masked2 == m2, row, N_EXP), axis=0, keepdims=True)
    oh1 = row == i1
    oh2 = row == i2
    s1 = jnp.sum(jnp.where(oh1, scores, 0.0), axis=0, keepdims=True)
    s2 = jnp.sum(jnp.where(oh2, scores, 0.0), axis=0, keepdims=True)
    den = s1 + s2

    comb = jnp.where(oh1, s1 / den, 0.0) + jnp.where(oh2, s2 / den, 0.0)
    comb_t = jnp.concatenate([comb, jnp.zeros((LANES - N_EXP, tm), f32)], axis=0).T
    h2_ref[rows, D:D + LANES] = comb_t

    row8 = lax.broadcasted_iota(jnp.int32, (8, tm), 0)
    ohg = row8 == bg
    ohg_f = jnp.where(ohg, 1.0, 0.0)
    tot = carry_ref[:, 0:1] + _dot(ohg_f.astype(bf16), tri_ref[...])
    rank = jnp.sum(jnp.where(ohg, tot, 0.0), axis=0, keepdims=True)
    carry_ref[...] = carry_ref[...] + jnp.sum(ohg_f, axis=1, keepdims=True)

    route_ref[0:1, rows] = bg.astype(f32)
    route_ref[1:2, rows] = rank
    route_ref[2:8, rows] = jnp.zeros((6, tm), f32)


def _post(l, x_ctx, x_lat, oa_ctx, oa_lat, yr_ctx, yr_lat, ga, gb, wpa, wpb, wo, mod, norm2_w, wr_pad,
          br_col, tri):
    tm = TM_POST
    row = lambda w: pl.BlockSpec((tm, w), lambda i: (i, 0))
    const = lambda a: pl.BlockSpec(a.shape, lambda i: (0,) * a.ndim)
    hbm = pl.BlockSpec(memory_space=pl.ANY)
    return pl.pallas_call(
        functools.partial(_post_kernel, l),
        grid=(T // tm,),
        in_specs=(_two_stream_specs(tm, D) + _two_stream_specs(tm, Q_A) + _two_stream_specs(tm, V_R)
                  + [row(D), row(D), hbm, hbm, hbm,
                     pl.BlockSpec((None, None, 6, D), lambda i: (l, _cond_of_tile(i, tm), 0, 0)),
                     pl.BlockSpec((1, D), lambda i: (0, 0)), const(wr_pad), const(br_col), const(tri)]),
        out_specs=[row(D), row(MOE_W), pl.BlockSpec((8, tm), lambda i: (0, i)),
                   pl.BlockSpec((8, LANES), lambda i: (0, 0))],
        out_shape=[jax.ShapeDtypeStruct((T, D), f32), jax.ShapeDtypeStruct((T, MOE_W), f32),
                   jax.ShapeDtypeStruct((8, T), f32), jax.ShapeDtypeStruct((8, LANES), f32)],
        scratch_shapes=[pltpu.VMEM((8, LANES), f32), pltpu.VMEM((Q_A, D), bf16),
                        pltpu.VMEM((V_R, D), bf16), pltpu.VMEM((D, D), bf16),
                        pltpu.VMEM((2, D, W_CHUNK), f32), pltpu.SemaphoreType.DMA((2,))],
        compiler_params=pltpu.CompilerParams(
            dimension_semantics=("arbitrary",), vmem_limit_bytes=V7X_VMEM_LIMIT),
        name="post_router",
    )(x_ctx, x_lat, oa_ctx, oa_lat, yr_ctx, yr_lat, ga, gb, wpa, wpb, wo, mod, norm2_w[l:l + 1],
      wr_pad, br_col, tri)


def _row_copy(src_ref, src_row, dst_ref, dst_row, sem):
    return pltpu.make_async_copy(src_ref.at[pl.ds(src_row, 1)], dst_ref.at[pl.ds(dst_row, 1)], sem)


def _dispatch_kernel(zf_ref, pos_ref, h_ref, xs_ref, zero_ref, sem):
    tm = h_ref.shape[0]

    @pl.when(pl.program_id(0) == 0)
    def _():
        zero_ref[...] = jnp.zeros_like(zero_ref)

        def tile_copy(t):
            return pltpu.make_async_copy(zero_ref, xs_ref.at[pl.ds(t * MOE_TR, MOE_TR)], sem)

        def start(t, carry):
            @pl.when(zf_ref[t] != 0)
            def _():
                tile_copy(t).start()
            return carry

        def wait(t, carry):
            @pl.when(zf_ref[t] != 0)
            def _():
                tile_copy(t).wait()
            return carry

        lax.fori_loop(0, MOE_TILES, start, 0)
        lax.fori_loop(0, MOE_TILES, wait, 0)

    for r in range(tm):
        _row_copy(h_ref, r, xs_ref, pos_ref[0, 0, r], sem).start(priority=r % 2)
    pltpu.make_async_copy(h_ref, xs_ref.at[pl.ds(0, tm)], sem).wait()


def _dispatch(zero_flags, pos_blocks, h2):
    tm = TM_DISP
    return pl.pallas_call(
        _dispatch_kernel,
        grid_spec=pltpu.PrefetchScalarGridSpec(
            num_scalar_prefetch=1,
            grid=(T // tm,),
            in_specs=[pl.BlockSpec((1, 1, tm), lambda i, zf: (i, 0, 0), memory_space=pltpu.SMEM),
                      pl.BlockSpec((tm, MOE_W), lambda i, zf: (i, 0))],
            out_specs=pl.BlockSpec(memory_space=pl.ANY),
            scratch_shapes=[pltpu.VMEM((MOE_TR, MOE_W), f32), pltpu.SemaphoreType.DMA(())],
        ),
        out_shape=jax.ShapeDtypeStruct((MOE_ROWS, MOE_W), f32),
        compiler_params=pltpu.CompilerParams(
            dimension_semantics=("arbitrary",), vmem_limit_bytes=V7X_VMEM_LIMIT),
        name="moe_dispatch",
    )(zero_flags, pos_blocks, h2)


def _experts_kernel(l, tg_ref, first_ref, nxt_ref, nt_ref, xs_ref, wg_hbm, wu_hbm, wd_hbm,
                    y_ref, wg_ref, wu_ref, wd_ref, sg_ref, su_ref, sd_ref, sems):
    i = pl.program_id(0)
    live = i < nt_ref[0]
    g = tg_ref[i]

    def fetch(grp):
        return [pltpu.make_async_copy(w.at[l, pl.ds(grp * EXP_PER_GROUP, EXP_PER_GROUP)], s, sems.at[k])
                for k, (w, s) in enumerate(((wg_hbm, sg_ref), (wu_hbm, su_ref), (wd_hbm, sd_ref)))]

    @pl.when(i == 0)
    def _():
        for c in fetch(tg_ref[0]):
            c.start()

    @pl.when(live & (first_ref[i] != 0))
    def _():
        for c in fetch(g):
            c.wait()
        for j in range(EXP_PER_GROUP):
            wg_ref[j] = sg_ref[j].astype(bf16)
            wu_ref[j] = su_ref[j].astype(bf16)
            wd_ref[j] = sd_ref[j].astype(bf16)

        @pl.when(nxt_ref[i] >= 0)
        def _():
            for c in fetch(nxt_ref[i]):
                c.start(priority=1)

    @pl.when(live)
    def _():
        x = xs_ref[:, 0:D].astype(bf16)
        comb = xs_ref[:, D:D + LANES]
        lane = lax.broadcasted_iota(jnp.int32, comb.shape, 1)
        acc = None
        for j in range(EXP_PER_GROUP):
            w_j = jnp.sum(jnp.where(lane == g * EXP_PER_GROUP + j, comb, 0.0), axis=1, keepdims=True)
            act = _silu(_dot(x, wg_ref[j])) * _dot(x, wu_ref[j]) * w_j
            part = _dot(act.astype(bf16), wd_ref[j])
            acc = part if acc is None else acc + part
        y_ref[...] = acc

    @pl.when(i >= nt_ref[0])
    def _():
        y_ref[...] = jnp.zeros_like(y_ref)


def _experts(l, plan, xs, wg, wu, wd):
    def tile(i, *prefetch):
        return jnp.minimum(i, prefetch[-1][0] - 1)

    hbm = pl.BlockSpec(memory_space=pl.ANY)
    return pl.pallas_call(
        functools.partial(_experts_kernel, l),
        grid_spec=pltpu.PrefetchScalarGridSpec(
            num_scalar_prefetch=4,
            grid=(MOE_TILES,),
            in_specs=[pl.BlockSpec((MOE_TR, MOE_W), lambda i, *p: (tile(i, *p), 0)), hbm, hbm, hbm],
            out_specs=pl.BlockSpec((MOE_TR, D), lambda i, *p: (i, 0)),
            scratch_shapes=[pltpu.VMEM((EXP_PER_GROUP, D, D_EXP), bf16),
                            pltpu.VMEM((EXP_PER_GROUP, D, D_EXP), bf16),
                            pltpu.VMEM((EXP_PER_GROUP, D_EXP, D), bf16),
                            pltpu.VMEM((EXP_PER_GROUP, D, D_EXP), f32),
                            pltpu.VMEM((EXP_PER_GROUP, D, D_EXP), f32),
                            pltpu.VMEM((EXP_PER_GROUP, D_EXP, D), f32), pltpu.SemaphoreType.DMA((3,))],
        ),
        out_shape=jax.ShapeDtypeStruct((MOE_ROWS, D), f32),
        compiler_params=pltpu.CompilerParams(
            dimension_semantics=("arbitrary",), vmem_limit_bytes=V7X_VMEM_LIMIT),
        name="moe_experts",
    )(*plan, xs, wg, wu, wd)


def _combine_kernel(pos_ref, x1_ref, mod_ref, y_ref, oc_ref, ol_ref, yt_ref, sem):
    tm = x1_ref.shape[0]
    i = pl.program_id(0)
    n_ctx = T_CTX // tm

    for r in range(tm):
        _row_copy(y_ref, pos_ref[0, 0, r], yt_ref, r, sem).start(priority=r % 2)
    pltpu.make_async_copy(y_ref.at[pl.ds(0, tm)], yt_ref, sem).wait()
    out = x1_ref[...] + mod_ref[5:6, :] * yt_ref[...]

    @pl.when(i < n_ctx)
    def _():
        oc_ref[...] = out

    @pl.when(i >= n_ctx)
    def _():
        ol_ref[...] = out


def _combine(l, pos_blocks, x1, mod, y):
    tm = TM_DISP
    return pl.pallas_call(
        _combine_kernel,
        grid=(T // tm,),
        in_specs=[pl.BlockSpec((1, 1, tm), lambda i: (i, 0, 0), memory_space=pltpu.SMEM),
                  pl.BlockSpec((tm, D), lambda i: (i, 0)),
                  pl.BlockSpec((None, None, 6, D), lambda i: (l, _cond_of_tile(i, tm), 0, 0)),
                  pl.BlockSpec(memory_space=pl.ANY)],
        out_specs=_two_stream_specs(tm, D),
        out_shape=[jax.ShapeDtypeStruct((T_CTX, D), f32), jax.ShapeDtypeStruct((T_LAT, D), f32)],
        scratch_shapes=[pltpu.VMEM((tm, D), f32), pltpu.SemaphoreType.DMA(())],
        compiler_params=pltpu.CompilerParams(
            dimension_semantics=("arbitrary",), vmem_limit_bytes=V7X_VMEM_LIMIT),
        name="moe_combine",
    )(pos_blocks, x1, mod, y)


def _moe_plan(route, counts):
    cnt = counts[0:N_GROUPS, 0].astype(jnp.int32)
    tiles = (cnt + MOE_TR - 1) // MOE_TR
    tile_end = jnp.cumsum(tiles)
    tile_start = tile_end - tiles
    off = (tile_start * MOE_TR).astype(f32)
    groups = jnp.arange(N_GROUPS, dtype=f32)[:, None]
    pos = jnp.sum(jnp.where(route[0][None, :] == groups, off[:, None], 0.0), axis=0) + route[1]
    pos_blocks = pos.astype(jnp.int32).reshape(T // TM_DISP, 1, TM_DISP)
    tid = jnp.arange(MOE_TILES, dtype=jnp.int32)
    tg = jnp.minimum(jnp.sum(tid[:, None] >= tile_end[None, :], axis=1), N_GROUPS - 1).astype(jnp.int32)
    n_tiles = tile_end[-1:].astype(jnp.int32)
    used = tiles > 0
    is_last = jnp.any((tid[:, None] == tile_end[None, :] - 1) & used[None, :], axis=1)
    zero_flags = ((tid >= n_tiles[0]) | is_last).astype(jnp.int32)
    gid = jnp.arange(N_GROUPS, dtype=jnp.int32)
    of_tile = tg[:, None] == gid[None, :]
    first = jnp.any((tid[:, None] == tile_start[None, :]) & used[None, :], axis=1).astype(jnp.int32)
    later = (gid[None, :] > gid[:, None]) & used[None, :]
    nxt_g = jnp.min(jnp.where(later, gid[None, :], N_GROUPS), axis=1)
    nxt_g = jnp.where(nxt_g == N_GROUPS, -1, nxt_g)
    nxt = jnp.sum(jnp.where(of_tile, nxt_g[None, :], 0), axis=1).astype(jnp.int32)
    return pos_blocks, zero_flags, (tg, first, nxt, n_tiles)


def _rope_tables():
    pos = np.arange(DEC_SEQ)
    half = HD_A // 4
    freqs = ROPE_BASE ** (-np.arange(half, dtype=np.float64) / half)
    ang_r = (pos // GRID_W)[:, None] * freqs[None, :]
    ang_c = (pos % GRID_W)[:, None] * freqs[None, :]
    ang = np.concatenate([ang_r, ang_r, ang_c, ang_c], axis=1)
    sign = np.concatenate([-np.ones(half), np.ones(half)] * 2)[None, :]
    cos = np.tile(np.cos(ang), (1, 2))
    sin = np.tile(np.sin(ang) * sign, (1, 2))
    ident_c = np.ones((TM_PROJ, LANES))
    ident_s = np.zeros((TM_PROJ, LANES))
    return (jnp.asarray(np.concatenate([cos, ident_c]), f32),
            jnp.asarray(np.concatenate([sin, ident_s]), f32))


def kernel(x_prompt, x_sample, c, cache_k, cache_v, state_ret_fwd, state_ret_bwd, c_ctx,
           norm1_w, norm2_w, w_ada, b_ada, w_in, q_norm_w, k_norm_w, attn_sink,
           ret_decay_fwd, ret_decay_bwd, ret_gn_w, w_pa, w_pb, w_o, w_router, b_router,
           w_exp_gate, w_exp_up, w_exp_down):
    x_ctx, x_lat = x_prompt.reshape(T_CTX, D), x_sample.reshape(T_LAT, D)
    cond8 = jnp.zeros((N_COND, D), f32).at[0].set(c_ctx).at[1:1 + DEC_BATCH].set(c)
    mod = _adaln(cond8, w_ada, b_ada).reshape(DEPTH, N_COND, 6, D)

    cos_t, sin_t = _rope_tables()
    blk = np.arange(Q_A) // HD_A
    ones_blk = jnp.asarray((blk[:, None] == blk[None, :]) / HD_A, bf16)
    tri = jnp.asarray(np.triu(np.ones((POST_SUB, POST_SUB)), 1), bf16)
    qnw = jnp.tile(q_norm_w, (1, NH_A))
    knw = jnp.tile(k_norm_w, (1, NKV_A))
    wr_pad = jnp.pad(w_router, ((0, 0), (0, LANES - N_EXP)))
    br_col = jnp.broadcast_to(b_router[:, None], (N_EXP, LANES))
    ck = cache_k.reshape(DEC_BATCH, DEPTH, PAST, KV_A)
    cv = cache_v.reshape(DEC_BATCH, DEPTH, PAST, KV_A)
    s0 = jnp.concatenate([state_ret_fwd, state_ret_bwd], axis=3)

    new_k, new_v, new_sf, new_sb = [], [], [], []
    for l in range(DEPTH):
        lg_f = jnp.log1p(-jnp.exp(ret_decay_fwd[l].astype(f32)))
        lg_b = jnp.log1p(-jnp.exp(ret_decay_bwd[l].astype(f32)))
        dec = _decay_tables(lg_f, lg_b)

        qa, ka, va, qr, kr, vr, gr, ga, gb = _inproj(l, x_ctx, x_lat, norm1_w, mod, w_in, cos_t, sin_t,
                                                     ones_blk, qnw, knw)
        oa_ctx = _ctx_attn(l, qa, ka, va, attn_sink)
        oa_lat = _lat_attn(l, qa, ka, va, ck, cv, cos_t, sin_t, attn_sink)
        yr_ctx, sf, sb = _retention(l, qr, kr, vr, gr, None, dec, ret_gn_w, BATCH, SEQ, 0)
        yr_lat, _, _ = _retention(l, qr, kr, vr, gr, s0[:, l], dec, ret_gn_w, DEC_BATCH, DEC_SEQ,
                                  T_CTX // DEC_SEQ)
        x1, h2, route, counts = _post(l, x_ctx, x_lat, oa_ctx, oa_lat, yr_ctx, yr_lat, ga, gb,
                                      w_pa, w_pb, w_o, mod, norm2_w, wr_pad, br_col, tri)
        pos_blocks, zero_flags, plan = _moe_plan(route, counts)
        xs = _dispatch(zero_flags, pos_blocks, h2)
        y = _experts(l, plan, xs, w_exp_gate, w_exp_up, w_exp_down)
        x_ctx, x_lat = _combine(l, pos_blocks, x1, mod, y)

        new_k.append(ka[:T_CTX].reshape(BATCH, SEQ, NKV_A, HD_A))
        new_v.append(va[:T_CTX].reshape(BATCH, SEQ, NKV_A, HD_A))
        new_sf.append(sf)
        new_sb.append(sb)

    return (x_ctx.reshape(BATCH, SEQ, D), x_lat.reshape(DEC_BATCH, DEC_SEQ, D),
            jnp.stack(new_k, axis=1), jnp.stack(new_v, axis=1),
            jnp.stack(new_sf, axis=1), jnp.stack(new_sb, axis=1))
```

```python
import functools

import numpy as np
import jax
import jax.numpy as jnp
from jax import lax
from jax.experimental import pallas as pl
from jax.experimental.pallas import tpu as pltpu

D = 1024
BATCH, SEQ = 16, 256
DEC_BATCH, DEC_SEQ = 2, 2048
DEPTH = 2
PAST = 512
GRID_W = 64
NH_A, NKV_A, HD_A = 8, 2, 64
WINDOW = 128
NH_R, DK_R, DV_R = 4, 64, 128
CHUNK = 128
N_EXP, N_GROUPS, EXP_PER_GROUP = 16, 4, 4
D_EXP = 512
ROPE_BASE = 10000.0
EPS = 1e-6
NEG_INF = -1e30

Q_A = NH_A * HD_A
KV_A = NKV_A * HD_A
QK_R = NH_R * DK_R
V_R = NH_R * DV_R
C_QA = (0, Q_A)
C_KA = (C_QA[1], C_QA[1] + KV_A)
C_VA = (C_KA[1], C_KA[1] + KV_A)
C_QR = (C_VA[1], C_VA[1] + QK_R)
C_KR = (C_QR[1], C_QR[1] + QK_R)
C_VR = (C_KR[1], C_KR[1] + V_R)
C_GR = (C_VR[1], C_VR[1] + V_R)
C_GA = (C_GR[1], C_GR[1] + D)
C_GB = (C_GA[1], C_GA[1] + D)
D_IN = C_GB[1]

T_CTX = BATCH * SEQ
T_LAT = DEC_BATCH * DEC_SEQ
T = T_CTX + T_LAT
N_COND = 8

LANES = 128
V7X_VMEM_LIMIT = 56 * 1024 * 1024

TM_PROJ = 512
TM_POST = 512
POST_SUB = 256
W_CHUNK = 256
TM_DISP = 512
MOE_TR = 256
MOE_TILES = T // MOE_TR + N_GROUPS
MOE_ROWS = MOE_TILES * MOE_TR
MOE_W = D + LANES
ATT_QB = 256

f32 = jnp.float32
bf16 = jnp.bfloat16


def _dot(a, b):
    return jnp.dot(a, b, preferred_element_type=f32)


def _dot_t(a, b):
    return lax.dot_general(a, b, (((1,), (1,)), ((), ())), preferred_element_type=f32)


def _split(x):
    hi = x.astype(bf16)
    lo = (x - hi.astype(f32)).astype(bf16)
    return hi, lo


def _dot3(a, b):
    ah, al = _split(a)
    bh, bl = _split(b)
    return _dot(ah, bh) + (_dot(ah, bl) + _dot(al, bh))


def _sigmoid(x):
    return 1.0 / (1.0 + jnp.exp(-x))


def _silu(x):
    return x * _sigmoid(x)


def _cond_of_tile(i, tm):
    n_ctx = T_CTX // tm
    per_b = DEC_SEQ // tm
    return jnp.where(i < n_ctx, 0, 1 + jnp.maximum(i - n_ctx, 0) // per_b)


def _adaln_kernel(cond_ref, w_ref, b_ref, o_ref):
    a = _silu(cond_ref[...])
    o_ref[0] = _dot3(a, w_ref[0]) + b_ref[0]


def _adaln(cond8, w_ada, b_ada):
    tn = 1536
    return pl.pallas_call(
        _adaln_kernel,
        grid=(DEPTH, 6 * D // tn),
        in_specs=[
            pl.BlockSpec((N_COND, D), lambda l, j: (0, 0)),
            pl.BlockSpec((1, D, tn), lambda l, j: (l, 0, j)),
            pl.BlockSpec((1, 1, tn), lambda l, j: (l, 0, j)),
        ],
        out_specs=pl.BlockSpec((1, N_COND, tn), lambda l, j: (l, 0, j)),
        out_shape=jax.ShapeDtypeStruct((DEPTH, N_COND, 6 * D), f32),
        compiler_params=pltpu.CompilerParams(
            dimension_semantics=("parallel", "parallel"), vmem_limit_bytes=V7X_VMEM_LIMIT),
        name="adaln",
    )(cond8, w_ada, b_ada.reshape(DEPTH, 1, 6 * D))


def _rope(x, cos, sin_signed, first_half):
    fwd = pltpu.roll(x, 16, 1)
    bwd = pltpu.roll(x, LANES - 16, 1)
    partner = jnp.where(first_half, bwd, fwd)
    return x * cos + partner * sin_signed


def _head_rms(x, ones_blk, w):
    n = x.shape[1]
    sq_hi, sq_lo = _split(x * x)
    blk = ones_blk[0:n, 0:n]
    mean = _dot(sq_hi, blk) + _dot(sq_lo, blk)
    return x * lax.rsqrt(mean + EPS) * w


def _load_cast(w_hbm, dst_ref, stage_ref, sems):
    k, n = w_hbm.shape

    def chunk_copy(c):
        return pltpu.make_async_copy(w_hbm.at[:, pl.ds(c * W_CHUNK, W_CHUNK)],
                                     stage_ref.at[c % 2, pl.ds(0, k)], sems.at[c % 2])

    n_chunks = n // W_CHUNK
    chunk_copy(0).start()
    for c in range(n_chunks):
        if c + 1 < n_chunks:
            chunk_copy(c + 1).start()
        chunk_copy(c).wait()
        dst_ref[:, c * W_CHUNK:(c + 1) * W_CHUNK] = stage_ref[c % 2, 0:k, :].astype(bf16)


def _tile_of_two(i, n_first, a_ref, b_ref):
    return jnp.where(i < n_first, a_ref[...], b_ref[...])


def _inproj_kernel(l, xc_ref, xl_ref, n1_ref, mod_ref, w_hbm, cos_ref, sin_ref, ones_ref, qnw_ref,
                   knw_ref, qa_ref, ka_ref, va_ref, qr_ref, kr_ref, vr_ref, gr_ref, ga_ref, gb_ref,
                   w_ref, stage_ref, sems):
    i = pl.program_id(0)

    @pl.when(i == 0)
    def _():
        _load_cast(w_hbm.at[l], w_ref, stage_ref, sems)

    x = _tile_of_two(i, T_CTX // xc_ref.shape[0], xc_ref, xl_ref)
    y = x * lax.rsqrt(jnp.mean(x * x, axis=-1, keepdims=True) + EPS) * n1_ref[...]
    h = (y * (1.0 + mod_ref[1:2, :]) + mod_ref[0:1, :]).astype(bf16)

    def proj(c):
        return _dot(h, w_ref[:, c[0]:c[1]])

    cos = cos_ref[...]
    sin = sin_ref[...]
    lane = lax.broadcasted_iota(jnp.int32, cos.shape, 1)
    first_half = (lane % 32) < 16

    def rope_all(v):
        parts = [_rope(v[:, j:j + LANES], cos, sin, first_half) for j in range(0, v.shape[1], LANES)]
        return parts[0] if len(parts) == 1 else jnp.concatenate(parts, axis=1)

    ones_blk = ones_ref[...]
    qa = _head_rms(proj(C_QA), ones_blk, qnw_ref[...])
    qa_ref[...] = (rope_all(qa) * (HD_A ** -0.5)).astype(bf16)
    ka_ref[...] = _head_rms(proj(C_KA), ones_blk, knw_ref[...])
    va_ref[...] = proj(C_VA)
    qr_ref[...] = rope_all(proj(C_QR)).astype(bf16)
    kr_ref[...] = (rope_all(proj(C_KR)) * (DK_R ** -0.5)).astype(bf16)
    vr_ref[...] = proj(C_VR).astype(bf16)
    gr_ref[...] = proj(C_GR).astype(bf16)
    ga_ref[...] = proj(C_GA).astype(bf16)
    gb_ref[...] = proj(C_GB).astype(bf16)


def _two_stream_specs(tm, width):
    n_ctx = T_CTX // tm
    return [pl.BlockSpec((tm, width), lambda i, *_: (jnp.minimum(i, n_ctx - 1), 0)),
            pl.BlockSpec((tm, width), lambda i, *_: (jnp.maximum(i - n_ctx, 0), 0))]


def _inproj(l, x_ctx, x_lat, norm1_w, mod, w_in, cos_t, sin_t, ones_blk, qnw, knw):
    tm = TM_PROJ
    n_ctx = T_CTX // tm
    per_b = DEC_SEQ // tm

    def tab_map(i):
        return (jnp.where(i < n_ctx, per_b, jnp.maximum(i - n_ctx, 0) % per_b), 0)

    row = lambda i: (i, 0)
    const = lambda i: (0, 0)
    widths = [(Q_A, bf16), (KV_A, f32), (KV_A, f32), (QK_R, bf16), (QK_R, bf16), (V_R, bf16),
              (V_R, bf16), (D, bf16), (D, bf16)]
    return pl.pallas_call(
        functools.partial(_inproj_kernel, l),
        grid=(T // tm,),
        in_specs=_two_stream_specs(tm, D) + [
            pl.BlockSpec((1, D), const),
            pl.BlockSpec((None, None, 6, D), lambda i: (l, _cond_of_tile(i, tm), 0, 0)),
            pl.BlockSpec(memory_space=pl.ANY),
            pl.BlockSpec((tm, LANES), tab_map),
            pl.BlockSpec((tm, LANES), tab_map),
            pl.BlockSpec((Q_A, Q_A), const),
            pl.BlockSpec((1, Q_A), const),
            pl.BlockSpec((1, KV_A), const),
        ],
        out_specs=[pl.BlockSpec((tm, w), row) for w, _ in widths],
        out_shape=[jax.ShapeDtypeStruct((T, w), dt) for w, dt in widths],
        scratch_shapes=[pltpu.VMEM((D, D_IN), bf16), pltpu.VMEM((2, D, W_CHUNK), f32),
                        pltpu.SemaphoreType.DMA((2,))],
        compiler_params=pltpu.CompilerParams(
            dimension_semantics=("arbitrary",), vmem_limit_bytes=V7X_VMEM_LIMIT),
        name="inproj",
    )(x_ctx, x_lat, norm1_w[l:l + 1], mod, w_in, cos_t, sin_t, ones_blk, qnw[l:l + 1], knw[l:l + 1])


def _head_blocks(t, kv, lo_mask):
    r = pltpu.roll(t, HD_A, 1)
    if kv == 0:
        a = jnp.where(lo_mask, t, 0.0)
        b = jnp.where(lo_mask, 0.0, r)
    else:
        a = jnp.where(lo_mask, r, 0.0)
        b = jnp.where(lo_mask, 0.0, t)
    return jnp.concatenate([a, b], axis=0).astype(bf16)


def _ctx_attn_kernel(sink_ref, q_ref, k_ref, v_ref, o_ref):
    k = k_ref[...]
    v = v_ref[...]
    n = k.shape[0]
    lo_mask = lax.broadcasted_iota(jnp.int32, k.shape, 1) < HD_A
    for kv in range(NKV_A):
        kblk = _head_blocks(k, kv, lo_mask)
        vblk = _head_blocks(v, kv, lo_mask)
        for pr in range(2):
            pi = kv * 2 + pr
            s = _dot_t(q_ref[:, pi * LANES:(pi + 1) * LANES], kblk)
            ps, invs = [], []
            for hh in range(2):
                sk = sink_ref[2 * pi + hh]
                sh = s[:, hh * n:(hh + 1) * n]
                m = jnp.maximum(jnp.max(sh, axis=-1, keepdims=True), sk)
                p = jnp.exp(sh - m)
                invs.append(1.0 / (jnp.sum(p, axis=-1, keepdims=True) + jnp.exp(sk - m)))
                ps.append(p.astype(bf16))
            o = _dot(jnp.concatenate(ps, axis=1), vblk) * jnp.where(lo_mask, invs[0], invs[1])
            o_ref[:, pi * LANES:(pi + 1) * LANES] = o.astype(bf16)


def _ctx_attn(l, qa, ka, va, sink):
    blk = lambda w: pl.BlockSpec((SEQ, w), lambda b: (b, 0))
    return pl.pallas_call(
        _ctx_attn_kernel,
        grid=(BATCH,),
        in_specs=[pl.BlockSpec(memory_space=pltpu.SMEM), blk(Q_A), blk(KV_A), blk(KV_A)],
        out_specs=blk(Q_A),
        out_shape=jax.ShapeDtypeStruct((T_CTX, Q_A), bf16),
        compiler_params=pltpu.CompilerParams(
            dimension_semantics=("parallel",), vmem_limit_bytes=V7X_VMEM_LIMIT),
        name="ctx_attn",
    )(sink[l], qa, ka, va)


def _lat_attn_kernel(sink_ref, q_ref, k_ref, v_ref, kc_ref, vc_ref, cos_ref, sin_ref, o_ref):
    j = pl.program_id(1)
    qb = ATT_QB
    win = 2 * qb
    ws = pl.multiple_of(jnp.clip(j * qb - WINDOW, 0, DEC_SEQ - win), WINDOW)
    lo_mask = lax.broadcasted_iota(jnp.int32, (win, LANES), 1) < HD_A
    lane = lax.broadcasted_iota(jnp.int32, (win, LANES), 1)
    kw = _rope(k_ref[pl.ds(ws, win), :], cos_ref[pl.ds(ws, win), :], sin_ref[pl.ds(ws, win), :],
               (lane % 32) < 16)
    vw = v_ref[pl.ds(ws, win), :]
    kc = kc_ref[...]
    vc = vc_ref[...]
    qpos = j * qb + (lax.broadcasted_iota(jnp.int32, (2 * qb, win), 0) & (qb - 1))
    kpos = ws + lax.broadcasted_iota(jnp.int32, (2 * qb, win), 1)
    valid = jnp.abs(qpos - kpos) <= WINDOW
    out_lo = lax.broadcasted_iota(jnp.int32, (2 * qb, LANES), 1) < HD_A
    for kv in range(NKV_A):
        kc_blk = _head_blocks(kc, kv, lo_mask[:PAST])
        vc_blk = _head_blocks(vc, kv, lo_mask[:PAST])
        kw_blk = _head_blocks(kw, kv, lo_mask)
        vw_blk = _head_blocks(vw, kv, lo_mask)
        q2 = jnp.concatenate([q_ref[:, (2 * kv) * LANES:(2 * kv + 1) * LANES],
                              q_ref[:, (2 * kv + 1) * LANES:(2 * kv + 2) * LANES]], axis=0)
        s_c = _dot_t(q2, kc_blk)
        s_w = _dot_t(q2, kw_blk)
        pcs, pws, invs = [], [], []
        for hh in range(2):
            row = lax.broadcasted_iota(jnp.int32, (2 * qb, 1), 0)
            sk = jnp.where(row < qb, sink_ref[4 * kv + hh], sink_ref[4 * kv + 2 + hh])
            sc = s_c[:, hh * PAST:(hh + 1) * PAST]
            sw = jnp.where(valid, s_w[:, hh * win:(hh + 1) * win], NEG_INF)
            m = jnp.maximum(jnp.maximum(jnp.max(sc, axis=-1, keepdims=True),
                                        jnp.max(sw, axis=-1, keepdims=True)), sk)
            pc = jnp.exp(sc - m)
            pw = jnp.exp(sw - m)
            den = (jnp.sum(pc, axis=-1, keepdims=True) + jnp.sum(pw, axis=-1, keepdims=True)
                   + jnp.exp(sk - m))
            invs.append(1.0 / den)
            pcs.append(pc.astype(bf16))
            pws.append(pw.astype(bf16))
        o = _dot(jnp.concatenate(pcs, axis=1), vc_blk) + _dot(jnp.concatenate(pws, axis=1), vw_blk)
        o = o * jnp.where(out_lo, invs[0], invs[1])
        o_ref[:, (2 * kv) * LANES:(2 * kv + 1) * LANES] = o[:qb].astype(bf16)
        o_ref[:, (2 * kv + 1) * LANES:(2 * kv + 2) * LANES] = o[qb:].astype(bf16)


def _lat_attn(l, qa, ka, va, cache_k, cache_v, cos_l, sin_l, sink):
    qb = ATT_QB
    nq = DEC_SEQ // qb
    ctx_blocks = T_CTX // DEC_SEQ
    seq = lambda b, j: (ctx_blocks + b, 0)
    return pl.pallas_call(
        _lat_attn_kernel,
        grid=(DEC_BATCH, nq),
        in_specs=[
            pl.BlockSpec(memory_space=pltpu.SMEM),
            pl.BlockSpec((qb, Q_A), lambda b, j: (T_CTX // qb + b * nq + j, 0)),
            pl.BlockSpec((DEC_SEQ, KV_A), seq),
            pl.BlockSpec((DEC_SEQ, KV_A), seq),
            pl.BlockSpec((None, None, PAST, KV_A), lambda b, j: (b, l, 0, 0)),
            pl.BlockSpec((None, None, PAST, KV_A), lambda b, j: (b, l, 0, 0)),
            pl.BlockSpec((DEC_SEQ, LANES), lambda b, j: (0, 0)),
            pl.BlockSpec((DEC_SEQ, LANES), lambda b, j: (0, 0)),
        ],
        out_specs=pl.BlockSpec((qb, Q_A), lambda b, j: (b * nq + j, 0)),
        out_shape=jax.ShapeDtypeStruct((T_LAT, Q_A), bf16),
        compiler_params=pltpu.CompilerParams(
            dimension_semantics=("parallel", "parallel"), vmem_limit_bytes=V7X_VMEM_LIMIT),
        name="lat_attn",
    )(sink[l], qa, ka, va, cache_k, cache_v, cos_l, sin_l)


def _dup_heads(pair, lo_mask):
    r = pltpu.roll(pair, DK_R, 1)
    return jnp.where(lo_mask, pair, r), jnp.where(lo_mask, r, pair)


def _retention_kernel(has_s0, n_chunks, *refs):
    if has_s0:
        (lg_ref, q_ref, k_ref, v_ref, g_ref, s0_ref, gnw_ref,
         y_ref, sf_ref, sb_ref, ds_ref, st_ref, mask_ref, qdec_ref, kdec_ref, cdec_ref) = refs
    else:
        (lg_ref, q_ref, k_ref, v_ref, g_ref, gnw_ref,
         y_ref, sf_ref, sb_ref, ds_ref, st_ref, mask_ref, qdec_ref, kdec_ref, cdec_ref) = refs
        s0_ref = None
    C = CHUNK
    lo_mask = lax.broadcasted_iota(jnp.int32, (C, LANES), 1) < DK_R

    ri = lax.broadcasted_iota(jnp.int32, (C, C), 0).astype(f32)
    diff = ri - lax.broadcasted_iota(jnp.int32, (C, C), 1).astype(f32)
    for h in range(NH_R):
        lg_f, lg_b = lg_ref[h], lg_ref[NH_R + h]
        mask_ref[h] = (jnp.where(diff >= 0, jnp.exp(jnp.maximum(diff, 0.0) * lg_f), 0.0)
                       + jnp.where(diff <= 0, jnp.exp(jnp.maximum(-diff, 0.0) * lg_b), 0.0))
        qdec_ref[:, h * LANES:(h + 1) * LANES] = jnp.exp(
            jnp.where(lo_mask, (ri + 1.0) * lg_f, (C - ri) * lg_b))
        kdec_ref[:, h * LANES:(h + 1) * LANES] = jnp.exp(
            jnp.where(lo_mask, (C - 1.0 - ri) * lg_f, ri * lg_b))
        cdec_ref[h] = jnp.exp(jnp.where(ri < DK_R, C * lg_f, C * lg_b))

    def inc_body(c, carry):
        r0 = pl.multiple_of(c * C, C)
        for pr in range(2):
            kp = k_ref[pl.ds(r0, C), pr * LANES:(pr + 1) * LANES].astype(f32)
            for hh, kd in enumerate(_dup_heads(kp, lo_mask)):
                h = 2 * pr + hh
                kd = (kd * kdec_ref[:, h * LANES:(h + 1) * LANES]).astype(bf16)
                vh = v_ref[pl.ds(r0, C), h * DV_R:(h + 1) * DV_R]
                ds_ref[c, h] = lax.dot_general(kd, vh, (((0,), (0,)), ((), ())),
                                               preferred_element_type=f32)
        return carry

    lax.fori_loop(0, n_chunks, inc_body, 0)

    for h in range(NH_R):
        cf = cdec_ref[h, 0:DK_R, :]
        cb = cdec_ref[h, DK_R:2 * DK_R, :]
        if has_s0:
            init_f = s0_ref[h, 0:DK_R, :]
            init_b = s0_ref[h, DK_R:2 * DK_R, :]
        else:
            init_f = jnp.zeros((DK_R, DV_R), f32)
            init_b = init_f

        def fwd_body(c, s, h=h, cf=cf):
            st_ref[c, h, 0:DK_R, :] = s
            return s * cf + ds_ref[c, h, 0:DK_R, :]

        def bwd_body(i, s, h=h, cb=cb):
            c = n_chunks - 1 - i
            st_ref[c, h, DK_R:2 * DK_R, :] = s
            return s * cb + ds_ref[c, h, DK_R:2 * DK_R, :]

        sf_ref[h] = lax.fori_loop(0, n_chunks, fwd_body, init_f)
        sb_ref[h] = lax.fori_loop(0, n_chunks, bwd_body, init_b)

    def out_body(c, carry):
        r0 = pl.multiple_of(c * C, C)
        for pr in range(2):
            qp = q_ref[pl.ds(r0, C), pr * LANES:(pr + 1) * LANES]
            kp = k_ref[pl.ds(r0, C), pr * LANES:(pr + 1) * LANES].astype(f32)
            kblk = jnp.concatenate([jnp.where(lo_mask, kp, 0.0), jnp.where(lo_mask, 0.0, kp)],
                                   axis=0).astype(bf16)
            a2 = _dot_t(qp, kblk)
            for hh, qd in enumerate(_dup_heads(qp.astype(f32), lo_mask)):
                h = 2 * pr + hh
                a = (a2[:, hh * C:(hh + 1) * C] * mask_ref[h]).astype(bf16)
                vh = v_ref[pl.ds(r0, C), h * DV_R:(h + 1) * DV_R]
                qd = (qd * qdec_ref[:, h * LANES:(h + 1) * LANES]).astype(bf16)
                o = _dot(a, vh) + _dot(qd, st_ref[c, h].astype(bf16))
                mu = jnp.mean(o, axis=-1, keepdims=True)
                d = o - mu
                var = jnp.mean(d * d, axis=-1, keepdims=True)
                yh = d * lax.rsqrt(var + EPS) * gnw_ref[:, h * DV_R:(h + 1) * DV_R]
                g = g_ref[pl.ds(r0, C), h * DV_R:(h + 1) * DV_R].astype(f32)
                y_ref[pl.ds(r0, C), h * DV_R:(h + 1) * DV_R] = (yh * _silu(g)).astype(bf16)
        return carry

    lax.fori_loop(0, n_chunks, out_body, 0)


def _retention(l, qr, kr, vr, gr, s0, lg, gnw, nb, seq, row_block0):
    n_chunks = seq // CHUNK
    has_s0 = s0 is not None
    tok = lambda w: pl.BlockSpec((seq, w), lambda b: (row_block0 + b, 0))
    in_specs = [pl.BlockSpec(memory_space=pltpu.SMEM), tok(QK_R), tok(QK_R), tok(V_R), tok(V_R)]
    args = [lg[l], qr, kr, vr, gr]
    if has_s0:
        in_specs.append(pl.BlockSpec((None, NH_R, 2 * DK_R, DV_R), lambda b: (b, 0, 0, 0)))
        args.append(s0)
    in_specs += [pl.BlockSpec((1, V_R), lambda b: (0, 0))]
    args += [gnw[l:l + 1]]
    st_spec = pl.BlockSpec((None, NH_R, DK_R, DV_R), lambda b: (b, 0, 0, 0))
    return pl.pallas_call(
        functools.partial(_retention_kernel, has_s0, n_chunks),
        grid=(nb,),
        in_specs=in_specs,
        out_specs=[pl.BlockSpec((seq, V_R), lambda b: (b, 0)), st_spec, st_spec],
        out_shape=[jax.ShapeDtypeStruct((nb * seq, V_R), bf16),
                   jax.ShapeDtypeStruct((nb, NH_R, DK_R, DV_R), f32),
                   jax.ShapeDtypeStruct((nb, NH_R, DK_R, DV_R), f32)],
        scratch_shapes=[pltpu.VMEM((n_chunks, NH_R, 2 * DK_R, DV_R), f32),
                        pltpu.VMEM((n_chunks, NH_R, 2 * DK_R, DV_R), f32),
                        pltpu.VMEM((NH_R, CHUNK, CHUNK), f32), pltpu.VMEM((CHUNK, NH_R * LANES), f32),
                        pltpu.VMEM((CHUNK, NH_R * LANES), f32), pltpu.VMEM((NH_R, 2 * DK_R, DV_R), f32)],
        compiler_params=pltpu.CompilerParams(
            dimension_semantics=("parallel",), vmem_limit_bytes=V7X_VMEM_LIMIT),
        name="retention_lat" if has_s0 else "retention_ctx",
    )(*args)


def _post_kernel(l, xc_ref, xl_ref, oac_ref, oal_ref, yrc_ref, yrl_ref, ga_ref, gb_ref, wpa_hbm, wpb_hbm,
                 wo_hbm, mod_ref, n2_ref, wr_ref, br_ref, tri_ref, x1_ref, h2_ref, route_ref, cnt_ref,
                 carry_ref, wpa_ref, wpb_ref, wo_ref, stage_ref, sems):
    i = pl.program_id(0)
    tm = xc_ref.shape[0]
    n_ctx = T_CTX // tm

    @pl.when(i == 0)
    def _():
        carry_ref[...] = jnp.zeros_like(carry_ref)
        _load_cast(wpa_hbm.at[l], wpa_ref, stage_ref, sems)
        _load_cast(wpb_hbm.at[l], wpb_ref, stage_ref, sems)
        _load_cast(wo_hbm.at[l], wo_ref, stage_ref, sems)

    for r0 in range(0, tm, POST_SUB):
        _post_subtile(i < n_ctx, slice(r0, r0 + POST_SUB), xc_ref, xl_ref, oac_ref, oal_ref, yrc_ref,
                      yrl_ref, ga_ref, gb_ref, mod_ref, n2_ref, wr_ref, br_ref, tri_ref, x1_ref, h2_ref,
                      route_ref, carry_ref, wpa_ref, wpb_ref, wo_ref)
    cnt_ref[...] = carry_ref[...]


def _post_subtile(is_ctx, rows, xc_ref, xl_ref, oac_ref, oal_ref, yrc_ref, yrl_ref, ga_ref, gb_ref, mod_ref,
                  n2_ref, wr_ref, br_ref, tri_ref, x1_ref, h2_ref, route_ref, carry_ref, wpa_ref, wpb_ref,
                  wo_ref):
    tm = rows.stop - rows.start
    pick = lambda a_ref, b_ref: jnp.where(is_ctx, a_ref[rows, :], b_ref[rows, :])
    ga = _sigmoid(ga_ref[rows, :].astype(f32))
    gb = _sigmoid(gb_ref[rows, :].astype(f32))
    merged = (ga * _dot(pick(oac_ref, oal_ref), wpa_ref[...])
              + gb * _dot(pick(yrc_ref, yrl_ref), wpb_ref[...]))
    mix = _dot(merged.astype(bf16), wo_ref[...])
    x1 = pick(xc_ref, xl_ref) + mod_ref[2:3, :] * mix
    x1_ref[rows, :] = x1
    y = x1 * lax.rsqrt(jnp.mean(x1 * x1, axis=-1, keepdims=True) + EPS) * n2_ref[...]
    h2 = y * (1.0 + mod_ref[4:5, :]) + mod_ref[3:4, :]
    h2_ref[rows, 0:D] = h2

    logits = _dot3(h2, wr_ref[...])
    lt = logits.T[0:N_EXP, :]
    scores = _sigmoid(lt)
    sel = scores + br_ref[:, 0:1]
    row = lax.broadcasted_iota(jnp.int32, (N_EXP, tm), 0)

    best = None
    bg = None
    for g in range(N_GROUPS):
        a, b, c, d = (sel[EXP_PER_GROUP * g + k:EXP_PER_GROUP * g + k + 1, :] for k in range(4))
        p, q = jnp.maximum(a, b), jnp.minimum(a, b)
        r, s = jnp.maximum(c, d), jnp.minimum(c, d)
        gs = jnp.maximum(p, r) + jnp.maximum(jnp.minimum(p, r), jnp.maximum(q, s))
        if g == 0:
            best, bg = gs, jnp.zeros((1, tm), jnp.int32)
        else:
            upd = gs > best
            bg = jnp.where(upd, g, bg)
            best = jnp.where(upd, gs, best)
    masked = jnp.where(jnp.right_shift(row, 2) == bg, sel, NEG_INF)
    m1 = jnp.max(masked, axis=0, keepdims=True)
    i1 = jnp.min(jnp.where(masked == m1, row, N_EXP), axis=0, keepdims=True)
    masked2 = jnp.where(row == i1, NEG_INF, masked)
    m2 = jnp.max(masked2, axis=0, keepdims=True)
    i2 = jnp.min(jnp.where(masked2 == m2, row, N_EXP), axis=0, keepdims=True)
    oh1 = row == i1
    oh2 = row == i2
    s1 = jnp.sum(jnp.where(oh1, scores, 0.0), axis=0, keepdims=True)
    s2 = jnp.sum(jnp.where(oh2, scores, 0.0), axis=0, keepdims=True)
    den = s1 + s2

    comb = jnp.where(oh1, s1 / den, 0.0) + jnp.where(oh2, s2 / den, 0.0)
    comb_t = jnp.concatenate([comb, jnp.zeros((LANES - N_EXP, tm), f32)], axis=0).T
    h2_ref[rows, D:D + LANES] = comb_t

    row8 = lax.broadcasted_iota(jnp.int32, (8, tm), 0)
    ohg = row8 == bg
    ohg_f = jnp.where(ohg, 1.0, 0.0)
    tot = carry_ref[:, 0:1] + _dot(ohg_f.astype(bf16), tri_ref[...])
    rank = jnp.sum(jnp.where(ohg, tot, 0.0), axis=0, keepdims=True)
    carry_ref[...] = carry_ref[...] + jnp.sum(ohg_f, axis=1, keepdims=True)

    route_ref[0:1, rows] = bg.astype(f32)
    route_ref[1:2, rows] = rank
    route_ref[2:8, rows] = jnp.zeros((6, tm), f32)


def _post(l, x_ctx, x_lat, oa_ctx, oa_lat, yr_ctx, yr_lat, ga, gb, wpa, wpb, wo, mod, norm2_w, wr_pad,
          br_col, tri):
    tm = TM_POST
    row = lambda w: pl.BlockSpec((tm, w), lambda i: (i, 0))
    const = lambda a: pl.BlockSpec(a.shape, lambda i: (0,) * a.ndim)
    hbm = pl.BlockSpec(memory_space=pl.ANY)
    return pl.pallas_call(
        functools.partial(_post_kernel, l),
        grid=(T // tm,),
        in_specs=(_two_stream_specs(tm, D) + _two_stream_specs(tm, Q_A) + _two_stream_specs(tm, V_R)
                  + [row(D), row(D), hbm, hbm, hbm,
                     pl.BlockSpec((None, None, 6, D), lambda i: (l, _cond_of_tile(i, tm), 0, 0)),
                     pl.BlockSpec((1, D), lambda i: (0, 0)), const(wr_pad), const(br_col), const(tri)]),
        out_specs=[row(D), row(MOE_W), pl.BlockSpec((8, tm), lambda i: (0, i)),
                   pl.BlockSpec((8, LANES), lambda i: (0, 0))],
        out_shape=[jax.ShapeDtypeStruct((T, D), f32), jax.ShapeDtypeStruct((T, MOE_W), f32),
                   jax.ShapeDtypeStruct((8, T), f32), jax.ShapeDtypeStruct((8, LANES), f32)],
        scratch_shapes=[pltpu.VMEM((8, LANES), f32), pltpu.VMEM((Q_A, D), bf16),
                        pltpu.VMEM((V_R, D), bf16), pltpu.VMEM((D, D), bf16),
                        pltpu.VMEM((2, D, W_CHUNK), f32), pltpu.SemaphoreType.DMA((2,))],
        compiler_params=pltpu.CompilerParams(
            dimension_semantics=("arbitrary",), vmem_limit_bytes=V7X_VMEM_LIMIT),
        name="post_router",
    )(x_ctx, x_lat, oa_ctx, oa_lat, yr_ctx, yr_lat, ga, gb, wpa, wpb, wo, mod, norm2_w[l:l + 1],
      wr_pad, br_col, tri)


def _row_copy(src_ref, src_row, dst_ref, dst_row, sem):
    return pltpu.make_async_copy(src_ref.at[pl.ds(src_row, 1)], dst_ref.at[pl.ds(dst_row, 1)], sem)


def _dispatch_kernel(zf_ref, pos_ref, h_ref, xs_ref, zero_ref, sem):
    tm = h_ref.shape[0]

    @pl.when(pl.program_id(0) == 0)
    def _():
        zero_ref[...] = jnp.zeros_like(zero_ref)

        def tile_copy(t):
            return pltpu.make_async_copy(zero_ref, xs_ref.at[pl.ds(t * MOE_TR, MOE_TR)], sem)

        def start(t, carry):
            @pl.when(zf_ref[t] != 0)
            def _():
                tile_copy(t).start()
            return carry

        def wait(t, carry):
            @pl.when(zf_ref[t] != 0)
            def _():
                tile_copy(t).wait()
            return carry

        lax.fori_loop(0, MOE_TILES, start, 0)
        lax.fori_loop(0, MOE_TILES, wait, 0)

    for r in range(tm):
        _row_copy(h_ref, r, xs_ref, pos_ref[0, 0, r], sem).start(priority=r % 2)
    pltpu.make_async_copy(h_ref, xs_ref.at[pl.ds(0, tm)], sem).wait()


def _dispatch(zero_flags, pos_blocks, h2):
    tm = TM_DISP
    return pl.pallas_call(
        _dispatch_kernel,
        grid_spec=pltpu.PrefetchScalarGridSpec(
            num_scalar_prefetch=1,
            grid=(T // tm,),
            in_specs=[pl.BlockSpec((1, 1, tm), lambda i, zf: (i, 0, 0), memory_space=pltpu.SMEM),
                      pl.BlockSpec((tm, MOE_W), lambda i, zf: (i, 0))],
            out_specs=pl.BlockSpec(memory_space=pl.ANY),
            scratch_shapes=[pltpu.VMEM((MOE_TR, MOE_W), f32), pltpu.SemaphoreType.DMA(())],
        ),
        out_shape=jax.ShapeDtypeStruct((MOE_ROWS, MOE_W), f32),
        compiler_params=pltpu.CompilerParams(
            dimension_semantics=("arbitrary",), vmem_limit_bytes=V7X_VMEM_LIMIT),
        name="moe_dispatch",
    )(zero_flags, pos_blocks, h2)


def _experts_kernel(l, tg_ref, first_ref, nxt_ref, nt_ref, xs_ref, wg_hbm, wu_hbm, wd_hbm,
                    y_ref, wg_ref, wu_ref, wd_ref, sg_ref, su_ref, sd_ref, sems):
    i = pl.program_id(0)
    live = i < nt_ref[0]
    g = tg_ref[i]

    def fetch(grp):
        return [pltpu.make_async_copy(w.at[l, pl.ds(grp * EXP_PER_GROUP, EXP_PER_GROUP)], s, sems.at[k])
                for k, (w, s) in enumerate(((wg_hbm, sg_ref), (wu_hbm, su_ref), (wd_hbm, sd_ref)))]

    @pl.when(i == 0)
    def _():
        for c in fetch(tg_ref[0]):
            c.start()

    @pl.when(live & (first_ref[i] != 0))
    def _():
        for c in fetch(g):
            c.wait()
        for j in range(EXP_PER_GROUP):
            wg_ref[j] = sg_ref[j].astype(bf16)
            wu_ref[j] = su_ref[j].astype(bf16)
            wd_ref[j] = sd_ref[j].astype(bf16)

        @pl.when(nxt_ref[i] >= 0)
        def _():
            for c in fetch(nxt_ref[i]):
                c.start(priority=1)

    @pl.when(live)
    def _():
        x = xs_ref[:, 0:D].astype(bf16)
        comb = xs_ref[:, D:D + LANES]
        lane = lax.broadcasted_iota(jnp.int32, comb.shape, 1)
        acc = None
        for j in range(EXP_PER_GROUP):
            w_j = jnp.sum(jnp.where(lane == g * EXP_PER_GROUP + j, comb, 0.0), axis=1, keepdims=True)
            act = _silu(_dot(x, wg_ref[j])) * _dot(x, wu_ref[j]) * w_j
            part = _dot(act.astype(bf16), wd_ref[j])
            acc = part if acc is None else acc + part
        y_ref[...] = acc

    @pl.when(i >= nt_ref[0])
    def _():
        y_ref[...] = jnp.zeros_like(y_ref)


def _experts(l, plan, xs, wg, wu, wd):
    def tile(i, *prefetch):
        return jnp.minimum(i, prefetch[-1][0] - 1)

    hbm = pl.BlockSpec(memory_space=pl.ANY)
    return pl.pallas_call(
        functools.partial(_experts_kernel, l),
        grid_spec=pltpu.PrefetchScalarGridSpec(
            num_scalar_prefetch=4,
            grid=(MOE_TILES,),
            in_specs=[pl.BlockSpec((MOE_TR, MOE_W), lambda i, *p: (tile(i, *p), 0)), hbm, hbm, hbm],
            out_specs=pl.BlockSpec((MOE_TR, D), lambda i, *p: (i, 0)),
            scratch_shapes=[pltpu.VMEM((EXP_PER_GROUP, D, D_EXP), bf16),
                            pltpu.VMEM((EXP_PER_GROUP, D, D_EXP), bf16),
                            pltpu.VMEM((EXP_PER_GROUP, D_EXP, D), bf16),
                            pltpu.VMEM((EXP_PER_GROUP, D, D_EXP), f32),
                            pltpu.VMEM((EXP_PER_GROUP, D, D_EXP), f32),
                            pltpu.VMEM((EXP_PER_GROUP, D_EXP, D), f32), pltpu.SemaphoreType.DMA((3,))],
        ),
        out_shape=jax.ShapeDtypeStruct((MOE_ROWS, D), f32),
        compiler_params=pltpu.CompilerParams(
            dimension_semantics=("arbitrary",), vmem_limit_bytes=V7X_VMEM_LIMIT),
        name="moe_experts",
    )(*plan, xs, wg, wu, wd)


def _combine_kernel(pos_ref, posn_ref, x1_ref, mod_ref, y_ref, oc_ref, ol_ref, ya_ref, yb_ref, sems):
    tm = x1_ref.shape[0]
    i = pl.program_id(0)
    n_ctx = T_CTX // tm

    def pull(p_ref, buf, sem):
        for r in range(tm):
            _row_copy(y_ref, p_ref[0, 0, r], buf, r, sem).start(priority=r % 2)

    @pl.when(i == 0)
    def _():
        pull(pos_ref, ya_ref, sems.at[0])

    def step(cur, nxt, sem_cur, sem_nxt):
        @pl.when(i + 1 < pl.num_programs(0))
        def _():
            pull(posn_ref, nxt, sem_nxt)

        pltpu.make_async_copy(y_ref.at[pl.ds(0, tm)], cur, sem_cur).wait()
        out = x1_ref[...] + mod_ref[5:6, :] * cur[...]

        @pl.when(i < n_ctx)
        def _():
            oc_ref[...] = out

        @pl.when(i >= n_ctx)
        def _():
            ol_ref[...] = out

    @pl.when(i % 2 == 0)
    def _():
        step(ya_ref, yb_ref, sems.at[0], sems.at[1])

    @pl.when(i % 2 == 1)
    def _():
        step(yb_ref, ya_ref, sems.at[1], sems.at[0])


def _combine(l, pos_blocks, x1, mod, y):
    tm = TM_DISP
    n = T // tm
    return pl.pallas_call(
        _combine_kernel,
        grid=(n,),
        in_specs=[pl.BlockSpec((1, 1, tm), lambda i: (i, 0, 0), memory_space=pltpu.SMEM),
                  pl.BlockSpec((1, 1, tm), lambda i: (jnp.minimum(i + 1, n - 1), 0, 0),
                               memory_space=pltpu.SMEM),
                  pl.BlockSpec((tm, D), lambda i: (i, 0)),
                  pl.BlockSpec((None, None, 6, D), lambda i: (l, _cond_of_tile(i, tm), 0, 0)),
                  pl.BlockSpec(memory_space=pl.ANY)],
        out_specs=_two_stream_specs(tm, D),
        out_shape=[jax.ShapeDtypeStruct((T_CTX, D), f32), jax.ShapeDtypeStruct((T_LAT, D), f32)],
        scratch_shapes=[pltpu.VMEM((tm, D), f32), pltpu.VMEM((tm, D), f32),
                        pltpu.SemaphoreType.DMA((2,))],
        compiler_params=pltpu.CompilerParams(
            dimension_semantics=("arbitrary",), vmem_limit_bytes=V7X_VMEM_LIMIT),
        name="moe_combine",
    )(pos_blocks, pos_blocks, x1, mod, y)


def _moe_plan(route, counts):
    cnt = counts[0:N_GROUPS, 0].astype(jnp.int32)
    tiles = (cnt + MOE_TR - 1) // MOE_TR
    tile_end = jnp.cumsum(tiles)
    tile_start = tile_end - tiles
    off = (tile_start * MOE_TR).astype(f32)
    groups = jnp.arange(N_GROUPS, dtype=f32)[:, None]
    pos = jnp.sum(jnp.where(route[0][None, :] == groups, off[:, None], 0.0), axis=0) + route[1]
    pos_blocks = pos.astype(jnp.int32).reshape(T // TM_DISP, 1, TM_DISP)
    tid = jnp.arange(MOE_TILES, dtype=jnp.int32)
    tg = jnp.minimum(jnp.sum(tid[:, None] >= tile_end[None, :], axis=1), N_GROUPS - 1).astype(jnp.int32)
    n_tiles = tile_end[-1:].astype(jnp.int32)
    used = tiles > 0
    is_last = jnp.any((tid[:, None] == tile_end[None, :] - 1) & used[None, :], axis=1)
    zero_flags = ((tid >= n_tiles[0]) | is_last).astype(jnp.int32)
    gid = jnp.arange(N_GROUPS, dtype=jnp.int32)
    of_tile = tg[:, None] == gid[None, :]
    first = jnp.any((tid[:, None] == tile_start[None, :]) & used[None, :], axis=1).astype(jnp.int32)
    later = (gid[None, :] > gid[:, None]) & used[None, :]
    nxt_g = jnp.min(jnp.where(later, gid[None, :], N_GROUPS), axis=1)
    nxt_g = jnp.where(nxt_g == N_GROUPS, -1, nxt_g)
    nxt = jnp.sum(jnp.where(of_tile, nxt_g[None, :], 0), axis=1).astype(jnp.int32)
    return pos_blocks, zero_flags, (tg, first, nxt, n_tiles)


def _rope_tables():
    pos = np.arange(DEC_SEQ)
    half = HD_A // 4
    freqs = ROPE_BASE ** (-np.arange(half, dtype=np.float64) / half)
    ang_r = (pos // GRID_W)[:, None] * freqs[None, :]
    ang_c = (pos % GRID_W)[:, None] * freqs[None, :]
    ang = np.concatenate([ang_r, ang_r, ang_c, ang_c], axis=1)
    sign = np.concatenate([-np.ones(half), np.ones(half)] * 2)[None, :]
    cos = np.tile(np.cos(ang), (1, 2))
    sin = np.tile(np.sin(ang) * sign, (1, 2))
    ident_c = np.ones((TM_PROJ, LANES))
    ident_s = np.zeros((TM_PROJ, LANES))
    return (jnp.asarray(np.concatenate([cos, ident_c]), f32),
            jnp.asarray(np.concatenate([sin, ident_s]), f32))


def kernel(x_prompt, x_sample, c, cache_k, cache_v, state_ret_fwd, state_ret_bwd, c_ctx,
           norm1_w, norm2_w, w_ada, b_ada, w_in, q_norm_w, k_norm_w, attn_sink,
           ret_decay_fwd, ret_decay_bwd, ret_gn_w, w_pa, w_pb, w_o, w_router, b_router,
           w_exp_gate, w_exp_up, w_exp_down):
    x_ctx, x_lat = x_prompt.reshape(T_CTX, D), x_sample.reshape(T_LAT, D)
    cond8 = jnp.zeros((N_COND, D), f32).at[0].set(c_ctx).at[1:1 + DEC_BATCH].set(c)
    mod = _adaln(cond8, w_ada, b_ada).reshape(DEPTH, N_COND, 6, D)

    cos_t, sin_t = _rope_tables()
    blk = np.arange(Q_A) // HD_A
    ones_blk = jnp.asarray((blk[:, None] == blk[None, :]) / HD_A, bf16)
    tri = jnp.asarray(np.triu(np.ones((POST_SUB, POST_SUB)), 1), bf16)
    qnw = jnp.tile(q_norm_w, (1, NH_A))
    knw = jnp.tile(k_norm_w, (1, NKV_A))
    wr_pad = jnp.pad(w_router, ((0, 0), (0, LANES - N_EXP)))
    br_col = jnp.broadcast_to(b_router[:, None], (N_EXP, LANES))
    ck = cache_k.reshape(DEC_BATCH, DEPTH, PAST, KV_A)
    cv = cache_v.reshape(DEC_BATCH, DEPTH, PAST, KV_A)
    s0 = jnp.concatenate([state_ret_fwd, state_ret_bwd], axis=3)
    lg = jnp.log1p(-jnp.exp(jnp.concatenate([ret_decay_fwd, ret_decay_bwd], axis=1).astype(f32)))

    new_k, new_v, new_sf, new_sb = [], [], [], []
    for l in range(DEPTH):
        qa, ka, va, qr, kr, vr, gr, ga, gb = _inproj(l, x_ctx, x_lat, norm1_w, mod, w_in, cos_t, sin_t,
                                                     ones_blk, qnw, knw)
        oa_ctx = _ctx_attn(l, qa, ka, va, attn_sink)
        oa_lat = _lat_attn(l, qa, ka, va, ck, cv, cos_t, sin_t, attn_sink)
        yr_ctx, sf, sb = _retention(l, qr, kr, vr, gr, None, lg, ret_gn_w, BATCH, SEQ, 0)
        yr_lat, _, _ = _retention(l, qr, kr, vr, gr, s0[:, l], lg, ret_gn_w, DEC_BATCH, DEC_SEQ,
                                  T_CTX // DEC_SEQ)
        x1, h2, route, counts = _post(l, x_ctx, x_lat, oa_ctx, oa_lat, yr_ctx, yr_lat, ga, gb,
                                      w_pa, w_pb, w_o, mod, norm2_w, wr_pad, br_col, tri)
        pos_blocks, zero_flags, plan = _moe_plan(route, counts)
        xs = _dispatch(zero_flags, pos_blocks, h2)
        y = _experts(l, plan, xs, w_exp_gate, w_exp_up, w_exp_down)
        x_ctx, x_lat = _combine(l, pos_blocks, x1, mod, y)

        new_k.append(ka[:T_CTX].reshape(BATCH, SEQ, NKV_A, HD_A))
        new_v.append(va[:T_CTX].reshape(BATCH, SEQ, NKV_A, HD_A))
        new_sf.append(sf)
        new_sb.append(sb)

    return (x_ctx.reshape(BATCH, SEQ, D), x_lat.reshape(DEC_BATCH, DEC_SEQ, D),
            jnp.stack(new_k, axis=1), jnp.stack(new_v, axis=1),
            jnp.stack(new_sf, axis=1), jnp.stack(new_sb, axis=1))
```

```python
import functools

import numpy as np
import jax
import jax.numpy as jnp
from jax import lax
from jax.experimental import pallas as pl
from jax.experimental.pallas import tpu as pltpu

D = 1024
BATCH, SEQ = 16, 256
DEC_BATCH, DEC_SEQ = 2, 2048
DEPTH = 2
PAST = 512
GRID_W = 64
NH_A, NKV_A, HD_A = 8, 2, 64
WINDOW = 128
NH_R, DK_R, DV_R = 4, 64, 128
CHUNK = 128
N_EXP, N_GROUPS, EXP_PER_GROUP = 16, 4, 4
D_EXP = 512
ROPE_BASE = 10000.0
EPS = 1e-6
NEG_INF = -1e30

Q_A = NH_A * HD_A
KV_A = NKV_A * HD_A
QK_R = NH_R * DK_R
V_R = NH_R * DV_R
C_QA = (0, Q_A)
C_KA = (C_QA[1], C_QA[1] + KV_A)
C_VA = (C_KA[1], C_KA[1] + KV_A)
C_QR = (C_VA[1], C_VA[1] + QK_R)
C_KR = (C_QR[1], C_QR[1] + QK_R)
C_VR = (C_KR[1], C_KR[1] + V_R)
C_GR = (C_VR[1], C_VR[1] + V_R)
C_GA = (C_GR[1], C_GR[1] + D)
C_GB = (C_GA[1], C_GA[1] + D)
D_IN = C_GB[1]

T_CTX = BATCH * SEQ
T_LAT = DEC_BATCH * DEC_SEQ
T = T_CTX + T_LAT
N_COND = 8

LANES = 128
V7X_VMEM_LIMIT = 56 * 1024 * 1024

TM_PROJ = 512
TM_POST = 512
POST_SUB = 256
W_CHUNK = 256
TM_DISP = 512
MOE_TR = 256
PAIRS = EXP_PER_GROUP * (EXP_PER_GROUP - 1) // 2
N_BUCKETS = N_GROUPS * PAIRS
BUCKET_ROWS = 32
MOE_TILES = T // MOE_TR + N_BUCKETS
MOE_ROWS = MOE_TILES * MOE_TR
MOE_W = D + LANES
ATT_QB = 256

f32 = jnp.float32
bf16 = jnp.bfloat16


def _dot(a, b):
    return jnp.dot(a, b, preferred_element_type=f32)


def _dot_t(a, b):
    return lax.dot_general(a, b, (((1,), (1,)), ((), ())), preferred_element_type=f32)


def _split(x):
    hi = x.astype(bf16)
    lo = (x - hi.astype(f32)).astype(bf16)
    return hi, lo


def _dot3(a, b):
    ah, al = _split(a)
    bh, bl = _split(b)
    return _dot(ah, bh) + (_dot(ah, bl) + _dot(al, bh))


def _sigmoid(x):
    return 1.0 / (1.0 + jnp.exp(-x))


def _silu(x):
    return x * _sigmoid(x)


def _cond_of_tile(i, tm):
    n_ctx = T_CTX // tm
    per_b = DEC_SEQ // tm
    return jnp.where(i < n_ctx, 0, 1 + jnp.maximum(i - n_ctx, 0) // per_b)


def _adaln_kernel(cond_ref, w_ref, b_ref, o_ref):
    a = _silu(cond_ref[...])
    o_ref[0] = _dot3(a, w_ref[0]) + b_ref[0]


def _adaln(cond8, w_ada, b_ada):
    tn = 1536
    return pl.pallas_call(
        _adaln_kernel,
        grid=(DEPTH, 6 * D // tn),
        in_specs=[
            pl.BlockSpec((N_COND, D), lambda l, j: (0, 0)),
            pl.BlockSpec((1, D, tn), lambda l, j: (l, 0, j)),
            pl.BlockSpec((1, 1, tn), lambda l, j: (l, 0, j)),
        ],
        out_specs=pl.BlockSpec((1, N_COND, tn), lambda l, j: (l, 0, j)),
        out_shape=jax.ShapeDtypeStruct((DEPTH, N_COND, 6 * D), f32),
        compiler_params=pltpu.CompilerParams(
            dimension_semantics=("parallel", "parallel"), vmem_limit_bytes=V7X_VMEM_LIMIT),
        name="adaln",
    )(cond8, w_ada, b_ada.reshape(DEPTH, 1, 6 * D))


def _rope(x, cos, sin_signed, first_half):
    fwd = pltpu.roll(x, 16, 1)
    bwd = pltpu.roll(x, LANES - 16, 1)
    partner = jnp.where(first_half, bwd, fwd)
    return x * cos + partner * sin_signed


def _head_rms(x, ones_blk, w):
    n = x.shape[1]
    sq_hi, sq_lo = _split(x * x)
    blk = ones_blk[0:n, 0:n]
    mean = _dot(sq_hi, blk) + _dot(sq_lo, blk)
    return x * lax.rsqrt(mean + EPS) * w


def _load_cast(w_hbm, dst_ref, stage_ref, sems):
    k, n = w_hbm.shape

    def chunk_copy(c):
        return pltpu.make_async_copy(w_hbm.at[:, pl.ds(c * W_CHUNK, W_CHUNK)],
                                     stage_ref.at[c % 2, pl.ds(0, k)], sems.at[c % 2])

    n_chunks = n // W_CHUNK
    chunk_copy(0).start()
    for c in range(n_chunks):
        if c + 1 < n_chunks:
            chunk_copy(c + 1).start()
        chunk_copy(c).wait()
        dst_ref[:, c * W_CHUNK:(c + 1) * W_CHUNK] = stage_ref[c % 2, 0:k, :].astype(bf16)


def _tile_of_two(i, n_first, a_ref, b_ref):
    return jnp.where(i < n_first, a_ref[...], b_ref[...])


def _inproj_kernel(l, xc_ref, xl_ref, n1_ref, mod_ref, w_hbm, cos_ref, sin_ref, ones_ref, qnw_ref,
                   knw_ref, qa_ref, ka_ref, va_ref, qr_ref, kr_ref, vr_ref, gr_ref, ga_ref, gb_ref,
                   w_ref, stage_ref, sems):
    i = pl.program_id(0)

    @pl.when(i == 0)
    def _():
        _load_cast(w_hbm.at[l], w_ref, stage_ref, sems)

    x = _tile_of_two(i, T_CTX // xc_ref.shape[0], xc_ref, xl_ref)
    y = x * lax.rsqrt(jnp.mean(x * x, axis=-1, keepdims=True) + EPS) * n1_ref[...]
    h = (y * (1.0 + mod_ref[1:2, :]) + mod_ref[0:1, :]).astype(bf16)

    def proj(c):
        return _dot(h, w_ref[:, c[0]:c[1]])

    cos = cos_ref[...]
    sin = sin_ref[...]
    lane = lax.broadcasted_iota(jnp.int32, cos.shape, 1)
    first_half = (lane % 32) < 16

    def rope_all(v):
        parts = [_rope(v[:, j:j + LANES], cos, sin, first_half) for j in range(0, v.shape[1], LANES)]
        return parts[0] if len(parts) == 1 else jnp.concatenate(parts, axis=1)

    ones_blk = ones_ref[...]
    qa = _head_rms(proj(C_QA), ones_blk, qnw_ref[...])
    qa_ref[...] = (rope_all(qa) * (HD_A ** -0.5)).astype(bf16)
    ka_ref[...] = _head_rms(proj(C_KA), ones_blk, knw_ref[...])
    va_ref[...] = proj(C_VA)
    qr_ref[...] = rope_all(proj(C_QR)).astype(bf16)
    kr_ref[...] = (rope_all(proj(C_KR)) * (DK_R ** -0.5)).astype(bf16)
    vr_ref[...] = proj(C_VR).astype(bf16)
    gr_ref[...] = proj(C_GR).astype(bf16)
    ga_ref[...] = proj(C_GA).astype(bf16)
    gb_ref[...] = proj(C_GB).astype(bf16)


def _two_stream_specs(tm, width):
    n_ctx = T_CTX // tm
    return [pl.BlockSpec((tm, width), lambda i, *_: (jnp.minimum(i, n_ctx - 1), 0)),
            pl.BlockSpec((tm, width), lambda i, *_: (jnp.maximum(i - n_ctx, 0), 0))]


def _inproj(l, x_ctx, x_lat, norm1_w, mod, w_in, cos_t, sin_t, ones_blk, qnw, knw):
    tm = TM_PROJ
    n_ctx = T_CTX // tm
    per_b = DEC_SEQ // tm

    def tab_map(i):
        return (jnp.where(i < n_ctx, per_b, jnp.maximum(i - n_ctx, 0) % per_b), 0)

    row = lambda i: (i, 0)
    const = lambda i: (0, 0)
    widths = [(Q_A, bf16), (KV_A, f32), (KV_A, f32), (QK_R, bf16), (QK_R, bf16), (V_R, bf16),
              (V_R, bf16), (D, bf16), (D, bf16)]
    return pl.pallas_call(
        functools.partial(_inproj_kernel, l),
        grid=(T // tm,),
        in_specs=_two_stream_specs(tm, D) + [
            pl.BlockSpec((1, D), const),
            pl.BlockSpec((None, None, 6, D), lambda i: (l, _cond_of_tile(i, tm), 0, 0)),
            pl.BlockSpec(memory_space=pl.ANY),
            pl.BlockSpec((tm, LANES), tab_map),
            pl.BlockSpec((tm, LANES), tab_map),
            pl.BlockSpec((Q_A, Q_A), const),
            pl.BlockSpec((1, Q_A), const),
            pl.BlockSpec((1, KV_A), const),
        ],
        out_specs=[pl.BlockSpec((tm, w), row) for w, _ in widths],
        out_shape=[jax.ShapeDtypeStruct((T, w), dt) for w, dt in widths],
        scratch_shapes=[pltpu.VMEM((D, D_IN), bf16), pltpu.VMEM((2, D, W_CHUNK), f32),
                        pltpu.SemaphoreType.DMA((2,))],
        compiler_params=pltpu.CompilerParams(
            dimension_semantics=("arbitrary",), vmem_limit_bytes=V7X_VMEM_LIMIT),
        name="inproj",
    )(x_ctx, x_lat, norm1_w[l:l + 1], mod, w_in, cos_t, sin_t, ones_blk, qnw[l:l + 1], knw[l:l + 1])


def _head_blocks(t, kv, lo_mask):
    r = pltpu.roll(t, HD_A, 1)
    if kv == 0:
        a = jnp.where(lo_mask, t, 0.0)
        b = jnp.where(lo_mask, 0.0, r)
    else:
        a = jnp.where(lo_mask, r, 0.0)
        b = jnp.where(lo_mask, 0.0, t)
    return jnp.concatenate([a, b], axis=0).astype(bf16)


def _ctx_attn_kernel(sink_ref, q_ref, k_ref, v_ref, o_ref):
    k = k_ref[...]
    v = v_ref[...]
    n = k.shape[0]
    lo_mask = lax.broadcasted_iota(jnp.int32, k.shape, 1) < HD_A
    for kv in range(NKV_A):
        kblk = _head_blocks(k, kv, lo_mask)
        vblk = _head_blocks(v, kv, lo_mask)
        for pr in range(2):
            pi = kv * 2 + pr
            s = _dot_t(q_ref[:, pi * LANES:(pi + 1) * LANES], kblk)
            ps, invs = [], []
            for hh in range(2):
                sk = sink_ref[2 * pi + hh]
                sh = s[:, hh * n:(hh + 1) * n]
                m = jnp.maximum(jnp.max(sh, axis=-1, keepdims=True), sk)
                p = jnp.exp(sh - m)
                invs.append(1.0 / (jnp.sum(p, axis=-1, keepdims=True) + jnp.exp(sk - m)))
                ps.append(p.astype(bf16))
            o = _dot(jnp.concatenate(ps, axis=1), vblk) * jnp.where(lo_mask, invs[0], invs[1])
            o_ref[:, pi * LANES:(pi + 1) * LANES] = o.astype(bf16)


def _ctx_attn(l, qa, ka, va, sink):
    blk = lambda w: pl.BlockSpec((SEQ, w), lambda b: (b, 0))
    return pl.pallas_call(
        _ctx_attn_kernel,
        grid=(BATCH,),
        in_specs=[pl.BlockSpec(memory_space=pltpu.SMEM), blk(Q_A), blk(KV_A), blk(KV_A)],
        out_specs=blk(Q_A),
        out_shape=jax.ShapeDtypeStruct((T_CTX, Q_A), bf16),
        compiler_params=pltpu.CompilerParams(
            dimension_semantics=("parallel",), vmem_limit_bytes=V7X_VMEM_LIMIT),
        name="ctx_attn",
    )(sink[l], qa, ka, va)


def _lat_attn_kernel(sink_ref, q_ref, k_ref, v_ref, kc_ref, vc_ref, cos_ref, sin_ref, o_ref):
    j = pl.program_id(1)
    qb = ATT_QB
    win = 2 * qb
    ws = pl.multiple_of(jnp.clip(j * qb - WINDOW, 0, DEC_SEQ - win), WINDOW)
    lo_mask = lax.broadcasted_iota(jnp.int32, (win, LANES), 1) < HD_A
    lane = lax.broadcasted_iota(jnp.int32, (win, LANES), 1)
    kw = _rope(k_ref[pl.ds(ws, win), :], cos_ref[pl.ds(ws, win), :], sin_ref[pl.ds(ws, win), :],
               (lane % 32) < 16)
    vw = v_ref[pl.ds(ws, win), :]
    kc = kc_ref[...]
    vc = vc_ref[...]
    qpos = j * qb + (lax.broadcasted_iota(jnp.int32, (2 * qb, win), 0) & (qb - 1))
    kpos = ws + lax.broadcasted_iota(jnp.int32, (2 * qb, win), 1)
    valid = jnp.abs(qpos - kpos) <= WINDOW
    out_lo = lax.broadcasted_iota(jnp.int32, (2 * qb, LANES), 1) < HD_A
    for kv in range(NKV_A):
        kc_blk = _head_blocks(kc, kv, lo_mask[:PAST])
        vc_blk = _head_blocks(vc, kv, lo_mask[:PAST])
        kw_blk = _head_blocks(kw, kv, lo_mask)
        vw_blk = _head_blocks(vw, kv, lo_mask)
        q2 = jnp.concatenate([q_ref[:, (2 * kv) * LANES:(2 * kv + 1) * LANES],
                              q_ref[:, (2 * kv + 1) * LANES:(2 * kv + 2) * LANES]], axis=0)
        s_c = _dot_t(q2, kc_blk)
        s_w = _dot_t(q2, kw_blk)
        pcs, pws, invs = [], [], []
        for hh in range(2):
            row = lax.broadcasted_iota(jnp.int32, (2 * qb, 1), 0)
            sk = jnp.where(row < qb, sink_ref[4 * kv + hh], sink_ref[4 * kv + 2 + hh])
            sc = s_c[:, hh * PAST:(hh + 1) * PAST]
            sw = jnp.where(valid, s_w[:, hh * win:(hh + 1) * win], NEG_INF)
            m = jnp.maximum(jnp.maximum(jnp.max(sc, axis=-1, keepdims=True),
                                        jnp.max(sw, axis=-1, keepdims=True)), sk)
            pc = jnp.exp(sc - m)
            pw = jnp.exp(sw - m)
            den = (jnp.sum(pc, axis=-1, keepdims=True) + jnp.sum(pw, axis=-1, keepdims=True)
                   + jnp.exp(sk - m))
            invs.append(1.0 / den)
            pcs.append(pc.astype(bf16))
            pws.append(pw.astype(bf16))
        o = _dot(jnp.concatenate(pcs, axis=1), vc_blk) + _dot(jnp.concatenate(pws, axis=1), vw_blk)
        o = o * jnp.where(out_lo, invs[0], invs[1])
        o_ref[:, (2 * kv) * LANES:(2 * kv + 1) * LANES] = o[:qb].astype(bf16)
        o_ref[:, (2 * kv + 1) * LANES:(2 * kv + 2) * LANES] = o[qb:].astype(bf16)


def _lat_attn(l, qa, ka, va, cache_k, cache_v, cos_l, sin_l, sink):
    qb = ATT_QB
    nq = DEC_SEQ // qb
    ctx_blocks = T_CTX // DEC_SEQ
    seq = lambda b, j: (ctx_blocks + b, 0)
    return pl.pallas_call(
        _lat_attn_kernel,
        grid=(DEC_BATCH, nq),
        in_specs=[
            pl.BlockSpec(memory_space=pltpu.SMEM),
            pl.BlockSpec((qb, Q_A), lambda b, j: (T_CTX // qb + b * nq + j, 0)),
            pl.BlockSpec((DEC_SEQ, KV_A), seq),
            pl.BlockSpec((DEC_SEQ, KV_A), seq),
            pl.BlockSpec((None, None, PAST, KV_A), lambda b, j: (b, l, 0, 0)),
            pl.BlockSpec((None, None, PAST, KV_A), lambda b, j: (b, l, 0, 0)),
            pl.BlockSpec((DEC_SEQ, LANES), lambda b, j: (0, 0)),
            pl.BlockSpec((DEC_SEQ, LANES), lambda b, j: (0, 0)),
        ],
        out_specs=pl.BlockSpec((qb, Q_A), lambda b, j: (b * nq + j, 0)),
        out_shape=jax.ShapeDtypeStruct((T_LAT, Q_A), bf16),
        compiler_params=pltpu.CompilerParams(
            dimension_semantics=("parallel", "parallel"), vmem_limit_bytes=V7X_VMEM_LIMIT),
        name="lat_attn",
    )(sink[l], qa, ka, va, cache_k, cache_v, cos_l, sin_l)


def _dup_heads(pair, lo_mask):
    r = pltpu.roll(pair, DK_R, 1)
    return jnp.where(lo_mask, pair, r), jnp.where(lo_mask, r, pair)


def _retention_kernel(has_s0, n_chunks, *refs):
    if has_s0:
        (lg_ref, q_ref, k_ref, v_ref, g_ref, s0_ref, gnw_ref,
         y_ref, sf_ref, sb_ref, ds_ref, st_ref, mask_ref, qdec_ref, kdec_ref, cdec_ref) = refs
    else:
        (lg_ref, q_ref, k_ref, v_ref, g_ref, gnw_ref,
         y_ref, sf_ref, sb_ref, ds_ref, st_ref, mask_ref, qdec_ref, kdec_ref, cdec_ref) = refs
        s0_ref = None
    C = CHUNK
    lo_mask = lax.broadcasted_iota(jnp.int32, (C, LANES), 1) < DK_R

    ri = lax.broadcasted_iota(jnp.int32, (C, C), 0).astype(f32)
    diff = ri - lax.broadcasted_iota(jnp.int32, (C, C), 1).astype(f32)
    for h in range(NH_R):
        lg_f, lg_b = lg_ref[h], lg_ref[NH_R + h]
        mask_ref[h] = (jnp.where(diff >= 0, jnp.exp(jnp.maximum(diff, 0.0) * lg_f), 0.0)
                       + jnp.where(diff <= 0, jnp.exp(jnp.maximum(-diff, 0.0) * lg_b), 0.0))
        qdec_ref[:, h * LANES:(h + 1) * LANES] = jnp.exp(
            jnp.where(lo_mask, (ri + 1.0) * lg_f, (C - ri) * lg_b))
        kdec_ref[:, h * LANES:(h + 1) * LANES] = jnp.exp(
            jnp.where(lo_mask, (C - 1.0 - ri) * lg_f, ri * lg_b))
        cdec_ref[h] = jnp.exp(jnp.where(ri < DK_R, C * lg_f, C * lg_b))

    def inc_body(c, carry):
        r0 = pl.multiple_of(c * C, C)
        for pr in range(2):
            kp = k_ref[pl.ds(r0, C), pr * LANES:(pr + 1) * LANES].astype(f32)
            for hh, kd in enumerate(_dup_heads(kp, lo_mask)):
                h = 2 * pr + hh
                kd = (kd * kdec_ref[:, h * LANES:(h + 1) * LANES]).astype(bf16)
                vh = v_ref[pl.ds(r0, C), h * DV_R:(h + 1) * DV_R]
                ds_ref[c, h] = lax.dot_general(kd, vh, (((0,), (0,)), ((), ())),
                                               preferred_element_type=f32)
        return carry

    lax.fori_loop(0, n_chunks, inc_body, 0)

    for h in range(NH_R):
        cf = cdec_ref[h, 0:DK_R, :]
        cb = cdec_ref[h, DK_R:2 * DK_R, :]
        if has_s0:
            init_f = s0_ref[h, 0:DK_R, :]
            init_b = s0_ref[h, DK_R:2 * DK_R, :]
        else:
            init_f = jnp.zeros((DK_R, DV_R), f32)
            init_b = init_f

        def fwd_body(c, s, h=h, cf=cf):
            st_ref[c, h, 0:DK_R, :] = s
            return s * cf + ds_ref[c, h, 0:DK_R, :]

        def bwd_body(i, s, h=h, cb=cb):
            c = n_chunks - 1 - i
            st_ref[c, h, DK_R:2 * DK_R, :] = s
            return s * cb + ds_ref[c, h, DK_R:2 * DK_R, :]

        sf_ref[h] = lax.fori_loop(0, n_chunks, fwd_body, init_f)
        sb_ref[h] = lax.fori_loop(0, n_chunks, bwd_body, init_b)

    def out_body(c, carry):
        r0 = pl.multiple_of(c * C, C)
        for pr in range(2):
            qp = q_ref[pl.ds(r0, C), pr * LANES:(pr + 1) * LANES]
            kp = k_ref[pl.ds(r0, C), pr * LANES:(pr + 1) * LANES].astype(f32)
            kblk = jnp.concatenate([jnp.where(lo_mask, kp, 0.0), jnp.where(lo_mask, 0.0, kp)],
                                   axis=0).astype(bf16)
            a2 = _dot_t(qp, kblk)
            for hh, qd in enumerate(_dup_heads(qp.astype(f32), lo_mask)):
                h = 2 * pr + hh
                a = (a2[:, hh * C:(hh + 1) * C] * mask_ref[h]).astype(bf16)
                vh = v_ref[pl.ds(r0, C), h * DV_R:(h + 1) * DV_R]
                qd = (qd * qdec_ref[:, h * LANES:(h + 1) * LANES]).astype(bf16)
                o = _dot(a, vh) + _dot(qd, st_ref[c, h].astype(bf16))
                mu = jnp.mean(o, axis=-1, keepdims=True)
                d = o - mu
                var = jnp.mean(d * d, axis=-1, keepdims=True)
                yh = d * lax.rsqrt(var + EPS) * gnw_ref[:, h * DV_R:(h + 1) * DV_R]
                g = g_ref[pl.ds(r0, C), h * DV_R:(h + 1) * DV_R].astype(f32)
                y_ref[pl.ds(r0, C), h * DV_R:(h + 1) * DV_R] = (yh * _silu(g)).astype(bf16)
        return carry

    lax.fori_loop(0, n_chunks, out_body, 0)


def _retention(l, qr, kr, vr, gr, s0, lg, gnw, nb, seq, row_block0):
    n_chunks = seq // CHUNK
    has_s0 = s0 is not None
    tok = lambda w: pl.BlockSpec((seq, w), lambda b: (row_block0 + b, 0))
    in_specs = [pl.BlockSpec(memory_space=pltpu.SMEM), tok(QK_R), tok(QK_R), tok(V_R), tok(V_R)]
    args = [lg[l], qr, kr, vr, gr]
    if has_s0:
        in_specs.append(pl.BlockSpec((None, NH_R, 2 * DK_R, DV_R), lambda b: (b, 0, 0, 0)))
        args.append(s0)
    in_specs += [pl.BlockSpec((1, V_R), lambda b: (0, 0))]
    args += [gnw[l:l + 1]]
    st_spec = pl.BlockSpec((None, NH_R, DK_R, DV_R), lambda b: (b, 0, 0, 0))
    return pl.pallas_call(
        functools.partial(_retention_kernel, has_s0, n_chunks),
        grid=(nb,),
        in_specs=in_specs,
        out_specs=[pl.BlockSpec((seq, V_R), lambda b: (b, 0)), st_spec, st_spec],
        out_shape=[jax.ShapeDtypeStruct((nb * seq, V_R), bf16),
                   jax.ShapeDtypeStruct((nb, NH_R, DK_R, DV_R), f32),
                   jax.ShapeDtypeStruct((nb, NH_R, DK_R, DV_R), f32)],
        scratch_shapes=[pltpu.VMEM((n_chunks, NH_R, 2 * DK_R, DV_R), f32),
                        pltpu.VMEM((n_chunks, NH_R, 2 * DK_R, DV_R), f32),
                        pltpu.VMEM((NH_R, CHUNK, CHUNK), f32), pltpu.VMEM((CHUNK, NH_R * LANES), f32),
                        pltpu.VMEM((CHUNK, NH_R * LANES), f32), pltpu.VMEM((NH_R, 2 * DK_R, DV_R), f32)],
        compiler_params=pltpu.CompilerParams(
            dimension_semantics=("parallel",), vmem_limit_bytes=V7X_VMEM_LIMIT),
        name="retention_lat" if has_s0 else "retention_ctx",
    )(*args)


def _post_kernel(l, xc_ref, xl_ref, oac_ref, oal_ref, yrc_ref, yrl_ref, ga_ref, gb_ref, wpa_hbm, wpb_hbm,
                 wo_hbm, mod_ref, n2_ref, wr_ref, br_ref, tri_ref, x1_ref, h2_ref, route_ref, cnt_ref,
                 carry_ref, wpa_ref, wpb_ref, wo_ref, stage_ref, sems):
    i = pl.program_id(0)
    tm = xc_ref.shape[0]
    n_ctx = T_CTX // tm

    @pl.when(i == 0)
    def _():
        carry_ref[...] = jnp.zeros_like(carry_ref)
        _load_cast(wpa_hbm.at[l], wpa_ref, stage_ref, sems)
        _load_cast(wpb_hbm.at[l], wpb_ref, stage_ref, sems)
        _load_cast(wo_hbm.at[l], wo_ref, stage_ref, sems)

    for r0 in range(0, tm, POST_SUB):
        _post_subtile(i < n_ctx, slice(r0, r0 + POST_SUB), xc_ref, xl_ref, oac_ref, oal_ref, yrc_ref,
                      yrl_ref, ga_ref, gb_ref, mod_ref, n2_ref, wr_ref, br_ref, tri_ref, x1_ref, h2_ref,
                      route_ref, carry_ref, wpa_ref, wpb_ref, wo_ref)
    cnt_ref[...] = carry_ref[...]


def _post_subtile(is_ctx, rows, xc_ref, xl_ref, oac_ref, oal_ref, yrc_ref, yrl_ref, ga_ref, gb_ref, mod_ref,
                  n2_ref, wr_ref, br_ref, tri_ref, x1_ref, h2_ref, route_ref, carry_ref, wpa_ref, wpb_ref,
                  wo_ref):
    tm = rows.stop - rows.start
    pick = lambda a_ref, b_ref: jnp.where(is_ctx, a_ref[rows, :], b_ref[rows, :])
    ga = _sigmoid(ga_ref[rows, :].astype(f32))
    gb = _sigmoid(gb_ref[rows, :].astype(f32))
    merged = (ga * _dot(pick(oac_ref, oal_ref), wpa_ref[...])
              + gb * _dot(pick(yrc_ref, yrl_ref), wpb_ref[...]))
    mix = _dot(merged.astype(bf16), wo_ref[...])
    x1 = pick(xc_ref, xl_ref) + mod_ref[2:3, :] * mix
    x1_ref[rows, :] = x1
    y = x1 * lax.rsqrt(jnp.mean(x1 * x1, axis=-1, keepdims=True) + EPS) * n2_ref[...]
    h2 = y * (1.0 + mod_ref[4:5, :]) + mod_ref[3:4, :]
    h2_ref[rows, 0:D] = h2

    logits = _dot3(h2, wr_ref[...])
    lt = logits.T[0:N_EXP, :]
    scores = _sigmoid(lt)
    sel = scores + br_ref[:, 0:1]
    row = lax.broadcasted_iota(jnp.int32, (N_EXP, tm), 0)

    best = None
    bg = None
    for g in range(N_GROUPS):
        a, b, c, d = (sel[EXP_PER_GROUP * g + k:EXP_PER_GROUP * g + k + 1, :] for k in range(4))
        p, q = jnp.maximum(a, b), jnp.minimum(a, b)
        r, s = jnp.maximum(c, d), jnp.minimum(c, d)
        gs = jnp.maximum(p, r) + jnp.maximum(jnp.minimum(p, r), jnp.maximum(q, s))
        if g == 0:
            best, bg = gs, jnp.zeros((1, tm), jnp.int32)
        else:
            upd = gs > best
            bg = jnp.where(upd, g, bg)
            best = jnp.where(upd, gs, best)
    masked = jnp.where(jnp.right_shift(row, 2) == bg, sel, NEG_INF)
    m1 = jnp.max(masked, axis=0, keepdims=True)
    i1 = jnp.min(jnp.where(masked == m1, row, N_EXP), axis=0, keepdims=True)
    masked2 = jnp.where(row == i1, NEG_INF, masked)
    m2 = jnp.max(masked2, axis=0, keepdims=True)
    i2 = jnp.min(jnp.where(masked2 == m2, row, N_EXP), axis=0, keepdims=True)
    oh1 = row == i1
    oh2 = row == i2
    s1 = jnp.sum(jnp.where(oh1, scores, 0.0), axis=0, keepdims=True)
    s2 = jnp.sum(jnp.where(oh2, scores, 0.0), axis=0, keepdims=True)
    den = s1 + s2

    comb = jnp.where(oh1, s1 / den, 0.0) + jnp.where(oh2, s2 / den, 0.0)
    comb_t = jnp.concatenate([comb, jnp.zeros((LANES - N_EXP, tm), f32)], axis=0).T
    h2_ref[rows, D:D + LANES] = comb_t

    e_lo = jnp.minimum(i1, i2) - EXP_PER_GROUP * bg
    e_hi = jnp.maximum(i1, i2) - EXP_PER_GROUP * bg
    pair = jnp.right_shift(e_lo * (7 - e_lo), 1) + (e_hi - e_lo - 1)
    bucket = bg * PAIRS + pair
    rowb = lax.broadcasted_iota(jnp.int32, (BUCKET_ROWS, tm), 0)
    ohb = rowb == bucket
    ohb_f = jnp.where(ohb, 1.0, 0.0)
    tot = carry_ref[:, 0:1] + _dot(ohb_f.astype(bf16), tri_ref[...])
    rank = jnp.sum(jnp.where(ohb, tot, 0.0), axis=0, keepdims=True)
    carry_ref[...] = carry_ref[...] + jnp.sum(ohb_f, axis=1, keepdims=True)

    route_ref[0:1, rows] = bucket.astype(f32)
    route_ref[1:2, rows] = rank
    route_ref[2:8, rows] = jnp.zeros((6, tm), f32)


def _post(l, x_ctx, x_lat, oa_ctx, oa_lat, yr_ctx, yr_lat, ga, gb, wpa, wpb, wo, mod, norm2_w, wr_pad,
          br_col, tri):
    tm = TM_POST
    row = lambda w: pl.BlockSpec((tm, w), lambda i: (i, 0))
    const = lambda a: pl.BlockSpec(a.shape, lambda i: (0,) * a.ndim)
    hbm = pl.BlockSpec(memory_space=pl.ANY)
    return pl.pallas_call(
        functools.partial(_post_kernel, l),
        grid=(T // tm,),
        in_specs=(_two_stream_specs(tm, D) + _two_stream_specs(tm, Q_A) + _two_stream_specs(tm, V_R)
                  + [row(D), row(D), hbm, hbm, hbm,
                     pl.BlockSpec((None, None, 6, D), lambda i: (l, _cond_of_tile(i, tm), 0, 0)),
                     pl.BlockSpec((1, D), lambda i: (0, 0)), const(wr_pad), const(br_col), const(tri)]),
        out_specs=[row(D), row(MOE_W), pl.BlockSpec((8, tm), lambda i: (0, i)),
                   pl.BlockSpec((BUCKET_ROWS, LANES), lambda i: (0, 0))],
        out_shape=[jax.ShapeDtypeStruct((T, D), f32), jax.ShapeDtypeStruct((T, MOE_W), f32),
                   jax.ShapeDtypeStruct((8, T), f32), jax.ShapeDtypeStruct((BUCKET_ROWS, LANES), f32)],
        scratch_shapes=[pltpu.VMEM((BUCKET_ROWS, LANES), f32), pltpu.VMEM((Q_A, D), bf16),
                        pltpu.VMEM((V_R, D), bf16), pltpu.VMEM((D, D), bf16),
                        pltpu.VMEM((2, D, W_CHUNK), f32), pltpu.SemaphoreType.DMA((2,))],
        compiler_params=pltpu.CompilerParams(
            dimension_semantics=("arbitrary",), vmem_limit_bytes=V7X_VMEM_LIMIT),
        name="post_router",
    )(x_ctx, x_lat, oa_ctx, oa_lat, yr_ctx, yr_lat, ga, gb, wpa, wpb, wo, mod, norm2_w[l:l + 1],
      wr_pad, br_col, tri)


def _row_copy(src_ref, src_row, dst_ref, dst_row, sem):
    return pltpu.make_async_copy(src_ref.at[pl.ds(src_row, 1)], dst_ref.at[pl.ds(dst_row, 1)], sem)


def _dispatch_kernel(zf_ref, pos_ref, h_ref, xs_ref, zero_ref, sem):
    tm = h_ref.shape[0]

    @pl.when(pl.program_id(0) == 0)
    def _():
        zero_ref[...] = jnp.zeros_like(zero_ref)

        def tile_copy(t):
            return pltpu.make_async_copy(zero_ref, xs_ref.at[pl.ds(t * MOE_TR, MOE_TR)], sem)

        def start(t, carry):
            @pl.when(zf_ref[t] != 0)
            def _():
                tile_copy(t).start()
            return carry

        def wait(t, carry):
            @pl.when(zf_ref[t] != 0)
            def _():
                tile_copy(t).wait()
            return carry

        lax.fori_loop(0, MOE_TILES, start, 0)
        lax.fori_loop(0, MOE_TILES, wait, 0)

    for r in range(tm):
        _row_copy(h_ref, r, xs_ref, pos_ref[0, 0, r], sem).start(priority=r % 2)
    pltpu.make_async_copy(h_ref, xs_ref.at[pl.ds(0, tm)], sem).wait()


def _dispatch(zero_flags, pos_blocks, h2):
    tm = TM_DISP
    return pl.pallas_call(
        _dispatch_kernel,
        grid_spec=pltpu.PrefetchScalarGridSpec(
            num_scalar_prefetch=1,
            grid=(T // tm,),
            in_specs=[pl.BlockSpec((1, 1, tm), lambda i, zf: (i, 0, 0), memory_space=pltpu.SMEM),
                      pl.BlockSpec((tm, MOE_W), lambda i, zf: (i, 0))],
            out_specs=pl.BlockSpec(memory_space=pl.ANY),
            scratch_shapes=[pltpu.VMEM((MOE_TR, MOE_W), f32), pltpu.SemaphoreType.DMA(())],
        ),
        out_shape=jax.ShapeDtypeStruct((MOE_ROWS, MOE_W), f32),
        compiler_params=pltpu.CompilerParams(
            dimension_semantics=("arbitrary",), vmem_limit_bytes=V7X_VMEM_LIMIT),
        name="moe_dispatch",
    )(zero_flags, pos_blocks, h2)


def _experts_kernel(l, tg_ref, ta_ref, tb_ref, first_ref, nxt_ref, nt_ref, xs_ref, wg_hbm, wu_hbm, wd_hbm,
                    y_ref, wg_ref, wu_ref, wd_ref, sg_ref, su_ref, sd_ref, sems):
    i = pl.program_id(0)
    live = i < nt_ref[0]
    g = tg_ref[i]

    def fetch(grp):
        return [pltpu.make_async_copy(w.at[l, pl.ds(grp * EXP_PER_GROUP, EXP_PER_GROUP)], s, sems.at[k])
                for k, (w, s) in enumerate(((wg_hbm, sg_ref), (wu_hbm, su_ref), (wd_hbm, sd_ref)))]

    @pl.when(i == 0)
    def _():
        for c in fetch(tg_ref[0]):
            c.start()

    @pl.when(live & (first_ref[i] != 0))
    def _():
        for c in fetch(g):
            c.wait()
        for j in range(EXP_PER_GROUP):
            wg_ref[j] = sg_ref[j].astype(bf16)
            wu_ref[j] = su_ref[j].astype(bf16)
            wd_ref[j] = sd_ref[j].astype(bf16)

        @pl.when(nxt_ref[i] >= 0)
        def _():
            for c in fetch(nxt_ref[i]):
                c.start(priority=1)

    @pl.when(live)
    def _():
        x = xs_ref[:, 0:D].astype(bf16)
        comb = xs_ref[:, D:D + LANES]
        lane = lax.broadcasted_iota(jnp.int32, comb.shape, 1)
        acc = None
        for j in (ta_ref[i], tb_ref[i]):
            w_j = jnp.sum(jnp.where(lane == g * EXP_PER_GROUP + j, comb, 0.0), axis=1, keepdims=True)
            act = _silu(_dot(x, wg_ref[j])) * _dot(x, wu_ref[j]) * w_j
            part = _dot(act.astype(bf16), wd_ref[j])
            acc = part if acc is None else acc + part
        y_ref[...] = acc

    @pl.when(i >= nt_ref[0])
    def _():
        y_ref[...] = jnp.zeros_like(y_ref)


def _experts(l, plan, xs, wg, wu, wd):
    def tile(i, *prefetch):
        return jnp.minimum(i, prefetch[-1][0] - 1)

    hbm = pl.BlockSpec(memory_space=pl.ANY)
    return pl.pallas_call(
        functools.partial(_experts_kernel, l),
        grid_spec=pltpu.PrefetchScalarGridSpec(
            num_scalar_prefetch=6,
            grid=(MOE_TILES,),
            in_specs=[pl.BlockSpec((MOE_TR, MOE_W), lambda i, *p: (tile(i, *p), 0)), hbm, hbm, hbm],
            out_specs=pl.BlockSpec((MOE_TR, D), lambda i, *p: (i, 0)),
            scratch_shapes=[pltpu.VMEM((EXP_PER_GROUP, D, D_EXP), bf16),
                            pltpu.VMEM((EXP_PER_GROUP, D, D_EXP), bf16),
                            pltpu.VMEM((EXP_PER_GROUP, D_EXP, D), bf16),
                            pltpu.VMEM((EXP_PER_GROUP, D, D_EXP), f32),
                            pltpu.VMEM((EXP_PER_GROUP, D, D_EXP), f32),
                            pltpu.VMEM((EXP_PER_GROUP, D_EXP, D), f32), pltpu.SemaphoreType.DMA((3,))],
        ),
        out_shape=jax.ShapeDtypeStruct((MOE_ROWS, D), f32),
        compiler_params=pltpu.CompilerParams(
            dimension_semantics=("arbitrary",), vmem_limit_bytes=V7X_VMEM_LIMIT),
        name="moe_experts",
    )(*plan, xs, wg, wu, wd)


def _combine_kernel(pos_ref, posn_ref, x1_ref, mod_ref, y_ref, oc_ref, ol_ref, ya_ref, yb_ref, sems):
    tm = x1_ref.shape[0]
    i = pl.program_id(0)
    n_ctx = T_CTX // tm

    def pull(p_ref, buf, sem):
        for r in range(tm):
            _row_copy(y_ref, p_ref[0, 0, r], buf, r, sem).start(priority=r % 2)

    @pl.when(i == 0)
    def _():
        pull(pos_ref, ya_ref, sems.at[0])

    def step(cur, nxt, sem_cur, sem_nxt):
        @pl.when(i + 1 < pl.num_programs(0))
        def _():
            pull(posn_ref, nxt, sem_nxt)

        pltpu.make_async_copy(y_ref.at[pl.ds(0, tm)], cur, sem_cur).wait()
        out = x1_ref[...] + mod_ref[5:6, :] * cur[...]

        @pl.when(i < n_ctx)
        def _():
            oc_ref[...] = out

        @pl.when(i >= n_ctx)
        def _():
            ol_ref[...] = out

    @pl.when(i % 2 == 0)
    def _():
        step(ya_ref, yb_ref, sems.at[0], sems.at[1])

    @pl.when(i % 2 == 1)
    def _():
        step(yb_ref, ya_ref, sems.at[1], sems.at[0])


def _combine(l, pos_blocks, x1, mod, y):
    tm = TM_DISP
    n = T // tm
    return pl.pallas_call(
        _combine_kernel,
        grid=(n,),
        in_specs=[pl.BlockSpec((1, 1, tm), lambda i: (i, 0, 0), memory_space=pltpu.SMEM),
                  pl.BlockSpec((1, 1, tm), lambda i: (jnp.minimum(i + 1, n - 1), 0, 0),
                               memory_space=pltpu.SMEM),
                  pl.BlockSpec((tm, D), lambda i: (i, 0)),
                  pl.BlockSpec((None, None, 6, D), lambda i: (l, _cond_of_tile(i, tm), 0, 0)),
                  pl.BlockSpec(memory_space=pl.ANY)],
        out_specs=_two_stream_specs(tm, D),
        out_shape=[jax.ShapeDtypeStruct((T_CTX, D), f32), jax.ShapeDtypeStruct((T_LAT, D), f32)],
        scratch_shapes=[pltpu.VMEM((tm, D), f32), pltpu.VMEM((tm, D), f32),
                        pltpu.SemaphoreType.DMA((2,))],
        compiler_params=pltpu.CompilerParams(
            dimension_semantics=("arbitrary",), vmem_limit_bytes=V7X_VMEM_LIMIT),
        name="moe_combine",
    )(pos_blocks, pos_blocks, x1, mod, y)


def _moe_plan(route, counts):
    cnt = counts[0:N_BUCKETS, 0].astype(jnp.int32)
    tiles = (cnt + MOE_TR - 1) // MOE_TR
    tile_end = jnp.cumsum(tiles)
    tile_start = tile_end - tiles
    off = (tile_start * MOE_TR).astype(f32)
    buckets = jnp.arange(N_BUCKETS, dtype=f32)[:, None]
    pos = jnp.sum(jnp.where(route[0][None, :] == buckets, off[:, None], 0.0), axis=0) + route[1]
    pos_blocks = pos.astype(jnp.int32).reshape(T // TM_DISP, 1, TM_DISP)
    tid = jnp.arange(MOE_TILES, dtype=jnp.int32)
    tb = jnp.minimum(jnp.sum(tid[:, None] >= tile_end[None, :], axis=1), N_BUCKETS - 1).astype(jnp.int32)
    n_tiles = tile_end[-1:].astype(jnp.int32)
    is_last = jnp.any((tid[:, None] == tile_end[None, :] - 1) & (tiles > 0)[None, :], axis=1)
    zero_flags = ((tid >= n_tiles[0]) | is_last).astype(jnp.int32)
    tg = tb // PAIRS
    pair = tb % PAIRS
    pair_lo = np.array([a for a in range(EXP_PER_GROUP) for _ in range(a + 1, EXP_PER_GROUP)])
    pair_hi = np.array([b for a in range(EXP_PER_GROUP) for b in range(a + 1, EXP_PER_GROUP)])
    of_pair = pair[:, None] == jnp.arange(PAIRS)[None, :]
    ta = jnp.sum(jnp.where(of_pair, pair_lo[None, :], 0), axis=1).astype(jnp.int32)
    tb2 = jnp.sum(jnp.where(of_pair, pair_hi[None, :], 0), axis=1).astype(jnp.int32)
    g_tiles = tiles.reshape(N_GROUPS, PAIRS).sum(axis=1)
    g_end = jnp.cumsum(g_tiles)
    g_start = g_end - g_tiles
    used = g_tiles > 0
    gid = jnp.arange(N_GROUPS, dtype=jnp.int32)
    first = jnp.any((tid[:, None] == g_start[None, :]) & used[None, :], axis=1).astype(jnp.int32)
    later = (gid[None, :] > gid[:, None]) & used[None, :]
    nxt_g = jnp.min(jnp.where(later, gid[None, :], N_GROUPS), axis=1)
    nxt_g = jnp.where(nxt_g == N_GROUPS, -1, nxt_g)
    nxt = jnp.sum(jnp.where(tg[:, None] == gid[None, :], nxt_g[None, :], 0), axis=1).astype(jnp.int32)
    return pos_blocks, zero_flags, (tg.astype(jnp.int32), ta, tb2, first, nxt, n_tiles)


def _rope_tables():
    pos = np.arange(DEC_SEQ)
    half = HD_A // 4
    freqs = ROPE_BASE ** (-np.arange(half, dtype=np.float64) / half)
    ang_r = (pos // GRID_W)[:, None] * freqs[None, :]
    ang_c = (pos % GRID_W)[:, None] * freqs[None, :]
    ang = np.concatenate([ang_r, ang_r, ang_c, ang_c], axis=1)
    sign = np.concatenate([-np.ones(half), np.ones(half)] * 2)[None, :]
    cos = np.tile(np.cos(ang), (1, 2))
    sin = np.tile(np.sin(ang) * sign, (1, 2))
    ident_c = np.ones((TM_PROJ, LANES))
    ident_s = np.zeros((TM_PROJ, LANES))
    return (jnp.asarray(np.concatenate([cos, ident_c]), f32),
            jnp.asarray(np.concatenate([sin, ident_s]), f32))


def kernel(x_prompt, x_sample, c, cache_k, cache_v, state_ret_fwd, state_ret_bwd, c_ctx,
           norm1_w, norm2_w, w_ada, b_ada, w_in, q_norm_w, k_norm_w, attn_sink,
           ret_decay_fwd, ret_decay_bwd, ret_gn_w, w_pa, w_pb, w_o, w_router, b_router,
           w_exp_gate, w_exp_up, w_exp_down):
    x_ctx, x_lat = x_prompt.reshape(T_CTX, D), x_sample.reshape(T_LAT, D)
    cond8 = jnp.zeros((N_COND, D), f32).at[0].set(c_ctx).at[1:1 + DEC_BATCH].set(c)
    mod = _adaln(cond8, w_ada, b_ada).reshape(DEPTH, N_COND, 6, D)

    cos_t, sin_t = _rope_tables()
    blk = np.arange(Q_A) // HD_A
    ones_blk = jnp.asarray((blk[:, None] == blk[None, :]) / HD_A, bf16)
    tri = jnp.asarray(np.triu(np.ones((POST_SUB, POST_SUB)), 1), bf16)
    qnw = jnp.tile(q_norm_w, (1, NH_A))
    knw = jnp.tile(k_norm_w, (1, NKV_A))
    wr_pad = jnp.pad(w_router, ((0, 0), (0, LANES - N_EXP)))
    br_col = jnp.broadcast_to(b_router[:, None], (N_EXP, LANES))
    ck = cache_k.reshape(DEC_BATCH, DEPTH, PAST, KV_A)
    cv = cache_v.reshape(DEC_BATCH, DEPTH, PAST, KV_A)
    s0 = jnp.concatenate([state_ret_fwd, state_ret_bwd], axis=3)
    lg = jnp.log1p(-jnp.exp(jnp.concatenate([ret_decay_fwd, ret_decay_bwd], axis=1).astype(f32)))

    new_k, new_v, new_sf, new_sb = [], [], [], []
    for l in range(DEPTH):
        qa, ka, va, qr, kr, vr, gr, ga, gb = _inproj(l, x_ctx, x_lat, norm1_w, mod, w_in, cos_t, sin_t,
                                                     ones_blk, qnw, knw)
        oa_ctx = _ctx_attn(l, qa, ka, va, attn_sink)
        oa_lat = _lat_attn(l, qa, ka, va, ck, cv, cos_t, sin_t, attn_sink)
        yr_ctx, sf, sb = _retention(l, qr, kr, vr, gr, None, lg, ret_gn_w, BATCH, SEQ, 0)
        yr_lat, _, _ = _retention(l, qr, kr, vr, gr, s0[:, l], lg, ret_gn_w, DEC_BATCH, DEC_SEQ,
                                  T_CTX // DEC_SEQ)
        x1, h2, route, counts = _post(l, x_ctx, x_lat, oa_ctx, oa_lat, yr_ctx, yr_lat, ga, gb,
                                      w_pa, w_pb, w_o, mod, norm2_w, wr_pad, br_col, tri)
        pos_blocks, zero_flags, plan = _moe_plan(route, counts)
        xs = _dispatch(zero_flags, pos_blocks, h2)
        y = _experts(l, plan, xs, w_exp_gate, w_exp_up, w_exp_down)
        x_ctx, x_lat = _combine(l, pos_blocks, x1, mod, y)

        new_k.append(ka[:T_CTX].reshape(BATCH, SEQ, NKV_A, HD_A))
        new_v.append(va[:T_CTX].reshape(BATCH, SEQ, NKV_A, HD_A))
        new_sf.append(sf)
        new_sb.append(sb)

    return (x_ctx.reshape(BATCH, SEQ, D), x_lat.reshape(DEC_BATCH, DEC_SEQ, D),
            jnp.stack(new_k, axis=1), jnp.stack(new_v, axis=1),
            jnp.stack(new_sf, axis=1), jnp.stack(new_sb, axis=1))
```

```python
import functools

import numpy as np
import jax
import jax.numpy as jnp
from jax import lax
from jax.experimental import pallas as pl
from jax.experimental.pallas import tpu as pltpu

D = 1024
BATCH, SEQ = 16, 256
DEC_BATCH, DEC_SEQ = 2, 2048
DEPTH = 2
PAST = 512
GRID_W = 64
NH_A, NKV_A, HD_A = 8, 2, 64
WINDOW = 128
NH_R, DK_R, DV_R = 4, 64, 128
CHUNK = 128
N_EXP, N_GROUPS, EXP_PER_GROUP = 16, 4, 4
D_EXP = 512
ROPE_BASE = 10000.0
EPS = 1e-6
NEG_INF = -1e30

Q_A = NH_A * HD_A
KV_A = NKV_A * HD_A
QK_R = NH_R * DK_R
V_R = NH_R * DV_R
C_QA = (0, Q_A)
C_KA = (C_QA[1], C_QA[1] + KV_A)
C_VA = (C_KA[1], C_KA[1] + KV_A)
C_QR = (C_VA[1], C_VA[1] + QK_R)
C_KR = (C_QR[1], C_QR[1] + QK_R)
C_VR = (C_KR[1], C_KR[1] + V_R)
C_GR = (C_VR[1], C_VR[1] + V_R)
C_GA = (C_GR[1], C_GR[1] + D)
C_GB = (C_GA[1], C_GA[1] + D)
D_IN = C_GB[1]

T_CTX = BATCH * SEQ
T_LAT = DEC_BATCH * DEC_SEQ
T = T_CTX + T_LAT
N_COND = 8

LANES = 128
V7X_VMEM_LIMIT = 56 * 1024 * 1024

TM_PROJ = 512
TM_POST = 512
POST_SUB = 256
W_CHUNK = 256
TM_DISP = 512
MOE_TR = 256
PAIRS = EXP_PER_GROUP * (EXP_PER_GROUP - 1) // 2
N_BUCKETS = N_GROUPS * PAIRS
BUCKET_ROWS = 32
MOE_TILES = T // MOE_TR + N_BUCKETS
MOE_ROWS = MOE_TILES * MOE_TR
MOE_W = D + LANES
ATT_QB = 256
CTX_SEQS_PER_STEP = 4

f32 = jnp.float32
bf16 = jnp.bfloat16


def _dot(a, b):
    return jnp.dot(a, b, preferred_element_type=f32)


def _dot_t(a, b):
    return lax.dot_general(a, b, (((1,), (1,)), ((), ())), preferred_element_type=f32)


def _split(x):
    hi = x.astype(bf16)
    lo = (x - hi.astype(f32)).astype(bf16)
    return hi, lo


def _dot3(a, b):
    ah, al = _split(a)
    bh, bl = _split(b)
    return _dot(ah, bh) + (_dot(ah, bl) + _dot(al, bh))


def _sigmoid(x):
    return 1.0 / (1.0 + jnp.exp(-x))


def _silu(x):
    return x * _sigmoid(x)


def _cond_of_tile(i, tm):
    n_ctx = T_CTX // tm
    per_b = DEC_SEQ // tm
    return jnp.where(i < n_ctx, 0, 1 + jnp.maximum(i - n_ctx, 0) // per_b)


def _adaln_kernel(cond_ref, w_ref, b_ref, o_ref):
    a = _silu(cond_ref[...])
    o_ref[0] = _dot3(a, w_ref[0]) + b_ref[0]


def _adaln(cond8, w_ada, b_ada):
    tn = 2048
    return pl.pallas_call(
        _adaln_kernel,
        grid=(DEPTH, 6 * D // tn),
        in_specs=[
            pl.BlockSpec((N_COND, D), lambda l, j: (0, 0)),
            pl.BlockSpec((1, D, tn), lambda l, j: (l, 0, j)),
            pl.BlockSpec((1, 1, tn), lambda l, j: (l, 0, j)),
        ],
        out_specs=pl.BlockSpec((1, N_COND, tn), lambda l, j: (l, 0, j)),
        out_shape=jax.ShapeDtypeStruct((DEPTH, N_COND, 6 * D), f32),
        compiler_params=pltpu.CompilerParams(
            dimension_semantics=("parallel", "parallel"), vmem_limit_bytes=V7X_VMEM_LIMIT),
        name="adaln",
    )(cond8, w_ada, b_ada.reshape(DEPTH, 1, 6 * D))


def _rope(x, cos, sin_signed, first_half):
    fwd = pltpu.roll(x, 16, 1)
    bwd = pltpu.roll(x, LANES - 16, 1)
    partner = jnp.where(first_half, bwd, fwd)
    return x * cos + partner * sin_signed


def _head_rms(x, ones_blk, w):
    n = x.shape[1]
    sq_hi, sq_lo = _split(x * x)
    blk = ones_blk[0:n, 0:n]
    mean = _dot(sq_hi, blk) + _dot(sq_lo, blk)
    return x * lax.rsqrt(mean + EPS) * w


def _load_cast(w_hbm, dst_ref, stage_ref, sems):
    k, n = w_hbm.shape

    def chunk_copy(c):
        return pltpu.make_async_copy(w_hbm.at[:, pl.ds(c * W_CHUNK, W_CHUNK)],
                                     stage_ref.at[c % 2, pl.ds(0, k)], sems.at[c % 2])

    n_chunks = n // W_CHUNK
    chunk_copy(0).start()
    for c in range(n_chunks):
        if c + 1 < n_chunks:
            chunk_copy(c + 1).start()
        chunk_copy(c).wait()
        dst_ref[:, c * W_CHUNK:(c + 1) * W_CHUNK] = stage_ref[c % 2, 0:k, :].astype(bf16)


def _tile_of_two(i, n_first, a_ref, b_ref):
    return jnp.where(i < n_first, a_ref[...], b_ref[...])


def _inproj_kernel(l, xc_ref, xl_ref, n1_ref, mod_ref, w_hbm, cos_ref, sin_ref, ones_ref, qnw_ref,
                   knw_ref, qa_ref, ka_ref, va_ref, qr_ref, kr_ref, vr_ref, gr_ref, ga_ref, gb_ref,
                   w_ref, stage_ref, sems):
    i = pl.program_id(0)

    @pl.when(i == 0)
    def _():
        _load_cast(w_hbm.at[l], w_ref, stage_ref, sems)

    x = _tile_of_two(i, T_CTX // xc_ref.shape[0], xc_ref, xl_ref)
    y = x * lax.rsqrt(jnp.mean(x * x, axis=-1, keepdims=True) + EPS) * n1_ref[...]
    h = (y * (1.0 + mod_ref[1:2, :]) + mod_ref[0:1, :]).astype(bf16)

    def proj(c):
        return _dot(h, w_ref[:, c[0]:c[1]])

    cos = cos_ref[...]
    sin = sin_ref[...]
    lane = lax.broadcasted_iota(jnp.int32, cos.shape, 1)
    first_half = (lane % 32) < 16

    def rope_all(v):
        parts = [_rope(v[:, j:j + LANES], cos, sin, first_half) for j in range(0, v.shape[1], LANES)]
        return parts[0] if len(parts) == 1 else jnp.concatenate(parts, axis=1)

    ones_blk = ones_ref[...]
    qa = _head_rms(proj(C_QA), ones_blk, qnw_ref[...])
    qa_ref[...] = (rope_all(qa) * (HD_A ** -0.5)).astype(bf16)
    ka_ref[...] = _head_rms(proj(C_KA), ones_blk, knw_ref[...])
    va_ref[...] = proj(C_VA)
    qr_ref[...] = rope_all(proj(C_QR)).astype(bf16)
    kr_ref[...] = (rope_all(proj(C_KR)) * (DK_R ** -0.5)).astype(bf16)
    vr_ref[...] = proj(C_VR).astype(bf16)
    gr_ref[...] = proj(C_GR).astype(bf16)
    ga_ref[...] = proj(C_GA).astype(bf16)
    gb_ref[...] = proj(C_GB).astype(bf16)


def _two_stream_specs(tm, width):
    n_ctx = T_CTX // tm
    return [pl.BlockSpec((tm, width), lambda i, *_: (jnp.minimum(i, n_ctx - 1), 0)),
            pl.BlockSpec((tm, width), lambda i, *_: (jnp.maximum(i - n_ctx, 0), 0))]


def _inproj(l, x_ctx, x_lat, norm1_w, mod, w_in, cos_t, sin_t, ones_blk, qnw, knw):
    tm = TM_PROJ
    n_ctx = T_CTX // tm
    per_b = DEC_SEQ // tm

    def tab_map(i):
        return (jnp.where(i < n_ctx, per_b, jnp.maximum(i - n_ctx, 0) % per_b), 0)

    row = lambda i: (i, 0)
    const = lambda i: (0, 0)
    widths = [(Q_A, bf16), (KV_A, f32), (KV_A, f32), (QK_R, bf16), (QK_R, bf16), (V_R, bf16),
              (V_R, bf16), (D, bf16), (D, bf16)]
    return pl.pallas_call(
        functools.partial(_inproj_kernel, l),
        grid=(T // tm,),
        in_specs=_two_stream_specs(tm, D) + [
            pl.BlockSpec((1, D), const),
            pl.BlockSpec((None, None, 6, D), lambda i: (l, _cond_of_tile(i, tm), 0, 0)),
            pl.BlockSpec(memory_space=pl.ANY),
            pl.BlockSpec((tm, LANES), tab_map),
            pl.BlockSpec((tm, LANES), tab_map),
            pl.BlockSpec((Q_A, Q_A), const),
            pl.BlockSpec((1, Q_A), const),
            pl.BlockSpec((1, KV_A), const),
        ],
        out_specs=[pl.BlockSpec((tm, w), row) for w, _ in widths],
        out_shape=[jax.ShapeDtypeStruct((T, w), dt) for w, dt in widths],
        scratch_shapes=[pltpu.VMEM((D, D_IN), bf16), pltpu.VMEM((2, D, W_CHUNK), f32),
                        pltpu.SemaphoreType.DMA((2,))],
        compiler_params=pltpu.CompilerParams(
            dimension_semantics=("arbitrary",), vmem_limit_bytes=V7X_VMEM_LIMIT),
        name="inproj",
    )(x_ctx, x_lat, norm1_w[l:l + 1], mod, w_in, cos_t, sin_t, ones_blk, qnw[l:l + 1], knw[l:l + 1])


def _head_blocks(t, kv, lo_mask):
    r = pltpu.roll(t, HD_A, 1)
    if kv == 0:
        a = jnp.where(lo_mask, t, 0.0)
        b = jnp.where(lo_mask, 0.0, r)
    else:
        a = jnp.where(lo_mask, r, 0.0)
        b = jnp.where(lo_mask, 0.0, t)
    return jnp.concatenate([a, b], axis=0).astype(bf16)


def _ctx_attn_kernel(sink_ref, q_ref, k_ref, v_ref, o_ref):
    n = SEQ
    lo_mask = lax.broadcasted_iota(jnp.int32, (n, LANES), 1) < HD_A
    for b in range(CTX_SEQS_PER_STEP):
        rows = slice(b * n, (b + 1) * n)
        k = k_ref[rows, :]
        v = v_ref[rows, :]
        for kv in range(NKV_A):
            kblk = _head_blocks(k, kv, lo_mask)
            vblk = _head_blocks(v, kv, lo_mask)
            for pr in range(2):
                pi = kv * 2 + pr
                s = _dot_t(q_ref[rows, pi * LANES:(pi + 1) * LANES], kblk)
                ps, invs = [], []
                for hh in range(2):
                    sk = sink_ref[2 * pi + hh]
                    sh = s[:, hh * n:(hh + 1) * n]
                    m = jnp.maximum(jnp.max(sh, axis=-1, keepdims=True), sk)
                    p = jnp.exp(sh - m)
                    invs.append(1.0 / (jnp.sum(p, axis=-1, keepdims=True) + jnp.exp(sk - m)))
                    ps.append(p.astype(bf16))
                o = _dot(jnp.concatenate(ps, axis=1), vblk) * jnp.where(lo_mask, invs[0], invs[1])
                o_ref[rows, pi * LANES:(pi + 1) * LANES] = o.astype(bf16)


def _ctx_attn(l, qa, ka, va, sink):
    blk = lambda w: pl.BlockSpec((CTX_SEQS_PER_STEP * SEQ, w), lambda b: (b, 0))
    return pl.pallas_call(
        _ctx_attn_kernel,
        grid=(BATCH // CTX_SEQS_PER_STEP,),
        in_specs=[pl.BlockSpec(memory_space=pltpu.SMEM), blk(Q_A), blk(KV_A), blk(KV_A)],
        out_specs=blk(Q_A),
        out_shape=jax.ShapeDtypeStruct((T_CTX, Q_A), bf16),
        compiler_params=pltpu.CompilerParams(
            dimension_semantics=("parallel",), vmem_limit_bytes=V7X_VMEM_LIMIT),
        name="ctx_attn",
    )(sink[l], qa, ka, va)


def _lat_attn_kernel(sink_ref, q_ref, k_ref, v_ref, kc_ref, vc_ref, cos_ref, sin_ref, o_ref):
    j = pl.program_id(1)
    qb = ATT_QB
    win = 2 * qb
    ws = pl.multiple_of(jnp.clip(j * qb - WINDOW, 0, DEC_SEQ - win), WINDOW)
    lo_mask = lax.broadcasted_iota(jnp.int32, (win, LANES), 1) < HD_A
    lane = lax.broadcasted_iota(jnp.int32, (win, LANES), 1)
    kw = _rope(k_ref[pl.ds(ws, win), :], cos_ref[pl.ds(ws, win), :], sin_ref[pl.ds(ws, win), :],
               (lane % 32) < 16)
    vw = v_ref[pl.ds(ws, win), :]
    kc = kc_ref[...]
    vc = vc_ref[...]
    qpos = j * qb + (lax.broadcasted_iota(jnp.int32, (2 * qb, win), 0) & (qb - 1))
    kpos = ws + lax.broadcasted_iota(jnp.int32, (2 * qb, win), 1)
    valid = jnp.abs(qpos - kpos) <= WINDOW
    out_lo = lax.broadcasted_iota(jnp.int32, (2 * qb, LANES), 1) < HD_A
    for kv in range(NKV_A):
        kc_blk = _head_blocks(kc, kv, lo_mask[:PAST])
        vc_blk = _head_blocks(vc, kv, lo_mask[:PAST])
        kw_blk = _head_blocks(kw, kv, lo_mask)
        vw_blk = _head_blocks(vw, kv, lo_mask)
        q2 = jnp.concatenate([q_ref[:, (2 * kv) * LANES:(2 * kv + 1) * LANES],
                              q_ref[:, (2 * kv + 1) * LANES:(2 * kv + 2) * LANES]], axis=0)
        s_c = _dot_t(q2, kc_blk)
        s_w = _dot_t(q2, kw_blk)
        pcs, pws, invs = [], [], []
        for hh in range(2):
            row = lax.broadcasted_iota(jnp.int32, (2 * qb, 1), 0)
            sk = jnp.where(row < qb, sink_ref[4 * kv + hh], sink_ref[4 * kv + 2 + hh])
            sc = s_c[:, hh * PAST:(hh + 1) * PAST]
            sw = jnp.where(valid, s_w[:, hh * win:(hh + 1) * win], NEG_INF)
            m = jnp.maximum(jnp.maximum(jnp.max(sc, axis=-1, keepdims=True),
                                        jnp.max(sw, axis=-1, keepdims=True)), sk)
            pc = jnp.exp(sc - m)
            pw = jnp.exp(sw - m)
            den = (jnp.sum(pc, axis=-1, keepdims=True) + jnp.sum(pw, axis=-1, keepdims=True)
                   + jnp.exp(sk - m))
            invs.append(1.0 / den)
            pcs.append(pc.astype(bf16))
            pws.append(pw.astype(bf16))
        o = _dot(jnp.concatenate(pcs, axis=1), vc_blk) + _dot(jnp.concatenate(pws, axis=1), vw_blk)
        o = o * jnp.where(out_lo, invs[0], invs[1])
        o_ref[:, (2 * kv) * LANES:(2 * kv + 1) * LANES] = o[:qb].astype(bf16)
        o_ref[:, (2 * kv + 1) * LANES:(2 * kv + 2) * LANES] = o[qb:].astype(bf16)


def _lat_attn(l, qa, ka, va, cache_k, cache_v, cos_l, sin_l, sink):
    qb = ATT_QB
    nq = DEC_SEQ // qb
    ctx_blocks = T_CTX // DEC_SEQ
    seq = lambda b, j: (ctx_blocks + b, 0)
    return pl.pallas_call(
        _lat_attn_kernel,
        grid=(DEC_BATCH, nq),
        in_specs=[
            pl.BlockSpec(memory_space=pltpu.SMEM),
            pl.BlockSpec((qb, Q_A), lambda b, j: (T_CTX // qb + b * nq + j, 0)),
            pl.BlockSpec((DEC_SEQ, KV_A), seq),
            pl.BlockSpec((DEC_SEQ, KV_A), seq),
            pl.BlockSpec((None, None, PAST, KV_A), lambda b, j: (b, l, 0, 0)),
            pl.BlockSpec((None, None, PAST, KV_A), lambda b, j: (b, l, 0, 0)),
            pl.BlockSpec((DEC_SEQ, LANES), lambda b, j: (0, 0)),
            pl.BlockSpec((DEC_SEQ, LANES), lambda b, j: (0, 0)),
        ],
        out_specs=pl.BlockSpec((qb, Q_A), lambda b, j: (b * nq + j, 0)),
        out_shape=jax.ShapeDtypeStruct((T_LAT, Q_A), bf16),
        compiler_params=pltpu.CompilerParams(
            dimension_semantics=("parallel", "parallel"), vmem_limit_bytes=V7X_VMEM_LIMIT),
        name="lat_attn",
    )(sink[l], qa, ka, va, cache_k, cache_v, cos_l, sin_l)


def _dup_heads(pair, lo_mask):
    r = pltpu.roll(pair, DK_R, 1)
    return jnp.where(lo_mask, pair, r), jnp.where(lo_mask, r, pair)


def _retention_kernel(has_s0, n_seq, n_chunks, *refs):
    if has_s0:
        (lg_ref, q_ref, k_ref, v_ref, g_ref, s0_ref, gnw_ref,
         y_ref, sf_ref, sb_ref, ds_ref, st_ref, mask_ref, qdec_ref, kdec_ref, cdec_ref) = refs
    else:
        (lg_ref, q_ref, k_ref, v_ref, g_ref, gnw_ref,
         y_ref, sf_ref, sb_ref, ds_ref, st_ref, mask_ref, qdec_ref, kdec_ref, cdec_ref) = refs
        s0_ref = None
    C = CHUNK
    lo_mask = lax.broadcasted_iota(jnp.int32, (C, LANES), 1) < DK_R

    ri = lax.broadcasted_iota(jnp.int32, (C, C), 0).astype(f32)
    diff = ri - lax.broadcasted_iota(jnp.int32, (C, C), 1).astype(f32)
    for h in range(NH_R):
        lg_f, lg_b = lg_ref[h], lg_ref[NH_R + h]
        mask_ref[h] = (jnp.where(diff >= 0, jnp.exp(jnp.maximum(diff, 0.0) * lg_f), 0.0)
                       + jnp.where(diff <= 0, jnp.exp(jnp.maximum(-diff, 0.0) * lg_b), 0.0))
        qdec_ref[:, h * LANES:(h + 1) * LANES] = jnp.exp(
            jnp.where(lo_mask, (ri + 1.0) * lg_f, (C - ri) * lg_b))
        kdec_ref[:, h * LANES:(h + 1) * LANES] = jnp.exp(
            jnp.where(lo_mask, (C - 1.0 - ri) * lg_f, ri * lg_b))
        cdec_ref[h] = jnp.exp(jnp.where(ri < DK_R, C * lg_f, C * lg_b))

    def inc_body(c, carry):
        r0 = pl.multiple_of(c * C, C)
        for pr in range(2):
            kp = k_ref[pl.ds(r0, C), pr * LANES:(pr + 1) * LANES].astype(f32)
            for hh, kd in enumerate(_dup_heads(kp, lo_mask)):
                h = 2 * pr + hh
                kd = (kd * kdec_ref[:, h * LANES:(h + 1) * LANES]).astype(bf16)
                vh = v_ref[pl.ds(r0, C), h * DV_R:(h + 1) * DV_R]
                ds_ref[c, h] = lax.dot_general(kd, vh, (((0,), (0,)), ((), ())),
                                               preferred_element_type=f32)
        return carry

    lax.fori_loop(0, n_seq * n_chunks, inc_body, 0)

    for sq in range(n_seq):
        base = sq * n_chunks
        for h in range(NH_R):
            cf = cdec_ref[h, 0:DK_R, :]
            cb = cdec_ref[h, DK_R:2 * DK_R, :]
            if has_s0:
                init_f = s0_ref[h, 0:DK_R, :]
                init_b = s0_ref[h, DK_R:2 * DK_R, :]
            else:
                init_f = jnp.zeros((DK_R, DV_R), f32)
                init_b = init_f

            def fwd_body(i, s, h=h, cf=cf, base=base):
                c = base + i
                st_ref[c, h, 0:DK_R, :] = s
                return s * cf + ds_ref[c, h, 0:DK_R, :]

            def bwd_body(i, s, h=h, cb=cb, base=base):
                c = base + n_chunks - 1 - i
                st_ref[c, h, DK_R:2 * DK_R, :] = s
                return s * cb + ds_ref[c, h, DK_R:2 * DK_R, :]

            sf_ref[sq, h] = lax.fori_loop(0, n_chunks, fwd_body, init_f)
            sb_ref[sq, h] = lax.fori_loop(0, n_chunks, bwd_body, init_b)

    def out_body(c, carry):
        r0 = pl.multiple_of(c * C, C)
        for pr in range(2):
            qp = q_ref[pl.ds(r0, C), pr * LANES:(pr + 1) * LANES]
            kp = k_ref[pl.ds(r0, C), pr * LANES:(pr + 1) * LANES].astype(f32)
            kblk = jnp.concatenate([jnp.where(lo_mask, kp, 0.0), jnp.where(lo_mask, 0.0, kp)],
                                   axis=0).astype(bf16)
            a2 = _dot_t(qp, kblk)
            for hh, qd in enumerate(_dup_heads(qp.astype(f32), lo_mask)):
                h = 2 * pr + hh
                a = (a2[:, hh * C:(hh + 1) * C] * mask_ref[h]).astype(bf16)
                vh = v_ref[pl.ds(r0, C), h * DV_R:(h + 1) * DV_R]
                qd = (qd * qdec_ref[:, h * LANES:(h + 1) * LANES]).astype(bf16)
                o = _dot(a, vh) + _dot(qd, st_ref[c, h].astype(bf16))
                mu = jnp.mean(o, axis=-1, keepdims=True)
                d = o - mu
                var = jnp.mean(d * d, axis=-1, keepdims=True)
                yh = d * lax.rsqrt(var + EPS) * gnw_ref[:, h * DV_R:(h + 1) * DV_R]
                g = g_ref[pl.ds(r0, C), h * DV_R:(h + 1) * DV_R].astype(f32)
                y_ref[pl.ds(r0, C), h * DV_R:(h + 1) * DV_R] = (yh * _silu(g)).astype(bf16)
        return carry

    lax.fori_loop(0, n_seq * n_chunks, out_body, 0)


def _retention(l, qr, kr, vr, gr, s0, lg, gnw, nb, seq, row_block0, n_seq):
    n_chunks = seq // CHUNK
    has_s0 = s0 is not None
    rows = n_seq * seq
    tok = lambda w: pl.BlockSpec((rows, w), lambda b: (row_block0 + b, 0))
    in_specs = [pl.BlockSpec(memory_space=pltpu.SMEM), tok(QK_R), tok(QK_R), tok(V_R), tok(V_R)]
    args = [lg[l], qr, kr, vr, gr]
    if has_s0:
        in_specs.append(pl.BlockSpec((None, NH_R, 2 * DK_R, DV_R), lambda b: (b, 0, 0, 0)))
        args.append(s0)
    in_specs += [pl.BlockSpec((1, V_R), lambda b: (0, 0))]
    args += [gnw[l:l + 1]]
    st_spec = pl.BlockSpec((n_seq, NH_R, DK_R, DV_R), lambda b: (b, 0, 0, 0))
    return pl.pallas_call(
        functools.partial(_retention_kernel, has_s0, n_seq, n_chunks),
        grid=(nb // n_seq,),
        in_specs=in_specs,
        out_specs=[pl.BlockSpec((rows, V_R), lambda b: (b, 0)), st_spec, st_spec],
        out_shape=[jax.ShapeDtypeStruct((nb * seq, V_R), bf16),
                   jax.ShapeDtypeStruct((nb, NH_R, DK_R, DV_R), f32),
                   jax.ShapeDtypeStruct((nb, NH_R, DK_R, DV_R), f32)],
        scratch_shapes=[pltpu.VMEM((n_seq * n_chunks, NH_R, 2 * DK_R, DV_R), f32),
                        pltpu.VMEM((n_seq * n_chunks, NH_R, 2 * DK_R, DV_R), f32),
                        pltpu.VMEM((NH_R, CHUNK, CHUNK), f32), pltpu.VMEM((CHUNK, NH_R * LANES), f32),
                        pltpu.VMEM((CHUNK, NH_R * LANES), f32), pltpu.VMEM((NH_R, 2 * DK_R, DV_R), f32)],
        compiler_params=pltpu.CompilerParams(
            dimension_semantics=("parallel",), vmem_limit_bytes=V7X_VMEM_LIMIT),
        name="retention_lat" if has_s0 else "retention_ctx",
    )(*args)


def _post_kernel(l, xc_ref, xl_ref, oac_ref, oal_ref, yrc_ref, yrl_ref, ga_ref, gb_ref, wpa_hbm, wpb_hbm,
                 wo_hbm, mod_ref, n2_ref, wr_ref, br_ref, tri_ref, x1_ref, h2_ref, route_ref, cnt_ref,
                 carry_ref, wpa_ref, wpb_ref, wo_ref, stage_ref, sems):
    i = pl.program_id(0)
    tm = xc_ref.shape[0]
    n_ctx = T_CTX // tm

    @pl.when(i == 0)
    def _():
        carry_ref[...] = jnp.zeros_like(carry_ref)
        _load_cast(wpa_hbm.at[l], wpa_ref, stage_ref, sems)
        _load_cast(wpb_hbm.at[l], wpb_ref, stage_ref, sems)
        _load_cast(wo_hbm.at[l], wo_ref, stage_ref, sems)

    for r0 in range(0, tm, POST_SUB):
        _post_subtile(i < n_ctx, slice(r0, r0 + POST_SUB), xc_ref, xl_ref, oac_ref, oal_ref, yrc_ref,
                      yrl_ref, ga_ref, gb_ref, mod_ref, n2_ref, wr_ref, br_ref, tri_ref, x1_ref, h2_ref,
                      route_ref, carry_ref, wpa_ref, wpb_ref, wo_ref)
    cnt_ref[...] = carry_ref[...]


def _post_subtile(is_ctx, rows, xc_ref, xl_ref, oac_ref, oal_ref, yrc_ref, yrl_ref, ga_ref, gb_ref, mod_ref,
                  n2_ref, wr_ref, br_ref, tri_ref, x1_ref, h2_ref, route_ref, carry_ref, wpa_ref, wpb_ref,
                  wo_ref):
    tm = rows.stop - rows.start
    pick = lambda a_ref, b_ref: jnp.where(is_ctx, a_ref[rows, :], b_ref[rows, :])
    ga = _sigmoid(ga_ref[rows, :].astype(f32))
    gb = _sigmoid(gb_ref[rows, :].astype(f32))
    merged = (ga * _dot(pick(oac_ref, oal_ref), wpa_ref[...])
              + gb * _dot(pick(yrc_ref, yrl_ref), wpb_ref[...]))
    mix = _dot(merged.astype(bf16), wo_ref[...])
    x1 = pick(xc_ref, xl_ref) + mod_ref[2:3, :] * mix
    x1_ref[rows, :] = x1
    y = x1 * lax.rsqrt(jnp.mean(x1 * x1, axis=-1, keepdims=True) + EPS) * n2_ref[...]
    h2 = y * (1.0 + mod_ref[4:5, :]) + mod_ref[3:4, :]
    h2_ref[rows, 0:D] = h2

    h_hi, h_lo = _split(h2)
    both = _dot(h_hi, wr_ref[...])
    logits = both[:, 0:LANES] + (both[:, LANES:2 * LANES] + _dot(h_lo, wr_ref[:, 0:LANES]))
    lt = logits.T[0:N_EXP, :]
    scores = _sigmoid(lt)
    sel = scores + br_ref[:, 0:1]
    row = lax.broadcasted_iota(jnp.int32, (N_EXP, tm), 0)

    best = None
    bg = None
    for g in range(N_GROUPS):
        a, b, c, d = (sel[EXP_PER_GROUP * g + k:EXP_PER_GROUP * g + k + 1, :] for k in range(4))
        p, q = jnp.maximum(a, b), jnp.minimum(a, b)
        r, s = jnp.maximum(c, d), jnp.minimum(c, d)
        gs = jnp.maximum(p, r) + jnp.maximum(jnp.minimum(p, r), jnp.maximum(q, s))
        if g == 0:
            best, bg = gs, jnp.zeros((1, tm), jnp.int32)
        else:
            upd = gs > best
            bg = jnp.where(upd, g, bg)
            best = jnp.where(upd, gs, best)
    masked = jnp.where(jnp.right_shift(row, 2) == bg, sel, NEG_INF)
    m1 = jnp.max(masked, axis=0, keepdims=True)
    i1 = jnp.min(jnp.where(masked == m1, row, N_EXP), axis=0, keepdims=True)
    masked2 = jnp.where(row == i1, NEG_INF, masked)
    m2 = jnp.max(masked2, axis=0, keepdims=True)
    i2 = jnp.min(jnp.where(masked2 == m2, row, N_EXP), axis=0, keepdims=True)
    oh1 = row == i1
    oh2 = row == i2
    s1 = jnp.sum(jnp.where(oh1, scores, 0.0), axis=0, keepdims=True)
    s2 = jnp.sum(jnp.where(oh2, scores, 0.0), axis=0, keepdims=True)
    den = s1 + s2

    comb = jnp.where(oh1, s1 / den, 0.0) + jnp.where(oh2, s2 / den, 0.0)
    comb_t = jnp.concatenate([comb, jnp.zeros((LANES - N_EXP, tm), f32)], axis=0).T
    h2_ref[rows, D:D + LANES] = comb_t

    e_lo = jnp.minimum(i1, i2) - EXP_PER_GROUP * bg
    e_hi = jnp.maximum(i1, i2) - EXP_PER_GROUP * bg
    pair = jnp.right_shift(e_lo * (7 - e_lo), 1) + (e_hi - e_lo - 1)
    bucket = bg * PAIRS + pair
    rowb = lax.broadcasted_iota(jnp.int32, (BUCKET_ROWS, tm), 0)
    ohb = rowb == bucket
    ohb_f = jnp.where(ohb, 1.0, 0.0)
    tot = carry_ref[:, 0:1] + _dot(ohb_f.astype(bf16), tri_ref[...])
    rank = jnp.sum(jnp.where(ohb, tot, 0.0), axis=0, keepdims=True)
    carry_ref[...] = carry_ref[...] + jnp.sum(ohb_f, axis=1, keepdims=True)

    route_ref[0:1, rows] = bucket.astype(f32)
    route_ref[1:2, rows] = rank
    route_ref[2:8, rows] = jnp.zeros((6, tm), f32)


def _post(l, x_ctx, x_lat, oa_ctx, oa_lat, yr_ctx, yr_lat, ga, gb, wpa, wpb, wo, mod, norm2_w, wr_pad,
          br_col, tri):
    tm = TM_POST
    row = lambda w: pl.BlockSpec((tm, w), lambda i: (i, 0))
    const = lambda a: pl.BlockSpec(a.shape, lambda i: (0,) * a.ndim)
    hbm = pl.BlockSpec(memory_space=pl.ANY)
    return pl.pallas_call(
        functools.partial(_post_kernel, l),
        grid=(T // tm,),
        in_specs=(_two_stream_specs(tm, D) + _two_stream_specs(tm, Q_A) + _two_stream_specs(tm, V_R)
                  + [row(D), row(D), hbm, hbm, hbm,
                     pl.BlockSpec((None, None, 6, D), lambda i: (l, _cond_of_tile(i, tm), 0, 0)),
                     pl.BlockSpec((1, D), lambda i: (0, 0)), const(wr_pad), const(br_col), const(tri)]),
        out_specs=[row(D), row(MOE_W), pl.BlockSpec((8, tm), lambda i: (0, i)),
                   pl.BlockSpec((BUCKET_ROWS, LANES), lambda i: (0, 0))],
        out_shape=[jax.ShapeDtypeStruct((T, D), f32), jax.ShapeDtypeStruct((T, MOE_W), f32),
                   jax.ShapeDtypeStruct((8, T), f32), jax.ShapeDtypeStruct((BUCKET_ROWS, LANES), f32)],
        scratch_shapes=[pltpu.VMEM((BUCKET_ROWS, LANES), f32), pltpu.VMEM((Q_A, D), bf16),
                        pltpu.VMEM((V_R, D), bf16), pltpu.VMEM((D, D), bf16),
                        pltpu.VMEM((2, D, W_CHUNK), f32), pltpu.SemaphoreType.DMA((2,))],
        compiler_params=pltpu.CompilerParams(
            dimension_semantics=("arbitrary",), vmem_limit_bytes=V7X_VMEM_LIMIT),
        name="post_router",
    )(x_ctx, x_lat, oa_ctx, oa_lat, yr_ctx, yr_lat, ga, gb, wpa, wpb, wo, mod, norm2_w[l:l + 1],
      wr_pad, br_col, tri)


def _row_copy(src_ref, src_row, dst_ref, dst_row, sem):
    return pltpu.make_async_copy(src_ref.at[pl.ds(src_row, 1)], dst_ref.at[pl.ds(dst_row, 1)], sem)


def _dispatch_kernel(zf_ref, pos_ref, h_ref, xs_ref, zero_ref, sem):
    tm = h_ref.shape[0]

    @pl.when(pl.program_id(0) == 0)
    def _():
        zero_ref[...] = jnp.zeros_like(zero_ref)

        def tile_copy(t):
            return pltpu.make_async_copy(zero_ref, xs_ref.at[pl.ds(t * MOE_TR, MOE_TR)], sem)

        def start(t, carry):
            @pl.when(zf_ref[t] != 0)
            def _():
                tile_copy(t).start()
            return carry

        def wait(t, carry):
            @pl.when(zf_ref[t] != 0)
            def _():
                tile_copy(t).wait()
            return carry

        lax.fori_loop(0, MOE_TILES, start, 0)
        lax.fori_loop(0, MOE_TILES, wait, 0)

    for r in range(tm):
        _row_copy(h_ref, r, xs_ref, pos_ref[0, 0, r], sem).start(priority=r % 2)
    pltpu.make_async_copy(h_ref, xs_ref.at[pl.ds(0, tm)], sem).wait()


def _dispatch(zero_flags, pos_blocks, h2):
    tm = TM_DISP
    return pl.pallas_call(
        _dispatch_kernel,
        grid_spec=pltpu.PrefetchScalarGridSpec(
            num_scalar_prefetch=1,
            grid=(T // tm,),
            in_specs=[pl.BlockSpec((1, 1, tm), lambda i, zf: (i, 0, 0), memory_space=pltpu.SMEM),
                      pl.BlockSpec((tm, MOE_W), lambda i, zf: (i, 0))],
            out_specs=pl.BlockSpec(memory_space=pl.ANY),
            scratch_shapes=[pltpu.VMEM((MOE_TR, MOE_W), f32), pltpu.SemaphoreType.DMA(())],
        ),
        out_shape=jax.ShapeDtypeStruct((MOE_ROWS, MOE_W), f32),
        compiler_params=pltpu.CompilerParams(
            dimension_semantics=("arbitrary",), vmem_limit_bytes=V7X_VMEM_LIMIT),
        name="moe_dispatch",
    )(zero_flags, pos_blocks, h2)


def _experts_kernel(l, tg_ref, ta_ref, tb_ref, first_ref, nxt_ref, nt_ref, xs_ref, wg_hbm, wu_hbm, wd_hbm,
                    y_ref, wg_ref, wu_ref, wd_ref, sg_ref, su_ref, sd_ref, sems):
    i = pl.program_id(0)
    live = i < nt_ref[0]
    g = tg_ref[i]

    def fetch(grp):
        return [pltpu.make_async_copy(w.at[l, pl.ds(grp * EXP_PER_GROUP, EXP_PER_GROUP)], s, sems.at[k])
                for k, (w, s) in enumerate(((wg_hbm, sg_ref), (wu_hbm, su_ref), (wd_hbm, sd_ref)))]

    @pl.when(i == 0)
    def _():
        for c in fetch(tg_ref[0]):
            c.start()

    @pl.when(live & (first_ref[i] != 0))
    def _():
        for c in fetch(g):
            c.wait()
        for j in range(EXP_PER_GROUP):
            wg_ref[j] = sg_ref[j].astype(bf16)
            wu_ref[j] = su_ref[j].astype(bf16)
            wd_ref[j] = sd_ref[j].astype(bf16)

        @pl.when(nxt_ref[i] >= 0)
        def _():
            for c in fetch(nxt_ref[i]):
                c.start(priority=1)

    @pl.when(live)
    def _():
        x = xs_ref[:, 0:D].astype(bf16)
        comb = xs_ref[:, D:D + LANES]
        lane = lax.broadcasted_iota(jnp.int32, comb.shape, 1)
        acc = None
        for j in (ta_ref[i], tb_ref[i]):
            w_j = jnp.sum(jnp.where(lane == g * EXP_PER_GROUP + j, comb, 0.0), axis=1, keepdims=True)
            act = _silu(_dot(x, wg_ref[j])) * _dot(x, wu_ref[j]) * w_j
            part = _dot(act.astype(bf16), wd_ref[j])
            acc = part if acc is None else acc + part
        y_ref[...] = acc

    @pl.when(i >= nt_ref[0])
    def _():
        y_ref[...] = jnp.zeros_like(y_ref)


def _experts(l, plan, xs, wg, wu, wd):
    def tile(i, *prefetch):
        return jnp.minimum(i, prefetch[-1][0] - 1)

    hbm = pl.BlockSpec(memory_space=pl.ANY)
    return pl.pallas_call(
        functools.partial(_experts_kernel, l),
        grid_spec=pltpu.PrefetchScalarGridSpec(
            num_scalar_prefetch=6,
            grid=(MOE_TILES,),
            in_specs=[pl.BlockSpec((MOE_TR, MOE_W), lambda i, *p: (tile(i, *p), 0)), hbm, hbm, hbm],
            out_specs=pl.BlockSpec((MOE_TR, D), lambda i, *p: (i, 0)),
            scratch_shapes=[pltpu.VMEM((EXP_PER_GROUP, D, D_EXP), bf16),
                            pltpu.VMEM((EXP_PER_GROUP, D, D_EXP), bf16),
                            pltpu.VMEM((EXP_PER_GROUP, D_EXP, D), bf16),
                            pltpu.VMEM((EXP_PER_GROUP, D, D_EXP), f32),
                            pltpu.VMEM((EXP_PER_GROUP, D, D_EXP), f32),
                            pltpu.VMEM((EXP_PER_GROUP, D_EXP, D), f32), pltpu.SemaphoreType.DMA((3,))],
        ),
        out_shape=jax.ShapeDtypeStruct((MOE_ROWS, D), f32),
        compiler_params=pltpu.CompilerParams(
            dimension_semantics=("arbitrary",), vmem_limit_bytes=V7X_VMEM_LIMIT),
        name="moe_experts",
    )(*plan, xs, wg, wu, wd)


def _combine_kernel(pos_ref, posn_ref, x1_ref, mod_ref, y_ref, oc_ref, ol_ref, ya_ref, yb_ref, sems):
    tm = x1_ref.shape[0]
    i = pl.program_id(0)
    n_ctx = T_CTX // tm

    def pull(p_ref, buf, sem):
        for r in range(tm):
            _row_copy(y_ref, p_ref[0, 0, r], buf, r, sem).start(priority=r % 2)

    @pl.when(i == 0)
    def _():
        pull(pos_ref, ya_ref, sems.at[0])

    def step(cur, nxt, sem_cur, sem_nxt):
        @pl.when(i + 1 < pl.num_programs(0))
        def _():
            pull(posn_ref, nxt, sem_nxt)

        pltpu.make_async_copy(y_ref.at[pl.ds(0, tm)], cur, sem_cur).wait()
        out = x1_ref[...] + mod_ref[5:6, :] * cur[...]

        @pl.when(i < n_ctx)
        def _():
            oc_ref[...] = out

        @pl.when(i >= n_ctx)
        def _():
            ol_ref[...] = out

    @pl.when(i % 2 == 0)
    def _():
        step(ya_ref, yb_ref, sems.at[0], sems.at[1])

    @pl.when(i % 2 == 1)
    def _():
        step(yb_ref, ya_ref, sems.at[1], sems.at[0])


def _combine(l, pos_blocks, x1, mod, y):
    tm = TM_DISP
    n = T // tm
    return pl.pallas_call(
        _combine_kernel,
        grid=(n,),
        in_specs=[pl.BlockSpec((1, 1, tm), lambda i: (i, 0, 0), memory_space=pltpu.SMEM),
                  pl.BlockSpec((1, 1, tm), lambda i: (jnp.minimum(i + 1, n - 1), 0, 0),
                               memory_space=pltpu.SMEM),
                  pl.BlockSpec((tm, D), lambda i: (i, 0)),
                  pl.BlockSpec((None, None, 6, D), lambda i: (l, _cond_of_tile(i, tm), 0, 0)),
                  pl.BlockSpec(memory_space=pl.ANY)],
        out_specs=_two_stream_specs(tm, D),
        out_shape=[jax.ShapeDtypeStruct((T_CTX, D), f32), jax.ShapeDtypeStruct((T_LAT, D), f32)],
        scratch_shapes=[pltpu.VMEM((tm, D), f32), pltpu.VMEM((tm, D), f32),
                        pltpu.SemaphoreType.DMA((2,))],
        compiler_params=pltpu.CompilerParams(
            dimension_semantics=("arbitrary",), vmem_limit_bytes=V7X_VMEM_LIMIT),
        name="moe_combine",
    )(pos_blocks, pos_blocks, x1, mod, y)


def _moe_plan(route, counts):
    cnt = counts[0:N_BUCKETS, 0].astype(jnp.int32)
    tiles = (cnt + MOE_TR - 1) // MOE_TR
    tile_end = jnp.cumsum(tiles)
    tile_start = tile_end - tiles
    off = (tile_start * MOE_TR).astype(f32)
    buckets = jnp.arange(N_BUCKETS, dtype=f32)[:, None]
    pos = jnp.sum(jnp.where(route[0][None, :] == buckets, off[:, None], 0.0), axis=0) + route[1]
    pos_blocks = pos.astype(jnp.int32).reshape(T // TM_DISP, 1, TM_DISP)
    tid = jnp.arange(MOE_TILES, dtype=jnp.int32)
    tb = jnp.minimum(jnp.sum(tid[:, None] >= tile_end[None, :], axis=1), N_BUCKETS - 1).astype(jnp.int32)
    n_tiles = tile_end[-1:].astype(jnp.int32)
    is_last = jnp.any((tid[:, None] == tile_end[None, :] - 1) & (tiles > 0)[None, :], axis=1)
    zero_flags = ((tid >= n_tiles[0]) | is_last).astype(jnp.int32)
    tg = tb // PAIRS
    pair = tb % PAIRS
    pair_lo = np.array([a for a in range(EXP_PER_GROUP) for _ in range(a + 1, EXP_PER_GROUP)])
    pair_hi = np.array([b for a in range(EXP_PER_GROUP) for b in range(a + 1, EXP_PER_GROUP)])
    of_pair = pair[:, None] == jnp.arange(PAIRS)[None, :]
    ta = jnp.sum(jnp.where(of_pair, pair_lo[None, :], 0), axis=1).astype(jnp.int32)
    tb2 = jnp.sum(jnp.where(of_pair, pair_hi[None, :], 0), axis=1).astype(jnp.int32)
    g_tiles = tiles.reshape(N_GROUPS, PAIRS).sum(axis=1)
    g_end = jnp.cumsum(g_tiles)
    g_start = g_end - g_tiles
    used = g_tiles > 0
    gid = jnp.arange(N_GROUPS, dtype=jnp.int32)
    first = jnp.any((tid[:, None] == g_start[None, :]) & used[None, :], axis=1).astype(jnp.int32)
    later = (gid[None, :] > gid[:, None]) & used[None, :]
    nxt_g = jnp.min(jnp.where(later, gid[None, :], N_GROUPS), axis=1)
    nxt_g = jnp.where(nxt_g == N_GROUPS, -1, nxt_g)
    nxt = jnp.sum(jnp.where(tg[:, None] == gid[None, :], nxt_g[None, :], 0), axis=1).astype(jnp.int32)
    return pos_blocks, zero_flags, (tg.astype(jnp.int32), ta, tb2, first, nxt, n_tiles)


def _rope_tables():
    pos = np.arange(DEC_SEQ)
    half = HD_A // 4
    freqs = ROPE_BASE ** (-np.arange(half, dtype=np.float64) / half)
    ang_r = (pos // GRID_W)[:, None] * freqs[None, :]
    ang_c = (pos % GRID_W)[:, None] * freqs[None, :]
    ang = np.concatenate([ang_r, ang_r, ang_c, ang_c], axis=1)
    sign = np.concatenate([-np.ones(half), np.ones(half)] * 2)[None, :]
    cos = np.tile(np.cos(ang), (1, 2))
    sin = np.tile(np.sin(ang) * sign, (1, 2))
    ident_c = np.ones((TM_PROJ, LANES))
    ident_s = np.zeros((TM_PROJ, LANES))
    return (jnp.asarray(np.concatenate([cos, ident_c]), f32),
            jnp.asarray(np.concatenate([sin, ident_s]), f32))


def kernel(x_prompt, x_sample, c, cache_k, cache_v, state_ret_fwd, state_ret_bwd, c_ctx,
           norm1_w, norm2_w, w_ada, b_ada, w_in, q_norm_w, k_norm_w, attn_sink,
           ret_decay_fwd, ret_decay_bwd, ret_gn_w, w_pa, w_pb, w_o, w_router, b_router,
           w_exp_gate, w_exp_up, w_exp_down):
    x_ctx, x_lat = x_prompt.reshape(T_CTX, D), x_sample.reshape(T_LAT, D)
    cond8 = jnp.zeros((N_COND, D), f32).at[0].set(c_ctx).at[1:1 + DEC_BATCH].set(c)
    mod = _adaln(cond8, w_ada, b_ada).reshape(DEPTH, N_COND, 6, D)

    cos_t, sin_t = _rope_tables()
    blk = np.arange(Q_A) // HD_A
    ones_blk = jnp.asarray((blk[:, None] == blk[None, :]) / HD_A, bf16)
    tri = jnp.asarray(np.triu(np.ones((POST_SUB, POST_SUB)), 1), bf16)
    qnw = jnp.tile(q_norm_w, (1, NH_A))
    knw = jnp.tile(k_norm_w, (1, NKV_A))
    wr_pad = jnp.concatenate(_split(jnp.pad(w_router, ((0, 0), (0, LANES - N_EXP)))), axis=1)
    br_col = jnp.broadcast_to(b_router[:, None], (N_EXP, LANES))
    ck = cache_k.reshape(DEC_BATCH, DEPTH, PAST, KV_A)
    cv = cache_v.reshape(DEC_BATCH, DEPTH, PAST, KV_A)
    s0 = jnp.concatenate([state_ret_fwd, state_ret_bwd], axis=3)
    lg = jnp.log1p(-jnp.exp(jnp.concatenate([ret_decay_fwd, ret_decay_bwd], axis=1).astype(f32)))

    new_k, new_v, new_sf, new_sb = [], [], [], []
    for l in range(DEPTH):
        qa, ka, va, qr, kr, vr, gr, ga, gb = _inproj(l, x_ctx, x_lat, norm1_w, mod, w_in, cos_t, sin_t,
                                                     ones_blk, qnw, knw)
        oa_ctx = _ctx_attn(l, qa, ka, va, attn_sink)
        oa_lat = _lat_attn(l, qa, ka, va, ck, cv, cos_t, sin_t, attn_sink)
        yr_ctx, sf, sb = _retention(l, qr, kr, vr, gr, None, lg, ret_gn_w, BATCH, SEQ, 0,
                                    CTX_SEQS_PER_STEP)
        yr_lat, _, _ = _retention(l, qr, kr, vr, gr, s0[:, l], lg, ret_gn_w, DEC_BATCH, DEC_SEQ,
                                  T_CTX // DEC_SEQ, 1)
        x1, h2, route, counts = _post(l, x_ctx, x_lat, oa_ctx, oa_lat, yr_ctx, yr_lat, ga, gb,
                                      w_pa, w_pb, w_o, mod, norm2_w, wr_pad, br_col, tri)
        pos_blocks, zero_flags, plan = _moe_plan(route, counts)
        xs = _dispatch(zero_flags, pos_blocks, h2)
        y = _experts(l, plan, xs, w_exp_gate, w_exp_up, w_exp_down)
        x_ctx, x_lat = _combine(l, pos_blocks, x1, mod, y)

        new_k.append(ka[:T_CTX].reshape(BATCH, SEQ, NKV_A, HD_A))
        new_v.append(va[:T_CTX].reshape(BATCH, SEQ, NKV_A, HD_A))
        new_sf.append(sf)
        new_sb.append(sb)

    return (x_ctx.reshape(BATCH, SEQ, D), x_lat.reshape(DEC_BATCH, DEC_SEQ, D),
            jnp.stack(new_k, axis=1), jnp.stack(new_v, axis=1),
            jnp.stack(new_sf, axis=1), jnp.stack(new_sb, axis=1))
```

```python
import functools

import numpy as np
import jax
import jax.numpy as jnp
from jax import lax
from jax.experimental import pallas as pl
from jax.experimental.pallas import tpu as pltpu

D = 1024
BATCH, SEQ = 16, 256
DEC_BATCH, DEC_SEQ = 2, 2048
DEPTH = 2
PAST = 512
GRID_W = 64
NH_A, NKV_A, HD_A = 8, 2, 64
WINDOW = 128
NH_R, DK_R, DV_R = 4, 64, 128
CHUNK = 128
N_EXP, N_GROUPS, EXP_PER_GROUP = 16, 4, 4
D_EXP = 512
ROPE_BASE = 10000.0
EPS = 1e-6
NEG_INF = -1e30

Q_A = NH_A * HD_A
KV_A = NKV_A * HD_A
QK_R = NH_R * DK_R
V_R = NH_R * DV_R
C_QA = (0, Q_A)
C_KA = (C_QA[1], C_QA[1] + KV_A)
C_VA = (C_KA[1], C_KA[1] + KV_A)
C_QR = (C_VA[1], C_VA[1] + QK_R)
C_KR = (C_QR[1], C_QR[1] + QK_R)
C_VR = (C_KR[1], C_KR[1] + V_R)
C_GR = (C_VR[1], C_VR[1] + V_R)
C_GA = (C_GR[1], C_GR[1] + D)
C_GB = (C_GA[1], C_GA[1] + D)
D_IN = C_GB[1]

T_CTX = BATCH * SEQ
T_LAT = DEC_BATCH * DEC_SEQ
T = T_CTX + T_LAT
N_COND = 8

LANES = 128
V7X_VMEM_LIMIT = 56 * 1024 * 1024

TM_PROJ = 512
TM_POST = 512
POST_SUB = 256
W_CHUNK = 256
TM_DISP = 512
MOE_TR = 256
PAIRS = EXP_PER_GROUP * (EXP_PER_GROUP - 1) // 2
N_BUCKETS = N_GROUPS * PAIRS
BUCKET_ROWS = 32
MOE_TILES = T // MOE_TR + N_BUCKETS
MOE_ROWS = MOE_TILES * MOE_TR
MOE_W = D + LANES
ATT_QB = 256
RET_UNROLL = 4
CTX_SEQS_PER_STEP = 4

f32 = jnp.float32
bf16 = jnp.bfloat16


def _dot(a, b):
    return jnp.dot(a, b, preferred_element_type=f32)


def _dot_t(a, b):
    return lax.dot_general(a, b, (((1,), (1,)), ((), ())), preferred_element_type=f32)


def _split(x):
    hi = x.astype(bf16)
    lo = (x - hi.astype(f32)).astype(bf16)
    return hi, lo


def _dot3(a, b):
    ah, al = _split(a)
    bh, bl = _split(b)
    return _dot(ah, bh) + (_dot(ah, bl) + _dot(al, bh))


def _sigmoid(x):
    return 1.0 / (1.0 + jnp.exp(-x))


def _silu(x):
    return x * _sigmoid(x)


def _cond_of_tile(i, tm):
    n_ctx = T_CTX // tm
    per_b = DEC_SEQ // tm
    return jnp.where(i < n_ctx, 0, 1 + jnp.maximum(i - n_ctx, 0) // per_b)


def _adaln_kernel(cond_ref, w_ref, b_ref, o_ref):
    a = _silu(cond_ref[...])
    o_ref[0] = _dot3(a, w_ref[0]) + b_ref[0]


def _adaln(cond8, w_ada, b_ada):
    tn = 2048
    return pl.pallas_call(
        _adaln_kernel,
        grid=(DEPTH, 6 * D // tn),
        in_specs=[
            pl.BlockSpec((N_COND, D), lambda l, j: (0, 0)),
            pl.BlockSpec((1, D, tn), lambda l, j: (l, 0, j)),
            pl.BlockSpec((1, 1, tn), lambda l, j: (l, 0, j)),
        ],
        out_specs=pl.BlockSpec((1, N_COND, tn), lambda l, j: (l, 0, j)),
        out_shape=jax.ShapeDtypeStruct((DEPTH, N_COND, 6 * D), f32),
        compiler_params=pltpu.CompilerParams(
            dimension_semantics=("parallel", "parallel"), vmem_limit_bytes=V7X_VMEM_LIMIT),
        name="adaln",
    )(cond8, w_ada, b_ada.reshape(DEPTH, 1, 6 * D))


def _rope(x, cos, sin_signed, first_half):
    fwd = pltpu.roll(x, 16, 1)
    bwd = pltpu.roll(x, LANES - 16, 1)
    partner = jnp.where(first_half, bwd, fwd)
    return x * cos + partner * sin_signed


def _head_rms(x, ones_blk, w):
    n = x.shape[1]
    sq_hi, sq_lo = _split(x * x)
    blk = ones_blk[0:n, 0:n]
    mean = _dot(sq_hi, blk) + _dot(sq_lo, blk)
    return x * lax.rsqrt(mean + EPS) * w


def _load_cast(w_hbm, dst_ref, stage_ref, sems):
    k, n = w_hbm.shape

    def chunk_copy(c):
        return pltpu.make_async_copy(w_hbm.at[:, pl.ds(c * W_CHUNK, W_CHUNK)],
                                     stage_ref.at[c % 2, pl.ds(0, k)], sems.at[c % 2])

    n_chunks = n // W_CHUNK
    chunk_copy(0).start()
    for c in range(n_chunks):
        if c + 1 < n_chunks:
            chunk_copy(c + 1).start()
        chunk_copy(c).wait()
        dst_ref[:, c * W_CHUNK:(c + 1) * W_CHUNK] = stage_ref[c % 2, 0:k, :].astype(bf16)


def _tile_of_two(i, n_first, a_ref, b_ref):
    return jnp.where(i < n_first, a_ref[...], b_ref[...])


def _inproj_kernel(l, xc_ref, xl_ref, n1_ref, mod_ref, w_hbm, cos_ref, sin_ref, ones_ref, qnw_ref,
                   knw_ref, qa_ref, ka_ref, va_ref, qr_ref, kr_ref, vr_ref, gr_ref, ga_ref, gb_ref,
                   w_ref, stage_ref, sems):
    i = pl.program_id(0)

    @pl.when(i == 0)
    def _():
        _load_cast(w_hbm.at[l], w_ref, stage_ref, sems)

    x = _tile_of_two(i, T_CTX // xc_ref.shape[0], xc_ref, xl_ref)
    y = x * lax.rsqrt(jnp.mean(x * x, axis=-1, keepdims=True) + EPS) * n1_ref[...]
    h = (y * (1.0 + mod_ref[1:2, :]) + mod_ref[0:1, :]).astype(bf16)

    def proj(c):
        return _dot(h, w_ref[:, c[0]:c[1]])

    cos = cos_ref[...]
    sin = sin_ref[...]
    lane = lax.broadcasted_iota(jnp.int32, cos.shape, 1)
    first_half = (lane % 32) < 16

    def rope_all(v):
        parts = [_rope(v[:, j:j + LANES], cos, sin, first_half) for j in range(0, v.shape[1], LANES)]
        return parts[0] if len(parts) == 1 else jnp.concatenate(parts, axis=1)

    ones_blk = ones_ref[...]
    qa = _head_rms(proj(C_QA), ones_blk, qnw_ref[...])
    qa_ref[...] = (rope_all(qa) * (HD_A ** -0.5)).astype(bf16)
    ka_ref[...] = _head_rms(proj(C_KA), ones_blk, knw_ref[...])
    va_ref[...] = proj(C_VA)
    qr_ref[...] = rope_all(proj(C_QR)).astype(bf16)
    kr_ref[...] = (rope_all(proj(C_KR)) * (DK_R ** -0.5)).astype(bf16)
    vr_ref[...] = proj(C_VR).astype(bf16)
    gr_ref[...] = proj(C_GR).astype(bf16)
    ga_ref[...] = proj(C_GA).astype(bf16)
    gb_ref[...] = proj(C_GB).astype(bf16)


def _two_stream_specs(tm, width):
    n_ctx = T_CTX // tm
    return [pl.BlockSpec((tm, width), lambda i, *_: (jnp.minimum(i, n_ctx - 1), 0)),
            pl.BlockSpec((tm, width), lambda i, *_: (jnp.maximum(i - n_ctx, 0), 0))]


def _inproj(l, x_ctx, x_lat, norm1_w, mod, w_in, cos_t, sin_t, ones_blk, qnw, knw):
    tm = TM_PROJ
    n_ctx = T_CTX // tm
    per_b = DEC_SEQ // tm

    def tab_map(i):
        return (jnp.where(i < n_ctx, per_b, jnp.maximum(i - n_ctx, 0) % per_b), 0)

    row = lambda i: (i, 0)
    const = lambda i: (0, 0)
    widths = [(Q_A, bf16), (KV_A, f32), (KV_A, f32), (QK_R, bf16), (QK_R, bf16), (V_R, bf16),
              (V_R, bf16), (D, bf16), (D, bf16)]
    return pl.pallas_call(
        functools.partial(_inproj_kernel, l),
        grid=(T // tm,),
        in_specs=_two_stream_specs(tm, D) + [
            pl.BlockSpec((1, D), const),
            pl.BlockSpec((None, None, 6, D), lambda i: (l, _cond_of_tile(i, tm), 0, 0)),
            pl.BlockSpec(memory_space=pl.ANY),
            pl.BlockSpec((tm, LANES), tab_map),
            pl.BlockSpec((tm, LANES), tab_map),
            pl.BlockSpec((Q_A, Q_A), const),
            pl.BlockSpec((1, Q_A), const),
            pl.BlockSpec((1, KV_A), const),
        ],
        out_specs=[pl.BlockSpec((tm, w), row) for w, _ in widths],
        out_shape=[jax.ShapeDtypeStruct((T, w), dt) for w, dt in widths],
        scratch_shapes=[pltpu.VMEM((D, D_IN), bf16), pltpu.VMEM((2, D, W_CHUNK), f32),
                        pltpu.SemaphoreType.DMA((2,))],
        compiler_params=pltpu.CompilerParams(
            dimension_semantics=("arbitrary",), vmem_limit_bytes=V7X_VMEM_LIMIT),
        name="inproj",
    )(x_ctx, x_lat, norm1_w[l:l + 1], mod, w_in, cos_t, sin_t, ones_blk, qnw[l:l + 1], knw[l:l + 1])


def _head_blocks(t, kv, lo_mask):
    r = pltpu.roll(t, HD_A, 1)
    if kv == 0:
        a = jnp.where(lo_mask, t, 0.0)
        b = jnp.where(lo_mask, 0.0, r)
    else:
        a = jnp.where(lo_mask, r, 0.0)
        b = jnp.where(lo_mask, 0.0, t)
    return jnp.concatenate([a, b], axis=0).astype(bf16)


def _ctx_attn_kernel(sink_ref, q_ref, k_ref, v_ref, o_ref):
    n = SEQ
    lo_mask = lax.broadcasted_iota(jnp.int32, (n, LANES), 1) < HD_A
    for b in range(CTX_SEQS_PER_STEP):
        rows = slice(b * n, (b + 1) * n)
        k = k_ref[rows, :]
        v = v_ref[rows, :]
        for kv in range(NKV_A):
            kblk = _head_blocks(k, kv, lo_mask)
            vblk = _head_blocks(v, kv, lo_mask)
            for pr in range(2):
                pi = kv * 2 + pr
                s = _dot_t(q_ref[rows, pi * LANES:(pi + 1) * LANES], kblk)
                ps, invs = [], []
                for hh in range(2):
                    sk = sink_ref[2 * pi + hh]
                    sh = s[:, hh * n:(hh + 1) * n]
                    m = jnp.maximum(jnp.max(sh, axis=-1, keepdims=True), sk)
                    p = jnp.exp(sh - m)
                    invs.append(1.0 / (jnp.sum(p, axis=-1, keepdims=True) + jnp.exp(sk - m)))
                    ps.append(p.astype(bf16))
                o = _dot(jnp.concatenate(ps, axis=1), vblk) * jnp.where(lo_mask, invs[0], invs[1])
                o_ref[rows, pi * LANES:(pi + 1) * LANES] = o.astype(bf16)


def _ctx_attn(l, qa, ka, va, sink):
    blk = lambda w: pl.BlockSpec((CTX_SEQS_PER_STEP * SEQ, w), lambda b: (b, 0))
    return pl.pallas_call(
        _ctx_attn_kernel,
        grid=(BATCH // CTX_SEQS_PER_STEP,),
        in_specs=[pl.BlockSpec(memory_space=pltpu.SMEM), blk(Q_A), blk(KV_A), blk(KV_A)],
        out_specs=blk(Q_A),
        out_shape=jax.ShapeDtypeStruct((T_CTX, Q_A), bf16),
        compiler_params=pltpu.CompilerParams(
            dimension_semantics=("parallel",), vmem_limit_bytes=V7X_VMEM_LIMIT),
        name="ctx_attn",
    )(sink[l], qa, ka, va)


def _lat_attn_kernel(sink_ref, q_ref, k_ref, v_ref, kc_ref, vc_ref, cos_ref, sin_ref, o_ref):
    j = pl.program_id(1)
    qb = ATT_QB
    win = 2 * qb
    ws = pl.multiple_of(jnp.clip(j * qb - WINDOW, 0, DEC_SEQ - win), WINDOW)
    lo_mask = lax.broadcasted_iota(jnp.int32, (win, LANES), 1) < HD_A
    lane = lax.broadcasted_iota(jnp.int32, (win, LANES), 1)
    kw = _rope(k_ref[pl.ds(ws, win), :], cos_ref[pl.ds(ws, win), :], sin_ref[pl.ds(ws, win), :],
               (lane % 32) < 16)
    vw = v_ref[pl.ds(ws, win), :]
    kc = kc_ref[...]
    vc = vc_ref[...]
    qpos = j * qb + (lax.broadcasted_iota(jnp.int32, (2 * qb, win), 0) & (qb - 1))
    kpos = ws + lax.broadcasted_iota(jnp.int32, (2 * qb, win), 1)
    valid = jnp.abs(qpos - kpos) <= WINDOW
    out_lo = lax.broadcasted_iota(jnp.int32, (2 * qb, LANES), 1) < HD_A
    for kv in range(NKV_A):
        kc_blk = _head_blocks(kc, kv, lo_mask[:PAST])
        vc_blk = _head_blocks(vc, kv, lo_mask[:PAST])
        kw_blk = _head_blocks(kw, kv, lo_mask)
        vw_blk = _head_blocks(vw, kv, lo_mask)
        q2 = jnp.concatenate([q_ref[:, (2 * kv) * LANES:(2 * kv + 1) * LANES],
                              q_ref[:, (2 * kv + 1) * LANES:(2 * kv + 2) * LANES]], axis=0)
        s_c = _dot_t(q2, kc_blk)
        s_w = _dot_t(q2, kw_blk)
        pcs, pws, invs = [], [], []
        for hh in range(2):
            row = lax.broadcasted_iota(jnp.int32, (2 * qb, 1), 0)
            sk = jnp.where(row < qb, sink_ref[4 * kv + hh], sink_ref[4 * kv + 2 + hh])
            sc = s_c[:, hh * PAST:(hh + 1) * PAST]
            sw = jnp.where(valid, s_w[:, hh * win:(hh + 1) * win], NEG_INF)
            m = jnp.maximum(jnp.maximum(jnp.max(sc, axis=-1, keepdims=True),
                                        jnp.max(sw, axis=-1, keepdims=True)), sk)
            pc = jnp.exp(sc - m)
            pw = jnp.exp(sw - m)
            den = (jnp.sum(pc, axis=-1, keepdims=True) + jnp.sum(pw, axis=-1, keepdims=True)
                   + jnp.exp(sk - m))
            invs.append(1.0 / den)
            pcs.append(pc.astype(bf16))
            pws.append(pw.astype(bf16))
        o = _dot(jnp.concatenate(pcs, axis=1), vc_blk) + _dot(jnp.concatenate(pws, axis=1), vw_blk)
        o = o * jnp.where(out_lo, invs[0], invs[1])
        o_ref[:, (2 * kv) * LANES:(2 * kv + 1) * LANES] = o[:qb].astype(bf16)
        o_ref[:, (2 * kv + 1) * LANES:(2 * kv + 2) * LANES] = o[qb:].astype(bf16)


def _lat_attn(l, qa, ka, va, cache_k, cache_v, cos_l, sin_l, sink):
    qb = ATT_QB
    nq = DEC_SEQ // qb
    ctx_blocks = T_CTX // DEC_SEQ
    seq = lambda b, j: (ctx_blocks + b, 0)
    return pl.pallas_call(
        _lat_attn_kernel,
        grid=(DEC_BATCH, nq),
        in_specs=[
            pl.BlockSpec(memory_space=pltpu.SMEM),
            pl.BlockSpec((qb, Q_A), lambda b, j: (T_CTX // qb + b * nq + j, 0)),
            pl.BlockSpec((DEC_SEQ, KV_A), seq),
            pl.BlockSpec((DEC_SEQ, KV_A), seq),
            pl.BlockSpec((None, None, PAST, KV_A), lambda b, j: (b, l, 0, 0)),
            pl.BlockSpec((None, None, PAST, KV_A), lambda b, j: (b, l, 0, 0)),
            pl.BlockSpec((DEC_SEQ, LANES), lambda b, j: (0, 0)),
            pl.BlockSpec((DEC_SEQ, LANES), lambda b, j: (0, 0)),
        ],
        out_specs=pl.BlockSpec((qb, Q_A), lambda b, j: (b * nq + j, 0)),
        out_shape=jax.ShapeDtypeStruct((T_LAT, Q_A), bf16),
        compiler_params=pltpu.CompilerParams(
            dimension_semantics=("parallel", "parallel"), vmem_limit_bytes=V7X_VMEM_LIMIT),
        name="lat_attn",
    )(sink[l], qa, ka, va, cache_k, cache_v, cos_l, sin_l)


def _dup_heads(pair, lo_mask):
    r = pltpu.roll(pair, DK_R, 1)
    return jnp.where(lo_mask, pair, r), jnp.where(lo_mask, r, pair)


def _retention_kernel(has_s0, n_seq, n_chunks, *refs):
    if has_s0:
        (lg_ref, q_ref, k_ref, v_ref, g_ref, s0_ref, gnw_ref,
         y_ref, sf_ref, sb_ref, ds_ref, st_ref, mask_ref, qdec_ref, kdec_ref, cdec_ref) = refs
    else:
        (lg_ref, q_ref, k_ref, v_ref, g_ref, gnw_ref,
         y_ref, sf_ref, sb_ref, ds_ref, st_ref, mask_ref, qdec_ref, kdec_ref, cdec_ref) = refs
        s0_ref = None
    C = CHUNK
    lo_mask = lax.broadcasted_iota(jnp.int32, (C, LANES), 1) < DK_R

    ri = lax.broadcasted_iota(jnp.int32, (C, C), 0).astype(f32)
    diff = ri - lax.broadcasted_iota(jnp.int32, (C, C), 1).astype(f32)
    for h in range(NH_R):
        lg_f, lg_b = lg_ref[h], lg_ref[NH_R + h]
        mask_ref[h] = (jnp.where(diff >= 0, jnp.exp(jnp.maximum(diff, 0.0) * lg_f), 0.0)
                       + jnp.where(diff <= 0, jnp.exp(jnp.maximum(-diff, 0.0) * lg_b), 0.0))
        qdec_ref[:, h * LANES:(h + 1) * LANES] = jnp.exp(
            jnp.where(lo_mask, (ri + 1.0) * lg_f, (C - ri) * lg_b))
        kdec_ref[:, h * LANES:(h + 1) * LANES] = jnp.exp(
            jnp.where(lo_mask, (C - 1.0 - ri) * lg_f, ri * lg_b))
        cdec_ref[h] = jnp.exp(jnp.where(ri < DK_R, C * lg_f, C * lg_b))

    def inc_body(c, carry):
        r0 = pl.multiple_of(c * C, C)
        for pr in range(2):
            kp = k_ref[pl.ds(r0, C), pr * LANES:(pr + 1) * LANES].astype(f32)
            for hh, kd in enumerate(_dup_heads(kp, lo_mask)):
                h = 2 * pr + hh
                kd = (kd * kdec_ref[:, h * LANES:(h + 1) * LANES]).astype(bf16)
                vh = v_ref[pl.ds(r0, C), h * DV_R:(h + 1) * DV_R]
                ds_ref[c, h] = lax.dot_general(kd, vh, (((0,), (0,)), ((), ())),
                                               preferred_element_type=f32)
        return carry

    lax.fori_loop(0, n_seq * n_chunks, inc_body, 0, unroll=RET_UNROLL)

    for sq in range(n_seq):
        base = sq * n_chunks
        for h in range(NH_R):
            cf = cdec_ref[h, 0:DK_R, :]
            cb = cdec_ref[h, DK_R:2 * DK_R, :]
            if has_s0:
                init_f = s0_ref[h, 0:DK_R, :]
                init_b = s0_ref[h, DK_R:2 * DK_R, :]
            else:
                init_f = jnp.zeros((DK_R, DV_R), f32)
                init_b = init_f

            def fwd_body(i, s, h=h, cf=cf, base=base):
                c = base + i
                st_ref[c, h, 0:DK_R, :] = s
                return s * cf + ds_ref[c, h, 0:DK_R, :]

            def bwd_body(i, s, h=h, cb=cb, base=base):
                c = base + n_chunks - 1 - i
                st_ref[c, h, DK_R:2 * DK_R, :] = s
                return s * cb + ds_ref[c, h, DK_R:2 * DK_R, :]

            sf_ref[sq, h] = lax.fori_loop(0, n_chunks, fwd_body, init_f)
            sb_ref[sq, h] = lax.fori_loop(0, n_chunks, bwd_body, init_b)

    def out_body(c, carry):
        r0 = pl.multiple_of(c * C, C)
        for pr in range(2):
            qp = q_ref[pl.ds(r0, C), pr * LANES:(pr + 1) * LANES]
            kp = k_ref[pl.ds(r0, C), pr * LANES:(pr + 1) * LANES].astype(f32)
            kblk = jnp.concatenate([jnp.where(lo_mask, kp, 0.0), jnp.where(lo_mask, 0.0, kp)],
                                   axis=0).astype(bf16)
            a2 = _dot_t(qp, kblk)
            for hh, qd in enumerate(_dup_heads(qp.astype(f32), lo_mask)):
                h = 2 * pr + hh
                a = (a2[:, hh * C:(hh + 1) * C] * mask_ref[h]).astype(bf16)
                vh = v_ref[pl.ds(r0, C), h * DV_R:(h + 1) * DV_R]
                qd = (qd * qdec_ref[:, h * LANES:(h + 1) * LANES]).astype(bf16)
                o = _dot(a, vh) + _dot(qd, st_ref[c, h].astype(bf16))
                mu = jnp.mean(o, axis=-1, keepdims=True)
                d = o - mu
                var = jnp.mean(d * d, axis=-1, keepdims=True)
                yh = d * lax.rsqrt(var + EPS) * gnw_ref[:, h * DV_R:(h + 1) * DV_R]
                g = g_ref[pl.ds(r0, C), h * DV_R:(h + 1) * DV_R].astype(f32)
                y_ref[pl.ds(r0, C), h * DV_R:(h + 1) * DV_R] = (yh * _silu(g)).astype(bf16)
        return carry

    lax.fori_loop(0, n_seq * n_chunks, out_body, 0, unroll=RET_UNROLL)


def _retention(l, qr, kr, vr, gr, s0, lg, gnw, nb, seq, row_block0, n_seq):
    n_chunks = seq // CHUNK
    has_s0 = s0 is not None
    rows = n_seq * seq
    tok = lambda w: pl.BlockSpec((rows, w), lambda b: (row_block0 + b, 0))
    in_specs = [pl.BlockSpec(memory_space=pltpu.SMEM), tok(QK_R), tok(QK_R), tok(V_R), tok(V_R)]
    args = [lg[l], qr, kr, vr, gr]
    if has_s0:
        in_specs.append(pl.BlockSpec((None, NH_R, 2 * DK_R, DV_R), lambda b: (b, 0, 0, 0)))
        args.append(s0)
    in_specs += [pl.BlockSpec((1, V_R), lambda b: (0, 0))]
    args += [gnw[l:l + 1]]
    st_spec = pl.BlockSpec((n_seq, NH_R, DK_R, DV_R), lambda b: (b, 0, 0, 0))
    return pl.pallas_call(
        functools.partial(_retention_kernel, has_s0, n_seq, n_chunks),
        grid=(nb // n_seq,),
        in_specs=in_specs,
        out_specs=[pl.BlockSpec((rows, V_R), lambda b: (b, 0)), st_spec, st_spec],
        out_shape=[jax.ShapeDtypeStruct((nb * seq, V_R), bf16),
                   jax.ShapeDtypeStruct((nb, NH_R, DK_R, DV_R), f32),
                   jax.ShapeDtypeStruct((nb, NH_R, DK_R, DV_R), f32)],
        scratch_shapes=[pltpu.VMEM((n_seq * n_chunks, NH_R, 2 * DK_R, DV_R), f32),
                        pltpu.VMEM((n_seq * n_chunks, NH_R, 2 * DK_R, DV_R), f32),
                        pltpu.VMEM((NH_R, CHUNK, CHUNK), f32), pltpu.VMEM((CHUNK, NH_R * LANES), f32),
                        pltpu.VMEM((CHUNK, NH_R * LANES), f32), pltpu.VMEM((NH_R, 2 * DK_R, DV_R), f32)],
        compiler_params=pltpu.CompilerParams(
            dimension_semantics=("parallel",), vmem_limit_bytes=V7X_VMEM_LIMIT),
        name="retention_lat" if has_s0 else "retention_ctx",
    )(*args)


def _post_kernel(l, xc_ref, xl_ref, oac_ref, oal_ref, yrc_ref, yrl_ref, ga_ref, gb_ref, wpa_hbm, wpb_hbm,
                 wo_hbm, mod_ref, n2_ref, wr_ref, br_ref, tri_ref, x1_ref, h2_ref, route_ref, cnt_ref,
                 carry_ref, wpa_ref, wpb_ref, wo_ref, stage_ref, sems):
    i = pl.program_id(0)
    tm = xc_ref.shape[0]
    n_ctx = T_CTX // tm

    @pl.when(i == 0)
    def _():
        carry_ref[...] = jnp.zeros_like(carry_ref)
        _load_cast(wpa_hbm.at[l], wpa_ref, stage_ref, sems)
        _load_cast(wpb_hbm.at[l], wpb_ref, stage_ref, sems)
        _load_cast(wo_hbm.at[l], wo_ref, stage_ref, sems)

    for r0 in range(0, tm, POST_SUB):
        _post_subtile(i < n_ctx, slice(r0, r0 + POST_SUB), xc_ref, xl_ref, oac_ref, oal_ref, yrc_ref,
                      yrl_ref, ga_ref, gb_ref, mod_ref, n2_ref, wr_ref, br_ref, tri_ref, x1_ref, h2_ref,
                      route_ref, carry_ref, wpa_ref, wpb_ref, wo_ref)
    cnt_ref[...] = carry_ref[...]


def _post_subtile(is_ctx, rows, xc_ref, xl_ref, oac_ref, oal_ref, yrc_ref, yrl_ref, ga_ref, gb_ref, mod_ref,
                  n2_ref, wr_ref, br_ref, tri_ref, x1_ref, h2_ref, route_ref, carry_ref, wpa_ref, wpb_ref,
                  wo_ref):
    tm = rows.stop - rows.start
    pick = lambda a_ref, b_ref: jnp.where(is_ctx, a_ref[rows, :], b_ref[rows, :])
    ga = _sigmoid(ga_ref[rows, :].astype(f32))
    gb = _sigmoid(gb_ref[rows, :].astype(f32))
    merged = (ga * _dot(pick(oac_ref, oal_ref), wpa_ref[...])
              + gb * _dot(pick(yrc_ref, yrl_ref), wpb_ref[...]))
    mix = _dot(merged.astype(bf16), wo_ref[...])
    x1 = pick(xc_ref, xl_ref) + mod_ref[2:3, :] * mix
    x1_ref[rows, :] = x1
    y = x1 * lax.rsqrt(jnp.mean(x1 * x1, axis=-1, keepdims=True) + EPS) * n2_ref[...]
    h2 = y * (1.0 + mod_ref[4:5, :]) + mod_ref[3:4, :]
    h2_ref[rows, 0:D] = h2

    h_hi, h_lo = _split(h2)
    both = _dot(h_hi, wr_ref[...])
    logits = both[:, 0:LANES] + (both[:, LANES:2 * LANES] + _dot(h_lo, wr_ref[:, 0:LANES]))
    lt = logits.T[0:N_EXP, :]
    scores = _sigmoid(lt)
    sel = scores + br_ref[:, 0:1]
    row = lax.broadcasted_iota(jnp.int32, (N_EXP, tm), 0)

    best = None
    bg = None
    for g in range(N_GROUPS):
        a, b, c, d = (sel[EXP_PER_GROUP * g + k:EXP_PER_GROUP * g + k + 1, :] for k in range(4))
        p, q = jnp.maximum(a, b), jnp.minimum(a, b)
        r, s = jnp.maximum(c, d), jnp.minimum(c, d)
        gs = jnp.maximum(p, r) + jnp.maximum(jnp.minimum(p, r), jnp.maximum(q, s))
        if g == 0:
            best, bg = gs, jnp.zeros((1, tm), jnp.int32)
        else:
            upd = gs > best
            bg = jnp.where(upd, g, bg)
            best = jnp.where(upd, gs, best)
    masked = jnp.where(jnp.right_shift(row, 2) == bg, sel, NEG_INF)
    m1 = jnp.max(masked, axis=0, keepdims=True)
    i1 = jnp.min(jnp.where(masked == m1, row, N_EXP), axis=0, keepdims=True)
    masked2 = jnp.where(row == i1, NEG_INF, masked)
    m2 = jnp.max(masked2, axis=0, keepdims=True)
    i2 = jnp.min(jnp.where(masked2 == m2, row, N_EXP), axis=0, keepdims=True)
    oh1 = row == i1
    oh2 = row == i2
    s1 = jnp.sum(jnp.where(oh1, scores, 0.0), axis=0, keepdims=True)
    s2 = jnp.sum(jnp.where(oh2, scores, 0.0), axis=0, keepdims=True)
    den = s1 + s2

    comb = jnp.where(oh1, s1 / den, 0.0) + jnp.where(oh2, s2 / den, 0.0)
    comb_t = jnp.concatenate([comb, jnp.zeros((LANES - N_EXP, tm), f32)], axis=0).T
    h2_ref[rows, D:D + LANES] = comb_t

    e_lo = jnp.minimum(i1, i2) - EXP_PER_GROUP * bg
    e_hi = jnp.maximum(i1, i2) - EXP_PER_GROUP * bg
    pair = jnp.right_shift(e_lo * (7 - e_lo), 1) + (e_hi - e_lo - 1)
    bucket = bg * PAIRS + pair
    rowb = lax.broadcasted_iota(jnp.int32, (BUCKET_ROWS, tm), 0)
    ohb = rowb == bucket
    ohb_f = jnp.where(ohb, 1.0, 0.0)
    tot = carry_ref[:, 0:1] + _dot(ohb_f.astype(bf16), tri_ref[...])
    rank = jnp.sum(jnp.where(ohb, tot, 0.0), axis=0, keepdims=True)
    carry_ref[...] = carry_ref[...] + jnp.sum(ohb_f, axis=1, keepdims=True)

    route_ref[0:1, rows] = bucket.astype(f32)
    route_ref[1:2, rows] = rank
    route_ref[2:8, rows] = jnp.zeros((6, tm), f32)


def _post(l, x_ctx, x_lat, oa_ctx, oa_lat, yr_ctx, yr_lat, ga, gb, wpa, wpb, wo, mod, norm2_w, wr_pad,
          br_col, tri):
    tm = TM_POST
    row = lambda w: pl.BlockSpec((tm, w), lambda i: (i, 0))
    const = lambda a: pl.BlockSpec(a.shape, lambda i: (0,) * a.ndim)
    hbm = pl.BlockSpec(memory_space=pl.ANY)
    return pl.pallas_call(
        functools.partial(_post_kernel, l),
        grid=(T // tm,),
        in_specs=(_two_stream_specs(tm, D) + _two_stream_specs(tm, Q_A) + _two_stream_specs(tm, V_R)
                  + [row(D), row(D), hbm, hbm, hbm,
                     pl.BlockSpec((None, None, 6, D), lambda i: (l, _cond_of_tile(i, tm), 0, 0)),
                     pl.BlockSpec((1, D), lambda i: (0, 0)), const(wr_pad), const(br_col), const(tri)]),
        out_specs=[row(D), row(MOE_W), pl.BlockSpec((8, tm), lambda i: (0, i)),
                   pl.BlockSpec((BUCKET_ROWS, LANES), lambda i: (0, 0))],
        out_shape=[jax.ShapeDtypeStruct((T, D), f32), jax.ShapeDtypeStruct((T, MOE_W), f32),
                   jax.ShapeDtypeStruct((8, T), f32), jax.ShapeDtypeStruct((BUCKET_ROWS, LANES), f32)],
        scratch_shapes=[pltpu.VMEM((BUCKET_ROWS, LANES), f32), pltpu.VMEM((Q_A, D), bf16),
                        pltpu.VMEM((V_R, D), bf16), pltpu.VMEM((D, D), bf16),
                        pltpu.VMEM((2, D, W_CHUNK), f32), pltpu.SemaphoreType.DMA((2,))],
        compiler_params=pltpu.CompilerParams(
            dimension_semantics=("arbitrary",), vmem_limit_bytes=V7X_VMEM_LIMIT),
        name="post_router",
    )(x_ctx, x_lat, oa_ctx, oa_lat, yr_ctx, yr_lat, ga, gb, wpa, wpb, wo, mod, norm2_w[l:l + 1],
      wr_pad, br_col, tri)


def _row_copy(src_ref, src_row, dst_ref, dst_row, sem):
    return pltpu.make_async_copy(src_ref.at[pl.ds(src_row, 1)], dst_ref.at[pl.ds(dst_row, 1)], sem)


def _dispatch_kernel(zf_ref, pos_ref, h_ref, xs_ref, zero_ref, sem):
    tm = h_ref.shape[0]

    @pl.when(pl.program_id(0) == 0)
    def _():
        zero_ref[...] = jnp.zeros_like(zero_ref)

        def tile_copy(t):
            return pltpu.make_async_copy(zero_ref, xs_ref.at[pl.ds(t * MOE_TR, MOE_TR)], sem)

        def start(t, carry):
            @pl.when(zf_ref[t] != 0)
            def _():
                tile_copy(t).start()
            return carry

        def wait(t, carry):
            @pl.when(zf_ref[t] != 0)
            def _():
                tile_copy(t).wait()
            return carry

        lax.fori_loop(0, MOE_TILES, start, 0)
        lax.fori_loop(0, MOE_TILES, wait, 0)

    for r in range(tm):
        _row_copy(h_ref, r, xs_ref, pos_ref[0, 0, r], sem).start(priority=r % 2)
    pltpu.make_async_copy(h_ref, xs_ref.at[pl.ds(0, tm)], sem).wait()


def _dispatch(zero_flags, pos_blocks, h2):
    tm = TM_DISP
    return pl.pallas_call(
        _dispatch_kernel,
        grid_spec=pltpu.PrefetchScalarGridSpec(
            num_scalar_prefetch=1,
            grid=(T // tm,),
            in_specs=[pl.BlockSpec((1, 1, tm), lambda i, zf: (i, 0, 0), memory_space=pltpu.SMEM),
                      pl.BlockSpec((tm, MOE_W), lambda i, zf: (i, 0))],
            out_specs=pl.BlockSpec(memory_space=pl.ANY),
            scratch_shapes=[pltpu.VMEM((MOE_TR, MOE_W), f32), pltpu.SemaphoreType.DMA(())],
        ),
        out_shape=jax.ShapeDtypeStruct((MOE_ROWS, MOE_W), f32),
        compiler_params=pltpu.CompilerParams(
            dimension_semantics=("arbitrary",), vmem_limit_bytes=V7X_VMEM_LIMIT),
        name="moe_dispatch",
    )(zero_flags, pos_blocks, h2)


def _experts_kernel(l, tg_ref, ta_ref, tb_ref, first_ref, nxt_ref, nt_ref, xs_ref, wg_hbm, wu_hbm, wd_hbm,
                    y_ref, wg_ref, wu_ref, wd_ref, sg_ref, su_ref, sd_ref, sems):
    i = pl.program_id(0)
    live = i < nt_ref[0]
    g = tg_ref[i]

    def fetch(grp):
        return [pltpu.make_async_copy(w.at[l, pl.ds(grp * EXP_PER_GROUP, EXP_PER_GROUP)], s, sems.at[k])
                for k, (w, s) in enumerate(((wg_hbm, sg_ref), (wu_hbm, su_ref), (wd_hbm, sd_ref)))]

    @pl.when(i == 0)
    def _():
        for c in fetch(tg_ref[0]):
            c.start()

    @pl.when(live & (first_ref[i] != 0))
    def _():
        for c in fetch(g):
            c.wait()
        for j in range(EXP_PER_GROUP):
            wg_ref[j] = sg_ref[j].astype(bf16)
            wu_ref[j] = su_ref[j].astype(bf16)
            wd_ref[j] = sd_ref[j].astype(bf16)

        @pl.when(nxt_ref[i] >= 0)
        def _():
            for c in fetch(nxt_ref[i]):
                c.start(priority=1)

    @pl.when(live)
    def _():
        x = xs_ref[:, 0:D].astype(bf16)
        comb = xs_ref[:, D:D + LANES]
        lane = lax.broadcasted_iota(jnp.int32, comb.shape, 1)
        acc = None
        for j in (ta_ref[i], tb_ref[i]):
            w_j = jnp.sum(jnp.where(lane == g * EXP_PER_GROUP + j, comb, 0.0), axis=1, keepdims=True)
            act = _silu(_dot(x, wg_ref[j])) * _dot(x, wu_ref[j]) * w_j
            part = _dot(act.astype(bf16), wd_ref[j])
            acc = part if acc is None else acc + part
        y_ref[...] = acc

    @pl.when(i >= nt_ref[0])
    def _():
        y_ref[...] = jnp.zeros_like(y_ref)


def _experts(l, plan, xs, wg, wu, wd):
    def tile(i, *prefetch):
        return jnp.minimum(i, prefetch[-1][0] - 1)

    hbm = pl.BlockSpec(memory_space=pl.ANY)
    return pl.pallas_call(
        functools.partial(_experts_kernel, l),
        grid_spec=pltpu.PrefetchScalarGridSpec(
            num_scalar_prefetch=6,
            grid=(MOE_TILES,),
            in_specs=[pl.BlockSpec((MOE_TR, MOE_W), lambda i, *p: (tile(i, *p), 0)), hbm, hbm, hbm],
            out_specs=pl.BlockSpec((MOE_TR, D), lambda i, *p: (i, 0)),
            scratch_shapes=[pltpu.VMEM((EXP_PER_GROUP, D, D_EXP), bf16),
                            pltpu.VMEM((EXP_PER_GROUP, D, D_EXP), bf16),
                            pltpu.VMEM((EXP_PER_GROUP, D_EXP, D), bf16),
                            pltpu.VMEM((EXP_PER_GROUP, D, D_EXP), f32),
                            pltpu.VMEM((EXP_PER_GROUP, D, D_EXP), f32),
                            pltpu.VMEM((EXP_PER_GROUP, D_EXP, D), f32), pltpu.SemaphoreType.DMA((3,))],
        ),
        out_shape=jax.ShapeDtypeStruct((MOE_ROWS, D), f32),
        compiler_params=pltpu.CompilerParams(
            dimension_semantics=("arbitrary",), vmem_limit_bytes=V7X_VMEM_LIMIT),
        name="moe_experts",
    )(*plan, xs, wg, wu, wd)


def _combine_kernel(pos_ref, posn_ref, x1_ref, mod_ref, y_ref, oc_ref, ol_ref, ya_ref, yb_ref, sems):
    tm = x1_ref.shape[0]
    i = pl.program_id(0)
    n_ctx = T_CTX // tm

    def pull(p_ref, buf, sem):
        for r in range(tm):
            _row_copy(y_ref, p_ref[0, 0, r], buf, r, sem).start(priority=r % 2)

    @pl.when(i == 0)
    def _():
        pull(pos_ref, ya_ref, sems.at[0])

    def step(cur, nxt, sem_cur, sem_nxt):
        @pl.when(i + 1 < pl.num_programs(0))
        def _():
            pull(posn_ref, nxt, sem_nxt)

        pltpu.make_async_copy(y_ref.at[pl.ds(0, tm)], cur, sem_cur).wait()
        out = x1_ref[...] + mod_ref[5:6, :] * cur[...]

        @pl.when(i < n_ctx)
        def _():
            oc_ref[...] = out

        @pl.when(i >= n_ctx)
        def _():
            ol_ref[...] = out

    @pl.when(i % 2 == 0)
    def _():
        step(ya_ref, yb_ref, sems.at[0], sems.at[1])

    @pl.when(i % 2 == 1)
    def _():
        step(yb_ref, ya_ref, sems.at[1], sems.at[0])


def _combine(l, pos_blocks, x1, mod, y):
    tm = TM_DISP
    n = T // tm
    return pl.pallas_call(
        _combine_kernel,
        grid=(n,),
        in_specs=[pl.BlockSpec((1, 1, tm), lambda i: (i, 0, 0), memory_space=pltpu.SMEM),
                  pl.BlockSpec((1, 1, tm), lambda i: (jnp.minimum(i + 1, n - 1), 0, 0),
                               memory_space=pltpu.SMEM),
                  pl.BlockSpec((tm, D), lambda i: (i, 0)),
                  pl.BlockSpec((None, None, 6, D), lambda i: (l, _cond_of_tile(i, tm), 0, 0)),
                  pl.BlockSpec(memory_space=pl.ANY)],
        out_specs=_two_stream_specs(tm, D),
        out_shape=[jax.ShapeDtypeStruct((T_CTX, D), f32), jax.ShapeDtypeStruct((T_LAT, D), f32)],
        scratch_shapes=[pltpu.VMEM((tm, D), f32), pltpu.VMEM((tm, D), f32),
                        pltpu.SemaphoreType.DMA((2,))],
        compiler_params=pltpu.CompilerParams(
            dimension_semantics=("arbitrary",), vmem_limit_bytes=V7X_VMEM_LIMIT),
        name="moe_combine",
    )(pos_blocks, pos_blocks, x1, mod, y)


def _moe_plan(route, counts):
    cnt = counts[0:N_BUCKETS, 0].astype(jnp.int32)
    tiles = (cnt + MOE_TR - 1) // MOE_TR
    tile_end = jnp.cumsum(tiles)
    tile_start = tile_end - tiles
    off = (tile_start * MOE_TR).astype(f32)
    buckets = jnp.arange(N_BUCKETS, dtype=f32)[:, None]
    pos = jnp.sum(jnp.where(route[0][None, :] == buckets, off[:, None], 0.0), axis=0) + route[1]
    pos_blocks = pos.astype(jnp.int32).reshape(T // TM_DISP, 1, TM_DISP)
    tid = jnp.arange(MOE_TILES, dtype=jnp.int32)
    tb = jnp.minimum(jnp.sum(tid[:, None] >= tile_end[None, :], axis=1), N_BUCKETS - 1).astype(jnp.int32)
    n_tiles = tile_end[-1:].astype(jnp.int32)
    is_last = jnp.any((tid[:, None] == tile_end[None, :] - 1) & (tiles > 0)[None, :], axis=1)
    zero_flags = ((tid >= n_tiles[0]) | is_last).astype(jnp.int32)
    tg = tb // PAIRS
    pair = tb % PAIRS
    pair_lo = np.array([a for a in range(EXP_PER_GROUP) for _ in range(a + 1, EXP_PER_GROUP)])
    pair_hi = np.array([b for a in range(EXP_PER_GROUP) for b in range(a + 1, EXP_PER_GROUP)])
    of_pair = pair[:, None] == jnp.arange(PAIRS)[None, :]
    ta = jnp.sum(jnp.where(of_pair, pair_lo[None, :], 0), axis=1).astype(jnp.int32)
    tb2 = jnp.sum(jnp.where(of_pair, pair_hi[None, :], 0), axis=1).astype(jnp.int32)
    g_tiles = tiles.reshape(N_GROUPS, PAIRS).sum(axis=1)
    g_end = jnp.cumsum(g_tiles)
    g_start = g_end - g_tiles
    used = g_tiles > 0
    gid = jnp.arange(N_GROUPS, dtype=jnp.int32)
    first = jnp.any((tid[:, None] == g_start[None, :]) & used[None, :], axis=1).astype(jnp.int32)
    later = (gid[None, :] > gid[:, None]) & used[None, :]
    nxt_g = jnp.min(jnp.where(later, gid[None, :], N_GROUPS), axis=1)
    nxt_g = jnp.where(nxt_g == N_GROUPS, -1, nxt_g)
    nxt = jnp.sum(jnp.where(tg[:, None] == gid[None, :], nxt_g[None, :], 0), axis=1).astype(jnp.int32)
    return pos_blocks, zero_flags, (tg.astype(jnp.int32), ta, tb2, first, nxt, n_tiles)


def _rope_tables():
    pos = np.arange(DEC_SEQ)
    half = HD_A // 4
    freqs = ROPE_BASE ** (-np.arange(half, dtype=np.float64) / half)
    ang_r = (pos // GRID_W)[:, None] * freqs[None, :]
    ang_c = (pos % GRID_W)[:, None] * freqs[None, :]
    ang = np.concatenate([ang_r, ang_r, ang_c, ang_c], axis=1)
    sign = np.concatenate([-np.ones(half), np.ones(half)] * 2)[None, :]
    cos = np.tile(np.cos(ang), (1, 2))
    sin = np.tile(np.sin(ang) * sign, (1, 2))
    ident_c = np.ones((TM_PROJ, LANES))
    ident_s = np.zeros((TM_PROJ, LANES))
    return (jnp.asarray(np.concatenate([cos, ident_c]), f32),
            jnp.asarray(np.concatenate([sin, ident_s]), f32))


def kernel(x_prompt, x_sample, c, cache_k, cache_v, state_ret_fwd, state_ret_bwd, c_ctx,
           norm1_w, norm2_w, w_ada, b_ada, w_in, q_norm_w, k_norm_w, attn_sink,
           ret_decay_fwd, ret_decay_bwd, ret_gn_w, w_pa, w_pb, w_o, w_router, b_router,
           w_exp_gate, w_exp_up, w_exp_down):
    x_ctx, x_lat = x_prompt.reshape(T_CTX, D), x_sample.reshape(T_LAT, D)
    cond8 = jnp.zeros((N_COND, D), f32).at[0].set(c_ctx).at[1:1 + DEC_BATCH].set(c)
    mod = _adaln(cond8, w_ada, b_ada).reshape(DEPTH, N_COND, 6, D)

    cos_t, sin_t = _rope_tables()
    blk = np.arange(Q_A) // HD_A
    ones_blk = jnp.asarray((blk[:, None] == blk[None, :]) / HD_A, bf16)
    tri = jnp.asarray(np.triu(np.ones((POST_SUB, POST_SUB)), 1), bf16)
    qnw = jnp.tile(q_norm_w, (1, NH_A))
    knw = jnp.tile(k_norm_w, (1, NKV_A))
    wr_pad = jnp.concatenate(_split(jnp.pad(w_router, ((0, 0), (0, LANES - N_EXP)))), axis=1)
    br_col = jnp.broadcast_to(b_router[:, None], (N_EXP, LANES))
    ck = cache_k.reshape(DEC_BATCH, DEPTH, PAST, KV_A)
    cv = cache_v.reshape(DEC_BATCH, DEPTH, PAST, KV_A)
    s0 = jnp.concatenate([state_ret_fwd, state_ret_bwd], axis=3)
    lg = jnp.log1p(-jnp.exp(jnp.concatenate([ret_decay_fwd, ret_decay_bwd], axis=1).astype(f32)))

    new_k, new_v, new_sf, new_sb = [], [], [], []
    for l in range(DEPTH):
        qa, ka, va, qr, kr, vr, gr, ga, gb = _inproj(l, x_ctx, x_lat, norm1_w, mod, w_in, cos_t, sin_t,
                                                     ones_blk, qnw, knw)
        oa_ctx = _ctx_attn(l, qa, ka, va, attn_sink)
        oa_lat = _lat_attn(l, qa, ka, va, ck, cv, cos_t, sin_t, attn_sink)
        yr_ctx, sf, sb = _retention(l, qr, kr, vr, gr, None, lg, ret_gn_w, BATCH, SEQ, 0,
                                    CTX_SEQS_PER_STEP)
        yr_lat, _, _ = _retention(l, qr, kr, vr, gr, s0[:, l], lg, ret_gn_w, DEC_BATCH, DEC_SEQ,
                                  T_CTX // DEC_SEQ, 1)
        x1, h2, route, counts = _post(l, x_ctx, x_lat, oa_ctx, oa_lat, yr_ctx, yr_lat, ga, gb,
                                      w_pa, w_pb, w_o, mod, norm2_w, wr_pad, br_col, tri)
        pos_blocks, zero_flags, plan = _moe_plan(route, counts)
        xs = _dispatch(zero_flags, pos_blocks, h2)
        y = _experts(l, plan, xs, w_exp_gate, w_exp_up, w_exp_down)
        x_ctx, x_lat = _combine(l, pos_blocks, x1, mod, y)

        new_k.append(ka[:T_CTX].reshape(BATCH, SEQ, NKV_A, HD_A))
        new_v.append(va[:T_CTX].reshape(BATCH, SEQ, NKV_A, HD_A))
        new_sf.append(sf)
        new_sb.append(sb)

    return (x_ctx.reshape(BATCH, SEQ, D), x_lat.reshape(DEC_BATCH, DEC_SEQ, D),
            jnp.stack(new_k, axis=1), jnp.stack(new_v, axis=1),
            jnp.stack(new_sf, axis=1), jnp.stack(new_sb, axis=1))
```

```python
import functools

import numpy as np
import jax
import jax.numpy as jnp
from jax import lax
from jax.experimental import pallas as pl
from jax.experimental.pallas import tpu as pltpu

D = 1024
BATCH, SEQ = 16, 256
DEC_BATCH, DEC_SEQ = 2, 2048
DEPTH = 2
PAST = 512
GRID_W = 64
NH_A, NKV_A, HD_A = 8, 2, 64
WINDOW = 128
NH_R, DK_R, DV_R = 4, 64, 128
CHUNK = 128
N_EXP, N_GROUPS, EXP_PER_GROUP = 16, 4, 4
D_EXP = 512
ROPE_BASE = 10000.0
EPS = 1e-6
NEG_INF = -1e30

Q_A = NH_A * HD_A
KV_A = NKV_A * HD_A
QK_R = NH_R * DK_R
V_R = NH_R * DV_R
C_QA = (0, Q_A)
C_KA = (C_QA[1], C_QA[1] + KV_A)
C_VA = (C_KA[1], C_KA[1] + KV_A)
C_QR = (C_VA[1], C_VA[1] + QK_R)
C_KR = (C_QR[1], C_QR[1] + QK_R)
C_VR = (C_KR[1], C_KR[1] + V_R)
C_GR = (C_VR[1], C_VR[1] + V_R)
C_GA = (C_GR[1], C_GR[1] + D)
C_GB = (C_GA[1], C_GA[1] + D)
D_IN = C_GB[1]

T_CTX = BATCH * SEQ
T_LAT = DEC_BATCH * DEC_SEQ
T = T_CTX + T_LAT
N_COND = 8

LANES = 128
V7X_VMEM_LIMIT = 56 * 1024 * 1024

TM_PROJ = 512
TM_POST = 512
POST_SUB = 256
W_CHUNK = 256
TM_DISP = 512
MOE_TR = 256
PAIRS = EXP_PER_GROUP * (EXP_PER_GROUP - 1) // 2
N_BUCKETS = N_GROUPS * PAIRS
BUCKET_ROWS = 32
MOE_TILES = T // MOE_TR + N_BUCKETS
MOE_ROWS = MOE_TILES * MOE_TR
MOE_W = D + LANES
ATT_QB = 256
RET_UNROLL = 4
CTX_SEQS_PER_STEP = 4

f32 = jnp.float32
bf16 = jnp.bfloat16


def _dot(a, b):
    return jnp.dot(a, b, preferred_element_type=f32)


def _dot_t(a, b):
    return lax.dot_general(a, b, (((1,), (1,)), ((), ())), preferred_element_type=f32)


def _split(x):
    hi = x.astype(bf16)
    lo = (x - hi.astype(f32)).astype(bf16)
    return hi, lo


def _dot3(a, b):
    ah, al = _split(a)
    bh, bl = _split(b)
    return _dot(ah, bh) + (_dot(ah, bl) + _dot(al, bh))


def _sigmoid(x):
    return 1.0 / (1.0 + jnp.exp(-x))


def _silu(x):
    return x * _sigmoid(x)


def _cond_of_tile(i, tm):
    n_ctx = T_CTX // tm
    per_b = DEC_SEQ // tm
    return jnp.where(i < n_ctx, 0, 1 + jnp.maximum(i - n_ctx, 0) // per_b)


def _adaln_kernel(cond_ref, w_ref, b_ref, o_ref):
    a = _silu(cond_ref[...])
    o_ref[0] = _dot3(a, w_ref[0]) + b_ref[0]


def _adaln(cond8, w_ada, b_ada):
    tn = 2048
    return pl.pallas_call(
        _adaln_kernel,
        grid=(DEPTH, 6 * D // tn),
        in_specs=[
            pl.BlockSpec((N_COND, D), lambda l, j: (0, 0)),
            pl.BlockSpec((1, D, tn), lambda l, j: (l, 0, j)),
            pl.BlockSpec((1, 1, tn), lambda l, j: (l, 0, j)),
        ],
        out_specs=pl.BlockSpec((1, N_COND, tn), lambda l, j: (l, 0, j)),
        out_shape=jax.ShapeDtypeStruct((DEPTH, N_COND, 6 * D), f32),
        compiler_params=pltpu.CompilerParams(
            dimension_semantics=("parallel", "parallel"), vmem_limit_bytes=V7X_VMEM_LIMIT),
        name="adaln",
    )(cond8, w_ada, b_ada.reshape(DEPTH, 1, 6 * D))


def _rope(x, cos, sin_signed, first_half):
    fwd = pltpu.roll(x, 16, 1)
    bwd = pltpu.roll(x, LANES - 16, 1)
    partner = jnp.where(first_half, bwd, fwd)
    return x * cos + partner * sin_signed


def _head_rms(x, ones_blk, w):
    n = x.shape[1]
    sq_hi, sq_lo = _split(x * x)
    blk = ones_blk[0:n, 0:n]
    mean = _dot(sq_hi, blk) + _dot(sq_lo, blk)
    return x * lax.rsqrt(mean + EPS) * w


def _load_cast(w_hbm, dst_ref, stage_ref, sems):
    k, n = w_hbm.shape

    def chunk_copy(c):
        return pltpu.make_async_copy(w_hbm.at[:, pl.ds(c * W_CHUNK, W_CHUNK)],
                                     stage_ref.at[c % 2, pl.ds(0, k)], sems.at[c % 2])

    n_chunks = n // W_CHUNK
    chunk_copy(0).start()
    for c in range(n_chunks):
        if c + 1 < n_chunks:
            chunk_copy(c + 1).start()
        chunk_copy(c).wait()
        dst_ref[:, c * W_CHUNK:(c + 1) * W_CHUNK] = stage_ref[c % 2, 0:k, :].astype(bf16)


def _tile_of_two(i, n_first, a_ref, b_ref):
    return jnp.where(i < n_first, a_ref[...], b_ref[...])


def _inproj_kernel(l, xc_ref, xl_ref, n1_ref, mod_ref, w_hbm, cos_ref, sin_ref, ones_ref, qnw_ref,
                   knw_ref, qa_ref, ka_ref, va_ref, qr_ref, kr_ref, vr_ref, gr_ref, ga_ref, gb_ref,
                   w_ref, stage_ref, sems):
    i = pl.program_id(0)

    @pl.when(i == 0)
    def _():
        _load_cast(w_hbm.at[l], w_ref, stage_ref, sems)

    x = _tile_of_two(i, T_CTX // xc_ref.shape[0], xc_ref, xl_ref)
    y = x * lax.rsqrt(jnp.mean(x * x, axis=-1, keepdims=True) + EPS) * n1_ref[...]
    h = (y * (1.0 + mod_ref[1:2, :]) + mod_ref[0:1, :]).astype(bf16)

    def proj(c):
        return _dot(h, w_ref[:, c[0]:c[1]])

    cos = cos_ref[...]
    sin = sin_ref[...]
    lane = lax.broadcasted_iota(jnp.int32, cos.shape, 1)
    first_half = (lane % 32) < 16

    def rope_all(v):
        parts = [_rope(v[:, j:j + LANES], cos, sin, first_half) for j in range(0, v.shape[1], LANES)]
        return parts[0] if len(parts) == 1 else jnp.concatenate(parts, axis=1)

    ones_blk = ones_ref[...]
    qa = _head_rms(proj(C_QA), ones_blk, qnw_ref[...])
    qa_ref[...] = (rope_all(qa) * (HD_A ** -0.5)).astype(bf16)
    ka_ref[...] = _head_rms(proj(C_KA), ones_blk, knw_ref[...])
    va_ref[...] = proj(C_VA)
    qr_ref[...] = rope_all(proj(C_QR)).astype(bf16)
    kr_ref[...] = (rope_all(proj(C_KR)) * (DK_R ** -0.5)).astype(bf16)
    vr_ref[...] = proj(C_VR).astype(bf16)
    gr_ref[...] = proj(C_GR).astype(bf16)
    ga_ref[...] = proj(C_GA).astype(bf16)
    gb_ref[...] = proj(C_GB).astype(bf16)


def _two_stream_specs(tm, width):
    n_ctx = T_CTX // tm
    return [pl.BlockSpec((tm, width), lambda i, *_: (jnp.minimum(i, n_ctx - 1), 0)),
            pl.BlockSpec((tm, width), lambda i, *_: (jnp.maximum(i - n_ctx, 0), 0))]


def _inproj(l, x_ctx, x_lat, norm1_w, mod, w_in, cos_t, sin_t, ones_blk, qnw, knw):
    tm = TM_PROJ
    n_ctx = T_CTX // tm
    per_b = DEC_SEQ // tm

    def tab_map(i):
        return (jnp.where(i < n_ctx, per_b, jnp.maximum(i - n_ctx, 0) % per_b), 0)

    row = lambda i: (i, 0)
    const = lambda i: (0, 0)
    widths = [(Q_A, bf16), (KV_A, f32), (KV_A, f32), (QK_R, bf16), (QK_R, bf16), (V_R, bf16),
              (V_R, bf16), (D, bf16), (D, bf16)]
    return pl.pallas_call(
        functools.partial(_inproj_kernel, l),
        grid=(T // tm,),
        in_specs=_two_stream_specs(tm, D) + [
            pl.BlockSpec((1, D), const),
            pl.BlockSpec((None, None, 6, D), lambda i: (l, _cond_of_tile(i, tm), 0, 0)),
            pl.BlockSpec(memory_space=pl.ANY),
            pl.BlockSpec((tm, LANES), tab_map),
            pl.BlockSpec((tm, LANES), tab_map),
            pl.BlockSpec((Q_A, Q_A), const),
            pl.BlockSpec((1, Q_A), const),
            pl.BlockSpec((1, KV_A), const),
        ],
        out_specs=[pl.BlockSpec((tm, w), row) for w, _ in widths],
        out_shape=[jax.ShapeDtypeStruct((T, w), dt) for w, dt in widths],
        scratch_shapes=[pltpu.VMEM((D, D_IN), bf16), pltpu.VMEM((2, D, W_CHUNK), f32),
                        pltpu.SemaphoreType.DMA((2,))],
        compiler_params=pltpu.CompilerParams(
            dimension_semantics=("arbitrary",), vmem_limit_bytes=V7X_VMEM_LIMIT),
        name="inproj",
    )(x_ctx, x_lat, norm1_w[l:l + 1], mod, w_in, cos_t, sin_t, ones_blk, qnw[l:l + 1], knw[l:l + 1])


def _head_blocks(t, kv, lo_mask):
    r = pltpu.roll(t, HD_A, 1)
    if kv == 0:
        a = jnp.where(lo_mask, t, 0.0)
        b = jnp.where(lo_mask, 0.0, r)
    else:
        a = jnp.where(lo_mask, r, 0.0)
        b = jnp.where(lo_mask, 0.0, t)
    return jnp.concatenate([a, b], axis=0).astype(bf16)


def _ctx_attn_kernel(sink_ref, q_ref, k_ref, v_ref, o_ref):
    n = SEQ
    lo_mask = lax.broadcasted_iota(jnp.int32, (n, LANES), 1) < HD_A
    for b in range(CTX_SEQS_PER_STEP):
        rows = slice(b * n, (b + 1) * n)
        k = k_ref[rows, :]
        v = v_ref[rows, :]
        for kv in range(NKV_A):
            kblk = _head_blocks(k, kv, lo_mask)
            vblk = _head_blocks(v, kv, lo_mask)
            for pr in range(2):
                pi = kv * 2 + pr
                s = _dot_t(q_ref[rows, pi * LANES:(pi + 1) * LANES], kblk)
                ps, invs = [], []
                for hh in range(2):
                    sk = sink_ref[2 * pi + hh]
                    sh = s[:, hh * n:(hh + 1) * n]
                    m = jnp.maximum(jnp.max(sh, axis=-1, keepdims=True), sk)
                    p = jnp.exp(sh - m)
                    invs.append(1.0 / (jnp.sum(p, axis=-1, keepdims=True) + jnp.exp(sk - m)))
                    ps.append(p.astype(bf16))
                o = _dot(jnp.concatenate(ps, axis=1), vblk) * jnp.where(lo_mask, invs[0], invs[1])
                o_ref[rows, pi * LANES:(pi + 1) * LANES] = o.astype(bf16)


def _ctx_attn(l, qa, ka, va, sink):
    blk = lambda w: pl.BlockSpec((CTX_SEQS_PER_STEP * SEQ, w), lambda b: (b, 0))
    return pl.pallas_call(
        _ctx_attn_kernel,
        grid=(BATCH // CTX_SEQS_PER_STEP,),
        in_specs=[pl.BlockSpec(memory_space=pltpu.SMEM), blk(Q_A), blk(KV_A), blk(KV_A)],
        out_specs=blk(Q_A),
        out_shape=jax.ShapeDtypeStruct((T_CTX, Q_A), bf16),
        compiler_params=pltpu.CompilerParams(
            dimension_semantics=("parallel",), vmem_limit_bytes=V7X_VMEM_LIMIT),
        name="ctx_attn",
    )(sink[l], qa, ka, va)


def _lat_attn_kernel(sink_ref, q_ref, k_ref, v_ref, kc_ref, vc_ref, cos_ref, sin_ref, o_ref):
    j = pl.program_id(1)
    qb = ATT_QB
    win = 2 * qb
    ws = pl.multiple_of(jnp.clip(j * qb - WINDOW, 0, DEC_SEQ - win), WINDOW)
    lo_mask = lax.broadcasted_iota(jnp.int32, (win, LANES), 1) < HD_A
    lane = lax.broadcasted_iota(jnp.int32, (win, LANES), 1)
    kw = _rope(k_ref[pl.ds(ws, win), :], cos_ref[pl.ds(ws, win), :], sin_ref[pl.ds(ws, win), :],
               (lane % 32) < 16)
    vw = v_ref[pl.ds(ws, win), :]
    kc = kc_ref[...]
    vc = vc_ref[...]
    qpos = j * qb + (lax.broadcasted_iota(jnp.int32, (2 * qb, win), 0) & (qb - 1))
    kpos = ws + lax.broadcasted_iota(jnp.int32, (2 * qb, win), 1)
    valid = jnp.abs(qpos - kpos) <= WINDOW
    out_lo = lax.broadcasted_iota(jnp.int32, (2 * qb, LANES), 1) < HD_A
    for kv in range(NKV_A):
        kc_blk = _head_blocks(kc, kv, lo_mask[:PAST])
        vc_blk = _head_blocks(vc, kv, lo_mask[:PAST])
        kw_blk = _head_blocks(kw, kv, lo_mask)
        vw_blk = _head_blocks(vw, kv, lo_mask)
        q2 = jnp.concatenate([q_ref[:, (2 * kv) * LANES:(2 * kv + 1) * LANES],
                              q_ref[:, (2 * kv + 1) * LANES:(2 * kv + 2) * LANES]], axis=0)
        s_c = _dot_t(q2, kc_blk)
        s_w = _dot_t(q2, kw_blk)
        pcs, pws, invs = [], [], []
        for hh in range(2):
            row = lax.broadcasted_iota(jnp.int32, (2 * qb, 1), 0)
            sk = jnp.where(row < qb, sink_ref[4 * kv + hh], sink_ref[4 * kv + 2 + hh])
            sc = s_c[:, hh * PAST:(hh + 1) * PAST]
            sw = jnp.where(valid, s_w[:, hh * win:(hh + 1) * win], NEG_INF)
            m = jnp.maximum(jnp.maximum(jnp.max(sc, axis=-1, keepdims=True),
                                        jnp.max(sw, axis=-1, keepdims=True)), sk)
            pc = jnp.exp(sc - m)
            pw = jnp.exp(sw - m)
            den = (jnp.sum(pc, axis=-1, keepdims=True) + jnp.sum(pw, axis=-1, keepdims=True)
                   + jnp.exp(sk - m))
            invs.append(1.0 / den)
            pcs.append(pc.astype(bf16))
            pws.append(pw.astype(bf16))
        o = _dot(jnp.concatenate(pcs, axis=1), vc_blk) + _dot(jnp.concatenate(pws, axis=1), vw_blk)
        o = o * jnp.where(out_lo, invs[0], invs[1])
        o_ref[:, (2 * kv) * LANES:(2 * kv + 1) * LANES] = o[:qb].astype(bf16)
        o_ref[:, (2 * kv + 1) * LANES:(2 * kv + 2) * LANES] = o[qb:].astype(bf16)


def _lat_attn(l, qa, ka, va, cache_k, cache_v, cos_l, sin_l, sink):
    qb = ATT_QB
    nq = DEC_SEQ // qb
    ctx_blocks = T_CTX // DEC_SEQ
    seq = lambda b, j: (ctx_blocks + b, 0)
    return pl.pallas_call(
        _lat_attn_kernel,
        grid=(DEC_BATCH, nq),
        in_specs=[
            pl.BlockSpec(memory_space=pltpu.SMEM),
            pl.BlockSpec((qb, Q_A), lambda b, j: (T_CTX // qb + b * nq + j, 0)),
            pl.BlockSpec((DEC_SEQ, KV_A), seq),
            pl.BlockSpec((DEC_SEQ, KV_A), seq),
            pl.BlockSpec((None, None, PAST, KV_A), lambda b, j: (b, l, 0, 0)),
            pl.BlockSpec((None, None, PAST, KV_A), lambda b, j: (b, l, 0, 0)),
            pl.BlockSpec((DEC_SEQ, LANES), lambda b, j: (0, 0)),
            pl.BlockSpec((DEC_SEQ, LANES), lambda b, j: (0, 0)),
        ],
        out_specs=pl.BlockSpec((qb, Q_A), lambda b, j: (b * nq + j, 0)),
        out_shape=jax.ShapeDtypeStruct((T_LAT, Q_A), bf16),
        compiler_params=pltpu.CompilerParams(
            dimension_semantics=("parallel", "parallel"), vmem_limit_bytes=V7X_VMEM_LIMIT),
        name="lat_attn",
    )(sink[l], qa, ka, va, cache_k, cache_v, cos_l, sin_l)


def _dup_heads(pair, lo_mask):
    r = pltpu.roll(pair, DK_R, 1)
    return jnp.where(lo_mask, pair, r), jnp.where(lo_mask, r, pair)


def _retention_kernel(has_s0, n_seq, n_chunks, *refs):
    if has_s0:
        (lg_ref, q_ref, k_ref, v_ref, g_ref, s0_ref, gnw_ref,
         y_ref, sf_ref, sb_ref, ds_ref, st_ref, mask_ref, qdec_ref, kdec_ref, cdec_ref) = refs
    else:
        (lg_ref, q_ref, k_ref, v_ref, g_ref, gnw_ref,
         y_ref, sf_ref, sb_ref, ds_ref, st_ref, mask_ref, qdec_ref, kdec_ref, cdec_ref) = refs
        s0_ref = None
    C = CHUNK
    lo_mask = lax.broadcasted_iota(jnp.int32, (C, LANES), 1) < DK_R

    ri = lax.broadcasted_iota(jnp.int32, (C, C), 0).astype(f32)
    diff = ri - lax.broadcasted_iota(jnp.int32, (C, C), 1).astype(f32)
    for h in range(NH_R):
        lg_f, lg_b = lg_ref[h], lg_ref[NH_R + h]
        mask_ref[h] = (jnp.where(diff >= 0, jnp.exp(jnp.maximum(diff, 0.0) * lg_f), 0.0)
                       + jnp.where(diff <= 0, jnp.exp(jnp.maximum(-diff, 0.0) * lg_b), 0.0))
        qdec_ref[:, h * LANES:(h + 1) * LANES] = jnp.exp(
            jnp.where(lo_mask, (ri + 1.0) * lg_f, (C - ri) * lg_b))
        kdec_ref[:, h * LANES:(h + 1) * LANES] = jnp.exp(
            jnp.where(lo_mask, (C - 1.0 - ri) * lg_f, ri * lg_b))
        cdec_ref[h] = jnp.exp(jnp.where(ri < DK_R, C * lg_f, C * lg_b))

    def inc_body(c, carry):
        r0 = pl.multiple_of(c * C, C)
        for pr in range(2):
            kp = k_ref[pl.ds(r0, C), pr * LANES:(pr + 1) * LANES].astype(f32)
            for hh, kd in enumerate(_dup_heads(kp, lo_mask)):
                h = 2 * pr + hh
                kd = (kd * kdec_ref[:, h * LANES:(h + 1) * LANES]).astype(bf16)
                vh = v_ref[pl.ds(r0, C), h * DV_R:(h + 1) * DV_R]
                ds_ref[c, h] = lax.dot_general(kd, vh, (((0,), (0,)), ((), ())),
                                               preferred_element_type=f32)
        return carry

    lax.fori_loop(0, n_seq * n_chunks, inc_body, 0, unroll=RET_UNROLL)

    for sq in range(n_seq):
        base = sq * n_chunks
        for h in range(NH_R):
            cf = cdec_ref[h, 0:DK_R, :]
            cb = cdec_ref[h, DK_R:2 * DK_R, :]
            if has_s0:
                init_f = s0_ref[h, 0:DK_R, :]
                init_b = s0_ref[h, DK_R:2 * DK_R, :]
            else:
                init_f = jnp.zeros((DK_R, DV_R), f32)
                init_b = init_f

            def fwd_body(i, s, h=h, cf=cf, base=base):
                c = base + i
                st_ref[c, h, 0:DK_R, :] = s
                return s * cf + ds_ref[c, h, 0:DK_R, :]

            def bwd_body(i, s, h=h, cb=cb, base=base):
                c = base + n_chunks - 1 - i
                st_ref[c, h, DK_R:2 * DK_R, :] = s
                return s * cb + ds_ref[c, h, DK_R:2 * DK_R, :]

            sf_ref[sq, h] = lax.fori_loop(0, n_chunks, fwd_body, init_f)
            sb_ref[sq, h] = lax.fori_loop(0, n_chunks, bwd_body, init_b)

    def out_body(c, carry):
        r0 = pl.multiple_of(c * C, C)
        for pr in range(2):
            qp = q_ref[pl.ds(r0, C), pr * LANES:(pr + 1) * LANES]
            kp = k_ref[pl.ds(r0, C), pr * LANES:(pr + 1) * LANES].astype(f32)
            kblk = jnp.concatenate([jnp.where(lo_mask, kp, 0.0), jnp.where(lo_mask, 0.0, kp)],
                                   axis=0).astype(bf16)
            a2 = _dot_t(qp, kblk)
            for hh, qd in enumerate(_dup_heads(qp.astype(f32), lo_mask)):
                h = 2 * pr + hh
                a = (a2[:, hh * C:(hh + 1) * C] * mask_ref[h]).astype(bf16)
                vh = v_ref[pl.ds(r0, C), h * DV_R:(h + 1) * DV_R]
                qd = (qd * qdec_ref[:, h * LANES:(h + 1) * LANES]).astype(bf16)
                o = _dot(a, vh) + _dot(qd, st_ref[c, h].astype(bf16))
                mu = jnp.mean(o, axis=-1, keepdims=True)
                d = o - mu
                var = jnp.mean(d * d, axis=-1, keepdims=True)
                yh = d * lax.rsqrt(var + EPS) * gnw_ref[:, h * DV_R:(h + 1) * DV_R]
                g = g_ref[pl.ds(r0, C), h * DV_R:(h + 1) * DV_R].astype(f32)
                y_ref[pl.ds(r0, C), h * DV_R:(h + 1) * DV_R] = (yh * _silu(g)).astype(bf16)
        return carry

    lax.fori_loop(0, n_seq * n_chunks, out_body, 0, unroll=RET_UNROLL)


def _retention(l, qr, kr, vr, gr, s0, lg, gnw, nb, seq, row_block0, n_seq):
    n_chunks = seq // CHUNK
    has_s0 = s0 is not None
    rows = n_seq * seq
    tok = lambda w: pl.BlockSpec((rows, w), lambda b: (row_block0 + b, 0))
    in_specs = [pl.BlockSpec(memory_space=pltpu.SMEM), tok(QK_R), tok(QK_R), tok(V_R), tok(V_R)]
    args = [lg[l], qr, kr, vr, gr]
    if has_s0:
        in_specs.append(pl.BlockSpec((None, NH_R, 2 * DK_R, DV_R), lambda b: (b, 0, 0, 0)))
        args.append(s0)
    in_specs += [pl.BlockSpec((1, V_R), lambda b: (0, 0))]
    args += [gnw[l:l + 1]]
    st_spec = pl.BlockSpec((n_seq, NH_R, DK_R, DV_R), lambda b: (b, 0, 0, 0))
    return pl.pallas_call(
        functools.partial(_retention_kernel, has_s0, n_seq, n_chunks),
        grid=(nb // n_seq,),
        in_specs=in_specs,
        out_specs=[pl.BlockSpec((rows, V_R), lambda b: (b, 0)), st_spec, st_spec],
        out_shape=[jax.ShapeDtypeStruct((nb * seq, V_R), bf16),
                   jax.ShapeDtypeStruct((nb, NH_R, DK_R, DV_R), f32),
                   jax.ShapeDtypeStruct((nb, NH_R, DK_R, DV_R), f32)],
        scratch_shapes=[pltpu.VMEM((n_seq * n_chunks, NH_R, 2 * DK_R, DV_R), f32),
                        pltpu.VMEM((n_seq * n_chunks, NH_R, 2 * DK_R, DV_R), f32),
                        pltpu.VMEM((NH_R, CHUNK, CHUNK), f32), pltpu.VMEM((CHUNK, NH_R * LANES), f32),
                        pltpu.VMEM((CHUNK, NH_R * LANES), f32), pltpu.VMEM((NH_R, 2 * DK_R, DV_R), f32)],
        compiler_params=pltpu.CompilerParams(
            dimension_semantics=("parallel",), vmem_limit_bytes=V7X_VMEM_LIMIT),
        name="retention_lat" if has_s0 else "retention_ctx",
    )(*args)


def _post_kernel(l, xc_ref, xl_ref, oac_ref, oal_ref, yrc_ref, yrl_ref, ga_ref, gb_ref, wpa_hbm, wpb_hbm,
                 wo_hbm, mod_ref, n2_ref, wr_ref, br_ref, tri_ref, x1_ref, h2_ref, route_ref, cnt_ref,
                 carry_ref, wpa_ref, wpb_ref, wo_ref, stage_ref, sems):
    i = pl.program_id(0)
    tm = xc_ref.shape[0]
    n_ctx = T_CTX // tm

    @pl.when(i == 0)
    def _():
        carry_ref[...] = jnp.zeros_like(carry_ref)
        _load_cast(wpa_hbm.at[l], wpa_ref, stage_ref, sems)
        _load_cast(wpb_hbm.at[l], wpb_ref, stage_ref, sems)
        _load_cast(wo_hbm.at[l], wo_ref, stage_ref, sems)

    for r0 in range(0, tm, POST_SUB):
        _post_subtile(i < n_ctx, slice(r0, r0 + POST_SUB), xc_ref, xl_ref, oac_ref, oal_ref, yrc_ref,
                      yrl_ref, ga_ref, gb_ref, mod_ref, n2_ref, wr_ref, br_ref, tri_ref, x1_ref, h2_ref,
                      route_ref, carry_ref, wpa_ref, wpb_ref, wo_ref)
    cnt_ref[...] = carry_ref[...]


def _post_subtile(is_ctx, rows, xc_ref, xl_ref, oac_ref, oal_ref, yrc_ref, yrl_ref, ga_ref, gb_ref, mod_ref,
                  n2_ref, wr_ref, br_ref, tri_ref, x1_ref, h2_ref, route_ref, carry_ref, wpa_ref, wpb_ref,
                  wo_ref):
    tm = rows.stop - rows.start
    pick = lambda a_ref, b_ref: jnp.where(is_ctx, a_ref[rows, :], b_ref[rows, :])
    ga = _sigmoid(ga_ref[rows, :].astype(f32))
    gb = _sigmoid(gb_ref[rows, :].astype(f32))
    merged = (ga * _dot(pick(oac_ref, oal_ref), wpa_ref[...])
              + gb * _dot(pick(yrc_ref, yrl_ref), wpb_ref[...]))
    mix = _dot(merged.astype(bf16), wo_ref[...])
    x1 = pick(xc_ref, xl_ref) + mod_ref[2:3, :] * mix
    x1_ref[rows, :] = x1
    y = x1 * lax.rsqrt(jnp.mean(x1 * x1, axis=-1, keepdims=True) + EPS) * n2_ref[...]
    h2 = y * (1.0 + mod_ref[4:5, :]) + mod_ref[3:4, :]
    h2_ref[rows, 0:D] = h2

    h_hi, h_lo = _split(h2)
    both = _dot(h_hi, wr_ref[...])
    logits = both[:, 0:LANES] + (both[:, LANES:2 * LANES] + _dot(h_lo, wr_ref[:, 0:LANES]))
    lt = logits.T[0:N_EXP, :]
    scores = _sigmoid(lt)
    sel = scores + br_ref[:, 0:1]
    row = lax.broadcasted_iota(jnp.int32, (N_EXP, tm), 0)

    best = None
    bg = None
    for g in range(N_GROUPS):
        a, b, c, d = (sel[EXP_PER_GROUP * g + k:EXP_PER_GROUP * g + k + 1, :] for k in range(4))
        p, q = jnp.maximum(a, b), jnp.minimum(a, b)
        r, s = jnp.maximum(c, d), jnp.minimum(c, d)
        gs = jnp.maximum(p, r) + jnp.maximum(jnp.minimum(p, r), jnp.maximum(q, s))
        if g == 0:
            best, bg = gs, jnp.zeros((1, tm), jnp.int32)
        else:
            upd = gs > best
            bg = jnp.where(upd, g, bg)
            best = jnp.where(upd, gs, best)
    masked = jnp.where(jnp.right_shift(row, 2) == bg, sel, NEG_INF)
    m1 = jnp.max(masked, axis=0, keepdims=True)
    i1 = jnp.min(jnp.where(masked == m1, row, N_EXP), axis=0, keepdims=True)
    masked2 = jnp.where(row == i1, NEG_INF, masked)
    m2 = jnp.max(masked2, axis=0, keepdims=True)
    i2 = jnp.min(jnp.where(masked2 == m2, row, N_EXP), axis=0, keepdims=True)
    oh1 = row == i1
    oh2 = row == i2
    s1 = jnp.sum(jnp.where(oh1, scores, 0.0), axis=0, keepdims=True)
    s2 = jnp.sum(jnp.where(oh2, scores, 0.0), axis=0, keepdims=True)
    den = s1 + s2

    comb = jnp.where(oh1, s1 / den, 0.0) + jnp.where(oh2, s2 / den, 0.0)
    comb_t = jnp.concatenate([comb, jnp.zeros((LANES - N_EXP, tm), f32)], axis=0).T
    h2_ref[rows, D:D + LANES] = comb_t

    e_lo = jnp.minimum(i1, i2) - EXP_PER_GROUP * bg
    e_hi = jnp.maximum(i1, i2) - EXP_PER_GROUP * bg
    pair = jnp.right_shift(e_lo * (7 - e_lo), 1) + (e_hi - e_lo - 1)
    bucket = bg * PAIRS + pair
    rowb = lax.broadcasted_iota(jnp.int32, (BUCKET_ROWS, tm), 0)
    ohb = rowb == bucket
    ohb_f = jnp.where(ohb, 1.0, 0.0)
    tot = carry_ref[:, 0:1] + _dot(ohb_f.astype(bf16), tri_ref[...])
    rank = jnp.sum(jnp.where(ohb, tot, 0.0), axis=0, keepdims=True)
    carry_ref[...] = carry_ref[...] + jnp.sum(ohb_f, axis=1, keepdims=True)

    route_ref[0:1, rows] = bucket.astype(f32)
    route_ref[1:2, rows] = rank
    route_ref[2:8, rows] = jnp.zeros((6, tm), f32)


def _post(l, x_ctx, x_lat, oa_ctx, oa_lat, yr_ctx, yr_lat, ga, gb, wpa, wpb, wo, mod, norm2_w, wr_pad,
          br_col, tri):
    tm = TM_POST
    row = lambda w: pl.BlockSpec((tm, w), lambda i: (i, 0))
    const = lambda a: pl.BlockSpec(a.shape, lambda i: (0,) * a.ndim)
    hbm = pl.BlockSpec(memory_space=pl.ANY)
    return pl.pallas_call(
        functools.partial(_post_kernel, l),
        grid=(T // tm,),
        in_specs=(_two_stream_specs(tm, D) + _two_stream_specs(tm, Q_A) + _two_stream_specs(tm, V_R)
                  + [row(D), row(D), hbm, hbm, hbm,
                     pl.BlockSpec((None, None, 6, D), lambda i: (l, _cond_of_tile(i, tm), 0, 0)),
                     pl.BlockSpec((1, D), lambda i: (0, 0)), const(wr_pad), const(br_col), const(tri)]),
        out_specs=[row(D), row(MOE_W), pl.BlockSpec((8, tm), lambda i: (0, i)),
                   pl.BlockSpec((BUCKET_ROWS, LANES), lambda i: (0, 0))],
        out_shape=[jax.ShapeDtypeStruct((T, D), f32), jax.ShapeDtypeStruct((T, MOE_W), f32),
                   jax.ShapeDtypeStruct((8, T), f32), jax.ShapeDtypeStruct((BUCKET_ROWS, LANES), f32)],
        scratch_shapes=[pltpu.VMEM((BUCKET_ROWS, LANES), f32), pltpu.VMEM((Q_A, D), bf16),
                        pltpu.VMEM((V_R, D), bf16), pltpu.VMEM((D, D), bf16),
                        pltpu.VMEM((2, D, W_CHUNK), f32), pltpu.SemaphoreType.DMA((2,))],
        compiler_params=pltpu.CompilerParams(
            dimension_semantics=("arbitrary",), vmem_limit_bytes=V7X_VMEM_LIMIT),
        name="post_router",
    )(x_ctx, x_lat, oa_ctx, oa_lat, yr_ctx, yr_lat, ga, gb, wpa, wpb, wo, mod, norm2_w[l:l + 1],
      wr_pad, br_col, tri)


def _row_copy(src_ref, src_row, dst_ref, dst_row, sem):
    return pltpu.make_async_copy(src_ref.at[pl.ds(src_row, 1)], dst_ref.at[pl.ds(dst_row, 1)], sem)


def _dispatch_kernel(zf_ref, pos_ref, h_ref, xs_ref, zero_ref, sem):
    tm = h_ref.shape[0]

    @pl.when(pl.program_id(0) == 0)
    def _():
        zero_ref[...] = jnp.zeros_like(zero_ref)

        def tile_copy(t):
            return pltpu.make_async_copy(zero_ref, xs_ref.at[pl.ds(t * MOE_TR, MOE_TR)], sem)

        def start(t, carry):
            @pl.when(zf_ref[t] != 0)
            def _():
                tile_copy(t).start()
            return carry

        def wait(t, carry):
            @pl.when(zf_ref[t] != 0)
            def _():
                tile_copy(t).wait()
            return carry

        lax.fori_loop(0, MOE_TILES, start, 0)
        lax.fori_loop(0, MOE_TILES, wait, 0)

    for r in range(tm):
        _row_copy(h_ref, r, xs_ref, pos_ref[0, 0, r], sem).start(priority=r % 2)
    pltpu.make_async_copy(h_ref, xs_ref.at[pl.ds(0, tm)], sem).wait()


def _dispatch(zero_flags, pos_blocks, h2):
    tm = TM_DISP
    return pl.pallas_call(
        _dispatch_kernel,
        grid_spec=pltpu.PrefetchScalarGridSpec(
            num_scalar_prefetch=1,
            grid=(T // tm,),
            in_specs=[pl.BlockSpec((1, 1, tm), lambda i, zf: (i, 0, 0), memory_space=pltpu.SMEM),
                      pl.BlockSpec((tm, MOE_W), lambda i, zf: (i, 0))],
            out_specs=pl.BlockSpec(memory_space=pl.ANY),
            scratch_shapes=[pltpu.VMEM((MOE_TR, MOE_W), f32), pltpu.SemaphoreType.DMA(())],
        ),
        out_shape=jax.ShapeDtypeStruct((MOE_ROWS, MOE_W), f32),
        compiler_params=pltpu.CompilerParams(
            dimension_semantics=("arbitrary",), vmem_limit_bytes=V7X_VMEM_LIMIT),
        name="moe_dispatch",
    )(zero_flags, pos_blocks, h2)


def _experts_kernel(l, tg_ref, ta_ref, tb_ref, first_ref, nxt_ref, nt_ref, xs_ref, wg_hbm, wu_hbm, wd_hbm,
                    y_ref, wg_ref, wu_ref, wd_ref, sg_ref, su_ref, sd_ref, sems):
    i = pl.program_id(0)
    live = i < nt_ref[0]
    g = tg_ref[i]

    def fetch(grp):
        return [pltpu.make_async_copy(w.at[l, pl.ds(grp * EXP_PER_GROUP, EXP_PER_GROUP)], s, sems.at[k])
                for k, (w, s) in enumerate(((wg_hbm, sg_ref), (wu_hbm, su_ref), (wd_hbm, sd_ref)))]

    @pl.when(i == 0)
    def _():
        for c in fetch(tg_ref[0]):
            c.start()

    @pl.when(live & (first_ref[i] != 0))
    def _():
        for c in fetch(g):
            c.wait()
        for j in range(EXP_PER_GROUP):
            wg_ref[j] = sg_ref[j].astype(bf16)
            wu_ref[j] = su_ref[j].astype(bf16)
            wd_ref[j] = sd_ref[j].astype(bf16)

        @pl.when(nxt_ref[i] >= 0)
        def _():
            for c in fetch(nxt_ref[i]):
                c.start(priority=1)

    @pl.when(live)
    def _():
        x = xs_ref[:, 0:D].astype(bf16)
        comb = xs_ref[:, D:D + LANES]
        lane = lax.broadcasted_iota(jnp.int32, comb.shape, 1)
        acc = None
        for j in (ta_ref[i], tb_ref[i]):
            w_j = jnp.sum(jnp.where(lane == g * EXP_PER_GROUP + j, comb, 0.0), axis=1, keepdims=True)
            act = _silu(_dot(x, wg_ref[j])) * _dot(x, wu_ref[j]) * w_j
            part = _dot(act.astype(bf16), wd_ref[j])
            acc = part if acc is None else acc + part
        y_ref[...] = acc

    @pl.when(i >= nt_ref[0])
    def _():
        y_ref[...] = jnp.zeros_like(y_ref)


def _experts(l, plan, xs, wg, wu, wd):
    def tile(i, *prefetch):
        return jnp.minimum(i, prefetch[-1][0] - 1)

    hbm = pl.BlockSpec(memory_space=pl.ANY)
    return pl.pallas_call(
        functools.partial(_experts_kernel, l),
        grid_spec=pltpu.PrefetchScalarGridSpec(
            num_scalar_prefetch=6,
            grid=(MOE_TILES,),
            in_specs=[pl.BlockSpec((MOE_TR, MOE_W), lambda i, *p: (tile(i, *p), 0)), hbm, hbm, hbm],
            out_specs=pl.BlockSpec((MOE_TR, D), lambda i, *p: (i, 0)),
            scratch_shapes=[pltpu.VMEM((EXP_PER_GROUP, D, D_EXP), bf16),
                            pltpu.VMEM((EXP_PER_GROUP, D, D_EXP), bf16),
                            pltpu.VMEM((EXP_PER_GROUP, D_EXP, D), bf16),
                            pltpu.VMEM((EXP_PER_GROUP, D, D_EXP), f32),
                            pltpu.VMEM((EXP_PER_GROUP, D, D_EXP), f32),
                            pltpu.VMEM((EXP_PER_GROUP, D_EXP, D), f32), pltpu.SemaphoreType.DMA((3,))],
        ),
        out_shape=jax.ShapeDtypeStruct((MOE_ROWS, D), f32),
        compiler_params=pltpu.CompilerParams(
            dimension_semantics=("arbitrary",), vmem_limit_bytes=V7X_VMEM_LIMIT),
        name="moe_experts",
    )(*plan, xs, wg, wu, wd)


def _combine_kernel(pos_ref, posn_ref, x1_ref, mod_ref, y_ref, oc_ref, ol_ref, ya_ref, yb_ref, sems):
    tm = x1_ref.shape[0]
    i = pl.program_id(0)
    n_ctx = T_CTX // tm

    def pull(p_ref, buf, sem):
        for r in range(tm):
            _row_copy(y_ref, p_ref[0, 0, r], buf, r, sem).start(priority=r % 2)

    @pl.when(i == 0)
    def _():
        pull(pos_ref, ya_ref, sems.at[0])

    def step(cur, nxt, sem_cur, sem_nxt):
        @pl.when(i + 1 < pl.num_programs(0))
        def _():
            pull(posn_ref, nxt, sem_nxt)

        pltpu.make_async_copy(y_ref.at[pl.ds(0, tm)], cur, sem_cur).wait()
        out = x1_ref[...] + mod_ref[5:6, :] * cur[...]

        @pl.when(i < n_ctx)
        def _():
            oc_ref[...] = out

        @pl.when(i >= n_ctx)
        def _():
            ol_ref[...] = out

    @pl.when(i % 2 == 0)
    def _():
        step(ya_ref, yb_ref, sems.at[0], sems.at[1])

    @pl.when(i % 2 == 1)
    def _():
        step(yb_ref, ya_ref, sems.at[1], sems.at[0])


def _combine(l, pos_blocks, x1, mod, y):
    tm = TM_DISP
    n = T // tm
    return pl.pallas_call(
        _combine_kernel,
        grid=(n,),
        in_specs=[pl.BlockSpec((1, 1, tm), lambda i: (i, 0, 0), memory_space=pltpu.SMEM),
                  pl.BlockSpec((1, 1, tm), lambda i: (jnp.minimum(i + 1, n - 1), 0, 0),
                               memory_space=pltpu.SMEM),
                  pl.BlockSpec((tm, D), lambda i: (i, 0)),
                  pl.BlockSpec((None, None, 6, D), lambda i: (l, _cond_of_tile(i, tm), 0, 0)),
                  pl.BlockSpec(memory_space=pl.ANY)],
        out_specs=_two_stream_specs(tm, D),
        out_shape=[jax.ShapeDtypeStruct((T_CTX, D), f32), jax.ShapeDtypeStruct((T_LAT, D), f32)],
        scratch_shapes=[pltpu.VMEM((tm, D), f32), pltpu.VMEM((tm, D), f32),
                        pltpu.SemaphoreType.DMA((2,))],
        compiler_params=pltpu.CompilerParams(
            dimension_semantics=("arbitrary",), vmem_limit_bytes=V7X_VMEM_LIMIT),
        name="moe_combine",
    )(pos_blocks, pos_blocks, x1, mod, y)


def _plan_kernel(cnt_ref, off_ref, tg_ref, ta_ref, tb_ref, first_ref, nxt_ref, nt_ref, zf_ref, nxtg_ref):
    def bucket(b, start):
        n = lax.shift_right_logical(cnt_ref[b] + (MOE_TR - 1), MOE_TR.bit_length() - 1)
        off_ref[b] = start * MOE_TR
        g = b // PAIRS
        p = b - g * PAIRS
        lo = jnp.where(p >= 3, 1, 0) + jnp.where(p >= 5, 1, 0)
        hi = p + 1 - jnp.where(lo >= 1, 2, 0) - jnp.where(lo >= 2, 1, 0)

        def tile(t, c):
            tg_ref[t] = g
            ta_ref[t] = lo
            tb_ref[t] = hi
            zf_ref[t] = jnp.where(t == start + n - 1, 1, 0)
            return c

        lax.fori_loop(start, start + n, tile, 0)
        return start + n

    nt = lax.fori_loop(0, N_BUCKETS, bucket, jnp.int32(0))
    nt_ref[0] = nt

    def unused(t, c):
        tg_ref[t] = N_GROUPS - 1
        ta_ref[t] = 0
        tb_ref[t] = 1
        zf_ref[t] = 1
        first_ref[t] = 0
        nxt_ref[t] = -1
        return c

    lax.fori_loop(nt, MOE_TILES, unused, 0)

    def back(i, nxt_group):
        t = nt - 1 - i
        g = tg_ref[t]
        later = jnp.where(t + 1 < nt, tg_ref[jnp.minimum(t + 1, MOE_TILES - 1)], -1)
        is_last_of_group = later != g
        nxt_group = jnp.where(is_last_of_group, later, nxt_group)
        nxtg_ref[g] = nxt_group
        return nxt_group

    lax.fori_loop(0, nt, back, jnp.int32(-1))

    def fwd(t, c):
        g = tg_ref[t]
        first_ref[t] = jnp.where((t == 0) | (tg_ref[jnp.maximum(t - 1, 0)] != g), 1, 0)
        nxt_ref[t] = nxtg_ref[g]
        return c

    lax.fori_loop(0, nt, fwd, 0)


def _moe_plan(route, counts):
    smem = pl.BlockSpec(memory_space=pltpu.SMEM)
    tile_i32 = jax.ShapeDtypeStruct((MOE_TILES,), jnp.int32)
    off, tg, ta, tb, first, nxt, n_tiles, zero_flags = pl.pallas_call(
        _plan_kernel,
        in_specs=[smem],
        out_specs=[smem] * 8,
        out_shape=[jax.ShapeDtypeStruct((N_BUCKETS,), jnp.int32), tile_i32, tile_i32, tile_i32, tile_i32,
                   tile_i32, jax.ShapeDtypeStruct((1,), jnp.int32), tile_i32],
        scratch_shapes=[pltpu.SMEM((N_GROUPS,), jnp.int32)],
        name="moe_plan",
    )(counts[0:N_BUCKETS, 0].astype(jnp.int32))
    buckets = jnp.arange(N_BUCKETS, dtype=f32)[:, None]
    pos = jnp.sum(jnp.where(route[0][None, :] == buckets, off.astype(f32)[:, None], 0.0), axis=0) + route[1]
    pos_blocks = pos.astype(jnp.int32).reshape(T // TM_DISP, 1, TM_DISP)
    return pos_blocks, zero_flags, (tg, ta, tb, first, nxt, n_tiles)


def _rope_tables():
    pos = np.arange(DEC_SEQ)
    half = HD_A // 4
    freqs = ROPE_BASE ** (-np.arange(half, dtype=np.float64) / half)
    ang_r = (pos // GRID_W)[:, None] * freqs[None, :]
    ang_c = (pos % GRID_W)[:, None] * freqs[None, :]
    ang = np.concatenate([ang_r, ang_r, ang_c, ang_c], axis=1)
    sign = np.concatenate([-np.ones(half), np.ones(half)] * 2)[None, :]
    cos = np.tile(np.cos(ang), (1, 2))
    sin = np.tile(np.sin(ang) * sign, (1, 2))
    ident_c = np.ones((TM_PROJ, LANES))
    ident_s = np.zeros((TM_PROJ, LANES))
    return (jnp.asarray(np.concatenate([cos, ident_c]), f32),
            jnp.asarray(np.concatenate([sin, ident_s]), f32))


def kernel(x_prompt, x_sample, c, cache_k, cache_v, state_ret_fwd, state_ret_bwd, c_ctx,
           norm1_w, norm2_w, w_ada, b_ada, w_in, q_norm_w, k_norm_w, attn_sink,
           ret_decay_fwd, ret_decay_bwd, ret_gn_w, w_pa, w_pb, w_o, w_router, b_router,
           w_exp_gate, w_exp_up, w_exp_down):
    x_ctx, x_lat = x_prompt.reshape(T_CTX, D), x_sample.reshape(T_LAT, D)
    cond8 = jnp.zeros((N_COND, D), f32).at[0].set(c_ctx).at[1:1 + DEC_BATCH].set(c)
    mod = _adaln(cond8, w_ada, b_ada).reshape(DEPTH, N_COND, 6, D)

    cos_t, sin_t = _rope_tables()
    blk = np.arange(Q_A) // HD_A
    ones_blk = jnp.asarray((blk[:, None] == blk[None, :]) / HD_A, bf16)
    tri = jnp.asarray(np.triu(np.ones((POST_SUB, POST_SUB)), 1), bf16)
    qnw = jnp.tile(q_norm_w, (1, NH_A))
    knw = jnp.tile(k_norm_w, (1, NKV_A))
    wr_pad = jnp.concatenate(_split(jnp.pad(w_router, ((0, 0), (0, LANES - N_EXP)))), axis=1)
    br_col = jnp.broadcast_to(b_router[:, None], (N_EXP, LANES))
    ck = cache_k.reshape(DEC_BATCH, DEPTH, PAST, KV_A)
    cv = cache_v.reshape(DEC_BATCH, DEPTH, PAST, KV_A)
    s0 = jnp.concatenate([state_ret_fwd, state_ret_bwd], axis=3)
    lg = jnp.log1p(-jnp.exp(jnp.concatenate([ret_decay_fwd, ret_decay_bwd], axis=1).astype(f32)))

    new_k, new_v, new_sf, new_sb = [], [], [], []
    for l in range(DEPTH):
        qa, ka, va, qr, kr, vr, gr, ga, gb = _inproj(l, x_ctx, x_lat, norm1_w, mod, w_in, cos_t, sin_t,
                                                     ones_blk, qnw, knw)
        oa_ctx = _ctx_attn(l, qa, ka, va, attn_sink)
        oa_lat = _lat_attn(l, qa, ka, va, ck, cv, cos_t, sin_t, attn_sink)
        yr_ctx, sf, sb = _retention(l, qr, kr, vr, gr, None, lg, ret_gn_w, BATCH, SEQ, 0,
                                    CTX_SEQS_PER_STEP)
        yr_lat, _, _ = _retention(l, qr, kr, vr, gr, s0[:, l], lg, ret_gn_w, DEC_BATCH, DEC_SEQ,
                                  T_CTX // DEC_SEQ, 1)
        x1, h2, route, counts = _post(l, x_ctx, x_lat, oa_ctx, oa_lat, yr_ctx, yr_lat, ga, gb,
                                      w_pa, w_pb, w_o, mod, norm2_w, wr_pad, br_col, tri)
        pos_blocks, zero_flags, plan = _moe_plan(route, counts)
        xs = _dispatch(zero_flags, pos_blocks, h2)
        y = _experts(l, plan, xs, w_exp_gate, w_exp_up, w_exp_down)
        x_ctx, x_lat = _combine(l, pos_blocks, x1, mod, y)

        new_k.append(ka[:T_CTX].reshape(BATCH, SEQ, NKV_A, HD_A))
        new_v.append(va[:T_CTX].reshape(BATCH, SEQ, NKV_A, HD_A))
        new_sf.append(sf)
        new_sb.append(sb)

    return (x_ctx.reshape(BATCH, SEQ, D), x_lat.reshape(DEC_BATCH, DEC_SEQ, D),
            jnp.stack(new_k, axis=1), jnp.stack(new_v, axis=1),
            jnp.stack(new_sf, axis=1), jnp.stack(new_sb, axis=1))
```

```python
import functools

import numpy as np
import jax
import jax.numpy as jnp
from jax import lax
from jax.experimental import pallas as pl
from jax.experimental.pallas import tpu as pltpu

D = 1024
BATCH, SEQ = 16, 256
DEC_BATCH, DEC_SEQ = 2, 2048
DEPTH = 2
PAST = 512
GRID_W = 64
NH_A, NKV_A, HD_A = 8, 2, 64
WINDOW = 128
NH_R, DK_R, DV_R = 4, 64, 128
CHUNK = 128
N_EXP, N_GROUPS, EXP_PER_GROUP = 16, 4, 4
D_EXP = 512
ROPE_BASE = 10000.0
EPS = 1e-6
NEG_INF = -1e30

Q_A = NH_A * HD_A
KV_A = NKV_A * HD_A
QK_R = NH_R * DK_R
V_R = NH_R * DV_R
C_QA = (0, Q_A)
C_KA = (C_QA[1], C_QA[1] + KV_A)
C_VA = (C_KA[1], C_KA[1] + KV_A)
C_QR = (C_VA[1], C_VA[1] + QK_R)
C_KR = (C_QR[1], C_QR[1] + QK_R)
C_VR = (C_KR[1], C_KR[1] + V_R)
C_GR = (C_VR[1], C_VR[1] + V_R)
C_GA = (C_GR[1], C_GR[1] + D)
C_GB = (C_GA[1], C_GA[1] + D)
D_IN = C_GB[1]

T_CTX = BATCH * SEQ
T_LAT = DEC_BATCH * DEC_SEQ
T = T_CTX + T_LAT
N_COND = 8

LANES = 128
V7X_VMEM_LIMIT = 56 * 1024 * 1024

TM_PROJ = 512
TM_POST = 512
POST_SUB = 256
W_CHUNK = 256
TM_DISP = 1024
MOE_TR = 256
PAIRS = EXP_PER_GROUP * (EXP_PER_GROUP - 1) // 2
N_BUCKETS = N_GROUPS * PAIRS
BUCKET_ROWS = 32
MOE_TILES = T // MOE_TR + N_BUCKETS
MOE_ROWS = MOE_TILES * MOE_TR
MOE_W = D + LANES
ATT_QB = 256
RET_UNROLL = 4
CTX_SEQS_PER_STEP = 8

f32 = jnp.float32
bf16 = jnp.bfloat16


def _dot(a, b):
    return jnp.dot(a, b, preferred_element_type=f32)


def _dot_t(a, b):
    return lax.dot_general(a, b, (((1,), (1,)), ((), ())), preferred_element_type=f32)


def _split(x):
    hi = x.astype(bf16)
    lo = (x - hi.astype(f32)).astype(bf16)
    return hi, lo


def _dot3(a, b):
    ah, al = _split(a)
    bh, bl = _split(b)
    return _dot(ah, bh) + (_dot(ah, bl) + _dot(al, bh))


def _sigmoid(x):
    return 1.0 / (1.0 + jnp.exp(-x))


def _silu(x):
    return x * _sigmoid(x)


def _cond_of_tile(i, tm):
    n_ctx = T_CTX // tm
    per_b = DEC_SEQ // tm
    return jnp.where(i < n_ctx, 0, 1 + jnp.maximum(i - n_ctx, 0) // per_b)


def _adaln_kernel(cond_ref, w_ref, b_ref, o_ref):
    a = _silu(cond_ref[...])
    o_ref[0] = _dot3(a, w_ref[0]) + b_ref[0]


def _adaln(cond8, w_ada, b_ada):
    tn = 2048
    return pl.pallas_call(
        _adaln_kernel,
        grid=(DEPTH, 6 * D // tn),
        in_specs=[
            pl.BlockSpec((N_COND, D), lambda l, j: (0, 0)),
            pl.BlockSpec((1, D, tn), lambda l, j: (l, 0, j)),
            pl.BlockSpec((1, 1, tn), lambda l, j: (l, 0, j)),
        ],
        out_specs=pl.BlockSpec((1, N_COND, tn), lambda l, j: (l, 0, j)),
        out_shape=jax.ShapeDtypeStruct((DEPTH, N_COND, 6 * D), f32),
        compiler_params=pltpu.CompilerParams(
            dimension_semantics=("parallel", "parallel"), vmem_limit_bytes=V7X_VMEM_LIMIT),
        name="adaln",
    )(cond8, w_ada, b_ada.reshape(DEPTH, 1, 6 * D))


def _rope(x, cos, sin_signed, first_half):
    fwd = pltpu.roll(x, 16, 1)
    bwd = pltpu.roll(x, LANES - 16, 1)
    partner = jnp.where(first_half, bwd, fwd)
    return x * cos + partner * sin_signed


def _head_rms(x, ones_blk, w):
    n = x.shape[1]
    sq_hi, sq_lo = _split(x * x)
    blk = ones_blk[0:n, 0:n]
    mean = _dot(sq_hi, blk) + _dot(sq_lo, blk)
    return x * lax.rsqrt(mean + EPS) * w


def _load_cast(w_hbm, dst_ref, stage_ref, sems):
    k, n = w_hbm.shape

    def chunk_copy(c):
        return pltpu.make_async_copy(w_hbm.at[:, pl.ds(c * W_CHUNK, W_CHUNK)],
                                     stage_ref.at[c % 2, pl.ds(0, k)], sems.at[c % 2])

    n_chunks = n // W_CHUNK
    chunk_copy(0).start()
    for c in range(n_chunks):
        if c + 1 < n_chunks:
            chunk_copy(c + 1).start()
        chunk_copy(c).wait()
        dst_ref[:, c * W_CHUNK:(c + 1) * W_CHUNK] = stage_ref[c % 2, 0:k, :].astype(bf16)


def _tile_of_two(i, n_first, a_ref, b_ref):
    return jnp.where(i < n_first, a_ref[...], b_ref[...])


def _inproj_kernel(l, xc_ref, xl_ref, n1_ref, mod_ref, w_hbm, cos_ref, sin_ref, ones_ref, qnw_ref,
                   knw_ref, qa_ref, ka_ref, va_ref, qr_ref, kr_ref, vr_ref, gr_ref, ga_ref, gb_ref,
                   w_ref, stage_ref, sems):
    i = pl.program_id(0)

    @pl.when(i == 0)
    def _():
        _load_cast(w_hbm.at[l], w_ref, stage_ref, sems)

    x = _tile_of_two(i, T_CTX // xc_ref.shape[0], xc_ref, xl_ref)
    y = x * lax.rsqrt(jnp.mean(x * x, axis=-1, keepdims=True) + EPS) * n1_ref[...]
    h = (y * (1.0 + mod_ref[1:2, :]) + mod_ref[0:1, :]).astype(bf16)

    def proj(c):
        return _dot(h, w_ref[:, c[0]:c[1]])

    cos = cos_ref[...]
    sin = sin_ref[...]
    lane = lax.broadcasted_iota(jnp.int32, cos.shape, 1)
    first_half = (lane % 32) < 16

    def rope_all(v):
        parts = [_rope(v[:, j:j + LANES], cos, sin, first_half) for j in range(0, v.shape[1], LANES)]
        return parts[0] if len(parts) == 1 else jnp.concatenate(parts, axis=1)

    ones_blk = ones_ref[...]
    qa = _head_rms(proj(C_QA), ones_blk, qnw_ref[...])
    qa_ref[...] = (rope_all(qa) * (HD_A ** -0.5)).astype(bf16)
    ka_ref[...] = _head_rms(proj(C_KA), ones_blk, knw_ref[...])
    va_ref[...] = proj(C_VA)
    qr_ref[...] = rope_all(proj(C_QR)).astype(bf16)
    kr_ref[...] = (rope_all(proj(C_KR)) * (DK_R ** -0.5)).astype(bf16)
    vr_ref[...] = proj(C_VR).astype(bf16)
    gr_ref[...] = proj(C_GR).astype(bf16)
    ga_ref[...] = proj(C_GA).astype(bf16)
    gb_ref[...] = proj(C_GB).astype(bf16)


def _two_stream_specs(tm, width):
    n_ctx = T_CTX // tm
    return [pl.BlockSpec((tm, width), lambda i, *_: (jnp.minimum(i, n_ctx - 1), 0)),
            pl.BlockSpec((tm, width), lambda i, *_: (jnp.maximum(i - n_ctx, 0), 0))]


def _inproj(l, x_ctx, x_lat, norm1_w, mod, w_in, cos_t, sin_t, ones_blk, qnw, knw):
    tm = TM_PROJ
    n_ctx = T_CTX // tm
    per_b = DEC_SEQ // tm

    def tab_map(i):
        return (jnp.where(i < n_ctx, per_b, jnp.maximum(i - n_ctx, 0) % per_b), 0)

    row = lambda i: (i, 0)
    const = lambda i: (0, 0)
    widths = [(Q_A, bf16), (KV_A, f32), (KV_A, f32), (QK_R, bf16), (QK_R, bf16), (V_R, bf16),
              (V_R, bf16), (D, bf16), (D, bf16)]
    return pl.pallas_call(
        functools.partial(_inproj_kernel, l),
        grid=(T // tm,),
        in_specs=_two_stream_specs(tm, D) + [
            pl.BlockSpec((1, D), const),
            pl.BlockSpec((None, None, 6, D), lambda i: (l, _cond_of_tile(i, tm), 0, 0)),
            pl.BlockSpec(memory_space=pl.ANY),
            pl.BlockSpec((tm, LANES), tab_map),
            pl.BlockSpec((tm, LANES), tab_map),
            pl.BlockSpec((Q_A, Q_A), const),
            pl.BlockSpec((1, Q_A), const),
            pl.BlockSpec((1, KV_A), const),
        ],
        out_specs=[pl.BlockSpec((tm, w), row) for w, _ in widths],
        out_shape=[jax.ShapeDtypeStruct((T, w), dt) for w, dt in widths],
        scratch_shapes=[pltpu.VMEM((D, D_IN), bf16), pltpu.VMEM((2, D, W_CHUNK), f32),
                        pltpu.SemaphoreType.DMA((2,))],
        compiler_params=pltpu.CompilerParams(
            dimension_semantics=("arbitrary",), vmem_limit_bytes=V7X_VMEM_LIMIT),
        name="inproj",
    )(x_ctx, x_lat, norm1_w[l:l + 1], mod, w_in, cos_t, sin_t, ones_blk, qnw[l:l + 1], knw[l:l + 1])


def _head_blocks(t, kv, lo_mask):
    r = pltpu.roll(t, HD_A, 1)
    if kv == 0:
        a = jnp.where(lo_mask, t, 0.0)
        b = jnp.where(lo_mask, 0.0, r)
    else:
        a = jnp.where(lo_mask, r, 0.0)
        b = jnp.where(lo_mask, 0.0, t)
    return jnp.concatenate([a, b], axis=0).astype(bf16)


def _ctx_attn_kernel(sink_ref, q_ref, k_ref, v_ref, o_ref):
    n = SEQ
    lo_mask = lax.broadcasted_iota(jnp.int32, (n, LANES), 1) < HD_A
    for b in range(CTX_SEQS_PER_STEP):
        rows = slice(b * n, (b + 1) * n)
        k = k_ref[rows, :]
        v = v_ref[rows, :]
        for kv in range(NKV_A):
            kblk = _head_blocks(k, kv, lo_mask)
            vblk = _head_blocks(v, kv, lo_mask)
            for pr in range(2):
                pi = kv * 2 + pr
                s = _dot_t(q_ref[rows, pi * LANES:(pi + 1) * LANES], kblk)
                ps, invs = [], []
                for hh in range(2):
                    sk = sink_ref[2 * pi + hh]
                    sh = s[:, hh * n:(hh + 1) * n]
                    m = jnp.maximum(jnp.max(sh, axis=-1, keepdims=True), sk)
                    p = jnp.exp(sh - m)
                    invs.append(1.0 / (jnp.sum(p, axis=-1, keepdims=True) + jnp.exp(sk - m)))
                    ps.append(p.astype(bf16))
                o = _dot(jnp.concatenate(ps, axis=1), vblk) * jnp.where(lo_mask, invs[0], invs[1])
                o_ref[rows, pi * LANES:(pi + 1) * LANES] = o.astype(bf16)


def _ctx_attn(l, qa, ka, va, sink):
    blk = lambda w: pl.BlockSpec((CTX_SEQS_PER_STEP * SEQ, w), lambda b: (b, 0))
    return pl.pallas_call(
        _ctx_attn_kernel,
        grid=(BATCH // CTX_SEQS_PER_STEP,),
        in_specs=[pl.BlockSpec(memory_space=pltpu.SMEM), blk(Q_A), blk(KV_A), blk(KV_A)],
        out_specs=blk(Q_A),
        out_shape=jax.ShapeDtypeStruct((T_CTX, Q_A), bf16),
        compiler_params=pltpu.CompilerParams(
            dimension_semantics=("parallel",), vmem_limit_bytes=V7X_VMEM_LIMIT),
        name="ctx_attn",
    )(sink[l], qa, ka, va)


def _lat_attn_kernel(sink_ref, q_ref, k_ref, v_ref, kc_ref, vc_ref, cos_ref, sin_ref, o_ref):
    j = pl.program_id(1)
    qb = ATT_QB
    win = 2 * qb
    ws = pl.multiple_of(jnp.clip(j * qb - WINDOW, 0, DEC_SEQ - win), WINDOW)
    lo_mask = lax.broadcasted_iota(jnp.int32, (win, LANES), 1) < HD_A
    lane = lax.broadcasted_iota(jnp.int32, (win, LANES), 1)
    kw = _rope(k_ref[pl.ds(ws, win), :], cos_ref[pl.ds(ws, win), :], sin_ref[pl.ds(ws, win), :],
               (lane % 32) < 16)
    vw = v_ref[pl.ds(ws, win), :]
    kc = kc_ref[...]
    vc = vc_ref[...]
    qpos = j * qb + (lax.broadcasted_iota(jnp.int32, (2 * qb, win), 0) & (qb - 1))
    kpos = ws + lax.broadcasted_iota(jnp.int32, (2 * qb, win), 1)
    valid = jnp.abs(qpos - kpos) <= WINDOW
    out_lo = lax.broadcasted_iota(jnp.int32, (2 * qb, LANES), 1) < HD_A
    for kv in range(NKV_A):
        kc_blk = _head_blocks(kc, kv, lo_mask[:PAST])
        vc_blk = _head_blocks(vc, kv, lo_mask[:PAST])
        kw_blk = _head_blocks(kw, kv, lo_mask)
        vw_blk = _head_blocks(vw, kv, lo_mask)
        q2 = jnp.concatenate([q_ref[:, (2 * kv) * LANES:(2 * kv + 1) * LANES],
                              q_ref[:, (2 * kv + 1) * LANES:(2 * kv + 2) * LANES]], axis=0)
        s_c = _dot_t(q2, kc_blk)
        s_w = _dot_t(q2, kw_blk)
        pcs, pws, invs = [], [], []
        for hh in range(2):
            row = lax.broadcasted_iota(jnp.int32, (2 * qb, 1), 0)
            sk = jnp.where(row < qb, sink_ref[4 * kv + hh], sink_ref[4 * kv + 2 + hh])
            sc = s_c[:, hh * PAST:(hh + 1) * PAST]
            sw = jnp.where(valid, s_w[:, hh * win:(hh + 1) * win], NEG_INF)
            m = jnp.maximum(jnp.maximum(jnp.max(sc, axis=-1, keepdims=True),
                                        jnp.max(sw, axis=-1, keepdims=True)), sk)
            pc = jnp.exp(sc - m)
            pw = jnp.exp(sw - m)
            den = (jnp.sum(pc, axis=-1, keepdims=True) + jnp.sum(pw, axis=-1, keepdims=True)
                   + jnp.exp(sk - m))
            invs.append(1.0 / den)
            pcs.append(pc.astype(bf16))
            pws.append(pw.astype(bf16))
        o = _dot(jnp.concatenate(pcs, axis=1), vc_blk) + _dot(jnp.concatenate(pws, axis=1), vw_blk)
        o = o * jnp.where(out_lo, invs[0], invs[1])
        o_ref[:, (2 * kv) * LANES:(2 * kv + 1) * LANES] = o[:qb].astype(bf16)
        o_ref[:, (2 * kv + 1) * LANES:(2 * kv + 2) * LANES] = o[qb:].astype(bf16)


def _lat_attn(l, qa, ka, va, cache_k, cache_v, cos_l, sin_l, sink):
    qb = ATT_QB
    nq = DEC_SEQ // qb
    ctx_blocks = T_CTX // DEC_SEQ
    seq = lambda b, j: (ctx_blocks + b, 0)
    return pl.pallas_call(
        _lat_attn_kernel,
        grid=(DEC_BATCH, nq),
        in_specs=[
            pl.BlockSpec(memory_space=pltpu.SMEM),
            pl.BlockSpec((qb, Q_A), lambda b, j: (T_CTX // qb + b * nq + j, 0)),
            pl.BlockSpec((DEC_SEQ, KV_A), seq),
            pl.BlockSpec((DEC_SEQ, KV_A), seq),
            pl.BlockSpec((None, None, PAST, KV_A), lambda b, j: (b, l, 0, 0)),
            pl.BlockSpec((None, None, PAST, KV_A), lambda b, j: (b, l, 0, 0)),
            pl.BlockSpec((DEC_SEQ, LANES), lambda b, j: (0, 0)),
            pl.BlockSpec((DEC_SEQ, LANES), lambda b, j: (0, 0)),
        ],
        out_specs=pl.BlockSpec((qb, Q_A), lambda b, j: (b * nq + j, 0)),
        out_shape=jax.ShapeDtypeStruct((T_LAT, Q_A), bf16),
        compiler_params=pltpu.CompilerParams(
            dimension_semantics=("parallel", "parallel"), vmem_limit_bytes=V7X_VMEM_LIMIT),
        name="lat_attn",
    )(sink[l], qa, ka, va, cache_k, cache_v, cos_l, sin_l)


def _dup_heads(pair, lo_mask):
    r = pltpu.roll(pair, DK_R, 1)
    return jnp.where(lo_mask, pair, r), jnp.where(lo_mask, r, pair)


def _retention_kernel(has_s0, n_seq, n_chunks, *refs):
    if has_s0:
        (lg_ref, q_ref, k_ref, v_ref, g_ref, s0_ref, gnw_ref,
         y_ref, sf_ref, sb_ref, ds_ref, st_ref, mask_ref, qdec_ref, kdec_ref, cdec_ref) = refs
    else:
        (lg_ref, q_ref, k_ref, v_ref, g_ref, gnw_ref,
         y_ref, sf_ref, sb_ref, ds_ref, st_ref, mask_ref, qdec_ref, kdec_ref, cdec_ref) = refs
        s0_ref = None
    C = CHUNK
    lo_mask = lax.broadcasted_iota(jnp.int32, (C, LANES), 1) < DK_R

    ri = lax.broadcasted_iota(jnp.int32, (C, C), 0).astype(f32)
    diff = ri - lax.broadcasted_iota(jnp.int32, (C, C), 1).astype(f32)
    for h in range(NH_R):
        lg_f, lg_b = lg_ref[h], lg_ref[NH_R + h]
        mask_ref[h] = (jnp.where(diff >= 0, jnp.exp(jnp.maximum(diff, 0.0) * lg_f), 0.0)
                       + jnp.where(diff <= 0, jnp.exp(jnp.maximum(-diff, 0.0) * lg_b), 0.0))
        qdec_ref[:, h * LANES:(h + 1) * LANES] = jnp.exp(
            jnp.where(lo_mask, (ri + 1.0) * lg_f, (C - ri) * lg_b))
        kdec_ref[:, h * LANES:(h + 1) * LANES] = jnp.exp(
            jnp.where(lo_mask, (C - 1.0 - ri) * lg_f, ri * lg_b))
        cdec_ref[h] = jnp.exp(jnp.where(ri < DK_R, C * lg_f, C * lg_b))

    def inc_body(c, carry):
        r0 = pl.multiple_of(c * C, C)
        for pr in range(2):
            kp = k_ref[pl.ds(r0, C), pr * LANES:(pr + 1) * LANES].astype(f32)
            for hh, kd in enumerate(_dup_heads(kp, lo_mask)):
                h = 2 * pr + hh
                kd = (kd * kdec_ref[:, h * LANES:(h + 1) * LANES]).astype(bf16)
                vh = v_ref[pl.ds(r0, C), h * DV_R:(h + 1) * DV_R]
                ds_ref[c, h] = lax.dot_general(kd, vh, (((0,), (0,)), ((), ())),
                                               preferred_element_type=f32)
        return carry

    lax.fori_loop(0, n_seq * n_chunks, inc_body, 0, unroll=RET_UNROLL)

    for sq in range(n_seq):
        base = sq * n_chunks
        for h in range(NH_R):
            cf = cdec_ref[h, 0:DK_R, :]
            cb = cdec_ref[h, DK_R:2 * DK_R, :]
            if has_s0:
                init_f = s0_ref[h, 0:DK_R, :]
                init_b = s0_ref[h, DK_R:2 * DK_R, :]
            else:
                init_f = jnp.zeros((DK_R, DV_R), f32)
                init_b = init_f

            def fwd_body(i, s, h=h, cf=cf, base=base):
                c = base + i
                st_ref[c, h, 0:DK_R, :] = s
                return s * cf + ds_ref[c, h, 0:DK_R, :]

            def bwd_body(i, s, h=h, cb=cb, base=base):
                c = base + n_chunks - 1 - i
                st_ref[c, h, DK_R:2 * DK_R, :] = s
                return s * cb + ds_ref[c, h, DK_R:2 * DK_R, :]

            sf_ref[sq, h] = lax.fori_loop(0, n_chunks, fwd_body, init_f)
            sb_ref[sq, h] = lax.fori_loop(0, n_chunks, bwd_body, init_b)

    def out_body(c, carry):
        r0 = pl.multiple_of(c * C, C)
        for pr in range(2):
            qp = q_ref[pl.ds(r0, C), pr * LANES:(pr + 1) * LANES]
            kp = k_ref[pl.ds(r0, C), pr * LANES:(pr + 1) * LANES].astype(f32)
            kblk = jnp.concatenate([jnp.where(lo_mask, kp, 0.0), jnp.where(lo_mask, 0.0, kp)],
                                   axis=0).astype(bf16)
            a2 = _dot_t(qp, kblk)
            for hh, qd in enumerate(_dup_heads(qp.astype(f32), lo_mask)):
                h = 2 * pr + hh
                a = (a2[:, hh * C:(hh + 1) * C] * mask_ref[h]).astype(bf16)
                vh = v_ref[pl.ds(r0, C), h * DV_R:(h + 1) * DV_R]
                qd = (qd * qdec_ref[:, h * LANES:(h + 1) * LANES]).astype(bf16)
                o = _dot(a, vh) + _dot(qd, st_ref[c, h].astype(bf16))
                mu = jnp.mean(o, axis=-1, keepdims=True)
                d = o - mu
                var = jnp.mean(d * d, axis=-1, keepdims=True)
                yh = d * lax.rsqrt(var + EPS) * gnw_ref[:, h * DV_R:(h + 1) * DV_R]
                g = g_ref[pl.ds(r0, C), h * DV_R:(h + 1) * DV_R].astype(f32)
                y_ref[pl.ds(r0, C), h * DV_R:(h + 1) * DV_R] = (yh * _silu(g)).astype(bf16)
        return carry

    lax.fori_loop(0, n_seq * n_chunks, out_body, 0, unroll=RET_UNROLL)


def _retention(l, qr, kr, vr, gr, s0, lg, gnw, nb, seq, row_block0, n_seq):
    n_chunks = seq // CHUNK
    has_s0 = s0 is not None
    rows = n_seq * seq
    tok = lambda w: pl.BlockSpec((rows, w), lambda b: (row_block0 + b, 0))
    in_specs = [pl.BlockSpec(memory_space=pltpu.SMEM), tok(QK_R), tok(QK_R), tok(V_R), tok(V_R)]
    args = [lg[l], qr, kr, vr, gr]
    if has_s0:
        in_specs.append(pl.BlockSpec((None, NH_R, 2 * DK_R, DV_R), lambda b: (b, 0, 0, 0)))
        args.append(s0)
    in_specs += [pl.BlockSpec((1, V_R), lambda b: (0, 0))]
    args += [gnw[l:l + 1]]
    st_spec = pl.BlockSpec((n_seq, NH_R, DK_R, DV_R), lambda b: (b, 0, 0, 0))
    return pl.pallas_call(
        functools.partial(_retention_kernel, has_s0, n_seq, n_chunks),
        grid=(nb // n_seq,),
        in_specs=in_specs,
        out_specs=[pl.BlockSpec((rows, V_R), lambda b: (b, 0)), st_spec, st_spec],
        out_shape=[jax.ShapeDtypeStruct((nb * seq, V_R), bf16),
                   jax.ShapeDtypeStruct((nb, NH_R, DK_R, DV_R), f32),
                   jax.ShapeDtypeStruct((nb, NH_R, DK_R, DV_R), f32)],
        scratch_shapes=[pltpu.VMEM((n_seq * n_chunks, NH_R, 2 * DK_R, DV_R), f32),
                        pltpu.VMEM((n_seq * n_chunks, NH_R, 2 * DK_R, DV_R), f32),
                        pltpu.VMEM((NH_R, CHUNK, CHUNK), f32), pltpu.VMEM((CHUNK, NH_R * LANES), f32),
                        pltpu.VMEM((CHUNK, NH_R * LANES), f32), pltpu.VMEM((NH_R, 2 * DK_R, DV_R), f32)],
        compiler_params=pltpu.CompilerParams(
            dimension_semantics=("parallel",), vmem_limit_bytes=V7X_VMEM_LIMIT),
        name="retention_lat" if has_s0 else "retention_ctx",
    )(*args)


def _post_kernel(l, xc_ref, xl_ref, oac_ref, oal_ref, yrc_ref, yrl_ref, ga_ref, gb_ref, wpa_hbm, wpb_hbm,
                 wo_hbm, mod_ref, n2_ref, wr_ref, br_ref, tri_ref, x1_ref, h2_ref, route_ref, cnt_ref,
                 carry_ref, wpa_ref, wpb_ref, wo_ref, stage_ref, sems):
    i = pl.program_id(0)
    tm = xc_ref.shape[0]
    n_ctx = T_CTX // tm

    @pl.when(i == 0)
    def _():
        carry_ref[...] = jnp.zeros_like(carry_ref)
        _load_cast(wpa_hbm.at[l], wpa_ref, stage_ref, sems)
        _load_cast(wpb_hbm.at[l], wpb_ref, stage_ref, sems)
        _load_cast(wo_hbm.at[l], wo_ref, stage_ref, sems)

    for r0 in range(0, tm, POST_SUB):
        _post_subtile(i < n_ctx, slice(r0, r0 + POST_SUB), xc_ref, xl_ref, oac_ref, oal_ref, yrc_ref,
                      yrl_ref, ga_ref, gb_ref, mod_ref, n2_ref, wr_ref, br_ref, tri_ref, x1_ref, h2_ref,
                      route_ref, carry_ref, wpa_ref, wpb_ref, wo_ref)
    cnt_ref[...] = carry_ref[...]


def _post_subtile(is_ctx, rows, xc_ref, xl_ref, oac_ref, oal_ref, yrc_ref, yrl_ref, ga_ref, gb_ref, mod_ref,
                  n2_ref, wr_ref, br_ref, tri_ref, x1_ref, h2_ref, route_ref, carry_ref, wpa_ref, wpb_ref,
                  wo_ref):
    tm = rows.stop - rows.start
    pick = lambda a_ref, b_ref: jnp.where(is_ctx, a_ref[rows, :], b_ref[rows, :])
    ga = _sigmoid(ga_ref[rows, :].astype(f32))
    gb = _sigmoid(gb_ref[rows, :].astype(f32))
    merged = (ga * _dot(pick(oac_ref, oal_ref), wpa_ref[...])
              + gb * _dot(pick(yrc_ref, yrl_ref), wpb_ref[...]))
    mix = _dot(merged.astype(bf16), wo_ref[...])
    x1 = pick(xc_ref, xl_ref) + mod_ref[2:3, :] * mix
    x1_ref[rows, :] = x1
    y = x1 * lax.rsqrt(jnp.mean(x1 * x1, axis=-1, keepdims=True) + EPS) * n2_ref[...]
    h2 = y * (1.0 + mod_ref[4:5, :]) + mod_ref[3:4, :]
    h2_ref[rows, 0:D] = h2

    h_hi, h_lo = _split(h2)
    both = _dot(h_hi, wr_ref[...])
    logits = both[:, 0:LANES] + (both[:, LANES:2 * LANES] + _dot(h_lo, wr_ref[:, 0:LANES]))
    lt = logits.T[0:N_EXP, :]
    scores = _sigmoid(lt)
    sel = scores + br_ref[:, 0:1]
    row = lax.broadcasted_iota(jnp.int32, (N_EXP, tm), 0)

    best = None
    bg = None
    for g in range(N_GROUPS):
        a, b, c, d = (sel[EXP_PER_GROUP * g + k:EXP_PER_GROUP * g + k + 1, :] for k in range(4))
        p, q = jnp.maximum(a, b), jnp.minimum(a, b)
        r, s = jnp.maximum(c, d), jnp.minimum(c, d)
        gs = jnp.maximum(p, r) + jnp.maximum(jnp.minimum(p, r), jnp.maximum(q, s))
        if g == 0:
            best, bg = gs, jnp.zeros((1, tm), jnp.int32)
        else:
            upd = gs > best
            bg = jnp.where(upd, g, bg)
            best = jnp.where(upd, gs, best)
    masked = jnp.where(jnp.right_shift(row, 2) == bg, sel, NEG_INF)
    m1 = jnp.max(masked, axis=0, keepdims=True)
    i1 = jnp.min(jnp.where(masked == m1, row, N_EXP), axis=0, keepdims=True)
    masked2 = jnp.where(row == i1, NEG_INF, masked)
    m2 = jnp.max(masked2, axis=0, keepdims=True)
    i2 = jnp.min(jnp.where(masked2 == m2, row, N_EXP), axis=0, keepdims=True)
    oh1 = row == i1
    oh2 = row == i2
    s1 = jnp.sum(jnp.where(oh1, scores, 0.0), axis=0, keepdims=True)
    s2 = jnp.sum(jnp.where(oh2, scores, 0.0), axis=0, keepdims=True)
    den = s1 + s2

    comb = jnp.where(oh1, s1 / den, 0.0) + jnp.where(oh2, s2 / den, 0.0)
    comb_t = jnp.concatenate([comb, jnp.zeros((LANES - N_EXP, tm), f32)], axis=0).T
    h2_ref[rows, D:D + LANES] = comb_t

    e_lo = jnp.minimum(i1, i2) - EXP_PER_GROUP * bg
    e_hi = jnp.maximum(i1, i2) - EXP_PER_GROUP * bg
    pair = jnp.right_shift(e_lo * (7 - e_lo), 1) + (e_hi - e_lo - 1)
    bucket = bg * PAIRS + pair
    rowb = lax.broadcasted_iota(jnp.int32, (BUCKET_ROWS, tm), 0)
    ohb = rowb == bucket
    ohb_f = jnp.where(ohb, 1.0, 0.0)
    tot = carry_ref[:, 0:1] + _dot(ohb_f.astype(bf16), tri_ref[...])
    rank = jnp.sum(jnp.where(ohb, tot, 0.0), axis=0, keepdims=True)
    carry_ref[...] = carry_ref[...] + jnp.sum(ohb_f, axis=1, keepdims=True)

    route_ref[0:1, rows] = bucket.astype(f32)
    route_ref[1:2, rows] = rank
    route_ref[2:8, rows] = jnp.zeros((6, tm), f32)


def _post(l, x_ctx, x_lat, oa_ctx, oa_lat, yr_ctx, yr_lat, ga, gb, wpa, wpb, wo, mod, norm2_w, wr_pad,
          br_col, tri):
    tm = TM_POST
    row = lambda w: pl.BlockSpec((tm, w), lambda i: (i, 0))
    const = lambda a: pl.BlockSpec(a.shape, lambda i: (0,) * a.ndim)
    hbm = pl.BlockSpec(memory_space=pl.ANY)
    return pl.pallas_call(
        functools.partial(_post_kernel, l),
        grid=(T // tm,),
        in_specs=(_two_stream_specs(tm, D) + _two_stream_specs(tm, Q_A) + _two_stream_specs(tm, V_R)
                  + [row(D), row(D), hbm, hbm, hbm,
                     pl.BlockSpec((None, None, 6, D), lambda i: (l, _cond_of_tile(i, tm), 0, 0)),
                     pl.BlockSpec((1, D), lambda i: (0, 0)), const(wr_pad), const(br_col), const(tri)]),
        out_specs=[row(D), row(MOE_W), pl.BlockSpec((8, tm), lambda i: (0, i)),
                   pl.BlockSpec((BUCKET_ROWS, LANES), lambda i: (0, 0))],
        out_shape=[jax.ShapeDtypeStruct((T, D), f32), jax.ShapeDtypeStruct((T, MOE_W), f32),
                   jax.ShapeDtypeStruct((8, T), f32), jax.ShapeDtypeStruct((BUCKET_ROWS, LANES), f32)],
        scratch_shapes=[pltpu.VMEM((BUCKET_ROWS, LANES), f32), pltpu.VMEM((Q_A, D), bf16),
                        pltpu.VMEM((V_R, D), bf16), pltpu.VMEM((D, D), bf16),
                        pltpu.VMEM((2, D, W_CHUNK), f32), pltpu.SemaphoreType.DMA((2,))],
        compiler_params=pltpu.CompilerParams(
            dimension_semantics=("arbitrary",), vmem_limit_bytes=V7X_VMEM_LIMIT),
        name="post_router",
    )(x_ctx, x_lat, oa_ctx, oa_lat, yr_ctx, yr_lat, ga, gb, wpa, wpb, wo, mod, norm2_w[l:l + 1],
      wr_pad, br_col, tri)


def _row_copy(src_ref, src_row, dst_ref, dst_row, sem):
    return pltpu.make_async_copy(src_ref.at[pl.ds(src_row, 1)], dst_ref.at[pl.ds(dst_row, 1)], sem)


def _dispatch_kernel(zf_ref, pos_ref, h_ref, xs_ref, zero_ref, sem):
    tm = h_ref.shape[0]

    @pl.when(pl.program_id(0) == 0)
    def _():
        zero_ref[...] = jnp.zeros_like(zero_ref)

        def tile_copy(t):
            return pltpu.make_async_copy(zero_ref, xs_ref.at[pl.ds(t * MOE_TR, MOE_TR)], sem)

        def start(t, carry):
            @pl.when(zf_ref[t] != 0)
            def _():
                tile_copy(t).start()
            return carry

        def wait(t, carry):
            @pl.when(zf_ref[t] != 0)
            def _():
                tile_copy(t).wait()
            return carry

        lax.fori_loop(0, MOE_TILES, start, 0)
        lax.fori_loop(0, MOE_TILES, wait, 0)

    for r in range(tm):
        _row_copy(h_ref, r, xs_ref, pos_ref[0, 0, r], sem).start(priority=r % 2)
    pltpu.make_async_copy(h_ref, xs_ref.at[pl.ds(0, tm)], sem).wait()


def _dispatch(zero_flags, pos_blocks, h2):
    tm = TM_DISP
    return pl.pallas_call(
        _dispatch_kernel,
        grid_spec=pltpu.PrefetchScalarGridSpec(
            num_scalar_prefetch=1,
            grid=(T // tm,),
            in_specs=[pl.BlockSpec((1, 1, tm), lambda i, zf: (i, 0, 0), memory_space=pltpu.SMEM),
                      pl.BlockSpec((tm, MOE_W), lambda i, zf: (i, 0))],
            out_specs=pl.BlockSpec(memory_space=pl.ANY),
            scratch_shapes=[pltpu.VMEM((MOE_TR, MOE_W), f32), pltpu.SemaphoreType.DMA(())],
        ),
        out_shape=jax.ShapeDtypeStruct((MOE_ROWS, MOE_W), f32),
        compiler_params=pltpu.CompilerParams(
            dimension_semantics=("arbitrary",), vmem_limit_bytes=V7X_VMEM_LIMIT),
        name="moe_dispatch",
    )(zero_flags, pos_blocks, h2)


def _experts_kernel(l, tg_ref, ta_ref, tb_ref, first_ref, nxt_ref, nt_ref, xs_ref, wg_hbm, wu_hbm, wd_hbm,
                    y_ref, wg_ref, wu_ref, wd_ref, sg_ref, su_ref, sd_ref, sems):
    i = pl.program_id(0)
    live = i < nt_ref[0]
    g = tg_ref[i]

    def fetch(grp):
        return [pltpu.make_async_copy(w.at[l, pl.ds(grp * EXP_PER_GROUP, EXP_PER_GROUP)], s, sems.at[k])
                for k, (w, s) in enumerate(((wg_hbm, sg_ref), (wu_hbm, su_ref), (wd_hbm, sd_ref)))]

    @pl.when(i == 0)
    def _():
        for c in fetch(tg_ref[0]):
            c.start()

    @pl.when(live & (first_ref[i] != 0))
    def _():
        for c in fetch(g):
            c.wait()
        for j in range(EXP_PER_GROUP):
            wg_ref[j] = sg_ref[j].astype(bf16)
            wu_ref[j] = su_ref[j].astype(bf16)
            wd_ref[j] = sd_ref[j].astype(bf16)

        @pl.when(nxt_ref[i] >= 0)
        def _():
            for c in fetch(nxt_ref[i]):
                c.start(priority=1)

    @pl.when(live)
    def _():
        x = xs_ref[:, 0:D].astype(bf16)
        comb = xs_ref[:, D:D + LANES]
        lane = lax.broadcasted_iota(jnp.int32, comb.shape, 1)
        acc = None
        for j in (ta_ref[i], tb_ref[i]):
            w_j = jnp.sum(jnp.where(lane == g * EXP_PER_GROUP + j, comb, 0.0), axis=1, keepdims=True)
            act = _silu(_dot(x, wg_ref[j])) * _dot(x, wu_ref[j]) * w_j
            part = _dot(act.astype(bf16), wd_ref[j])
            acc = part if acc is None else acc + part
        y_ref[...] = acc

    @pl.when(i >= nt_ref[0])
    def _():
        y_ref[...] = jnp.zeros_like(y_ref)


def _experts(l, plan, xs, wg, wu, wd):
    def tile(i, *prefetch):
        return jnp.minimum(i, prefetch[-1][0] - 1)

    hbm = pl.BlockSpec(memory_space=pl.ANY)
    return pl.pallas_call(
        functools.partial(_experts_kernel, l),
        grid_spec=pltpu.PrefetchScalarGridSpec(
            num_scalar_prefetch=6,
            grid=(MOE_TILES,),
            in_specs=[pl.BlockSpec((MOE_TR, MOE_W), lambda i, *p: (tile(i, *p), 0)), hbm, hbm, hbm],
            out_specs=pl.BlockSpec((MOE_TR, D), lambda i, *p: (i, 0)),
            scratch_shapes=[pltpu.VMEM((EXP_PER_GROUP, D, D_EXP), bf16),
                            pltpu.VMEM((EXP_PER_GROUP, D, D_EXP), bf16),
                            pltpu.VMEM((EXP_PER_GROUP, D_EXP, D), bf16),
                            pltpu.VMEM((EXP_PER_GROUP, D, D_EXP), f32),
                            pltpu.VMEM((EXP_PER_GROUP, D, D_EXP), f32),
                            pltpu.VMEM((EXP_PER_GROUP, D_EXP, D), f32), pltpu.SemaphoreType.DMA((3,))],
        ),
        out_shape=jax.ShapeDtypeStruct((MOE_ROWS, D), f32),
        compiler_params=pltpu.CompilerParams(
            dimension_semantics=("arbitrary",), vmem_limit_bytes=V7X_VMEM_LIMIT),
        name="moe_experts",
    )(*plan, xs, wg, wu, wd)


def _combine_kernel(pos_ref, posn_ref, x1_ref, mod_ref, y_ref, oc_ref, ol_ref, ya_ref, yb_ref, sems):
    tm = x1_ref.shape[0]
    i = pl.program_id(0)
    n_ctx = T_CTX // tm

    def pull(p_ref, buf, sem):
        for r in range(tm):
            _row_copy(y_ref, p_ref[0, 0, r], buf, r, sem).start(priority=r % 2)

    @pl.when(i == 0)
    def _():
        pull(pos_ref, ya_ref, sems.at[0])

    def step(cur, nxt, sem_cur, sem_nxt):
        @pl.when(i + 1 < pl.num_programs(0))
        def _():
            pull(posn_ref, nxt, sem_nxt)

        pltpu.make_async_copy(y_ref.at[pl.ds(0, tm)], cur, sem_cur).wait()
        out = x1_ref[...] + mod_ref[5:6, :] * cur[...]

        @pl.when(i < n_ctx)
        def _():
            oc_ref[...] = out

        @pl.when(i >= n_ctx)
        def _():
            ol_ref[...] = out

    @pl.when(i % 2 == 0)
    def _():
        step(ya_ref, yb_ref, sems.at[0], sems.at[1])

    @pl.when(i % 2 == 1)
    def _():
        step(yb_ref, ya_ref, sems.at[1], sems.at[0])


def _combine(l, pos_blocks, x1, mod, y):
    tm = TM_DISP
    n = T // tm
    return pl.pallas_call(
        _combine_kernel,
        grid=(n,),
        in_specs=[pl.BlockSpec((1, 1, tm), lambda i: (i, 0, 0), memory_space=pltpu.SMEM),
                  pl.BlockSpec((1, 1, tm), lambda i: (jnp.minimum(i + 1, n - 1), 0, 0),
                               memory_space=pltpu.SMEM),
                  pl.BlockSpec((tm, D), lambda i: (i, 0)),
                  pl.BlockSpec((None, None, 6, D), lambda i: (l, _cond_of_tile(i, tm), 0, 0)),
                  pl.BlockSpec(memory_space=pl.ANY)],
        out_specs=_two_stream_specs(tm, D),
        out_shape=[jax.ShapeDtypeStruct((T_CTX, D), f32), jax.ShapeDtypeStruct((T_LAT, D), f32)],
        scratch_shapes=[pltpu.VMEM((tm, D), f32), pltpu.VMEM((tm, D), f32),
                        pltpu.SemaphoreType.DMA((2,))],
        compiler_params=pltpu.CompilerParams(
            dimension_semantics=("arbitrary",), vmem_limit_bytes=V7X_VMEM_LIMIT),
        name="moe_combine",
    )(pos_blocks, pos_blocks, x1, mod, y)


def _plan_kernel(cnt_ref, off_ref, tg_ref, ta_ref, tb_ref, first_ref, nxt_ref, nt_ref, zf_ref, nxtg_ref):
    def bucket(b, start):
        n = lax.shift_right_logical(cnt_ref[b] + (MOE_TR - 1), MOE_TR.bit_length() - 1)
        off_ref[b] = start * MOE_TR
        g = b // PAIRS
        p = b - g * PAIRS
        lo = jnp.where(p >= 3, 1, 0) + jnp.where(p >= 5, 1, 0)
        hi = p + 1 - jnp.where(lo >= 1, 2, 0) - jnp.where(lo >= 2, 1, 0)

        def tile(t, c):
            tg_ref[t] = g
            ta_ref[t] = lo
            tb_ref[t] = hi
            zf_ref[t] = jnp.where(t == start + n - 1, 1, 0)
            return c

        lax.fori_loop(start, start + n, tile, 0)
        return start + n

    nt = lax.fori_loop(0, N_BUCKETS, bucket, jnp.int32(0))
    nt_ref[0] = nt

    def unused(t, c):
        tg_ref[t] = N_GROUPS - 1
        ta_ref[t] = 0
        tb_ref[t] = 1
        zf_ref[t] = 1
        first_ref[t] = 0
        nxt_ref[t] = -1
        return c

    lax.fori_loop(nt, MOE_TILES, unused, 0)

    def back(i, nxt_group):
        t = nt - 1 - i
        g = tg_ref[t]
        later = jnp.where(t + 1 < nt, tg_ref[jnp.minimum(t + 1, MOE_TILES - 1)], -1)
        is_last_of_group = later != g
        nxt_group = jnp.where(is_last_of_group, later, nxt_group)
        nxtg_ref[g] = nxt_group
        return nxt_group

    lax.fori_loop(0, nt, back, jnp.int32(-1))

    def fwd(t, c):
        g = tg_ref[t]
        first_ref[t] = jnp.where((t == 0) | (tg_ref[jnp.maximum(t - 1, 0)] != g), 1, 0)
        nxt_ref[t] = nxtg_ref[g]
        return c

    lax.fori_loop(0, nt, fwd, 0)


def _moe_plan(route, counts):
    smem = pl.BlockSpec(memory_space=pltpu.SMEM)
    tile_i32 = jax.ShapeDtypeStruct((MOE_TILES,), jnp.int32)
    off, tg, ta, tb, first, nxt, n_tiles, zero_flags = pl.pallas_call(
        _plan_kernel,
        in_specs=[smem],
        out_specs=[smem] * 8,
        out_shape=[jax.ShapeDtypeStruct((N_BUCKETS,), jnp.int32), tile_i32, tile_i32, tile_i32, tile_i32,
                   tile_i32, jax.ShapeDtypeStruct((1,), jnp.int32), tile_i32],
        scratch_shapes=[pltpu.SMEM((N_GROUPS,), jnp.int32)],
        name="moe_plan",
    )(counts[0:N_BUCKETS, 0].astype(jnp.int32))
    buckets = jnp.arange(N_BUCKETS, dtype=f32)[:, None]
    pos = jnp.sum(jnp.where(route[0][None, :] == buckets, off.astype(f32)[:, None], 0.0), axis=0) + route[1]
    pos_blocks = pos.astype(jnp.int32).reshape(T // TM_DISP, 1, TM_DISP)
    return pos_blocks, zero_flags, (tg, ta, tb, first, nxt, n_tiles)


def _rope_tables():
    pos = np.arange(DEC_SEQ)
    half = HD_A // 4
    freqs = ROPE_BASE ** (-np.arange(half, dtype=np.float64) / half)
    ang_r = (pos // GRID_W)[:, None] * freqs[None, :]
    ang_c = (pos % GRID_W)[:, None] * freqs[None, :]
    ang = np.concatenate([ang_r, ang_r, ang_c, ang_c], axis=1)
    sign = np.concatenate([-np.ones(half), np.ones(half)] * 2)[None, :]
    cos = np.tile(np.cos(ang), (1, 2))
    sin = np.tile(np.sin(ang) * sign, (1, 2))
    ident_c = np.ones((TM_PROJ, LANES))
    ident_s = np.zeros((TM_PROJ, LANES))
    return (jnp.asarray(np.concatenate([cos, ident_c]), f32),
            jnp.asarray(np.concatenate([sin, ident_s]), f32))


def kernel(x_prompt, x_sample, c, cache_k, cache_v, state_ret_fwd, state_ret_bwd, c_ctx,
           norm1_w, norm2_w, w_ada, b_ada, w_in, q_norm_w, k_norm_w, attn_sink,
           ret_decay_fwd, ret_decay_bwd, ret_gn_w, w_pa, w_pb, w_o, w_router, b_router,
           w_exp_gate, w_exp_up, w_exp_down):
    x_ctx, x_lat = x_prompt.reshape(T_CTX, D), x_sample.reshape(T_LAT, D)
    cond8 = jnp.zeros((N_COND, D), f32).at[0].set(c_ctx).at[1:1 + DEC_BATCH].set(c)
    mod = _adaln(cond8, w_ada, b_ada).reshape(DEPTH, N_COND, 6, D)

    cos_t, sin_t = _rope_tables()
    blk = np.arange(Q_A) // HD_A
    ones_blk = jnp.asarray((blk[:, None] == blk[None, :]) / HD_A, bf16)
    tri = jnp.asarray(np.triu(np.ones((POST_SUB, POST_SUB)), 1), bf16)
    qnw = jnp.tile(q_norm_w, (1, NH_A))
    knw = jnp.tile(k_norm_w, (1, NKV_A))
    wr_pad = jnp.concatenate(_split(jnp.pad(w_router, ((0, 0), (0, LANES - N_EXP)))), axis=1)
    br_col = jnp.broadcast_to(b_router[:, None], (N_EXP, LANES))
    ck = cache_k.reshape(DEC_BATCH, DEPTH, PAST, KV_A)
    cv = cache_v.reshape(DEC_BATCH, DEPTH, PAST, KV_A)
    s0 = jnp.concatenate([state_ret_fwd, state_ret_bwd], axis=3)
    lg = jnp.log1p(-jnp.exp(jnp.concatenate([ret_decay_fwd, ret_decay_bwd], axis=1).astype(f32)))

    new_k, new_v, new_sf, new_sb = [], [], [], []
    for l in range(DEPTH):
        qa, ka, va, qr, kr, vr, gr, ga, gb = _inproj(l, x_ctx, x_lat, norm1_w, mod, w_in, cos_t, sin_t,
                                                     ones_blk, qnw, knw)
        oa_ctx = _ctx_attn(l, qa, ka, va, attn_sink)
        oa_lat = _lat_attn(l, qa, ka, va, ck, cv, cos_t, sin_t, attn_sink)
        yr_ctx, sf, sb = _retention(l, qr, kr, vr, gr, None, lg, ret_gn_w, BATCH, SEQ, 0,
                                    CTX_SEQS_PER_STEP)
        yr_lat, _, _ = _retention(l, qr, kr, vr, gr, s0[:, l], lg, ret_gn_w, DEC_BATCH, DEC_SEQ,
                                  T_CTX // DEC_SEQ, 1)
        x1, h2, route, counts = _post(l, x_ctx, x_lat, oa_ctx, oa_lat, yr_ctx, yr_lat, ga, gb,
                                      w_pa, w_pb, w_o, mod, norm2_w, wr_pad, br_col, tri)
        pos_blocks, zero_flags, plan = _moe_plan(route, counts)
        xs = _dispatch(zero_flags, pos_blocks, h2)
        y = _experts(l, plan, xs, w_exp_gate, w_exp_up, w_exp_down)
        x_ctx, x_lat = _combine(l, pos_blocks, x1, mod, y)

        new_k.append(ka[:T_CTX].reshape(BATCH, SEQ, NKV_A, HD_A))
        new_v.append(va[:T_CTX].reshape(BATCH, SEQ, NKV_A, HD_A))
        new_sf.append(sf)
        new_sb.append(sb)

    return (x_ctx.reshape(BATCH, SEQ, D), x_lat.reshape(DEC_BATCH, DEC_SEQ, D),
            jnp.stack(new_k, axis=1), jnp.stack(new_v, axis=1),
            jnp.stack(new_sf, axis=1), jnp.stack(new_sb, axis=1))
```

```python
import functools

import numpy as np
import jax
import jax.numpy as jnp
from jax import lax
from jax.experimental import pallas as pl
from jax.experimental.pallas import tpu as pltpu

D = 1024
BATCH, SEQ = 16, 256
DEC_BATCH, DEC_SEQ = 2, 2048
DEPTH = 2
PAST = 512
GRID_W = 64
NH_A, NKV_A, HD_A = 8, 2, 64
WINDOW = 128
NH_R, DK_R, DV_R = 4, 64, 128
CHUNK = 128
N_EXP, N_GROUPS, EXP_PER_GROUP = 16, 4, 4
D_EXP = 512
ROPE_BASE = 10000.0
EPS = 1e-6
NEG_INF = -1e30

Q_A = NH_A * HD_A
KV_A = NKV_A * HD_A
QK_R = NH_R * DK_R
V_R = NH_R * DV_R
C_QA = (0, Q_A)
C_KA = (C_QA[1], C_QA[1] + KV_A)
C_VA = (C_KA[1], C_KA[1] + KV_A)
C_QR = (C_VA[1], C_VA[1] + QK_R)
C_KR = (C_QR[1], C_QR[1] + QK_R)
C_VR = (C_KR[1], C_KR[1] + V_R)
C_GR = (C_VR[1], C_VR[1] + V_R)
C_GA = (C_GR[1], C_GR[1] + D)
C_GB = (C_GA[1], C_GA[1] + D)
D_IN = C_GB[1]

T_CTX = BATCH * SEQ
T_LAT = DEC_BATCH * DEC_SEQ
T = T_CTX + T_LAT
N_COND = 8

LANES = 128
MXU_TILE = 256
V7X_VMEM_LIMIT = 56 * 1024 * 1024

TM_PROJ = 512
TM_POST = 512
POST_SUB = 256
W_CHUNK = 256
TM_DISP = 512
MOE_TR = 256
PAIRS = EXP_PER_GROUP * (EXP_PER_GROUP - 1) // 2
N_BUCKETS = N_GROUPS * PAIRS
BUCKET_ROWS = 32
MOE_TILES = T // MOE_TR + N_BUCKETS
MOE_ROWS = MOE_TILES * MOE_TR
MOE_W = D + LANES
ATT_QB = 256
RET_UNROLL = 4
CTX_SEQS_PER_STEP = 4

f32 = jnp.float32
bf16 = jnp.bfloat16


def _dot(a, b):
    return jnp.dot(a, b, preferred_element_type=f32)


def _dot_t(a, b):
    return lax.dot_general(a, b, (((1,), (1,)), ((), ())), preferred_element_type=f32)


def _split(x):
    hi = x.astype(bf16)
    lo = (x - hi.astype(f32)).astype(bf16)
    return hi, lo


def _dot3(a, b):
    ah, al = _split(a)
    bh, bl = _split(b)
    return _dot(ah, bh) + (_dot(ah, bl) + _dot(al, bh))


def _sigmoid(x):
    return 1.0 / (1.0 + jnp.exp(-x))


def _silu(x):
    return x * _sigmoid(x)


def _cond_of_tile(i, tm):
    n_ctx = T_CTX // tm
    per_b = DEC_SEQ // tm
    return jnp.where(i < n_ctx, 0, 1 + jnp.maximum(i - n_ctx, 0) // per_b)


def _adaln_kernel(cond_ref, w_ref, b_ref, o_ref):
    a = _silu(cond_ref[...])
    o_ref[0] = _dot3(a, w_ref[0]) + b_ref[0]


def _adaln(cond8, w_ada, b_ada):
    tn = 2048
    return pl.pallas_call(
        _adaln_kernel,
        grid=(DEPTH, 6 * D // tn),
        in_specs=[
            pl.BlockSpec((N_COND, D), lambda l, j: (0, 0)),
            pl.BlockSpec((1, D, tn), lambda l, j: (l, 0, j)),
            pl.BlockSpec((1, 1, tn), lambda l, j: (l, 0, j)),
        ],
        out_specs=pl.BlockSpec((1, N_COND, tn), lambda l, j: (l, 0, j)),
        out_shape=jax.ShapeDtypeStruct((DEPTH, N_COND, 6 * D), f32),
        compiler_params=pltpu.CompilerParams(
            dimension_semantics=("parallel", "parallel"), vmem_limit_bytes=V7X_VMEM_LIMIT),
        name="adaln",
    )(cond8, w_ada, b_ada.reshape(DEPTH, 1, 6 * D))


def _rope(x, cos, sin_signed, first_half):
    fwd = pltpu.roll(x, 16, 1)
    bwd = pltpu.roll(x, LANES - 16, 1)
    partner = jnp.where(first_half, bwd, fwd)
    return x * cos + partner * sin_signed


def _head_rms(x, ones_blk, w):
    n = x.shape[1]
    sq_hi, sq_lo = _split(x * x)
    wt = min(n, MXU_TILE)
    blk = ones_blk[0:wt, 0:wt]
    parts = [_dot(sq_hi[:, c:c + wt], blk) + _dot(sq_lo[:, c:c + wt], blk) for c in range(0, n, wt)]
    mean = parts[0] if len(parts) == 1 else jnp.concatenate(parts, axis=1)
    return x * lax.rsqrt(mean + EPS) * w


def _load_cast(w_hbm, dst_ref, stage_ref, sems):
    k, n = w_hbm.shape

    def chunk_copy(c):
        return pltpu.make_async_copy(w_hbm.at[:, pl.ds(c * W_CHUNK, W_CHUNK)],
                                     stage_ref.at[c % 2, pl.ds(0, k)], sems.at[c % 2])

    n_chunks = n // W_CHUNK
    chunk_copy(0).start()
    for c in range(n_chunks):
        if c + 1 < n_chunks:
            chunk_copy(c + 1).start()
        chunk_copy(c).wait()
        dst_ref[:, c * W_CHUNK:(c + 1) * W_CHUNK] = stage_ref[c % 2, 0:k, :].astype(bf16)


def _tile_of_two(i, n_first, a_ref, b_ref):
    return jnp.where(i < n_first, a_ref[...], b_ref[...])


def _inproj_kernel(l, xc_ref, xl_ref, n1_ref, mod_ref, w_hbm, cos_ref, sin_ref, ones_ref, qnw_ref,
                   knw_ref, qa_ref, ka_ref, va_ref, qr_ref, kr_ref, vr_ref, gr_ref, ga_ref, gb_ref,
                   w_ref, stage_ref, sems):
    i = pl.program_id(0)

    @pl.when(i == 0)
    def _():
        _load_cast(w_hbm.at[l], w_ref, stage_ref, sems)

    x = _tile_of_two(i, T_CTX // xc_ref.shape[0], xc_ref, xl_ref)
    y = x * lax.rsqrt(jnp.mean(x * x, axis=-1, keepdims=True) + EPS) * n1_ref[...]
    h = (y * (1.0 + mod_ref[1:2, :]) + mod_ref[0:1, :]).astype(bf16)

    whole = _dot(h, w_ref[...])

    def proj(c):
        return whole[:, c[0]:c[1]]

    cos = cos_ref[...]
    sin = sin_ref[...]
    lane = lax.broadcasted_iota(jnp.int32, cos.shape, 1)
    first_half = (lane % 32) < 16

    def rope_all(v):
        parts = [_rope(v[:, j:j + LANES], cos, sin, first_half) for j in range(0, v.shape[1], LANES)]
        return parts[0] if len(parts) == 1 else jnp.concatenate(parts, axis=1)

    ones_blk = ones_ref[...]
    qa = _head_rms(proj(C_QA), ones_blk, qnw_ref[...])
    qa_ref[...] = rope_all(qa).astype(bf16)
    ka_ref[...] = _head_rms(proj(C_KA), ones_blk, knw_ref[...])
    va_ref[...] = proj(C_VA)
    qr_ref[...] = rope_all(proj(C_QR)).astype(bf16)
    kr_ref[...] = (rope_all(proj(C_KR)) * (DK_R ** -0.5)).astype(bf16)
    vr_ref[...] = proj(C_VR).astype(bf16)
    gr_ref[...] = proj(C_GR).astype(bf16)
    ga_ref[...] = proj(C_GA).astype(bf16)
    gb_ref[...] = proj(C_GB).astype(bf16)


def _two_stream_specs(tm, width):
    n_ctx = T_CTX // tm
    return [pl.BlockSpec((tm, width), lambda i, *_: (jnp.minimum(i, n_ctx - 1), 0)),
            pl.BlockSpec((tm, width), lambda i, *_: (jnp.maximum(i - n_ctx, 0), 0))]


def _inproj(l, x_ctx, x_lat, norm1_w, mod, w_in, cos_t, sin_t, ones_blk, qnw, knw):
    tm = TM_PROJ
    n_ctx = T_CTX // tm
    per_b = DEC_SEQ // tm

    def tab_map(i):
        return (jnp.where(i < n_ctx, per_b, jnp.maximum(i - n_ctx, 0) % per_b), 0)

    row = lambda i: (i, 0)
    const = lambda i: (0, 0)
    widths = [(Q_A, bf16), (KV_A, f32), (KV_A, f32), (QK_R, bf16), (QK_R, bf16), (V_R, bf16),
              (V_R, bf16), (D, bf16), (D, bf16)]
    return pl.pallas_call(
        functools.partial(_inproj_kernel, l),
        grid=(T // tm,),
        in_specs=_two_stream_specs(tm, D) + [
            pl.BlockSpec((1, D), const),
            pl.BlockSpec((None, None, 6, D), lambda i: (l, _cond_of_tile(i, tm), 0, 0)),
            pl.BlockSpec(memory_space=pl.ANY),
            pl.BlockSpec((tm, LANES), tab_map),
            pl.BlockSpec((tm, LANES), tab_map),
            pl.BlockSpec((Q_A, Q_A), const),
            pl.BlockSpec((1, Q_A), const),
            pl.BlockSpec((1, KV_A), const),
        ],
        out_specs=[pl.BlockSpec((tm, w), row) for w, _ in widths],
        out_shape=[jax.ShapeDtypeStruct((T, w), dt) for w, dt in widths],
        scratch_shapes=[pltpu.VMEM((D, D_IN), bf16), pltpu.VMEM((2, D, W_CHUNK), f32),
                        pltpu.SemaphoreType.DMA((2,))],
        compiler_params=pltpu.CompilerParams(
            dimension_semantics=("arbitrary",), vmem_limit_bytes=V7X_VMEM_LIMIT),
        name="inproj",
    )(x_ctx, x_lat, norm1_w[l:l + 1], mod, w_in, cos_t, sin_t, ones_blk, qnw[l:l + 1], knw[l:l + 1])


def _head_blocks(t, kv, lo_mask):
    r = pltpu.roll(t, HD_A, 1)
    if kv == 0:
        a = jnp.where(lo_mask, t, 0.0)
        b = jnp.where(lo_mask, 0.0, r)
    else:
        a = jnp.where(lo_mask, r, 0.0)
        b = jnp.where(lo_mask, 0.0, t)
    return jnp.concatenate([a, b], axis=0).astype(bf16)


def _ctx_attn_kernel(sink_ref, q_ref, k_ref, v_ref, o_ref):
    n = SEQ
    lo_mask = lax.broadcasted_iota(jnp.int32, (n, LANES), 1) < HD_A
    for b in range(CTX_SEQS_PER_STEP):
        rows = slice(b * n, (b + 1) * n)
        k = k_ref[rows, :]
        v = v_ref[rows, :]
        for kv in range(NKV_A):
            kblk = _head_blocks(k, kv, lo_mask)
            vblk = _head_blocks(v, kv, lo_mask)
            for pr in range(2):
                pi = kv * 2 + pr
                s = _dot_t(q_ref[rows, pi * LANES:(pi + 1) * LANES], kblk)
                ps, invs = [], []
                for hh in range(2):
                    sk = sink_ref[2 * pi + hh]
                    sh = s[:, hh * n:(hh + 1) * n]
                    m = jnp.maximum(jnp.max(sh, axis=-1, keepdims=True), sk)
                    p = jnp.exp(sh - m)
                    invs.append(1.0 / (jnp.sum(p, axis=-1, keepdims=True) + jnp.exp(sk - m)))
                    ps.append(p.astype(bf16))
                o = _dot(jnp.concatenate(ps, axis=1), vblk) * jnp.where(lo_mask, invs[0], invs[1])
                o_ref[rows, pi * LANES:(pi + 1) * LANES] = o.astype(bf16)


def _ctx_attn(l, qa, ka, va, sink):
    blk = lambda w: pl.BlockSpec((CTX_SEQS_PER_STEP * SEQ, w), lambda b: (b, 0))
    return pl.pallas_call(
        _ctx_attn_kernel,
        grid=(BATCH // CTX_SEQS_PER_STEP,),
        in_specs=[pl.BlockSpec(memory_space=pltpu.SMEM), blk(Q_A), blk(KV_A), blk(KV_A)],
        out_specs=blk(Q_A),
        out_shape=jax.ShapeDtypeStruct((T_CTX, Q_A), bf16),
        compiler_params=pltpu.CompilerParams(
            dimension_semantics=("parallel",), vmem_limit_bytes=V7X_VMEM_LIMIT),
        name="ctx_attn",
    )(sink[l], qa, ka, va)


def _lat_attn_kernel(sink_ref, q_ref, k_ref, v_ref, kc_ref, vc_ref, cos_ref, sin_ref, o_ref):
    j = pl.program_id(1)
    qb = ATT_QB
    win = 2 * qb
    ws = pl.multiple_of(jnp.clip(j * qb - WINDOW, 0, DEC_SEQ - win), WINDOW)
    lo_mask = lax.broadcasted_iota(jnp.int32, (win, LANES), 1) < HD_A
    lane = lax.broadcasted_iota(jnp.int32, (win, LANES), 1)
    kw = _rope(k_ref[pl.ds(ws, win), :], cos_ref[pl.ds(ws, win), :], sin_ref[pl.ds(ws, win), :],
               (lane % 32) < 16)
    vw = v_ref[pl.ds(ws, win), :]
    kc = kc_ref[...]
    vc = vc_ref[...]
    qpos = j * qb + (lax.broadcasted_iota(jnp.int32, (2 * qb, win), 0) & (qb - 1))
    kpos = ws + lax.broadcasted_iota(jnp.int32, (2 * qb, win), 1)
    valid = jnp.abs(qpos - kpos) <= WINDOW
    out_lo = lax.broadcasted_iota(jnp.int32, (2 * qb, LANES), 1) < HD_A
    for kv in range(NKV_A):
        kc_blk = _head_blocks(kc, kv, lo_mask[:PAST])
        vc_blk = _head_blocks(vc, kv, lo_mask[:PAST])
        kw_blk = _head_blocks(kw, kv, lo_mask)
        vw_blk = _head_blocks(vw, kv, lo_mask)
        q2 = jnp.concatenate([q_ref[:, (2 * kv) * LANES:(2 * kv + 1) * LANES],
                              q_ref[:, (2 * kv + 1) * LANES:(2 * kv + 2) * LANES]], axis=0)
        s_c = _dot_t(q2, kc_blk)
        s_w = _dot_t(q2, kw_blk)
        pcs, pws, invs = [], [], []
        for hh in range(2):
            row = lax.broadcasted_iota(jnp.int32, (2 * qb, 1), 0)
            sk = jnp.where(row < qb, sink_ref[4 * kv + hh], sink_ref[4 * kv + 2 + hh])
            sc = s_c[:, hh * PAST:(hh + 1) * PAST]
            sw = jnp.where(valid, s_w[:, hh * win:(hh + 1) * win], NEG_INF)
            m = jnp.maximum(jnp.maximum(jnp.max(sc, axis=-1, keepdims=True),
                                        jnp.max(sw, axis=-1, keepdims=True)), sk)
            pc = jnp.exp(sc - m)
            pw = jnp.exp(sw - m)
            den = (jnp.sum(pc, axis=-1, keepdims=True) + jnp.sum(pw, axis=-1, keepdims=True)
                   + jnp.exp(sk - m))
            invs.append(1.0 / den)
            pcs.append(pc.astype(bf16))
            pws.append(pw.astype(bf16))
        o = _dot(jnp.concatenate(pcs, axis=1), vc_blk) + _dot(jnp.concatenate(pws, axis=1), vw_blk)
        o = o * jnp.where(out_lo, invs[0], invs[1])
        o_ref[:, (2 * kv) * LANES:(2 * kv + 1) * LANES] = o[:qb].astype(bf16)
        o_ref[:, (2 * kv + 1) * LANES:(2 * kv + 2) * LANES] = o[qb:].astype(bf16)


def _lat_attn(l, qa, ka, va, cache_k, cache_v, cos_l, sin_l, sink):
    qb = ATT_QB
    nq = DEC_SEQ // qb
    ctx_blocks = T_CTX // DEC_SEQ
    seq = lambda b, j: (ctx_blocks + b, 0)
    return pl.pallas_call(
        _lat_attn_kernel,
        grid=(DEC_BATCH, nq),
        in_specs=[
            pl.BlockSpec(memory_space=pltpu.SMEM),
            pl.BlockSpec((qb, Q_A), lambda b, j: (T_CTX // qb + b * nq + j, 0)),
            pl.BlockSpec((DEC_SEQ, KV_A), seq),
            pl.BlockSpec((DEC_SEQ, KV_A), seq),
            pl.BlockSpec((None, None, PAST, KV_A), lambda b, j: (b, l, 0, 0)),
            pl.BlockSpec((None, None, PAST, KV_A), lambda b, j: (b, l, 0, 0)),
            pl.BlockSpec((DEC_SEQ, LANES), lambda b, j: (0, 0)),
            pl.BlockSpec((DEC_SEQ, LANES), lambda b, j: (0, 0)),
        ],
        out_specs=pl.BlockSpec((qb, Q_A), lambda b, j: (b * nq + j, 0)),
        out_shape=jax.ShapeDtypeStruct((T_LAT, Q_A), bf16),
        compiler_params=pltpu.CompilerParams(
            dimension_semantics=("parallel", "parallel"), vmem_limit_bytes=V7X_VMEM_LIMIT),
        name="lat_attn",
    )(sink[l], qa, ka, va, cache_k, cache_v, cos_l, sin_l)


def _dup_heads(pair, lo_mask):
    r = pltpu.roll(pair, DK_R, 1)
    return jnp.where(lo_mask, pair, r), jnp.where(lo_mask, r, pair)


def _retention_kernel(has_s0, n_seq, n_chunks, *refs):
    if has_s0:
        (lg_ref, q_ref, k_ref, v_ref, g_ref, s0_ref, gnw_ref,
         y_ref, sf_ref, sb_ref, ds_ref, st_ref, mask_ref, qdec_ref, kdec_ref, cdec_ref) = refs
    else:
        (lg_ref, q_ref, k_ref, v_ref, g_ref, gnw_ref,
         y_ref, sf_ref, sb_ref, ds_ref, st_ref, mask_ref, qdec_ref, kdec_ref, cdec_ref) = refs
        s0_ref = None
    C = CHUNK
    lo_mask = lax.broadcasted_iota(jnp.int32, (C, LANES), 1) < DK_R

    ri = lax.broadcasted_iota(jnp.int32, (C, C), 0).astype(f32)
    diff = ri - lax.broadcasted_iota(jnp.int32, (C, C), 1).astype(f32)
    for h in range(NH_R):
        lg_f, lg_b = lg_ref[h], lg_ref[NH_R + h]
        mask_ref[h] = (jnp.where(diff >= 0, jnp.exp(jnp.maximum(diff, 0.0) * lg_f), 0.0)
                       + jnp.where(diff <= 0, jnp.exp(jnp.maximum(-diff, 0.0) * lg_b), 0.0))
        qdec_ref[:, h * LANES:(h + 1) * LANES] = jnp.exp(
            jnp.where(lo_mask, (ri + 1.0) * lg_f, (C - ri) * lg_b))
        kdec_ref[:, h * LANES:(h + 1) * LANES] = jnp.exp(
            jnp.where(lo_mask, (C - 1.0 - ri) * lg_f, ri * lg_b))
        cdec_ref[h] = jnp.exp(jnp.where(ri < DK_R, C * lg_f, C * lg_b))

    def inc_body(c, carry):
        r0 = pl.multiple_of(c * C, C)
        for pr in range(2):
            kp = k_ref[pl.ds(r0, C), pr * LANES:(pr + 1) * LANES].astype(f32)
            for hh, kd in enumerate(_dup_heads(kp, lo_mask)):
                h = 2 * pr + hh
                kd = (kd * kdec_ref[:, h * LANES:(h + 1) * LANES]).astype(bf16)
                vh = v_ref[pl.ds(r0, C), h * DV_R:(h + 1) * DV_R]
                ds_ref[c, h] = lax.dot_general(kd, vh, (((0,), (0,)), ((), ())),
                                               preferred_element_type=f32)
        return carry

    lax.fori_loop(0, n_seq * n_chunks, inc_body, 0, unroll=RET_UNROLL)

    for sq in range(n_seq):
        base = sq * n_chunks
        for h in range(NH_R):
            cf = cdec_ref[h, 0:DK_R, :]
            cb = cdec_ref[h, DK_R:2 * DK_R, :]
            if has_s0:
                init_f = s0_ref[h, 0:DK_R, :]
                init_b = s0_ref[h, DK_R:2 * DK_R, :]
            else:
                init_f = jnp.zeros((DK_R, DV_R), f32)
                init_b = init_f

            def fwd_body(i, s, h=h, cf=cf, base=base):
                c = base + i
                st_ref[c, h, 0:DK_R, :] = s
                return s * cf + ds_ref[c, h, 0:DK_R, :]

            def bwd_body(i, s, h=h, cb=cb, base=base):
                c = base + n_chunks - 1 - i
                st_ref[c, h, DK_R:2 * DK_R, :] = s
                return s * cb + ds_ref[c, h, DK_R:2 * DK_R, :]

            sf_ref[sq, h] = lax.fori_loop(0, n_chunks, fwd_body, init_f)
            sb_ref[sq, h] = lax.fori_loop(0, n_chunks, bwd_body, init_b)

    def out_body(c, carry):
        r0 = pl.multiple_of(c * C, C)
        for pr in range(2):
            qp = q_ref[pl.ds(r0, C), pr * LANES:(pr + 1) * LANES]
            kp = k_ref[pl.ds(r0, C), pr * LANES:(pr + 1) * LANES].astype(f32)
            kblk = jnp.concatenate([jnp.where(lo_mask, kp, 0.0), jnp.where(lo_mask, 0.0, kp)],
                                   axis=0).astype(bf16)
            a2 = _dot_t(qp, kblk)
            for hh, qd in enumerate(_dup_heads(qp.astype(f32), lo_mask)):
                h = 2 * pr + hh
                a = (a2[:, hh * C:(hh + 1) * C] * mask_ref[h]).astype(bf16)
                vh = v_ref[pl.ds(r0, C), h * DV_R:(h + 1) * DV_R]
                qd = (qd * qdec_ref[:, h * LANES:(h + 1) * LANES]).astype(bf16)
                o = _dot(a, vh) + _dot(qd, st_ref[c, h].astype(bf16))
                mu = jnp.mean(o, axis=-1, keepdims=True)
                d = o - mu
                var = jnp.mean(d * d, axis=-1, keepdims=True)
                yh = d * lax.rsqrt(var + EPS) * gnw_ref[:, h * DV_R:(h + 1) * DV_R]
                g = g_ref[pl.ds(r0, C), h * DV_R:(h + 1) * DV_R].astype(f32)
                y_ref[pl.ds(r0, C), h * DV_R:(h + 1) * DV_R] = (yh * _silu(g)).astype(bf16)
        return carry

    lax.fori_loop(0, n_seq * n_chunks, out_body, 0, unroll=RET_UNROLL)


def _retention(l, qr, kr, vr, gr, s0, lg, gnw, nb, seq, row_block0, n_seq):
    n_chunks = seq // CHUNK
    has_s0 = s0 is not None
    rows = n_seq * seq
    tok = lambda w: pl.BlockSpec((rows, w), lambda b: (row_block0 + b, 0))
    in_specs = [pl.BlockSpec(memory_space=pltpu.SMEM), tok(QK_R), tok(QK_R), tok(V_R), tok(V_R)]
    args = [lg[l], qr, kr, vr, gr]
    if has_s0:
        in_specs.append(pl.BlockSpec((None, NH_R, 2 * DK_R, DV_R), lambda b: (b, 0, 0, 0)))
        args.append(s0)
    in_specs += [pl.BlockSpec((1, V_R), lambda b: (0, 0))]
    args += [gnw[l:l + 1]]
    st_spec = pl.BlockSpec((n_seq, NH_R, DK_R, DV_R), lambda b: (b, 0, 0, 0))
    return pl.pallas_call(
        functools.partial(_retention_kernel, has_s0, n_seq, n_chunks),
        grid=(nb // n_seq,),
        in_specs=in_specs,
        out_specs=[pl.BlockSpec((rows, V_R), lambda b: (b, 0)), st_spec, st_spec],
        out_shape=[jax.ShapeDtypeStruct((nb * seq, V_R), bf16),
                   jax.ShapeDtypeStruct((nb, NH_R, DK_R, DV_R), f32),
                   jax.ShapeDtypeStruct((nb, NH_R, DK_R, DV_R), f32)],
        scratch_shapes=[pltpu.VMEM((n_seq * n_chunks, NH_R, 2 * DK_R, DV_R), f32),
                        pltpu.VMEM((n_seq * n_chunks, NH_R, 2 * DK_R, DV_R), f32),
                        pltpu.VMEM((NH_R, CHUNK, CHUNK), f32), pltpu.VMEM((CHUNK, NH_R * LANES), f32),
                        pltpu.VMEM((CHUNK, NH_R * LANES), f32), pltpu.VMEM((NH_R, 2 * DK_R, DV_R), f32)],
        compiler_params=pltpu.CompilerParams(
            dimension_semantics=("parallel",), vmem_limit_bytes=V7X_VMEM_LIMIT),
        name="retention_lat" if has_s0 else "retention_ctx",
    )(*args)


def _post_kernel(l, xc_ref, xl_ref, oac_ref, oal_ref, yrc_ref, yrl_ref, ga_ref, gb_ref, wpa_hbm, wpb_hbm,
                 wo_hbm, mod_ref, n2_ref, wr_ref, br_ref, tri_ref, x1_ref, h2_ref, route_ref, cnt_ref,
                 carry_ref, wpa_ref, wpb_ref, wo_ref, stage_ref, sems):
    i = pl.program_id(0)
    tm = xc_ref.shape[0]
    n_ctx = T_CTX // tm

    @pl.when(i == 0)
    def _():
        carry_ref[...] = jnp.zeros_like(carry_ref)
        _load_cast(wpa_hbm.at[l], wpa_ref, stage_ref, sems)
        _load_cast(wpb_hbm.at[l], wpb_ref, stage_ref, sems)
        _load_cast(wo_hbm.at[l], wo_ref, stage_ref, sems)

    for r0 in range(0, tm, POST_SUB):
        _post_subtile(i < n_ctx, slice(r0, r0 + POST_SUB), xc_ref, xl_ref, oac_ref, oal_ref, yrc_ref,
                      yrl_ref, ga_ref, gb_ref, mod_ref, n2_ref, wr_ref, br_ref, tri_ref, x1_ref, h2_ref,
                      route_ref, carry_ref, wpa_ref, wpb_ref, wo_ref)
    cnt_ref[...] = carry_ref[...]


def _post_subtile(is_ctx, rows, xc_ref, xl_ref, oac_ref, oal_ref, yrc_ref, yrl_ref, ga_ref, gb_ref, mod_ref,
                  n2_ref, wr_ref, br_ref, tri_ref, x1_ref, h2_ref, route_ref, carry_ref, wpa_ref, wpb_ref,
                  wo_ref):
    tm = rows.stop - rows.start
    pick = lambda a_ref, b_ref: jnp.where(is_ctx, a_ref[rows, :], b_ref[rows, :])
    ga = _sigmoid(ga_ref[rows, :].astype(f32))
    gb = _sigmoid(gb_ref[rows, :].astype(f32))
    merged = (ga * _dot(pick(oac_ref, oal_ref), wpa_ref[...])
              + gb * _dot(pick(yrc_ref, yrl_ref), wpb_ref[...]))
    mix = _dot(merged.astype(bf16), wo_ref[...])
    x1 = pick(xc_ref, xl_ref) + mod_ref[2:3, :] * mix
    x1_ref[rows, :] = x1
    y = x1 * lax.rsqrt(jnp.mean(x1 * x1, axis=-1, keepdims=True) + EPS) * n2_ref[...]
    h2 = y * (1.0 + mod_ref[4:5, :]) + mod_ref[3:4, :]
    h2_ref[rows, 0:D] = h2

    h_hi, h_lo = _split(h2)
    both = _dot(h_hi, wr_ref[...])
    logits = both[:, 0:LANES] + (both[:, LANES:2 * LANES] + _dot(h_lo, wr_ref[:, 0:LANES]))
    lt = logits.T[0:N_EXP, :]
    scores = _sigmoid(lt)
    sel = scores + br_ref[:, 0:1]
    row = lax.broadcasted_iota(jnp.int32, (N_EXP, tm), 0)

    best = None
    bg = None
    for g in range(N_GROUPS):
        a, b, c, d = (sel[EXP_PER_GROUP * g + k:EXP_PER_GROUP * g + k + 1, :] for k in range(4))
        p, q = jnp.maximum(a, b), jnp.minimum(a, b)
        r, s = jnp.maximum(c, d), jnp.minimum(c, d)
        gs = jnp.maximum(p, r) + jnp.maximum(jnp.minimum(p, r), jnp.maximum(q, s))
        if g == 0:
            best, bg = gs, jnp.zeros((1, tm), jnp.int32)
        else:
            upd = gs > best
            bg = jnp.where(upd, g, bg)
            best = jnp.where(upd, gs, best)
    masked = jnp.where(jnp.right_shift(row, 2) == bg, sel, NEG_INF)
    m1 = jnp.max(masked, axis=0, keepdims=True)
    i1 = jnp.min(jnp.where(masked == m1, row, N_EXP), axis=0, keepdims=True)
    masked2 = jnp.where(row == i1, NEG_INF, masked)
    m2 = jnp.max(masked2, axis=0, keepdims=True)
    i2 = jnp.min(jnp.where(masked2 == m2, row, N_EXP), axis=0, keepdims=True)
    oh1 = row == i1
    oh2 = row == i2
    s1 = jnp.sum(jnp.where(oh1, scores, 0.0), axis=0, keepdims=True)
    s2 = jnp.sum(jnp.where(oh2, scores, 0.0), axis=0, keepdims=True)
    den = s1 + s2

    comb = jnp.where(oh1, s1 / den, 0.0) + jnp.where(oh2, s2 / den, 0.0)
    comb_t = jnp.concatenate([comb, jnp.zeros((LANES - N_EXP, tm), f32)], axis=0).T
    h2_ref[rows, D:D + LANES] = comb_t

    e_lo = jnp.minimum(i1, i2) - EXP_PER_GROUP * bg
    e_hi = jnp.maximum(i1, i2) - EXP_PER_GROUP * bg
    pair = jnp.right_shift(e_lo * (7 - e_lo), 1) + (e_hi - e_lo - 1)
    bucket = bg * PAIRS + pair
    rowb = lax.broadcasted_iota(jnp.int32, (BUCKET_ROWS, tm), 0)
    ohb = rowb == bucket
    ohb_f = jnp.where(ohb, 1.0, 0.0)
    tot = carry_ref[:, 0:1] + _dot(ohb_f.astype(bf16), tri_ref[...])
    rank = jnp.sum(jnp.where(ohb, tot, 0.0), axis=0, keepdims=True)
    carry_ref[...] = carry_ref[...] + jnp.sum(ohb_f, axis=1, keepdims=True)

    route_ref[0:1, rows] = bucket.astype(f32)
    route_ref[1:2, rows] = rank
    route_ref[2:8, rows] = jnp.zeros((6, tm), f32)


def _post(l, x_ctx, x_lat, oa_ctx, oa_lat, yr_ctx, yr_lat, ga, gb, wpa, wpb, wo, mod, norm2_w, wr_pad,
          br_col, tri):
    tm = TM_POST
    row = lambda w: pl.BlockSpec((tm, w), lambda i: (i, 0))
    const = lambda a: pl.BlockSpec(a.shape, lambda i: (0,) * a.ndim)
    hbm = pl.BlockSpec(memory_space=pl.ANY)
    return pl.pallas_call(
        functools.partial(_post_kernel, l),
        grid=(T // tm,),
        in_specs=(_two_stream_specs(tm, D) + _two_stream_specs(tm, Q_A) + _two_stream_specs(tm, V_R)
                  + [row(D), row(D), hbm, hbm, hbm,
                     pl.BlockSpec((None, None, 6, D), lambda i: (l, _cond_of_tile(i, tm), 0, 0)),
                     pl.BlockSpec((1, D), lambda i: (0, 0)), const(wr_pad), const(br_col), const(tri)]),
        out_specs=[row(D), row(MOE_W), pl.BlockSpec((8, tm), lambda i: (0, i)),
                   pl.BlockSpec((BUCKET_ROWS, LANES), lambda i: (0, 0))],
        out_shape=[jax.ShapeDtypeStruct((T, D), f32), jax.ShapeDtypeStruct((T, MOE_W), f32),
                   jax.ShapeDtypeStruct((8, T), f32), jax.ShapeDtypeStruct((BUCKET_ROWS, LANES), f32)],
        scratch_shapes=[pltpu.VMEM((BUCKET_ROWS, LANES), f32), pltpu.VMEM((Q_A, D), bf16),
                        pltpu.VMEM((V_R, D), bf16), pltpu.VMEM((D, D), bf16),
                        pltpu.VMEM((2, D, W_CHUNK), f32), pltpu.SemaphoreType.DMA((2,))],
        compiler_params=pltpu.CompilerParams(
            dimension_semantics=("arbitrary",), vmem_limit_bytes=V7X_VMEM_LIMIT),
        name="post_router",
    )(x_ctx, x_lat, oa_ctx, oa_lat, yr_ctx, yr_lat, ga, gb, wpa, wpb, wo, mod, norm2_w[l:l + 1],
      wr_pad, br_col, tri)


def _row_copy(src_ref, src_row, dst_ref, dst_row, sem):
    return pltpu.make_async_copy(src_ref.at[pl.ds(src_row, 1)], dst_ref.at[pl.ds(dst_row, 1)], sem)


def _dispatch_kernel(zf_ref, pos_ref, h_ref, xs_ref, zero_ref, sem):
    tm = h_ref.shape[0]

    @pl.when(pl.program_id(0) == 0)
    def _():
        zero_ref[...] = jnp.zeros_like(zero_ref)

        def tile_copy(t):
            return pltpu.make_async_copy(zero_ref, xs_ref.at[pl.ds(t * MOE_TR, MOE_TR)], sem)

        def start(t, carry):
            @pl.when(zf_ref[t] != 0)
            def _():
                tile_copy(t).start()
            return carry

        def wait(t, carry):
            @pl.when(zf_ref[t] != 0)
            def _():
                tile_copy(t).wait()
            return carry

        lax.fori_loop(0, MOE_TILES, start, 0)
        lax.fori_loop(0, MOE_TILES, wait, 0)

    for r in range(tm):
        _row_copy(h_ref, r, xs_ref, pos_ref[0, 0, r], sem).start(priority=r % 2)
    pltpu.make_async_copy(h_ref, xs_ref.at[pl.ds(0, tm)], sem).wait()


def _dispatch(zero_flags, pos_blocks, h2):
    tm = TM_DISP
    return pl.pallas_call(
        _dispatch_kernel,
        grid_spec=pltpu.PrefetchScalarGridSpec(
            num_scalar_prefetch=1,
            grid=(T // tm,),
            in_specs=[pl.BlockSpec((1, 1, tm), lambda i, zf: (i, 0, 0), memory_space=pltpu.SMEM),
                      pl.BlockSpec((tm, MOE_W), lambda i, zf: (i, 0))],
            out_specs=pl.BlockSpec(memory_space=pl.ANY),
            scratch_shapes=[pltpu.VMEM((MOE_TR, MOE_W), f32), pltpu.SemaphoreType.DMA(())],
        ),
        out_shape=jax.ShapeDtypeStruct((MOE_ROWS, MOE_W), f32),
        compiler_params=pltpu.CompilerParams(
            dimension_semantics=("arbitrary",), vmem_limit_bytes=V7X_VMEM_LIMIT),
        name="moe_dispatch",
    )(zero_flags, pos_blocks, h2)


def _experts_kernel(l, tg_ref, ta_ref, tb_ref, first_ref, nxt_ref, nt_ref, xs_ref, wg_hbm, wu_hbm, wd_hbm,
                    y_ref, wg_ref, wu_ref, wd_ref, sg_ref, su_ref, sd_ref, sems):
    i = pl.program_id(0)
    live = i < nt_ref[0]
    g = tg_ref[i]

    def fetch(grp):
        return [pltpu.make_async_copy(w.at[l, pl.ds(grp * EXP_PER_GROUP, EXP_PER_GROUP)], s, sems.at[k])
                for k, (w, s) in enumerate(((wg_hbm, sg_ref), (wu_hbm, su_ref), (wd_hbm, sd_ref)))]

    @pl.when(i == 0)
    def _():
        for c in fetch(tg_ref[0]):
            c.start()

    @pl.when(live & (first_ref[i] != 0))
    def _():
        for c in fetch(g):
            c.wait()
        for j in range(EXP_PER_GROUP):
            wg_ref[j] = sg_ref[j].astype(bf16)
            wu_ref[j] = su_ref[j].astype(bf16)
            wd_ref[j] = sd_ref[j].astype(bf16)

        @pl.when(nxt_ref[i] >= 0)
        def _():
            for c in fetch(nxt_ref[i]):
                c.start(priority=1)

    @pl.when(live)
    def _():
        x = xs_ref[:, 0:D].astype(bf16)
        comb = xs_ref[:, D:D + LANES]
        lane = lax.broadcasted_iota(jnp.int32, comb.shape, 1)
        acc = None
        for j in (ta_ref[i], tb_ref[i]):
            w_j = jnp.sum(jnp.where(lane == g * EXP_PER_GROUP + j, comb, 0.0), axis=1, keepdims=True)
            act = _silu(_dot(x, wg_ref[j])) * _dot(x, wu_ref[j]) * w_j
            part = _dot(act.astype(bf16), wd_ref[j])
            acc = part if acc is None else acc + part
        y_ref[...] = acc

    @pl.when(i >= nt_ref[0])
    def _():
        y_ref[...] = jnp.zeros_like(y_ref)


def _experts(l, plan, xs, wg, wu, wd):
    def tile(i, *prefetch):
        return jnp.minimum(i, prefetch[-1][0] - 1)

    hbm = pl.BlockSpec(memory_space=pl.ANY)
    return pl.pallas_call(
        functools.partial(_experts_kernel, l),
        grid_spec=pltpu.PrefetchScalarGridSpec(
            num_scalar_prefetch=6,
            grid=(MOE_TILES,),
            in_specs=[pl.BlockSpec((MOE_TR, MOE_W), lambda i, *p: (tile(i, *p), 0)), hbm, hbm, hbm],
            out_specs=pl.BlockSpec((MOE_TR, D), lambda i, *p: (i, 0)),
            scratch_shapes=[pltpu.VMEM((EXP_PER_GROUP, D, D_EXP), bf16),
                            pltpu.VMEM((EXP_PER_GROUP, D, D_EXP), bf16),
                            pltpu.VMEM((EXP_PER_GROUP, D_EXP, D), bf16),
                            pltpu.VMEM((EXP_PER_GROUP, D, D_EXP), f32),
                            pltpu.VMEM((EXP_PER_GROUP, D, D_EXP), f32),
                            pltpu.VMEM((EXP_PER_GROUP, D_EXP, D), f32), pltpu.SemaphoreType.DMA((3,))],
        ),
        out_shape=jax.ShapeDtypeStruct((MOE_ROWS, D), f32),
        compiler_params=pltpu.CompilerParams(
            dimension_semantics=("arbitrary",), vmem_limit_bytes=V7X_VMEM_LIMIT),
        name="moe_experts",
    )(*plan, xs, wg, wu, wd)


def _combine_kernel(pos_ref, posn_ref, x1_ref, mod_ref, y_ref, oc_ref, ol_ref, ya_ref, yb_ref, sems):
    tm = x1_ref.shape[0]
    i = pl.program_id(0)
    n_ctx = T_CTX // tm

    def pull(p_ref, buf, sem):
        for r in range(tm):
            _row_copy(y_ref, p_ref[0, 0, r], buf, r, sem).start(priority=r % 2)

    @pl.when(i == 0)
    def _():
        pull(pos_ref, ya_ref, sems.at[0])

    def step(cur, nxt, sem_cur, sem_nxt):
        @pl.when(i + 1 < pl.num_programs(0))
        def _():
            pull(posn_ref, nxt, sem_nxt)

        pltpu.make_async_copy(y_ref.at[pl.ds(0, tm)], cur, sem_cur).wait()
        out = x1_ref[...] + mod_ref[5:6, :] * cur[...]

        @pl.when(i < n_ctx)
        def _():
            oc_ref[...] = out

        @pl.when(i >= n_ctx)
        def _():
            ol_ref[...] = out

    @pl.when(i % 2 == 0)
    def _():
        step(ya_ref, yb_ref, sems.at[0], sems.at[1])

    @pl.when(i % 2 == 1)
    def _():
        step(yb_ref, ya_ref, sems.at[1], sems.at[0])


def _combine(l, pos_blocks, x1, mod, y):
    tm = TM_DISP
    n = T // tm
    return pl.pallas_call(
        _combine_kernel,
        grid=(n,),
        in_specs=[pl.BlockSpec((1, 1, tm), lambda i: (i, 0, 0), memory_space=pltpu.SMEM),
                  pl.BlockSpec((1, 1, tm), lambda i: (jnp.minimum(i + 1, n - 1), 0, 0),
                               memory_space=pltpu.SMEM),
                  pl.BlockSpec((tm, D), lambda i: (i, 0)),
                  pl.BlockSpec((None, None, 6, D), lambda i: (l, _cond_of_tile(i, tm), 0, 0)),
                  pl.BlockSpec(memory_space=pl.ANY)],
        out_specs=_two_stream_specs(tm, D),
        out_shape=[jax.ShapeDtypeStruct((T_CTX, D), f32), jax.ShapeDtypeStruct((T_LAT, D), f32)],
        scratch_shapes=[pltpu.VMEM((tm, D), f32), pltpu.VMEM((tm, D), f32),
                        pltpu.SemaphoreType.DMA((2,))],
        compiler_params=pltpu.CompilerParams(
            dimension_semantics=("arbitrary",), vmem_limit_bytes=V7X_VMEM_LIMIT),
        name="moe_combine",
    )(pos_blocks, pos_blocks, x1, mod, y)


def _plan_kernel(cnt_ref, off_ref, tg_ref, ta_ref, tb_ref, first_ref, nxt_ref, nt_ref, zf_ref, nxtg_ref):
    def bucket(b, start):
        n = lax.shift_right_logical(cnt_ref[b] + (MOE_TR - 1), MOE_TR.bit_length() - 1)
        off_ref[b] = start * MOE_TR
        g = b // PAIRS
        p = b - g * PAIRS
        lo = jnp.where(p >= 3, 1, 0) + jnp.where(p >= 5, 1, 0)
        hi = p + 1 - jnp.where(lo >= 1, 2, 0) - jnp.where(lo >= 2, 1, 0)

        def tile(t, c):
            tg_ref[t] = g
            ta_ref[t] = lo
            tb_ref[t] = hi
            zf_ref[t] = jnp.where(t == start + n - 1, 1, 0)
            return c

        lax.fori_loop(start, start + n, tile, 0)
        return start + n

    nt = lax.fori_loop(0, N_BUCKETS, bucket, jnp.int32(0))
    nt_ref[0] = nt

    def unused(t, c):
        tg_ref[t] = N_GROUPS - 1
        ta_ref[t] = 0
        tb_ref[t] = 1
        zf_ref[t] = 1
        first_ref[t] = 0
        nxt_ref[t] = -1
        return c

    lax.fori_loop(nt, MOE_TILES, unused, 0)

    def back(i, nxt_group):
        t = nt - 1 - i
        g = tg_ref[t]
        later = jnp.where(t + 1 < nt, tg_ref[jnp.minimum(t + 1, MOE_TILES - 1)], -1)
        is_last_of_group = later != g
        nxt_group = jnp.where(is_last_of_group, later, nxt_group)
        nxtg_ref[g] = nxt_group
        return nxt_group

    lax.fori_loop(0, nt, back, jnp.int32(-1))

    def fwd(t, c):
        g = tg_ref[t]
        first_ref[t] = jnp.where((t == 0) | (tg_ref[jnp.maximum(t - 1, 0)] != g), 1, 0)
        nxt_ref[t] = nxtg_ref[g]
        return c

    lax.fori_loop(0, nt, fwd, 0)


def _moe_plan(route, counts):
    smem = pl.BlockSpec(memory_space=pltpu.SMEM)
    tile_i32 = jax.ShapeDtypeStruct((MOE_TILES,), jnp.int32)
    off, tg, ta, tb, first, nxt, n_tiles, zero_flags = pl.pallas_call(
        _plan_kernel,
        in_specs=[smem],
        out_specs=[smem] * 8,
        out_shape=[jax.ShapeDtypeStruct((N_BUCKETS,), jnp.int32), tile_i32, tile_i32, tile_i32, tile_i32,
                   tile_i32, jax.ShapeDtypeStruct((1,), jnp.int32), tile_i32],
        scratch_shapes=[pltpu.SMEM((N_GROUPS,), jnp.int32)],
        name="moe_plan",
    )(counts[0:N_BUCKETS, 0].astype(jnp.int32))
    buckets = jnp.arange(N_BUCKETS, dtype=f32)[:, None]
    pos = jnp.sum(jnp.where(route[0][None, :] == buckets, off.astype(f32)[:, None], 0.0), axis=0) + route[1]
    pos_blocks = pos.astype(jnp.int32).reshape(T // TM_DISP, 1, TM_DISP)
    return pos_blocks, zero_flags, (tg, ta, tb, first, nxt, n_tiles)


def _rope_tables():
    pos = np.arange(DEC_SEQ)
    half = HD_A // 4
    freqs = ROPE_BASE ** (-np.arange(half, dtype=np.float64) / half)
    ang_r = (pos // GRID_W)[:, None] * freqs[None, :]
    ang_c = (pos % GRID_W)[:, None] * freqs[None, :]
    ang = np.concatenate([ang_r, ang_r, ang_c, ang_c], axis=1)
    sign = np.concatenate([-np.ones(half), np.ones(half)] * 2)[None, :]
    cos = np.tile(np.cos(ang), (1, 2))
    sin = np.tile(np.sin(ang) * sign, (1, 2))
    ident_c = np.ones((TM_PROJ, LANES))
    ident_s = np.zeros((TM_PROJ, LANES))
    return (jnp.asarray(np.concatenate([cos, ident_c]), f32),
            jnp.asarray(np.concatenate([sin, ident_s]), f32))


def kernel(x_prompt, x_sample, c, cache_k, cache_v, state_ret_fwd, state_ret_bwd, c_ctx,
           norm1_w, norm2_w, w_ada, b_ada, w_in, q_norm_w, k_norm_w, attn_sink,
           ret_decay_fwd, ret_decay_bwd, ret_gn_w, w_pa, w_pb, w_o, w_router, b_router,
           w_exp_gate, w_exp_up, w_exp_down):
    x_ctx, x_lat = x_prompt.reshape(T_CTX, D), x_sample.reshape(T_LAT, D)
    cond8 = jnp.zeros((N_COND, D), f32).at[0].set(c_ctx).at[1:1 + DEC_BATCH].set(c)
    mod = _adaln(cond8, w_ada, b_ada).reshape(DEPTH, N_COND, 6, D)

    cos_t, sin_t = _rope_tables()
    blk = np.arange(Q_A) // HD_A
    ones_blk = jnp.asarray((blk[:, None] == blk[None, :]) / HD_A, bf16)
    tri = jnp.asarray(np.triu(np.ones((POST_SUB, POST_SUB)), 1), bf16)
    qnw = jnp.tile(q_norm_w, (1, NH_A)) * (HD_A ** -0.5)
    knw = jnp.tile(k_norm_w, (1, NKV_A))
    wr_pad = jnp.concatenate(_split(jnp.pad(w_router, ((0, 0), (0, LANES - N_EXP)))), axis=1)
    br_col = jnp.broadcast_to(b_router[:, None], (N_EXP, LANES))
    ck = cache_k.reshape(DEC_BATCH, DEPTH, PAST, KV_A)
    cv = cache_v.reshape(DEC_BATCH, DEPTH, PAST, KV_A)
    s0 = jnp.concatenate([state_ret_fwd, state_ret_bwd], axis=3)
    lg = jnp.log1p(-jnp.exp(jnp.concatenate([ret_decay_fwd, ret_decay_bwd], axis=1).astype(f32)))

    new_k, new_v, new_sf, new_sb = [], [], [], []
    for l in range(DEPTH):
        qa, ka, va, qr, kr, vr, gr, ga, gb = _inproj(l, x_ctx, x_lat, norm1_w, mod, w_in, cos_t, sin_t,
                                                     ones_blk, qnw, knw)
        oa_ctx = _ctx_attn(l, qa, ka, va, attn_sink)
        oa_lat = _lat_attn(l, qa, ka, va, ck, cv, cos_t, sin_t, attn_sink)
        yr_ctx, sf, sb = _retention(l, qr, kr, vr, gr, None, lg, ret_gn_w, BATCH, SEQ, 0,
                                    CTX_SEQS_PER_STEP)
        yr_lat, _, _ = _retention(l, qr, kr, vr, gr, s0[:, l], lg, ret_gn_w, DEC_BATCH, DEC_SEQ,
                                  T_CTX // DEC_SEQ, 1)
        x1, h2, route, counts = _post(l, x_ctx, x_lat, oa_ctx, oa_lat, yr_ctx, yr_lat, ga, gb,
                                      w_pa, w_pb, w_o, mod, norm2_w, wr_pad, br_col, tri)
        pos_blocks, zero_flags, plan = _moe_plan(route, counts)
        xs = _dispatch(zero_flags, pos_blocks, h2)
        y = _experts(l, plan, xs, w_exp_gate, w_exp_up, w_exp_down)
        x_ctx, x_lat = _combine(l, pos_blocks, x1, mod, y)

        new_k.append(ka[:T_CTX].reshape(BATCH, SEQ, NKV_A, HD_A))
        new_v.append(va[:T_CTX].reshape(BATCH, SEQ, NKV_A, HD_A))
        new_sf.append(sf)
        new_sb.append(sb)

    return (x_ctx.reshape(BATCH, SEQ, D), x_lat.reshape(DEC_BATCH, DEC_SEQ, D),
            jnp.stack(new_k, axis=1), jnp.stack(new_v, axis=1),
            jnp.stack(new_sf, axis=1), jnp.stack(new_sb, axis=1))
```

```python
import functools

import numpy as np
import jax
import jax.numpy as jnp
from jax import lax
from jax.experimental import pallas as pl
from jax.experimental.pallas import tpu as pltpu

D = 1024
BATCH, SEQ = 16, 256
DEC_BATCH, DEC_SEQ = 2, 2048
DEPTH = 2
PAST = 512
GRID_W = 64
NH_A, NKV_A, HD_A = 8, 2, 64
WINDOW = 128
NH_R, DK_R, DV_R = 4, 64, 128
CHUNK = 128
N_EXP, N_GROUPS, EXP_PER_GROUP = 16, 4, 4
D_EXP = 512
ROPE_BASE = 10000.0
EPS = 1e-6
NEG_INF = -1e30

Q_A = NH_A * HD_A
KV_A = NKV_A * HD_A
QK_R = NH_R * DK_R
V_R = NH_R * DV_R
C_QA = (0, Q_A)
C_KA = (C_QA[1], C_QA[1] + KV_A)
C_VA = (C_KA[1], C_KA[1] + KV_A)
C_QR = (C_VA[1], C_VA[1] + QK_R)
C_KR = (C_QR[1], C_QR[1] + QK_R)
C_VR = (C_KR[1], C_KR[1] + V_R)
C_GR = (C_VR[1], C_VR[1] + V_R)
C_GA = (C_GR[1], C_GR[1] + D)
C_GB = (C_GA[1], C_GA[1] + D)
D_IN = C_GB[1]

T_CTX = BATCH * SEQ
T_LAT = DEC_BATCH * DEC_SEQ
T = T_CTX + T_LAT
N_COND = 8

LANES = 128
MXU_TILE = 256
V7X_VMEM_LIMIT = 56 * 1024 * 1024

TM_PROJ = 512
TM_POST = 512
POST_SUB = 512
W_CHUNK = 256
TM_DISP = 512
MOE_TR = 256
PAIRS = EXP_PER_GROUP * (EXP_PER_GROUP - 1) // 2
N_BUCKETS = N_GROUPS * PAIRS
BUCKET_ROWS = 32
MOE_TILES = T // MOE_TR + N_BUCKETS
MOE_ROWS = MOE_TILES * MOE_TR
MOE_W = D + LANES
ATT_QB = 256
RET_UNROLL = 4
CTX_SEQS_PER_STEP = 4

f32 = jnp.float32
bf16 = jnp.bfloat16


def _dot(a, b):
    return jnp.dot(a, b, preferred_element_type=f32)


def _dot_t(a, b):
    return lax.dot_general(a, b, (((1,), (1,)), ((), ())), preferred_element_type=f32)


def _split(x):
    hi = x.astype(bf16)
    lo = (x - hi.astype(f32)).astype(bf16)
    return hi, lo


def _dot3(a, b):
    ah, al = _split(a)
    bh, bl = _split(b)
    return _dot(ah, bh) + (_dot(ah, bl) + _dot(al, bh))


def _sigmoid(x):
    return 1.0 / (1.0 + jnp.exp(-x))


def _silu(x):
    return x * _sigmoid(x)


def _cond_of_tile(i, tm):
    n_ctx = T_CTX // tm
    per_b = DEC_SEQ // tm
    return jnp.where(i < n_ctx, 0, 1 + jnp.maximum(i - n_ctx, 0) // per_b)


def _adaln_kernel(cond_ref, w_ref, b_ref, o_ref):
    a = _silu(cond_ref[...])
    o_ref[0] = _dot3(a, w_ref[0]) + b_ref[0]


def _adaln(cond8, w_ada, b_ada):
    tn = 2048
    return pl.pallas_call(
        _adaln_kernel,
        grid=(DEPTH, 6 * D // tn),
        in_specs=[
            pl.BlockSpec((N_COND, D), lambda l, j: (0, 0)),
            pl.BlockSpec((1, D, tn), lambda l, j: (l, 0, j)),
            pl.BlockSpec((1, 1, tn), lambda l, j: (l, 0, j)),
        ],
        out_specs=pl.BlockSpec((1, N_COND, tn), lambda l, j: (l, 0, j)),
        out_shape=jax.ShapeDtypeStruct((DEPTH, N_COND, 6 * D), f32),
        compiler_params=pltpu.CompilerParams(
            dimension_semantics=("parallel", "parallel"), vmem_limit_bytes=V7X_VMEM_LIMIT),
        name="adaln",
    )(cond8, w_ada, b_ada.reshape(DEPTH, 1, 6 * D))


def _rope(x, cos, sin_signed, first_half):
    fwd = pltpu.roll(x, 16, 1)
    bwd = pltpu.roll(x, LANES - 16, 1)
    partner = jnp.where(first_half, bwd, fwd)
    return x * cos + partner * sin_signed


def _head_rms(x, ones_blk, w):
    n = x.shape[1]
    sq_hi, sq_lo = _split(x * x)
    wt = min(n, MXU_TILE)
    blk = ones_blk[0:wt, 0:wt]
    parts = [_dot(sq_hi[:, c:c + wt], blk) + _dot(sq_lo[:, c:c + wt], blk) for c in range(0, n, wt)]
    mean = parts[0] if len(parts) == 1 else jnp.concatenate(parts, axis=1)
    return x * lax.rsqrt(mean + EPS) * w


def _load_cast(w_hbm, dst_ref, stage_ref, sems):
    k, n = w_hbm.shape

    def chunk_copy(c):
        return pltpu.make_async_copy(w_hbm.at[:, pl.ds(c * W_CHUNK, W_CHUNK)],
                                     stage_ref.at[c % 2, pl.ds(0, k)], sems.at[c % 2])

    n_chunks = n // W_CHUNK
    chunk_copy(0).start()
    for c in range(n_chunks):
        if c + 1 < n_chunks:
            chunk_copy(c + 1).start()
        chunk_copy(c).wait()
        dst_ref[:, c * W_CHUNK:(c + 1) * W_CHUNK] = stage_ref[c % 2, 0:k, :].astype(bf16)


def _tile_of_two(i, n_first, a_ref, b_ref):
    return jnp.where(i < n_first, a_ref[...], b_ref[...])


def _inproj_kernel(l, xc_ref, xl_ref, n1_ref, mod_ref, w_hbm, cos_ref, sin_ref, ones_ref, qnw_ref,
                   knw_ref, qa_ref, ka_ref, va_ref, qr_ref, kr_ref, vr_ref, gr_ref, ga_ref, gb_ref,
                   w_ref, stage_ref, sems):
    i = pl.program_id(0)

    @pl.when(i == 0)
    def _():
        _load_cast(w_hbm.at[l], w_ref, stage_ref, sems)

    x = _tile_of_two(i, T_CTX // xc_ref.shape[0], xc_ref, xl_ref)
    y = x * lax.rsqrt(jnp.mean(x * x, axis=-1, keepdims=True) + EPS) * n1_ref[...]
    h = (y * (1.0 + mod_ref[1:2, :]) + mod_ref[0:1, :]).astype(bf16)

    whole = _dot(h, w_ref[...])

    def proj(c):
        return whole[:, c[0]:c[1]]

    cos = cos_ref[...]
    sin = sin_ref[...]
    lane = lax.broadcasted_iota(jnp.int32, cos.shape, 1)
    first_half = (lane % 32) < 16

    def rope_all(v):
        parts = [_rope(v[:, j:j + LANES], cos, sin, first_half) for j in range(0, v.shape[1], LANES)]
        return parts[0] if len(parts) == 1 else jnp.concatenate(parts, axis=1)

    ones_blk = ones_ref[...]
    qa = _head_rms(proj(C_QA), ones_blk, qnw_ref[...])
    qa_ref[...] = rope_all(qa).astype(bf16)
    ka_ref[...] = _head_rms(proj(C_KA), ones_blk, knw_ref[...])
    va_ref[...] = proj(C_VA)
    qr_ref[...] = rope_all(proj(C_QR)).astype(bf16)
    kr_ref[...] = (rope_all(proj(C_KR)) * (DK_R ** -0.5)).astype(bf16)
    vr_ref[...] = proj(C_VR).astype(bf16)
    gr_ref[...] = proj(C_GR).astype(bf16)
    ga_ref[...] = proj(C_GA).astype(bf16)
    gb_ref[...] = proj(C_GB).astype(bf16)


def _two_stream_specs(tm, width):
    n_ctx = T_CTX // tm
    return [pl.BlockSpec((tm, width), lambda i, *_: (jnp.minimum(i, n_ctx - 1), 0)),
            pl.BlockSpec((tm, width), lambda i, *_: (jnp.maximum(i - n_ctx, 0), 0))]


def _inproj(l, x_ctx, x_lat, norm1_w, mod, w_in, cos_t, sin_t, ones_blk, qnw, knw):
    tm = TM_PROJ
    n_ctx = T_CTX // tm
    per_b = DEC_SEQ // tm

    def tab_map(i):
        return (jnp.where(i < n_ctx, per_b, jnp.maximum(i - n_ctx, 0) % per_b), 0)

    row = lambda i: (i, 0)
    const = lambda i: (0, 0)
    widths = [(Q_A, bf16), (KV_A, f32), (KV_A, f32), (QK_R, bf16), (QK_R, bf16), (V_R, bf16),
              (V_R, bf16), (D, bf16), (D, bf16)]
    return pl.pallas_call(
        functools.partial(_inproj_kernel, l),
        grid=(T // tm,),
        in_specs=_two_stream_specs(tm, D) + [
            pl.BlockSpec((1, D), const),
            pl.BlockSpec((None, None, 6, D), lambda i: (l, _cond_of_tile(i, tm), 0, 0)),
            pl.BlockSpec(memory_space=pl.ANY),
            pl.BlockSpec((tm, LANES), tab_map),
            pl.BlockSpec((tm, LANES), tab_map),
            pl.BlockSpec((Q_A, Q_A), const),
            pl.BlockSpec((1, Q_A), const),
            pl.BlockSpec((1, KV_A), const),
        ],
        out_specs=[pl.BlockSpec((tm, w), row) for w, _ in widths],
        out_shape=[jax.ShapeDtypeStruct((T, w), dt) for w, dt in widths],
        scratch_shapes=[pltpu.VMEM((D, D_IN), bf16), pltpu.VMEM((2, D, W_CHUNK), f32),
                        pltpu.SemaphoreType.DMA((2,))],
        compiler_params=pltpu.CompilerParams(
            dimension_semantics=("arbitrary",), vmem_limit_bytes=V7X_VMEM_LIMIT),
        name="inproj",
    )(x_ctx, x_lat, norm1_w[l:l + 1], mod, w_in, cos_t, sin_t, ones_blk, qnw[l:l + 1], knw[l:l + 1])


def _head_blocks(t, kv, lo_mask):
    r = pltpu.roll(t, HD_A, 1)
    if kv == 0:
        a = jnp.where(lo_mask, t, 0.0)
        b = jnp.where(lo_mask, 0.0, r)
    else:
        a = jnp.where(lo_mask, r, 0.0)
        b = jnp.where(lo_mask, 0.0, t)
    return jnp.concatenate([a, b], axis=0).astype(bf16)


def _ctx_attn_kernel(sink_ref, q_ref, k_ref, v_ref, o_ref):
    n = SEQ
    lo_mask = lax.broadcasted_iota(jnp.int32, (n, LANES), 1) < HD_A
    for b in range(CTX_SEQS_PER_STEP):
        rows = slice(b * n, (b + 1) * n)
        k = k_ref[rows, :]
        v = v_ref[rows, :]
        for kv in range(NKV_A):
            kblk = _head_blocks(k, kv, lo_mask)
            vblk = _head_blocks(v, kv, lo_mask)
            q2 = jnp.concatenate([q_ref[rows, (2 * kv) * LANES:(2 * kv + 1) * LANES],
                                  q_ref[rows, (2 * kv + 1) * LANES:(2 * kv + 2) * LANES]], axis=0)
            s = _dot_t(q2, kblk)
            row = lax.broadcasted_iota(jnp.int32, (2 * n, 1), 0)
            ps, invs = [], []
            for hh in range(2):
                sk = jnp.where(row < n, sink_ref[4 * kv + hh], sink_ref[4 * kv + 2 + hh])
                sh = s[:, hh * n:(hh + 1) * n]
                m = jnp.maximum(jnp.max(sh, axis=-1, keepdims=True), sk)
                p = jnp.exp(sh - m)
                invs.append(1.0 / (jnp.sum(p, axis=-1, keepdims=True) + jnp.exp(sk - m)))
                ps.append(p.astype(bf16))
            lo2 = lax.broadcasted_iota(jnp.int32, (2 * n, LANES), 1) < HD_A
            o = _dot(jnp.concatenate(ps, axis=1), vblk) * jnp.where(lo2, invs[0], invs[1])
            o_ref[rows, (2 * kv) * LANES:(2 * kv + 1) * LANES] = o[:n].astype(bf16)
            o_ref[rows, (2 * kv + 1) * LANES:(2 * kv + 2) * LANES] = o[n:].astype(bf16)


def _ctx_attn(l, qa, ka, va, sink):
    blk = lambda w: pl.BlockSpec((CTX_SEQS_PER_STEP * SEQ, w), lambda b: (b, 0))
    return pl.pallas_call(
        _ctx_attn_kernel,
        grid=(BATCH // CTX_SEQS_PER_STEP,),
        in_specs=[pl.BlockSpec(memory_space=pltpu.SMEM), blk(Q_A), blk(KV_A), blk(KV_A)],
        out_specs=blk(Q_A),
        out_shape=jax.ShapeDtypeStruct((T_CTX, Q_A), bf16),
        compiler_params=pltpu.CompilerParams(
            dimension_semantics=("parallel",), vmem_limit_bytes=V7X_VMEM_LIMIT),
        name="ctx_attn",
    )(sink[l], qa, ka, va)


def _lat_attn_kernel(sink_ref, q_ref, k_ref, v_ref, kc_ref, vc_ref, cos_ref, sin_ref, o_ref):
    j = pl.program_id(1)
    qb = ATT_QB
    win = 2 * qb
    ws = pl.multiple_of(jnp.clip(j * qb - WINDOW, 0, DEC_SEQ - win), WINDOW)
    lo_mask = lax.broadcasted_iota(jnp.int32, (win, LANES), 1) < HD_A
    lane = lax.broadcasted_iota(jnp.int32, (win, LANES), 1)
    kw = _rope(k_ref[pl.ds(ws, win), :], cos_ref[pl.ds(ws, win), :], sin_ref[pl.ds(ws, win), :],
               (lane % 32) < 16)
    vw = v_ref[pl.ds(ws, win), :]
    kc = kc_ref[...]
    vc = vc_ref[...]
    qpos = j * qb + (lax.broadcasted_iota(jnp.int32, (2 * qb, win), 0) & (qb - 1))
    kpos = ws + lax.broadcasted_iota(jnp.int32, (2 * qb, win), 1)
    valid = jnp.abs(qpos - kpos) <= WINDOW
    out_lo = lax.broadcasted_iota(jnp.int32, (2 * qb, LANES), 1) < HD_A
    for kv in range(NKV_A):
        kc_blk = _head_blocks(kc, kv, lo_mask[:PAST])
        vc_blk = _head_blocks(vc, kv, lo_mask[:PAST])
        kw_blk = _head_blocks(kw, kv, lo_mask)
        vw_blk = _head_blocks(vw, kv, lo_mask)
        q2 = jnp.concatenate([q_ref[:, (2 * kv) * LANES:(2 * kv + 1) * LANES],
                              q_ref[:, (2 * kv + 1) * LANES:(2 * kv + 2) * LANES]], axis=0)
        s_c = _dot_t(q2, kc_blk)
        s_w = _dot_t(q2, kw_blk)
        pcs, pws, invs = [], [], []
        for hh in range(2):
            row = lax.broadcasted_iota(jnp.int32, (2 * qb, 1), 0)
            sk = jnp.where(row < qb, sink_ref[4 * kv + hh], sink_ref[4 * kv + 2 + hh])
            sc = s_c[:, hh * PAST:(hh + 1) * PAST]
            sw = jnp.where(valid, s_w[:, hh * win:(hh + 1) * win], NEG_INF)
            m = jnp.maximum(jnp.maximum(jnp.max(sc, axis=-1, keepdims=True),
                                        jnp.max(sw, axis=-1, keepdims=True)), sk)
            pc = jnp.exp(sc - m)
            pw = jnp.exp(sw - m)
            den = (jnp.sum(pc, axis=-1, keepdims=True) + jnp.sum(pw, axis=-1, keepdims=True)
                   + jnp.exp(sk - m))
            invs.append(1.0 / den)
            pcs.append(pc.astype(bf16))
            pws.append(pw.astype(bf16))
        o = _dot(jnp.concatenate(pcs, axis=1), vc_blk) + _dot(jnp.concatenate(pws, axis=1), vw_blk)
        o = o * jnp.where(out_lo, invs[0], invs[1])
        o_ref[:, (2 * kv) * LANES:(2 * kv + 1) * LANES] = o[:qb].astype(bf16)
        o_ref[:, (2 * kv + 1) * LANES:(2 * kv + 2) * LANES] = o[qb:].astype(bf16)


def _lat_attn(l, qa, ka, va, cache_k, cache_v, cos_l, sin_l, sink):
    qb = ATT_QB
    nq = DEC_SEQ // qb
    ctx_blocks = T_CTX // DEC_SEQ
    seq = lambda b, j: (ctx_blocks + b, 0)
    return pl.pallas_call(
        _lat_attn_kernel,
        grid=(DEC_BATCH, nq),
        in_specs=[
            pl.BlockSpec(memory_space=pltpu.SMEM),
            pl.BlockSpec((qb, Q_A), lambda b, j: (T_CTX // qb + b * nq + j, 0)),
            pl.BlockSpec((DEC_SEQ, KV_A), seq),
            pl.BlockSpec((DEC_SEQ, KV_A), seq),
            pl.BlockSpec((None, None, PAST, KV_A), lambda b, j: (b, l, 0, 0)),
            pl.BlockSpec((None, None, PAST, KV_A), lambda b, j: (b, l, 0, 0)),
            pl.BlockSpec((DEC_SEQ, LANES), lambda b, j: (0, 0)),
            pl.BlockSpec((DEC_SEQ, LANES), lambda b, j: (0, 0)),
        ],
        out_specs=pl.BlockSpec((qb, Q_A), lambda b, j: (b * nq + j, 0)),
        out_shape=jax.ShapeDtypeStruct((T_LAT, Q_A), bf16),
        compiler_params=pltpu.CompilerParams(
            dimension_semantics=("parallel", "parallel"), vmem_limit_bytes=V7X_VMEM_LIMIT),
        name="lat_attn",
    )(sink[l], qa, ka, va, cache_k, cache_v, cos_l, sin_l)


def _dup_heads(pair, lo_mask):
    r = pltpu.roll(pair, DK_R, 1)
    return jnp.where(lo_mask, pair, r), jnp.where(lo_mask, r, pair)


def _retention_kernel(has_s0, n_seq, n_chunks, *refs):
    if has_s0:
        (lg_ref, q_ref, k_ref, v_ref, g_ref, s0_ref, gnw_ref,
         y_ref, sf_ref, sb_ref, ds_ref, st_ref, mask_ref, qdec_ref, kdec_ref, cdec_ref) = refs
    else:
        (lg_ref, q_ref, k_ref, v_ref, g_ref, gnw_ref,
         y_ref, sf_ref, sb_ref, ds_ref, st_ref, mask_ref, qdec_ref, kdec_ref, cdec_ref) = refs
        s0_ref = None
    C = CHUNK
    lo_mask = lax.broadcasted_iota(jnp.int32, (C, LANES), 1) < DK_R

    ri = lax.broadcasted_iota(jnp.int32, (C, C), 0).astype(f32)
    diff = ri - lax.broadcasted_iota(jnp.int32, (C, C), 1).astype(f32)
    for h in range(NH_R):
        lg_f, lg_b = lg_ref[h], lg_ref[NH_R + h]
        mask_ref[h] = (jnp.where(diff >= 0, jnp.exp(jnp.maximum(diff, 0.0) * lg_f), 0.0)
                       + jnp.where(diff <= 0, jnp.exp(jnp.maximum(-diff, 0.0) * lg_b), 0.0))
        qdec_ref[:, h * LANES:(h + 1) * LANES] = jnp.exp(
            jnp.where(lo_mask, (ri + 1.0) * lg_f, (C - ri) * lg_b))
        kdec_ref[:, h * LANES:(h + 1) * LANES] = jnp.exp(
            jnp.where(lo_mask, (C - 1.0 - ri) * lg_f, ri * lg_b))
        cdec_ref[h] = jnp.exp(jnp.where(ri < DK_R, C * lg_f, C * lg_b))

    def inc_body(c, carry):
        r0 = pl.multiple_of(c * C, C)
        for pr in range(2):
            kp = k_ref[pl.ds(r0, C), pr * LANES:(pr + 1) * LANES].astype(f32)
            for hh, kd in enumerate(_dup_heads(kp, lo_mask)):
                h = 2 * pr + hh
                kd = (kd * kdec_ref[:, h * LANES:(h + 1) * LANES]).astype(bf16)
                vh = v_ref[pl.ds(r0, C), h * DV_R:(h + 1) * DV_R]
                ds_ref[c, h] = lax.dot_general(kd, vh, (((0,), (0,)), ((), ())),
                                               preferred_element_type=f32)
        return carry

    lax.fori_loop(0, n_seq * n_chunks, inc_body, 0, unroll=RET_UNROLL)

    for sq in range(n_seq):
        base = sq * n_chunks
        for h in range(NH_R):
            cf = cdec_ref[h, 0:DK_R, :]
            cb = cdec_ref[h, DK_R:2 * DK_R, :]
            if has_s0:
                init_f = s0_ref[h, 0:DK_R, :]
                init_b = s0_ref[h, DK_R:2 * DK_R, :]
            else:
                init_f = jnp.zeros((DK_R, DV_R), f32)
                init_b = init_f

            def fwd_body(i, s, h=h, cf=cf, base=base):
                c = base + i
                st_ref[c, h, 0:DK_R, :] = s
                return s * cf + ds_ref[c, h, 0:DK_R, :]

            def bwd_body(i, s, h=h, cb=cb, base=base):
                c = base + n_chunks - 1 - i
                st_ref[c, h, DK_R:2 * DK_R, :] = s
                return s * cb + ds_ref[c, h, DK_R:2 * DK_R, :]

            sf_ref[sq, h] = lax.fori_loop(0, n_chunks, fwd_body, init_f)
            sb_ref[sq, h] = lax.fori_loop(0, n_chunks, bwd_body, init_b)

    def out_body(c, carry):
        r0 = pl.multiple_of(c * C, C)
        for pr in range(2):
            qp = q_ref[pl.ds(r0, C), pr * LANES:(pr + 1) * LANES]
            kp = k_ref[pl.ds(r0, C), pr * LANES:(pr + 1) * LANES].astype(f32)
            kblk = jnp.concatenate([jnp.where(lo_mask, kp, 0.0), jnp.where(lo_mask, 0.0, kp)],
                                   axis=0).astype(bf16)
            a2 = _dot_t(qp, kblk)
            for hh, qd in enumerate(_dup_heads(qp.astype(f32), lo_mask)):
                h = 2 * pr + hh
                a = (a2[:, hh * C:(hh + 1) * C] * mask_ref[h]).astype(bf16)
                vh = v_ref[pl.ds(r0, C), h * DV_R:(h + 1) * DV_R]
                qd = (qd * qdec_ref[:, h * LANES:(h + 1) * LANES]).astype(bf16)
                o = _dot(a, vh) + _dot(qd, st_ref[c, h].astype(bf16))
                mu = jnp.mean(o, axis=-1, keepdims=True)
                d = o - mu
                var = jnp.mean(d * d, axis=-1, keepdims=True)
                yh = d * lax.rsqrt(var + EPS) * gnw_ref[:, h * DV_R:(h + 1) * DV_R]
                g = g_ref[pl.ds(r0, C), h * DV_R:(h + 1) * DV_R].astype(f32)
                y_ref[pl.ds(r0, C), h * DV_R:(h + 1) * DV_R] = (yh * _silu(g)).astype(bf16)
        return carry

    lax.fori_loop(0, n_seq * n_chunks, out_body, 0, unroll=RET_UNROLL)


def _retention(l, qr, kr, vr, gr, s0, lg, gnw, nb, seq, row_block0, n_seq):
    n_chunks = seq // CHUNK
    has_s0 = s0 is not None
    rows = n_seq * seq
    tok = lambda w: pl.BlockSpec((rows, w), lambda b: (row_block0 + b, 0))
    in_specs = [pl.BlockSpec(memory_space=pltpu.SMEM), tok(QK_R), tok(QK_R), tok(V_R), tok(V_R)]
    args = [lg[l], qr, kr, vr, gr]
    if has_s0:
        in_specs.append(pl.BlockSpec((None, NH_R, 2 * DK_R, DV_R), lambda b: (b, 0, 0, 0)))
        args.append(s0)
    in_specs += [pl.BlockSpec((1, V_R), lambda b: (0, 0))]
    args += [gnw[l:l + 1]]
    st_spec = pl.BlockSpec((n_seq, NH_R, DK_R, DV_R), lambda b: (b, 0, 0, 0))
    return pl.pallas_call(
        functools.partial(_retention_kernel, has_s0, n_seq, n_chunks),
        grid=(nb // n_seq,),
        in_specs=in_specs,
        out_specs=[pl.BlockSpec((rows, V_R), lambda b: (b, 0)), st_spec, st_spec],
        out_shape=[jax.ShapeDtypeStruct((nb * seq, V_R), bf16),
                   jax.ShapeDtypeStruct((nb, NH_R, DK_R, DV_R), f32),
                   jax.ShapeDtypeStruct((nb, NH_R, DK_R, DV_R), f32)],
        scratch_shapes=[pltpu.VMEM((n_seq * n_chunks, NH_R, 2 * DK_R, DV_R), f32),
                        pltpu.VMEM((n_seq * n_chunks, NH_R, 2 * DK_R, DV_R), f32),
                        pltpu.VMEM((NH_R, CHUNK, CHUNK), f32), pltpu.VMEM((CHUNK, NH_R * LANES), f32),
                        pltpu.VMEM((CHUNK, NH_R * LANES), f32), pltpu.VMEM((NH_R, 2 * DK_R, DV_R), f32)],
        compiler_params=pltpu.CompilerParams(
            dimension_semantics=("parallel",), vmem_limit_bytes=V7X_VMEM_LIMIT),
        name="retention_lat" if has_s0 else "retention_ctx",
    )(*args)


def _post_kernel(l, xc_ref, xl_ref, oac_ref, oal_ref, yrc_ref, yrl_ref, ga_ref, gb_ref, wpa_hbm, wpb_hbm,
                 wo_hbm, mod_ref, n2_ref, wr_ref, br_ref, tri_ref, x1_ref, h2_ref, route_ref, cnt_ref,
                 carry_ref, wpa_ref, wpb_ref, wo_ref, stage_ref, sems):
    i = pl.program_id(0)
    tm = xc_ref.shape[0]
    n_ctx = T_CTX // tm

    @pl.when(i == 0)
    def _():
        carry_ref[...] = jnp.zeros_like(carry_ref)
        _load_cast(wpa_hbm.at[l], wpa_ref, stage_ref, sems)
        _load_cast(wpb_hbm.at[l], wpb_ref, stage_ref, sems)
        _load_cast(wo_hbm.at[l], wo_ref, stage_ref, sems)

    for r0 in range(0, tm, POST_SUB):
        _post_subtile(i < n_ctx, slice(r0, r0 + POST_SUB), xc_ref, xl_ref, oac_ref, oal_ref, yrc_ref,
                      yrl_ref, ga_ref, gb_ref, mod_ref, n2_ref, wr_ref, br_ref, tri_ref, x1_ref, h2_ref,
                      route_ref, carry_ref, wpa_ref, wpb_ref, wo_ref)
    cnt_ref[...] = carry_ref[...]


def _post_subtile(is_ctx, rows, xc_ref, xl_ref, oac_ref, oal_ref, yrc_ref, yrl_ref, ga_ref, gb_ref, mod_ref,
                  n2_ref, wr_ref, br_ref, tri_ref, x1_ref, h2_ref, route_ref, carry_ref, wpa_ref, wpb_ref,
                  wo_ref):
    tm = rows.stop - rows.start
    pick = lambda a_ref, b_ref: jnp.where(is_ctx, a_ref[rows, :], b_ref[rows, :])
    ga = _sigmoid(ga_ref[rows, :].astype(f32))
    gb = _sigmoid(gb_ref[rows, :].astype(f32))
    merged = (ga * _dot(pick(oac_ref, oal_ref), wpa_ref[...])
              + gb * _dot(pick(yrc_ref, yrl_ref), wpb_ref[...]))
    mix = _dot(merged.astype(bf16), wo_ref[...])
    x1 = pick(xc_ref, xl_ref) + mod_ref[2:3, :] * mix
    x1_ref[rows, :] = x1
    y = x1 * lax.rsqrt(jnp.mean(x1 * x1, axis=-1, keepdims=True) + EPS) * n2_ref[...]
    h2 = y * (1.0 + mod_ref[4:5, :]) + mod_ref[3:4, :]
    h2_ref[rows, 0:D] = h2

    h_hi, h_lo = _split(h2)
    both = _dot(h_hi, wr_ref[...])
    logits = both[:, 0:LANES] + (both[:, LANES:2 * LANES] + _dot(h_lo, wr_ref[:, 0:LANES]))
    lt = logits.T[0:N_EXP, :]
    scores = _sigmoid(lt)
    sel = scores + br_ref[:, 0:1]
    row = lax.broadcasted_iota(jnp.int32, (N_EXP, tm), 0)

    best = None
    bg = None
    for g in range(N_GROUPS):
        a, b, c, d = (sel[EXP_PER_GROUP * g + k:EXP_PER_GROUP * g + k + 1, :] for k in range(4))
        p, q = jnp.maximum(a, b), jnp.minimum(a, b)
        r, s = jnp.maximum(c, d), jnp.minimum(c, d)
        gs = jnp.maximum(p, r) + jnp.maximum(jnp.minimum(p, r), jnp.maximum(q, s))
        if g == 0:
            best, bg = gs, jnp.zeros((1, tm), jnp.int32)
        else:
            upd = gs > best
            bg = jnp.where(upd, g, bg)
            best = jnp.where(upd, gs, best)
    masked = jnp.where(jnp.right_shift(row, 2) == bg, sel, NEG_INF)
    m1 = jnp.max(masked, axis=0, keepdims=True)
    i1 = jnp.min(jnp.where(masked == m1, row, N_EXP), axis=0, keepdims=True)
    masked2 = jnp.where(row == i1, NEG_INF, masked)
    m2 = jnp.max(masked2, axis=0, keepdims=True)
    i2 = jnp.min(jnp.where(masked2 == m2, row, N_EXP), axis=0, keepdims=True)
    oh1 = row == i1
    oh2 = row == i2
    s1 = jnp.sum(jnp.where(oh1, scores, 0.0), axis=0, keepdims=True)
    s2 = jnp.sum(jnp.where(oh2, scores, 0.0), axis=0, keepdims=True)
    den = s1 + s2

    comb = jnp.where(oh1, s1 / den, 0.0) + jnp.where(oh2, s2 / den, 0.0)
    comb_t = jnp.concatenate([comb, jnp.zeros((LANES - N_EXP, tm), f32)], axis=0).T
    h2_ref[rows, D:D + LANES] = comb_t

    e_lo = jnp.minimum(i1, i2) - EXP_PER_GROUP * bg
    e_hi = jnp.maximum(i1, i2) - EXP_PER_GROUP * bg
    pair = jnp.right_shift(e_lo * (7 - e_lo), 1) + (e_hi - e_lo - 1)
    bucket = bg * PAIRS + pair
    rowb = lax.broadcasted_iota(jnp.int32, (BUCKET_ROWS, tm), 0)
    ohb = rowb == bucket
    ohb_f = jnp.where(ohb, 1.0, 0.0)
    tot = carry_ref[:, 0:1] + _dot(ohb_f.astype(bf16), tri_ref[...])
    rank = jnp.sum(jnp.where(ohb, tot, 0.0), axis=0, keepdims=True)
    carry_ref[...] = carry_ref[...] + jnp.sum(ohb_f, axis=1, keepdims=True)

    route_ref[0:1, rows] = bucket.astype(f32)
    route_ref[1:2, rows] = rank
    route_ref[2:8, rows] = jnp.zeros((6, tm), f32)


def _post(l, x_ctx, x_lat, oa_ctx, oa_lat, yr_ctx, yr_lat, ga, gb, wpa, wpb, wo, mod, norm2_w, wr_pad,
          br_col, tri):
    tm = TM_POST
    row = lambda w: pl.BlockSpec((tm, w), lambda i: (i, 0))
    const = lambda a: pl.BlockSpec(a.shape, lambda i: (0,) * a.ndim)
    hbm = pl.BlockSpec(memory_space=pl.ANY)
    return pl.pallas_call(
        functools.partial(_post_kernel, l),
        grid=(T // tm,),
        in_specs=(_two_stream_specs(tm, D) + _two_stream_specs(tm, Q_A) + _two_stream_specs(tm, V_R)
                  + [row(D), row(D), hbm, hbm, hbm,
                     pl.BlockSpec((None, None, 6, D), lambda i: (l, _cond_of_tile(i, tm), 0, 0)),
                     pl.BlockSpec((1, D), lambda i: (0, 0)), const(wr_pad), const(br_col), const(tri)]),
        out_specs=[row(D), row(MOE_W), pl.BlockSpec((8, tm), lambda i: (0, i)),
                   pl.BlockSpec((BUCKET_ROWS, LANES), lambda i: (0, 0))],
        out_shape=[jax.ShapeDtypeStruct((T, D), f32), jax.ShapeDtypeStruct((T, MOE_W), f32),
                   jax.ShapeDtypeStruct((8, T), f32), jax.ShapeDtypeStruct((BUCKET_ROWS, LANES), f32)],
        scratch_shapes=[pltpu.VMEM((BUCKET_ROWS, LANES), f32), pltpu.VMEM((Q_A, D), bf16),
                        pltpu.VMEM((V_R, D), bf16), pltpu.VMEM((D, D), bf16),
                        pltpu.VMEM((2, D, W_CHUNK), f32), pltpu.SemaphoreType.DMA((2,))],
        compiler_params=pltpu.CompilerParams(
            dimension_semantics=("arbitrary",), vmem_limit_bytes=V7X_VMEM_LIMIT),
        name="post_router",
    )(x_ctx, x_lat, oa_ctx, oa_lat, yr_ctx, yr_lat, ga, gb, wpa, wpb, wo, mod, norm2_w[l:l + 1],
      wr_pad, br_col, tri)


def _row_copy(src_ref, src_row, dst_ref, dst_row, sem):
    return pltpu.make_async_copy(src_ref.at[pl.ds(src_row, 1)], dst_ref.at[pl.ds(dst_row, 1)], sem)


def _dispatch_kernel(zf_ref, pos_ref, h_ref, xs_ref, zero_ref, sem):
    tm = h_ref.shape[0]

    @pl.when(pl.program_id(0) == 0)
    def _():
        zero_ref[...] = jnp.zeros_like(zero_ref)

        def tile_copy(t):
            return pltpu.make_async_copy(zero_ref, xs_ref.at[pl.ds(t * MOE_TR, MOE_TR)], sem)

        def start(t, carry):
            @pl.when(zf_ref[t] != 0)
            def _():
                tile_copy(t).start()
            return carry

        def wait(t, carry):
            @pl.when(zf_ref[t] != 0)
            def _():
                tile_copy(t).wait()
            return carry

        lax.fori_loop(0, MOE_TILES, start, 0)
        lax.fori_loop(0, MOE_TILES, wait, 0)

    for r in range(tm):
        _row_copy(h_ref, r, xs_ref, pos_ref[0, 0, r], sem).start(priority=r % 2)
    pltpu.make_async_copy(h_ref, xs_ref.at[pl.ds(0, tm)], sem).wait()


def _dispatch(zero_flags, pos_blocks, h2):
    tm = TM_DISP
    return pl.pallas_call(
        _dispatch_kernel,
        grid_spec=pltpu.PrefetchScalarGridSpec(
            num_scalar_prefetch=1,
            grid=(T // tm,),
            in_specs=[pl.BlockSpec((1, 1, tm), lambda i, zf: (i, 0, 0), memory_space=pltpu.SMEM),
                      pl.BlockSpec((tm, MOE_W), lambda i, zf: (i, 0))],
            out_specs=pl.BlockSpec(memory_space=pl.ANY),
            scratch_shapes=[pltpu.VMEM((MOE_TR, MOE_W), f32), pltpu.SemaphoreType.DMA(())],
        ),
        out_shape=jax.ShapeDtypeStruct((MOE_ROWS, MOE_W), f32),
        compiler_params=pltpu.CompilerParams(
            dimension_semantics=("arbitrary",), vmem_limit_bytes=V7X_VMEM_LIMIT),
        name="moe_dispatch",
    )(zero_flags, pos_blocks, h2)


def _experts_kernel(l, tg_ref, ta_ref, tb_ref, first_ref, nxt_ref, nt_ref, xs_ref, wg_hbm, wu_hbm, wd_hbm,
                    y_ref, wg_ref, wu_ref, wd_ref, sg_ref, su_ref, sd_ref, sems):
    i = pl.program_id(0)
    live = i < nt_ref[0]
    g = tg_ref[i]

    def fetch(grp):
        return [pltpu.make_async_copy(w.at[l, pl.ds(grp * EXP_PER_GROUP, EXP_PER_GROUP)], s, sems.at[k])
                for k, (w, s) in enumerate(((wg_hbm, sg_ref), (wu_hbm, su_ref), (wd_hbm, sd_ref)))]

    @pl.when(i == 0)
    def _():
        for c in fetch(tg_ref[0]):
            c.start()

    @pl.when(live & (first_ref[i] != 0))
    def _():
        for c in fetch(g):
            c.wait()
        for j in range(EXP_PER_GROUP):
            wg_ref[j] = sg_ref[j].astype(bf16)
            wu_ref[j] = su_ref[j].astype(bf16)
            wd_ref[j] = sd_ref[j].astype(bf16)

        @pl.when(nxt_ref[i] >= 0)
        def _():
            for c in fetch(nxt_ref[i]):
                c.start(priority=1)

    @pl.when(live)
    def _():
        x = xs_ref[:, 0:D].astype(bf16)
        comb = xs_ref[:, D:D + LANES]
        lane = lax.broadcasted_iota(jnp.int32, comb.shape, 1)
        acc = None
        for j in (ta_ref[i], tb_ref[i]):
            w_j = jnp.sum(jnp.where(lane == g * EXP_PER_GROUP + j, comb, 0.0), axis=1, keepdims=True)
            act = _silu(_dot(x, wg_ref[j])) * _dot(x, wu_ref[j]) * w_j
            part = _dot(act.astype(bf16), wd_ref[j])
            acc = part if acc is None else acc + part
        y_ref[...] = acc

    @pl.when(i >= nt_ref[0])
    def _():
        y_ref[...] = jnp.zeros_like(y_ref)


def _experts(l, plan, xs, wg, wu, wd):
    def tile(i, *prefetch):
        return jnp.minimum(i, prefetch[-1][0] - 1)

    hbm = pl.BlockSpec(memory_space=pl.ANY)
    return pl.pallas_call(
        functools.partial(_experts_kernel, l),
        grid_spec=pltpu.PrefetchScalarGridSpec(
            num_scalar_prefetch=6,
            grid=(MOE_TILES,),
            in_specs=[pl.BlockSpec((MOE_TR, MOE_W), lambda i, *p: (tile(i, *p), 0)), hbm, hbm, hbm],
            out_specs=pl.BlockSpec((MOE_TR, D), lambda i, *p: (i, 0)),
            scratch_shapes=[pltpu.VMEM((EXP_PER_GROUP, D, D_EXP), bf16),
                            pltpu.VMEM((EXP_PER_GROUP, D, D_EXP), bf16),
                            pltpu.VMEM((EXP_PER_GROUP, D_EXP, D), bf16),
                            pltpu.VMEM((EXP_PER_GROUP, D, D_EXP), f32),
                            pltpu.VMEM((EXP_PER_GROUP, D, D_EXP), f32),
                            pltpu.VMEM((EXP_PER_GROUP, D_EXP, D), f32), pltpu.SemaphoreType.DMA((3,))],
        ),
        out_shape=jax.ShapeDtypeStruct((MOE_ROWS, D), f32),
        compiler_params=pltpu.CompilerParams(
            dimension_semantics=("arbitrary",), vmem_limit_bytes=V7X_VMEM_LIMIT),
        name="moe_experts",
    )(*plan, xs, wg, wu, wd)


def _combine_kernel(pos_ref, posn_ref, x1_ref, mod_ref, y_ref, oc_ref, ol_ref, ya_ref, yb_ref, sems):
    tm = x1_ref.shape[0]
    i = pl.program_id(0)
    n_ctx = T_CTX // tm

    def pull(p_ref, buf, sem):
        for r in range(tm):
            _row_copy(y_ref, p_ref[0, 0, r], buf, r, sem).start(priority=r % 2)

    @pl.when(i == 0)
    def _():
        pull(pos_ref, ya_ref, sems.at[0])

    def step(cur, nxt, sem_cur, sem_nxt):
        @pl.when(i + 1 < pl.num_programs(0))
        def _():
            pull(posn_ref, nxt, sem_nxt)

        pltpu.make_async_copy(y_ref.at[pl.ds(0, tm)], cur, sem_cur).wait()
        out = x1_ref[...] + mod_ref[5:6, :] * cur[...]

        @pl.when(i < n_ctx)
        def _():
            oc_ref[...] = out

        @pl.when(i >= n_ctx)
        def _():
            ol_ref[...] = out

    @pl.when(i % 2 == 0)
    def _():
        step(ya_ref, yb_ref, sems.at[0], sems.at[1])

    @pl.when(i % 2 == 1)
    def _():
        step(yb_ref, ya_ref, sems.at[1], sems.at[0])


def _combine(l, pos_blocks, x1, mod, y):
    tm = TM_DISP
    n = T // tm
    return pl.pallas_call(
        _combine_kernel,
        grid=(n,),
        in_specs=[pl.BlockSpec((1, 1, tm), lambda i: (i, 0, 0), memory_space=pltpu.SMEM),
                  pl.BlockSpec((1, 1, tm), lambda i: (jnp.minimum(i + 1, n - 1), 0, 0),
                               memory_space=pltpu.SMEM),
                  pl.BlockSpec((tm, D), lambda i: (i, 0)),
                  pl.BlockSpec((None, None, 6, D), lambda i: (l, _cond_of_tile(i, tm), 0, 0)),
                  pl.BlockSpec(memory_space=pl.ANY)],
        out_specs=_two_stream_specs(tm, D),
        out_shape=[jax.ShapeDtypeStruct((T_CTX, D), f32), jax.ShapeDtypeStruct((T_LAT, D), f32)],
        scratch_shapes=[pltpu.VMEM((tm, D), f32), pltpu.VMEM((tm, D), f32),
                        pltpu.SemaphoreType.DMA((2,))],
        compiler_params=pltpu.CompilerParams(
            dimension_semantics=("arbitrary",), vmem_limit_bytes=V7X_VMEM_LIMIT),
        name="moe_combine",
    )(pos_blocks, pos_blocks, x1, mod, y)


def _plan_kernel(cnt_ref, off_ref, tg_ref, ta_ref, tb_ref, first_ref, nxt_ref, nt_ref, zf_ref, nxtg_ref):
    def bucket(b, start):
        n = lax.shift_right_logical(cnt_ref[b] + (MOE_TR - 1), MOE_TR.bit_length() - 1)
        off_ref[b] = start * MOE_TR
        g = b // PAIRS
        p = b - g * PAIRS
        lo = jnp.where(p >= 3, 1, 0) + jnp.where(p >= 5, 1, 0)
        hi = p + 1 - jnp.where(lo >= 1, 2, 0) - jnp.where(lo >= 2, 1, 0)

        def tile(t, c):
            tg_ref[t] = g
            ta_ref[t] = lo
            tb_ref[t] = hi
            zf_ref[t] = jnp.where(t == start + n - 1, 1, 0)
            return c

        lax.fori_loop(start, start + n, tile, 0)
        return start + n

    nt = lax.fori_loop(0, N_BUCKETS, bucket, jnp.int32(0))
    nt_ref[0] = nt

    def unused(t, c):
        tg_ref[t] = N_GROUPS - 1
        ta_ref[t] = 0
        tb_ref[t] = 1
        zf_ref[t] = 1
        first_ref[t] = 0
        nxt_ref[t] = -1
        return c

    lax.fori_loop(nt, MOE_TILES, unused, 0)

    def back(i, nxt_group):
        t = nt - 1 - i
        g = tg_ref[t]
        later = jnp.where(t + 1 < nt, tg_ref[jnp.minimum(t + 1, MOE_TILES - 1)], -1)
        is_last_of_group = later != g
        nxt_group = jnp.where(is_last_of_group, later, nxt_group)
        nxtg_ref[g] = nxt_group
        return nxt_group

    lax.fori_loop(0, nt, back, jnp.int32(-1))

    def fwd(t, c):
        g = tg_ref[t]
        first_ref[t] = jnp.where((t == 0) | (tg_ref[jnp.maximum(t - 1, 0)] != g), 1, 0)
        nxt_ref[t] = nxtg_ref[g]
        return c

    lax.fori_loop(0, nt, fwd, 0)


def _moe_plan(route, counts):
    smem = pl.BlockSpec(memory_space=pltpu.SMEM)
    tile_i32 = jax.ShapeDtypeStruct((MOE_TILES,), jnp.int32)
    off, tg, ta, tb, first, nxt, n_tiles, zero_flags = pl.pallas_call(
        _plan_kernel,
        in_specs=[smem],
        out_specs=[smem] * 8,
        out_shape=[jax.ShapeDtypeStruct((N_BUCKETS,), jnp.int32), tile_i32, tile_i32, tile_i32, tile_i32,
                   tile_i32, jax.ShapeDtypeStruct((1,), jnp.int32), tile_i32],
        scratch_shapes=[pltpu.SMEM((N_GROUPS,), jnp.int32)],
        name="moe_plan",
    )(counts[0:N_BUCKETS, 0].astype(jnp.int32))
    buckets = jnp.arange(N_BUCKETS, dtype=f32)[:, None]
    pos = jnp.sum(jnp.where(route[0][None, :] == buckets, off.astype(f32)[:, None], 0.0), axis=0) + route[1]
    pos_blocks = pos.astype(jnp.int32).reshape(T // TM_DISP, 1, TM_DISP)
    return pos_blocks, zero_flags, (tg, ta, tb, first, nxt, n_tiles)


def _rope_tables():
    pos = np.arange(DEC_SEQ)
    half = HD_A // 4
    freqs = ROPE_BASE ** (-np.arange(half, dtype=np.float64) / half)
    ang_r = (pos // GRID_W)[:, None] * freqs[None, :]
    ang_c = (pos % GRID_W)[:, None] * freqs[None, :]
    ang = np.concatenate([ang_r, ang_r, ang_c, ang_c], axis=1)
    sign = np.concatenate([-np.ones(half), np.ones(half)] * 2)[None, :]
    cos = np.tile(np.cos(ang), (1, 2))
    sin = np.tile(np.sin(ang) * sign, (1, 2))
    ident_c = np.ones((TM_PROJ, LANES))
    ident_s = np.zeros((TM_PROJ, LANES))
    return (jnp.asarray(np.concatenate([cos, ident_c]), f32),
            jnp.asarray(np.concatenate([sin, ident_s]), f32))


def kernel(x_prompt, x_sample, c, cache_k, cache_v, state_ret_fwd, state_ret_bwd, c_ctx,
           norm1_w, norm2_w, w_ada, b_ada, w_in, q_norm_w, k_norm_w, attn_sink,
           ret_decay_fwd, ret_decay_bwd, ret_gn_w, w_pa, w_pb, w_o, w_router, b_router,
           w_exp_gate, w_exp_up, w_exp_down):
    x_ctx, x_lat = x_prompt.reshape(T_CTX, D), x_sample.reshape(T_LAT, D)
    cond8 = jnp.zeros((N_COND, D), f32).at[0].set(c_ctx).at[1:1 + DEC_BATCH].set(c)
    mod = _adaln(cond8, w_ada, b_ada).reshape(DEPTH, N_COND, 6, D)

    cos_t, sin_t = _rope_tables()
    blk = np.arange(Q_A) // HD_A
    ones_blk = jnp.asarray((blk[:, None] == blk[None, :]) / HD_A, bf16)
    tri = jnp.asarray(np.triu(np.ones((POST_SUB, POST_SUB)), 1), bf16)
    qnw = jnp.tile(q_norm_w, (1, NH_A)) * (HD_A ** -0.5)
    knw = jnp.tile(k_norm_w, (1, NKV_A))
    wr_pad = jnp.concatenate(_split(jnp.pad(w_router, ((0, 0), (0, LANES - N_EXP)))), axis=1)
    br_col = jnp.broadcast_to(b_router[:, None], (N_EXP, LANES))
    ck = cache_k.reshape(DEC_BATCH, DEPTH, PAST, KV_A)
    cv = cache_v.reshape(DEC_BATCH, DEPTH, PAST, KV_A)
    s0 = jnp.concatenate([state_ret_fwd, state_ret_bwd], axis=3)
    lg = jnp.log1p(-jnp.exp(jnp.concatenate([ret_decay_fwd, ret_decay_bwd], axis=1).astype(f32)))

    new_k, new_v, new_sf, new_sb = [], [], [], []
    for l in range(DEPTH):
        qa, ka, va, qr, kr, vr, gr, ga, gb = _inproj(l, x_ctx, x_lat, norm1_w, mod, w_in, cos_t, sin_t,
                                                     ones_blk, qnw, knw)
        oa_ctx = _ctx_attn(l, qa, ka, va, attn_sink)
        oa_lat = _lat_attn(l, qa, ka, va, ck, cv, cos_t, sin_t, attn_sink)
        yr_ctx, sf, sb = _retention(l, qr, kr, vr, gr, None, lg, ret_gn_w, BATCH, SEQ, 0,
                                    CTX_SEQS_PER_STEP)
        yr_lat, _, _ = _retention(l, qr, kr, vr, gr, s0[:, l], lg, ret_gn_w, DEC_BATCH, DEC_SEQ,
                                  T_CTX // DEC_SEQ, 1)
        x1, h2, route, counts = _post(l, x_ctx, x_lat, oa_ctx, oa_lat, yr_ctx, yr_lat, ga, gb,
                                      w_pa, w_pb, w_o, mod, norm2_w, wr_pad, br_col, tri)
        pos_blocks, zero_flags, plan = _moe_plan(route, counts)
        xs = _dispatch(zero_flags, pos_blocks, h2)
        y = _experts(l, plan, xs, w_exp_gate, w_exp_up, w_exp_down)
        x_ctx, x_lat = _combine(l, pos_blocks, x1, mod, y)

        new_k.append(ka[:T_CTX].reshape(BATCH, SEQ, NKV_A, HD_A))
        new_v.append(va[:T_CTX].reshape(BATCH, SEQ, NKV_A, HD_A))
        new_sf.append(sf)
        new_sb.append(sb)

    return (x_ctx.reshape(BATCH, SEQ, D), x_lat.reshape(DEC_BATCH, DEC_SEQ, D),
            jnp.stack(new_k, axis=1), jnp.stack(new_v, axis=1),
            jnp.stack(new_sf, axis=1), jnp.stack(new_sb, axis=1))
```

```python
import functools

import numpy as np
import jax
import jax.numpy as jnp
from jax import lax
from jax.experimental import pallas as pl
from jax.experimental.pallas import tpu as pltpu

D = 1024
BATCH, SEQ = 16, 256
DEC_BATCH, DEC_SEQ = 2, 2048
DEPTH = 2
PAST = 512
GRID_W = 64
NH_A, NKV_A, HD_A = 8, 2, 64
WINDOW = 128
NH_R, DK_R, DV_R = 4, 64, 128
CHUNK = 128
N_EXP, N_GROUPS, EXP_PER_GROUP = 16, 4, 4
D_EXP = 512
ROPE_BASE = 10000.0
EPS = 1e-6
NEG_INF = -1e30

Q_A = NH_A * HD_A
KV_A = NKV_A * HD_A
QK_R = NH_R * DK_R
V_R = NH_R * DV_R
C_QA = (0, Q_A)
C_KA = (C_QA[1], C_QA[1] + KV_A)
C_VA = (C_KA[1], C_KA[1] + KV_A)
C_QR = (C_VA[1], C_VA[1] + QK_R)
C_KR = (C_QR[1], C_QR[1] + QK_R)
C_VR = (C_KR[1], C_KR[1] + V_R)
C_GR = (C_VR[1], C_VR[1] + V_R)
C_GA = (C_GR[1], C_GR[1] + D)
C_GB = (C_GA[1], C_GA[1] + D)
D_IN = C_GB[1]

T_CTX = BATCH * SEQ
T_LAT = DEC_BATCH * DEC_SEQ
T = T_CTX + T_LAT
N_COND = 8

LANES = 128
MXU_TILE = 256
V7X_VMEM_LIMIT = 56 * 1024 * 1024

TM_PROJ = 512
TM_POST = 512
POST_SUB = 512
W_CHUNK = 256
TM_DISP = 512
MOE_TR = 256
PAIRS = EXP_PER_GROUP * (EXP_PER_GROUP - 1) // 2
N_BUCKETS = N_GROUPS * PAIRS
BUCKET_ROWS = 32
MOE_TILES = T // MOE_TR + N_BUCKETS
MOE_ROWS = MOE_TILES * MOE_TR
MOE_W = D + LANES
ATT_QB = 256
RET_UNROLL = 4
CTX_SEQS_PER_STEP = 4

f32 = jnp.float32
bf16 = jnp.bfloat16


def _dot(a, b):
    return jnp.dot(a, b, preferred_element_type=f32)


def _dot_t(a, b):
    return lax.dot_general(a, b, (((1,), (1,)), ((), ())), preferred_element_type=f32)


def _split(x):
    hi = x.astype(bf16)
    lo = (x - hi.astype(f32)).astype(bf16)
    return hi, lo


def _dot3(a, b):
    ah, al = _split(a)
    bh, bl = _split(b)
    return _dot(ah, bh) + (_dot(ah, bl) + _dot(al, bh))


def _sigmoid(x):
    return 1.0 / (1.0 + jnp.exp(-x))


def _silu(x):
    return x * _sigmoid(x)


def _layer_row(l, n):
    return pl.BlockSpec((None, 1, n), lambda *_: (l, 0, 0))


def _cond_of_tile(i, tm):
    n_ctx = T_CTX // tm
    per_b = DEC_SEQ // tm
    return jnp.where(i < n_ctx, 0, 1 + jnp.maximum(i - n_ctx, 0) // per_b)


def _adaln_kernel(cond_ref, w_ref, b_ref, o_ref):
    a = _silu(cond_ref[...])
    o_ref[0] = _dot3(a, w_ref[0]) + b_ref[0]


def _adaln(cond8, w_ada, b_ada):
    tn = 2048
    return pl.pallas_call(
        _adaln_kernel,
        grid=(DEPTH, 6 * D // tn),
        in_specs=[
            pl.BlockSpec((N_COND, D), lambda l, j: (0, 0)),
            pl.BlockSpec((1, D, tn), lambda l, j: (l, 0, j)),
            pl.BlockSpec((1, 1, tn), lambda l, j: (l, 0, j)),
        ],
        out_specs=pl.BlockSpec((1, N_COND, tn), lambda l, j: (l, 0, j)),
        out_shape=jax.ShapeDtypeStruct((DEPTH, N_COND, 6 * D), f32),
        compiler_params=pltpu.CompilerParams(
            dimension_semantics=("parallel", "parallel"), vmem_limit_bytes=V7X_VMEM_LIMIT),
        name="adaln",
    )(cond8, w_ada, b_ada.reshape(DEPTH, 1, 6 * D))


def _rope(x, cos, sin_signed, first_half):
    fwd = pltpu.roll(x, 16, 1)
    bwd = pltpu.roll(x, LANES - 16, 1)
    partner = jnp.where(first_half, bwd, fwd)
    return x * cos + partner * sin_signed


def _head_rms(x, ones_blk, w):
    n = x.shape[1]
    sq_hi, sq_lo = _split(x * x)
    wt = min(n, MXU_TILE)
    blk = ones_blk[0:wt, 0:wt]
    parts = [_dot(sq_hi[:, c:c + wt], blk) + _dot(sq_lo[:, c:c + wt], blk) for c in range(0, n, wt)]
    mean = parts[0] if len(parts) == 1 else jnp.concatenate(parts, axis=1)
    return x * lax.rsqrt(mean + EPS) * w


def _load_cast(w_hbm, dst_ref, stage_ref, sems):
    k, n = w_hbm.shape

    def chunk_copy(c):
        return pltpu.make_async_copy(w_hbm.at[:, pl.ds(c * W_CHUNK, W_CHUNK)],
                                     stage_ref.at[c % 2, pl.ds(0, k)], sems.at[c % 2])

    n_chunks = n // W_CHUNK
    chunk_copy(0).start()
    for c in range(n_chunks):
        if c + 1 < n_chunks:
            chunk_copy(c + 1).start()
        chunk_copy(c).wait()
        dst_ref[:, c * W_CHUNK:(c + 1) * W_CHUNK] = stage_ref[c % 2, 0:k, :].astype(bf16)


def _tile_of_two(i, n_first, a_ref, b_ref):
    return jnp.where(i < n_first, a_ref[...], b_ref[...])


def _inproj_kernel(l, xc_ref, xl_ref, n1_ref, mod_ref, w_hbm, cos_ref, sin_ref, ones_ref, qnw_ref,
                   knw_ref, qa_ref, ka_ref, va_ref, qr_ref, kr_ref, vr_ref, gr_ref, ga_ref, gb_ref,
                   w_ref, stage_ref, sems):
    i = pl.program_id(0)

    @pl.when(i == 0)
    def _():
        _load_cast(w_hbm.at[l], w_ref, stage_ref, sems)

    x = _tile_of_two(i, T_CTX // xc_ref.shape[0], xc_ref, xl_ref)
    y = x * lax.rsqrt(jnp.mean(x * x, axis=-1, keepdims=True) + EPS) * n1_ref[...]
    h = (y * (1.0 + mod_ref[1:2, :]) + mod_ref[0:1, :]).astype(bf16)

    whole = _dot(h, w_ref[...])

    def proj(c):
        return whole[:, c[0]:c[1]]

    cos = cos_ref[...]
    sin = sin_ref[...]
    lane = lax.broadcasted_iota(jnp.int32, cos.shape, 1)
    first_half = (lane % 32) < 16

    def rope_all(v):
        parts = [_rope(v[:, j:j + LANES], cos, sin, first_half) for j in range(0, v.shape[1], LANES)]
        return parts[0] if len(parts) == 1 else jnp.concatenate(parts, axis=1)

    ones_blk = ones_ref[...]
    qa = _head_rms(proj(C_QA), ones_blk, qnw_ref[...])
    qa_ref[...] = rope_all(qa).astype(bf16)
    ka_ref[...] = _head_rms(proj(C_KA), ones_blk, knw_ref[...])
    va_ref[...] = proj(C_VA)
    qr_ref[...] = rope_all(proj(C_QR)).astype(bf16)
    kr_ref[...] = (rope_all(proj(C_KR)) * (DK_R ** -0.5)).astype(bf16)
    vr_ref[...] = proj(C_VR).astype(bf16)
    gr_ref[...] = proj(C_GR).astype(bf16)
    ga_ref[...] = proj(C_GA).astype(bf16)
    gb_ref[...] = proj(C_GB).astype(bf16)


def _two_stream_specs(tm, width):
    n_ctx = T_CTX // tm
    return [pl.BlockSpec((tm, width), lambda i, *_: (jnp.minimum(i, n_ctx - 1), 0)),
            pl.BlockSpec((tm, width), lambda i, *_: (jnp.maximum(i - n_ctx, 0), 0))]


def _inproj(l, x_ctx, x_lat, norm1_w, mod, w_in, cos_t, sin_t, ones_blk, qnw, knw):
    tm = TM_PROJ
    n_ctx = T_CTX // tm
    per_b = DEC_SEQ // tm

    def tab_map(i):
        return (jnp.where(i < n_ctx, per_b, jnp.maximum(i - n_ctx, 0) % per_b), 0)

    row = lambda i: (i, 0)
    const = lambda i: (0, 0)
    widths = [(Q_A, bf16), (KV_A, f32), (KV_A, f32), (QK_R, bf16), (QK_R, bf16), (V_R, bf16),
              (V_R, bf16), (D, bf16), (D, bf16)]
    return pl.pallas_call(
        functools.partial(_inproj_kernel, l),
        grid=(T // tm,),
        in_specs=_two_stream_specs(tm, D) + [
            _layer_row(l, D),
            pl.BlockSpec((None, None, 6, D), lambda i: (l, _cond_of_tile(i, tm), 0, 0)),
            pl.BlockSpec(memory_space=pl.ANY),
            pl.BlockSpec((tm, LANES), tab_map),
            pl.BlockSpec((tm, LANES), tab_map),
            pl.BlockSpec((Q_A, Q_A), const),
            _layer_row(l, Q_A),
            _layer_row(l, KV_A),
        ],
        out_specs=[pl.BlockSpec((tm, w), row) for w, _ in widths],
        out_shape=[jax.ShapeDtypeStruct((T, w), dt) for w, dt in widths],
        scratch_shapes=[pltpu.VMEM((D, D_IN), bf16), pltpu.VMEM((2, D, W_CHUNK), f32),
                        pltpu.SemaphoreType.DMA((2,))],
        compiler_params=pltpu.CompilerParams(
            dimension_semantics=("arbitrary",), vmem_limit_bytes=V7X_VMEM_LIMIT),
        name="inproj",
    )(x_ctx, x_lat, norm1_w, mod, w_in, cos_t, sin_t, ones_blk, qnw, knw)


def _head_blocks(t, kv, lo_mask):
    r = pltpu.roll(t, HD_A, 1)
    if kv == 0:
        a = jnp.where(lo_mask, t, 0.0)
        b = jnp.where(lo_mask, 0.0, r)
    else:
        a = jnp.where(lo_mask, r, 0.0)
        b = jnp.where(lo_mask, 0.0, t)
    return jnp.concatenate([a, b], axis=0).astype(bf16)


def _ctx_attn_kernel(l, sink_ref, q_ref, k_ref, v_ref, o_ref):
    n = SEQ
    lo_mask = lax.broadcasted_iota(jnp.int32, (n, LANES), 1) < HD_A
    for b in range(CTX_SEQS_PER_STEP):
        rows = slice(b * n, (b + 1) * n)
        k = k_ref[rows, :]
        v = v_ref[rows, :]
        for kv in range(NKV_A):
            kblk = _head_blocks(k, kv, lo_mask)
            vblk = _head_blocks(v, kv, lo_mask)
            q2 = jnp.concatenate([q_ref[rows, (2 * kv) * LANES:(2 * kv + 1) * LANES],
                                  q_ref[rows, (2 * kv + 1) * LANES:(2 * kv + 2) * LANES]], axis=0)
            s = _dot_t(q2, kblk)
            row = lax.broadcasted_iota(jnp.int32, (2 * n, 1), 0)
            ps, invs = [], []
            for hh in range(2):
                sk = jnp.where(row < n, sink_ref[l, 4 * kv + hh], sink_ref[l, 4 * kv + 2 + hh])
                sh = s[:, hh * n:(hh + 1) * n]
                m = jnp.maximum(jnp.max(sh, axis=-1, keepdims=True), sk)
                p = jnp.exp(sh - m)
                invs.append(1.0 / (jnp.sum(p, axis=-1, keepdims=True) + jnp.exp(sk - m)))
                ps.append(p.astype(bf16))
            lo2 = lax.broadcasted_iota(jnp.int32, (2 * n, LANES), 1) < HD_A
            o = _dot(jnp.concatenate(ps, axis=1), vblk) * jnp.where(lo2, invs[0], invs[1])
            o_ref[rows, (2 * kv) * LANES:(2 * kv + 1) * LANES] = o[:n].astype(bf16)
            o_ref[rows, (2 * kv + 1) * LANES:(2 * kv + 2) * LANES] = o[n:].astype(bf16)


def _ctx_attn(l, qa, ka, va, sink):
    blk = lambda w: pl.BlockSpec((CTX_SEQS_PER_STEP * SEQ, w), lambda b: (b, 0))
    return pl.pallas_call(
        functools.partial(_ctx_attn_kernel, l),
        grid=(BATCH // CTX_SEQS_PER_STEP,),
        in_specs=[pl.BlockSpec(memory_space=pltpu.SMEM), blk(Q_A), blk(KV_A), blk(KV_A)],
        out_specs=blk(Q_A),
        out_shape=jax.ShapeDtypeStruct((T_CTX, Q_A), bf16),
        compiler_params=pltpu.CompilerParams(
            dimension_semantics=("parallel",), vmem_limit_bytes=V7X_VMEM_LIMIT),
        name="ctx_attn",
    )(sink, qa, ka, va)


def _lat_attn_kernel(l, sink_ref, q_ref, k_ref, v_ref, kc_ref, vc_ref, cos_ref, sin_ref, o_ref):
    j = pl.program_id(1)
    qb = ATT_QB
    win = 2 * qb
    ws = pl.multiple_of(jnp.clip(j * qb - WINDOW, 0, DEC_SEQ - win), WINDOW)
    lo_mask = lax.broadcasted_iota(jnp.int32, (win, LANES), 1) < HD_A
    lane = lax.broadcasted_iota(jnp.int32, (win, LANES), 1)
    kw = _rope(k_ref[pl.ds(ws, win), :], cos_ref[pl.ds(ws, win), :], sin_ref[pl.ds(ws, win), :],
               (lane % 32) < 16)
    vw = v_ref[pl.ds(ws, win), :]
    kc = kc_ref[...]
    vc = vc_ref[...]
    qpos = j * qb + (lax.broadcasted_iota(jnp.int32, (2 * qb, win), 0) & (qb - 1))
    kpos = ws + lax.broadcasted_iota(jnp.int32, (2 * qb, win), 1)
    valid = jnp.abs(qpos - kpos) <= WINDOW
    out_lo = lax.broadcasted_iota(jnp.int32, (2 * qb, LANES), 1) < HD_A
    for kv in range(NKV_A):
        kc_blk = _head_blocks(kc, kv, lo_mask[:PAST])
        vc_blk = _head_blocks(vc, kv, lo_mask[:PAST])
        kw_blk = _head_blocks(kw, kv, lo_mask)
        vw_blk = _head_blocks(vw, kv, lo_mask)
        q2 = jnp.concatenate([q_ref[:, (2 * kv) * LANES:(2 * kv + 1) * LANES],
                              q_ref[:, (2 * kv + 1) * LANES:(2 * kv + 2) * LANES]], axis=0)
        s_c = _dot_t(q2, kc_blk)
        s_w = _dot_t(q2, kw_blk)
        pcs, pws, invs = [], [], []
        for hh in range(2):
            row = lax.broadcasted_iota(jnp.int32, (2 * qb, 1), 0)
            sk = jnp.where(row < qb, sink_ref[l, 4 * kv + hh], sink_ref[l, 4 * kv + 2 + hh])
            sc = s_c[:, hh * PAST:(hh + 1) * PAST]
            sw = jnp.where(valid, s_w[:, hh * win:(hh + 1) * win], NEG_INF)
            m = jnp.maximum(jnp.maximum(jnp.max(sc, axis=-1, keepdims=True),
                                        jnp.max(sw, axis=-1, keepdims=True)), sk)
            pc = jnp.exp(sc - m)
            pw = jnp.exp(sw - m)
            den = (jnp.sum(pc, axis=-1, keepdims=True) + jnp.sum(pw, axis=-1, keepdims=True)
                   + jnp.exp(sk - m))
            invs.append(1.0 / den)
            pcs.append(pc.astype(bf16))
            pws.append(pw.astype(bf16))
        o = _dot(jnp.concatenate(pcs, axis=1), vc_blk) + _dot(jnp.concatenate(pws, axis=1), vw_blk)
        o = o * jnp.where(out_lo, invs[0], invs[1])
        o_ref[:, (2 * kv) * LANES:(2 * kv + 1) * LANES] = o[:qb].astype(bf16)
        o_ref[:, (2 * kv + 1) * LANES:(2 * kv + 2) * LANES] = o[qb:].astype(bf16)


def _lat_attn(l, qa, ka, va, cache_k, cache_v, cos_l, sin_l, sink):
    qb = ATT_QB
    nq = DEC_SEQ // qb
    ctx_blocks = T_CTX // DEC_SEQ
    seq = lambda b, j: (ctx_blocks + b, 0)
    return pl.pallas_call(
        functools.partial(_lat_attn_kernel, l),
        grid=(DEC_BATCH, nq),
        in_specs=[
            pl.BlockSpec(memory_space=pltpu.SMEM),
            pl.BlockSpec((qb, Q_A), lambda b, j: (T_CTX // qb + b * nq + j, 0)),
            pl.BlockSpec((DEC_SEQ, KV_A), seq),
            pl.BlockSpec((DEC_SEQ, KV_A), seq),
            pl.BlockSpec((None, None, PAST, KV_A), lambda b, j: (b, l, 0, 0)),
            pl.BlockSpec((None, None, PAST, KV_A), lambda b, j: (b, l, 0, 0)),
            pl.BlockSpec((DEC_SEQ, LANES), lambda b, j: (0, 0)),
            pl.BlockSpec((DEC_SEQ, LANES), lambda b, j: (0, 0)),
        ],
        out_specs=pl.BlockSpec((qb, Q_A), lambda b, j: (b * nq + j, 0)),
        out_shape=jax.ShapeDtypeStruct((T_LAT, Q_A), bf16),
        compiler_params=pltpu.CompilerParams(
            dimension_semantics=("parallel", "parallel"), vmem_limit_bytes=V7X_VMEM_LIMIT),
        name="lat_attn",
    )(sink, qa, ka, va, cache_k, cache_v, cos_l, sin_l)


def _dup_heads(pair, lo_mask):
    r = pltpu.roll(pair, DK_R, 1)
    return jnp.where(lo_mask, pair, r), jnp.where(lo_mask, r, pair)


def _retention_kernel(l, has_s0, n_seq, n_chunks, *refs):
    if has_s0:
        (lg_ref, q_ref, k_ref, v_ref, g_ref, s0_ref, gnw_ref,
         y_ref, sf_ref, sb_ref, ds_ref, st_ref, mask_ref, qdec_ref, kdec_ref, cdec_ref) = refs
    else:
        (lg_ref, q_ref, k_ref, v_ref, g_ref, gnw_ref,
         y_ref, sf_ref, sb_ref, ds_ref, st_ref, mask_ref, qdec_ref, kdec_ref, cdec_ref) = refs
        s0_ref = None
    C = CHUNK
    lo_mask = lax.broadcasted_iota(jnp.int32, (C, LANES), 1) < DK_R

    ri = lax.broadcasted_iota(jnp.int32, (C, C), 0).astype(f32)
    diff = ri - lax.broadcasted_iota(jnp.int32, (C, C), 1).astype(f32)
    for h in range(NH_R):
        lg_f, lg_b = lg_ref[l, h], lg_ref[l, NH_R + h]
        mask_ref[h] = (jnp.where(diff >= 0, jnp.exp(jnp.maximum(diff, 0.0) * lg_f), 0.0)
                       + jnp.where(diff <= 0, jnp.exp(jnp.maximum(-diff, 0.0) * lg_b), 0.0))
        qdec_ref[:, h * LANES:(h + 1) * LANES] = jnp.exp(
            jnp.where(lo_mask, (ri + 1.0) * lg_f, (C - ri) * lg_b))
        kdec_ref[:, h * LANES:(h + 1) * LANES] = jnp.exp(
            jnp.where(lo_mask, (C - 1.0 - ri) * lg_f, ri * lg_b))
        cdec_ref[h] = jnp.exp(jnp.where(ri < DK_R, C * lg_f, C * lg_b))

    def inc_body(c, carry):
        r0 = pl.multiple_of(c * C, C)
        for pr in range(2):
            kp = k_ref[pl.ds(r0, C), pr * LANES:(pr + 1) * LANES].astype(f32)
            for hh, kd in enumerate(_dup_heads(kp, lo_mask)):
                h = 2 * pr + hh
                kd = (kd * kdec_ref[:, h * LANES:(h + 1) * LANES]).astype(bf16)
                vh = v_ref[pl.ds(r0, C), h * DV_R:(h + 1) * DV_R]
                ds_ref[c, h] = lax.dot_general(kd, vh, (((0,), (0,)), ((), ())),
                                               preferred_element_type=f32)
        return carry

    lax.fori_loop(0, n_seq * n_chunks, inc_body, 0, unroll=RET_UNROLL)

    for sq in range(n_seq):
        base = sq * n_chunks
        for h in range(NH_R):
            cf = cdec_ref[h, 0:DK_R, :]
            cb = cdec_ref[h, DK_R:2 * DK_R, :]
            if has_s0:
                init_f = s0_ref[h, 0:DK_R, :]
                init_b = s0_ref[h, DK_R:2 * DK_R, :]
            else:
                init_f = jnp.zeros((DK_R, DV_R), f32)
                init_b = init_f

            def fwd_body(i, s, h=h, cf=cf, base=base):
                c = base + i
                st_ref[c, h, 0:DK_R, :] = s
                return s * cf + ds_ref[c, h, 0:DK_R, :]

            def bwd_body(i, s, h=h, cb=cb, base=base):
                c = base + n_chunks - 1 - i
                st_ref[c, h, DK_R:2 * DK_R, :] = s
                return s * cb + ds_ref[c, h, DK_R:2 * DK_R, :]

            sf_ref[sq, h] = lax.fori_loop(0, n_chunks, fwd_body, init_f)
            sb_ref[sq, h] = lax.fori_loop(0, n_chunks, bwd_body, init_b)

    def out_body(c, carry):
        r0 = pl.multiple_of(c * C, C)
        for pr in range(2):
            qp = q_ref[pl.ds(r0, C), pr * LANES:(pr + 1) * LANES]
            kp = k_ref[pl.ds(r0, C), pr * LANES:(pr + 1) * LANES].astype(f32)
            kblk = jnp.concatenate([jnp.where(lo_mask, kp, 0.0), jnp.where(lo_mask, 0.0, kp)],
                                   axis=0).astype(bf16)
            a2 = _dot_t(qp, kblk)
            for hh, qd in enumerate(_dup_heads(qp.astype(f32), lo_mask)):
                h = 2 * pr + hh
                a = (a2[:, hh * C:(hh + 1) * C] * mask_ref[h]).astype(bf16)
                vh = v_ref[pl.ds(r0, C), h * DV_R:(h + 1) * DV_R]
                qd = (qd * qdec_ref[:, h * LANES:(h + 1) * LANES]).astype(bf16)
                o = _dot(a, vh) + _dot(qd, st_ref[c, h].astype(bf16))
                mu = jnp.mean(o, axis=-1, keepdims=True)
                d = o - mu
                var = jnp.mean(d * d, axis=-1, keepdims=True)
                yh = d * lax.rsqrt(var + EPS) * gnw_ref[:, h * DV_R:(h + 1) * DV_R]
                g = g_ref[pl.ds(r0, C), h * DV_R:(h + 1) * DV_R].astype(f32)
                y_ref[pl.ds(r0, C), h * DV_R:(h + 1) * DV_R] = (yh * _silu(g)).astype(bf16)
        return carry

    lax.fori_loop(0, n_seq * n_chunks, out_body, 0, unroll=RET_UNROLL)


def _retention(l, qr, kr, vr, gr, s0, lg, gnw, nb, seq, row_block0, n_seq):
    n_chunks = seq // CHUNK
    has_s0 = s0 is not None
    rows = n_seq * seq
    tok = lambda w: pl.BlockSpec((rows, w), lambda b: (row_block0 + b, 0))
    in_specs = [pl.BlockSpec(memory_space=pltpu.SMEM), tok(QK_R), tok(QK_R), tok(V_R), tok(V_R)]
    args = [lg, qr, kr, vr, gr]
    if has_s0:
        in_specs.append(pl.BlockSpec((None, None, NH_R, 2 * DK_R, DV_R), lambda b: (b, l, 0, 0, 0)))
        args.append(s0)
    in_specs += [_layer_row(l, V_R)]
    args += [gnw]
    st_spec = pl.BlockSpec((n_seq, NH_R, DK_R, DV_R), lambda b: (b, 0, 0, 0))
    return pl.pallas_call(
        functools.partial(_retention_kernel, l, has_s0, n_seq, n_chunks),
        grid=(nb // n_seq,),
        in_specs=in_specs,
        out_specs=[pl.BlockSpec((rows, V_R), lambda b: (b, 0)), st_spec, st_spec],
        out_shape=[jax.ShapeDtypeStruct((nb * seq, V_R), bf16),
                   jax.ShapeDtypeStruct((nb, NH_R, DK_R, DV_R), f32),
                   jax.ShapeDtypeStruct((nb, NH_R, DK_R, DV_R), f32)],
        scratch_shapes=[pltpu.VMEM((n_seq * n_chunks, NH_R, 2 * DK_R, DV_R), f32),
                        pltpu.VMEM((n_seq * n_chunks, NH_R, 2 * DK_R, DV_R), f32),
                        pltpu.VMEM((NH_R, CHUNK, CHUNK), f32), pltpu.VMEM((CHUNK, NH_R * LANES), f32),
                        pltpu.VMEM((CHUNK, NH_R * LANES), f32), pltpu.VMEM((NH_R, 2 * DK_R, DV_R), f32)],
        compiler_params=pltpu.CompilerParams(
            dimension_semantics=("parallel",), vmem_limit_bytes=V7X_VMEM_LIMIT),
        name="retention_lat" if has_s0 else "retention_ctx",
    )(*args)


def _post_kernel(l, xc_ref, xl_ref, oac_ref, oal_ref, yrc_ref, yrl_ref, ga_ref, gb_ref, wpa_hbm, wpb_hbm,
                 wo_hbm, mod_ref, n2_ref, wr_ref, br_ref, tri_ref, x1_ref, h2_ref, route_ref, cnt_ref,
                 carry_ref, wpa_ref, wpb_ref, wo_ref, stage_ref, sems):
    i = pl.program_id(0)
    tm = xc_ref.shape[0]
    n_ctx = T_CTX // tm

    @pl.when(i == 0)
    def _():
        carry_ref[...] = jnp.zeros_like(carry_ref)
        _load_cast(wpa_hbm.at[l], wpa_ref, stage_ref, sems)
        _load_cast(wpb_hbm.at[l], wpb_ref, stage_ref, sems)
        _load_cast(wo_hbm.at[l], wo_ref, stage_ref, sems)

    for r0 in range(0, tm, POST_SUB):
        _post_subtile(i < n_ctx, slice(r0, r0 + POST_SUB), xc_ref, xl_ref, oac_ref, oal_ref, yrc_ref,
                      yrl_ref, ga_ref, gb_ref, mod_ref, n2_ref, wr_ref, br_ref, tri_ref, x1_ref, h2_ref,
                      route_ref, carry_ref, wpa_ref, wpb_ref, wo_ref)
    cnt_ref[...] = carry_ref[...]


def _post_subtile(is_ctx, rows, xc_ref, xl_ref, oac_ref, oal_ref, yrc_ref, yrl_ref, ga_ref, gb_ref, mod_ref,
                  n2_ref, wr_ref, br_ref, tri_ref, x1_ref, h2_ref, route_ref, carry_ref, wpa_ref, wpb_ref,
                  wo_ref):
    tm = rows.stop - rows.start
    pick = lambda a_ref, b_ref: jnp.where(is_ctx, a_ref[rows, :], b_ref[rows, :])
    ga = _sigmoid(ga_ref[rows, :].astype(f32))
    gb = _sigmoid(gb_ref[rows, :].astype(f32))
    merged = (ga * _dot(pick(oac_ref, oal_ref), wpa_ref[...])
              + gb * _dot(pick(yrc_ref, yrl_ref), wpb_ref[...]))
    mix = _dot(merged.astype(bf16), wo_ref[...])
    x1 = pick(xc_ref, xl_ref) + mod_ref[2:3, :] * mix
    x1_ref[rows, :] = x1
    y = x1 * lax.rsqrt(jnp.mean(x1 * x1, axis=-1, keepdims=True) + EPS) * n2_ref[...]
    h2 = y * (1.0 + mod_ref[4:5, :]) + mod_ref[3:4, :]
    h2_ref[rows, 0:D] = h2

    h_hi, h_lo = _split(h2)
    both = _dot(h_hi, wr_ref[...])
    logits = both[:, 0:LANES] + (both[:, LANES:2 * LANES] + _dot(h_lo, wr_ref[:, 0:LANES]))
    lt = logits.T[0:N_EXP, :]
    scores = _sigmoid(lt)
    sel = scores + br_ref[:, 0:1]
    row = lax.broadcasted_iota(jnp.int32, (N_EXP, tm), 0)

    best = None
    bg = None
    for g in range(N_GROUPS):
        a, b, c, d = (sel[EXP_PER_GROUP * g + k:EXP_PER_GROUP * g + k + 1, :] for k in range(4))
        p, q = jnp.maximum(a, b), jnp.minimum(a, b)
        r, s = jnp.maximum(c, d), jnp.minimum(c, d)
        gs = jnp.maximum(p, r) + jnp.maximum(jnp.minimum(p, r), jnp.maximum(q, s))
        if g == 0:
            best, bg = gs, jnp.zeros((1, tm), jnp.int32)
        else:
            upd = gs > best
            bg = jnp.where(upd, g, bg)
            best = jnp.where(upd, gs, best)
    masked = jnp.where(jnp.right_shift(row, 2) == bg, sel, NEG_INF)
    m1 = jnp.max(masked, axis=0, keepdims=True)
    i1 = jnp.min(jnp.where(masked == m1, row, N_EXP), axis=0, keepdims=True)
    masked2 = jnp.where(row == i1, NEG_INF, masked)
    m2 = jnp.max(masked2, axis=0, keepdims=True)
    i2 = jnp.min(jnp.where(masked2 == m2, row, N_EXP), axis=0, keepdims=True)
    oh1 = row == i1
    oh2 = row == i2
    s1 = jnp.sum(jnp.where(oh1, scores, 0.0), axis=0, keepdims=True)
    s2 = jnp.sum(jnp.where(oh2, scores, 0.0), axis=0, keepdims=True)
    den = s1 + s2

    comb = jnp.where(oh1, s1 / den, 0.0) + jnp.where(oh2, s2 / den, 0.0)
    comb_t = jnp.concatenate([comb, jnp.zeros((LANES - N_EXP, tm), f32)], axis=0).T
    h2_ref[rows, D:D + LANES] = comb_t

    e_lo = jnp.minimum(i1, i2) - EXP_PER_GROUP * bg
    e_hi = jnp.maximum(i1, i2) - EXP_PER_GROUP * bg
    pair = jnp.right_shift(e_lo * (7 - e_lo), 1) + (e_hi - e_lo - 1)
    bucket = bg * PAIRS + pair
    rowb = lax.broadcasted_iota(jnp.int32, (BUCKET_ROWS, tm), 0)
    ohb = rowb == bucket
    ohb_f = jnp.where(ohb, 1.0, 0.0)
    tot = carry_ref[:, 0:1] + _dot(ohb_f.astype(bf16), tri_ref[...])
    rank = jnp.sum(jnp.where(ohb, tot, 0.0), axis=0, keepdims=True)
    carry_ref[...] = carry_ref[...] + jnp.sum(ohb_f, axis=1, keepdims=True)

    route_ref[0:1, rows] = bucket.astype(f32)
    route_ref[1:2, rows] = rank
    route_ref[2:8, rows] = jnp.zeros((6, tm), f32)


def _post(l, x_ctx, x_lat, oa_ctx, oa_lat, yr_ctx, yr_lat, ga, gb, wpa, wpb, wo, mod, norm2_w, wr_pad,
          br_col, tri):
    tm = TM_POST
    row = lambda w: pl.BlockSpec((tm, w), lambda i: (i, 0))
    const = lambda a: pl.BlockSpec(a.shape, lambda i: (0,) * a.ndim)
    hbm = pl.BlockSpec(memory_space=pl.ANY)
    return pl.pallas_call(
        functools.partial(_post_kernel, l),
        grid=(T // tm,),
        in_specs=(_two_stream_specs(tm, D) + _two_stream_specs(tm, Q_A) + _two_stream_specs(tm, V_R)
                  + [row(D), row(D), hbm, hbm, hbm,
                     pl.BlockSpec((None, None, 6, D), lambda i: (l, _cond_of_tile(i, tm), 0, 0)),
                     _layer_row(l, D), const(wr_pad), const(br_col), const(tri)]),
        out_specs=[row(D), row(MOE_W), pl.BlockSpec((8, tm), lambda i: (0, i)),
                   pl.BlockSpec((BUCKET_ROWS, LANES), lambda i: (0, 0))],
        out_shape=[jax.ShapeDtypeStruct((T, D), f32), jax.ShapeDtypeStruct((T, MOE_W), f32),
                   jax.ShapeDtypeStruct((8, T), f32), jax.ShapeDtypeStruct((BUCKET_ROWS, LANES), f32)],
        scratch_shapes=[pltpu.VMEM((BUCKET_ROWS, LANES), f32), pltpu.VMEM((Q_A, D), bf16),
                        pltpu.VMEM((V_R, D), bf16), pltpu.VMEM((D, D), bf16),
                        pltpu.VMEM((2, D, W_CHUNK), f32), pltpu.SemaphoreType.DMA((2,))],
        compiler_params=pltpu.CompilerParams(
            dimension_semantics=("arbitrary",), vmem_limit_bytes=V7X_VMEM_LIMIT),
        name="post_router",
    )(x_ctx, x_lat, oa_ctx, oa_lat, yr_ctx, yr_lat, ga, gb, wpa, wpb, wo, mod, norm2_w, wr_pad, br_col, tri)


def _row_copy(src_ref, src_row, dst_ref, dst_row, sem):
    return pltpu.make_async_copy(src_ref.at[pl.ds(src_row, 1)], dst_ref.at[pl.ds(dst_row, 1)], sem)


def _dispatch_kernel(zf_ref, pos_ref, h_ref, xs_ref, zero_ref, sem):
    tm = h_ref.shape[0]

    @pl.when(pl.program_id(0) == 0)
    def _():
        zero_ref[...] = jnp.zeros_like(zero_ref)

        def tile_copy(t):
            return pltpu.make_async_copy(zero_ref, xs_ref.at[pl.ds(t * MOE_TR, MOE_TR)], sem)

        def start(t, carry):
            @pl.when(zf_ref[t] != 0)
            def _():
                tile_copy(t).start()
            return carry

        def wait(t, carry):
            @pl.when(zf_ref[t] != 0)
            def _():
                tile_copy(t).wait()
            return carry

        lax.fori_loop(0, MOE_TILES, start, 0)
        lax.fori_loop(0, MOE_TILES, wait, 0)

    for r in range(tm):
        _row_copy(h_ref, r, xs_ref, pos_ref[0, 0, r], sem).start(priority=r % 2)
    pltpu.make_async_copy(h_ref, xs_ref.at[pl.ds(0, tm)], sem).wait()


def _dispatch(zero_flags, pos_blocks, h2):
    tm = TM_DISP
    return pl.pallas_call(
        _dispatch_kernel,
        grid_spec=pltpu.PrefetchScalarGridSpec(
            num_scalar_prefetch=1,
            grid=(T // tm,),
            in_specs=[pl.BlockSpec((1, 1, tm), lambda i, zf: (i, 0, 0), memory_space=pltpu.SMEM),
                      pl.BlockSpec((tm, MOE_W), lambda i, zf: (i, 0))],
            out_specs=pl.BlockSpec(memory_space=pl.ANY),
            scratch_shapes=[pltpu.VMEM((MOE_TR, MOE_W), f32), pltpu.SemaphoreType.DMA(())],
        ),
        out_shape=jax.ShapeDtypeStruct((MOE_ROWS, MOE_W), f32),
        compiler_params=pltpu.CompilerParams(
            dimension_semantics=("arbitrary",), vmem_limit_bytes=V7X_VMEM_LIMIT),
        name="moe_dispatch",
    )(zero_flags, pos_blocks, h2)


def _experts_kernel(l, tg_ref, ta_ref, tb_ref, first_ref, nxt_ref, nt_ref, xs_ref, wg_hbm, wu_hbm, wd_hbm,
                    y_ref, wg_ref, wu_ref, wd_ref, sg_ref, su_ref, sd_ref, sems):
    i = pl.program_id(0)
    live = i < nt_ref[0]
    g = tg_ref[i]

    def fetch(grp):
        return [pltpu.make_async_copy(w.at[l, pl.ds(grp * EXP_PER_GROUP, EXP_PER_GROUP)], s, sems.at[k])
                for k, (w, s) in enumerate(((wg_hbm, sg_ref), (wu_hbm, su_ref), (wd_hbm, sd_ref)))]

    @pl.when(i == 0)
    def _():
        for c in fetch(tg_ref[0]):
            c.start()

    @pl.when(live & (first_ref[i] != 0))
    def _():
        for c in fetch(g):
            c.wait()
        for j in range(EXP_PER_GROUP):
            wg_ref[j] = sg_ref[j].astype(bf16)
            wu_ref[j] = su_ref[j].astype(bf16)
            wd_ref[j] = sd_ref[j].astype(bf16)

        @pl.when(nxt_ref[i] >= 0)
        def _():
            for c in fetch(nxt_ref[i]):
                c.start(priority=1)

    @pl.when(live)
    def _():
        x = xs_ref[:, 0:D].astype(bf16)
        comb = xs_ref[:, D:D + LANES]
        lane = lax.broadcasted_iota(jnp.int32, comb.shape, 1)
        acc = None
        for j in (ta_ref[i], tb_ref[i]):
            w_j = jnp.sum(jnp.where(lane == g * EXP_PER_GROUP + j, comb, 0.0), axis=1, keepdims=True)
            act = _silu(_dot(x, wg_ref[j])) * _dot(x, wu_ref[j]) * w_j
            part = _dot(act.astype(bf16), wd_ref[j])
            acc = part if acc is None else acc + part
        y_ref[...] = acc

    @pl.when(i >= nt_ref[0])
    def _():
        y_ref[...] = jnp.zeros_like(y_ref)


def _experts(l, plan, xs, wg, wu, wd):
    def tile(i, *prefetch):
        return jnp.minimum(i, prefetch[-1][0] - 1)

    hbm = pl.BlockSpec(memory_space=pl.ANY)
    return pl.pallas_call(
        functools.partial(_experts_kernel, l),
        grid_spec=pltpu.PrefetchScalarGridSpec(
            num_scalar_prefetch=6,
            grid=(MOE_TILES,),
            in_specs=[pl.BlockSpec((MOE_TR, MOE_W), lambda i, *p: (tile(i, *p), 0)), hbm, hbm, hbm],
            out_specs=pl.BlockSpec((MOE_TR, D), lambda i, *p: (i, 0)),
            scratch_shapes=[pltpu.VMEM((EXP_PER_GROUP, D, D_EXP), bf16),
                            pltpu.VMEM((EXP_PER_GROUP, D, D_EXP), bf16),
                            pltpu.VMEM((EXP_PER_GROUP, D_EXP, D), bf16),
                            pltpu.VMEM((EXP_PER_GROUP, D, D_EXP), f32),
                            pltpu.VMEM((EXP_PER_GROUP, D, D_EXP), f32),
                            pltpu.VMEM((EXP_PER_GROUP, D_EXP, D), f32), pltpu.SemaphoreType.DMA((3,))],
        ),
        out_shape=jax.ShapeDtypeStruct((MOE_ROWS, D), f32),
        compiler_params=pltpu.CompilerParams(
            dimension_semantics=("arbitrary",), vmem_limit_bytes=V7X_VMEM_LIMIT),
        name="moe_experts",
    )(*plan, xs, wg, wu, wd)


def _combine_kernel(pos_ref, posn_ref, x1_ref, mod_ref, y_ref, oc_ref, ol_ref, ya_ref, yb_ref, sems):
    tm = x1_ref.shape[0]
    i = pl.program_id(0)
    n_ctx = T_CTX // tm

    def pull(p_ref, buf, sem):
        for r in range(tm):
            _row_copy(y_ref, p_ref[0, 0, r], buf, r, sem).start(priority=r % 2)

    @pl.when(i == 0)
    def _():
        pull(pos_ref, ya_ref, sems.at[0])

    def step(cur, nxt, sem_cur, sem_nxt):
        @pl.when(i + 1 < pl.num_programs(0))
        def _():
            pull(posn_ref, nxt, sem_nxt)

        pltpu.make_async_copy(y_ref.at[pl.ds(0, tm)], cur, sem_cur).wait()
        out = x1_ref[...] + mod_ref[5:6, :] * cur[...]

        @pl.when(i < n_ctx)
        def _():
            oc_ref[...] = out

        @pl.when(i >= n_ctx)
        def _():
            ol_ref[...] = out

    @pl.when(i % 2 == 0)
    def _():
        step(ya_ref, yb_ref, sems.at[0], sems.at[1])

    @pl.when(i % 2 == 1)
    def _():
        step(yb_ref, ya_ref, sems.at[1], sems.at[0])


def _combine(l, pos_blocks, x1, mod, y):
    tm = TM_DISP
    n = T // tm
    return pl.pallas_call(
        _combine_kernel,
        grid=(n,),
        in_specs=[pl.BlockSpec((1, 1, tm), lambda i: (i, 0, 0), memory_space=pltpu.SMEM),
                  pl.BlockSpec((1, 1, tm), lambda i: (jnp.minimum(i + 1, n - 1), 0, 0),
                               memory_space=pltpu.SMEM),
                  pl.BlockSpec((tm, D), lambda i: (i, 0)),
                  pl.BlockSpec((None, None, 6, D), lambda i: (l, _cond_of_tile(i, tm), 0, 0)),
                  pl.BlockSpec(memory_space=pl.ANY)],
        out_specs=_two_stream_specs(tm, D),
        out_shape=[jax.ShapeDtypeStruct((T_CTX, D), f32), jax.ShapeDtypeStruct((T_LAT, D), f32)],
        scratch_shapes=[pltpu.VMEM((tm, D), f32), pltpu.VMEM((tm, D), f32),
                        pltpu.SemaphoreType.DMA((2,))],
        compiler_params=pltpu.CompilerParams(
            dimension_semantics=("arbitrary",), vmem_limit_bytes=V7X_VMEM_LIMIT),
        name="moe_combine",
    )(pos_blocks, pos_blocks, x1, mod, y)


def _plan_kernel(cnt_ref, off_ref, tg_ref, ta_ref, tb_ref, first_ref, nxt_ref, nt_ref, zf_ref, nxtg_ref):
    def bucket(b, start):
        n = lax.shift_right_logical(cnt_ref[b] + (MOE_TR - 1), MOE_TR.bit_length() - 1)
        off_ref[b] = start * MOE_TR
        g = b // PAIRS
        p = b - g * PAIRS
        lo = jnp.where(p >= 3, 1, 0) + jnp.where(p >= 5, 1, 0)
        hi = p + 1 - jnp.where(lo >= 1, 2, 0) - jnp.where(lo >= 2, 1, 0)

        def tile(t, c):
            tg_ref[t] = g
            ta_ref[t] = lo
            tb_ref[t] = hi
            zf_ref[t] = jnp.where(t == start + n - 1, 1, 0)
            return c

        lax.fori_loop(start, start + n, tile, 0)
        return start + n

    nt = lax.fori_loop(0, N_BUCKETS, bucket, jnp.int32(0))
    nt_ref[0] = nt

    def unused(t, c):
        tg_ref[t] = N_GROUPS - 1
        ta_ref[t] = 0
        tb_ref[t] = 1
        zf_ref[t] = 1
        first_ref[t] = 0
        nxt_ref[t] = -1
        return c

    lax.fori_loop(nt, MOE_TILES, unused, 0)

    def back(i, nxt_group):
        t = nt - 1 - i
        g = tg_ref[t]
        later = jnp.where(t + 1 < nt, tg_ref[jnp.minimum(t + 1, MOE_TILES - 1)], -1)
        is_last_of_group = later != g
        nxt_group = jnp.where(is_last_of_group, later, nxt_group)
        nxtg_ref[g] = nxt_group
        return nxt_group

    lax.fori_loop(0, nt, back, jnp.int32(-1))

    def fwd(t, c):
        g = tg_ref[t]
        first_ref[t] = jnp.where((t == 0) | (tg_ref[jnp.maximum(t - 1, 0)] != g), 1, 0)
        nxt_ref[t] = nxtg_ref[g]
        return c

    lax.fori_loop(0, nt, fwd, 0)


def _moe_plan(route, counts):
    smem = pl.BlockSpec(memory_space=pltpu.SMEM)
    tile_i32 = jax.ShapeDtypeStruct((MOE_TILES,), jnp.int32)
    off, tg, ta, tb, first, nxt, n_tiles, zero_flags = pl.pallas_call(
        _plan_kernel,
        in_specs=[smem],
        out_specs=[smem] * 8,
        out_shape=[jax.ShapeDtypeStruct((N_BUCKETS,), jnp.int32), tile_i32, tile_i32, tile_i32, tile_i32,
                   tile_i32, jax.ShapeDtypeStruct((1,), jnp.int32), tile_i32],
        scratch_shapes=[pltpu.SMEM((N_GROUPS,), jnp.int32)],
        name="moe_plan",
    )(counts[0:N_BUCKETS, 0].astype(jnp.int32))
    buckets = jnp.arange(N_BUCKETS, dtype=f32)[:, None]
    pos = jnp.sum(jnp.where(route[0][None, :] == buckets, off.astype(f32)[:, None], 0.0), axis=0) + route[1]
    pos_blocks = pos.astype(jnp.int32).reshape(T // TM_DISP, 1, TM_DISP)
    return pos_blocks, zero_flags, (tg, ta, tb, first, nxt, n_tiles)


def _rope_tables():
    pos = np.arange(DEC_SEQ)
    half = HD_A // 4
    freqs = ROPE_BASE ** (-np.arange(half, dtype=np.float64) / half)
    ang_r = (pos // GRID_W)[:, None] * freqs[None, :]
    ang_c = (pos % GRID_W)[:, None] * freqs[None, :]
    ang = np.concatenate([ang_r, ang_r, ang_c, ang_c], axis=1)
    sign = np.concatenate([-np.ones(half), np.ones(half)] * 2)[None, :]
    cos = np.tile(np.cos(ang), (1, 2))
    sin = np.tile(np.sin(ang) * sign, (1, 2))
    ident_c = np.ones((TM_PROJ, LANES))
    ident_s = np.zeros((TM_PROJ, LANES))
    return (jnp.asarray(np.concatenate([cos, ident_c]), f32),
            jnp.asarray(np.concatenate([sin, ident_s]), f32))


def kernel(x_prompt, x_sample, c, cache_k, cache_v, state_ret_fwd, state_ret_bwd, c_ctx,
           norm1_w, norm2_w, w_ada, b_ada, w_in, q_norm_w, k_norm_w, attn_sink,
           ret_decay_fwd, ret_decay_bwd, ret_gn_w, w_pa, w_pb, w_o, w_router, b_router,
           w_exp_gate, w_exp_up, w_exp_down):
    x_ctx, x_lat = x_prompt.reshape(T_CTX, D), x_sample.reshape(T_LAT, D)
    cond8 = jnp.zeros((N_COND, D), f32).at[0].set(c_ctx).at[1:1 + DEC_BATCH].set(c)
    mod = _adaln(cond8, w_ada, b_ada).reshape(DEPTH, N_COND, 6, D)

    cos_t, sin_t = _rope_tables()
    blk = np.arange(Q_A) // HD_A
    ones_blk = jnp.asarray((blk[:, None] == blk[None, :]) / HD_A, bf16)
    tri = jnp.asarray(np.triu(np.ones((POST_SUB, POST_SUB)), 1), bf16)
    per_layer = lambda w: w.reshape(DEPTH, 1, -1)
    norm1_w, norm2_w, gnw = per_layer(norm1_w), per_layer(norm2_w), per_layer(ret_gn_w)
    qnw = per_layer(jnp.tile(q_norm_w, (1, NH_A)) * (HD_A ** -0.5))
    knw = per_layer(jnp.tile(k_norm_w, (1, NKV_A)))
    wr_pad = jnp.concatenate(_split(jnp.pad(w_router, ((0, 0), (0, LANES - N_EXP)))), axis=1)
    br_col = jnp.broadcast_to(b_router[:, None], (N_EXP, LANES))
    ck = cache_k.reshape(DEC_BATCH, DEPTH, PAST, KV_A)
    cv = cache_v.reshape(DEC_BATCH, DEPTH, PAST, KV_A)
    s0 = jnp.concatenate([state_ret_fwd, state_ret_bwd], axis=3)
    lg = jnp.log1p(-jnp.exp(jnp.concatenate([ret_decay_fwd, ret_decay_bwd], axis=1).astype(f32)))

    new_k, new_v, new_sf, new_sb = [], [], [], []
    for l in range(DEPTH):
        qa, ka, va, qr, kr, vr, gr, ga, gb = _inproj(l, x_ctx, x_lat, norm1_w, mod, w_in, cos_t, sin_t,
                                                     ones_blk, qnw, knw)
        oa_ctx = _ctx_attn(l, qa, ka, va, attn_sink)
        oa_lat = _lat_attn(l, qa, ka, va, ck, cv, cos_t, sin_t, attn_sink)
        yr_ctx, sf, sb = _retention(l, qr, kr, vr, gr, None, lg, gnw, BATCH, SEQ, 0, CTX_SEQS_PER_STEP)
        yr_lat, _, _ = _retention(l, qr, kr, vr, gr, s0, lg, gnw, DEC_BATCH, DEC_SEQ,
                                  T_CTX // DEC_SEQ, 1)
        x1, h2, route, counts = _post(l, x_ctx, x_lat, oa_ctx, oa_lat, yr_ctx, yr_lat, ga, gb,
                                      w_pa, w_pb, w_o, mod, norm2_w, wr_pad, br_col, tri)
        pos_blocks, zero_flags, plan = _moe_plan(route, counts)
        xs = _dispatch(zero_flags, pos_blocks, h2)
        y = _experts(l, plan, xs, w_exp_gate, w_exp_up, w_exp_down)
        x_ctx, x_lat = _combine(l, pos_blocks, x1, mod, y)

        new_k.append(ka[:T_CTX].reshape(BATCH, SEQ, NKV_A, HD_A))
        new_v.append(va[:T_CTX].reshape(BATCH, SEQ, NKV_A, HD_A))
        new_sf.append(sf)
        new_sb.append(sb)

    return (x_ctx.reshape(BATCH, SEQ, D), x_lat.reshape(DEC_BATCH, DEC_SEQ, D),
            jnp.stack(new_k, axis=1), jnp.stack(new_v, axis=1),
            jnp.stack(new_sf, axis=1), jnp.stack(new_sb, axis=1))
```

```python
import functools

import numpy as np
import jax
import jax.numpy as jnp
from jax import lax
from jax.experimental import pallas as pl
from jax.experimental.pallas import tpu as pltpu

D = 1024
BATCH, SEQ = 16, 256
DEC_BATCH, DEC_SEQ = 2, 2048
DEPTH = 2
PAST = 512
GRID_W = 64
NH_A, NKV_A, HD_A = 8, 2, 64
WINDOW = 128
NH_R, DK_R, DV_R = 4, 64, 128
CHUNK = 128
N_EXP, N_GROUPS, EXP_PER_GROUP = 16, 4, 4
D_EXP = 512
ROPE_BASE = 10000.0
EPS = 1e-6
NEG_INF = -1e30

Q_A = NH_A * HD_A
KV_A = NKV_A * HD_A
QK_R = NH_R * DK_R
V_R = NH_R * DV_R
C_QA = (0, Q_A)
C_KA = (C_QA[1], C_QA[1] + KV_A)
C_VA = (C_KA[1], C_KA[1] + KV_A)
C_QR = (C_VA[1], C_VA[1] + QK_R)
C_KR = (C_QR[1], C_QR[1] + QK_R)
C_VR = (C_KR[1], C_KR[1] + V_R)
C_GR = (C_VR[1], C_VR[1] + V_R)
C_GA = (C_GR[1], C_GR[1] + D)
C_GB = (C_GA[1], C_GA[1] + D)
D_IN = C_GB[1]

T_CTX = BATCH * SEQ
T_LAT = DEC_BATCH * DEC_SEQ
T = T_CTX + T_LAT
N_COND = 8

LANES = 128
MXU_TILE = 256
V7X_VMEM_LIMIT = 56 * 1024 * 1024

TM_PROJ = 1024
TM_POST = 512
POST_SUB = 512
W_CHUNK = 256
TM_DISP = 512
MOE_TR = 256
PAIRS = EXP_PER_GROUP * (EXP_PER_GROUP - 1) // 2
N_BUCKETS = N_GROUPS * PAIRS
BUCKET_ROWS = 32
MOE_TILES = T // MOE_TR + N_BUCKETS
MOE_ROWS = MOE_TILES * MOE_TR
MOE_W = D + LANES
ATT_QB = 256
RET_UNROLL = 4
CTX_SEQS_PER_STEP = 4

f32 = jnp.float32
bf16 = jnp.bfloat16


def _dot(a, b):
    return jnp.dot(a, b, preferred_element_type=f32)


def _dot_t(a, b):
    return lax.dot_general(a, b, (((1,), (1,)), ((), ())), preferred_element_type=f32)


def _split(x):
    hi = x.astype(bf16)
    lo = (x - hi.astype(f32)).astype(bf16)
    return hi, lo


def _dot3(a, b):
    m = a.shape[0]
    ah, al = _split(a)
    bh, bl = _split(b)
    both = _dot(jnp.concatenate([ah, al], axis=0), bh)
    return both[:m] + (_dot(ah, bl) + both[m:])


def _sigmoid(x):
    return 1.0 / (1.0 + jnp.exp(-x))


def _silu(x):
    return x * _sigmoid(x)


def _layer_row(l, n):
    return pl.BlockSpec((None, 1, n), lambda *_: (l, 0, 0))


def _cond_of_tile(i, tm):
    n_ctx = T_CTX // tm
    per_b = DEC_SEQ // tm
    return jnp.where(i < n_ctx, 0, 1 + jnp.maximum(i - n_ctx, 0) // per_b)


def _adaln_kernel(cond_ref, w_ref, b_ref, o_ref):
    a = _silu(cond_ref[...])
    o_ref[0] = _dot3(a, w_ref[0]) + b_ref[0]


def _adaln(cond8, w_ada, b_ada):
    tn = 2048
    return pl.pallas_call(
        _adaln_kernel,
        grid=(DEPTH, 6 * D // tn),
        in_specs=[
            pl.BlockSpec((N_COND, D), lambda l, j: (0, 0)),
            pl.BlockSpec((1, D, tn), lambda l, j: (l, 0, j)),
            pl.BlockSpec((1, 1, tn), lambda l, j: (l, 0, j)),
        ],
        out_specs=pl.BlockSpec((1, N_COND, tn), lambda l, j: (l, 0, j)),
        out_shape=jax.ShapeDtypeStruct((DEPTH, N_COND, 6 * D), f32),
        compiler_params=pltpu.CompilerParams(
            dimension_semantics=("parallel", "parallel"), vmem_limit_bytes=V7X_VMEM_LIMIT),
        name="adaln",
    )(cond8, w_ada, b_ada.reshape(DEPTH, 1, 6 * D))


def _rope(x, cos, sin_signed, first_half):
    fwd = pltpu.roll(x, 16, 1)
    bwd = pltpu.roll(x, LANES - 16, 1)
    partner = jnp.where(first_half, bwd, fwd)
    return x * cos + partner * sin_signed


def _head_rms(x, ones_blk, w):
    n = x.shape[1]
    sq_hi, sq_lo = _split(x * x)
    wt = min(n, MXU_TILE)
    blk = ones_blk[0:wt, 0:wt]
    parts = [_dot(sq_hi[:, c:c + wt], blk) + _dot(sq_lo[:, c:c + wt], blk) for c in range(0, n, wt)]
    mean = parts[0] if len(parts) == 1 else jnp.concatenate(parts, axis=1)
    return x * lax.rsqrt(mean + EPS) * w


def _load_cast(w_hbm, dst_ref, stage_ref, sems):
    k, n = w_hbm.shape

    def chunk_copy(c):
        return pltpu.make_async_copy(w_hbm.at[:, pl.ds(c * W_CHUNK, W_CHUNK)],
                                     stage_ref.at[c % 2, pl.ds(0, k)], sems.at[c % 2])

    n_chunks = n // W_CHUNK
    chunk_copy(0).start()
    for c in range(n_chunks):
        if c + 1 < n_chunks:
            chunk_copy(c + 1).start()
        chunk_copy(c).wait()
        dst_ref[:, c * W_CHUNK:(c + 1) * W_CHUNK] = stage_ref[c % 2, 0:k, :].astype(bf16)


def _tile_of_two(i, n_first, a_ref, b_ref):
    return jnp.where(i < n_first, a_ref[...], b_ref[...])


def _inproj_kernel(l, xc_ref, xl_ref, n1_ref, mod_ref, w_hbm, cos_ref, sin_ref, ones_ref, qnw_ref,
                   knw_ref, qa_ref, ka_ref, va_ref, qr_ref, kr_ref, vr_ref, gr_ref, ga_ref, gb_ref,
                   w_ref, stage_ref, sems):
    i = pl.program_id(0)

    @pl.when(i == 0)
    def _():
        _load_cast(w_hbm.at[l], w_ref, stage_ref, sems)

    x = _tile_of_two(i, T_CTX // xc_ref.shape[0], xc_ref, xl_ref)
    y = x * lax.rsqrt(jnp.mean(x * x, axis=-1, keepdims=True) + EPS) * n1_ref[...]
    h = (y * (1.0 + mod_ref[1:2, :]) + mod_ref[0:1, :]).astype(bf16)

    whole = _dot(h, w_ref[...])

    def proj(c):
        return whole[:, c[0]:c[1]]

    cos = cos_ref[...]
    sin = sin_ref[...]
    lane = lax.broadcasted_iota(jnp.int32, cos.shape, 1)
    first_half = (lane % 32) < 16

    def rope_all(v):
        parts = [_rope(v[:, j:j + LANES], cos, sin, first_half) for j in range(0, v.shape[1], LANES)]
        return parts[0] if len(parts) == 1 else jnp.concatenate(parts, axis=1)

    ones_blk = ones_ref[...]
    qa = _head_rms(proj(C_QA), ones_blk, qnw_ref[...])
    qa_ref[...] = rope_all(qa).astype(bf16)
    ka_ref[...] = _head_rms(proj(C_KA), ones_blk, knw_ref[...])
    va_ref[...] = proj(C_VA)
    qr_ref[...] = rope_all(proj(C_QR)).astype(bf16)
    kr_ref[...] = (rope_all(proj(C_KR)) * (DK_R ** -0.5)).astype(bf16)
    vr_ref[...] = proj(C_VR).astype(bf16)
    gr_ref[...] = proj(C_GR).astype(bf16)
    ga_ref[...] = proj(C_GA).astype(bf16)
    gb_ref[...] = proj(C_GB).astype(bf16)


def _two_stream_specs(tm, width):
    n_ctx = T_CTX // tm
    return [pl.BlockSpec((tm, width), lambda i, *_: (jnp.minimum(i, n_ctx - 1), 0)),
            pl.BlockSpec((tm, width), lambda i, *_: (jnp.maximum(i - n_ctx, 0), 0))]


def _inproj(l, x_ctx, x_lat, norm1_w, mod, w_in, cos_t, sin_t, ones_blk, qnw, knw):
    tm = TM_PROJ
    n_ctx = T_CTX // tm
    per_b = DEC_SEQ // tm

    def tab_map(i):
        return (jnp.where(i < n_ctx, per_b, jnp.maximum(i - n_ctx, 0) % per_b), 0)

    row = lambda i: (i, 0)
    const = lambda i: (0, 0)
    widths = [(Q_A, bf16), (KV_A, f32), (KV_A, f32), (QK_R, bf16), (QK_R, bf16), (V_R, bf16),
              (V_R, bf16), (D, bf16), (D, bf16)]
    return pl.pallas_call(
        functools.partial(_inproj_kernel, l),
        grid=(T // tm,),
        in_specs=_two_stream_specs(tm, D) + [
            _layer_row(l, D),
            pl.BlockSpec((None, None, 6, D), lambda i: (l, _cond_of_tile(i, tm), 0, 0)),
            pl.BlockSpec(memory_space=pl.ANY),
            pl.BlockSpec((tm, LANES), tab_map),
            pl.BlockSpec((tm, LANES), tab_map),
            pl.BlockSpec((Q_A, Q_A), const),
            _layer_row(l, Q_A),
            _layer_row(l, KV_A),
        ],
        out_specs=[pl.BlockSpec((tm, w), row) for w, _ in widths],
        out_shape=[jax.ShapeDtypeStruct((T, w), dt) for w, dt in widths],
        scratch_shapes=[pltpu.VMEM((D, D_IN), bf16), pltpu.VMEM((2, D, W_CHUNK), f32),
                        pltpu.SemaphoreType.DMA((2,))],
        compiler_params=pltpu.CompilerParams(
            dimension_semantics=("arbitrary",), vmem_limit_bytes=V7X_VMEM_LIMIT),
        name="inproj",
    )(x_ctx, x_lat, norm1_w, mod, w_in, cos_t, sin_t, ones_blk, qnw, knw)


def _head_blocks(t, kv, lo_mask):
    r = pltpu.roll(t, HD_A, 1)
    if kv == 0:
        a = jnp.where(lo_mask, t, 0.0)
        b = jnp.where(lo_mask, 0.0, r)
    else:
        a = jnp.where(lo_mask, r, 0.0)
        b = jnp.where(lo_mask, 0.0, t)
    return jnp.concatenate([a, b], axis=0).astype(bf16)


def _ctx_attn_kernel(l, sink_ref, q_ref, k_ref, v_ref, o_ref):
    n = SEQ
    lo_mask = lax.broadcasted_iota(jnp.int32, (n, LANES), 1) < HD_A
    for b in range(CTX_SEQS_PER_STEP):
        rows = slice(b * n, (b + 1) * n)
        k = k_ref[rows, :]
        v = v_ref[rows, :]
        for kv in range(NKV_A):
            kblk = _head_blocks(k, kv, lo_mask)
            vblk = _head_blocks(v, kv, lo_mask)
            q2 = jnp.concatenate([q_ref[rows, (2 * kv) * LANES:(2 * kv + 1) * LANES],
                                  q_ref[rows, (2 * kv + 1) * LANES:(2 * kv + 2) * LANES]], axis=0)
            s = _dot_t(q2, kblk)
            row = lax.broadcasted_iota(jnp.int32, (2 * n, 1), 0)
            ps, invs = [], []
            for hh in range(2):
                sk = jnp.where(row < n, sink_ref[l, 4 * kv + hh], sink_ref[l, 4 * kv + 2 + hh])
                sh = s[:, hh * n:(hh + 1) * n]
                m = jnp.maximum(jnp.max(sh, axis=-1, keepdims=True), sk)
                p = jnp.exp(sh - m)
                invs.append(1.0 / (jnp.sum(p, axis=-1, keepdims=True) + jnp.exp(sk - m)))
                ps.append(p.astype(bf16))
            lo2 = lax.broadcasted_iota(jnp.int32, (2 * n, LANES), 1) < HD_A
            o = _dot(jnp.concatenate(ps, axis=1), vblk) * jnp.where(lo2, invs[0], invs[1])
            o_ref[rows, (2 * kv) * LANES:(2 * kv + 1) * LANES] = o[:n].astype(bf16)
            o_ref[rows, (2 * kv + 1) * LANES:(2 * kv + 2) * LANES] = o[n:].astype(bf16)


def _ctx_attn(l, qa, ka, va, sink):
    blk = lambda w: pl.BlockSpec((CTX_SEQS_PER_STEP * SEQ, w), lambda b: (b, 0))
    return pl.pallas_call(
        functools.partial(_ctx_attn_kernel, l),
        grid=(BATCH // CTX_SEQS_PER_STEP,),
        in_specs=[pl.BlockSpec(memory_space=pltpu.SMEM), blk(Q_A), blk(KV_A), blk(KV_A)],
        out_specs=blk(Q_A),
        out_shape=jax.ShapeDtypeStruct((T_CTX, Q_A), bf16),
        compiler_params=pltpu.CompilerParams(
            dimension_semantics=("parallel",), vmem_limit_bytes=V7X_VMEM_LIMIT),
        name="ctx_attn",
    )(sink, qa, ka, va)


def _lat_attn_kernel(l, sink_ref, q_ref, k_ref, v_ref, kc_ref, vc_ref, cos_ref, sin_ref, o_ref):
    j = pl.program_id(1)
    qb = ATT_QB
    win = 2 * qb
    ws = pl.multiple_of(jnp.clip(j * qb - WINDOW, 0, DEC_SEQ - win), WINDOW)
    lo_mask = lax.broadcasted_iota(jnp.int32, (win, LANES), 1) < HD_A
    lane = lax.broadcasted_iota(jnp.int32, (win, LANES), 1)
    kw = _rope(k_ref[pl.ds(ws, win), :], cos_ref[pl.ds(ws, win), :], sin_ref[pl.ds(ws, win), :],
               (lane % 32) < 16)
    vw = v_ref[pl.ds(ws, win), :]
    kc = kc_ref[...]
    vc = vc_ref[...]
    qpos = j * qb + (lax.broadcasted_iota(jnp.int32, (2 * qb, win), 0) & (qb - 1))
    kpos = ws + lax.broadcasted_iota(jnp.int32, (2 * qb, win), 1)
    valid = jnp.abs(qpos - kpos) <= WINDOW
    out_lo = lax.broadcasted_iota(jnp.int32, (2 * qb, LANES), 1) < HD_A
    for kv in range(NKV_A):
        kc_blk = _head_blocks(kc, kv, lo_mask[:PAST])
        vc_blk = _head_blocks(vc, kv, lo_mask[:PAST])
        kw_blk = _head_blocks(kw, kv, lo_mask)
        vw_blk = _head_blocks(vw, kv, lo_mask)
        q2 = jnp.concatenate([q_ref[:, (2 * kv) * LANES:(2 * kv + 1) * LANES],
                              q_ref[:, (2 * kv + 1) * LANES:(2 * kv + 2) * LANES]], axis=0)
        s_c = _dot_t(q2, kc_blk)
        s_w = _dot_t(q2, kw_blk)
        pcs, pws, invs = [], [], []
        for hh in range(2):
            row = lax.broadcasted_iota(jnp.int32, (2 * qb, 1), 0)
            sk = jnp.where(row < qb, sink_ref[l, 4 * kv + hh], sink_ref[l, 4 * kv + 2 + hh])
            sc = s_c[:, hh * PAST:(hh + 1) * PAST]
            sw = jnp.where(valid, s_w[:, hh * win:(hh + 1) * win], NEG_INF)
            m = jnp.maximum(jnp.maximum(jnp.max(sc, axis=-1, keepdims=True),
                                        jnp.max(sw, axis=-1, keepdims=True)), sk)
            pc = jnp.exp(sc - m)
            pw = jnp.exp(sw - m)
            den = (jnp.sum(pc, axis=-1, keepdims=True) + jnp.sum(pw, axis=-1, keepdims=True)
                   + jnp.exp(sk - m))
            invs.append(1.0 / den)
            pcs.append(pc.astype(bf16))
            pws.append(pw.astype(bf16))
        o = _dot(jnp.concatenate(pcs, axis=1), vc_blk) + _dot(jnp.concatenate(pws, axis=1), vw_blk)
        o = o * jnp.where(out_lo, invs[0], invs[1])
        o_ref[:, (2 * kv) * LANES:(2 * kv + 1) * LANES] = o[:qb].astype(bf16)
        o_ref[:, (2 * kv + 1) * LANES:(2 * kv + 2) * LANES] = o[qb:].astype(bf16)


def _lat_attn(l, qa, ka, va, cache_k, cache_v, cos_l, sin_l, sink):
    qb = ATT_QB
    nq = DEC_SEQ // qb
    ctx_blocks = T_CTX // DEC_SEQ
    seq = lambda b, j: (ctx_blocks + b, 0)
    return pl.pallas_call(
        functools.partial(_lat_attn_kernel, l),
        grid=(DEC_BATCH, nq),
        in_specs=[
            pl.BlockSpec(memory_space=pltpu.SMEM),
            pl.BlockSpec((qb, Q_A), lambda b, j: (T_CTX // qb + b * nq + j, 0)),
            pl.BlockSpec((DEC_SEQ, KV_A), seq),
            pl.BlockSpec((DEC_SEQ, KV_A), seq),
            pl.BlockSpec((None, None, PAST, KV_A), lambda b, j: (b, l, 0, 0)),
            pl.BlockSpec((None, None, PAST, KV_A), lambda b, j: (b, l, 0, 0)),
            pl.BlockSpec((DEC_SEQ, LANES), lambda b, j: (0, 0)),
            pl.BlockSpec((DEC_SEQ, LANES), lambda b, j: (0, 0)),
        ],
        out_specs=pl.BlockSpec((qb, Q_A), lambda b, j: (b * nq + j, 0)),
        out_shape=jax.ShapeDtypeStruct((T_LAT, Q_A), bf16),
        compiler_params=pltpu.CompilerParams(
            dimension_semantics=("parallel", "parallel"), vmem_limit_bytes=V7X_VMEM_LIMIT),
        name="lat_attn",
    )(sink, qa, ka, va, cache_k, cache_v, cos_l, sin_l)


def _dup_heads(pair, lo_mask):
    r = pltpu.roll(pair, DK_R, 1)
    return jnp.where(lo_mask, pair, r), jnp.where(lo_mask, r, pair)


def _retention_kernel(l, has_s0, n_seq, n_chunks, *refs):
    if has_s0:
        (lg_ref, q_ref, k_ref, v_ref, g_ref, s0_ref, gnw_ref,
         y_ref, sf_ref, sb_ref, ds_ref, st_ref, mask_ref, qdec_ref, kdec_ref, cdec_ref) = refs
    else:
        (lg_ref, q_ref, k_ref, v_ref, g_ref, gnw_ref,
         y_ref, sf_ref, sb_ref, ds_ref, st_ref, mask_ref, qdec_ref, kdec_ref, cdec_ref) = refs
        s0_ref = None
    C = CHUNK
    lo_mask = lax.broadcasted_iota(jnp.int32, (C, LANES), 1) < DK_R

    ri = lax.broadcasted_iota(jnp.int32, (C, C), 0).astype(f32)
    diff = ri - lax.broadcasted_iota(jnp.int32, (C, C), 1).astype(f32)
    for h in range(NH_R):
        lg_f, lg_b = lg_ref[l, h], lg_ref[l, NH_R + h]
        mask_ref[h] = (jnp.where(diff >= 0, jnp.exp(jnp.maximum(diff, 0.0) * lg_f), 0.0)
                       + jnp.where(diff <= 0, jnp.exp(jnp.maximum(-diff, 0.0) * lg_b), 0.0))
        qdec_ref[:, h * LANES:(h + 1) * LANES] = jnp.exp(
            jnp.where(lo_mask, (ri + 1.0) * lg_f, (C - ri) * lg_b))
        kdec_ref[:, h * LANES:(h + 1) * LANES] = jnp.exp(
            jnp.where(lo_mask, (C - 1.0 - ri) * lg_f, ri * lg_b))
        cdec_ref[h] = jnp.exp(jnp.where(ri < DK_R, C * lg_f, C * lg_b))

    def inc_body(c, carry):
        r0 = pl.multiple_of(c * C, C)
        for pr in range(2):
            kp = k_ref[pl.ds(r0, C), pr * LANES:(pr + 1) * LANES].astype(f32)
            for hh, kd in enumerate(_dup_heads(kp, lo_mask)):
                h = 2 * pr + hh
                kd = (kd * kdec_ref[:, h * LANES:(h + 1) * LANES]).astype(bf16)
                vh = v_ref[pl.ds(r0, C), h * DV_R:(h + 1) * DV_R]
                ds_ref[c, h] = lax.dot_general(kd, vh, (((0,), (0,)), ((), ())),
                                               preferred_element_type=f32)
        return carry

    lax.fori_loop(0, n_seq * n_chunks, inc_body, 0, unroll=RET_UNROLL)

    for sq in range(n_seq):
        base = sq * n_chunks
        for h in range(NH_R):
            cf = cdec_ref[h, 0:DK_R, :]
            cb = cdec_ref[h, DK_R:2 * DK_R, :]
            if has_s0:
                init_f = s0_ref[h, 0:DK_R, :]
                init_b = s0_ref[h, DK_R:2 * DK_R, :]
            else:
                init_f = jnp.zeros((DK_R, DV_R), f32)
                init_b = init_f

            def fwd_body(i, s, h=h, cf=cf, base=base):
                c = base + i
                st_ref[c, h, 0:DK_R, :] = s
                return s * cf + ds_ref[c, h, 0:DK_R, :]

            def bwd_body(i, s, h=h, cb=cb, base=base):
                c = base + n_chunks - 1 - i
                st_ref[c, h, DK_R:2 * DK_R, :] = s
                return s * cb + ds_ref[c, h, DK_R:2 * DK_R, :]

            sf_ref[sq, h] = lax.fori_loop(0, n_chunks, fwd_body, init_f)
            sb_ref[sq, h] = lax.fori_loop(0, n_chunks, bwd_body, init_b)

    def out_body(c, carry):
        r0 = pl.multiple_of(c * C, C)
        for pr in range(2):
            qp = q_ref[pl.ds(r0, C), pr * LANES:(pr + 1) * LANES]
            kp = k_ref[pl.ds(r0, C), pr * LANES:(pr + 1) * LANES].astype(f32)
            kblk = jnp.concatenate([jnp.where(lo_mask, kp, 0.0), jnp.where(lo_mask, 0.0, kp)],
                                   axis=0).astype(bf16)
            a2 = _dot_t(qp, kblk)
            for hh, qd in enumerate(_dup_heads(qp.astype(f32), lo_mask)):
                h = 2 * pr + hh
                a = (a2[:, hh * C:(hh + 1) * C] * mask_ref[h]).astype(bf16)
                vh = v_ref[pl.ds(r0, C), h * DV_R:(h + 1) * DV_R]
                qd = (qd * qdec_ref[:, h * LANES:(h + 1) * LANES]).astype(bf16)
                o = _dot(a, vh) + _dot(qd, st_ref[c, h].astype(bf16))
                mu = jnp.mean(o, axis=-1, keepdims=True)
                d = o - mu
                var = jnp.mean(d * d, axis=-1, keepdims=True)
                yh = d * lax.rsqrt(var + EPS) * gnw_ref[:, h * DV_R:(h + 1) * DV_R]
                g = g_ref[pl.ds(r0, C), h * DV_R:(h + 1) * DV_R].astype(f32)
                y_ref[pl.ds(r0, C), h * DV_R:(h + 1) * DV_R] = (yh * _silu(g)).astype(bf16)
        return carry

    lax.fori_loop(0, n_seq * n_chunks, out_body, 0, unroll=RET_UNROLL)


def _retention(l, qr, kr, vr, gr, s0, lg, gnw, nb, seq, row_block0, n_seq):
    n_chunks = seq // CHUNK
    has_s0 = s0 is not None
    rows = n_seq * seq
    tok = lambda w: pl.BlockSpec((rows, w), lambda b: (row_block0 + b, 0))
    in_specs = [pl.BlockSpec(memory_space=pltpu.SMEM), tok(QK_R), tok(QK_R), tok(V_R), tok(V_R)]
    args = [lg, qr, kr, vr, gr]
    if has_s0:
        in_specs.append(pl.BlockSpec((None, None, NH_R, 2 * DK_R, DV_R), lambda b: (b, l, 0, 0, 0)))
        args.append(s0)
    in_specs += [_layer_row(l, V_R)]
    args += [gnw]
    st_spec = pl.BlockSpec((n_seq, NH_R, DK_R, DV_R), lambda b: (b, 0, 0, 0))
    return pl.pallas_call(
        functools.partial(_retention_kernel, l, has_s0, n_seq, n_chunks),
        grid=(nb // n_seq,),
        in_specs=in_specs,
        out_specs=[pl.BlockSpec((rows, V_R), lambda b: (b, 0)), st_spec, st_spec],
        out_shape=[jax.ShapeDtypeStruct((nb * seq, V_R), bf16),
                   jax.ShapeDtypeStruct((nb, NH_R, DK_R, DV_R), f32),
                   jax.ShapeDtypeStruct((nb, NH_R, DK_R, DV_R), f32)],
        scratch_shapes=[pltpu.VMEM((n_seq * n_chunks, NH_R, 2 * DK_R, DV_R), f32),
                        pltpu.VMEM((n_seq * n_chunks, NH_R, 2 * DK_R, DV_R), f32),
                        pltpu.VMEM((NH_R, CHUNK, CHUNK), f32), pltpu.VMEM((CHUNK, NH_R * LANES), f32),
                        pltpu.VMEM((CHUNK, NH_R * LANES), f32), pltpu.VMEM((NH_R, 2 * DK_R, DV_R), f32)],
        compiler_params=pltpu.CompilerParams(
            dimension_semantics=("parallel",), vmem_limit_bytes=V7X_VMEM_LIMIT),
        name="retention_lat" if has_s0 else "retention_ctx",
    )(*args)


def _post_kernel(l, xc_ref, xl_ref, oac_ref, oal_ref, yrc_ref, yrl_ref, ga_ref, gb_ref, wpa_hbm, wpb_hbm,
                 wo_hbm, mod_ref, n2_ref, wr_ref, br_ref, tri_ref, x1_ref, h2_ref, route_ref, cnt_ref,
                 carry_ref, wpa_ref, wpb_ref, wo_ref, stage_ref, sems):
    i = pl.program_id(0)
    tm = xc_ref.shape[0]
    n_ctx = T_CTX // tm

    @pl.when(i == 0)
    def _():
        carry_ref[...] = jnp.zeros_like(carry_ref)
        _load_cast(wpa_hbm.at[l], wpa_ref, stage_ref, sems)
        _load_cast(wpb_hbm.at[l], wpb_ref, stage_ref, sems)
        _load_cast(wo_hbm.at[l], wo_ref, stage_ref, sems)

    for r0 in range(0, tm, POST_SUB):
        _post_subtile(i < n_ctx, slice(r0, r0 + POST_SUB), xc_ref, xl_ref, oac_ref, oal_ref, yrc_ref,
                      yrl_ref, ga_ref, gb_ref, mod_ref, n2_ref, wr_ref, br_ref, tri_ref, x1_ref, h2_ref,
                      route_ref, carry_ref, wpa_ref, wpb_ref, wo_ref)
    cnt_ref[...] = carry_ref[...]


def _post_subtile(is_ctx, rows, xc_ref, xl_ref, oac_ref, oal_ref, yrc_ref, yrl_ref, ga_ref, gb_ref, mod_ref,
                  n2_ref, wr_ref, br_ref, tri_ref, x1_ref, h2_ref, route_ref, carry_ref, wpa_ref, wpb_ref,
                  wo_ref):
    tm = rows.stop - rows.start
    pick = lambda a_ref, b_ref: jnp.where(is_ctx, a_ref[rows, :], b_ref[rows, :])
    ga = _sigmoid(ga_ref[rows, :].astype(f32))
    gb = _sigmoid(gb_ref[rows, :].astype(f32))
    merged = (ga * _dot(pick(oac_ref, oal_ref), wpa_ref[...])
              + gb * _dot(pick(yrc_ref, yrl_ref), wpb_ref[...]))
    mix = _dot(merged.astype(bf16), wo_ref[...])
    x1 = pick(xc_ref, xl_ref) + mod_ref[2:3, :] * mix
    x1_ref[rows, :] = x1
    y = x1 * lax.rsqrt(jnp.mean(x1 * x1, axis=-1, keepdims=True) + EPS) * n2_ref[...]
    h2 = y * (1.0 + mod_ref[4:5, :]) + mod_ref[3:4, :]
    h2_ref[rows, 0:D] = h2

    h_hi, h_lo = _split(h2)
    both = _dot(h_hi, wr_ref[...])
    logits = both[:, 0:LANES] + (both[:, LANES:2 * LANES] + _dot(h_lo, wr_ref[:, 0:LANES]))
    lt = logits.T[0:N_EXP, :]
    scores = _sigmoid(lt)
    sel = scores + br_ref[:, 0:1]
    row = lax.broadcasted_iota(jnp.int32, (N_EXP, tm), 0)

    best = None
    bg = None
    for g in range(N_GROUPS):
        a, b, c, d = (sel[EXP_PER_GROUP * g + k:EXP_PER_GROUP * g + k + 1, :] for k in range(4))
        p, q = jnp.maximum(a, b), jnp.minimum(a, b)
        r, s = jnp.maximum(c, d), jnp.minimum(c, d)
        gs = jnp.maximum(p, r) + jnp.maximum(jnp.minimum(p, r), jnp.maximum(q, s))
        if g == 0:
            best, bg = gs, jnp.zeros((1, tm), jnp.int32)
        else:
            upd = gs > best
            bg = jnp.where(upd, g, bg)
            best = jnp.where(upd, gs, best)
    masked = jnp.where(jnp.right_shift(row, 2) == bg, sel, NEG_INF)
    m1 = jnp.max(masked, axis=0, keepdims=True)
    i1 = jnp.min(jnp.where(masked == m1, row, N_EXP), axis=0, keepdims=True)
    masked2 = jnp.where(row == i1, NEG_INF, masked)
    m2 = jnp.max(masked2, axis=0, keepdims=True)
    i2 = jnp.min(jnp.where(masked2 == m2, row, N_EXP), axis=0, keepdims=True)
    oh1 = row == i1
    oh2 = row == i2
    s1 = jnp.sum(jnp.where(oh1, scores, 0.0), axis=0, keepdims=True)
    s2 = jnp.sum(jnp.where(oh2, scores, 0.0), axis=0, keepdims=True)
    den = s1 + s2

    comb = jnp.where(oh1, s1 / den, 0.0) + jnp.where(oh2, s2 / den, 0.0)
    comb_t = jnp.concatenate([comb, jnp.zeros((LANES - N_EXP, tm), f32)], axis=0).T
    h2_ref[rows, D:D + LANES] = comb_t

    e_lo = jnp.minimum(i1, i2) - EXP_PER_GROUP * bg
    e_hi = jnp.maximum(i1, i2) - EXP_PER_GROUP * bg
    pair = jnp.right_shift(e_lo * (7 - e_lo), 1) + (e_hi - e_lo - 1)
    bucket = bg * PAIRS + pair
    rowb = lax.broadcasted_iota(jnp.int32, (BUCKET_ROWS, tm), 0)
    ohb = rowb == bucket
    ohb_f = jnp.where(ohb, 1.0, 0.0)
    tot = carry_ref[:, 0:1] + _dot(ohb_f.astype(bf16), tri_ref[...])
    rank = jnp.sum(jnp.where(ohb, tot, 0.0), axis=0, keepdims=True)
    carry_ref[...] = carry_ref[...] + jnp.sum(ohb_f, axis=1, keepdims=True)

    route_ref[0:1, rows] = bucket.astype(f32)
    route_ref[1:2, rows] = rank
    route_ref[2:8, rows] = jnp.zeros((6, tm), f32)


def _post(l, x_ctx, x_lat, oa_ctx, oa_lat, yr_ctx, yr_lat, ga, gb, wpa, wpb, wo, mod, norm2_w, wr_pad,
          br_col, tri):
    tm = TM_POST
    row = lambda w: pl.BlockSpec((tm, w), lambda i: (i, 0))
    const = lambda a: pl.BlockSpec(a.shape, lambda i: (0,) * a.ndim)
    hbm = pl.BlockSpec(memory_space=pl.ANY)
    return pl.pallas_call(
        functools.partial(_post_kernel, l),
        grid=(T // tm,),
        in_specs=(_two_stream_specs(tm, D) + _two_stream_specs(tm, Q_A) + _two_stream_specs(tm, V_R)
                  + [row(D), row(D), hbm, hbm, hbm,
                     pl.BlockSpec((None, None, 6, D), lambda i: (l, _cond_of_tile(i, tm), 0, 0)),
                     _layer_row(l, D), const(wr_pad), const(br_col), const(tri)]),
        out_specs=[row(D), row(MOE_W), pl.BlockSpec((8, tm), lambda i: (0, i)),
                   pl.BlockSpec((BUCKET_ROWS, LANES), lambda i: (0, 0))],
        out_shape=[jax.ShapeDtypeStruct((T, D), f32), jax.ShapeDtypeStruct((T, MOE_W), f32),
                   jax.ShapeDtypeStruct((8, T), f32), jax.ShapeDtypeStruct((BUCKET_ROWS, LANES), f32)],
        scratch_shapes=[pltpu.VMEM((BUCKET_ROWS, LANES), f32), pltpu.VMEM((Q_A, D), bf16),
                        pltpu.VMEM((V_R, D), bf16), pltpu.VMEM((D, D), bf16),
                        pltpu.VMEM((2, D, W_CHUNK), f32), pltpu.SemaphoreType.DMA((2,))],
        compiler_params=pltpu.CompilerParams(
            dimension_semantics=("arbitrary",), vmem_limit_bytes=V7X_VMEM_LIMIT),
        name="post_router",
    )(x_ctx, x_lat, oa_ctx, oa_lat, yr_ctx, yr_lat, ga, gb, wpa, wpb, wo, mod, norm2_w, wr_pad, br_col, tri)


def _row_copy(src_ref, src_row, dst_ref, dst_row, sem):
    return pltpu.make_async_copy(src_ref.at[pl.ds(src_row, 1)], dst_ref.at[pl.ds(dst_row, 1)], sem)


def _dispatch_kernel(zf_ref, pos_ref, h_ref, xs_ref, zero_ref, sem):
    tm = h_ref.shape[0]

    @pl.when(pl.program_id(0) == 0)
    def _():
        zero_ref[...] = jnp.zeros_like(zero_ref)

        def tile_copy(t):
            return pltpu.make_async_copy(zero_ref, xs_ref.at[pl.ds(t * MOE_TR, MOE_TR)], sem)

        def start(t, carry):
            @pl.when(zf_ref[t] != 0)
            def _():
                tile_copy(t).start()
            return carry

        def wait(t, carry):
            @pl.when(zf_ref[t] != 0)
            def _():
                tile_copy(t).wait()
            return carry

        lax.fori_loop(0, MOE_TILES, start, 0)
        lax.fori_loop(0, MOE_TILES, wait, 0)

    for r in range(tm):
        _row_copy(h_ref, r, xs_ref, pos_ref[0, 0, r], sem).start(priority=r % 2)
    pltpu.make_async_copy(h_ref, xs_ref.at[pl.ds(0, tm)], sem).wait()


def _dispatch(zero_flags, pos_blocks, h2):
    tm = TM_DISP
    return pl.pallas_call(
        _dispatch_kernel,
        grid_spec=pltpu.PrefetchScalarGridSpec(
            num_scalar_prefetch=1,
            grid=(T // tm,),
            in_specs=[pl.BlockSpec((1, 1, tm), lambda i, zf: (i, 0, 0), memory_space=pltpu.SMEM),
                      pl.BlockSpec((tm, MOE_W), lambda i, zf: (i, 0))],
            out_specs=pl.BlockSpec(memory_space=pl.ANY),
            scratch_shapes=[pltpu.VMEM((MOE_TR, MOE_W), f32), pltpu.SemaphoreType.DMA(())],
        ),
        out_shape=jax.ShapeDtypeStruct((MOE_ROWS, MOE_W), f32),
        compiler_params=pltpu.CompilerParams(
            dimension_semantics=("arbitrary",), vmem_limit_bytes=V7X_VMEM_LIMIT),
        name="moe_dispatch",
    )(zero_flags, pos_blocks, h2)


def _experts_kernel(l, tg_ref, ta_ref, tb_ref, first_ref, nxt_ref, nt_ref, xs_ref, wg_hbm, wu_hbm, wd_hbm,
                    y_ref, wg_ref, wu_ref, wd_ref, sg_ref, su_ref, sd_ref, sems):
    i = pl.program_id(0)
    live = i < nt_ref[0]
    g = tg_ref[i]

    def fetch(grp):
        return [pltpu.make_async_copy(w.at[l, pl.ds(grp * EXP_PER_GROUP, EXP_PER_GROUP)], s, sems.at[k])
                for k, (w, s) in enumerate(((wg_hbm, sg_ref), (wu_hbm, su_ref), (wd_hbm, sd_ref)))]

    @pl.when(i == 0)
    def _():
        for c in fetch(tg_ref[0]):
            c.start()

    @pl.when(live & (first_ref[i] != 0))
    def _():
        for c in fetch(g):
            c.wait()
        for j in range(EXP_PER_GROUP):
            wg_ref[j] = sg_ref[j].astype(bf16)
            wu_ref[j] = su_ref[j].astype(bf16)
            wd_ref[j] = sd_ref[j].astype(bf16)

        @pl.when(nxt_ref[i] >= 0)
        def _():
            for c in fetch(nxt_ref[i]):
                c.start(priority=1)

    @pl.when(live)
    def _():
        x = xs_ref[:, 0:D].astype(bf16)
        comb = xs_ref[:, D:D + LANES]
        lane = lax.broadcasted_iota(jnp.int32, comb.shape, 1)
        acc = None
        for j in (ta_ref[i], tb_ref[i]):
            w_j = jnp.sum(jnp.where(lane == g * EXP_PER_GROUP + j, comb, 0.0), axis=1, keepdims=True)
            act = _silu(_dot(x, wg_ref[j])) * _dot(x, wu_ref[j]) * w_j
            part = _dot(act.astype(bf16), wd_ref[j])
            acc = part if acc is None else acc + part
        y_ref[...] = acc

    @pl.when(i >= nt_ref[0])
    def _():
        y_ref[...] = jnp.zeros_like(y_ref)


def _experts(l, plan, xs, wg, wu, wd):
    def tile(i, *prefetch):
        return jnp.minimum(i, prefetch[-1][0] - 1)

    hbm = pl.BlockSpec(memory_space=pl.ANY)
    return pl.pallas_call(
        functools.partial(_experts_kernel, l),
        grid_spec=pltpu.PrefetchScalarGridSpec(
            num_scalar_prefetch=6,
            grid=(MOE_TILES,),
            in_specs=[pl.BlockSpec((MOE_TR, MOE_W), lambda i, *p: (tile(i, *p), 0)), hbm, hbm, hbm],
            out_specs=pl.BlockSpec((MOE_TR, D), lambda i, *p: (i, 0)),
            scratch_shapes=[pltpu.VMEM((EXP_PER_GROUP, D, D_EXP), bf16),
                            pltpu.VMEM((EXP_PER_GROUP, D, D_EXP), bf16),
                            pltpu.VMEM((EXP_PER_GROUP, D_EXP, D), bf16),
                            pltpu.VMEM((EXP_PER_GROUP, D, D_EXP), f32),
                            pltpu.VMEM((EXP_PER_GROUP, D, D_EXP), f32),
                            pltpu.VMEM((EXP_PER_GROUP, D_EXP, D), f32), pltpu.SemaphoreType.DMA((3,))],
        ),
        out_shape=jax.ShapeDtypeStruct((MOE_ROWS, D), f32),
        compiler_params=pltpu.CompilerParams(
            dimension_semantics=("arbitrary",), vmem_limit_bytes=V7X_VMEM_LIMIT),
        name="moe_experts",
    )(*plan, xs, wg, wu, wd)


def _combine_kernel(pos_ref, posn_ref, x1_ref, mod_ref, y_ref, oc_ref, ol_ref, ya_ref, yb_ref, sems):
    tm = x1_ref.shape[0]
    i = pl.program_id(0)
    n_ctx = T_CTX // tm

    def pull(p_ref, buf, sem):
        for r in range(tm):
            _row_copy(y_ref, p_ref[0, 0, r], buf, r, sem).start(priority=r % 2)

    @pl.when(i == 0)
    def _():
        pull(pos_ref, ya_ref, sems.at[0])

    def step(cur, nxt, sem_cur, sem_nxt):
        @pl.when(i + 1 < pl.num_programs(0))
        def _():
            pull(posn_ref, nxt, sem_nxt)

        pltpu.make_async_copy(y_ref.at[pl.ds(0, tm)], cur, sem_cur).wait()
        out = x1_ref[...] + mod_ref[5:6, :] * cur[...]

        @pl.when(i < n_ctx)
        def _():
            oc_ref[...] = out

        @pl.when(i >= n_ctx)
        def _():
            ol_ref[...] = out

    @pl.when(i % 2 == 0)
    def _():
        step(ya_ref, yb_ref, sems.at[0], sems.at[1])

    @pl.when(i % 2 == 1)
    def _():
        step(yb_ref, ya_ref, sems.at[1], sems.at[0])


def _combine(l, pos_blocks, x1, mod, y):
    tm = TM_DISP
    n = T // tm
    return pl.pallas_call(
        _combine_kernel,
        grid=(n,),
        in_specs=[pl.BlockSpec((1, 1, tm), lambda i: (i, 0, 0), memory_space=pltpu.SMEM),
                  pl.BlockSpec((1, 1, tm), lambda i: (jnp.minimum(i + 1, n - 1), 0, 0),
                               memory_space=pltpu.SMEM),
                  pl.BlockSpec((tm, D), lambda i: (i, 0)),
                  pl.BlockSpec((None, None, 6, D), lambda i: (l, _cond_of_tile(i, tm), 0, 0)),
                  pl.BlockSpec(memory_space=pl.ANY)],
        out_specs=_two_stream_specs(tm, D),
        out_shape=[jax.ShapeDtypeStruct((T_CTX, D), f32), jax.ShapeDtypeStruct((T_LAT, D), f32)],
        scratch_shapes=[pltpu.VMEM((tm, D), f32), pltpu.VMEM((tm, D), f32),
                        pltpu.SemaphoreType.DMA((2,))],
        compiler_params=pltpu.CompilerParams(
            dimension_semantics=("arbitrary",), vmem_limit_bytes=V7X_VMEM_LIMIT),
        name="moe_combine",
    )(pos_blocks, pos_blocks, x1, mod, y)


def _plan_kernel(cnt_ref, off_ref, tg_ref, ta_ref, tb_ref, first_ref, nxt_ref, nt_ref, zf_ref, nxtg_ref):
    def bucket(b, start):
        n = lax.shift_right_logical(cnt_ref[b] + (MOE_TR - 1), MOE_TR.bit_length() - 1)
        off_ref[b] = start * MOE_TR
        g = b // PAIRS
        p = b - g * PAIRS
        lo = jnp.where(p >= 3, 1, 0) + jnp.where(p >= 5, 1, 0)
        hi = p + 1 - jnp.where(lo >= 1, 2, 0) - jnp.where(lo >= 2, 1, 0)

        def tile(t, c):
            tg_ref[t] = g
            ta_ref[t] = lo
            tb_ref[t] = hi
            zf_ref[t] = jnp.where(t == start + n - 1, 1, 0)
            return c

        lax.fori_loop(start, start + n, tile, 0)
        return start + n

    nt = lax.fori_loop(0, N_BUCKETS, bucket, jnp.int32(0))
    nt_ref[0] = nt

    def unused(t, c):
        tg_ref[t] = N_GROUPS - 1
        ta_ref[t] = 0
        tb_ref[t] = 1
        zf_ref[t] = 1
        first_ref[t] = 0
        nxt_ref[t] = -1
        return c

    lax.fori_loop(nt, MOE_TILES, unused, 0)

    def back(i, nxt_group):
        t = nt - 1 - i
        g = tg_ref[t]
        later = jnp.where(t + 1 < nt, tg_ref[jnp.minimum(t + 1, MOE_TILES - 1)], -1)
        is_last_of_group = later != g
        nxt_group = jnp.where(is_last_of_group, later, nxt_group)
        nxtg_ref[g] = nxt_group
        return nxt_group

    lax.fori_loop(0, nt, back, jnp.int32(-1))

    def fwd(t, c):
        g = tg_ref[t]
        first_ref[t] = jnp.where((t == 0) | (tg_ref[jnp.maximum(t - 1, 0)] != g), 1, 0)
        nxt_ref[t] = nxtg_ref[g]
        return c

    lax.fori_loop(0, nt, fwd, 0)


def _moe_plan(route, counts):
    smem = pl.BlockSpec(memory_space=pltpu.SMEM)
    tile_i32 = jax.ShapeDtypeStruct((MOE_TILES,), jnp.int32)
    off, tg, ta, tb, first, nxt, n_tiles, zero_flags = pl.pallas_call(
        _plan_kernel,
        in_specs=[smem],
        out_specs=[smem] * 8,
        out_shape=[jax.ShapeDtypeStruct((N_BUCKETS,), jnp.int32), tile_i32, tile_i32, tile_i32, tile_i32,
                   tile_i32, jax.ShapeDtypeStruct((1,), jnp.int32), tile_i32],
        scratch_shapes=[pltpu.SMEM((N_GROUPS,), jnp.int32)],
        name="moe_plan",
    )(counts[0:N_BUCKETS, 0].astype(jnp.int32))
    buckets = jnp.arange(N_BUCKETS, dtype=f32)[:, None]
    pos = jnp.sum(jnp.where(route[0][None, :] == buckets, off.astype(f32)[:, None], 0.0), axis=0) + route[1]
    pos_blocks = pos.astype(jnp.int32).reshape(T // TM_DISP, 1, TM_DISP)
    return pos_blocks, zero_flags, (tg, ta, tb, first, nxt, n_tiles)


def _rope_tables():
    pos = np.arange(DEC_SEQ)
    half = HD_A // 4
    freqs = ROPE_BASE ** (-np.arange(half, dtype=np.float64) / half)
    ang_r = (pos // GRID_W)[:, None] * freqs[None, :]
    ang_c = (pos % GRID_W)[:, None] * freqs[None, :]
    ang = np.concatenate([ang_r, ang_r, ang_c, ang_c], axis=1)
    sign = np.concatenate([-np.ones(half), np.ones(half)] * 2)[None, :]
    cos = np.tile(np.cos(ang), (1, 2))
    sin = np.tile(np.sin(ang) * sign, (1, 2))
    ident_c = np.ones((TM_PROJ, LANES))
    ident_s = np.zeros((TM_PROJ, LANES))
    return (jnp.asarray(np.concatenate([cos, ident_c]), f32),
            jnp.asarray(np.concatenate([sin, ident_s]), f32))


def kernel(x_prompt, x_sample, c, cache_k, cache_v, state_ret_fwd, state_ret_bwd, c_ctx,
           norm1_w, norm2_w, w_ada, b_ada, w_in, q_norm_w, k_norm_w, attn_sink,
           ret_decay_fwd, ret_decay_bwd, ret_gn_w, w_pa, w_pb, w_o, w_router, b_router,
           w_exp_gate, w_exp_up, w_exp_down):
    x_ctx, x_lat = x_prompt.reshape(T_CTX, D), x_sample.reshape(T_LAT, D)
    cond8 = jnp.zeros((N_COND, D), f32).at[0].set(c_ctx).at[1:1 + DEC_BATCH].set(c)
    mod = _adaln(cond8, w_ada, b_ada).reshape(DEPTH, N_COND, 6, D)

    cos_t, sin_t = _rope_tables()
    blk = np.arange(Q_A) // HD_A
    ones_blk = jnp.asarray((blk[:, None] == blk[None, :]) / HD_A, bf16)
    tri = jnp.asarray(np.triu(np.ones((POST_SUB, POST_SUB)), 1), bf16)
    per_layer = lambda w: w.reshape(DEPTH, 1, -1)
    norm1_w, norm2_w, gnw = per_layer(norm1_w), per_layer(norm2_w), per_layer(ret_gn_w)
    qnw = per_layer(jnp.tile(q_norm_w, (1, NH_A)) * (HD_A ** -0.5))
    knw = per_layer(jnp.tile(k_norm_w, (1, NKV_A)))
    wr_pad = jnp.concatenate(_split(jnp.pad(w_router, ((0, 0), (0, LANES - N_EXP)))), axis=1)
    br_col = jnp.broadcast_to(b_router[:, None], (N_EXP, LANES))
    ck = cache_k.reshape(DEC_BATCH, DEPTH, PAST, KV_A)
    cv = cache_v.reshape(DEC_BATCH, DEPTH, PAST, KV_A)
    s0 = jnp.concatenate([state_ret_fwd, state_ret_bwd], axis=3)
    lg = jnp.log1p(-jnp.exp(jnp.concatenate([ret_decay_fwd, ret_decay_bwd], axis=1).astype(f32)))

    new_k, new_v, new_sf, new_sb = [], [], [], []
    for l in range(DEPTH):
        qa, ka, va, qr, kr, vr, gr, ga, gb = _inproj(l, x_ctx, x_lat, norm1_w, mod, w_in, cos_t, sin_t,
                                                     ones_blk, qnw, knw)
        oa_ctx = _ctx_attn(l, qa, ka, va, attn_sink)
        oa_lat = _lat_attn(l, qa, ka, va, ck, cv, cos_t, sin_t, attn_sink)
        yr_ctx, sf, sb = _retention(l, qr, kr, vr, gr, None, lg, gnw, BATCH, SEQ, 0, CTX_SEQS_PER_STEP)
        yr_lat, _, _ = _retention(l, qr, kr, vr, gr, s0, lg, gnw, DEC_BATCH, DEC_SEQ,
                                  T_CTX // DEC_SEQ, 1)
        x1, h2, route, counts = _post(l, x_ctx, x_lat, oa_ctx, oa_lat, yr_ctx, yr_lat, ga, gb,
                                      w_pa, w_pb, w_o, mod, norm2_w, wr_pad, br_col, tri)
        pos_blocks, zero_flags, plan = _moe_plan(route, counts)
        xs = _dispatch(zero_flags, pos_blocks, h2)
        y = _experts(l, plan, xs, w_exp_gate, w_exp_up, w_exp_down)
        x_ctx, x_lat = _combine(l, pos_blocks, x1, mod, y)

        new_k.append(ka[:T_CTX].reshape(BATCH, SEQ, NKV_A, HD_A))
        new_v.append(va[:T_CTX].reshape(BATCH, SEQ, NKV_A, HD_A))
        new_sf.append(sf)
        new_sb.append(sb)

    return (x_ctx.reshape(BATCH, SEQ, D), x_lat.reshape(DEC_BATCH, DEC_SEQ, D),
            jnp.stack(new_k, axis=1), jnp.stack(new_v, axis=1),
            jnp.stack(new_sf, axis=1), jnp.stack(new_sb, axis=1))
```

```python
import functools

import numpy as np
import jax
import jax.numpy as jnp
from jax import lax
from jax.experimental import pallas as pl
from jax.experimental.pallas import tpu as pltpu

D = 1024
BATCH, SEQ = 16, 256
DEC_BATCH, DEC_SEQ = 2, 2048
DEPTH = 2
PAST = 512
GRID_W = 64
NH_A, NKV_A, HD_A = 8, 2, 64
WINDOW = 128
NH_R, DK_R, DV_R = 4, 64, 128
CHUNK = 128
N_EXP, N_GROUPS, EXP_PER_GROUP = 16, 4, 4
D_EXP = 512
ROPE_BASE = 10000.0
EPS = 1e-6
NEG_INF = -1e30

Q_A = NH_A * HD_A
KV_A = NKV_A * HD_A
QK_R = NH_R * DK_R
V_R = NH_R * DV_R
C_QA = (0, Q_A)
C_KA = (C_QA[1], C_QA[1] + KV_A)
C_VA = (C_KA[1], C_KA[1] + KV_A)
C_QR = (C_VA[1], C_VA[1] + QK_R)
C_KR = (C_QR[1], C_QR[1] + QK_R)
C_VR = (C_KR[1], C_KR[1] + V_R)
C_GR = (C_VR[1], C_VR[1] + V_R)
C_GA = (C_GR[1], C_GR[1] + D)
C_GB = (C_GA[1], C_GA[1] + D)
D_IN = C_GB[1]

T_CTX = BATCH * SEQ
T_LAT = DEC_BATCH * DEC_SEQ
T = T_CTX + T_LAT
N_COND = 8

LANES = 128
MXU_TILE = 256
V7X_VMEM_LIMIT = 56 * 1024 * 1024

TM_PROJ = 512
TM_POST = 512
POST_SUB = 512
W_CHUNK = 256
W_STAGES = 4
TM_DISP = 512
MOE_TR = 256
PAIRS = EXP_PER_GROUP * (EXP_PER_GROUP - 1) // 2
N_BUCKETS = N_GROUPS * PAIRS
BUCKET_ROWS = 32
MOE_TILES = T // MOE_TR + N_BUCKETS
MOE_ROWS = MOE_TILES * MOE_TR
MOE_W = D + LANES
ATT_QB = 256
RET_UNROLL = 4
CTX_SEQS_PER_STEP = 4

f32 = jnp.float32
bf16 = jnp.bfloat16


def _dot(a, b):
    return jnp.dot(a, b, preferred_element_type=f32)


def _dot_t(a, b):
    return lax.dot_general(a, b, (((1,), (1,)), ((), ())), preferred_element_type=f32)


def _split(x):
    hi = x.astype(bf16)
    lo = (x - hi.astype(f32)).astype(bf16)
    return hi, lo


def _dot3(a, b):
    m = a.shape[0]
    ah, al = _split(a)
    bh, bl = _split(b)
    both = _dot(jnp.concatenate([ah, al], axis=0), bh)
    return both[:m] + (_dot(ah, bl) + both[m:])


def _sigmoid(x):
    return 1.0 / (1.0 + jnp.exp(-x))


def _silu(x):
    return x * _sigmoid(x)


def _layer_row(l, n):
    return pl.BlockSpec((None, 1, n), lambda *_: (l, 0, 0))


def _cond_of_tile(i, tm):
    n_ctx = T_CTX // tm
    per_b = DEC_SEQ // tm
    return jnp.where(i < n_ctx, 0, 1 + jnp.maximum(i - n_ctx, 0) // per_b)


def _adaln_kernel(cond_ref, w_ref, b_ref, o_ref):
    a = _silu(cond_ref[...])
    o_ref[0] = _dot3(a, w_ref[0]) + b_ref[0]


def _adaln(cond8, w_ada, b_ada):
    tn = 2048
    return pl.pallas_call(
        _adaln_kernel,
        grid=(DEPTH, 6 * D // tn),
        in_specs=[
            pl.BlockSpec((N_COND, D), lambda l, j: (0, 0)),
            pl.BlockSpec((1, D, tn), lambda l, j: (l, 0, j)),
            pl.BlockSpec((1, 1, tn), lambda l, j: (l, 0, j)),
        ],
        out_specs=pl.BlockSpec((1, N_COND, tn), lambda l, j: (l, 0, j)),
        out_shape=jax.ShapeDtypeStruct((DEPTH, N_COND, 6 * D), f32),
        compiler_params=pltpu.CompilerParams(
            dimension_semantics=("parallel", "parallel"), vmem_limit_bytes=V7X_VMEM_LIMIT),
        name="adaln",
    )(cond8, w_ada, b_ada.reshape(DEPTH, 1, 6 * D))


def _rope(x, cos, sin_signed, first_half):
    fwd = pltpu.roll(x, 16, 1)
    bwd = pltpu.roll(x, LANES - 16, 1)
    partner = jnp.where(first_half, bwd, fwd)
    return x * cos + partner * sin_signed


def _head_rms(x, ones_blk, w):
    n = x.shape[1]
    sq_hi, sq_lo = _split(x * x)
    wt = min(n, MXU_TILE)
    blk = ones_blk[0:wt, 0:wt]
    parts = [_dot(sq_hi[:, c:c + wt], blk) + _dot(sq_lo[:, c:c + wt], blk) for c in range(0, n, wt)]
    mean = parts[0] if len(parts) == 1 else jnp.concatenate(parts, axis=1)
    return x * lax.rsqrt(mean + EPS) * w


def _load_cast(w_hbm, dst_ref, stage_ref, sems):
    k, n = w_hbm.shape
    depth = stage_ref.shape[0]

    def chunk_copy(c):
        return pltpu.make_async_copy(w_hbm.at[:, pl.ds(c * W_CHUNK, W_CHUNK)],
                                     stage_ref.at[c % depth, pl.ds(0, k)], sems.at[c % depth])

    n_chunks = n // W_CHUNK
    for c in range(min(depth - 1, n_chunks)):
        chunk_copy(c).start()
    for c in range(n_chunks):
        if c + depth - 1 < n_chunks:
            chunk_copy(c + depth - 1).start()
        chunk_copy(c).wait()
        dst_ref[:, c * W_CHUNK:(c + 1) * W_CHUNK] = stage_ref[c % depth, 0:k, :].astype(bf16)


def _tile_of_two(i, n_first, a_ref, b_ref):
    return jnp.where(i < n_first, a_ref[...], b_ref[...])


def _inproj_kernel(l, xc_ref, xl_ref, n1_ref, mod_ref, w_hbm, cos_ref, sin_ref, ones_ref, qnw_ref,
                   knw_ref, qa_ref, ka_ref, va_ref, qr_ref, kr_ref, vr_ref, gr_ref, ga_ref, gb_ref,
                   w_ref, stage_ref, sems):
    i = pl.program_id(0)

    @pl.when(i == 0)
    def _():
        _load_cast(w_hbm.at[l], w_ref, stage_ref, sems)

    x = _tile_of_two(i, T_CTX // xc_ref.shape[0], xc_ref, xl_ref)
    y = x * lax.rsqrt(jnp.mean(x * x, axis=-1, keepdims=True) + EPS) * n1_ref[...]
    h = (y * (1.0 + mod_ref[1:2, :]) + mod_ref[0:1, :]).astype(bf16)

    whole = _dot(h, w_ref[...])

    def proj(c):
        return whole[:, c[0]:c[1]]

    cos = cos_ref[...]
    sin = sin_ref[...]
    lane = lax.broadcasted_iota(jnp.int32, cos.shape, 1)
    first_half = (lane % 32) < 16

    def rope_all(v):
        parts = [_rope(v[:, j:j + LANES], cos, sin, first_half) for j in range(0, v.shape[1], LANES)]
        return parts[0] if len(parts) == 1 else jnp.concatenate(parts, axis=1)

    ones_blk = ones_ref[...]
    qa = _head_rms(proj(C_QA), ones_blk, qnw_ref[...])
    qa_ref[...] = rope_all(qa).astype(bf16)
    ka_ref[...] = _head_rms(proj(C_KA), ones_blk, knw_ref[...])
    va_ref[...] = proj(C_VA)
    qr_ref[...] = rope_all(proj(C_QR)).astype(bf16)
    kr_ref[...] = (rope_all(proj(C_KR)) * (DK_R ** -0.5)).astype(bf16)
    vr_ref[...] = proj(C_VR).astype(bf16)
    gr_ref[...] = proj(C_GR).astype(bf16)
    ga_ref[...] = proj(C_GA).astype(bf16)
    gb_ref[...] = proj(C_GB).astype(bf16)


def _two_stream_specs(tm, width):
    n_ctx = T_CTX // tm
    return [pl.BlockSpec((tm, width), lambda i, *_: (jnp.minimum(i, n_ctx - 1), 0)),
            pl.BlockSpec((tm, width), lambda i, *_: (jnp.maximum(i - n_ctx, 0), 0))]


def _inproj(l, x_ctx, x_lat, norm1_w, mod, w_in, cos_t, sin_t, ones_blk, qnw, knw):
    tm = TM_PROJ
    n_ctx = T_CTX // tm
    per_b = DEC_SEQ // tm

    def tab_map(i):
        return (jnp.where(i < n_ctx, per_b, jnp.maximum(i - n_ctx, 0) % per_b), 0)

    row = lambda i: (i, 0)
    const = lambda i: (0, 0)
    widths = [(Q_A, bf16), (KV_A, f32), (KV_A, f32), (QK_R, bf16), (QK_R, bf16), (V_R, bf16),
              (V_R, bf16), (D, bf16), (D, bf16)]
    return pl.pallas_call(
        functools.partial(_inproj_kernel, l),
        grid=(T // tm,),
        in_specs=_two_stream_specs(tm, D) + [
            _layer_row(l, D),
            pl.BlockSpec((None, None, 6, D), lambda i: (l, _cond_of_tile(i, tm), 0, 0)),
            pl.BlockSpec(memory_space=pl.ANY),
            pl.BlockSpec((tm, LANES), tab_map),
            pl.BlockSpec((tm, LANES), tab_map),
            pl.BlockSpec((Q_A, Q_A), const),
            _layer_row(l, Q_A),
            _layer_row(l, KV_A),
        ],
        out_specs=[pl.BlockSpec((tm, w), row) for w, _ in widths],
        out_shape=[jax.ShapeDtypeStruct((T, w), dt) for w, dt in widths],
        scratch_shapes=[pltpu.VMEM((D, D_IN), bf16), pltpu.VMEM((W_STAGES, D, W_CHUNK), f32),
                        pltpu.SemaphoreType.DMA((W_STAGES,))],
        compiler_params=pltpu.CompilerParams(
            dimension_semantics=("arbitrary",), vmem_limit_bytes=V7X_VMEM_LIMIT),
        name="inproj",
    )(x_ctx, x_lat, norm1_w, mod, w_in, cos_t, sin_t, ones_blk, qnw, knw)


def _head_blocks(t, kv, lo_mask):
    r = pltpu.roll(t, HD_A, 1)
    if kv == 0:
        a = jnp.where(lo_mask, t, 0.0)
        b = jnp.where(lo_mask, 0.0, r)
    else:
        a = jnp.where(lo_mask, r, 0.0)
        b = jnp.where(lo_mask, 0.0, t)
    return jnp.concatenate([a, b], axis=0).astype(bf16)


def _ctx_attn_kernel(l, sink_ref, q_ref, k_ref, v_ref, o_ref):
    n = SEQ
    lo_mask = lax.broadcasted_iota(jnp.int32, (n, LANES), 1) < HD_A
    for b in range(CTX_SEQS_PER_STEP):
        rows = slice(b * n, (b + 1) * n)
        k = k_ref[rows, :]
        v = v_ref[rows, :]
        for kv in range(NKV_A):
            kblk = _head_blocks(k, kv, lo_mask)
            vblk = _head_blocks(v, kv, lo_mask)
            q2 = jnp.concatenate([q_ref[rows, (2 * kv) * LANES:(2 * kv + 1) * LANES],
                                  q_ref[rows, (2 * kv + 1) * LANES:(2 * kv + 2) * LANES]], axis=0)
            s = _dot_t(q2, kblk)
            row = lax.broadcasted_iota(jnp.int32, (2 * n, 1), 0)
            ps, invs = [], []
            for hh in range(2):
                sk = jnp.where(row < n, sink_ref[l, 4 * kv + hh], sink_ref[l, 4 * kv + 2 + hh])
                sh = s[:, hh * n:(hh + 1) * n]
                m = jnp.maximum(jnp.max(sh, axis=-1, keepdims=True), sk)
                p = jnp.exp(sh - m)
                invs.append(1.0 / (jnp.sum(p, axis=-1, keepdims=True) + jnp.exp(sk - m)))
                ps.append(p.astype(bf16))
            lo2 = lax.broadcasted_iota(jnp.int32, (2 * n, LANES), 1) < HD_A
            o = _dot(jnp.concatenate(ps, axis=1), vblk) * jnp.where(lo2, invs[0], invs[1])
            o_ref[rows, (2 * kv) * LANES:(2 * kv + 1) * LANES] = o[:n].astype(bf16)
            o_ref[rows, (2 * kv + 1) * LANES:(2 * kv + 2) * LANES] = o[n:].astype(bf16)


def _ctx_attn(l, qa, ka, va, sink):
    blk = lambda w: pl.BlockSpec((CTX_SEQS_PER_STEP * SEQ, w), lambda b: (b, 0))
    return pl.pallas_call(
        functools.partial(_ctx_attn_kernel, l),
        grid=(BATCH // CTX_SEQS_PER_STEP,),
        in_specs=[pl.BlockSpec(memory_space=pltpu.SMEM), blk(Q_A), blk(KV_A), blk(KV_A)],
        out_specs=blk(Q_A),
        out_shape=jax.ShapeDtypeStruct((T_CTX, Q_A), bf16),
        compiler_params=pltpu.CompilerParams(
            dimension_semantics=("parallel",), vmem_limit_bytes=V7X_VMEM_LIMIT),
        name="ctx_attn",
    )(sink, qa, ka, va)


def _lat_attn_kernel(l, sink_ref, q_ref, k_ref, v_ref, kc_ref, vc_ref, cos_ref, sin_ref, o_ref):
    j = pl.program_id(1)
    qb = ATT_QB
    win = 2 * qb
    ws = pl.multiple_of(jnp.clip(j * qb - WINDOW, 0, DEC_SEQ - win), WINDOW)
    lo_mask = lax.broadcasted_iota(jnp.int32, (win, LANES), 1) < HD_A
    lane = lax.broadcasted_iota(jnp.int32, (win, LANES), 1)
    kw = _rope(k_ref[pl.ds(ws, win), :], cos_ref[pl.ds(ws, win), :], sin_ref[pl.ds(ws, win), :],
               (lane % 32) < 16)
    vw = v_ref[pl.ds(ws, win), :]
    kc = kc_ref[...]
    vc = vc_ref[...]
    qpos = j * qb + (lax.broadcasted_iota(jnp.int32, (2 * qb, win), 0) & (qb - 1))
    kpos = ws + lax.broadcasted_iota(jnp.int32, (2 * qb, win), 1)
    valid = jnp.abs(qpos - kpos) <= WINDOW
    out_lo = lax.broadcasted_iota(jnp.int32, (2 * qb, LANES), 1) < HD_A
    for kv in range(NKV_A):
        kc_blk = _head_blocks(kc, kv, lo_mask[:PAST])
        vc_blk = _head_blocks(vc, kv, lo_mask[:PAST])
        kw_blk = _head_blocks(kw, kv, lo_mask)
        vw_blk = _head_blocks(vw, kv, lo_mask)
        q2 = jnp.concatenate([q_ref[:, (2 * kv) * LANES:(2 * kv + 1) * LANES],
                              q_ref[:, (2 * kv + 1) * LANES:(2 * kv + 2) * LANES]], axis=0)
        s_c = _dot_t(q2, kc_blk)
        s_w = _dot_t(q2, kw_blk)
        pcs, pws, invs = [], [], []
        for hh in range(2):
            row = lax.broadcasted_iota(jnp.int32, (2 * qb, 1), 0)
            sk = jnp.where(row < qb, sink_ref[l, 4 * kv + hh], sink_ref[l, 4 * kv + 2 + hh])
            sc = s_c[:, hh * PAST:(hh + 1) * PAST]
            sw = jnp.where(valid, s_w[:, hh * win:(hh + 1) * win], NEG_INF)
            m = jnp.maximum(jnp.maximum(jnp.max(sc, axis=-1, keepdims=True),
                                        jnp.max(sw, axis=-1, keepdims=True)), sk)
            pc = jnp.exp(sc - m)
            pw = jnp.exp(sw - m)
            den = (jnp.sum(pc, axis=-1, keepdims=True) + jnp.sum(pw, axis=-1, keepdims=True)
                   + jnp.exp(sk - m))
            invs.append(1.0 / den)
            pcs.append(pc.astype(bf16))
            pws.append(pw.astype(bf16))
        o = _dot(jnp.concatenate(pcs, axis=1), vc_blk) + _dot(jnp.concatenate(pws, axis=1), vw_blk)
        o = o * jnp.where(out_lo, invs[0], invs[1])
        o_ref[:, (2 * kv) * LANES:(2 * kv + 1) * LANES] = o[:qb].astype(bf16)
        o_ref[:, (2 * kv + 1) * LANES:(2 * kv + 2) * LANES] = o[qb:].astype(bf16)


def _lat_attn(l, qa, ka, va, cache_k, cache_v, cos_l, sin_l, sink):
    qb = ATT_QB
    nq = DEC_SEQ // qb
    ctx_blocks = T_CTX // DEC_SEQ
    seq = lambda b, j: (ctx_blocks + b, 0)
    return pl.pallas_call(
        functools.partial(_lat_attn_kernel, l),
        grid=(DEC_BATCH, nq),
        in_specs=[
            pl.BlockSpec(memory_space=pltpu.SMEM),
            pl.BlockSpec((qb, Q_A), lambda b, j: (T_CTX // qb + b * nq + j, 0)),
            pl.BlockSpec((DEC_SEQ, KV_A), seq),
            pl.BlockSpec((DEC_SEQ, KV_A), seq),
            pl.BlockSpec((None, None, PAST, KV_A), lambda b, j: (b, l, 0, 0)),
            pl.BlockSpec((None, None, PAST, KV_A), lambda b, j: (b, l, 0, 0)),
            pl.BlockSpec((DEC_SEQ, LANES), lambda b, j: (0, 0)),
            pl.BlockSpec((DEC_SEQ, LANES), lambda b, j: (0, 0)),
        ],
        out_specs=pl.BlockSpec((qb, Q_A), lambda b, j: (b * nq + j, 0)),
        out_shape=jax.ShapeDtypeStruct((T_LAT, Q_A), bf16),
        compiler_params=pltpu.CompilerParams(
            dimension_semantics=("parallel", "parallel"), vmem_limit_bytes=V7X_VMEM_LIMIT),
        name="lat_attn",
    )(sink, qa, ka, va, cache_k, cache_v, cos_l, sin_l)


def _dup_heads(pair, lo_mask):
    r = pltpu.roll(pair, DK_R, 1)
    return jnp.where(lo_mask, pair, r), jnp.where(lo_mask, r, pair)


def _retention_kernel(l, has_s0, n_seq, n_chunks, *refs):
    if has_s0:
        (lg_ref, q_ref, k_ref, v_ref, g_ref, s0_ref, gnw_ref,
         y_ref, sf_ref, sb_ref, ds_ref, st_ref, mask_ref, qdec_ref, kdec_ref, cdec_ref) = refs
    else:
        (lg_ref, q_ref, k_ref, v_ref, g_ref, gnw_ref,
         y_ref, sf_ref, sb_ref, ds_ref, st_ref, mask_ref, qdec_ref, kdec_ref, cdec_ref) = refs
        s0_ref = None
    C = CHUNK
    lo_mask = lax.broadcasted_iota(jnp.int32, (C, LANES), 1) < DK_R

    ri = lax.broadcasted_iota(jnp.int32, (C, C), 0).astype(f32)
    diff = ri - lax.broadcasted_iota(jnp.int32, (C, C), 1).astype(f32)
    for h in range(NH_R):
        lg_f, lg_b = lg_ref[l, h], lg_ref[l, NH_R + h]
        mask_ref[h] = (jnp.where(diff >= 0, jnp.exp(jnp.maximum(diff, 0.0) * lg_f), 0.0)
                       + jnp.where(diff <= 0, jnp.exp(jnp.maximum(-diff, 0.0) * lg_b), 0.0))
        qdec_ref[:, h * LANES:(h + 1) * LANES] = jnp.exp(
            jnp.where(lo_mask, (ri + 1.0) * lg_f, (C - ri) * lg_b))
        kdec_ref[:, h * LANES:(h + 1) * LANES] = jnp.exp(
            jnp.where(lo_mask, (C - 1.0 - ri) * lg_f, ri * lg_b))
        cdec_ref[h] = jnp.exp(jnp.where(ri < DK_R, C * lg_f, C * lg_b))

    def inc_body(c, carry):
        r0 = pl.multiple_of(c * C, C)
        for pr in range(2):
            kp = k_ref[pl.ds(r0, C), pr * LANES:(pr + 1) * LANES].astype(f32)
            for hh, kd in enumerate(_dup_heads(kp, lo_mask)):
                h = 2 * pr + hh
                kd = (kd * kdec_ref[:, h * LANES:(h + 1) * LANES]).astype(bf16)
                vh = v_ref[pl.ds(r0, C), h * DV_R:(h + 1) * DV_R]
                ds_ref[c, h] = lax.dot_general(kd, vh, (((0,), (0,)), ((), ())),
                                               preferred_element_type=f32)
        return carry

    lax.fori_loop(0, n_seq * n_chunks, inc_body, 0, unroll=RET_UNROLL)

    for sq in range(n_seq):
        base = sq * n_chunks
        for h in range(NH_R):
            cf = cdec_ref[h, 0:DK_R, :]
            cb = cdec_ref[h, DK_R:2 * DK_R, :]
            if has_s0:
                init_f = s0_ref[h, 0:DK_R, :]
                init_b = s0_ref[h, DK_R:2 * DK_R, :]
            else:
                init_f = jnp.zeros((DK_R, DV_R), f32)
                init_b = init_f

            def fwd_body(i, s, h=h, cf=cf, base=base):
                c = base + i
                st_ref[c, h, 0:DK_R, :] = s
                return s * cf + ds_ref[c, h, 0:DK_R, :]

            def bwd_body(i, s, h=h, cb=cb, base=base):
                c = base + n_chunks - 1 - i
                st_ref[c, h, DK_R:2 * DK_R, :] = s
                return s * cb + ds_ref[c, h, DK_R:2 * DK_R, :]

            sf_ref[sq, h] = lax.fori_loop(0, n_chunks, fwd_body, init_f)
            sb_ref[sq, h] = lax.fori_loop(0, n_chunks, bwd_body, init_b)

    def out_body(c, carry):
        r0 = pl.multiple_of(c * C, C)
        for pr in range(2):
            qp = q_ref[pl.ds(r0, C), pr * LANES:(pr + 1) * LANES]
            kp = k_ref[pl.ds(r0, C), pr * LANES:(pr + 1) * LANES].astype(f32)
            kblk = jnp.concatenate([jnp.where(lo_mask, kp, 0.0), jnp.where(lo_mask, 0.0, kp)],
                                   axis=0).astype(bf16)
            a2 = _dot_t(qp, kblk)
            for hh, qd in enumerate(_dup_heads(qp.astype(f32), lo_mask)):
                h = 2 * pr + hh
                a = (a2[:, hh * C:(hh + 1) * C] * mask_ref[h]).astype(bf16)
                vh = v_ref[pl.ds(r0, C), h * DV_R:(h + 1) * DV_R]
                qd = (qd * qdec_ref[:, h * LANES:(h + 1) * LANES]).astype(bf16)
                o = _dot(a, vh) + _dot(qd, st_ref[c, h].astype(bf16))
                mu = jnp.mean(o, axis=-1, keepdims=True)
                d = o - mu
                var = jnp.mean(d * d, axis=-1, keepdims=True)
                yh = d * lax.rsqrt(var + EPS) * gnw_ref[:, h * DV_R:(h + 1) * DV_R]
                g = g_ref[pl.ds(r0, C), h * DV_R:(h + 1) * DV_R].astype(f32)
                y_ref[pl.ds(r0, C), h * DV_R:(h + 1) * DV_R] = (yh * _silu(g)).astype(bf16)
        return carry

    lax.fori_loop(0, n_seq * n_chunks, out_body, 0, unroll=RET_UNROLL)


def _retention(l, qr, kr, vr, gr, s0, lg, gnw, nb, seq, row_block0, n_seq):
    n_chunks = seq // CHUNK
    has_s0 = s0 is not None
    rows = n_seq * seq
    tok = lambda w: pl.BlockSpec((rows, w), lambda b: (row_block0 + b, 0))
    in_specs = [pl.BlockSpec(memory_space=pltpu.SMEM), tok(QK_R), tok(QK_R), tok(V_R), tok(V_R)]
    args = [lg, qr, kr, vr, gr]
    if has_s0:
        in_specs.append(pl.BlockSpec((None, None, NH_R, 2 * DK_R, DV_R), lambda b: (b, l, 0, 0, 0)))
        args.append(s0)
    in_specs += [_layer_row(l, V_R)]
    args += [gnw]
    st_spec = pl.BlockSpec((n_seq, NH_R, DK_R, DV_R), lambda b: (b, 0, 0, 0))
    return pl.pallas_call(
        functools.partial(_retention_kernel, l, has_s0, n_seq, n_chunks),
        grid=(nb // n_seq,),
        in_specs=in_specs,
        out_specs=[pl.BlockSpec((rows, V_R), lambda b: (b, 0)), st_spec, st_spec],
        out_shape=[jax.ShapeDtypeStruct((nb * seq, V_R), bf16),
                   jax.ShapeDtypeStruct((nb, NH_R, DK_R, DV_R), f32),
                   jax.ShapeDtypeStruct((nb, NH_R, DK_R, DV_R), f32)],
        scratch_shapes=[pltpu.VMEM((n_seq * n_chunks, NH_R, 2 * DK_R, DV_R), f32),
                        pltpu.VMEM((n_seq * n_chunks, NH_R, 2 * DK_R, DV_R), f32),
                        pltpu.VMEM((NH_R, CHUNK, CHUNK), f32), pltpu.VMEM((CHUNK, NH_R * LANES), f32),
                        pltpu.VMEM((CHUNK, NH_R * LANES), f32), pltpu.VMEM((NH_R, 2 * DK_R, DV_R), f32)],
        compiler_params=pltpu.CompilerParams(
            dimension_semantics=("parallel",), vmem_limit_bytes=V7X_VMEM_LIMIT),
        name="retention_lat" if has_s0 else "retention_ctx",
    )(*args)


def _post_kernel(l, xc_ref, xl_ref, oac_ref, oal_ref, yrc_ref, yrl_ref, ga_ref, gb_ref, wpa_hbm, wpb_hbm,
                 wo_hbm, mod_ref, n2_ref, wr_ref, br_ref, tri_ref, x1_ref, h2_ref, route_ref, cnt_ref,
                 carry_ref, wpa_ref, wpb_ref, wo_ref, stage_ref, sems):
    i = pl.program_id(0)
    tm = xc_ref.shape[0]
    n_ctx = T_CTX // tm

    @pl.when(i == 0)
    def _():
        carry_ref[...] = jnp.zeros_like(carry_ref)
        _load_cast(wpa_hbm.at[l], wpa_ref, stage_ref, sems)
        _load_cast(wpb_hbm.at[l], wpb_ref, stage_ref, sems)
        _load_cast(wo_hbm.at[l], wo_ref, stage_ref, sems)

    for r0 in range(0, tm, POST_SUB):
        _post_subtile(i < n_ctx, slice(r0, r0 + POST_SUB), xc_ref, xl_ref, oac_ref, oal_ref, yrc_ref,
                      yrl_ref, ga_ref, gb_ref, mod_ref, n2_ref, wr_ref, br_ref, tri_ref, x1_ref, h2_ref,
                      route_ref, carry_ref, wpa_ref, wpb_ref, wo_ref)
    cnt_ref[...] = carry_ref[...]


def _post_subtile(is_ctx, rows, xc_ref, xl_ref, oac_ref, oal_ref, yrc_ref, yrl_ref, ga_ref, gb_ref, mod_ref,
                  n2_ref, wr_ref, br_ref, tri_ref, x1_ref, h2_ref, route_ref, carry_ref, wpa_ref, wpb_ref,
                  wo_ref):
    tm = rows.stop - rows.start
    pick = lambda a_ref, b_ref: jnp.where(is_ctx, a_ref[rows, :], b_ref[rows, :])
    ga = _sigmoid(ga_ref[rows, :].astype(f32))
    gb = _sigmoid(gb_ref[rows, :].astype(f32))
    merged = (ga * _dot(pick(oac_ref, oal_ref), wpa_ref[...])
              + gb * _dot(pick(yrc_ref, yrl_ref), wpb_ref[...]))
    mix = _dot(merged.astype(bf16), wo_ref[...])
    x1 = pick(xc_ref, xl_ref) + mod_ref[2:3, :] * mix
    x1_ref[rows, :] = x1
    y = x1 * lax.rsqrt(jnp.mean(x1 * x1, axis=-1, keepdims=True) + EPS) * n2_ref[...]
    h2 = y * (1.0 + mod_ref[4:5, :]) + mod_ref[3:4, :]
    h2_ref[rows, 0:D] = h2

    h_hi, h_lo = _split(h2)
    both = _dot(h_hi, wr_ref[...])
    logits = both[:, 0:LANES] + (both[:, LANES:2 * LANES] + _dot(h_lo, wr_ref[:, 0:LANES]))
    lt = logits.T[0:N_EXP, :]
    scores = _sigmoid(lt)
    sel = scores + br_ref[:, 0:1]
    row = lax.broadcasted_iota(jnp.int32, (N_EXP, tm), 0)

    best = None
    bg = None
    for g in range(N_GROUPS):
        a, b, c, d = (sel[EXP_PER_GROUP * g + k:EXP_PER_GROUP * g + k + 1, :] for k in range(4))
        p, q = jnp.maximum(a, b), jnp.minimum(a, b)
        r, s = jnp.maximum(c, d), jnp.minimum(c, d)
        gs = jnp.maximum(p, r) + jnp.maximum(jnp.minimum(p, r), jnp.maximum(q, s))
        if g == 0:
            best, bg = gs, jnp.zeros((1, tm), jnp.int32)
        else:
            upd = gs > best
            bg = jnp.where(upd, g, bg)
            best = jnp.where(upd, gs, best)
    masked = jnp.where(jnp.right_shift(row, 2) == bg, sel, NEG_INF)
    m1 = jnp.max(masked, axis=0, keepdims=True)
    i1 = jnp.min(jnp.where(masked == m1, row, N_EXP), axis=0, keepdims=True)
    masked2 = jnp.where(row == i1, NEG_INF, masked)
    m2 = jnp.max(masked2, axis=0, keepdims=True)
    i2 = jnp.min(jnp.where(masked2 == m2, row, N_EXP), axis=0, keepdims=True)
    oh1 = row == i1
    oh2 = row == i2
    s1 = jnp.sum(jnp.where(oh1, scores, 0.0), axis=0, keepdims=True)
    s2 = jnp.sum(jnp.where(oh2, scores, 0.0), axis=0, keepdims=True)
    den = s1 + s2

    comb = jnp.where(oh1, s1 / den, 0.0) + jnp.where(oh2, s2 / den, 0.0)
    comb_t = jnp.concatenate([comb, jnp.zeros((LANES - N_EXP, tm), f32)], axis=0).T
    h2_ref[rows, D:D + LANES] = comb_t

    e_lo = jnp.minimum(i1, i2) - EXP_PER_GROUP * bg
    e_hi = jnp.maximum(i1, i2) - EXP_PER_GROUP * bg
    pair = jnp.right_shift(e_lo * (7 - e_lo), 1) + (e_hi - e_lo - 1)
    bucket = bg * PAIRS + pair
    rowb = lax.broadcasted_iota(jnp.int32, (BUCKET_ROWS, tm), 0)
    ohb = rowb == bucket
    ohb_f = jnp.where(ohb, 1.0, 0.0)
    tot = carry_ref[:, 0:1] + _dot(ohb_f.astype(bf16), tri_ref[...])
    rank = jnp.sum(jnp.where(ohb, tot, 0.0), axis=0, keepdims=True)
    carry_ref[...] = carry_ref[...] + jnp.sum(ohb_f, axis=1, keepdims=True)

    route_ref[0:1, rows] = bucket.astype(f32)
    route_ref[1:2, rows] = rank
    route_ref[2:8, rows] = jnp.zeros((6, tm), f32)


def _post(l, x_ctx, x_lat, oa_ctx, oa_lat, yr_ctx, yr_lat, ga, gb, wpa, wpb, wo, mod, norm2_w, wr_pad,
          br_col, tri):
    tm = TM_POST
    row = lambda w: pl.BlockSpec((tm, w), lambda i: (i, 0))
    const = lambda a: pl.BlockSpec(a.shape, lambda i: (0,) * a.ndim)
    hbm = pl.BlockSpec(memory_space=pl.ANY)
    return pl.pallas_call(
        functools.partial(_post_kernel, l),
        grid=(T // tm,),
        in_specs=(_two_stream_specs(tm, D) + _two_stream_specs(tm, Q_A) + _two_stream_specs(tm, V_R)
                  + [row(D), row(D), hbm, hbm, hbm,
                     pl.BlockSpec((None, None, 6, D), lambda i: (l, _cond_of_tile(i, tm), 0, 0)),
                     _layer_row(l, D), const(wr_pad), const(br_col), const(tri)]),
        out_specs=[row(D), row(MOE_W), pl.BlockSpec((8, tm), lambda i: (0, i)),
                   pl.BlockSpec((BUCKET_ROWS, LANES), lambda i: (0, 0))],
        out_shape=[jax.ShapeDtypeStruct((T, D), f32), jax.ShapeDtypeStruct((T, MOE_W), f32),
                   jax.ShapeDtypeStruct((8, T), f32), jax.ShapeDtypeStruct((BUCKET_ROWS, LANES), f32)],
        scratch_shapes=[pltpu.VMEM((BUCKET_ROWS, LANES), f32), pltpu.VMEM((Q_A, D), bf16),
                        pltpu.VMEM((V_R, D), bf16), pltpu.VMEM((D, D), bf16),
                        pltpu.VMEM((W_STAGES, D, W_CHUNK), f32), pltpu.SemaphoreType.DMA((W_STAGES,))],
        compiler_params=pltpu.CompilerParams(
            dimension_semantics=("arbitrary",), vmem_limit_bytes=V7X_VMEM_LIMIT),
        name="post_router",
    )(x_ctx, x_lat, oa_ctx, oa_lat, yr_ctx, yr_lat, ga, gb, wpa, wpb, wo, mod, norm2_w, wr_pad, br_col, tri)


def _row_copy(src_ref, src_row, dst_ref, dst_row, sem):
    return pltpu.make_async_copy(src_ref.at[pl.ds(src_row, 1)], dst_ref.at[pl.ds(dst_row, 1)], sem)


def _dispatch_kernel(zf_ref, pos_ref, h_ref, xs_ref, zero_ref, sem):
    tm = h_ref.shape[0]

    @pl.when(pl.program_id(0) == 0)
    def _():
        zero_ref[...] = jnp.zeros_like(zero_ref)

        def tile_copy(t):
            return pltpu.make_async_copy(zero_ref, xs_ref.at[pl.ds(t * MOE_TR, MOE_TR)], sem)

        def start(t, carry):
            @pl.when(zf_ref[t] != 0)
            def _():
                tile_copy(t).start()
            return carry

        def wait(t, carry):
            @pl.when(zf_ref[t] != 0)
            def _():
                tile_copy(t).wait()
            return carry

        lax.fori_loop(0, MOE_TILES, start, 0)
        lax.fori_loop(0, MOE_TILES, wait, 0)

    for r in range(tm):
        _row_copy(h_ref, r, xs_ref, pos_ref[0, 0, r], sem).start(priority=r % 2)
    pltpu.make_async_copy(h_ref, xs_ref.at[pl.ds(0, tm)], sem).wait()


def _dispatch(zero_flags, pos_blocks, h2):
    tm = TM_DISP
    return pl.pallas_call(
        _dispatch_kernel,
        grid_spec=pltpu.PrefetchScalarGridSpec(
            num_scalar_prefetch=1,
            grid=(T // tm,),
            in_specs=[pl.BlockSpec((1, 1, tm), lambda i, zf: (i, 0, 0), memory_space=pltpu.SMEM),
                      pl.BlockSpec((tm, MOE_W), lambda i, zf: (i, 0))],
            out_specs=pl.BlockSpec(memory_space=pl.ANY),
            scratch_shapes=[pltpu.VMEM((MOE_TR, MOE_W), f32), pltpu.SemaphoreType.DMA(())],
        ),
        out_shape=jax.ShapeDtypeStruct((MOE_ROWS, MOE_W), f32),
        compiler_params=pltpu.CompilerParams(
            dimension_semantics=("arbitrary",), vmem_limit_bytes=V7X_VMEM_LIMIT),
        name="moe_dispatch",
    )(zero_flags, pos_blocks, h2)


def _experts_kernel(l, tg_ref, ta_ref, tb_ref, first_ref, nxt_ref, nt_ref, xs_ref, wg_hbm, wu_hbm, wd_hbm,
                    y_ref, wg_ref, wu_ref, wd_ref, sg_ref, su_ref, sd_ref, sems):
    i = pl.program_id(0)
    live = i < nt_ref[0]
    g = tg_ref[i]

    def fetch(grp):
        return [pltpu.make_async_copy(w.at[l, pl.ds(grp * EXP_PER_GROUP, EXP_PER_GROUP)], s, sems.at[k])
                for k, (w, s) in enumerate(((wg_hbm, sg_ref), (wu_hbm, su_ref), (wd_hbm, sd_ref)))]

    @pl.when(i == 0)
    def _():
        for c in fetch(tg_ref[0]):
            c.start()

    @pl.when(live & (first_ref[i] != 0))
    def _():
        for c in fetch(g):
            c.wait()
        for j in range(EXP_PER_GROUP):
            wg_ref[j] = sg_ref[j].astype(bf16)
            wu_ref[j] = su_ref[j].astype(bf16)
            wd_ref[j] = sd_ref[j].astype(bf16)

        @pl.when(nxt_ref[i] >= 0)
        def _():
            for c in fetch(nxt_ref[i]):
                c.start(priority=1)

    @pl.when(live)
    def _():
        x = xs_ref[:, 0:D].astype(bf16)
        comb = xs_ref[:, D:D + LANES]
        lane = lax.broadcasted_iota(jnp.int32, comb.shape, 1)
        acc = None
        for j in (ta_ref[i], tb_ref[i]):
            w_j = jnp.sum(jnp.where(lane == g * EXP_PER_GROUP + j, comb, 0.0), axis=1, keepdims=True)
            act = _silu(_dot(x, wg_ref[j])) * _dot(x, wu_ref[j]) * w_j
            part = _dot(act.astype(bf16), wd_ref[j])
            acc = part if acc is None else acc + part
        y_ref[...] = acc

    @pl.when(i >= nt_ref[0])
    def _():
        y_ref[...] = jnp.zeros_like(y_ref)


def _experts(l, plan, xs, wg, wu, wd):
    def tile(i, *prefetch):
        return jnp.minimum(i, prefetch[-1][0] - 1)

    hbm = pl.BlockSpec(memory_space=pl.ANY)
    return pl.pallas_call(
        functools.partial(_experts_kernel, l),
        grid_spec=pltpu.PrefetchScalarGridSpec(
            num_scalar_prefetch=6,
            grid=(MOE_TILES,),
            in_specs=[pl.BlockSpec((MOE_TR, MOE_W), lambda i, *p: (tile(i, *p), 0)), hbm, hbm, hbm],
            out_specs=pl.BlockSpec((MOE_TR, D), lambda i, *p: (i, 0)),
            scratch_shapes=[pltpu.VMEM((EXP_PER_GROUP, D, D_EXP), bf16),
                            pltpu.VMEM((EXP_PER_GROUP, D, D_EXP), bf16),
                            pltpu.VMEM((EXP_PER_GROUP, D_EXP, D), bf16),
                            pltpu.VMEM((EXP_PER_GROUP, D, D_EXP), f32),
                            pltpu.VMEM((EXP_PER_GROUP, D, D_EXP), f32),
                            pltpu.VMEM((EXP_PER_GROUP, D_EXP, D), f32), pltpu.SemaphoreType.DMA((3,))],
        ),
        out_shape=jax.ShapeDtypeStruct((MOE_ROWS, D), f32),
        compiler_params=pltpu.CompilerParams(
            dimension_semantics=("arbitrary",), vmem_limit_bytes=V7X_VMEM_LIMIT),
        name="moe_experts",
    )(*plan, xs, wg, wu, wd)


def _combine_kernel(pos_ref, posn_ref, x1_ref, mod_ref, y_ref, oc_ref, ol_ref, ya_ref, yb_ref, sems):
    tm = x1_ref.shape[0]
    i = pl.program_id(0)
    n_ctx = T_CTX // tm

    def pull(p_ref, buf, sem):
        for r in range(tm):
            _row_copy(y_ref, p_ref[0, 0, r], buf, r, sem).start(priority=r % 2)

    @pl.when(i == 0)
    def _():
        pull(pos_ref, ya_ref, sems.at[0])

    def step(cur, nxt, sem_cur, sem_nxt):
        @pl.when(i + 1 < pl.num_programs(0))
        def _():
            pull(posn_ref, nxt, sem_nxt)

        pltpu.make_async_copy(y_ref.at[pl.ds(0, tm)], cur, sem_cur).wait()
        out = x1_ref[...] + mod_ref[5:6, :] * cur[...]

        @pl.when(i < n_ctx)
        def _():
            oc_ref[...] = out

        @pl.when(i >= n_ctx)
        def _():
            ol_ref[...] = out

    @pl.when(i % 2 == 0)
    def _():
        step(ya_ref, yb_ref, sems.at[0], sems.at[1])

    @pl.when(i % 2 == 1)
    def _():
        step(yb_ref, ya_ref, sems.at[1], sems.at[0])


def _combine(l, pos_blocks, x1, mod, y):
    tm = TM_DISP
    n = T // tm
    return pl.pallas_call(
        _combine_kernel,
        grid=(n,),
        in_specs=[pl.BlockSpec((1, 1, tm), lambda i: (i, 0, 0), memory_space=pltpu.SMEM),
                  pl.BlockSpec((1, 1, tm), lambda i: (jnp.minimum(i + 1, n - 1), 0, 0),
                               memory_space=pltpu.SMEM),
                  pl.BlockSpec((tm, D), lambda i: (i, 0)),
                  pl.BlockSpec((None, None, 6, D), lambda i: (l, _cond_of_tile(i, tm), 0, 0)),
                  pl.BlockSpec(memory_space=pl.ANY)],
        out_specs=_two_stream_specs(tm, D),
        out_shape=[jax.ShapeDtypeStruct((T_CTX, D), f32), jax.ShapeDtypeStruct((T_LAT, D), f32)],
        scratch_shapes=[pltpu.VMEM((tm, D), f32), pltpu.VMEM((tm, D), f32),
                        pltpu.SemaphoreType.DMA((2,))],
        compiler_params=pltpu.CompilerParams(
            dimension_semantics=("arbitrary",), vmem_limit_bytes=V7X_VMEM_LIMIT),
        name="moe_combine",
    )(pos_blocks, pos_blocks, x1, mod, y)


def _plan_kernel(cnt_ref, off_ref, tg_ref, ta_ref, tb_ref, first_ref, nxt_ref, nt_ref, zf_ref, nxtg_ref):
    def bucket(b, start):
        n = lax.shift_right_logical(cnt_ref[b] + (MOE_TR - 1), MOE_TR.bit_length() - 1)
        off_ref[b] = start * MOE_TR
        g = b // PAIRS
        p = b - g * PAIRS
        lo = jnp.where(p >= 3, 1, 0) + jnp.where(p >= 5, 1, 0)
        hi = p + 1 - jnp.where(lo >= 1, 2, 0) - jnp.where(lo >= 2, 1, 0)

        def tile(t, c):
            tg_ref[t] = g
            ta_ref[t] = lo
            tb_ref[t] = hi
            zf_ref[t] = jnp.where(t == start + n - 1, 1, 0)
            return c

        lax.fori_loop(start, start + n, tile, 0)
        return start + n

    nt = lax.fori_loop(0, N_BUCKETS, bucket, jnp.int32(0))
    nt_ref[0] = nt

    def unused(t, c):
        tg_ref[t] = N_GROUPS - 1
        ta_ref[t] = 0
        tb_ref[t] = 1
        zf_ref[t] = 1
        first_ref[t] = 0
        nxt_ref[t] = -1
        return c

    lax.fori_loop(nt, MOE_TILES, unused, 0)

    def back(i, nxt_group):
        t = nt - 1 - i
        g = tg_ref[t]
        later = jnp.where(t + 1 < nt, tg_ref[jnp.minimum(t + 1, MOE_TILES - 1)], -1)
        is_last_of_group = later != g
        nxt_group = jnp.where(is_last_of_group, later, nxt_group)
        nxtg_ref[g] = nxt_group
        return nxt_group

    lax.fori_loop(0, nt, back, jnp.int32(-1))

    def fwd(t, c):
        g = tg_ref[t]
        first_ref[t] = jnp.where((t == 0) | (tg_ref[jnp.maximum(t - 1, 0)] != g), 1, 0)
        nxt_ref[t] = nxtg_ref[g]
        return c

    lax.fori_loop(0, nt, fwd, 0)


def _moe_plan(route, counts):
    smem = pl.BlockSpec(memory_space=pltpu.SMEM)
    tile_i32 = jax.ShapeDtypeStruct((MOE_TILES,), jnp.int32)
    off, tg, ta, tb, first, nxt, n_tiles, zero_flags = pl.pallas_call(
        _plan_kernel,
        in_specs=[smem],
        out_specs=[smem] * 8,
        out_shape=[jax.ShapeDtypeStruct((N_BUCKETS,), jnp.int32), tile_i32, tile_i32, tile_i32, tile_i32,
                   tile_i32, jax.ShapeDtypeStruct((1,), jnp.int32), tile_i32],
        scratch_shapes=[pltpu.SMEM((N_GROUPS,), jnp.int32)],
        name="moe_plan",
    )(counts[0:N_BUCKETS, 0].astype(jnp.int32))
    buckets = jnp.arange(N_BUCKETS, dtype=f32)[:, None]
    pos = jnp.sum(jnp.where(route[0][None, :] == buckets, off.astype(f32)[:, None], 0.0), axis=0) + route[1]
    pos_blocks = pos.astype(jnp.int32).reshape(T // TM_DISP, 1, TM_DISP)
    return pos_blocks, zero_flags, (tg, ta, tb, first, nxt, n_tiles)


def _rope_tables():
    pos = np.arange(DEC_SEQ)
    half = HD_A // 4
    freqs = ROPE_BASE ** (-np.arange(half, dtype=np.float64) / half)
    ang_r = (pos // GRID_W)[:, None] * freqs[None, :]
    ang_c = (pos % GRID_W)[:, None] * freqs[None, :]
    ang = np.concatenate([ang_r, ang_r, ang_c, ang_c], axis=1)
    sign = np.concatenate([-np.ones(half), np.ones(half)] * 2)[None, :]
    cos = np.tile(np.cos(ang), (1, 2))
    sin = np.tile(np.sin(ang) * sign, (1, 2))
    ident_c = np.ones((TM_PROJ, LANES))
    ident_s = np.zeros((TM_PROJ, LANES))
    return (jnp.asarray(np.concatenate([cos, ident_c]), f32),
            jnp.asarray(np.concatenate([sin, ident_s]), f32))


def kernel(x_prompt, x_sample, c, cache_k, cache_v, state_ret_fwd, state_ret_bwd, c_ctx,
           norm1_w, norm2_w, w_ada, b_ada, w_in, q_norm_w, k_norm_w, attn_sink,
           ret_decay_fwd, ret_decay_bwd, ret_gn_w, w_pa, w_pb, w_o, w_router, b_router,
           w_exp_gate, w_exp_up, w_exp_down):
    x_ctx, x_lat = x_prompt.reshape(T_CTX, D), x_sample.reshape(T_LAT, D)
    cond8 = jnp.zeros((N_COND, D), f32).at[0].set(c_ctx).at[1:1 + DEC_BATCH].set(c)
    mod = _adaln(cond8, w_ada, b_ada).reshape(DEPTH, N_COND, 6, D)

    cos_t, sin_t = _rope_tables()
    blk = np.arange(Q_A) // HD_A
    ones_blk = jnp.asarray((blk[:, None] == blk[None, :]) / HD_A, bf16)
    tri = jnp.asarray(np.triu(np.ones((POST_SUB, POST_SUB)), 1), bf16)
    per_layer = lambda w: w.reshape(DEPTH, 1, -1)
    norm1_w, norm2_w, gnw = per_layer(norm1_w), per_layer(norm2_w), per_layer(ret_gn_w)
    qnw = per_layer(jnp.tile(q_norm_w, (1, NH_A)) * (HD_A ** -0.5))
    knw = per_layer(jnp.tile(k_norm_w, (1, NKV_A)))
    wr_pad = jnp.concatenate(_split(jnp.pad(w_router, ((0, 0), (0, LANES - N_EXP)))), axis=1)
    br_col = jnp.broadcast_to(b_router[:, None], (N_EXP, LANES))
    ck = cache_k.reshape(DEC_BATCH, DEPTH, PAST, KV_A)
    cv = cache_v.reshape(DEC_BATCH, DEPTH, PAST, KV_A)
    s0 = jnp.concatenate([state_ret_fwd, state_ret_bwd], axis=3)
    lg = jnp.log1p(-jnp.exp(jnp.concatenate([ret_decay_fwd, ret_decay_bwd], axis=1).astype(f32)))

    new_k, new_v, new_sf, new_sb = [], [], [], []
    for l in range(DEPTH):
        qa, ka, va, qr, kr, vr, gr, ga, gb = _inproj(l, x_ctx, x_lat, norm1_w, mod, w_in, cos_t, sin_t,
                                                     ones_blk, qnw, knw)
        oa_ctx = _ctx_attn(l, qa, ka, va, attn_sink)
        oa_lat = _lat_attn(l, qa, ka, va, ck, cv, cos_t, sin_t, attn_sink)
        yr_ctx, sf, sb = _retention(l, qr, kr, vr, gr, None, lg, gnw, BATCH, SEQ, 0, CTX_SEQS_PER_STEP)
        yr_lat, _, _ = _retention(l, qr, kr, vr, gr, s0, lg, gnw, DEC_BATCH, DEC_SEQ,
                                  T_CTX // DEC_SEQ, 1)
        x1, h2, route, counts = _post(l, x_ctx, x_lat, oa_ctx, oa_lat, yr_ctx, yr_lat, ga, gb,
                                      w_pa, w_pb, w_o, mod, norm2_w, wr_pad, br_col, tri)
        pos_blocks, zero_flags, plan = _moe_plan(route, counts)
        xs = _dispatch(zero_flags, pos_blocks, h2)
        y = _experts(l, plan, xs, w_exp_gate, w_exp_up, w_exp_down)
        x_ctx, x_lat = _combine(l, pos_blocks, x1, mod, y)

        new_k.append(ka[:T_CTX].reshape(BATCH, SEQ, NKV_A, HD_A))
        new_v.append(va[:T_CTX].reshape(BATCH, SEQ, NKV_A, HD_A))
        new_sf.append(sf)
        new_sb.append(sb)

    return (x_ctx.reshape(BATCH, SEQ, D), x_lat.reshape(DEC_BATCH, DEC_SEQ, D),
            jnp.stack(new_k, axis=1), jnp.stack(new_v, axis=1),
            jnp.stack(new_sf, axis=1), jnp.stack(new_sb, axis=1))
```

```python
import functools

import numpy as np
import jax
import jax.numpy as jnp
from jax import lax
from jax.experimental import pallas as pl
from jax.experimental.pallas import tpu as pltpu

D = 1024
BATCH, SEQ = 16, 256
DEC_BATCH, DEC_SEQ = 2, 2048
DEPTH = 2
PAST = 512
GRID_W = 64
NH_A, NKV_A, HD_A = 8, 2, 64
WINDOW = 128
NH_R, DK_R, DV_R = 4, 64, 128
CHUNK = 128
N_EXP, N_GROUPS, EXP_PER_GROUP = 16, 4, 4
D_EXP = 512
ROPE_BASE = 10000.0
EPS = 1e-6
NEG_INF = -1e30

Q_A = NH_A * HD_A
KV_A = NKV_A * HD_A
QK_R = NH_R * DK_R
V_R = NH_R * DV_R
C_QA = (0, Q_A)
C_KA = (C_QA[1], C_QA[1] + KV_A)
C_VA = (C_KA[1], C_KA[1] + KV_A)
C_QR = (C_VA[1], C_VA[1] + QK_R)
C_KR = (C_QR[1], C_QR[1] + QK_R)
C_VR = (C_KR[1], C_KR[1] + V_R)
C_GR = (C_VR[1], C_VR[1] + V_R)
C_GA = (C_GR[1], C_GR[1] + D)
C_GB = (C_GA[1], C_GA[1] + D)
D_IN = C_GB[1]

T_CTX = BATCH * SEQ
T_LAT = DEC_BATCH * DEC_SEQ
T = T_CTX + T_LAT
N_COND = 8

LANES = 128
MXU_TILE = 256
V7X_VMEM_LIMIT = 56 * 1024 * 1024

TM_PROJ = 512
TM_POST = 512
POST_SUB = 512
W_CHUNK = 256
W_STAGES = 4
TM_DISP = 512
MOE_TR = 256
PAIRS = EXP_PER_GROUP * (EXP_PER_GROUP - 1) // 2
N_BUCKETS = N_GROUPS * PAIRS
BUCKET_ROWS = 32
MOE_TILES = T // MOE_TR + N_BUCKETS
MOE_ROWS = MOE_TILES * MOE_TR
MOE_W = D + LANES
ATT_QB = 256
RET_UNROLL = 4
CTX_SEQS_PER_STEP = 4

f32 = jnp.float32
bf16 = jnp.bfloat16


def _dot(a, b):
    return jnp.dot(a, b, preferred_element_type=f32)


def _dot_t(a, b):
    return lax.dot_general(a, b, (((1,), (1,)), ((), ())), preferred_element_type=f32)


def _split(x):
    hi = x.astype(bf16)
    lo = (x - hi.astype(f32)).astype(bf16)
    return hi, lo


def _dot3(a, b):
    m = a.shape[0]
    ah, al = _split(a)
    bh, bl = _split(b)
    both = _dot(jnp.concatenate([ah, al], axis=0), bh)
    return both[:m] + (_dot(ah, bl) + both[m:])


def _sigmoid(x):
    return 1.0 / (1.0 + jnp.exp(-x))


def _silu(x):
    return x * _sigmoid(x)


def _layer_row(l, n):
    return pl.BlockSpec((None, 1, n), lambda *_: (l, 0, 0))


def _cond_of_tile(i, tm):
    n_ctx = T_CTX // tm
    per_b = DEC_SEQ // tm
    return jnp.where(i < n_ctx, 0, 1 + jnp.maximum(i - n_ctx, 0) // per_b)


def _adaln_kernel(cond_ref, w_ref, b_ref, o_ref):
    a = _silu(cond_ref[...])
    o_ref[0] = _dot3(a, w_ref[0]) + b_ref[0]


def _adaln(cond8, w_ada, b_ada):
    tn = 2048
    return pl.pallas_call(
        _adaln_kernel,
        grid=(DEPTH, 6 * D // tn),
        in_specs=[
            pl.BlockSpec((N_COND, D), lambda l, j: (0, 0)),
            pl.BlockSpec((1, D, tn), lambda l, j: (l, 0, j)),
            pl.BlockSpec((1, 1, tn), lambda l, j: (l, 0, j)),
        ],
        out_specs=pl.BlockSpec((1, N_COND, tn), lambda l, j: (l, 0, j)),
        out_shape=jax.ShapeDtypeStruct((DEPTH, N_COND, 6 * D), f32),
        compiler_params=pltpu.CompilerParams(
            dimension_semantics=("parallel", "parallel"), vmem_limit_bytes=V7X_VMEM_LIMIT),
        name="adaln",
    )(cond8, w_ada, b_ada.reshape(DEPTH, 1, 6 * D))


def _rope(x, cos, sin_signed, first_half):
    fwd = pltpu.roll(x, 16, 1)
    bwd = pltpu.roll(x, LANES - 16, 1)
    partner = jnp.where(first_half, bwd, fwd)
    return x * cos + partner * sin_signed


def _head_rms(x, ones_blk, w):
    n = x.shape[1]
    sq_hi, sq_lo = _split(x * x)
    wt = min(n, MXU_TILE)
    blk = ones_blk[0:wt, 0:wt]
    parts = [_dot(sq_hi[:, c:c + wt], blk) + _dot(sq_lo[:, c:c + wt], blk) for c in range(0, n, wt)]
    mean = parts[0] if len(parts) == 1 else jnp.concatenate(parts, axis=1)
    return x * lax.rsqrt(mean + EPS) * w


def _load_cast(w_hbm, dst_ref, stage_ref, sems):
    k, n = w_hbm.shape
    depth = stage_ref.shape[0]

    def chunk_copy(c):
        return pltpu.make_async_copy(w_hbm.at[:, pl.ds(c * W_CHUNK, W_CHUNK)],
                                     stage_ref.at[c % depth, pl.ds(0, k)], sems.at[c % depth])

    n_chunks = n // W_CHUNK
    for c in range(min(depth - 1, n_chunks)):
        chunk_copy(c).start()
    for c in range(n_chunks):
        if c + depth - 1 < n_chunks:
            chunk_copy(c + depth - 1).start()
        chunk_copy(c).wait()
        dst_ref[:, c * W_CHUNK:(c + 1) * W_CHUNK] = stage_ref[c % depth, 0:k, :].astype(bf16)


def _tile_of_two(i, n_first, a_ref, b_ref):
    return jnp.where(i < n_first, a_ref[...], b_ref[...])


def _inproj_kernel(l, xc_ref, xl_ref, n1_ref, mod_ref, w_hbm, cos_ref, sin_ref, ones_ref, qnw_ref,
                   knw_ref, qa_ref, ka_ref, va_ref, qr_ref, kr_ref, vr_ref, gr_ref, ga_ref, gb_ref,
                   w_ref, stage_ref, sems):
    i = pl.program_id(0)

    @pl.when(i == 0)
    def _():
        _load_cast(w_hbm.at[l], w_ref, stage_ref, sems)

    x = _tile_of_two(i, T_CTX // xc_ref.shape[0], xc_ref, xl_ref)
    y = x * lax.rsqrt(jnp.mean(x * x, axis=-1, keepdims=True) + EPS) * n1_ref[...]
    h = (y * (1.0 + mod_ref[1:2, :]) + mod_ref[0:1, :]).astype(bf16)

    whole = _dot(h, w_ref[...])

    def proj(c):
        return whole[:, c[0]:c[1]]

    cos = cos_ref[...]
    sin = sin_ref[...]
    lane = lax.broadcasted_iota(jnp.int32, cos.shape, 1)
    first_half = (lane % 32) < 16

    def rope_all(v):
        parts = [_rope(v[:, j:j + LANES], cos, sin, first_half) for j in range(0, v.shape[1], LANES)]
        return parts[0] if len(parts) == 1 else jnp.concatenate(parts, axis=1)

    ones_blk = ones_ref[...]
    qa = _head_rms(proj(C_QA), ones_blk, qnw_ref[...])
    qa_ref[...] = rope_all(qa).astype(bf16)
    ka_ref[...] = _head_rms(proj(C_KA), ones_blk, knw_ref[...])
    va_ref[...] = proj(C_VA)
    qr_ref[...] = rope_all(proj(C_QR)).astype(bf16)
    kr_ref[...] = (rope_all(proj(C_KR)) * (DK_R ** -0.5)).astype(bf16)
    vr_ref[...] = proj(C_VR).astype(bf16)
    gr_ref[...] = proj(C_GR).astype(bf16)
    ga_ref[...] = proj(C_GA).astype(bf16)
    gb_ref[...] = proj(C_GB).astype(bf16)


def _two_stream_specs(tm, width):
    n_ctx = T_CTX // tm
    return [pl.BlockSpec((tm, width), lambda i, *_: (jnp.minimum(i, n_ctx - 1), 0)),
            pl.BlockSpec((tm, width), lambda i, *_: (jnp.maximum(i - n_ctx, 0), 0))]


def _inproj(l, x_ctx, x_lat, norm1_w, mod, w_in, cos_t, sin_t, ones_blk, qnw, knw):
    tm = TM_PROJ
    n_ctx = T_CTX // tm
    per_b = DEC_SEQ // tm

    def tab_map(i):
        return (jnp.where(i < n_ctx, per_b, jnp.maximum(i - n_ctx, 0) % per_b), 0)

    row = lambda i: (i, 0)
    const = lambda i: (0, 0)
    widths = [(Q_A, bf16), (KV_A, f32), (KV_A, f32), (QK_R, bf16), (QK_R, bf16), (V_R, bf16),
              (V_R, bf16), (D, bf16), (D, bf16)]
    return pl.pallas_call(
        functools.partial(_inproj_kernel, l),
        grid=(T // tm,),
        in_specs=_two_stream_specs(tm, D) + [
            _layer_row(l, D),
            pl.BlockSpec((None, None, 6, D), lambda i: (l, _cond_of_tile(i, tm), 0, 0)),
            pl.BlockSpec(memory_space=pl.ANY),
            pl.BlockSpec((tm, LANES), tab_map),
            pl.BlockSpec((tm, LANES), tab_map),
            pl.BlockSpec((Q_A, Q_A), const),
            _layer_row(l, Q_A),
            _layer_row(l, KV_A),
        ],
        out_specs=[pl.BlockSpec((tm, w), row) for w, _ in widths],
        out_shape=[jax.ShapeDtypeStruct((T, w), dt) for w, dt in widths],
        scratch_shapes=[pltpu.VMEM((D, D_IN), bf16), pltpu.VMEM((W_STAGES, D, W_CHUNK), f32),
                        pltpu.SemaphoreType.DMA((W_STAGES,))],
        compiler_params=pltpu.CompilerParams(
            dimension_semantics=("arbitrary",), vmem_limit_bytes=V7X_VMEM_LIMIT),
        name="inproj",
    )(x_ctx, x_lat, norm1_w, mod, w_in, cos_t, sin_t, ones_blk, qnw, knw)


def _head_blocks(t, kv, lo_mask):
    r = pltpu.roll(t, HD_A, 1)
    if kv == 0:
        a = jnp.where(lo_mask, t, 0.0)
        b = jnp.where(lo_mask, 0.0, r)
    else:
        a = jnp.where(lo_mask, r, 0.0)
        b = jnp.where(lo_mask, 0.0, t)
    return jnp.concatenate([a, b], axis=0).astype(bf16)


def _ctx_attn_kernel(l, sink_ref, q_ref, k_ref, v_ref, o_ref):
    n = SEQ
    lo_mask = lax.broadcasted_iota(jnp.int32, (n, LANES), 1) < HD_A
    for b in range(CTX_SEQS_PER_STEP):
        rows = slice(b * n, (b + 1) * n)
        k = k_ref[rows, :]
        v = v_ref[rows, :]
        for kv in range(NKV_A):
            kblk = _head_blocks(k, kv, lo_mask)
            vblk = _head_blocks(v, kv, lo_mask)
            q2 = jnp.concatenate([q_ref[rows, (2 * kv) * LANES:(2 * kv + 1) * LANES],
                                  q_ref[rows, (2 * kv + 1) * LANES:(2 * kv + 2) * LANES]], axis=0)
            s = _dot_t(q2, kblk)
            row = lax.broadcasted_iota(jnp.int32, (2 * n, 1), 0)
            ps, invs = [], []
            for hh in range(2):
                sk = jnp.where(row < n, sink_ref[l, 4 * kv + hh], sink_ref[l, 4 * kv + 2 + hh])
                sh = s[:, hh * n:(hh + 1) * n]
                m = jnp.maximum(jnp.max(sh, axis=-1, keepdims=True), sk)
                p = jnp.exp(sh - m)
                invs.append(1.0 / (jnp.sum(p, axis=-1, keepdims=True) + jnp.exp(sk - m)))
                ps.append(p.astype(bf16))
            lo2 = lax.broadcasted_iota(jnp.int32, (2 * n, LANES), 1) < HD_A
            o = _dot(jnp.concatenate(ps, axis=1), vblk) * jnp.where(lo2, invs[0], invs[1])
            o_ref[rows, (2 * kv) * LANES:(2 * kv + 1) * LANES] = o[:n].astype(bf16)
            o_ref[rows, (2 * kv + 1) * LANES:(2 * kv + 2) * LANES] = o[n:].astype(bf16)


def _ctx_attn(l, qa, ka, va, sink):
    blk = lambda w: pl.BlockSpec((CTX_SEQS_PER_STEP * SEQ, w), lambda b: (b, 0))
    return pl.pallas_call(
        functools.partial(_ctx_attn_kernel, l),
        grid=(BATCH // CTX_SEQS_PER_STEP,),
        in_specs=[pl.BlockSpec(memory_space=pltpu.SMEM), blk(Q_A), blk(KV_A), blk(KV_A)],
        out_specs=blk(Q_A),
        out_shape=jax.ShapeDtypeStruct((T_CTX, Q_A), bf16),
        compiler_params=pltpu.CompilerParams(
            dimension_semantics=("parallel",), vmem_limit_bytes=V7X_VMEM_LIMIT),
        name="ctx_attn",
    )(sink, qa, ka, va)


def _lat_attn_kernel(l, sink_ref, q_ref, k_ref, v_ref, kc_ref, vc_ref, cos_ref, sin_ref, o_ref):
    j = pl.program_id(1)
    qb = ATT_QB
    win = 2 * qb
    ws = pl.multiple_of(jnp.clip(j * qb - WINDOW, 0, DEC_SEQ - win), WINDOW)
    lo_mask = lax.broadcasted_iota(jnp.int32, (win, LANES), 1) < HD_A
    lane = lax.broadcasted_iota(jnp.int32, (win, LANES), 1)
    kw = _rope(k_ref[pl.ds(ws, win), :], cos_ref[pl.ds(ws, win), :], sin_ref[pl.ds(ws, win), :],
               (lane % 32) < 16)
    vw = v_ref[pl.ds(ws, win), :]
    kc = kc_ref[...]
    vc = vc_ref[...]
    qpos = j * qb + (lax.broadcasted_iota(jnp.int32, (2 * qb, win), 0) & (qb - 1))
    kpos = ws + lax.broadcasted_iota(jnp.int32, (2 * qb, win), 1)
    valid = jnp.abs(qpos - kpos) <= WINDOW
    out_lo = lax.broadcasted_iota(jnp.int32, (2 * qb, LANES), 1) < HD_A
    for kv in range(NKV_A):
        kc_blk = _head_blocks(kc, kv, lo_mask[:PAST])
        vc_blk = _head_blocks(vc, kv, lo_mask[:PAST])
        kw_blk = _head_blocks(kw, kv, lo_mask)
        vw_blk = _head_blocks(vw, kv, lo_mask)
        q2 = jnp.concatenate([q_ref[:, (2 * kv) * LANES:(2 * kv + 1) * LANES],
                              q_ref[:, (2 * kv + 1) * LANES:(2 * kv + 2) * LANES]], axis=0)
        s_c = _dot_t(q2, kc_blk)
        s_w = _dot_t(q2, kw_blk)
        pcs, pws, invs = [], [], []
        for hh in range(2):
            row = lax.broadcasted_iota(jnp.int32, (2 * qb, 1), 0)
            sk = jnp.where(row < qb, sink_ref[l, 4 * kv + hh], sink_ref[l, 4 * kv + 2 + hh])
            sc = s_c[:, hh * PAST:(hh + 1) * PAST]
            sw = jnp.where(valid, s_w[:, hh * win:(hh + 1) * win], NEG_INF)
            m = jnp.maximum(jnp.maximum(jnp.max(sc, axis=-1, keepdims=True),
                                        jnp.max(sw, axis=-1, keepdims=True)), sk)
            pc = jnp.exp(sc - m)
            pw = jnp.exp(sw - m)
            den = (jnp.sum(pc, axis=-1, keepdims=True) + jnp.sum(pw, axis=-1, keepdims=True)
                   + jnp.exp(sk - m))
            invs.append(1.0 / den)
            pcs.append(pc.astype(bf16))
            pws.append(pw.astype(bf16))
        o = _dot(jnp.concatenate(pcs, axis=1), vc_blk) + _dot(jnp.concatenate(pws, axis=1), vw_blk)
        o = o * jnp.where(out_lo, invs[0], invs[1])
        o_ref[:, (2 * kv) * LANES:(2 * kv + 1) * LANES] = o[:qb].astype(bf16)
        o_ref[:, (2 * kv + 1) * LANES:(2 * kv + 2) * LANES] = o[qb:].astype(bf16)


def _lat_attn(l, qa, ka, va, cache_k, cache_v, cos_l, sin_l, sink):
    qb = ATT_QB
    nq = DEC_SEQ // qb
    ctx_blocks = T_CTX // DEC_SEQ
    seq = lambda b, j: (ctx_blocks + b, 0)
    return pl.pallas_call(
        functools.partial(_lat_attn_kernel, l),
        grid=(DEC_BATCH, nq),
        in_specs=[
            pl.BlockSpec(memory_space=pltpu.SMEM),
            pl.BlockSpec((qb, Q_A), lambda b, j: (T_CTX // qb + b * nq + j, 0)),
            pl.BlockSpec((DEC_SEQ, KV_A), seq),
            pl.BlockSpec((DEC_SEQ, KV_A), seq),
            pl.BlockSpec((None, None, PAST, KV_A), lambda b, j: (b, l, 0, 0)),
            pl.BlockSpec((None, None, PAST, KV_A), lambda b, j: (b, l, 0, 0)),
            pl.BlockSpec((DEC_SEQ, LANES), lambda b, j: (0, 0)),
            pl.BlockSpec((DEC_SEQ, LANES), lambda b, j: (0, 0)),
        ],
        out_specs=pl.BlockSpec((qb, Q_A), lambda b, j: (b * nq + j, 0)),
        out_shape=jax.ShapeDtypeStruct((T_LAT, Q_A), bf16),
        compiler_params=pltpu.CompilerParams(
            dimension_semantics=("parallel", "parallel"), vmem_limit_bytes=V7X_VMEM_LIMIT),
        name="lat_attn",
    )(sink, qa, ka, va, cache_k, cache_v, cos_l, sin_l)


def _dup_heads(pair, lo_mask):
    r = pltpu.roll(pair, DK_R, 1)
    return jnp.where(lo_mask, pair, r), jnp.where(lo_mask, r, pair)


def _retention_kernel(l, has_s0, n_aliased, n_seq, n_chunks, *refs):
    n_in = 7 if has_s0 else 6
    refs = refs[:n_in] + refs[n_in + n_aliased:]
    if has_s0:
        (lg_ref, q_ref, k_ref, v_ref, g_ref, s0_ref, gnw_ref,
         y_ref, sf_ref, sb_ref, ds_ref, st_ref, mask_ref, qdec_ref, kdec_ref, cdec_ref) = refs
    else:
        (lg_ref, q_ref, k_ref, v_ref, g_ref, gnw_ref,
         y_ref, sf_ref, sb_ref, ds_ref, st_ref, mask_ref, qdec_ref, kdec_ref, cdec_ref) = refs
        s0_ref = None
    stacked = len(sf_ref.shape) == 5
    slot = l if n_aliased == 0 else 0
    if stacked and n_aliased == 0:
        sf_ref[...] = jnp.zeros(sf_ref.shape, f32)
        sb_ref[...] = jnp.zeros(sb_ref.shape, f32)
    C = CHUNK
    lo_mask = lax.broadcasted_iota(jnp.int32, (C, LANES), 1) < DK_R

    ri = lax.broadcasted_iota(jnp.int32, (C, C), 0).astype(f32)
    diff = ri - lax.broadcasted_iota(jnp.int32, (C, C), 1).astype(f32)
    for h in range(NH_R):
        lg_f, lg_b = lg_ref[l, h], lg_ref[l, NH_R + h]
        mask_ref[h] = (jnp.where(diff >= 0, jnp.exp(jnp.maximum(diff, 0.0) * lg_f), 0.0)
                       + jnp.where(diff <= 0, jnp.exp(jnp.maximum(-diff, 0.0) * lg_b), 0.0))
        qdec_ref[:, h * LANES:(h + 1) * LANES] = jnp.exp(
            jnp.where(lo_mask, (ri + 1.0) * lg_f, (C - ri) * lg_b))
        kdec_ref[:, h * LANES:(h + 1) * LANES] = jnp.exp(
            jnp.where(lo_mask, (C - 1.0 - ri) * lg_f, ri * lg_b))
        cdec_ref[h] = jnp.exp(jnp.where(ri < DK_R, C * lg_f, C * lg_b))

    def inc_body(c, carry):
        r0 = pl.multiple_of(c * C, C)
        for pr in range(2):
            kp = k_ref[pl.ds(r0, C), pr * LANES:(pr + 1) * LANES].astype(f32)
            for hh, kd in enumerate(_dup_heads(kp, lo_mask)):
                h = 2 * pr + hh
                kd = (kd * kdec_ref[:, h * LANES:(h + 1) * LANES]).astype(bf16)
                vh = v_ref[pl.ds(r0, C), h * DV_R:(h + 1) * DV_R]
                ds_ref[c, h] = lax.dot_general(kd, vh, (((0,), (0,)), ((), ())),
                                               preferred_element_type=f32)
        return carry

    lax.fori_loop(0, n_seq * n_chunks, inc_body, 0, unroll=RET_UNROLL)

    for sq in range(n_seq):
        base = sq * n_chunks
        for h in range(NH_R):
            cf = cdec_ref[h, 0:DK_R, :]
            cb = cdec_ref[h, DK_R:2 * DK_R, :]
            if has_s0:
                init_f = s0_ref[h, 0:DK_R, :]
                init_b = s0_ref[h, DK_R:2 * DK_R, :]
            else:
                init_f = jnp.zeros((DK_R, DV_R), f32)
                init_b = init_f

            def fwd_body(i, s, h=h, cf=cf, base=base):
                c = base + i
                st_ref[c, h, 0:DK_R, :] = s
                return s * cf + ds_ref[c, h, 0:DK_R, :]

            def bwd_body(i, s, h=h, cb=cb, base=base):
                c = base + n_chunks - 1 - i
                st_ref[c, h, DK_R:2 * DK_R, :] = s
                return s * cb + ds_ref[c, h, DK_R:2 * DK_R, :]

            s_f = lax.fori_loop(0, n_chunks, fwd_body, init_f)
            s_b = lax.fori_loop(0, n_chunks, bwd_body, init_b)
            if stacked:
                sf_ref[sq, slot, h] = s_f
                sb_ref[sq, slot, h] = s_b
            else:
                sf_ref[sq, h] = s_f
                sb_ref[sq, h] = s_b

    def out_body(c, carry):
        r0 = pl.multiple_of(c * C, C)
        for pr in range(2):
            qp = q_ref[pl.ds(r0, C), pr * LANES:(pr + 1) * LANES]
            kp = k_ref[pl.ds(r0, C), pr * LANES:(pr + 1) * LANES].astype(f32)
            kblk = jnp.concatenate([jnp.where(lo_mask, kp, 0.0), jnp.where(lo_mask, 0.0, kp)],
                                   axis=0).astype(bf16)
            a2 = _dot_t(qp, kblk)
            for hh, qd in enumerate(_dup_heads(qp.astype(f32), lo_mask)):
                h = 2 * pr + hh
                a = (a2[:, hh * C:(hh + 1) * C] * mask_ref[h]).astype(bf16)
                vh = v_ref[pl.ds(r0, C), h * DV_R:(h + 1) * DV_R]
                qd = (qd * qdec_ref[:, h * LANES:(h + 1) * LANES]).astype(bf16)
                o = _dot(a, vh) + _dot(qd, st_ref[c, h].astype(bf16))
                mu = jnp.mean(o, axis=-1, keepdims=True)
                d = o - mu
                var = jnp.mean(d * d, axis=-1, keepdims=True)
                yh = d * lax.rsqrt(var + EPS) * gnw_ref[:, h * DV_R:(h + 1) * DV_R]
                g = g_ref[pl.ds(r0, C), h * DV_R:(h + 1) * DV_R].astype(f32)
                y_ref[pl.ds(r0, C), h * DV_R:(h + 1) * DV_R] = (yh * _silu(g)).astype(bf16)
        return carry

    lax.fori_loop(0, n_seq * n_chunks, out_body, 0, unroll=RET_UNROLL)


def _retention(l, qr, kr, vr, gr, s0, lg, gnw, nb, seq, row_block0, n_seq, stack_into=None):
    n_chunks = seq // CHUNK
    has_s0 = s0 is not None
    rows = n_seq * seq
    tok = lambda w: pl.BlockSpec((rows, w), lambda b: (row_block0 + b, 0))
    in_specs = [pl.BlockSpec(memory_space=pltpu.SMEM), tok(QK_R), tok(QK_R), tok(V_R), tok(V_R)]
    args = [lg, qr, kr, vr, gr]
    if has_s0:
        in_specs.append(pl.BlockSpec((None, None, NH_R, 2 * DK_R, DV_R), lambda b: (b, l, 0, 0, 0)))
        args.append(s0)
    in_specs += [_layer_row(l, V_R)]
    args += [gnw]
    if stack_into is None:
        stack_into = ()
        st_spec = pl.BlockSpec((n_seq, NH_R, DK_R, DV_R), lambda b: (b, 0, 0, 0))
        st_shape = jax.ShapeDtypeStruct((nb, NH_R, DK_R, DV_R), f32)
    elif len(stack_into) == 0:
        st_spec = pl.BlockSpec((n_seq, DEPTH, NH_R, DK_R, DV_R), lambda b: (b, 0, 0, 0, 0))
        st_shape = jax.ShapeDtypeStruct((nb, DEPTH, NH_R, DK_R, DV_R), f32)
    else:
        st_spec = pl.BlockSpec((n_seq, 1, NH_R, DK_R, DV_R), lambda b: (b, l, 0, 0, 0))
        st_shape = jax.ShapeDtypeStruct((nb, DEPTH, NH_R, DK_R, DV_R), f32)
    aliases = {len(args) + j: 1 + j for j in range(len(stack_into))}
    in_specs += [pl.BlockSpec(memory_space=pl.ANY)] * len(stack_into)
    args += list(stack_into)
    return pl.pallas_call(
        functools.partial(_retention_kernel, l, has_s0, len(stack_into), n_seq, n_chunks),
        grid=(nb // n_seq,),
        in_specs=in_specs,
        out_specs=[pl.BlockSpec((rows, V_R), lambda b: (b, 0)), st_spec, st_spec],
        out_shape=[jax.ShapeDtypeStruct((nb * seq, V_R), bf16), st_shape, st_shape],
        input_output_aliases=aliases,
        scratch_shapes=[pltpu.VMEM((n_seq * n_chunks, NH_R, 2 * DK_R, DV_R), f32),
                        pltpu.VMEM((n_seq * n_chunks, NH_R, 2 * DK_R, DV_R), f32),
                        pltpu.VMEM((NH_R, CHUNK, CHUNK), f32), pltpu.VMEM((CHUNK, NH_R * LANES), f32),
                        pltpu.VMEM((CHUNK, NH_R * LANES), f32), pltpu.VMEM((NH_R, 2 * DK_R, DV_R), f32)],
        compiler_params=pltpu.CompilerParams(
            dimension_semantics=("parallel",), vmem_limit_bytes=V7X_VMEM_LIMIT),
        name="retention_lat" if has_s0 else "retention_ctx",
    )(*args)


def _post_kernel(l, xc_ref, xl_ref, oac_ref, oal_ref, yrc_ref, yrl_ref, ga_ref, gb_ref, wpa_hbm, wpb_hbm,
                 wo_hbm, mod_ref, n2_ref, wr_ref, br_ref, tri_ref, x1_ref, h2_ref, route_ref, cnt_ref,
                 carry_ref, wpa_ref, wpb_ref, wo_ref, stage_ref, sems):
    i = pl.program_id(0)
    tm = xc_ref.shape[0]
    n_ctx = T_CTX // tm

    @pl.when(i == 0)
    def _():
        carry_ref[...] = jnp.zeros_like(carry_ref)
        _load_cast(wpa_hbm.at[l], wpa_ref, stage_ref, sems)
        _load_cast(wpb_hbm.at[l], wpb_ref, stage_ref, sems)
        _load_cast(wo_hbm.at[l], wo_ref, stage_ref, sems)

    for r0 in range(0, tm, POST_SUB):
        _post_subtile(i < n_ctx, slice(r0, r0 + POST_SUB), xc_ref, xl_ref, oac_ref, oal_ref, yrc_ref,
                      yrl_ref, ga_ref, gb_ref, mod_ref, n2_ref, wr_ref, br_ref, tri_ref, x1_ref, h2_ref,
                      route_ref, carry_ref, wpa_ref, wpb_ref, wo_ref)
    cnt_ref[...] = carry_ref[...]


def _post_subtile(is_ctx, rows, xc_ref, xl_ref, oac_ref, oal_ref, yrc_ref, yrl_ref, ga_ref, gb_ref, mod_ref,
                  n2_ref, wr_ref, br_ref, tri_ref, x1_ref, h2_ref, route_ref, carry_ref, wpa_ref, wpb_ref,
                  wo_ref):
    tm = rows.stop - rows.start
    pick = lambda a_ref, b_ref: jnp.where(is_ctx, a_ref[rows, :], b_ref[rows, :])
    ga = _sigmoid(ga_ref[rows, :].astype(f32))
    gb = _sigmoid(gb_ref[rows, :].astype(f32))
    merged = (ga * _dot(pick(oac_ref, oal_ref), wpa_ref[...])
              + gb * _dot(pick(yrc_ref, yrl_ref), wpb_ref[...]))
    mix = _dot(merged.astype(bf16), wo_ref[...])
    x1 = pick(xc_ref, xl_ref) + mod_ref[2:3, :] * mix
    x1_ref[rows, :] = x1
    y = x1 * lax.rsqrt(jnp.mean(x1 * x1, axis=-1, keepdims=True) + EPS) * n2_ref[...]
    h2 = y * (1.0 + mod_ref[4:5, :]) + mod_ref[3:4, :]
    h2_ref[rows, 0:D] = h2

    h_hi, h_lo = _split(h2)
    both = _dot(h_hi, wr_ref[...])
    logits = both[:, 0:LANES] + (both[:, LANES:2 * LANES] + _dot(h_lo, wr_ref[:, 0:LANES]))
    lt = logits.T[0:N_EXP, :]
    scores = _sigmoid(lt)
    sel = scores + br_ref[:, 0:1]
    row = lax.broadcasted_iota(jnp.int32, (N_EXP, tm), 0)

    best = None
    bg = None
    for g in range(N_GROUPS):
        a, b, c, d = (sel[EXP_PER_GROUP * g + k:EXP_PER_GROUP * g + k + 1, :] for k in range(4))
        p, q = jnp.maximum(a, b), jnp.minimum(a, b)
        r, s = jnp.maximum(c, d), jnp.minimum(c, d)
        gs = jnp.maximum(p, r) + jnp.maximum(jnp.minimum(p, r), jnp.maximum(q, s))
        if g == 0:
            best, bg = gs, jnp.zeros((1, tm), jnp.int32)
        else:
            upd = gs > best
            bg = jnp.where(upd, g, bg)
            best = jnp.where(upd, gs, best)
    masked = jnp.where(jnp.right_shift(row, 2) == bg, sel, NEG_INF)
    m1 = jnp.max(masked, axis=0, keepdims=True)
    i1 = jnp.min(jnp.where(masked == m1, row, N_EXP), axis=0, keepdims=True)
    masked2 = jnp.where(row == i1, NEG_INF, masked)
    m2 = jnp.max(masked2, axis=0, keepdims=True)
    i2 = jnp.min(jnp.where(masked2 == m2, row, N_EXP), axis=0, keepdims=True)
    oh1 = row == i1
    oh2 = row == i2
    s1 = jnp.sum(jnp.where(oh1, scores, 0.0), axis=0, keepdims=True)
    s2 = jnp.sum(jnp.where(oh2, scores, 0.0), axis=0, keepdims=True)
    den = s1 + s2

    comb = jnp.where(oh1, s1 / den, 0.0) + jnp.where(oh2, s2 / den, 0.0)
    comb_t = jnp.concatenate([comb, jnp.zeros((LANES - N_EXP, tm), f32)], axis=0).T
    h2_ref[rows, D:D + LANES] = comb_t

    e_lo = jnp.minimum(i1, i2) - EXP_PER_GROUP * bg
    e_hi = jnp.maximum(i1, i2) - EXP_PER_GROUP * bg
    pair = jnp.right_shift(e_lo * (7 - e_lo), 1) + (e_hi - e_lo - 1)
    bucket = bg * PAIRS + pair
    rowb = lax.broadcasted_iota(jnp.int32, (BUCKET_ROWS, tm), 0)
    ohb = rowb == bucket
    ohb_f = jnp.where(ohb, 1.0, 0.0)
    tot = carry_ref[:, 0:1] + _dot(ohb_f.astype(bf16), tri_ref[...])
    rank = jnp.sum(jnp.where(ohb, tot, 0.0), axis=0, keepdims=True)
    carry_ref[...] = carry_ref[...] + jnp.sum(ohb_f, axis=1, keepdims=True)

    route_ref[0:1, rows] = bucket.astype(f32)
    route_ref[1:2, rows] = rank
    route_ref[2:8, rows] = jnp.zeros((6, tm), f32)


def _post(l, x_ctx, x_lat, oa_ctx, oa_lat, yr_ctx, yr_lat, ga, gb, wpa, wpb, wo, mod, norm2_w, wr_pad,
          br_col, tri):
    tm = TM_POST
    row = lambda w: pl.BlockSpec((tm, w), lambda i: (i, 0))
    const = lambda a: pl.BlockSpec(a.shape, lambda i: (0,) * a.ndim)
    hbm = pl.BlockSpec(memory_space=pl.ANY)
    return pl.pallas_call(
        functools.partial(_post_kernel, l),
        grid=(T // tm,),
        in_specs=(_two_stream_specs(tm, D) + _two_stream_specs(tm, Q_A) + _two_stream_specs(tm, V_R)
                  + [row(D), row(D), hbm, hbm, hbm,
                     pl.BlockSpec((None, None, 6, D), lambda i: (l, _cond_of_tile(i, tm), 0, 0)),
                     _layer_row(l, D), const(wr_pad), const(br_col), const(tri)]),
        out_specs=[row(D), row(MOE_W), pl.BlockSpec((8, tm), lambda i: (0, i)),
                   pl.BlockSpec((BUCKET_ROWS, LANES), lambda i: (0, 0))],
        out_shape=[jax.ShapeDtypeStruct((T, D), f32), jax.ShapeDtypeStruct((T, MOE_W), f32),
                   jax.ShapeDtypeStruct((8, T), f32), jax.ShapeDtypeStruct((BUCKET_ROWS, LANES), f32)],
        scratch_shapes=[pltpu.VMEM((BUCKET_ROWS, LANES), f32), pltpu.VMEM((Q_A, D), bf16),
                        pltpu.VMEM((V_R, D), bf16), pltpu.VMEM((D, D), bf16),
                        pltpu.VMEM((W_STAGES, D, W_CHUNK), f32), pltpu.SemaphoreType.DMA((W_STAGES,))],
        compiler_params=pltpu.CompilerParams(
            dimension_semantics=("arbitrary",), vmem_limit_bytes=V7X_VMEM_LIMIT),
        name="post_router",
    )(x_ctx, x_lat, oa_ctx, oa_lat, yr_ctx, yr_lat, ga, gb, wpa, wpb, wo, mod, norm2_w, wr_pad, br_col, tri)


def _row_copy(src_ref, src_row, dst_ref, dst_row, sem):
    return pltpu.make_async_copy(src_ref.at[pl.ds(src_row, 1)], dst_ref.at[pl.ds(dst_row, 1)], sem)


def _dispatch_kernel(zf_ref, pos_ref, h_ref, xs_ref, zero_ref, sem):
    tm = h_ref.shape[0]

    @pl.when(pl.program_id(0) == 0)
    def _():
        zero_ref[...] = jnp.zeros_like(zero_ref)

        def tile_copy(t):
            return pltpu.make_async_copy(zero_ref, xs_ref.at[pl.ds(t * MOE_TR, MOE_TR)], sem)

        def start(t, carry):
            @pl.when(zf_ref[t] != 0)
            def _():
                tile_copy(t).start()
            return carry

        def wait(t, carry):
            @pl.when(zf_ref[t] != 0)
            def _():
                tile_copy(t).wait()
            return carry

        lax.fori_loop(0, MOE_TILES, start, 0)
        lax.fori_loop(0, MOE_TILES, wait, 0)

    for r in range(tm):
        _row_copy(h_ref, r, xs_ref, pos_ref[0, 0, r], sem).start(priority=r % 2)
    pltpu.make_async_copy(h_ref, xs_ref.at[pl.ds(0, tm)], sem).wait()


def _dispatch(zero_flags, pos_blocks, h2):
    tm = TM_DISP
    return pl.pallas_call(
        _dispatch_kernel,
        grid_spec=pltpu.PrefetchScalarGridSpec(
            num_scalar_prefetch=1,
            grid=(T // tm,),
            in_specs=[pl.BlockSpec((1, 1, tm), lambda i, zf: (i, 0, 0), memory_space=pltpu.SMEM),
                      pl.BlockSpec((tm, MOE_W), lambda i, zf: (i, 0))],
            out_specs=pl.BlockSpec(memory_space=pl.ANY),
            scratch_shapes=[pltpu.VMEM((MOE_TR, MOE_W), f32), pltpu.SemaphoreType.DMA(())],
        ),
        out_shape=jax.ShapeDtypeStruct((MOE_ROWS, MOE_W), f32),
        compiler_params=pltpu.CompilerParams(
            dimension_semantics=("arbitrary",), vmem_limit_bytes=V7X_VMEM_LIMIT),
        name="moe_dispatch",
    )(zero_flags, pos_blocks, h2)


def _experts_kernel(l, tg_ref, ta_ref, tb_ref, first_ref, nxt_ref, nt_ref, xs_ref, wg_hbm, wu_hbm, wd_hbm,
                    y_ref, wg_ref, wu_ref, wd_ref, sg_ref, su_ref, sd_ref, sems):
    i = pl.program_id(0)
    live = i < nt_ref[0]
    g = tg_ref[i]

    def fetch(grp):
        return [pltpu.make_async_copy(w.at[l, pl.ds(grp * EXP_PER_GROUP, EXP_PER_GROUP)], s, sems.at[k])
                for k, (w, s) in enumerate(((wg_hbm, sg_ref), (wu_hbm, su_ref), (wd_hbm, sd_ref)))]

    @pl.when(i == 0)
    def _():
        for c in fetch(tg_ref[0]):
            c.start()

    @pl.when(live & (first_ref[i] != 0))
    def _():
        for c in fetch(g):
            c.wait()
        for j in range(EXP_PER_GROUP):
            wg_ref[j] = sg_ref[j].astype(bf16)
            wu_ref[j] = su_ref[j].astype(bf16)
            wd_ref[j] = sd_ref[j].astype(bf16)

        @pl.when(nxt_ref[i] >= 0)
        def _():
            for c in fetch(nxt_ref[i]):
                c.start(priority=1)

    @pl.when(live)
    def _():
        x = xs_ref[:, 0:D].astype(bf16)
        comb = xs_ref[:, D:D + LANES]
        lane = lax.broadcasted_iota(jnp.int32, comb.shape, 1)
        acc = None
        for j in (ta_ref[i], tb_ref[i]):
            w_j = jnp.sum(jnp.where(lane == g * EXP_PER_GROUP + j, comb, 0.0), axis=1, keepdims=True)
            act = _silu(_dot(x, wg_ref[j])) * _dot(x, wu_ref[j]) * w_j
            part = _dot(act.astype(bf16), wd_ref[j])
            acc = part if acc is None else acc + part
        y_ref[...] = acc

    @pl.when(i >= nt_ref[0])
    def _():
        y_ref[...] = jnp.zeros_like(y_ref)


def _experts(l, plan, xs, wg, wu, wd):
    def tile(i, *prefetch):
        return jnp.minimum(i, prefetch[-1][0] - 1)

    hbm = pl.BlockSpec(memory_space=pl.ANY)
    return pl.pallas_call(
        functools.partial(_experts_kernel, l),
        grid_spec=pltpu.PrefetchScalarGridSpec(
            num_scalar_prefetch=6,
            grid=(MOE_TILES,),
            in_specs=[pl.BlockSpec((MOE_TR, MOE_W), lambda i, *p: (tile(i, *p), 0)), hbm, hbm, hbm],
            out_specs=pl.BlockSpec((MOE_TR, D), lambda i, *p: (i, 0)),
            scratch_shapes=[pltpu.VMEM((EXP_PER_GROUP, D, D_EXP), bf16),
                            pltpu.VMEM((EXP_PER_GROUP, D, D_EXP), bf16),
                            pltpu.VMEM((EXP_PER_GROUP, D_EXP, D), bf16),
                            pltpu.VMEM((EXP_PER_GROUP, D, D_EXP), f32),
                            pltpu.VMEM((EXP_PER_GROUP, D, D_EXP), f32),
                            pltpu.VMEM((EXP_PER_GROUP, D_EXP, D), f32), pltpu.SemaphoreType.DMA((3,))],
        ),
        out_shape=jax.ShapeDtypeStruct((MOE_ROWS, D), f32),
        compiler_params=pltpu.CompilerParams(
            dimension_semantics=("arbitrary",), vmem_limit_bytes=V7X_VMEM_LIMIT),
        name="moe_experts",
    )(*plan, xs, wg, wu, wd)


def _combine_kernel(pos_ref, posn_ref, x1_ref, mod_ref, y_ref, oc_ref, ol_ref, ya_ref, yb_ref, sems):
    tm = x1_ref.shape[0]
    i = pl.program_id(0)
    n_ctx = T_CTX // tm

    def pull(p_ref, buf, sem):
        for r in range(tm):
            _row_copy(y_ref, p_ref[0, 0, r], buf, r, sem).start(priority=r % 2)

    @pl.when(i == 0)
    def _():
        pull(pos_ref, ya_ref, sems.at[0])

    def step(cur, nxt, sem_cur, sem_nxt):
        @pl.when(i + 1 < pl.num_programs(0))
        def _():
            pull(posn_ref, nxt, sem_nxt)

        pltpu.make_async_copy(y_ref.at[pl.ds(0, tm)], cur, sem_cur).wait()
        out = x1_ref[...] + mod_ref[5:6, :] * cur[...]

        @pl.when(i < n_ctx)
        def _():
            oc_ref[...] = out

        @pl.when(i >= n_ctx)
        def _():
            ol_ref[...] = out

    @pl.when(i % 2 == 0)
    def _():
        step(ya_ref, yb_ref, sems.at[0], sems.at[1])

    @pl.when(i % 2 == 1)
    def _():
        step(yb_ref, ya_ref, sems.at[1], sems.at[0])


def _combine(l, pos_blocks, x1, mod, y):
    tm = TM_DISP
    n = T // tm
    return pl.pallas_call(
        _combine_kernel,
        grid=(n,),
        in_specs=[pl.BlockSpec((1, 1, tm), lambda i: (i, 0, 0), memory_space=pltpu.SMEM),
                  pl.BlockSpec((1, 1, tm), lambda i: (jnp.minimum(i + 1, n - 1), 0, 0),
                               memory_space=pltpu.SMEM),
                  pl.BlockSpec((tm, D), lambda i: (i, 0)),
                  pl.BlockSpec((None, None, 6, D), lambda i: (l, _cond_of_tile(i, tm), 0, 0)),
                  pl.BlockSpec(memory_space=pl.ANY)],
        out_specs=_two_stream_specs(tm, D),
        out_shape=[jax.ShapeDtypeStruct((T_CTX, D), f32), jax.ShapeDtypeStruct((T_LAT, D), f32)],
        scratch_shapes=[pltpu.VMEM((tm, D), f32), pltpu.VMEM((tm, D), f32),
                        pltpu.SemaphoreType.DMA((2,))],
        compiler_params=pltpu.CompilerParams(
            dimension_semantics=("arbitrary",), vmem_limit_bytes=V7X_VMEM_LIMIT),
        name="moe_combine",
    )(pos_blocks, pos_blocks, x1, mod, y)


def _plan_kernel(cnt_ref, off_ref, tg_ref, ta_ref, tb_ref, first_ref, nxt_ref, nt_ref, zf_ref, nxtg_ref):
    def bucket(b, start):
        n = lax.shift_right_logical(cnt_ref[b] + (MOE_TR - 1), MOE_TR.bit_length() - 1)
        off_ref[b] = start * MOE_TR
        g = b // PAIRS
        p = b - g * PAIRS
        lo = jnp.where(p >= 3, 1, 0) + jnp.where(p >= 5, 1, 0)
        hi = p + 1 - jnp.where(lo >= 1, 2, 0) - jnp.where(lo >= 2, 1, 0)

        def tile(t, c):
            tg_ref[t] = g
            ta_ref[t] = lo
            tb_ref[t] = hi
            zf_ref[t] = jnp.where(t == start + n - 1, 1, 0)
            return c

        lax.fori_loop(start, start + n, tile, 0)
        return start + n

    nt = lax.fori_loop(0, N_BUCKETS, bucket, jnp.int32(0))
    nt_ref[0] = nt

    def unused(t, c):
        tg_ref[t] = N_GROUPS - 1
        ta_ref[t] = 0
        tb_ref[t] = 1
        zf_ref[t] = 1
        first_ref[t] = 0
        nxt_ref[t] = -1
        return c

    lax.fori_loop(nt, MOE_TILES, unused, 0)

    def back(i, nxt_group):
        t = nt - 1 - i
        g = tg_ref[t]
        later = jnp.where(t + 1 < nt, tg_ref[jnp.minimum(t + 1, MOE_TILES - 1)], -1)
        is_last_of_group = later != g
        nxt_group = jnp.where(is_last_of_group, later, nxt_group)
        nxtg_ref[g] = nxt_group
        return nxt_group

    lax.fori_loop(0, nt, back, jnp.int32(-1))

    def fwd(t, c):
        g = tg_ref[t]
        first_ref[t] = jnp.where((t == 0) | (tg_ref[jnp.maximum(t - 1, 0)] != g), 1, 0)
        nxt_ref[t] = nxtg_ref[g]
        return c

    lax.fori_loop(0, nt, fwd, 0)


def _moe_plan(route, counts):
    smem = pl.BlockSpec(memory_space=pltpu.SMEM)
    tile_i32 = jax.ShapeDtypeStruct((MOE_TILES,), jnp.int32)
    off, tg, ta, tb, first, nxt, n_tiles, zero_flags = pl.pallas_call(
        _plan_kernel,
        in_specs=[smem],
        out_specs=[smem] * 8,
        out_shape=[jax.ShapeDtypeStruct((N_BUCKETS,), jnp.int32), tile_i32, tile_i32, tile_i32, tile_i32,
                   tile_i32, jax.ShapeDtypeStruct((1,), jnp.int32), tile_i32],
        scratch_shapes=[pltpu.SMEM((N_GROUPS,), jnp.int32)],
        name="moe_plan",
    )(counts[0:N_BUCKETS, 0].astype(jnp.int32))
    buckets = jnp.arange(N_BUCKETS, dtype=f32)[:, None]
    pos = jnp.sum(jnp.where(route[0][None, :] == buckets, off.astype(f32)[:, None], 0.0), axis=0) + route[1]
    pos_blocks = pos.astype(jnp.int32).reshape(T // TM_DISP, 1, TM_DISP)
    return pos_blocks, zero_flags, (tg, ta, tb, first, nxt, n_tiles)


def _rope_tables():
    pos = np.arange(DEC_SEQ)
    half = HD_A // 4
    freqs = ROPE_BASE ** (-np.arange(half, dtype=np.float64) / half)
    ang_r = (pos // GRID_W)[:, None] * freqs[None, :]
    ang_c = (pos % GRID_W)[:, None] * freqs[None, :]
    ang = np.concatenate([ang_r, ang_r, ang_c, ang_c], axis=1)
    sign = np.concatenate([-np.ones(half), np.ones(half)] * 2)[None, :]
    cos = np.tile(np.cos(ang), (1, 2))
    sin = np.tile(np.sin(ang) * sign, (1, 2))
    ident_c = np.ones((TM_PROJ, LANES))
    ident_s = np.zeros((TM_PROJ, LANES))
    return (jnp.asarray(np.concatenate([cos, ident_c]), f32),
            jnp.asarray(np.concatenate([sin, ident_s]), f32))


def kernel(x_prompt, x_sample, c, cache_k, cache_v, state_ret_fwd, state_ret_bwd, c_ctx,
           norm1_w, norm2_w, w_ada, b_ada, w_in, q_norm_w, k_norm_w, attn_sink,
           ret_decay_fwd, ret_decay_bwd, ret_gn_w, w_pa, w_pb, w_o, w_router, b_router,
           w_exp_gate, w_exp_up, w_exp_down):
    x_ctx, x_lat = x_prompt.reshape(T_CTX, D), x_sample.reshape(T_LAT, D)
    cond8 = jnp.zeros((N_COND, D), f32).at[0].set(c_ctx).at[1:1 + DEC_BATCH].set(c)
    mod = _adaln(cond8, w_ada, b_ada).reshape(DEPTH, N_COND, 6, D)

    cos_t, sin_t = _rope_tables()
    blk = np.arange(Q_A) // HD_A
    ones_blk = jnp.asarray((blk[:, None] == blk[None, :]) / HD_A, bf16)
    tri = jnp.asarray(np.triu(np.ones((POST_SUB, POST_SUB)), 1), bf16)
    per_layer = lambda w: w.reshape(DEPTH, 1, -1)
    norm1_w, norm2_w, gnw = per_layer(norm1_w), per_layer(norm2_w), per_layer(ret_gn_w)
    qnw = per_layer(jnp.tile(q_norm_w, (1, NH_A)) * (HD_A ** -0.5))
    knw = per_layer(jnp.tile(k_norm_w, (1, NKV_A)))
    wr_pad = jnp.concatenate(_split(jnp.pad(w_router, ((0, 0), (0, LANES - N_EXP)))), axis=1)
    br_col = jnp.broadcast_to(b_router[:, None], (N_EXP, LANES))
    ck = cache_k.reshape(DEC_BATCH, DEPTH, PAST, KV_A)
    cv = cache_v.reshape(DEC_BATCH, DEPTH, PAST, KV_A)
    s0 = jnp.concatenate([state_ret_fwd, state_ret_bwd], axis=3)
    lg = jnp.log1p(-jnp.exp(jnp.concatenate([ret_decay_fwd, ret_decay_bwd], axis=1).astype(f32)))

    new_k, new_v, states = [], [], ()
    for l in range(DEPTH):
        qa, ka, va, qr, kr, vr, gr, ga, gb = _inproj(l, x_ctx, x_lat, norm1_w, mod, w_in, cos_t, sin_t,
                                                     ones_blk, qnw, knw)
        oa_ctx = _ctx_attn(l, qa, ka, va, attn_sink)
        oa_lat = _lat_attn(l, qa, ka, va, ck, cv, cos_t, sin_t, attn_sink)
        yr_ctx, *states = _retention(l, qr, kr, vr, gr, None, lg, gnw, BATCH, SEQ, 0, CTX_SEQS_PER_STEP,
                                     stack_into=states)
        yr_lat, _, _ = _retention(l, qr, kr, vr, gr, s0, lg, gnw, DEC_BATCH, DEC_SEQ,
                                  T_CTX // DEC_SEQ, 1)
        x1, h2, route, counts = _post(l, x_ctx, x_lat, oa_ctx, oa_lat, yr_ctx, yr_lat, ga, gb,
                                      w_pa, w_pb, w_o, mod, norm2_w, wr_pad, br_col, tri)
        pos_blocks, zero_flags, plan = _moe_plan(route, counts)
        xs = _dispatch(zero_flags, pos_blocks, h2)
        y = _experts(l, plan, xs, w_exp_gate, w_exp_up, w_exp_down)
        x_ctx, x_lat = _combine(l, pos_blocks, x1, mod, y)

        new_k.append(ka[:T_CTX].reshape(BATCH, SEQ, NKV_A, HD_A))
        new_v.append(va[:T_CTX].reshape(BATCH, SEQ, NKV_A, HD_A))

    return (x_ctx.reshape(BATCH, SEQ, D), x_lat.reshape(DEC_BATCH, DEC_SEQ, D),
            jnp.stack(new_k, axis=1), jnp.stack(new_v, axis=1),
            states[0], states[1])
```
